```python
import math
import jax
import jax.numpy as jnp
from jax import lax
import numpy as np

D_MODEL = 1024
BATCH = 8
SEQ = 4096
DEPTH = 2

MEM_LEN = 256
N_HEADS_GROUP = 4
HEAD_DIM = 64
GROUP_WIDTH = N_HEADS_GROUP * HEAD_DIM
N_GROUPS = 5
MIX_WIDTH = N_GROUPS * GROUP_WIDTH
MLA_Q_RANK = 192
MLA_KV_RANK = 128
MLA_NOPE_DIM = 64
MLA_ROPE_DIM = 32
MLA_QK_DIM = MLA_NOPE_DIM + MLA_ROPE_DIM
DIFF_QK_DIM = HEAD_DIM // 2
ROPE_THETA = 500000.0
ROT_MOBA = HEAD_DIM // 4
ROT_DIFF = DIFF_QK_DIM // 4
POS_OFFSET_MAX = 8192
MOBA_BLOCK = 256
MOBA_TOPK = 3
MOBA_Q_CHUNK = 64
Q_BLOCK = 128
D_FF = ((8 * D_MODEL // 3 + 127) // 128) * 128
CONV_WIDTH = 3
EPS = 1e-6
NEG_INF = -1e30
IN_SIZES = (MLA_Q_RANK, MLA_KV_RANK, MLA_ROPE_DIM,
            3 * GROUP_WIDTH, N_HEADS_GROUP,
            3 * GROUP_WIDTH,
            3 * GROUP_WIDTH,
            GROUP_WIDTH)
D_IN = sum(IN_SIZES)

kernel_name = "hybrid_parallel_heads_mla_fox_moba_diff"


def _rms_norm(x, g):
    xf = x.astype(jnp.float32)
    y = xf * lax.rsqrt(jnp.mean(xf * xf, axis=-1, keepdims=True) + EPS)
    return (y * g.astype(jnp.float32)).astype(x.dtype)


def _heads(t, n):
    b, s, w = t.shape
    return t.reshape(b, s, n, w // n).transpose(0, 2, 1, 3)


def _merge(t):
    b, h, s, d = t.shape
    return t.transpose(0, 2, 1, 3).reshape(b, s, h * d)


def _split_pairs(t):
    b, s, _ = t.shape
    return t.reshape(b, s, N_HEADS_GROUP, 2, DIFF_QK_DIM).transpose(3, 0, 2, 1, 4)


def _rope_tables(positions, rot):
    inv = ROPE_THETA ** (-jnp.arange(0, rot, 2, dtype=jnp.float32) / rot)
    ang = positions.astype(jnp.float32)[:, None, :, None] * inv
    return jnp.cos(ang), jnp.sin(ang)


def _rope(x, cos, sin):
    rot = 2 * cos.shape[-1]
    xr = x[..., :rot].astype(jnp.float32)
    x1, x2 = xr[..., :rot // 2], xr[..., rot // 2:]
    r = jnp.concatenate([x1 * cos - x2 * sin, x1 * sin + x2 * cos], axis=-1).astype(x.dtype)
    return jnp.concatenate([r, x[..., rot:]], axis=-1)


def _causal_mask(start, n_q, n_k):
    return jnp.arange(n_k)[None, :] <= (start + jnp.arange(n_q))[:, None]


def _sweep(block_fn, seq, block):
    out = lax.map(block_fn, jnp.arange(seq // block))
    nb, b, h, q, d = out.shape
    return out.transpose(1, 2, 0, 3, 4).reshape(b, h, nb * q, d)


def _dense_causal_attention(q, k, v, scale, log_decay=None):
    seq = q.shape[2]

    def block(i):
        start = i * Q_BLOCK
        qb = lax.dynamic_slice_in_dim(q, start, Q_BLOCK, axis=2)
        s = jnp.einsum('bhqd,bhkd->bhqk', qb, k).astype(jnp.float32) * scale
        if log_decay is not None:
            cq = lax.dynamic_slice_in_dim(log_decay, start, Q_BLOCK, axis=2)
            s = s + (cq[..., :, None] - log_decay[..., None, :])
        s = jnp.where(_causal_mask(start, Q_BLOCK, seq), s, NEG_INF)
        p = jax.nn.softmax(s, axis=-1).astype(v.dtype)
        return jnp.einsum('bhqk,bhkd->bhqd', p, v)

    return _sweep(block, seq, Q_BLOCK)


def _diff_causal_attention(q1, q2, k1, k2, v, lam, scale):
    seq = q1.shape[2]

    def block(i):
        start = i * Q_BLOCK
        mask = _causal_mask(start, Q_BLOCK, seq)

        def probs(q, k):
            qb = lax.dynamic_slice_in_dim(q, start, Q_BLOCK, axis=2)
            s = jnp.einsum('bhqd,bhkd->bhqk', qb, k).astype(jnp.float32) * scale
            return jax.nn.softmax(jnp.where(mask, s, NEG_INF), axis=-1)

        p = (probs(q1, k1) - lam * probs(q2, k2)).astype(v.dtype)
        return jnp.einsum('bhqk,bhkd->bhqd', p, v)

    return _sweep(block, seq, Q_BLOCK)


def _moba_attention(q, k, v, scale):
    b, h, seq, d = q.shape
    nb = -(-seq // MOBA_BLOCK)
    pad = nb * MOBA_BLOCK - seq
    kp = jnp.pad(k, ((0, 0), (0, 0), (0, pad), (0, 0)))
    vp = jnp.pad(v, ((0, 0), (0, 0), (0, pad), (0, 0)))
    kb = kp.reshape(b, h, nb, MOBA_BLOCK, d)
    vb = vp.reshape(b, h, nb, MOBA_BLOCK, d)
    k_mean = jnp.mean(kb, axis=3)
    gate = jnp.einsum('bhsd,bhnd->bhsn', q, k_mean).astype(jnp.float32)
    q_blk = jnp.arange(seq) // MOBA_BLOCK
    past = jnp.arange(nb)[None, :] < q_blk[:, None]
    gate = jnp.where(past, gate, NEG_INF)
    topk = min(MOBA_TOPK, nb)
    _, sel = lax.top_k(gate, topk)
    sel_ok = sel < q_blk[:, None]
    bi = jnp.arange(b)[:, None, None, None]
    hi = jnp.arange(h)[None, :, None, None]
    n_sel = topk * MOBA_BLOCK

    def chunk(i):
        start = i * MOBA_Q_CHUNK
        qc = lax.dynamic_slice_in_dim(q, start, MOBA_Q_CHUNK, axis=2)
        idx = lax.dynamic_slice_in_dim(sel, start, MOBA_Q_CHUNK, axis=2)
        ok = lax.dynamic_slice_in_dim(sel_ok, start, MOBA_Q_CHUNK, axis=2)
        kg = kb[bi, hi, idx]
        vg = vb[bi, hi, idx]
        s_past = jnp.einsum('bhqd,bhqnkd->bhqnk', qc, kg).astype(jnp.float32) * scale
        s_past = jnp.where(ok[..., None], s_past, NEG_INF).reshape(b, h, MOBA_Q_CHUNK, n_sel)
        own = (start // MOBA_BLOCK) * MOBA_BLOCK
        k_own = lax.dynamic_slice_in_dim(kp, own, MOBA_BLOCK, axis=2)
        v_own = lax.dynamic_slice_in_dim(vp, own, MOBA_BLOCK, axis=2)
        s_own = jnp.einsum('bhqd,bhkd->bhqk', qc, k_own).astype(jnp.float32) * scale
        causal = (own + jnp.arange(MOBA_BLOCK))[None, :] <= (start + jnp.arange(MOBA_Q_CHUNK))[:, None]
        s_own = jnp.where(causal, s_own, NEG_INF)
        p = jax.nn.softmax(jnp.concatenate([s_past, s_own], axis=-1), axis=-1).astype(v.dtype)
        p_past = p[..., :n_sel].reshape(b, h, MOBA_Q_CHUNK, topk, MOBA_BLOCK)
        return (jnp.einsum('bhqnk,bhqnkd->bhqd', p_past, vg)
                + jnp.einsum('bhqk,bhkd->bhqd', p[..., n_sel:], v_own))

    return _sweep(chunk, seq, MOBA_Q_CHUNK)


def _memory_attention(q, k, v, scale):
    s = jnp.einsum('bhqd,bhmd->bhqm', q, k).astype(jnp.float32) * scale
    p = jax.nn.softmax(s, axis=-1).astype(v.dtype)
    return jnp.einsum('bhqm,bhmd->bhqd', p, v)


def _causal_dwconv(h, w, bias):
    seq = h.shape[1]
    hp = jnp.pad(h, ((0, 0), (CONV_WIDTH - 1, 0), (0, 0)))
    y = bias
    for j in range(CONV_WIDTH):
        y = y + w[j] * hp[:, j:j + seq]
    return y


def setup_inputs(seed: int = 0) -> dict:
    key = jax.random.key(seed)
    ks = iter(jax.random.split(key, 40))
    L = DEPTH

    def nrm(shape, scale):
        return scale * jax.random.normal(next(ks), shape, jnp.float32)

    def gain(shape):
        return 1.0 + 0.05 * jax.random.normal(next(ks), shape, jnp.float32)

    x = nrm((BATCH, SEQ, D_MODEL), 1.0)
    mem = nrm((BATCH, MEM_LEN, D_MODEL), 1.0)
    offset = jax.random.randint(next(ks), (BATCH, 1), 0, POS_OFFSET_MAX, dtype=jnp.int32)
    positions = offset + jnp.arange(SEQ, dtype=jnp.int32)[None, :]
    return {
        "x": x,
        "mem": mem,
        "positions": positions,
        "attn_norm": gain((L, D_MODEL)),
        "ffn_norm": gain((L, D_MODEL)),
        "mem_norm": gain((L, D_MODEL)),
        "w_in": nrm((L, D_MODEL, D_IN), D_MODEL ** -0.5),
        "mla_cq_norm": gain((L, MLA_Q_RANK)),
        "mla_ckv_norm": gain((L, MLA_KV_RANK)),
        "mla_w_uq": nrm((L, MLA_Q_RANK, N_HEADS_GROUP * MLA_QK_DIM), MLA_Q_RANK ** -0.5),
        "mla_w_ukv": nrm((L, MLA_KV_RANK, N_HEADS_GROUP * (MLA_NOPE_DIM + HEAD_DIM)), MLA_KV_RANK ** -0.5),
        "mla_q_norm": gain((L, MLA_QK_DIM)),
        "mla_k_norm": gain((L, MLA_QK_DIM)),
        "fox_b_f": 3.0 + 0.5 * jax.random.normal(next(ks), (L, N_HEADS_GROUP), jnp.float32),
        "fox_q_norm": gain((L, HEAD_DIM)),
        "fox_k_norm": gain((L, HEAD_DIM)),
        "moba_q_norm": gain((L, HEAD_DIM)),
        "moba_k_norm": gain((L, HEAD_DIM)),
        "diff_lambda": nrm((L, 4, DIFF_QK_DIM), 0.1),
        "diff_q_norm": gain((L, DIFF_QK_DIM)),
        "diff_k_norm": gain((L, DIFF_QK_DIM)),
        "diff_sub_norm": gain((L, HEAD_DIM)),
        "mem_w_kv": nrm((L, D_MODEL, 2 * GROUP_WIDTH), D_MODEL ** -0.5),
        "mem_q_norm": gain((L, HEAD_DIM)),
        "mem_k_norm": gain((L, HEAD_DIM)),
        "w_o": nrm((L, MIX_WIDTH, D_MODEL), MIX_WIDTH ** -0.5),
        "ffn_w_gate": nrm((L, D_MODEL, D_FF), D_MODEL ** -0.5),
        "ffn_w_up": nrm((L, D_MODEL, D_FF), D_MODEL ** -0.5),
        "ffn_conv_w": nrm((L, CONV_WIDTH, D_FF), CONV_WIDTH ** -0.5),
        "ffn_conv_b": nrm((L, D_FF), 0.02),
        "ffn_w_down": nrm((L, D_FF, D_MODEL), D_FF ** -0.5),
    }


def reference(x, mem, positions, attn_norm, ffn_norm, mem_norm, w_in,
              mla_cq_norm, mla_ckv_norm, mla_w_uq, mla_w_ukv, mla_q_norm, mla_k_norm,
              fox_b_f, fox_q_norm, fox_k_norm,
              moba_q_norm, moba_k_norm,
              diff_lambda, diff_q_norm, diff_k_norm, diff_sub_norm,
              mem_w_kv, mem_q_norm, mem_k_norm,
              w_o, ffn_w_gate, ffn_w_up, ffn_conv_w, ffn_conv_b, ffn_w_down):
    bsz, seq, _ = x.shape
    R = MLA_ROPE_DIM
    cos_a, sin_a = _rope_tables(positions, MLA_ROPE_DIM)
    cos_c, sin_c = _rope_tables(positions, ROT_MOBA)
    cos_d, sin_d = _rope_tables(positions, ROT_DIFF)
    split_at = np.cumsum(IN_SIZES)[:-1].tolist()

    for l in range(DEPTH):
        xn = _rms_norm(x, attn_norm[l])
        h = xn @ w_in[l]
        cq, ckv, kr, fox_qkv, fox_f, moba_qkv, diff_qkv, mem_q = jnp.split(h, split_at, axis=-1)

        q_a = _heads(_rms_norm(cq, mla_cq_norm[l]) @ mla_w_uq[l], N_HEADS_GROUP)
        kv_a = _heads(_rms_norm(ckv, mla_ckv_norm[l]) @ mla_w_ukv[l], N_HEADS_GROUP)
        gq, gk = mla_q_norm[l], mla_k_norm[l]
        q_a = jnp.concatenate([_rope(_rms_norm(q_a[..., :R], gq[:R]), cos_a, sin_a),
                               _rms_norm(q_a[..., R:], gq[R:])], axis=-1)
        k_nope = _rms_norm(kv_a[..., :MLA_NOPE_DIM], gk[R:])
        k_rope = _rope(_rms_norm(kr[:, None], gk[:R]), cos_a, sin_a)
        k_a = jnp.concatenate([jnp.broadcast_to(k_rope, k_nope.shape[:-1] + (R,)), k_nope], axis=-1)
        o_a = _dense_causal_attention(q_a, k_a, kv_a[..., MLA_NOPE_DIM:], MLA_QK_DIM ** -0.5)

        q_b, k_b, v_b = [_heads(t, N_HEADS_GROUP) for t in jnp.split(fox_qkv, 3, axis=-1)]
        log_f = jax.nn.log_sigmoid(fox_f.astype(jnp.float32) + fox_b_f[l].astype(jnp.float32))
        decay = jnp.cumsum(log_f, axis=1).transpose(0, 2, 1)
        o_b = _dense_causal_attention(_rms_norm(q_b, fox_q_norm[l]), _rms_norm(k_b, fox_k_norm[l]),
                                      v_b, HEAD_DIM ** -0.5, decay)

        q_c, k_c, v_c = [_heads(t, N_HEADS_GROUP) for t in jnp.split(moba_qkv, 3, axis=-1)]
        q_c = _rope(_rms_norm(q_c, moba_q_norm[l]), cos_c, sin_c)
        k_c = _rope(_rms_norm(k_c, moba_k_norm[l]), cos_c, sin_c)
        o_c = _moba_attention(q_c, k_c, v_c, HEAD_DIM ** -0.5)

        dq, dk, dv = jnp.split(diff_qkv, 3, axis=-1)
        q_d = _rope(_rms_norm(_split_pairs(dq), diff_q_norm[l]), cos_d, sin_d)
        k_d = _rope(_rms_norm(_split_pairs(dk), diff_k_norm[l]), cos_d, sin_d)
        lam_vec = diff_lambda[l].astype(jnp.float32)
        lam_init = 0.8 - 0.6 * math.exp(-0.3 * l)
        lam = (jnp.exp(jnp.sum(lam_vec[0] * lam_vec[1]))
               - jnp.exp(jnp.sum(lam_vec[2] * lam_vec[3])) + lam_init)
        o_d = _diff_causal_attention(q_d[0], q_d[1], k_d[0], k_d[1], _heads(dv, N_HEADS_GROUP),
                                     lam, DIFF_QK_DIM ** -0.5)
        o_d = _rms_norm(o_d, diff_sub_norm[l]) * (1.0 - lam_init)

        q_e = _rms_norm(_heads(mem_q, N_HEADS_GROUP), mem_q_norm[l])
        mem_kv = _rms_norm(mem, mem_norm[l]) @ mem_w_kv[l]
        k_e, v_e = [_heads(t, N_HEADS_GROUP) for t in jnp.split(mem_kv, 2, axis=-1)]
        o_e = _memory_attention(q_e, _rms_norm(k_e, mem_k_norm[l]), v_e, HEAD_DIM ** -0.5)

        mixed = jnp.concatenate([_merge(o_a), _merge(o_b), _merge(o_c), _merge(o_d), _merge(o_e)], axis=-1)
        x = x + mixed @ w_o[l]

        xn = _rms_norm(x, ffn_norm[l])
        gate = _causal_dwconv(xn @ ffn_w_gate[l], ffn_conv_w[l], ffn_conv_b[l])
        x = x + (jax.nn.silu(gate) * (xn @ ffn_w_up[l])) @ ffn_w_down[l]
    return x
```

```python
import functools
import math

import numpy as np
import jax
import jax.numpy as jnp
from jax import lax
from jax.experimental import pallas as pl
from jax.experimental.pallas import tpu as pltpu

N_HEADS = 4
HEAD_DIM = 64
GROUP_WIDTH = N_HEADS * HEAD_DIM
MLA_Q_RANK = 192
MLA_KV_RANK = 128
MLA_NOPE = 64
MLA_ROPE = 32
MLA_QK = MLA_NOPE + MLA_ROPE
DIFF_QK = HEAD_DIM // 2
ROPE_THETA = 500000.0
ROT_MOBA = HEAD_DIM // 4
ROT_DIFF = DIFF_QK // 4
MOBA_BLOCK = 256
MOBA_TOPK = 3
CONV_WIDTH = 3
EPS = 1e-6
NEG_INF = -1e30
REMOVED = -3e38

LANES = 128
SUBLANES = 8
MXU_WIDTH = 256
TAIL_ROWS = 16
VMEM_LIMIT_BYTES = 56 * 1024 * 1024

_SRC_CQ = 0
_SRC_CKV = _SRC_CQ + MLA_Q_RANK
_SRC_KR = _SRC_CKV + MLA_KV_RANK
_SRC_FOX = _SRC_KR + MLA_ROPE
_SRC_FOXF = _SRC_FOX + 3 * GROUP_WIDTH
_SRC_MOBA = _SRC_FOXF + N_HEADS
_SRC_DIFF = _SRC_MOBA + 3 * GROUP_WIDTH
_SRC_MEMQ = _SRC_DIFF + 3 * GROUP_WIDTH
_SRC_END = _SRC_MEMQ + GROUP_WIDTH

PK_CQ = 0
PK_CKV = 256
PK_KR = 384
PK_FQ, PK_FK, PK_FV = 896, 1152, 1408
PK_FF = 1664
PK_MQ, PK_MK, PK_MV = 1792, 2048, 2304
PK_DQ, PK_DK, PK_DV = 2560, 2816, 3072
PK_EQ = 3328
PK_END = 3584

PAIR_STRIDE = MLA_QK


def _pair_lane(h, d):
    return (h // 2) * MXU_WIDTH + (h % 2) * PAIR_STRIDE + d


N_FREQ = MLA_ROPE // 2 + ROT_MOBA // 2 + ROT_DIFF // 2
FREQ_BASE_MLA = 0
FREQ_BASE_MOBA = MLA_ROPE // 2
FREQ_BASE_DIFF = FREQ_BASE_MOBA + ROT_MOBA // 2
TR_ONE = 4 * N_FREQ
TR_WIDTH = LANES

(P_CQ, P_CKV, P_GQ, P_GK, P_FQ, P_FK, P_FB, P_MQ, P_MK, P_DQ, P_DK, P_EQ) = range(12)
P_ROWS = 16


def _np_in_index():
    idx = np.full((PK_END,), _SRC_END, np.int32)
    idx[PK_CQ:PK_CQ + MLA_Q_RANK] = np.arange(_SRC_CQ, _SRC_CQ + MLA_Q_RANK)
    idx[PK_CKV:PK_CKV + MLA_KV_RANK] = np.arange(_SRC_CKV, _SRC_CKV + MLA_KV_RANK)
    for h in range(N_HEADS):
        for d in range(MLA_ROPE):
            idx[PK_KR + _pair_lane(h, d)] = _SRC_KR + d
    idx[PK_FQ:PK_FQ + 3 * GROUP_WIDTH] = np.arange(_SRC_FOX, _SRC_FOX + 3 * GROUP_WIDTH)
    idx[PK_FF:PK_FF + N_HEADS] = np.arange(_SRC_FOXF, _SRC_FOXF + N_HEADS)
    idx[PK_MQ:PK_MQ + 3 * GROUP_WIDTH] = np.arange(_SRC_MOBA, _SRC_MOBA + 3 * GROUP_WIDTH)
    idx[PK_DQ:PK_DQ + 3 * GROUP_WIDTH] = np.arange(_SRC_DIFF, _SRC_DIFF + 3 * GROUP_WIDTH)
    idx[PK_EQ:PK_EQ + GROUP_WIDTH] = np.arange(_SRC_MEMQ, _SRC_MEMQ + GROUP_WIDTH)
    return idx


def _np_uq_index():
    idx = np.full((2 * MXU_WIDTH,), N_HEADS * MLA_QK, np.int32)
    for h in range(N_HEADS):
        for d in range(MLA_QK):
            idx[_pair_lane(h, d)] = h * MLA_QK + d
    return idx


def _np_ukv_index():
    zero = N_HEADS * (MLA_NOPE + HEAD_DIM)
    idx_k = np.full((2 * MXU_WIDTH,), zero, np.int32)
    idx_v = np.zeros((GROUP_WIDTH,), np.int32)
    for h in range(N_HEADS):
        for d in range(MLA_NOPE):
            idx_k[_pair_lane(h, MLA_ROPE + d)] = h * (MLA_NOPE + HEAD_DIM) + d
        for d in range(HEAD_DIM):
            idx_v[h * HEAD_DIM + d] = h * (MLA_NOPE + HEAD_DIM) + MLA_NOPE + d
    return idx_k, idx_v


def _np_group_matrix(groups):
    g = np.zeros((MXU_WIDTH, MXU_WIDTH), np.float32)
    for lo, size in groups:
        g[lo:lo + size, lo:lo + size] = 1.0 / size
    return g


_PAIR_GROUPS = [(0, MLA_ROPE), (MLA_ROPE, MLA_NOPE), (PAIR_STRIDE, MLA_ROPE), (PAIR_STRIDE + MLA_ROPE, MLA_NOPE)]
_G64_GROUPS = [(h * HEAD_DIM, HEAD_DIM) for h in range(N_HEADS)]
_G32_GROUPS = [(g * DIFF_QK, DIFF_QK) for g in range(2 * N_HEADS)]


def _np_rope_expand(regions, rot, base):
    half = rot // 2
    e = np.zeros((TR_WIDTH, 3 * MXU_WIDTH), np.float32)
    e[TR_ONE, 0:MXU_WIDTH] = 1.0
    for lo in regions:
        for r in range(half):
            f = base + r
            for lane, tab, sign in ((lo + r, 2, -1.0), (lo + half + r, 1, 1.0)):
                e[TR_ONE, lane] = 0.0
                e[f, lane] = 1.0
                e[N_FREQ + f, lane] = 1.0
                e[2 * N_FREQ + f, tab * MXU_WIDTH + lane] = sign
                e[3 * N_FREQ + f, tab * MXU_WIDTH + lane] = sign
    return e


def _np_rope_expand_all():
    return np.concatenate([
        _np_rope_expand([0, PAIR_STRIDE], MLA_ROPE, FREQ_BASE_MLA),
        _np_rope_expand([h * HEAD_DIM for h in range(N_HEADS)], ROT_MOBA, FREQ_BASE_MOBA),
        _np_rope_expand([g * DIFF_QK for g in range(2 * N_HEADS)], ROT_DIFF, FREQ_BASE_DIFF),
    ], axis=1)


def _dot(a, b):
    return jnp.dot(a, b, preferred_element_type=jnp.float32)


def _dot_nt(a, b):
    return lax.dot_general(a, b, (((1,), (1,)), ((), ())), preferred_element_type=jnp.float32)


def _split2(a):
    hi = a.astype(jnp.bfloat16)
    lo = (a - hi.astype(jnp.float32)).astype(jnp.bfloat16)
    return hi, lo


def _split3(a):
    hi = a.astype(jnp.bfloat16)
    r = a - hi.astype(jnp.float32)
    mid = r.astype(jnp.bfloat16)
    lo = (r - mid.astype(jnp.float32)).astype(jnp.bfloat16)
    return hi, mid, lo


def _group_mean_sq(a, g_bf16):
    hi, lo = _split2(a * a)
    return _dot(hi, g_bf16) + _dot(lo, g_bf16)


def _rope(x, tabs, half):
    w = x.shape[-1]
    c, s1, s2 = tabs[:, 0:w], tabs[:, w:2 * w], tabs[:, 2 * w:3 * w]
    return x * c + pltpu.roll(x, half, 1) * s1 + pltpu.roll(x, w - half, 1) * s2


def _lane_mask(shape, lo, hi):
    lane = lax.broadcasted_iota(jnp.int32, shape, len(shape) - 1)
    return (lane >= lo) & (lane < hi)


def _prep_body(x_ref, tr_ref, anorm_ref, win_ref, wuq_ref, wukvk_ref, wukvv_ref, gpair_ref, g64_ref, g32_ref,
               exp_ref, tril_ref, par_ref,
               aq_ref, ak_ref, av_ref, fq_ref, fk_ref, fv_ref, fdec_ref, mq_ref, mk_ref, mv_ref, msel_ref,
               dq_ref, dk_ref, dv_ref, eq_ref,
               kmean_s, carry_s):
    j = pl.program_id(1)
    tm = x_ref.shape[1]
    bf = jnp.bfloat16

    @pl.when(j == 0)
    def _():
        kmean_s[...] = jnp.zeros_like(kmean_s)
        carry_s[...] = jnp.zeros_like(carry_s)

    x = x_ref[0]
    xn = x * lax.rsqrt(jnp.mean(x * x, axis=-1, keepdims=True) + EPS) * anorm_ref[...]
    xb = xn.astype(bf)

    def proj(off, width):
        return _dot(xb, win_ref[:, off:off + width])

    def prow(r, width=MXU_WIDTH):
        return par_ref[r:r + 1, 0:width]

    tabs = _dot(tr_ref[0], exp_ref[...])
    tab_mla = tabs[:, 0:3 * MXU_WIDTH]
    tab_moba = tabs[:, 3 * MXU_WIDTH:6 * MXU_WIDTH]
    tab_diff = tabs[:, 6 * MXU_WIDTH:9 * MXU_WIDTH]
    gpair, g64, g32 = gpair_ref[...], g64_ref[...], g32_ref[...]

    def group_norm(a, g, gain_row):
        return a * lax.rsqrt(_group_mean_sq(a, g) + EPS) * prow(gain_row)

    cq = proj(PK_CQ, MXU_WIDTH)
    cqn = cq * lax.rsqrt(jnp.sum(cq * cq, axis=-1, keepdims=True) * (1.0 / MLA_Q_RANK) + EPS) * prow(P_CQ)
    qa = _dot(cqn.astype(bf), wuq_ref[...])
    ckv = proj(PK_CKV, MLA_KV_RANK)
    ckvn = ckv * lax.rsqrt(jnp.mean(ckv * ckv, axis=-1, keepdims=True) + EPS) * prow(P_CKV, MLA_KV_RANK)
    ckvb = ckvn.astype(bf)
    ka = proj(PK_KR, 2 * MXU_WIDTH) + _dot(ckvb, wukvk_ref[...])
    for p in range(2):
        sl = slice(p * MXU_WIDTH, (p + 1) * MXU_WIDTH)
        aq_ref[0, :, sl] = _rope(group_norm(qa[:, sl], gpair, P_GQ), tab_mla, MLA_ROPE // 2).astype(bf)
        ak_ref[0, :, sl] = _rope(group_norm(ka[:, sl], gpair, P_GK), tab_mla, MLA_ROPE // 2).astype(bf)
    av_ref[0] = _dot(ckvb, wukvv_ref[...]).astype(bf)

    fq_ref[0] = group_norm(proj(PK_FQ, MXU_WIDTH), g64, P_FQ).astype(bf)
    fk_ref[0] = group_norm(proj(PK_FK, MXU_WIDTH), g64, P_FK).astype(bf)
    fv_ref[0] = proj(PK_FV, MXU_WIDTH).astype(bf)
    z = proj(PK_FF, LANES) + prow(P_FB, LANES)
    log_f = jnp.minimum(z, 0.0) - jnp.log1p(jnp.exp(-jnp.abs(z)))
    log_f = jnp.where(_lane_mask(log_f.shape, 0, N_HEADS), log_f, 0.0)
    tril = tril_ref[...]
    l1, l2, l3 = _split3(log_f)
    dec = carry_s[...] + ((_dot(tril, l1) + _dot(tril, l2)) + _dot(tril, l3))
    fdec_ref[0] = dec
    carry_s[...] = dec[tm - 1:tm, :]

    mq = _rope(group_norm(proj(PK_MQ, MXU_WIDTH), g64, P_MQ), tab_moba, ROT_MOBA // 2)
    mk = _rope(group_norm(proj(PK_MK, MXU_WIDTH), g64, P_MK), tab_moba, ROT_MOBA // 2)
    mq_ref[0] = (mq * (HEAD_DIM ** -0.5)).astype(bf)
    mk_ref[0] = mk.astype(bf)
    mv_ref[0] = proj(PK_MV, MXU_WIDTH).astype(bf)
    kmean_s[pl.ds(j, 1), :] = jnp.mean(mk, axis=0, keepdims=True)
    km_hi, km_lo = _split2(kmean_s[...])
    lane = lax.broadcasted_iota(jnp.int32, (tm, LANES), 1)
    past = lane < j
    for h in range(N_HEADS):
        q_hi, q_lo = _split2(jnp.where(_lane_mask(mq.shape, h * HEAD_DIM, (h + 1) * HEAD_DIM), mq, 0.0))
        gate = (_dot_nt(q_hi, km_hi) + _dot_nt(q_hi, km_lo)) + _dot_nt(q_lo, km_hi)
        work = jnp.where(past, gate, NEG_INF)
        sel = jnp.zeros((tm, LANES), jnp.bool_)
        for _ in range(MOBA_TOPK):
            mx = jnp.max(work, axis=-1, keepdims=True)
            first = jnp.min(jnp.where(work == mx, lane, LANES), axis=-1, keepdims=True)
            pick = lane == first
            sel = sel | pick
            work = jnp.where(pick, REMOVED, work)
        msel_ref[0, :, h * LANES:(h + 1) * LANES] = jnp.where(sel & past, 0.0, NEG_INF)

    dq_ref[0] = _rope(group_norm(proj(PK_DQ, MXU_WIDTH), g32, P_DQ), tab_diff, ROT_DIFF // 2).astype(bf)
    dk_ref[0] = _rope(group_norm(proj(PK_DK, MXU_WIDTH), g32, P_DK), tab_diff, ROT_DIFF // 2).astype(bf)
    dv_ref[0] = proj(PK_DV, MXU_WIDTH).astype(bf)

    eq_ref[0] = group_norm(proj(PK_EQ, MXU_WIDTH), g64, P_EQ).astype(bf)


def _full_spec(shape):
    n = len(shape)
    return pl.BlockSpec(shape, lambda *_: (0,) * n)


def _prep_call(x, tr, anorm, win, wuq, wukvk, wukvv, gpair, g64, g32, expand, tril, par):
    bsz, seq, d = x.shape
    tm = MOBA_BLOCK
    assert seq % tm == 0 and seq // tm <= LANES
    bf = jnp.bfloat16

    def tok(width):
        return pl.BlockSpec((1, tm, width), lambda b, j: (b, j, 0))

    widths = [(2 * MXU_WIDTH, bf), (2 * MXU_WIDTH, bf), (GROUP_WIDTH, bf),
              (GROUP_WIDTH, bf), (GROUP_WIDTH, bf), (GROUP_WIDTH, bf), (LANES, jnp.float32),
              (GROUP_WIDTH, bf), (GROUP_WIDTH, bf), (GROUP_WIDTH, bf), (N_HEADS * LANES, jnp.float32),
              (GROUP_WIDTH, bf), (GROUP_WIDTH, bf), (GROUP_WIDTH, bf),
              (GROUP_WIDTH, bf)]
    consts = [anorm, win, wuq, wukvk, wukvv, gpair, g64, g32, expand, tril, par]
    return pl.pallas_call(
        _prep_body,
        grid=(bsz, seq // tm),
        in_specs=[tok(d), tok(TR_WIDTH)] + [_full_spec(c.shape) for c in consts],
        out_specs=[tok(w) for w, _ in widths],
        out_shape=[jax.ShapeDtypeStruct((bsz, seq, w), dt) for w, dt in widths],
        scratch_shapes=[pltpu.VMEM((LANES, GROUP_WIDTH), jnp.float32), pltpu.VMEM((1, LANES), jnp.float32)],
        compiler_params=pltpu.CompilerParams(dimension_semantics=("arbitrary", "arbitrary"),
                                             vmem_limit_bytes=VMEM_LIMIT_BYTES),
        name="prep",
    )(x, tr, *consts)


def _memkv_body(mem_ref, mnorm_ref, w_ref, g64_ref, gain_ref, k_ref, v_ref):
    m = mem_ref[0]
    mn = m * lax.rsqrt(jnp.mean(m * m, axis=-1, keepdims=True) + EPS) * mnorm_ref[...]
    kv = _dot(mn.astype(jnp.bfloat16), w_ref[...])
    k = kv[:, 0:GROUP_WIDTH]
    k = k * lax.rsqrt(_group_mean_sq(k, g64_ref[...]) + EPS) * gain_ref[...]
    k_ref[0] = k.astype(jnp.bfloat16)
    v_ref[0] = kv[:, GROUP_WIDTH:2 * GROUP_WIDTH].astype(jnp.bfloat16)


def _memkv_call(mem, mnorm, w, g64, gain):
    bsz, mlen, d = mem.shape
    spec_o = pl.BlockSpec((1, mlen, GROUP_WIDTH), lambda b: (b, 0, 0))
    return pl.pallas_call(
        _memkv_body,
        grid=(bsz,),
        in_specs=[pl.BlockSpec((1, mlen, d), lambda b: (b, 0, 0)), _full_spec(mnorm.shape), _full_spec(w.shape),
                  _full_spec(g64.shape), _full_spec(gain.shape)],
        out_specs=[spec_o, spec_o],
        out_shape=[jax.ShapeDtypeStruct((bsz, mlen, GROUP_WIDTH), jnp.bfloat16)] * 2,
        compiler_params=pltpu.CompilerParams(dimension_semantics=("arbitrary",), vmem_limit_bytes=VMEM_LIMIT_BYTES),
        name="mem_kv",
    )(mem, mnorm, w, g64, gain)


class _AttnCfg:
    def __init__(self, name, vheads, n_acc, causal, decay=False, select=False, diff=False):
        self.name = name
        self.vheads = vheads
        self.n_acc = n_acc
        self.causal = causal
        self.decay = decay
        self.select = select
        self.diff = diff


_PLAIN_VHEADS = [(0, h * HEAD_DIM, (h + 1) * HEAD_DIM, 0, h) for h in range(N_HEADS)]
_CFG_MLA = _AttnCfg("attn_mla", [((h // 2) * MXU_WIDTH, (h % 2) * PAIR_STRIDE, (h % 2) * PAIR_STRIDE + MLA_QK, 0, h)
                                 for h in range(N_HEADS)], 1, True)
_CFG_FOX = _AttnCfg("attn_fox", _PLAIN_VHEADS, 1, True, decay=True)
_CFG_MOBA = _AttnCfg("attn_moba", _PLAIN_VHEADS, 1, True, select=True)
_CFG_DIFF = _AttnCfg("attn_diff", [(0, h * HEAD_DIM + c * DIFF_QK, h * HEAD_DIM + (c + 1) * DIFF_QK, c, h)
                                   for c in range(2) for h in range(N_HEADS)], 2, True, diff=True)
_CFG_MEM = _AttnCfg("attn_mem", _PLAIN_VHEADS, 1, False)


def _head_lane_select(vals, shape):
    lane = lax.broadcasted_iota(jnp.int32, shape, 1)
    out = jnp.broadcast_to(vals[N_HEADS - 1], shape)
    for h in range(N_HEADS - 2, -1, -1):
        out = jnp.where(lane < (h + 1) * HEAD_DIM, vals[h], out)
    return out


def _attn_body(cfg, qi_ref, kj_ref, *refs):
    refs = list(refs)
    q_ref, k_ref, v_ref = refs[:3]
    pos = 3
    if cfg.decay:
        dcol_ref, drow_ref = refs[pos:pos + 2]
        pos += 2
    if cfg.select:
        sel_ref = refs[pos]
        pos += 1
    if cfg.diff:
        g64_ref, gsub_ref, lam_ref = refs[pos:pos + 3]
        pos += 3
    o_ref, qm_s, m_s, l_s, acc_s = refs[pos:pos + 5]

    t = pl.program_id(1)
    i = qi_ref[t]
    j = kj_ref[t]
    tq = q_ref.shape[1]
    tk = k_ref.shape[1]
    n_vh = len(cfg.vheads)

    @pl.when(j == 0)
    def _():
        for n, (off, lo, hi, _, _) in enumerate(cfg.vheads):
            qb = q_ref[0, :, off:off + MXU_WIDTH]
            qm_s[n] = jnp.where(_lane_mask(qb.shape, lo, hi), qb, jnp.zeros_like(qb))
        m_s[...] = jnp.full(m_s.shape, NEG_INF, jnp.float32)
        l_s[...] = jnp.zeros_like(l_s)
        acc_s[...] = jnp.zeros_like(acc_s)

    def step(diag):
        vb = v_ref[0]
        pv = [None] * cfg.n_acc
        alphas = [[None] * N_HEADS for _ in range(cfg.n_acc)]
        if diag:
            row = lax.broadcasted_iota(jnp.int32, (tq, tk), 0)
            col = lax.broadcasted_iota(jnp.int32, (tq, tk), 1)
            keep = row >= col
        for n, (off, lo, hi, a, h) in enumerate(cfg.vheads):
            s = _dot_nt(qm_s[n], k_ref[0, :, off:off + MXU_WIDTH])
            if cfg.decay:
                s = s + (dcol_ref[0, :, h:h + 1] - drow_ref[0, h:h + 1, :])
            if cfg.select and not diag:
                lane = lax.broadcasted_iota(jnp.int32, (tq, LANES), 1)
                bias = jnp.sum(jnp.where(lane == j, sel_ref[0, :, h * LANES:(h + 1) * LANES], 0.0),
                               axis=-1, keepdims=True)
                s = s + bias
            if diag:
                s = jnp.where(keep, s, NEG_INF)
            m_prev = m_s[n][:, 0:1]
            m_new = jnp.maximum(m_prev, jnp.max(s, axis=-1, keepdims=True))
            alpha = jnp.exp(m_prev - m_new)
            p = jnp.exp(s - m_new)
            l_s[n] = alpha * l_s[n] + jnp.sum(p, axis=-1, keepdims=True)
            m_s[n] = jnp.broadcast_to(m_new, (tq, LANES))
            vm = jnp.where(_lane_mask(vb.shape, h * HEAD_DIM, (h + 1) * HEAD_DIM), vb, jnp.zeros_like(vb))
            d = _dot(p.astype(jnp.bfloat16), vm)
            pv[a] = d if pv[a] is None else pv[a] + d
            alphas[a][h] = alpha
        for a in range(cfg.n_acc):
            acc_s[a] = acc_s[a] * _head_lane_select(alphas[a], (tq, GROUP_WIDTH)) + pv[a]

    if cfg.causal:
        pl.when(j < i)(functools.partial(step, False))
        pl.when(j == i)(functools.partial(step, True))
        last = j == i
    else:
        step(False)
        last = j == 0

    @pl.when(last)
    def _():
        outs = []
        for a in range(cfg.n_acc):
            ls = [l_s[a * N_HEADS + h][:, 0:1] for h in range(N_HEADS)]
            outs.append(acc_s[a] / _head_lane_select(ls, (tq, GROUP_WIDTH)))
        if cfg.diff:
            o = outs[0] - lam_ref[0:1, :] * outs[1]
            o = o * lax.rsqrt(_group_mean_sq(o, g64_ref[...]) + EPS) * gsub_ref[...]
        else:
            o = outs[0]
        o_ref[0] = o.astype(o_ref.dtype)


def _attn_call(cfg, q, k, v, extras, tq, tk):
    bsz, seq, wq = q.shape
    sk = k.shape[1]
    nq = seq // tq
    assert seq % tq == 0 and sk % tk == 0
    if cfg.causal:
        assert tq == tk and sk == seq
        pairs = [(i, j) for i in range(nq) for j in range(i + 1)]
    else:
        assert sk == tk
        pairs = [(i, 0) for i in range(nq)]
    qi = jnp.asarray(np.array([p[0] for p in pairs], np.int32))
    kj = jnp.asarray(np.array([p[1] for p in pairs], np.int32))
    n_vh = len(cfg.vheads)

    in_specs = [pl.BlockSpec((1, tq, wq), lambda b, t, qi, kj: (b, qi[t], 0)),
                pl.BlockSpec((1, tk, wq), lambda b, t, qi, kj: (b, kj[t], 0)),
                pl.BlockSpec((1, tk, GROUP_WIDTH), lambda b, t, qi, kj: (b, kj[t], 0))]
    args = [q, k, v]
    if cfg.decay:
        dcol, drow = extras
        in_specs += [pl.BlockSpec((1, tq, LANES), lambda b, t, qi, kj: (b, qi[t], 0)),
                     pl.BlockSpec((1, SUBLANES, tk), lambda b, t, qi, kj: (b, 0, kj[t]))]
        args += [dcol, drow]
    if cfg.select:
        (sel,) = extras
        in_specs += [pl.BlockSpec((1, tq, N_HEADS * LANES), lambda b, t, qi, kj: (b, qi[t], 0))]
        args += [sel]
    if cfg.diff:
        for c in extras:
            in_specs += [pl.BlockSpec(c.shape, lambda b, t, qi, kj: (0, 0))]
        args += list(extras)

    grid_spec = pltpu.PrefetchScalarGridSpec(
        num_scalar_prefetch=2,
        grid=(bsz, len(pairs)),
        in_specs=in_specs,
        out_specs=pl.BlockSpec((1, tq, GROUP_WIDTH), lambda b, t, qi, kj: (b, qi[t], 0)),
        scratch_shapes=[pltpu.VMEM((n_vh, tq, MXU_WIDTH), jnp.bfloat16),
                        pltpu.VMEM((n_vh, tq, LANES), jnp.float32),
                        pltpu.VMEM((n_vh, tq, LANES), jnp.float32),
                        pltpu.VMEM((cfg.n_acc, tq, GROUP_WIDTH), jnp.float32)])
    return pl.pallas_call(
        functools.partial(_attn_body, cfg),
        grid_spec=grid_spec,
        out_shape=jax.ShapeDtypeStruct((bsz, seq, GROUP_WIDTH), jnp.bfloat16),
        compiler_params=pltpu.CompilerParams(dimension_semantics=("arbitrary", "arbitrary"),
                                             vmem_limit_bytes=VMEM_LIMIT_BYTES),
        name=cfg.name,
    )(qi, kj, *args)


def _ffn_body(x_ref, oa_ref, ob_ref, oc_ref, od_ref, oe_ref, wo_ref, fnorm_ref, wg_ref, wu_ref, cw_ref, cb_ref, wd_ref,
              out_ref, xnew_s, xn_s, tail_s, acc_s):
    i = pl.program_id(1)
    f = pl.program_id(2)
    tm = x_ref.shape[1]

    @pl.when(f == 0)
    def _():
        @pl.when(i == 0)
        def _():
            tail_s[...] = jnp.zeros_like(tail_s)

        @pl.when(i > 0)
        def _():
            tail_s[...] = xn_s[tm - TAIL_ROWS:tm, :]

        xnew = x_ref[0]
        for g, o_ref in enumerate((oa_ref, ob_ref, oc_ref, od_ref, oe_ref)):
            xnew = xnew + _dot(o_ref[0], wo_ref[g])
        xnew_s[...] = xnew
        xn = xnew * lax.rsqrt(jnp.mean(xnew * xnew, axis=-1, keepdims=True) + EPS) * fnorm_ref[...]
        xn_s[...] = xn.astype(xn_s.dtype)
        acc_s[...] = jnp.zeros_like(acc_s)

    wg = wg_ref[...]
    g0 = _dot(xn_s[...], wg)
    gt = _dot(tail_s[...], wg)
    u = _dot(xn_s[...], wu_ref[...])
    t1 = gt[TAIL_ROWS - 1:TAIL_ROWS, :]
    t2 = gt[TAIL_ROWS - 2:TAIL_ROWS - 1, :]
    row = lax.broadcasted_iota(jnp.int32, g0.shape, 0)
    g1 = jnp.where(row == 0, t1, pltpu.roll(g0, 1, 0))
    g2 = jnp.where(row == 0, t2, jnp.where(row == 1, t1, pltpu.roll(g0, 2, 0)))
    y = cb_ref[...] + cw_ref[0:1, :] * g2
    y = y + cw_ref[1:2, :] * g1
    y = y + cw_ref[2:3, :] * g0
    hmid = (y * (1.0 / (1.0 + jnp.exp(-y)))) * u
    acc_s[...] += _dot(hmid.astype(jnp.bfloat16), wd_ref[...])

    @pl.when(f == pl.num_programs(2) - 1)
    def _():
        out_ref[0] = xnew_s[...] + acc_s[...]


def _ffn_call(x, outs, wo, fnorm, wg, wu, cw, cb, wd, tm, tf):
    bsz, seq, d = x.shape
    dff = wg.shape[1]
    assert seq % tm == 0 and dff % tf == 0
    tok = lambda w: pl.BlockSpec((1, tm, w), lambda b, i, f: (b, i, 0))
    return pl.pallas_call(
        _ffn_body,
        grid=(bsz, seq // tm, dff // tf),
        in_specs=[tok(d)] + [tok(GROUP_WIDTH)] * 5 + [
            pl.BlockSpec(wo.shape, lambda b, i, f: (0, 0, 0)),
            pl.BlockSpec(fnorm.shape, lambda b, i, f: (0, 0)),
            pl.BlockSpec((d, tf), lambda b, i, f: (0, f)),
            pl.BlockSpec((d, tf), lambda b, i, f: (0, f)),
            pl.BlockSpec((SUBLANES, tf), lambda b, i, f: (0, f)),
            pl.BlockSpec((1, tf), lambda b, i, f: (0, f)),
            pl.BlockSpec((tf, d), lambda b, i, f: (f, 0))],
        out_specs=tok(d),
        out_shape=jax.ShapeDtypeStruct((bsz, seq, d), jnp.float32),
        scratch_shapes=[pltpu.VMEM((tm, d), jnp.float32), pltpu.VMEM((tm, d), jnp.bfloat16),
                        pltpu.VMEM((TAIL_ROWS, d), jnp.bfloat16), pltpu.VMEM((tm, d), jnp.float32)],
        compiler_params=pltpu.CompilerParams(dimension_semantics=("arbitrary", "arbitrary", "arbitrary"),
                                             vmem_limit_bytes=VMEM_LIMIT_BYTES),
        name="ffn",
    )(x, *outs, wo, fnorm, wg, wu, cw, cb, wd)


def _pad_row(v, width=MXU_WIDTH):
    v = v.astype(jnp.float32).reshape(-1)
    return jnp.pad(v, (0, width - v.shape[0]))


def _pair_gain(g):
    one = jnp.concatenate([g.astype(jnp.float32), g.astype(jnp.float32),
                           jnp.zeros((MXU_WIDTH - 2 * MLA_QK,), jnp.float32)])
    return one


def _rope_table(positions):
    pos = positions.astype(jnp.float32)[:, :, None]
    cols_c, cols_s = [], []
    for rot in (MLA_ROPE, ROT_MOBA, ROT_DIFF):
        inv = ROPE_THETA ** (-jnp.arange(0, rot, 2, dtype=jnp.float32) / rot)
        ang = pos * inv
        cols_c.append(jnp.cos(ang))
        cols_s.append(jnp.sin(ang))
    c = jnp.concatenate(cols_c, axis=-1)
    s = jnp.concatenate(cols_s, axis=-1)
    c_hi = c.astype(jnp.bfloat16)
    c_lo = (c - c_hi.astype(jnp.float32)).astype(jnp.bfloat16)
    s_hi = s.astype(jnp.bfloat16)
    s_lo = (s - s_hi.astype(jnp.float32)).astype(jnp.bfloat16)
    one = jnp.ones(pos.shape[:2] + (1,), jnp.bfloat16)
    pad = jnp.zeros(pos.shape[:2] + (TR_WIDTH - TR_ONE - 1,), jnp.bfloat16)
    return jnp.concatenate([c_hi, c_lo, s_hi, s_lo, one, pad], axis=-1)


def _pick_tile(n, pref):
    t = pref
    while n % t:
        t //= 2
    return t


def kernel(x, mem, positions, attn_norm, ffn_norm, mem_norm, w_in, mla_cq_norm, mla_ckv_norm, mla_w_uq, mla_w_ukv, mla_q_norm, mla_k_norm, fox_b_f, fox_q_norm, fox_k_norm, moba_q_norm, moba_k_norm, diff_lambda, diff_q_norm, diff_k_norm, diff_sub_norm, mem_w_kv, mem_q_norm, mem_k_norm, w_o, ffn_w_gate, ffn_w_up, ffn_conv_w, ffn_conv_b, ffn_w_down):
    bsz, seq, d = x.shape
    depth = w_in.shape[0]
    dff = ffn_w_gate.shape[2]
    bf = jnp.bfloat16
    f32 = jnp.float32

    in_idx = _np_in_index()
    uq_idx = _np_uq_index()
    ukvk_idx, ukvv_idx = _np_ukv_index()
    gpair = jnp.asarray(_np_group_matrix(_PAIR_GROUPS), bf)
    g64 = jnp.asarray(_np_group_matrix(_G64_GROUPS), bf)
    g32 = jnp.asarray(_np_group_matrix(_G32_GROUPS), bf)
    expand = jnp.asarray(_np_rope_expand_all(), bf)
    tril = jnp.asarray(np.tril(np.ones((MOBA_BLOCK, MOBA_BLOCK), np.float32)), bf)
    tr = _rope_table(positions)

    t_dense = _pick_tile(seq, 512)
    t_ffn = _pick_tile(seq, 512)
    tf = dff // 2 if (dff // 2) % LANES == 0 else dff

    for l in range(depth):
        win = jnp.concatenate([w_in[l], jnp.zeros((d, 1), f32)], axis=1)[:, in_idx].astype(bf)
        wuq = jnp.concatenate([mla_w_uq[l], jnp.zeros((MLA_Q_RANK, 1), f32)], axis=1)[:, uq_idx]
        wuq = jnp.pad(wuq, ((0, MXU_WIDTH - MLA_Q_RANK), (0, 0))).astype(bf)
        wukv = jnp.concatenate([mla_w_ukv[l], jnp.zeros((MLA_KV_RANK, 1), f32)], axis=1)
        wukvk = wukv[:, ukvk_idx].astype(bf)
        wukvv = wukv[:, ukvv_idx].astype(bf)
        rows = [None] * P_ROWS
        rows[P_CQ] = _pad_row(mla_cq_norm[l])
        rows[P_CKV] = _pad_row(mla_ckv_norm[l])
        rows[P_GQ] = _pair_gain(mla_q_norm[l]) * (MLA_QK ** -0.5)
        rows[P_GK] = _pair_gain(mla_k_norm[l])
        rows[P_FQ] = jnp.tile(fox_q_norm[l].astype(f32), N_HEADS) * (HEAD_DIM ** -0.5)
        rows[P_FK] = jnp.tile(fox_k_norm[l].astype(f32), N_HEADS)
        rows[P_FB] = _pad_row(fox_b_f[l])
        rows[P_MQ] = jnp.tile(moba_q_norm[l].astype(f32), N_HEADS)
        rows[P_MK] = jnp.tile(moba_k_norm[l].astype(f32), N_HEADS)
        rows[P_DQ] = jnp.tile(diff_q_norm[l].astype(f32), 2 * N_HEADS) * (DIFF_QK ** -0.5)
        rows[P_DK] = jnp.tile(diff_k_norm[l].astype(f32), 2 * N_HEADS)
        rows[P_EQ] = jnp.tile(mem_q_norm[l].astype(f32), N_HEADS) * (HEAD_DIM ** -0.5)
        par = jnp.stack([r if r is not None else jnp.zeros((MXU_WIDTH,), f32) for r in rows])

        (aq, ak, av, fq, fk, fv, fdec, mq, mk, mv, msel, dq, dk, dv, eq) = _prep_call(
            x, tr, attn_norm[l].reshape(1, d).astype(f32), win, wuq, wukvk, wukvv, gpair, g64, g32, expand, tril, par)
        ek, ev = _memkv_call(mem, mem_norm[l].reshape(1, d).astype(f32), mem_w_kv[l].astype(bf), g64,
                             jnp.tile(mem_k_norm[l].astype(f32), N_HEADS).reshape(1, GROUP_WIDTH))

        drow = jnp.transpose(fdec[:, :, 0:SUBLANES], (0, 2, 1))
        lam_vec = diff_lambda[l].astype(f32)
        lam_init = 0.8 - 0.6 * math.exp(-0.3 * l)
        lam = (jnp.exp(jnp.sum(lam_vec[0] * lam_vec[1])) - jnp.exp(jnp.sum(lam_vec[2] * lam_vec[3])) + lam_init)
        lam_row = jnp.full((1, GROUP_WIDTH), 1.0, f32) * lam
        gsub = (jnp.tile(diff_sub_norm[l].astype(f32), N_HEADS) * (1.0 - lam_init)).reshape(1, GROUP_WIDTH)

        o_a = _attn_call(_CFG_MLA, aq, ak, av, (), t_dense, t_dense)
        o_b = _attn_call(_CFG_FOX, fq, fk, fv, (fdec, drow), t_dense, t_dense)
        o_c = _attn_call(_CFG_MOBA, mq, mk, mv, (msel,), MOBA_BLOCK, MOBA_BLOCK)
        o_d = _attn_call(_CFG_DIFF, dq, dk, dv, (g64, gsub, lam_row), t_dense, t_dense)
        o_e = _attn_call(_CFG_MEM, eq, ek, ev, (), t_dense, mem.shape[1])

        cw = jnp.pad(ffn_conv_w[l].astype(f32), ((0, SUBLANES - CONV_WIDTH), (0, 0)))
        x = _ffn_call(x, (o_a, o_b, o_c, o_d, o_e), w_o[l].reshape(5, GROUP_WIDTH, d).astype(bf),
                      ffn_norm[l].reshape(1, d).astype(f32), ffn_w_gate[l].astype(bf), ffn_w_up[l].astype(bf),
                      cw, ffn_conv_b[l].reshape(1, dff).astype(f32), ffn_w_down[l].astype(bf), t_ffn, tf)
    return x
```

```python
import functools
import math

import numpy as np
import jax
import jax.numpy as jnp
from jax import lax
from jax.experimental import pallas as pl
from jax.experimental.pallas import tpu as pltpu

N_HEADS = 4
HEAD_DIM = 64
GROUP_WIDTH = N_HEADS * HEAD_DIM
MLA_Q_RANK = 192
MLA_KV_RANK = 128
MLA_NOPE = 64
MLA_ROPE = 32
MLA_QK = MLA_NOPE + MLA_ROPE
DIFF_QK = HEAD_DIM // 2
ROPE_THETA = 500000.0
ROT_MOBA = HEAD_DIM // 4
ROT_DIFF = DIFF_QK // 4
MOBA_BLOCK = 256
MOBA_TOPK = 3
CONV_WIDTH = 3
EPS = 1e-6
NEG_INF = -1e30
LOG2E = math.log2(math.e)
REMOVED = -3e38

LANES = 128
SUBLANES = 8
MXU_WIDTH = 256
TAIL_ROWS = 16
VMEM_LIMIT_BYTES = 56 * 1024 * 1024

_SRC_CQ = 0
_SRC_CKV = _SRC_CQ + MLA_Q_RANK
_SRC_KR = _SRC_CKV + MLA_KV_RANK
_SRC_FOX = _SRC_KR + MLA_ROPE
_SRC_FOXF = _SRC_FOX + 3 * GROUP_WIDTH
_SRC_MOBA = _SRC_FOXF + N_HEADS
_SRC_DIFF = _SRC_MOBA + 3 * GROUP_WIDTH
_SRC_MEMQ = _SRC_DIFF + 3 * GROUP_WIDTH
_SRC_END = _SRC_MEMQ + GROUP_WIDTH

PK_CQ = 0
PK_CKV = 256
PK_KR = 384
PK_FQ, PK_FK, PK_FV = 896, 1152, 1408
PK_FF = 1664
PK_MQ, PK_MK, PK_MV = 1792, 2048, 2304
PK_DQ, PK_DK, PK_DV = 2560, 2816, 3072
PK_EQ = 3328
PK_END = 3584

PAIR_STRIDE = MLA_QK


def _pair_lane(h, d):
    return (h // 2) * MXU_WIDTH + (h % 2) * PAIR_STRIDE + d


N_FREQ = MLA_ROPE // 2 + ROT_MOBA // 2 + ROT_DIFF // 2
FREQ_BASE_MLA = 0
FREQ_BASE_MOBA = MLA_ROPE // 2
FREQ_BASE_DIFF = FREQ_BASE_MOBA + ROT_MOBA // 2
TR_ONE = 4 * N_FREQ
TR_WIDTH = LANES

(P_CQ, P_CKV, P_GQ, P_GK, P_FQ, P_FK, P_FB, P_MQ, P_MK, P_DQ, P_DK, P_EQ) = range(12)
P_ROWS = 16


def _np_in_index():
    idx = np.full((PK_END,), _SRC_END, np.int32)
    idx[PK_CQ:PK_CQ + MLA_Q_RANK] = np.arange(_SRC_CQ, _SRC_CQ + MLA_Q_RANK)
    idx[PK_CKV:PK_CKV + MLA_KV_RANK] = np.arange(_SRC_CKV, _SRC_CKV + MLA_KV_RANK)
    for h in range(N_HEADS):
        for d in range(MLA_ROPE):
            idx[PK_KR + _pair_lane(h, d)] = _SRC_KR + d
    idx[PK_FQ:PK_FQ + 3 * GROUP_WIDTH] = np.arange(_SRC_FOX, _SRC_FOX + 3 * GROUP_WIDTH)
    idx[PK_FF:PK_FF + N_HEADS] = np.arange(_SRC_FOXF, _SRC_FOXF + N_HEADS)
    idx[PK_MQ:PK_MQ + 3 * GROUP_WIDTH] = np.arange(_SRC_MOBA, _SRC_MOBA + 3 * GROUP_WIDTH)
    idx[PK_DQ:PK_DQ + 3 * GROUP_WIDTH] = np.arange(_SRC_DIFF, _SRC_DIFF + 3 * GROUP_WIDTH)
    idx[PK_EQ:PK_EQ + GROUP_WIDTH] = np.arange(_SRC_MEMQ, _SRC_MEMQ + GROUP_WIDTH)
    return idx


def _np_uq_index():
    idx = np.full((2 * MXU_WIDTH,), N_HEADS * MLA_QK, np.int32)
    for h in range(N_HEADS):
        for d in range(MLA_QK):
            idx[_pair_lane(h, d)] = h * MLA_QK + d
    return idx


def _np_ukv_index():
    zero = N_HEADS * (MLA_NOPE + HEAD_DIM)
    idx_k = np.full((2 * MXU_WIDTH,), zero, np.int32)
    idx_v = np.zeros((GROUP_WIDTH,), np.int32)
    for h in range(N_HEADS):
        for d in range(MLA_NOPE):
            idx_k[_pair_lane(h, MLA_ROPE + d)] = h * (MLA_NOPE + HEAD_DIM) + d
        for d in range(HEAD_DIM):
            idx_v[h * HEAD_DIM + d] = h * (MLA_NOPE + HEAD_DIM) + MLA_NOPE + d
    return idx_k, idx_v


def _np_group_matrix(groups):
    g = np.zeros((MXU_WIDTH, MXU_WIDTH), np.float32)
    for lo, size in groups:
        g[lo:lo + size, lo:lo + size] = 1.0 / size
    return g


_PAIR_GROUPS = [(0, MLA_ROPE), (MLA_ROPE, MLA_NOPE), (PAIR_STRIDE, MLA_ROPE), (PAIR_STRIDE + MLA_ROPE, MLA_NOPE)]
_G64_GROUPS = [(h * HEAD_DIM, HEAD_DIM) for h in range(N_HEADS)]
_G32_GROUPS = [(g * DIFF_QK, DIFF_QK) for g in range(2 * N_HEADS)]


def _np_rope_expand(regions, rot, base):
    half = rot // 2
    e = np.zeros((TR_WIDTH, 3 * MXU_WIDTH), np.float32)
    e[TR_ONE, 0:MXU_WIDTH] = 1.0
    for lo in regions:
        for r in range(half):
            f = base + r
            for lane, tab, sign in ((lo + r, 2, -1.0), (lo + half + r, 1, 1.0)):
                e[TR_ONE, lane] = 0.0
                e[f, lane] = 1.0
                e[N_FREQ + f, lane] = 1.0
                e[2 * N_FREQ + f, tab * MXU_WIDTH + lane] = sign
                e[3 * N_FREQ + f, tab * MXU_WIDTH + lane] = sign
    return e


def _np_rope_expand_all():
    return np.concatenate([
        _np_rope_expand([0, PAIR_STRIDE], MLA_ROPE, FREQ_BASE_MLA),
        _np_rope_expand([h * HEAD_DIM for h in range(N_HEADS)], ROT_MOBA, FREQ_BASE_MOBA),
        _np_rope_expand([g * DIFF_QK for g in range(2 * N_HEADS)], ROT_DIFF, FREQ_BASE_DIFF),
    ], axis=1)


def _dot(a, b):
    return jnp.dot(a, b, preferred_element_type=jnp.float32)


def _dot_nt(a, b):
    return lax.dot_general(a, b, (((1,), (1,)), ((), ())), preferred_element_type=jnp.float32)


def _split2(a):
    hi = a.astype(jnp.bfloat16)
    lo = (a - hi.astype(jnp.float32)).astype(jnp.bfloat16)
    return hi, lo


def _split3(a):
    hi = a.astype(jnp.bfloat16)
    r = a - hi.astype(jnp.float32)
    mid = r.astype(jnp.bfloat16)
    lo = (r - mid.astype(jnp.float32)).astype(jnp.bfloat16)
    return hi, mid, lo


def _group_mean_sq(a, g_bf16):
    hi, lo = _split2(a * a)
    return _dot(hi, g_bf16) + _dot(lo, g_bf16)


def _rope(x, tabs, half):
    w = x.shape[-1]
    c, s1, s2 = tabs[:, 0:w], tabs[:, w:2 * w], tabs[:, 2 * w:3 * w]
    return x * c + pltpu.roll(x, half, 1) * s1 + pltpu.roll(x, w - half, 1) * s2


def _lane_mask(shape, lo, hi):
    lane = lax.broadcasted_iota(jnp.int32, shape, len(shape) - 1)
    return (lane >= lo) & (lane < hi)


def _prep_body(x_ref, tr_ref, anorm_ref, win_ref, wuq_ref, wukvk_ref, wukvv_ref, gpair_ref, g64_ref, g32_ref,
               exp_ref, tril_ref, rep_ref, par_ref,
               aq_ref, ak_ref, av_ref, fq_ref, fk_ref, fv_ref, fdcol_ref, fdrow_ref, mq_ref, mk_ref, mv_ref, msel_ref,
               dq_ref, dk_ref, dv_ref, eq_ref,
               kmean_s, carry_s):
    j = pl.program_id(1)
    tm = x_ref.shape[1]
    bf = jnp.bfloat16

    @pl.when(j == 0)
    def _():
        kmean_s[...] = jnp.zeros_like(kmean_s)
        carry_s[...] = jnp.zeros_like(carry_s)

    x = x_ref[0]
    xn = x * lax.rsqrt(jnp.mean(x * x, axis=-1, keepdims=True) + EPS) * anorm_ref[...]
    xb = xn.astype(bf)

    def proj(off, width):
        return _dot(xb, win_ref[:, off:off + width])

    def prow(r, width=MXU_WIDTH):
        return par_ref[r:r + 1, 0:width]

    tabs = _dot(tr_ref[0], exp_ref[...])
    tab_mla = tabs[:, 0:3 * MXU_WIDTH]
    tab_moba = tabs[:, 3 * MXU_WIDTH:6 * MXU_WIDTH]
    tab_diff = tabs[:, 6 * MXU_WIDTH:9 * MXU_WIDTH]
    gpair, g64, g32 = gpair_ref[...], g64_ref[...], g32_ref[...]

    def group_norm(a, g, gain_row):
        return a * lax.rsqrt(_group_mean_sq(a, g) + EPS) * prow(gain_row)

    cq = proj(PK_CQ, MXU_WIDTH)
    cqn = cq * lax.rsqrt(jnp.sum(cq * cq, axis=-1, keepdims=True) * (1.0 / MLA_Q_RANK) + EPS) * prow(P_CQ)
    qa = _dot(cqn.astype(bf), wuq_ref[...])
    ckv = proj(PK_CKV, MLA_KV_RANK)
    ckvn = ckv * lax.rsqrt(jnp.mean(ckv * ckv, axis=-1, keepdims=True) + EPS) * prow(P_CKV, MLA_KV_RANK)
    ckvb = ckvn.astype(bf)
    ka = proj(PK_KR, 2 * MXU_WIDTH) + _dot(ckvb, wukvk_ref[...])
    for p in range(2):
        sl = slice(p * MXU_WIDTH, (p + 1) * MXU_WIDTH)
        aq_ref[0, :, sl] = _rope(group_norm(qa[:, sl], gpair, P_GQ), tab_mla, MLA_ROPE // 2).astype(bf)
        ak_ref[0, :, sl] = _rope(group_norm(ka[:, sl], gpair, P_GK), tab_mla, MLA_ROPE // 2).astype(bf)
    av_ref[0] = _values_even_odd(_dot(ckvb, wukvv_ref[...]))

    fq_ref[0] = group_norm(proj(PK_FQ, MXU_WIDTH), g64, P_FQ).astype(bf)
    fk_ref[0] = group_norm(proj(PK_FK, MXU_WIDTH), g64, P_FK).astype(bf)
    fv_ref[0] = _values_even_odd(proj(PK_FV, MXU_WIDTH))
    z = proj(PK_FF, LANES) + prow(P_FB, LANES)
    log_f = jnp.minimum(z, 0.0) - jnp.log1p(jnp.exp(-jnp.abs(z)))
    log_f = jnp.where(_lane_mask(log_f.shape, 0, N_HEADS), log_f, 0.0)
    tril = tril_ref[...]
    l1, l2, l3 = _split3(log_f)
    dec = carry_s[...] + ((_dot(tril, l1) + _dot(tril, l2)) + _dot(tril, l3))
    carry_s[...] = dec[tm - 1:tm, :]
    d1, d2, d3 = _split3(dec * LOG2E)
    rep = rep_ref[...]
    fdcol_ref[0] = (_dot(d1, rep) + _dot(d2, rep)) + _dot(d3, rep)
    row_sel = jnp.where(lax.broadcasted_iota(jnp.int32, (SUBLANES, LANES), 0)
                        == lax.broadcasted_iota(jnp.int32, (SUBLANES, LANES), 1), 1.0, 0.0).astype(bf)
    fdrow_ref[0] = (_dot_nt(row_sel, d1) + _dot_nt(row_sel, d2)) + _dot_nt(row_sel, d3)

    mq = _rope(group_norm(proj(PK_MQ, MXU_WIDTH), g64, P_MQ), tab_moba, ROT_MOBA // 2)
    mk = _rope(group_norm(proj(PK_MK, MXU_WIDTH), g64, P_MK), tab_moba, ROT_MOBA // 2)
    mq_ref[0] = (mq * (LOG2E * HEAD_DIM ** -0.5)).astype(bf)
    mk_ref[0] = mk.astype(bf)
    mv_ref[0] = _values_even_odd(proj(PK_MV, MXU_WIDTH))
    kmean_s[pl.ds(j, 1), :] = jnp.mean(mk, axis=0, keepdims=True)
    km_hi, km_lo = _split2(kmean_s[...])
    lane = lax.broadcasted_iota(jnp.int32, (tm, LANES), 1)
    past = lane < j
    for h in range(N_HEADS):
        q_hi, q_lo = _split2(jnp.where(_lane_mask(mq.shape, h * HEAD_DIM, (h + 1) * HEAD_DIM), mq, 0.0))
        gate = (_dot_nt(q_hi, km_hi) + _dot_nt(q_hi, km_lo)) + _dot_nt(q_lo, km_hi)
        work = jnp.where(past, gate, NEG_INF)
        sel = jnp.zeros((tm, LANES), jnp.bool_)
        for _ in range(MOBA_TOPK):
            mx = jnp.max(work, axis=-1, keepdims=True)
            first = jnp.min(jnp.where(work == mx, lane, LANES), axis=-1, keepdims=True)
            pick = lane == first
            sel = sel | pick
            work = jnp.where(pick, REMOVED, work)
        msel_ref[0, :, h * LANES:(h + 1) * LANES] = jnp.where(sel & past, 0.0, NEG_INF)

    dq_ref[0] = _rope(group_norm(proj(PK_DQ, MXU_WIDTH), g32, P_DQ), tab_diff, ROT_DIFF // 2).astype(bf)
    dk_ref[0] = _rope(group_norm(proj(PK_DK, MXU_WIDTH), g32, P_DK), tab_diff, ROT_DIFF // 2).astype(bf)
    dv_ref[0] = _values_even_odd(proj(PK_DV, MXU_WIDTH))

    eq_ref[0] = group_norm(proj(PK_EQ, MXU_WIDTH), g64, P_EQ).astype(bf)


def _full_spec(shape):
    n = len(shape)
    return pl.BlockSpec(shape, lambda *_: (0,) * n)


def _prep_call(x, tr, anorm, win, wuq, wukvk, wukvv, gpair, g64, g32, expand, tril, rep, par):
    bsz, seq, d = x.shape
    tm = MOBA_BLOCK
    assert seq % tm == 0 and seq // tm <= LANES
    bf = jnp.bfloat16

    def tok(width):
        return pl.BlockSpec((1, tm, width), lambda b, j: (b, j, 0))

    f32 = jnp.float32
    widths = [(2 * MXU_WIDTH, bf), (2 * MXU_WIDTH, bf), (V_WIDTH, bf),
              (GROUP_WIDTH, bf), (GROUP_WIDTH, bf), (V_WIDTH, bf),
              (N_HEADS * LANES, f32), (None, f32),
              (GROUP_WIDTH, bf), (GROUP_WIDTH, bf), (V_WIDTH, bf), (N_HEADS * LANES, f32),
              (GROUP_WIDTH, bf), (GROUP_WIDTH, bf), (V_WIDTH, bf),
              (GROUP_WIDTH, bf)]
    consts = [anorm, win, wuq, wukvk, wukvv, gpair, g64, g32, expand, tril, rep, par]
    return pl.pallas_call(
        _prep_body,
        grid=(bsz, seq // tm),
        in_specs=[tok(d), tok(TR_WIDTH)] + [_full_spec(c.shape) for c in consts],
        out_specs=[tok(w) if w else pl.BlockSpec((1, SUBLANES, tm), lambda b, j: (b, 0, j)) for w, _ in widths],
        out_shape=[jax.ShapeDtypeStruct((bsz, seq, w) if w else (bsz, SUBLANES, seq), dt) for w, dt in widths],
        scratch_shapes=[pltpu.VMEM((LANES, GROUP_WIDTH), jnp.float32), pltpu.VMEM((1, LANES), jnp.float32)],
        compiler_params=pltpu.CompilerParams(dimension_semantics=("arbitrary", "arbitrary"),
                                             vmem_limit_bytes=VMEM_LIMIT_BYTES),
        name="prep",
    )(x, tr, *consts)


def _memkv_body(mem_ref, mnorm_ref, w_ref, g64_ref, gain_ref, k_ref, v_ref):
    m = mem_ref[0]
    mn = m * lax.rsqrt(jnp.mean(m * m, axis=-1, keepdims=True) + EPS) * mnorm_ref[...]
    kv = _dot(mn.astype(jnp.bfloat16), w_ref[...])
    k = kv[:, 0:GROUP_WIDTH]
    k = k * lax.rsqrt(_group_mean_sq(k, g64_ref[...]) + EPS) * gain_ref[...]
    k_ref[0] = k.astype(jnp.bfloat16)
    v_ref[0] = _values_even_odd(kv[:, GROUP_WIDTH:2 * GROUP_WIDTH])


def _memkv_call(mem, mnorm, w, g64, gain):
    bsz, mlen, d = mem.shape
    return pl.pallas_call(
        _memkv_body,
        grid=(bsz,),
        in_specs=[pl.BlockSpec((1, mlen, d), lambda b: (b, 0, 0)), _full_spec(mnorm.shape), _full_spec(w.shape),
                  _full_spec(g64.shape), _full_spec(gain.shape)],
        out_specs=[pl.BlockSpec((1, mlen, w), lambda b: (b, 0, 0)) for w in (GROUP_WIDTH, V_WIDTH)],
        out_shape=[jax.ShapeDtypeStruct((bsz, mlen, w), jnp.bfloat16) for w in (GROUP_WIDTH, V_WIDTH)],
        compiler_params=pltpu.CompilerParams(dimension_semantics=("arbitrary",), vmem_limit_bytes=VMEM_LIMIT_BYTES),
        name="mem_kv",
    )(mem, mnorm, w, g64, gain)


class _AttnCfg:
    def __init__(self, name, vheads, n_maps, causal, decay=False, select=False, diff=False):
        self.name = name
        self.vheads = vheads
        self.n_maps = n_maps
        self.n_acc = 2 * n_maps
        self.causal = causal
        self.decay = decay
        self.select = select
        self.diff = diff


_PLAIN_VHEADS = [(0, h * HEAD_DIM, (h + 1) * HEAD_DIM, 0, h) for h in range(N_HEADS)]
_CFG_MLA = _AttnCfg("attn_mla", [((h // 2) * MXU_WIDTH, (h % 2) * PAIR_STRIDE, (h % 2) * PAIR_STRIDE + MLA_QK, 0, h)
                                 for h in range(N_HEADS)], 1, True)
_CFG_FOX = _AttnCfg("attn_fox", _PLAIN_VHEADS, 1, True, decay=True)
_CFG_MOBA = _AttnCfg("attn_moba", _PLAIN_VHEADS, 1, True, select=True)
_CFG_DIFF = _AttnCfg("attn_diff", [(0, h * HEAD_DIM + c * DIFF_QK, h * HEAD_DIM + (c + 1) * DIFF_QK, c, h)
                                   for c in range(2) for h in range(N_HEADS)], 2, True, diff=True)
_CFG_MEM = _AttnCfg("attn_mem", _PLAIN_VHEADS, 1, False)


_ONES_LANE = (HEAD_DIM, 0, 3 * HEAD_DIM, 2 * HEAD_DIM)
V_WIDTH = 2 * GROUP_WIDTH


def _values_even_odd(v):
    lane = lax.broadcasted_iota(jnp.int32, v.shape, 1)
    even_head = (lane & HEAD_DIM) == 0
    unit = jnp.where((lane & (HEAD_DIM - 1)) == 0, 1.0, 0.0)
    return jnp.concatenate([jnp.where(even_head, v, unit), jnp.where(even_head, unit, v)],
                           axis=1).astype(jnp.bfloat16)


def _head_lane_select(vals, shape):
    lane = lax.broadcasted_iota(jnp.int32, shape, 1)
    out = jnp.broadcast_to(vals[N_HEADS - 1], shape)
    for h in range(N_HEADS - 2, -1, -1):
        out = jnp.where(lane < (h + 1) * HEAD_DIM, vals[h], out)
    return out


def _tile_lanes(x, width):
    return jnp.tile(x, (1, width // LANES)) if width != LANES else x


def _attn_body(cfg, qi_ref, kj_ref, *refs):
    refs = list(refs)
    q_ref, k_ref, v_ref = refs[:3]
    pos = 3
    if cfg.decay:
        dcol_ref, drow_ref = refs[pos:pos + 2]
        pos += 2
    if cfg.select:
        sel_ref = refs[pos]
        pos += 1
    if cfg.diff:
        g64_ref, gsub_ref, lam_ref = refs[pos:pos + 3]
        pos += 3
    o_ref, qm_s, m_s, acc_s = refs[pos:pos + 4]

    t = pl.program_id(1)
    i = qi_ref[t]
    j = kj_ref[t]
    tq = q_ref.shape[1]
    tk = k_ref.shape[1]

    @pl.when(j == 0)
    def _():
        for n, (off, lo, hi, _, _) in enumerate(cfg.vheads):
            qb = q_ref[0, :, off:off + MXU_WIDTH]
            qm_s[n] = jnp.where(_lane_mask(qb.shape, lo, hi), qb, jnp.zeros_like(qb))
        m_s[...] = jnp.full(m_s.shape, NEG_INF, jnp.float32)
        acc_s[...] = jnp.zeros_like(acc_s)

    def step(diag):
        pv = [[None, None] for _ in range(cfg.n_acc)]
        alphas = [[None, None] for _ in range(cfg.n_acc)]
        if diag:
            row = lax.broadcasted_iota(jnp.int32, (tq, tk), 0)
            col = lax.broadcasted_iota(jnp.int32, (tq, tk), 1)
            keep = row >= col
        for n, (off, lo, hi, c, h) in enumerate(cfg.vheads):
            s = _dot_nt(qm_s[n], k_ref[0, :, off:off + MXU_WIDTH])
            if cfg.decay:
                s = s + (_tile_lanes(dcol_ref[0, :, h * LANES:(h + 1) * LANES], tk) - drow_ref[0, h:h + 1, :])
            if cfg.select and not diag:
                lane = lax.broadcasted_iota(jnp.int32, (tq, LANES), 1)
                bias = jnp.sum(jnp.where(lane == j, sel_ref[0, :, h * LANES:(h + 1) * LANES], 0.0),
                               axis=-1, keepdims=True)
                s = s + bias
            if diag:
                s = jnp.where(keep, s, NEG_INF)
            m_prev = m_s[n]
            m_new = jnp.maximum(m_prev, jnp.max(s, axis=-1, keepdims=True))
            alpha = jnp.exp2(m_prev - m_new)
            p = jnp.exp2(s - _tile_lanes(m_new, tk))
            m_s[n] = m_new
            a = 2 * c + h % 2
            vcol = (h % 2) * GROUP_WIDTH + (h // 2) * LANES
            pv[a][h // 2] = _dot(p.astype(jnp.bfloat16), v_ref[0, :, vcol:vcol + LANES])
            alphas[a][h // 2] = alpha
        for a in range(cfg.n_acc):
            acc_s[a] = acc_s[a] * jnp.concatenate(alphas[a], axis=1) + jnp.concatenate(pv[a], axis=1)

    if cfg.causal:
        pl.when(j < i)(functools.partial(step, False))
        pl.when(j == i)(functools.partial(step, True))
        last = j == i
    else:
        step(False)
        last = j == 0

    @pl.when(last)
    def _():
        lane = lax.broadcasted_iota(jnp.int32, (tq, GROUP_WIDTH), 1)
        even_head = (lane & HEAD_DIM) == 0
        outs = []
        for c in range(cfg.n_maps):
            accs = (acc_s[2 * c], acc_s[2 * c + 1])
            ls = [jnp.sum(jnp.where(lane == _ONES_LANE[h], accs[h % 2], 0.0), axis=-1, keepdims=True)
                  for h in range(N_HEADS)]
            outs.append(jnp.where(even_head, accs[0], accs[1]) / _head_lane_select(ls, (tq, GROUP_WIDTH)))
        if cfg.diff:
            o = outs[0] - lam_ref[0:1, :] * outs[1]
            o = o * lax.rsqrt(_group_mean_sq(o, g64_ref[...]) + EPS) * gsub_ref[...]
        else:
            o = outs[0]
        o_ref[0] = o.astype(o_ref.dtype)


def _attn_call(cfg, q, k, v, extras, tq, tk):
    bsz, seq, wq = q.shape
    sk = k.shape[1]
    nq = seq // tq
    assert seq % tq == 0 and sk % tk == 0
    if cfg.causal:
        assert tq == tk and sk == seq
        pairs = [(i, j) for i in range(nq) for j in range(i + 1)]
    else:
        assert sk == tk
        pairs = [(i, 0) for i in range(nq)]
    qi = jnp.asarray(np.array([p[0] for p in pairs], np.int32))
    kj = jnp.asarray(np.array([p[1] for p in pairs], np.int32))
    n_vh = len(cfg.vheads)

    in_specs = [pl.BlockSpec((1, tq, wq), lambda b, t, qi, kj: (b, qi[t], 0)),
                pl.BlockSpec((1, tk, wq), lambda b, t, qi, kj: (b, kj[t], 0)),
                pl.BlockSpec((1, tk, V_WIDTH), lambda b, t, qi, kj: (b, kj[t], 0))]
    args = [q, k, v]
    if cfg.decay:
        dcol, drow = extras
        in_specs += [pl.BlockSpec((1, tq, N_HEADS * LANES), lambda b, t, qi, kj: (b, qi[t], 0)),
                     pl.BlockSpec((1, SUBLANES, tk), lambda b, t, qi, kj: (b, 0, kj[t]))]
        args += [dcol, drow]
    if cfg.select:
        (sel,) = extras
        in_specs += [pl.BlockSpec((1, tq, N_HEADS * LANES), lambda b, t, qi, kj: (b, qi[t], 0))]
        args += [sel]
    if cfg.diff:
        for c in extras:
            in_specs += [pl.BlockSpec(c.shape, lambda b, t, qi, kj: (0, 0))]
        args += list(extras)

    grid_spec = pltpu.PrefetchScalarGridSpec(
        num_scalar_prefetch=2,
        grid=(bsz, len(pairs)),
        in_specs=in_specs,
        out_specs=pl.BlockSpec((1, tq, GROUP_WIDTH), lambda b, t, qi, kj: (b, qi[t], 0)),
        scratch_shapes=[pltpu.VMEM((n_vh, tq, MXU_WIDTH), jnp.bfloat16),
                        pltpu.VMEM((n_vh, tq, LANES), jnp.float32),
                        pltpu.VMEM((cfg.n_acc, tq, GROUP_WIDTH), jnp.float32)])
    return pl.pallas_call(
        functools.partial(_attn_body, cfg),
        grid_spec=grid_spec,
        out_shape=jax.ShapeDtypeStruct((bsz, seq, GROUP_WIDTH), jnp.bfloat16),
        compiler_params=pltpu.CompilerParams(dimension_semantics=("arbitrary", "arbitrary"),
                                             vmem_limit_bytes=VMEM_LIMIT_BYTES),
        name=cfg.name,
    )(qi, kj, *args)


def _ffn_body(x_ref, oa_ref, ob_ref, oc_ref, od_ref, oe_ref, wo_ref, fnorm_ref, wg_ref, wu_ref, cw_ref, cb_ref, wd_ref,
              out_ref, xnew_s, xn_s, tail_s, acc_s):
    i = pl.program_id(1)
    f = pl.program_id(2)
    tm = x_ref.shape[1]

    @pl.when(f == 0)
    def _():
        @pl.when(i == 0)
        def _():
            tail_s[...] = jnp.zeros_like(tail_s)

        @pl.when(i > 0)
        def _():
            tail_s[...] = xn_s[tm - TAIL_ROWS:tm, :]

        xnew = x_ref[0]
        for g, o_ref in enumerate((oa_ref, ob_ref, oc_ref, od_ref, oe_ref)):
            xnew = xnew + _dot(o_ref[0], wo_ref[g])
        xnew_s[...] = xnew
        xn = xnew * lax.rsqrt(jnp.mean(xnew * xnew, axis=-1, keepdims=True) + EPS) * fnorm_ref[...]
        xn_s[...] = xn.astype(xn_s.dtype)
        acc_s[...] = jnp.zeros_like(acc_s)

    wg = wg_ref[...]
    g0 = _dot(xn_s[...], wg)
    gt = _dot(tail_s[...], wg)
    u = _dot(xn_s[...], wu_ref[...])
    t1 = gt[TAIL_ROWS - 1:TAIL_ROWS, :]
    t2 = gt[TAIL_ROWS - 2:TAIL_ROWS - 1, :]
    row = lax.broadcasted_iota(jnp.int32, g0.shape, 0)
    g1 = jnp.where(row == 0, t1, pltpu.roll(g0, 1, 0))
    g2 = jnp.where(row == 0, t2, jnp.where(row == 1, t1, pltpu.roll(g0, 2, 0)))
    y = cb_ref[...] + cw_ref[0:1, :] * g2
    y = y + cw_ref[1:2, :] * g1
    y = y + cw_ref[2:3, :] * g0
    hmid = (y * (1.0 / (1.0 + jnp.exp(-y)))) * u
    acc_s[...] += _dot(hmid.astype(jnp.bfloat16), wd_ref[...])

    @pl.when(f == pl.num_programs(2) - 1)
    def _():
        out_ref[0] = xnew_s[...] + acc_s[...]


def _ffn_call(x, outs, wo, fnorm, wg, wu, cw, cb, wd, tm, tf):
    bsz, seq, d = x.shape
    dff = wg.shape[1]
    assert seq % tm == 0 and dff % tf == 0
    tok = lambda w: pl.BlockSpec((1, tm, w), lambda b, i, f: (b, i, 0))
    return pl.pallas_call(
        _ffn_body,
        grid=(bsz, seq // tm, dff // tf),
        in_specs=[tok(d)] + [tok(GROUP_WIDTH)] * 5 + [
            pl.BlockSpec(wo.shape, lambda b, i, f: (0, 0, 0)),
            pl.BlockSpec(fnorm.shape, lambda b, i, f: (0, 0)),
            pl.BlockSpec((d, tf), lambda b, i, f: (0, f)),
            pl.BlockSpec((d, tf), lambda b, i, f: (0, f)),
            pl.BlockSpec((SUBLANES, tf), lambda b, i, f: (0, f)),
            pl.BlockSpec((1, tf), lambda b, i, f: (0, f)),
            pl.BlockSpec((tf, d), lambda b, i, f: (f, 0))],
        out_specs=tok(d),
        out_shape=jax.ShapeDtypeStruct((bsz, seq, d), jnp.float32),
        scratch_shapes=[pltpu.VMEM((tm, d), jnp.float32), pltpu.VMEM((tm, d), jnp.bfloat16),
                        pltpu.VMEM((TAIL_ROWS, d), jnp.bfloat16), pltpu.VMEM((tm, d), jnp.float32)],
        compiler_params=pltpu.CompilerParams(dimension_semantics=("arbitrary", "arbitrary", "arbitrary"),
                                             vmem_limit_bytes=VMEM_LIMIT_BYTES),
        name="ffn",
    )(x, *outs, wo, fnorm, wg, wu, cw, cb, wd)


def _pad_row(v, width=MXU_WIDTH):
    v = v.astype(jnp.float32).reshape(-1)
    return jnp.pad(v, (0, width - v.shape[0]))


def _pair_gain(g):
    one = jnp.concatenate([g.astype(jnp.float32), g.astype(jnp.float32),
                           jnp.zeros((MXU_WIDTH - 2 * MLA_QK,), jnp.float32)])
    return one


def _rope_table(positions):
    pos = positions.astype(jnp.float32)[:, :, None]
    cols_c, cols_s = [], []
    for rot in (MLA_ROPE, ROT_MOBA, ROT_DIFF):
        inv = ROPE_THETA ** (-jnp.arange(0, rot, 2, dtype=jnp.float32) / rot)
        ang = pos * inv
        cols_c.append(jnp.cos(ang))
        cols_s.append(jnp.sin(ang))
    c = jnp.concatenate(cols_c, axis=-1)
    s = jnp.concatenate(cols_s, axis=-1)
    c_hi = c.astype(jnp.bfloat16)
    c_lo = (c - c_hi.astype(jnp.float32)).astype(jnp.bfloat16)
    s_hi = s.astype(jnp.bfloat16)
    s_lo = (s - s_hi.astype(jnp.float32)).astype(jnp.bfloat16)
    one = jnp.ones(pos.shape[:2] + (1,), jnp.bfloat16)
    pad = jnp.zeros(pos.shape[:2] + (TR_WIDTH - TR_ONE - 1,), jnp.bfloat16)
    return jnp.concatenate([c_hi, c_lo, s_hi, s_lo, one, pad], axis=-1)


def _pick_tile(n, pref):
    t = pref
    while n % t:
        t //= 2
    return t


def kernel(x, mem, positions, attn_norm, ffn_norm, mem_norm, w_in, mla_cq_norm, mla_ckv_norm, mla_w_uq, mla_w_ukv, mla_q_norm, mla_k_norm, fox_b_f, fox_q_norm, fox_k_norm, moba_q_norm, moba_k_norm, diff_lambda, diff_q_norm, diff_k_norm, diff_sub_norm, mem_w_kv, mem_q_norm, mem_k_norm, w_o, ffn_w_gate, ffn_w_up, ffn_conv_w, ffn_conv_b, ffn_w_down):
    bsz, seq, d = x.shape
    depth = w_in.shape[0]
    dff = ffn_w_gate.shape[2]
    bf = jnp.bfloat16
    f32 = jnp.float32

    in_idx = _np_in_index()
    uq_idx = _np_uq_index()
    ukvk_idx, ukvv_idx = _np_ukv_index()
    gpair = jnp.asarray(_np_group_matrix(_PAIR_GROUPS), bf)
    g64 = jnp.asarray(_np_group_matrix(_G64_GROUPS), bf)
    g32 = jnp.asarray(_np_group_matrix(_G32_GROUPS), bf)
    expand = jnp.asarray(_np_rope_expand_all(), bf)
    tril = jnp.asarray(np.tril(np.ones((MOBA_BLOCK, MOBA_BLOCK), np.float32)), bf)
    rep_np = np.zeros((LANES, N_HEADS * LANES), np.float32)
    for h in range(N_HEADS):
        rep_np[h, h * LANES:(h + 1) * LANES] = 1.0
    rep = jnp.asarray(rep_np, bf)
    tr = _rope_table(positions)

    t_dense = _pick_tile(seq, 512)
    t_ffn = _pick_tile(seq, 512)
    tf = dff // 2 if (dff // 2) % LANES == 0 else dff

    for l in range(depth):
        win = jnp.concatenate([w_in[l], jnp.zeros((d, 1), f32)], axis=1)[:, in_idx].astype(bf)
        wuq = jnp.concatenate([mla_w_uq[l], jnp.zeros((MLA_Q_RANK, 1), f32)], axis=1)[:, uq_idx]
        wuq = jnp.pad(wuq, ((0, MXU_WIDTH - MLA_Q_RANK), (0, 0))).astype(bf)
        wukv = jnp.concatenate([mla_w_ukv[l], jnp.zeros((MLA_KV_RANK, 1), f32)], axis=1)
        wukvk = wukv[:, ukvk_idx].astype(bf)
        wukvv = wukv[:, ukvv_idx].astype(bf)
        rows = [None] * P_ROWS
        rows[P_CQ] = _pad_row(mla_cq_norm[l])
        rows[P_CKV] = _pad_row(mla_ckv_norm[l])
        rows[P_GQ] = _pair_gain(mla_q_norm[l]) * (LOG2E * MLA_QK ** -0.5)
        rows[P_GK] = _pair_gain(mla_k_norm[l])
        rows[P_FQ] = jnp.tile(fox_q_norm[l].astype(f32), N_HEADS) * (LOG2E * HEAD_DIM ** -0.5)
        rows[P_FK] = jnp.tile(fox_k_norm[l].astype(f32), N_HEADS)
        rows[P_FB] = _pad_row(fox_b_f[l])
        rows[P_MQ] = jnp.tile(moba_q_norm[l].astype(f32), N_HEADS)
        rows[P_MK] = jnp.tile(moba_k_norm[l].astype(f32), N_HEADS)
        rows[P_DQ] = jnp.tile(diff_q_norm[l].astype(f32), 2 * N_HEADS) * (LOG2E * DIFF_QK ** -0.5)
        rows[P_DK] = jnp.tile(diff_k_norm[l].astype(f32), 2 * N_HEADS)
        rows[P_EQ] = jnp.tile(mem_q_norm[l].astype(f32), N_HEADS) * (LOG2E * HEAD_DIM ** -0.5)
        par = jnp.stack([r if r is not None else jnp.zeros((MXU_WIDTH,), f32) for r in rows])

        (aq, ak, av, fq, fk, fv, fdcol, fdrow, mq, mk, mv, msel, dq, dk, dv, eq) = _prep_call(
            x, tr, attn_norm[l].reshape(1, d).astype(f32), win, wuq, wukvk, wukvv, gpair, g64, g32, expand, tril,
            rep, par)
        ek, ev = _memkv_call(mem, mem_norm[l].reshape(1, d).astype(f32), mem_w_kv[l].astype(bf), g64,
                             jnp.tile(mem_k_norm[l].astype(f32), N_HEADS).reshape(1, GROUP_WIDTH))

        lam_vec = diff_lambda[l].astype(f32)
        lam_init = 0.8 - 0.6 * math.exp(-0.3 * l)
        lam = (jnp.exp(jnp.sum(lam_vec[0] * lam_vec[1])) - jnp.exp(jnp.sum(lam_vec[2] * lam_vec[3])) + lam_init)
        lam_row = jnp.full((1, GROUP_WIDTH), 1.0, f32) * lam
        gsub = (jnp.tile(diff_sub_norm[l].astype(f32), N_HEADS) * (1.0 - lam_init)).reshape(1, GROUP_WIDTH)

        o_a = _attn_call(_CFG_MLA, aq, ak, av, (), t_dense, t_dense)
        o_b = _attn_call(_CFG_FOX, fq, fk, fv, (fdcol, fdrow), t_dense, t_dense)
        o_c = _attn_call(_CFG_MOBA, mq, mk, mv, (msel,), MOBA_BLOCK, MOBA_BLOCK)
        o_d = _attn_call(_CFG_DIFF, dq, dk, dv, (g64, gsub, lam_row), t_dense, t_dense)
        o_e = _attn_call(_CFG_MEM, eq, ek, ev, (), t_dense, mem.shape[1])

        cw = jnp.pad(ffn_conv_w[l].astype(f32), ((0, SUBLANES - CONV_WIDTH), (0, 0)))
        x = _ffn_call(x, (o_a, o_b, o_c, o_d, o_e), w_o[l].reshape(5, GROUP_WIDTH, d).astype(bf),
                      ffn_norm[l].reshape(1, d).astype(f32), ffn_w_gate[l].astype(bf), ffn_w_up[l].astype(bf),
                      cw, ffn_conv_b[l].reshape(1, dff).astype(f32), ffn_w_down[l].astype(bf), t_ffn, tf)
    return x
```

```python
import functools
import math

import numpy as np
import jax
import jax.numpy as jnp
from jax import lax
from jax.experimental import pallas as pl
from jax.experimental.pallas import tpu as pltpu

N_HEADS = 4
HEAD_DIM = 64
GROUP_WIDTH = N_HEADS * HEAD_DIM
MLA_Q_RANK = 192
MLA_KV_RANK = 128
MLA_NOPE = 64
MLA_ROPE = 32
MLA_QK = MLA_NOPE + MLA_ROPE
DIFF_QK = HEAD_DIM // 2
ROPE_THETA = 500000.0
ROT_MOBA = HEAD_DIM // 4
ROT_DIFF = DIFF_QK // 4
MOBA_BLOCK = 256
MOBA_TOPK = 3
CONV_WIDTH = 3
EPS = 1e-6
NEG_INF = -1e30
LOG2E = math.log2(math.e)
REMOVED = -3e38

LANES = 128
SUBLANES = 8
MXU_WIDTH = 256
TAIL_ROWS = 16
VMEM_LIMIT_BYTES = 56 * 1024 * 1024

_SRC_CQ = 0
_SRC_CKV = _SRC_CQ + MLA_Q_RANK
_SRC_KR = _SRC_CKV + MLA_KV_RANK
_SRC_FOX = _SRC_KR + MLA_ROPE
_SRC_FOXF = _SRC_FOX + 3 * GROUP_WIDTH
_SRC_MOBA = _SRC_FOXF + N_HEADS
_SRC_DIFF = _SRC_MOBA + 3 * GROUP_WIDTH
_SRC_MEMQ = _SRC_DIFF + 3 * GROUP_WIDTH
_SRC_END = _SRC_MEMQ + GROUP_WIDTH

PK_CQ = 0
PK_CKV = 256
PK_KR = 384
PK_FQ, PK_FK, PK_FV = 896, 1152, 1408
PK_FF = 1664
PK_MQ, PK_MK, PK_MV = 1792, 2048, 2304
PK_DQ, PK_DK, PK_DV = 2560, 2816, 3072
PK_EQ = 3328
PK_END = 3584

PAIR_STRIDE = MLA_QK


def _pair_lane(h, d):
    return (h // 2) * MXU_WIDTH + (h % 2) * PAIR_STRIDE + d


N_FREQ = MLA_ROPE // 2 + ROT_MOBA // 2 + ROT_DIFF // 2
FREQ_BASE_MLA = 0
FREQ_BASE_MOBA = MLA_ROPE // 2
FREQ_BASE_DIFF = FREQ_BASE_MOBA + ROT_MOBA // 2
TR_ONE = 4 * N_FREQ
TR_WIDTH = LANES

(P_CQ, P_CKV, P_GQ, P_GK, P_FQ, P_FK, P_FB, P_MQ, P_MK, P_DQ, P_DK, P_EQ) = range(12)
P_ROWS = 16


def _np_in_index():
    idx = np.full((PK_END,), _SRC_END, np.int32)
    idx[PK_CQ:PK_CQ + MLA_Q_RANK] = np.arange(_SRC_CQ, _SRC_CQ + MLA_Q_RANK)
    idx[PK_CKV:PK_CKV + MLA_KV_RANK] = np.arange(_SRC_CKV, _SRC_CKV + MLA_KV_RANK)
    for h in range(N_HEADS):
        for d in range(MLA_ROPE):
            idx[PK_KR + _pair_lane(h, d)] = _SRC_KR + d
    idx[PK_FQ:PK_FQ + 3 * GROUP_WIDTH] = np.arange(_SRC_FOX, _SRC_FOX + 3 * GROUP_WIDTH)
    idx[PK_FF:PK_FF + N_HEADS] = np.arange(_SRC_FOXF, _SRC_FOXF + N_HEADS)
    idx[PK_MQ:PK_MQ + 3 * GROUP_WIDTH] = np.arange(_SRC_MOBA, _SRC_MOBA + 3 * GROUP_WIDTH)
    idx[PK_DQ:PK_DQ + 3 * GROUP_WIDTH] = np.arange(_SRC_DIFF, _SRC_DIFF + 3 * GROUP_WIDTH)
    idx[PK_EQ:PK_EQ + GROUP_WIDTH] = np.arange(_SRC_MEMQ, _SRC_MEMQ + GROUP_WIDTH)
    return idx


def _np_uq_index():
    idx = np.full((2 * MXU_WIDTH,), N_HEADS * MLA_QK, np.int32)
    for h in range(N_HEADS):
        for d in range(MLA_QK):
            idx[_pair_lane(h, d)] = h * MLA_QK + d
    return idx


def _np_ukv_index():
    zero = N_HEADS * (MLA_NOPE + HEAD_DIM)
    idx_k = np.full((2 * MXU_WIDTH,), zero, np.int32)
    idx_v = np.zeros((GROUP_WIDTH,), np.int32)
    for h in range(N_HEADS):
        for d in range(MLA_NOPE):
            idx_k[_pair_lane(h, MLA_ROPE + d)] = h * (MLA_NOPE + HEAD_DIM) + d
        for d in range(HEAD_DIM):
            idx_v[h * HEAD_DIM + d] = h * (MLA_NOPE + HEAD_DIM) + MLA_NOPE + d
    return idx_k, idx_v


def _np_group_matrix(groups):
    g = np.zeros((MXU_WIDTH, MXU_WIDTH), np.float32)
    for lo, size in groups:
        g[lo:lo + size, lo:lo + size] = 1.0 / size
    return g


_PAIR_GROUPS = [(0, MLA_ROPE), (MLA_ROPE, MLA_NOPE), (PAIR_STRIDE, MLA_ROPE), (PAIR_STRIDE + MLA_ROPE, MLA_NOPE)]
_G64_GROUPS = [(h * HEAD_DIM, HEAD_DIM) for h in range(N_HEADS)]
_G32_GROUPS = [(g * DIFF_QK, DIFF_QK) for g in range(2 * N_HEADS)]


def _np_rope_expand(regions, rot, base):
    half = rot // 2
    e = np.zeros((TR_WIDTH, 3 * MXU_WIDTH), np.float32)
    e[TR_ONE, 0:MXU_WIDTH] = 1.0
    for lo in regions:
        for r in range(half):
            f = base + r
            for lane, tab, sign in ((lo + r, 2, -1.0), (lo + half + r, 1, 1.0)):
                e[TR_ONE, lane] = 0.0
                e[f, lane] = 1.0
                e[N_FREQ + f, lane] = 1.0
                e[2 * N_FREQ + f, tab * MXU_WIDTH + lane] = sign
                e[3 * N_FREQ + f, tab * MXU_WIDTH + lane] = sign
    return e


def _np_rope_expand_all():
    return np.concatenate([
        _np_rope_expand([0, PAIR_STRIDE], MLA_ROPE, FREQ_BASE_MLA),
        _np_rope_expand([h * HEAD_DIM for h in range(N_HEADS)], ROT_MOBA, FREQ_BASE_MOBA),
        _np_rope_expand([g * DIFF_QK for g in range(2 * N_HEADS)], ROT_DIFF, FREQ_BASE_DIFF),
    ], axis=1)


def _dot(a, b):
    return jnp.dot(a, b, preferred_element_type=jnp.float32)


def _dot_nt(a, b):
    return lax.dot_general(a, b, (((1,), (1,)), ((), ())), preferred_element_type=jnp.float32)


def _split2(a):
    hi = a.astype(jnp.bfloat16)
    lo = (a - hi.astype(jnp.float32)).astype(jnp.bfloat16)
    return hi, lo


def _split3(a):
    hi = a.astype(jnp.bfloat16)
    r = a - hi.astype(jnp.float32)
    mid = r.astype(jnp.bfloat16)
    lo = (r - mid.astype(jnp.float32)).astype(jnp.bfloat16)
    return hi, mid, lo


def _group_mean_sq(a, g_bf16):
    return _dot((a * a).astype(jnp.bfloat16), g_bf16)


def _rope(x, tabs, half):
    w = x.shape[-1]
    c, s1, s2 = tabs[:, 0:w], tabs[:, w:2 * w], tabs[:, 2 * w:3 * w]
    return x * c + pltpu.roll(x, half, 1) * s1 + pltpu.roll(x, w - half, 1) * s2


def _lane_mask(shape, lo, hi):
    lane = lax.broadcasted_iota(jnp.int32, shape, len(shape) - 1)
    return (lane >= lo) & (lane < hi)


def _prep_body(x_ref, tr_ref, anorm_ref, win_ref, wuq_ref, wukvk_ref, wukvv_ref, gpair_ref, g64_ref, g32_ref,
               exp_ref, tril_ref, rep_ref, par_ref,
               aq_ref, ak_ref, av_ref, fq_ref, fk_ref, fv_ref, fdcol_ref, fdrow_ref, mq_ref, mk_ref, mv_ref, msel_ref,
               dq_ref, dk_ref, dv_ref, eq_ref,
               kmean_s, carry_s):
    j = pl.program_id(1)
    tm = x_ref.shape[1]
    bf = jnp.bfloat16

    @pl.when(j == 0)
    def _():
        kmean_s[...] = jnp.zeros_like(kmean_s)
        carry_s[...] = jnp.zeros_like(carry_s)

    x = x_ref[0]
    xn = x * lax.rsqrt(jnp.mean(x * x, axis=-1, keepdims=True) + EPS) * anorm_ref[...]
    xb = xn.astype(bf)

    def proj(off, width):
        return _dot(xb, win_ref[:, off:off + width])

    def prow(r, width=MXU_WIDTH):
        return par_ref[r:r + 1, 0:width]

    tabs = _dot(tr_ref[0], exp_ref[...])
    tab_mla = tabs[:, 0:3 * MXU_WIDTH]
    tab_moba = tabs[:, 3 * MXU_WIDTH:6 * MXU_WIDTH]
    tab_diff = tabs[:, 6 * MXU_WIDTH:9 * MXU_WIDTH]
    gpair, g64, g32 = gpair_ref[...], g64_ref[...], g32_ref[...]

    def group_norm(a, g, gain_row):
        return a * lax.rsqrt(_group_mean_sq(a, g) + EPS) * prow(gain_row)

    cq = proj(PK_CQ, MXU_WIDTH)
    cqn = cq * lax.rsqrt(jnp.sum(cq * cq, axis=-1, keepdims=True) * (1.0 / MLA_Q_RANK) + EPS) * prow(P_CQ)
    qa = _dot(cqn.astype(bf), wuq_ref[...])
    ckv = proj(PK_CKV, MLA_KV_RANK)
    ckvn = ckv * lax.rsqrt(jnp.mean(ckv * ckv, axis=-1, keepdims=True) + EPS) * prow(P_CKV, MLA_KV_RANK)
    ckvb = ckvn.astype(bf)
    ka = proj(PK_KR, 2 * MXU_WIDTH) + _dot(ckvb, wukvk_ref[...])
    for p in range(2):
        sl = slice(p * MXU_WIDTH, (p + 1) * MXU_WIDTH)
        aq_ref[0, :, sl] = _rope(group_norm(qa[:, sl], gpair, P_GQ), tab_mla, MLA_ROPE // 2).astype(bf)
        ak_ref[0, :, sl] = _rope(group_norm(ka[:, sl], gpair, P_GK), tab_mla, MLA_ROPE // 2).astype(bf)
    av_ref[0] = _values_even_odd(_dot(ckvb, wukvv_ref[...]))

    fq_ref[0] = group_norm(proj(PK_FQ, MXU_WIDTH), g64, P_FQ).astype(bf)
    fk_ref[0] = group_norm(proj(PK_FK, MXU_WIDTH), g64, P_FK).astype(bf)
    fv_ref[0] = _values_even_odd(proj(PK_FV, MXU_WIDTH))
    z = proj(PK_FF, LANES) + prow(P_FB, LANES)
    log_f = jnp.minimum(z, 0.0) - jnp.log1p(jnp.exp(-jnp.abs(z)))
    log_f = jnp.where(_lane_mask(log_f.shape, 0, N_HEADS), log_f, 0.0)
    tril = tril_ref[...]
    l1, l2, l3 = _split3(log_f)
    dec = carry_s[...] + ((_dot(tril, l1) + _dot(tril, l2)) + _dot(tril, l3))
    carry_s[...] = dec[tm - 1:tm, :]
    d1, d2, d3 = _split3(dec * LOG2E)
    rep = rep_ref[...]
    fdcol_ref[0] = (_dot(d1, rep) + _dot(d2, rep)) + _dot(d3, rep)
    row_sel = jnp.where(lax.broadcasted_iota(jnp.int32, (SUBLANES, LANES), 0)
                        == lax.broadcasted_iota(jnp.int32, (SUBLANES, LANES), 1), 1.0, 0.0).astype(bf)
    fdrow_ref[0] = (_dot_nt(row_sel, d1) + _dot_nt(row_sel, d2)) + _dot_nt(row_sel, d3)

    mq = _rope(group_norm(proj(PK_MQ, MXU_WIDTH), g64, P_MQ), tab_moba, ROT_MOBA // 2)
    mk = _rope(group_norm(proj(PK_MK, MXU_WIDTH), g64, P_MK), tab_moba, ROT_MOBA // 2)
    mq_ref[0] = (mq * (LOG2E * HEAD_DIM ** -0.5)).astype(bf)
    mk_ref[0] = mk.astype(bf)
    mv_ref[0] = _values_even_odd(proj(PK_MV, MXU_WIDTH))
    kmean_s[pl.ds(j, 1), :] = jnp.mean(mk, axis=0, keepdims=True)
    km_hi, km_lo = _split2(kmean_s[...])
    lane = lax.broadcasted_iota(jnp.int32, (tm, LANES), 1)
    past = lane < j
    for h in range(N_HEADS):
        q_hi, q_lo = _split2(jnp.where(_lane_mask(mq.shape, h * HEAD_DIM, (h + 1) * HEAD_DIM), mq, 0.0))
        gate = (_dot_nt(q_hi, km_hi) + _dot_nt(q_hi, km_lo)) + _dot_nt(q_lo, km_hi)
        work = jnp.where(past, gate, NEG_INF)
        sel = jnp.zeros((tm, LANES), jnp.bool_)
        for _ in range(MOBA_TOPK):
            mx = jnp.max(work, axis=-1, keepdims=True)
            first = jnp.min(jnp.where(work == mx, lane, LANES), axis=-1, keepdims=True)
            pick = lane == first
            sel = sel | pick
            work = jnp.where(pick, REMOVED, work)
        msel_ref[0, :, h * LANES:(h + 1) * LANES] = jnp.where(sel & past, 0.0, NEG_INF)

    dq_ref[0] = _rope(group_norm(proj(PK_DQ, MXU_WIDTH), g32, P_DQ), tab_diff, ROT_DIFF // 2).astype(bf)
    dk_ref[0] = _rope(group_norm(proj(PK_DK, MXU_WIDTH), g32, P_DK), tab_diff, ROT_DIFF // 2).astype(bf)
    dv_ref[0] = _values_even_odd(proj(PK_DV, MXU_WIDTH))

    eq_ref[0] = group_norm(proj(PK_EQ, MXU_WIDTH), g64, P_EQ).astype(bf)


def _full_spec(shape):
    n = len(shape)
    return pl.BlockSpec(shape, lambda *_: (0,) * n)


def _prep_call(x, tr, anorm, win, wuq, wukvk, wukvv, gpair, g64, g32, expand, tril, rep, par):
    bsz, seq, d = x.shape
    tm = MOBA_BLOCK
    assert seq % tm == 0 and seq // tm <= LANES
    bf = jnp.bfloat16

    def tok(width):
        return pl.BlockSpec((1, tm, width), lambda b, j: (b, j, 0))

    f32 = jnp.float32
    widths = [(2 * MXU_WIDTH, bf), (2 * MXU_WIDTH, bf), (V_WIDTH, bf),
              (GROUP_WIDTH, bf), (GROUP_WIDTH, bf), (V_WIDTH, bf),
              (N_HEADS * LANES, f32), (None, f32),
              (GROUP_WIDTH, bf), (GROUP_WIDTH, bf), (V_WIDTH, bf), (N_HEADS * LANES, f32),
              (GROUP_WIDTH, bf), (GROUP_WIDTH, bf), (V_WIDTH, bf),
              (GROUP_WIDTH, bf)]
    consts = [anorm, win, wuq, wukvk, wukvv, gpair, g64, g32, expand, tril, rep, par]
    return pl.pallas_call(
        _prep_body,
        grid=(bsz, seq // tm),
        in_specs=[tok(d), tok(TR_WIDTH)] + [_full_spec(c.shape) for c in consts],
        out_specs=[tok(w) if w else pl.BlockSpec((1, SUBLANES, tm), lambda b, j: (b, 0, j)) for w, _ in widths],
        out_shape=[jax.ShapeDtypeStruct((bsz, seq, w) if w else (bsz, SUBLANES, seq), dt) for w, dt in widths],
        scratch_shapes=[pltpu.VMEM((LANES, GROUP_WIDTH), jnp.float32), pltpu.VMEM((1, LANES), jnp.float32)],
        compiler_params=pltpu.CompilerParams(dimension_semantics=("arbitrary", "arbitrary"),
                                             vmem_limit_bytes=VMEM_LIMIT_BYTES),
        name="prep",
    )(x, tr, *consts)


def _memkv_body(mem_ref, mnorm_ref, w_ref, g64_ref, gain_ref, k_ref, v_ref):
    m = mem_ref[0]
    mn = m * lax.rsqrt(jnp.mean(m * m, axis=-1, keepdims=True) + EPS) * mnorm_ref[...]
    kv = _dot(mn.astype(jnp.bfloat16), w_ref[...])
    k = kv[:, 0:GROUP_WIDTH]
    k = k * lax.rsqrt(_group_mean_sq(k, g64_ref[...]) + EPS) * gain_ref[...]
    k_ref[0] = k.astype(jnp.bfloat16)
    v_ref[0] = _values_even_odd(kv[:, GROUP_WIDTH:2 * GROUP_WIDTH])


def _memkv_call(mem, mnorm, w, g64, gain):
    bsz, mlen, d = mem.shape
    return pl.pallas_call(
        _memkv_body,
        grid=(bsz,),
        in_specs=[pl.BlockSpec((1, mlen, d), lambda b: (b, 0, 0)), _full_spec(mnorm.shape), _full_spec(w.shape),
                  _full_spec(g64.shape), _full_spec(gain.shape)],
        out_specs=[pl.BlockSpec((1, mlen, w), lambda b: (b, 0, 0)) for w in (GROUP_WIDTH, V_WIDTH)],
        out_shape=[jax.ShapeDtypeStruct((bsz, mlen, w), jnp.bfloat16) for w in (GROUP_WIDTH, V_WIDTH)],
        compiler_params=pltpu.CompilerParams(dimension_semantics=("arbitrary",), vmem_limit_bytes=VMEM_LIMIT_BYTES),
        name="mem_kv",
    )(mem, mnorm, w, g64, gain)


class _AttnCfg:
    def __init__(self, name, vheads, n_maps, causal, decay=False, select=False, diff=False):
        self.name = name
        self.vheads = vheads
        self.n_maps = n_maps
        self.n_acc = 2 * n_maps
        self.causal = causal
        self.decay = decay
        self.select = select
        self.diff = diff


_PLAIN_VHEADS = [(0, h * HEAD_DIM, (h + 1) * HEAD_DIM, 0, h) for h in range(N_HEADS)]
_CFG_MLA = _AttnCfg("attn_mla", [((h // 2) * MXU_WIDTH, (h % 2) * PAIR_STRIDE, (h % 2) * PAIR_STRIDE + MLA_QK, 0, h)
                                 for h in range(N_HEADS)], 1, True)
_CFG_FOX = _AttnCfg("attn_fox", _PLAIN_VHEADS, 1, True, decay=True)
_CFG_MOBA = _AttnCfg("attn_moba", _PLAIN_VHEADS, 1, True, select=True)
_CFG_DIFF = _AttnCfg("attn_diff", [(0, h * HEAD_DIM + c * DIFF_QK, h * HEAD_DIM + (c + 1) * DIFF_QK, c, h)
                                   for c in range(2) for h in range(N_HEADS)], 2, True, diff=True)
_CFG_MEM = _AttnCfg("attn_mem", _PLAIN_VHEADS, 1, False)


_ONES_LANE = (HEAD_DIM, 0, 3 * HEAD_DIM, 2 * HEAD_DIM)
V_WIDTH = 2 * GROUP_WIDTH


def _values_even_odd(v):
    lane = lax.broadcasted_iota(jnp.int32, v.shape, 1)
    even_head = (lane & HEAD_DIM) == 0
    unit = jnp.where((lane & (HEAD_DIM - 1)) == 0, 1.0, 0.0)
    return jnp.concatenate([jnp.where(even_head, v, unit), jnp.where(even_head, unit, v)],
                           axis=1).astype(jnp.bfloat16)


def _head_lane_select(vals, shape):
    lane = lax.broadcasted_iota(jnp.int32, shape, 1)
    out = jnp.broadcast_to(vals[N_HEADS - 1], shape)
    for h in range(N_HEADS - 2, -1, -1):
        out = jnp.where(lane < (h + 1) * HEAD_DIM, vals[h], out)
    return out


def _tile_lanes(x, width):
    return jnp.tile(x, (1, width // LANES)) if width != LANES else x


def _attn_body(cfg, qi_ref, kj_ref, *refs):
    refs = list(refs)
    q_ref, k_ref, v_ref = refs[:3]
    pos = 3
    if cfg.decay:
        dcol_ref, drow_ref = refs[pos:pos + 2]
        pos += 2
    if cfg.select:
        sel_ref = refs[pos]
        pos += 1
    if cfg.diff:
        g64_ref, gsub_ref, lam_ref = refs[pos:pos + 3]
        pos += 3
    o_ref, qm_s, m_s, acc_s = refs[pos:pos + 4]

    t = pl.program_id(1)
    i = qi_ref[t]
    j = kj_ref[t]
    tq = q_ref.shape[1]
    tk = k_ref.shape[1]

    @pl.when(j == 0)
    def _():
        for n, (off, lo, hi, _, _) in enumerate(cfg.vheads):
            qb = q_ref[0, :, off:off + MXU_WIDTH]
            qm_s[n] = jnp.where(_lane_mask(qb.shape, lo, hi), qb, jnp.zeros_like(qb))
        m_s[...] = jnp.full(m_s.shape, NEG_INF, jnp.float32)
        acc_s[...] = jnp.zeros_like(acc_s)

    def step(diag):
        pv = [[None, None] for _ in range(cfg.n_acc)]
        alphas = [[None, None] for _ in range(cfg.n_acc)]
        if diag:
            row = lax.broadcasted_iota(jnp.int32, (tq, tk), 0)
            col = lax.broadcasted_iota(jnp.int32, (tq, tk), 1)
            keep = row >= col
        for n, (off, lo, hi, c, h) in enumerate(cfg.vheads):
            s = _dot_nt(qm_s[n], k_ref[0, :, off:off + MXU_WIDTH])
            if cfg.decay:
                s = s + (_tile_lanes(dcol_ref[0, :, h * LANES:(h + 1) * LANES], tk) - drow_ref[0, h:h + 1, :])
            if cfg.select:
                lane = lax.broadcasted_iota(jnp.int32, (tq, LANES), 1)
                qrow = lax.broadcasted_iota(jnp.int32, (tq, 1), 0)
                sel_h = sel_ref[0, :, h * LANES:(h + 1) * LANES]
                parts = []
                for kb in range(tk // MOBA_BLOCK):
                    if diag and kb == tk // MOBA_BLOCK - 1:
                        parts.append(jnp.zeros((tq, MOBA_BLOCK), jnp.float32))
                        continue
                    blk = j * (tk // MOBA_BLOCK) + kb
                    bias = jnp.sum(jnp.where(lane == blk, sel_h, 0.0), axis=-1, keepdims=True)
                    if diag:
                        bias = jnp.where(qrow < (kb + 1) * MOBA_BLOCK, 0.0, bias)
                    parts.append(jnp.broadcast_to(bias, (tq, MOBA_BLOCK)))
                s = s + (parts[0] if len(parts) == 1 else jnp.concatenate(parts, axis=1))
            if diag:
                s = jnp.where(keep, s, NEG_INF)
            m_prev = m_s[n]
            m_new = jnp.maximum(m_prev, jnp.max(s, axis=-1, keepdims=True))
            alpha = jnp.exp2(m_prev - m_new)
            p = jnp.exp2(s - _tile_lanes(m_new, tk))
            m_s[n] = m_new
            a = 2 * c + h % 2
            vcol = (h % 2) * GROUP_WIDTH + (h // 2) * LANES
            pv[a][h // 2] = _dot(p.astype(jnp.bfloat16), v_ref[0, :, vcol:vcol + LANES])
            alphas[a][h // 2] = alpha
        for a in range(cfg.n_acc):
            acc_s[a] = acc_s[a] * jnp.concatenate(alphas[a], axis=1) + jnp.concatenate(pv[a], axis=1)

    if cfg.causal:
        pl.when(j < i)(functools.partial(step, False))
        pl.when(j == i)(functools.partial(step, True))
        last = j == i
    else:
        step(False)
        last = j == 0

    @pl.when(last)
    def _():
        lane = lax.broadcasted_iota(jnp.int32, (tq, GROUP_WIDTH), 1)
        even_head = (lane & HEAD_DIM) == 0
        outs = []
        for c in range(cfg.n_maps):
            accs = (acc_s[2 * c], acc_s[2 * c + 1])
            ls = [jnp.sum(jnp.where(lane == _ONES_LANE[h], accs[h % 2], 0.0), axis=-1, keepdims=True)
                  for h in range(N_HEADS)]
            outs.append(jnp.where(even_head, accs[0], accs[1]) / _head_lane_select(ls, (tq, GROUP_WIDTH)))
        if cfg.diff:
            o = outs[0] - lam_ref[0:1, :] * outs[1]
            o = o * lax.rsqrt(_group_mean_sq(o, g64_ref[...]) + EPS) * gsub_ref[...]
        else:
            o = outs[0]
        o_ref[0] = o.astype(o_ref.dtype)


def _attn_call(cfg, q, k, v, extras, tq, tk):
    bsz, seq, wq = q.shape
    sk = k.shape[1]
    nq = seq // tq
    assert seq % tq == 0 and sk % tk == 0
    if cfg.causal:
        assert tq == tk and sk == seq
        pairs = [(i, j) for i in range(nq) for j in range(i + 1)]
    else:
        assert sk == tk
        pairs = [(i, 0) for i in range(nq)]
    qi = jnp.asarray(np.array([p[0] for p in pairs], np.int32))
    kj = jnp.asarray(np.array([p[1] for p in pairs], np.int32))
    n_vh = len(cfg.vheads)

    in_specs = [pl.BlockSpec((1, tq, wq), lambda b, t, qi, kj: (b, qi[t], 0)),
                pl.BlockSpec((1, tk, wq), lambda b, t, qi, kj: (b, kj[t], 0)),
                pl.BlockSpec((1, tk, V_WIDTH), lambda b, t, qi, kj: (b, kj[t], 0))]
    args = [q, k, v]
    if cfg.decay:
        dcol, drow = extras
        in_specs += [pl.BlockSpec((1, tq, N_HEADS * LANES), lambda b, t, qi, kj: (b, qi[t], 0)),
                     pl.BlockSpec((1, SUBLANES, tk), lambda b, t, qi, kj: (b, 0, kj[t]))]
        args += [dcol, drow]
    if cfg.select:
        (sel,) = extras
        in_specs += [pl.BlockSpec((1, tq, N_HEADS * LANES), lambda b, t, qi, kj: (b, qi[t], 0))]
        args += [sel]
    if cfg.diff:
        for c in extras:
            in_specs += [pl.BlockSpec(c.shape, lambda b, t, qi, kj: (0, 0))]
        args += list(extras)

    grid_spec = pltpu.PrefetchScalarGridSpec(
        num_scalar_prefetch=2,
        grid=(bsz, len(pairs)),
        in_specs=in_specs,
        out_specs=pl.BlockSpec((1, tq, GROUP_WIDTH), lambda b, t, qi, kj: (b, qi[t], 0)),
        scratch_shapes=[pltpu.VMEM((n_vh, tq, MXU_WIDTH), jnp.bfloat16),
                        pltpu.VMEM((n_vh, tq, LANES), jnp.float32),
                        pltpu.VMEM((cfg.n_acc, tq, GROUP_WIDTH), jnp.float32)])
    return pl.pallas_call(
        functools.partial(_attn_body, cfg),
        grid_spec=grid_spec,
        out_shape=jax.ShapeDtypeStruct((bsz, seq, GROUP_WIDTH), jnp.bfloat16),
        compiler_params=pltpu.CompilerParams(dimension_semantics=("arbitrary", "arbitrary"),
                                             vmem_limit_bytes=VMEM_LIMIT_BYTES),
        name=cfg.name,
    )(qi, kj, *args)


def _ffn_body(x_ref, oa_ref, ob_ref, oc_ref, od_ref, oe_ref, wo_ref, fnorm_ref, wg_ref, wu_ref, cw_ref, cb_ref, wd_ref,
              out_ref, xnew_s, xn_s, tail_s, acc_s):
    i = pl.program_id(1)
    f = pl.program_id(2)
    tm = x_ref.shape[1]

    @pl.when(f == 0)
    def _():
        @pl.when(i == 0)
        def _():
            tail_s[...] = jnp.zeros_like(tail_s)

        @pl.when(i > 0)
        def _():
            tail_s[...] = xn_s[tm - TAIL_ROWS:tm, :]

        xnew = x_ref[0]
        for g, o_ref in enumerate((oa_ref, ob_ref, oc_ref, od_ref, oe_ref)):
            xnew = xnew + _dot(o_ref[0], wo_ref[g])
        xnew_s[...] = xnew
        xn = xnew * lax.rsqrt(jnp.mean(xnew * xnew, axis=-1, keepdims=True) + EPS) * fnorm_ref[...]
        xn_s[...] = xn.astype(xn_s.dtype)
        acc_s[...] = jnp.zeros_like(acc_s)

    wg = wg_ref[...]
    g0 = _dot(xn_s[...], wg)
    gt = _dot(tail_s[...], wg)
    u = _dot(xn_s[...], wu_ref[...])
    t1 = gt[TAIL_ROWS - 1:TAIL_ROWS, :]
    t2 = gt[TAIL_ROWS - 2:TAIL_ROWS - 1, :]
    row = lax.broadcasted_iota(jnp.int32, g0.shape, 0)
    g1 = jnp.where(row == 0, t1, pltpu.roll(g0, 1, 0))
    g2 = jnp.where(row == 0, t2, jnp.where(row == 1, t1, pltpu.roll(g0, 2, 0)))
    y = cb_ref[...] + cw_ref[0:1, :] * g2
    y = y + cw_ref[1:2, :] * g1
    y = y + cw_ref[2:3, :] * g0
    hmid = (y * (1.0 / (1.0 + jnp.exp(-y)))) * u
    acc_s[...] += _dot(hmid.astype(jnp.bfloat16), wd_ref[...])

    @pl.when(f == pl.num_programs(2) - 1)
    def _():
        out_ref[0] = xnew_s[...] + acc_s[...]


def _ffn_call(x, outs, wo, fnorm, wg, wu, cw, cb, wd, tm, tf):
    bsz, seq, d = x.shape
    dff = wg.shape[1]
    assert seq % tm == 0 and dff % tf == 0
    tok = lambda w: pl.BlockSpec((1, tm, w), lambda b, i, f: (b, i, 0))
    return pl.pallas_call(
        _ffn_body,
        grid=(bsz, seq // tm, dff // tf),
        in_specs=[tok(d)] + [tok(GROUP_WIDTH)] * 5 + [
            pl.BlockSpec(wo.shape, lambda b, i, f: (0, 0, 0)),
            pl.BlockSpec(fnorm.shape, lambda b, i, f: (0, 0)),
            pl.BlockSpec((d, tf), lambda b, i, f: (0, f)),
            pl.BlockSpec((d, tf), lambda b, i, f: (0, f)),
            pl.BlockSpec((SUBLANES, tf), lambda b, i, f: (0, f)),
            pl.BlockSpec((1, tf), lambda b, i, f: (0, f)),
            pl.BlockSpec((tf, d), lambda b, i, f: (f, 0))],
        out_specs=tok(d),
        out_shape=jax.ShapeDtypeStruct((bsz, seq, d), jnp.float32),
        scratch_shapes=[pltpu.VMEM((tm, d), jnp.float32), pltpu.VMEM((tm, d), jnp.bfloat16),
                        pltpu.VMEM((TAIL_ROWS, d), jnp.bfloat16), pltpu.VMEM((tm, d), jnp.float32)],
        compiler_params=pltpu.CompilerParams(dimension_semantics=("arbitrary", "arbitrary", "arbitrary"),
                                             vmem_limit_bytes=VMEM_LIMIT_BYTES),
        name="ffn",
    )(x, *outs, wo, fnorm, wg, wu, cw, cb, wd)


def _pad_row(v, width=MXU_WIDTH):
    v = v.astype(jnp.float32).reshape(-1)
    return jnp.pad(v, (0, width - v.shape[0]))


def _pair_gain(g):
    one = jnp.concatenate([g.astype(jnp.float32), g.astype(jnp.float32),
                           jnp.zeros((MXU_WIDTH - 2 * MLA_QK,), jnp.float32)])
    return one


def _rope_table(positions):
    pos = positions.astype(jnp.float32)[:, :, None]
    cols_c, cols_s = [], []
    for rot in (MLA_ROPE, ROT_MOBA, ROT_DIFF):
        inv = ROPE_THETA ** (-jnp.arange(0, rot, 2, dtype=jnp.float32) / rot)
        ang = pos * inv
        cols_c.append(jnp.cos(ang))
        cols_s.append(jnp.sin(ang))
    c = jnp.concatenate(cols_c, axis=-1)
    s = jnp.concatenate(cols_s, axis=-1)
    c_hi = c.astype(jnp.bfloat16)
    c_lo = (c - c_hi.astype(jnp.float32)).astype(jnp.bfloat16)
    s_hi = s.astype(jnp.bfloat16)
    s_lo = (s - s_hi.astype(jnp.float32)).astype(jnp.bfloat16)
    one = jnp.ones(pos.shape[:2] + (1,), jnp.bfloat16)
    pad = jnp.zeros(pos.shape[:2] + (TR_WIDTH - TR_ONE - 1,), jnp.bfloat16)
    return jnp.concatenate([c_hi, c_lo, s_hi, s_lo, one, pad], axis=-1)


def _pick_tile(n, pref):
    t = pref
    while n % t:
        t //= 2
    return t


def kernel(x, mem, positions, attn_norm, ffn_norm, mem_norm, w_in, mla_cq_norm, mla_ckv_norm, mla_w_uq, mla_w_ukv, mla_q_norm, mla_k_norm, fox_b_f, fox_q_norm, fox_k_norm, moba_q_norm, moba_k_norm, diff_lambda, diff_q_norm, diff_k_norm, diff_sub_norm, mem_w_kv, mem_q_norm, mem_k_norm, w_o, ffn_w_gate, ffn_w_up, ffn_conv_w, ffn_conv_b, ffn_w_down):
    bsz, seq, d = x.shape
    depth = w_in.shape[0]
    dff = ffn_w_gate.shape[2]
    bf = jnp.bfloat16
    f32 = jnp.float32

    in_idx = _np_in_index()
    uq_idx = _np_uq_index()
    ukvk_idx, ukvv_idx = _np_ukv_index()
    gpair = jnp.asarray(_np_group_matrix(_PAIR_GROUPS), bf)
    g64 = jnp.asarray(_np_group_matrix(_G64_GROUPS), bf)
    g32 = jnp.asarray(_np_group_matrix(_G32_GROUPS), bf)
    expand = jnp.asarray(_np_rope_expand_all(), bf)
    tril = jnp.asarray(np.tril(np.ones((MOBA_BLOCK, MOBA_BLOCK), np.float32)), bf)
    rep_np = np.zeros((LANES, N_HEADS * LANES), np.float32)
    for h in range(N_HEADS):
        rep_np[h, h * LANES:(h + 1) * LANES] = 1.0
    rep = jnp.asarray(rep_np, bf)
    tr = _rope_table(positions)

    t_dense = _pick_tile(seq, 512)
    t_ffn = _pick_tile(seq, 512)
    tf = dff // 2 if (dff // 2) % LANES == 0 else dff

    for l in range(depth):
        win = jnp.concatenate([w_in[l], jnp.zeros((d, 1), f32)], axis=1)[:, in_idx].astype(bf)
        wuq = jnp.concatenate([mla_w_uq[l], jnp.zeros((MLA_Q_RANK, 1), f32)], axis=1)[:, uq_idx]
        wuq = jnp.pad(wuq, ((0, MXU_WIDTH - MLA_Q_RANK), (0, 0))).astype(bf)
        wukv = jnp.concatenate([mla_w_ukv[l], jnp.zeros((MLA_KV_RANK, 1), f32)], axis=1)
        wukvk = wukv[:, ukvk_idx].astype(bf)
        wukvv = wukv[:, ukvv_idx].astype(bf)
        rows = [None] * P_ROWS
        rows[P_CQ] = _pad_row(mla_cq_norm[l])
        rows[P_CKV] = _pad_row(mla_ckv_norm[l])
        rows[P_GQ] = _pair_gain(mla_q_norm[l]) * (LOG2E * MLA_QK ** -0.5)
        rows[P_GK] = _pair_gain(mla_k_norm[l])
        rows[P_FQ] = jnp.tile(fox_q_norm[l].astype(f32), N_HEADS) * (LOG2E * HEAD_DIM ** -0.5)
        rows[P_FK] = jnp.tile(fox_k_norm[l].astype(f32), N_HEADS)
        rows[P_FB] = _pad_row(fox_b_f[l])
        rows[P_MQ] = jnp.tile(moba_q_norm[l].astype(f32), N_HEADS)
        rows[P_MK] = jnp.tile(moba_k_norm[l].astype(f32), N_HEADS)
        rows[P_DQ] = jnp.tile(diff_q_norm[l].astype(f32), 2 * N_HEADS) * (LOG2E * DIFF_QK ** -0.5)
        rows[P_DK] = jnp.tile(diff_k_norm[l].astype(f32), 2 * N_HEADS)
        rows[P_EQ] = jnp.tile(mem_q_norm[l].astype(f32), N_HEADS) * (LOG2E * HEAD_DIM ** -0.5)
        par = jnp.stack([r if r is not None else jnp.zeros((MXU_WIDTH,), f32) for r in rows])

        (aq, ak, av, fq, fk, fv, fdcol, fdrow, mq, mk, mv, msel, dq, dk, dv, eq) = _prep_call(
            x, tr, attn_norm[l].reshape(1, d).astype(f32), win, wuq, wukvk, wukvv, gpair, g64, g32, expand, tril,
            rep, par)
        ek, ev = _memkv_call(mem, mem_norm[l].reshape(1, d).astype(f32), mem_w_kv[l].astype(bf), g64,
                             jnp.tile(mem_k_norm[l].astype(f32), N_HEADS).reshape(1, GROUP_WIDTH))

        lam_vec = diff_lambda[l].astype(f32)
        lam_init = 0.8 - 0.6 * math.exp(-0.3 * l)
        lam = (jnp.exp(jnp.sum(lam_vec[0] * lam_vec[1])) - jnp.exp(jnp.sum(lam_vec[2] * lam_vec[3])) + lam_init)
        lam_row = jnp.full((1, GROUP_WIDTH), 1.0, f32) * lam
        gsub = (jnp.tile(diff_sub_norm[l].astype(f32), N_HEADS) * (1.0 - lam_init)).reshape(1, GROUP_WIDTH)

        o_a = _attn_call(_CFG_MLA, aq, ak, av, (), t_dense, t_dense)
        o_b = _attn_call(_CFG_FOX, fq, fk, fv, (fdcol, fdrow), t_dense, t_dense)
        t_moba = max(t_dense, MOBA_BLOCK)
        o_c = _attn_call(_CFG_MOBA, mq, mk, mv, (msel,), t_moba, t_moba)
        o_d = _attn_call(_CFG_DIFF, dq, dk, dv, (g64, gsub, lam_row), t_dense, t_dense)
        o_e = _attn_call(_CFG_MEM, eq, ek, ev, (), t_dense, mem.shape[1])

        cw = jnp.pad(ffn_conv_w[l].astype(f32), ((0, SUBLANES - CONV_WIDTH), (0, 0)))
        x = _ffn_call(x, (o_a, o_b, o_c, o_d, o_e), w_o[l].reshape(5, GROUP_WIDTH, d).astype(bf),
                      ffn_norm[l].reshape(1, d).astype(f32), ffn_w_gate[l].astype(bf), ffn_w_up[l].astype(bf),
                      cw, ffn_conv_b[l].reshape(1, dff).astype(f32), ffn_w_down[l].astype(bf), t_ffn, tf)
    return x
```

```python
import functools
import math

import numpy as np
import jax
import jax.numpy as jnp
from jax import lax
from jax.experimental import pallas as pl
from jax.experimental.pallas import tpu as pltpu

N_HEADS = 4
HEAD_DIM = 64
GROUP_WIDTH = N_HEADS * HEAD_DIM
MLA_Q_RANK = 192
MLA_KV_RANK = 128
MLA_NOPE = 64
MLA_ROPE = 32
MLA_QK = MLA_NOPE + MLA_ROPE
DIFF_QK = HEAD_DIM // 2
ROPE_THETA = 500000.0
ROT_MOBA = HEAD_DIM // 4
ROT_DIFF = DIFF_QK // 4
MOBA_BLOCK = 256
MOBA_TOPK = 3
CONV_WIDTH = 3
EPS = 1e-6
NEG_INF = -1e30
LOG2E = math.log2(math.e)
REMOVED = -3e38

LANES = 128
SUBLANES = 8
MXU_WIDTH = 256
TAIL_ROWS = 16
VMEM_LIMIT_BYTES = 56 * 1024 * 1024

_SRC_CQ = 0
_SRC_CKV = _SRC_CQ + MLA_Q_RANK
_SRC_KR = _SRC_CKV + MLA_KV_RANK
_SRC_FOX = _SRC_KR + MLA_ROPE
_SRC_FOXF = _SRC_FOX + 3 * GROUP_WIDTH
_SRC_MOBA = _SRC_FOXF + N_HEADS
_SRC_DIFF = _SRC_MOBA + 3 * GROUP_WIDTH
_SRC_MEMQ = _SRC_DIFF + 3 * GROUP_WIDTH
_SRC_END = _SRC_MEMQ + GROUP_WIDTH

PK_CQ = 0
PK_CKV = 256
PK_KR = 384
PK_FQ, PK_FK, PK_FV = 896, 1152, 1408
PK_FF = 1664
PK_MQ, PK_MK, PK_MV = 1792, 2048, 2304
PK_DQ, PK_DK, PK_DV = 2560, 2816, 3072
PK_EQ = 3328
PK_END = 3584

PAIR_STRIDE = MLA_QK


def _pair_lane(h, d):
    return (h // 2) * MXU_WIDTH + (h % 2) * PAIR_STRIDE + d


N_FREQ = MLA_ROPE // 2 + ROT_MOBA // 2 + ROT_DIFF // 2
FREQ_BASE_MLA = 0
FREQ_BASE_MOBA = MLA_ROPE // 2
FREQ_BASE_DIFF = FREQ_BASE_MOBA + ROT_MOBA // 2
TR_ONE = 4 * N_FREQ
TR_WIDTH = LANES

(P_CQ, P_CKV, P_GQ, P_GK, P_FQ, P_FK, P_FB, P_MQ, P_MK, P_DQ, P_DK, P_EQ) = range(12)
P_ROWS = 16


def _np_in_index():
    idx = np.full((PK_END,), _SRC_END, np.int32)
    idx[PK_CQ:PK_CQ + MLA_Q_RANK] = np.arange(_SRC_CQ, _SRC_CQ + MLA_Q_RANK)
    idx[PK_CKV:PK_CKV + MLA_KV_RANK] = np.arange(_SRC_CKV, _SRC_CKV + MLA_KV_RANK)
    for h in range(N_HEADS):
        for d in range(MLA_ROPE):
            idx[PK_KR + _pair_lane(h, d)] = _SRC_KR + d
    idx[PK_FQ:PK_FQ + 3 * GROUP_WIDTH] = np.arange(_SRC_FOX, _SRC_FOX + 3 * GROUP_WIDTH)
    idx[PK_FF:PK_FF + N_HEADS] = np.arange(_SRC_FOXF, _SRC_FOXF + N_HEADS)
    idx[PK_MQ:PK_MQ + 3 * GROUP_WIDTH] = np.arange(_SRC_MOBA, _SRC_MOBA + 3 * GROUP_WIDTH)
    idx[PK_DQ:PK_DQ + 3 * GROUP_WIDTH] = np.arange(_SRC_DIFF, _SRC_DIFF + 3 * GROUP_WIDTH)
    idx[PK_EQ:PK_EQ + GROUP_WIDTH] = np.arange(_SRC_MEMQ, _SRC_MEMQ + GROUP_WIDTH)
    return idx


def _np_uq_index():
    idx = np.full((2 * MXU_WIDTH,), N_HEADS * MLA_QK, np.int32)
    for h in range(N_HEADS):
        for d in range(MLA_QK):
            idx[_pair_lane(h, d)] = h * MLA_QK + d
    return idx


def _np_ukv_index():
    zero = N_HEADS * (MLA_NOPE + HEAD_DIM)
    idx_k = np.full((2 * MXU_WIDTH,), zero, np.int32)
    idx_v = np.zeros((GROUP_WIDTH,), np.int32)
    for h in range(N_HEADS):
        for d in range(MLA_NOPE):
            idx_k[_pair_lane(h, MLA_ROPE + d)] = h * (MLA_NOPE + HEAD_DIM) + d
        for d in range(HEAD_DIM):
            idx_v[h * HEAD_DIM + d] = h * (MLA_NOPE + HEAD_DIM) + MLA_NOPE + d
    return idx_k, idx_v


def _np_group_matrix(groups):
    g = np.zeros((MXU_WIDTH, MXU_WIDTH), np.float32)
    for lo, size in groups:
        g[lo:lo + size, lo:lo + size] = 1.0 / size
    return g


_PAIR_GROUPS = [(0, MLA_ROPE), (MLA_ROPE, MLA_NOPE), (PAIR_STRIDE, MLA_ROPE), (PAIR_STRIDE + MLA_ROPE, MLA_NOPE)]
_G64_GROUPS = [(h * HEAD_DIM, HEAD_DIM) for h in range(N_HEADS)]
_G32_GROUPS = [(g * DIFF_QK, DIFF_QK) for g in range(2 * N_HEADS)]


def _np_rope_expand(regions, rot, base):
    half = rot // 2
    e = np.zeros((TR_WIDTH, 3 * MXU_WIDTH), np.float32)
    e[TR_ONE, 0:MXU_WIDTH] = 1.0
    for lo in regions:
        for r in range(half):
            f = base + r
            for lane, tab, sign in ((lo + r, 2, -1.0), (lo + half + r, 1, 1.0)):
                e[TR_ONE, lane] = 0.0
                e[f, lane] = 1.0
                e[N_FREQ + f, lane] = 1.0
                e[2 * N_FREQ + f, tab * MXU_WIDTH + lane] = sign
                e[3 * N_FREQ + f, tab * MXU_WIDTH + lane] = sign
    return e


def _np_rope_expand_all():
    return np.concatenate([
        _np_rope_expand([0, PAIR_STRIDE], MLA_ROPE, FREQ_BASE_MLA),
        _np_rope_expand([h * HEAD_DIM for h in range(N_HEADS)], ROT_MOBA, FREQ_BASE_MOBA),
        _np_rope_expand([g * DIFF_QK for g in range(2 * N_HEADS)], ROT_DIFF, FREQ_BASE_DIFF),
    ], axis=1)


def _dot(a, b):
    return jnp.dot(a, b, preferred_element_type=jnp.float32)


def _dot_nt(a, b):
    return lax.dot_general(a, b, (((1,), (1,)), ((), ())), preferred_element_type=jnp.float32)


def _split2(a):
    hi = a.astype(jnp.bfloat16)
    lo = (a - hi.astype(jnp.float32)).astype(jnp.bfloat16)
    return hi, lo


def _split3(a):
    hi = a.astype(jnp.bfloat16)
    r = a - hi.astype(jnp.float32)
    mid = r.astype(jnp.bfloat16)
    lo = (r - mid.astype(jnp.float32)).astype(jnp.bfloat16)
    return hi, mid, lo


def _group_mean_sq(a, g_bf16):
    return _dot((a * a).astype(jnp.bfloat16), g_bf16)


def _rope(x, tabs, half):
    w = x.shape[-1]
    c, s1, s2 = tabs[:, 0:w], tabs[:, w:2 * w], tabs[:, 2 * w:3 * w]
    return x * c + pltpu.roll(x, half, 1) * s1 + pltpu.roll(x, w - half, 1) * s2


def _lane_mask(shape, lo, hi):
    lane = lax.broadcasted_iota(jnp.int32, shape, len(shape) - 1)
    return (lane >= lo) & (lane < hi)


def _prep_body(x_ref, tr_ref, anorm_ref, win_ref, wuq_ref, wukvk_ref, wukvv_ref, gpair_ref, g64_ref, g32_ref,
               exp_ref, tril_ref, rep_ref, vsel_ref, par_ref,
               aq_ref, ak_ref, av_ref, fq_ref, fk_ref, fv_ref, fdcol_ref, fdrow_ref, mq_ref, mk_ref, mv_ref, msel_ref,
               dq_ref, dk_ref, dv_ref, eq_ref,
               kmean_s, carry_s):
    j = pl.program_id(1)
    tm = x_ref.shape[1]
    bf = jnp.bfloat16

    @pl.when(j == 0)
    def _():
        kmean_s[...] = jnp.zeros_like(kmean_s)
        carry_s[...] = jnp.zeros_like(carry_s)

    x = x_ref[0]
    xn = x * lax.rsqrt(jnp.mean(x * x, axis=-1, keepdims=True) + EPS) * anorm_ref[...]
    xb = xn.astype(bf)

    def proj(off, width):
        return _dot(xb, win_ref[:, off:off + width])

    def prow(r, width=MXU_WIDTH):
        return par_ref[r:r + 1, 0:width]

    tabs = _dot(tr_ref[0], exp_ref[...])
    tab_mla = tabs[:, 0:3 * MXU_WIDTH]
    tab_moba = tabs[:, 3 * MXU_WIDTH:6 * MXU_WIDTH]
    tab_diff = tabs[:, 6 * MXU_WIDTH:9 * MXU_WIDTH]
    gpair, g64, g32 = gpair_ref[...], g64_ref[...], g32_ref[...]

    def group_norm(a, g, gain_row):
        return a * lax.rsqrt(_group_mean_sq(a, g) + EPS) * prow(gain_row)

    cq = proj(PK_CQ, MXU_WIDTH)
    cqn = cq * lax.rsqrt(jnp.sum(cq * cq, axis=-1, keepdims=True) * (1.0 / MLA_Q_RANK) + EPS) * prow(P_CQ)
    qa = _dot(cqn.astype(bf), wuq_ref[...])
    ckv = proj(PK_CKV, MLA_KV_RANK)
    ckvn = ckv * lax.rsqrt(jnp.mean(ckv * ckv, axis=-1, keepdims=True) + EPS) * prow(P_CKV, MLA_KV_RANK)
    ckvb = ckvn.astype(bf)
    ka = proj(PK_KR, 2 * MXU_WIDTH) + _dot(ckvb, wukvk_ref[...])
    for p in range(2):
        sl = slice(p * MXU_WIDTH, (p + 1) * MXU_WIDTH)
        aq_ref[0, :, sl] = _rope(group_norm(qa[:, sl], gpair, P_GQ), tab_mla, MLA_ROPE // 2).astype(bf)
        ak_ref[0, :, sl] = _rope(group_norm(ka[:, sl], gpair, P_GK), tab_mla, MLA_ROPE // 2).astype(bf)
    vsel = vsel_ref[...]
    av_ref[0] = _values_t(_dot(ckvb, wukvv_ref[...]), vsel)

    fq_ref[0] = group_norm(proj(PK_FQ, MXU_WIDTH), g64, P_FQ).astype(bf)
    fk_ref[0] = group_norm(proj(PK_FK, MXU_WIDTH), g64, P_FK).astype(bf)
    fv_ref[0] = _values_t(proj(PK_FV, MXU_WIDTH), vsel)
    z = proj(PK_FF, LANES) + prow(P_FB, LANES)
    log_f = jnp.minimum(z, 0.0) - jnp.log1p(jnp.exp(-jnp.abs(z)))
    log_f = jnp.where(_lane_mask(log_f.shape, 0, N_HEADS), log_f, 0.0)
    tril = tril_ref[...]
    l1, l2, l3 = _split3(log_f)
    dec = carry_s[...] + ((_dot(tril, l1) + _dot(tril, l2)) + _dot(tril, l3))
    carry_s[...] = dec[tm - 1:tm, :]
    d1, d2, d3 = _split3(dec * LOG2E)
    rep = rep_ref[...]
    fdcol_ref[0] = (_dot(d1, rep) + _dot(d2, rep)) + _dot(d3, rep)
    row_sel = jnp.where(lax.broadcasted_iota(jnp.int32, (SUBLANES, LANES), 0)
                        == lax.broadcasted_iota(jnp.int32, (SUBLANES, LANES), 1), 1.0, 0.0).astype(bf)
    fdrow_ref[0] = (_dot_nt(row_sel, d1) + _dot_nt(row_sel, d2)) + _dot_nt(row_sel, d3)

    mq = _rope(group_norm(proj(PK_MQ, MXU_WIDTH), g64, P_MQ), tab_moba, ROT_MOBA // 2)
    mk = _rope(group_norm(proj(PK_MK, MXU_WIDTH), g64, P_MK), tab_moba, ROT_MOBA // 2)
    mq_ref[0] = (mq * (LOG2E * HEAD_DIM ** -0.5)).astype(bf)
    mk_ref[0] = mk.astype(bf)
    mv_ref[0] = _values_t(proj(PK_MV, MXU_WIDTH), vsel)
    kmean_s[pl.ds(j, 1), :] = jnp.mean(mk, axis=0, keepdims=True)
    km_hi, km_lo = _split2(kmean_s[...])
    nbp = kmean_s.shape[0]
    blk = lax.broadcasted_iota(jnp.int32, (nbp, tm), 0)
    past = blk < j
    for h in range(N_HEADS):
        q_hi, q_lo = _split2(jnp.where(_lane_mask(mq.shape, h * HEAD_DIM, (h + 1) * HEAD_DIM), mq, 0.0))
        gate = (_dot_nt(km_hi, q_hi) + _dot_nt(km_lo, q_hi)) + _dot_nt(km_hi, q_lo)
        work = jnp.where(past, gate, NEG_INF)
        sel = jnp.zeros((nbp, tm), jnp.bool_)
        for _ in range(MOBA_TOPK):
            mx = jnp.max(work, axis=0, keepdims=True)
            first = jnp.min(jnp.where(work == mx, blk, nbp), axis=0, keepdims=True)
            pick = blk == first
            sel = sel | pick
            work = jnp.where(pick, REMOVED, work)
        msel_ref[0, h * nbp:(h + 1) * nbp, :] = jnp.where(sel & past, 0.0, NEG_INF)

    dq_ref[0] = _rope(group_norm(proj(PK_DQ, MXU_WIDTH), g32, P_DQ), tab_diff, ROT_DIFF // 2).astype(bf)
    dk_ref[0] = _rope(group_norm(proj(PK_DK, MXU_WIDTH), g32, P_DK), tab_diff, ROT_DIFF // 2).astype(bf)
    dv_ref[0] = _values_t(proj(PK_DV, MXU_WIDTH), vsel)

    eq_ref[0] = group_norm(proj(PK_EQ, MXU_WIDTH), g64, P_EQ).astype(bf)


def _full_spec(shape):
    n = len(shape)
    return pl.BlockSpec(shape, lambda *_: (0,) * n)


def _moba_blocks_padded(seq):
    return -(-(seq // MOBA_BLOCK) // SUBLANES) * SUBLANES


def _prep_call(x, tr, anorm, win, wuq, wukvk, wukvv, gpair, g64, g32, expand, tril, rep, vsel, par):
    bsz, seq, d = x.shape
    tm = MOBA_BLOCK
    assert seq % tm == 0
    nbp = _moba_blocks_padded(seq)
    bf = jnp.bfloat16
    f32 = jnp.float32

    vt = -N_HEADS * V_ROWS
    widths = [(2 * MXU_WIDTH, bf), (2 * MXU_WIDTH, bf), (vt, bf),
              (GROUP_WIDTH, bf), (GROUP_WIDTH, bf), (vt, bf),
              (N_HEADS * LANES, f32), (-SUBLANES, f32),
              (GROUP_WIDTH, bf), (GROUP_WIDTH, bf), (vt, bf), (-N_HEADS * nbp, f32),
              (GROUP_WIDTH, bf), (GROUP_WIDTH, bf), (vt, bf),
              (GROUP_WIDTH, bf)]

    def tok(width):
        if width > 0:
            return pl.BlockSpec((1, tm, width), lambda b, j: (b, j, 0))
        return pl.BlockSpec((1, -width, tm), lambda b, j: (b, 0, j))

    def shape(width):
        return (bsz, seq, width) if width > 0 else (bsz, -width, seq)

    consts = [anorm, win, wuq, wukvk, wukvv, gpair, g64, g32, expand, tril, rep, vsel, par]
    return pl.pallas_call(
        _prep_body,
        grid=(bsz, seq // tm),
        in_specs=[tok(d), tok(TR_WIDTH)] + [_full_spec(c.shape) for c in consts],
        out_specs=[tok(w) for w, _ in widths],
        out_shape=[jax.ShapeDtypeStruct(shape(w), dt) for w, dt in widths],
        scratch_shapes=[pltpu.VMEM((nbp, GROUP_WIDTH), jnp.float32), pltpu.VMEM((1, LANES), jnp.float32)],
        compiler_params=pltpu.CompilerParams(dimension_semantics=("arbitrary", "arbitrary"),
                                             vmem_limit_bytes=VMEM_LIMIT_BYTES),
        name="prep",
    )(x, tr, *consts)


def _memkv_body(mem_ref, mnorm_ref, w_ref, g64_ref, gain_ref, vsel_ref, k_ref, v_ref):
    m = mem_ref[0]
    mn = m * lax.rsqrt(jnp.mean(m * m, axis=-1, keepdims=True) + EPS) * mnorm_ref[...]
    kv = _dot(mn.astype(jnp.bfloat16), w_ref[...])
    k = kv[:, 0:GROUP_WIDTH]
    k = k * lax.rsqrt(_group_mean_sq(k, g64_ref[...]) + EPS) * gain_ref[...]
    k_ref[0] = k.astype(jnp.bfloat16)
    v_ref[0] = _values_t(kv[:, GROUP_WIDTH:2 * GROUP_WIDTH], vsel_ref[...])


def _memkv_call(mem, mnorm, w, g64, gain, vsel):
    bsz, mlen, d = mem.shape
    k_shape, vt_shape = (bsz, mlen, GROUP_WIDTH), (bsz, N_HEADS * V_ROWS, mlen)
    return pl.pallas_call(
        _memkv_body,
        grid=(bsz,),
        in_specs=[pl.BlockSpec((1, mlen, d), lambda b: (b, 0, 0)), _full_spec(mnorm.shape), _full_spec(w.shape),
                  _full_spec(g64.shape), _full_spec(gain.shape), _full_spec(vsel.shape)],
        out_specs=[pl.BlockSpec((1,) + s[1:], lambda b: (b, 0, 0)) for s in (k_shape, vt_shape)],
        out_shape=[jax.ShapeDtypeStruct(s, jnp.bfloat16) for s in (k_shape, vt_shape)],
        compiler_params=pltpu.CompilerParams(dimension_semantics=("arbitrary",), vmem_limit_bytes=VMEM_LIMIT_BYTES),
        name="mem_kv",
    )(mem, mnorm, w, g64, gain, vsel)


class _AttnCfg:
    def __init__(self, name, vheads, n_maps, causal, decay=False, select=False, diff=False):
        self.name = name
        self.vheads = vheads
        self.n_maps = n_maps
        self.causal = causal
        self.decay = decay
        self.select = select
        self.diff = diff


_PLAIN_VHEADS = [(0, h * HEAD_DIM, (h + 1) * HEAD_DIM, 0, h) for h in range(N_HEADS)]
_CFG_MLA = _AttnCfg("attn_mla", [((h // 2) * MXU_WIDTH, (h % 2) * PAIR_STRIDE, (h % 2) * PAIR_STRIDE + MLA_QK, 0, h)
                                 for h in range(N_HEADS)], 1, True)
_CFG_FOX = _AttnCfg("attn_fox", _PLAIN_VHEADS, 1, True, decay=True)
_CFG_MOBA = _AttnCfg("attn_moba", _PLAIN_VHEADS, 1, True, select=True)
_CFG_DIFF = _AttnCfg("attn_diff", [(0, h * HEAD_DIM + c * DIFF_QK, h * HEAD_DIM + (c + 1) * DIFF_QK, c, h)
                                   for c in range(2) for h in range(N_HEADS)], 2, True, diff=True)
_CFG_MEM = _AttnCfg("attn_mem", _PLAIN_VHEADS, 1, False)


ONES_ROW = HEAD_DIM
V_ROWS = HEAD_DIM + 16
QK_LOOKAHEAD = 2


def _np_value_select():
    sel = np.zeros((N_HEADS * V_ROWS, GROUP_WIDTH), np.float32)
    for h in range(N_HEADS):
        for d in range(HEAD_DIM):
            sel[h * V_ROWS + d, h * HEAD_DIM + d] = 1.0
    return sel


def _values_t(v, vsel):
    vt = _dot_nt(vsel, v.astype(jnp.bfloat16))
    row = lax.broadcasted_iota(jnp.int32, vt.shape, 0)
    ones = row == ONES_ROW
    for h in range(1, N_HEADS):
        ones = ones | (row == h * V_ROWS + ONES_ROW)
    return jnp.where(ones, 1.0, vt).astype(jnp.bfloat16)


def _tile_lanes(x, width):
    return jnp.tile(x, (1, width // LANES)) if width != LANES else x


def _attn_body(cfg, qi_ref, kj_ref, *refs):
    refs = list(refs)
    q_ref, k_ref, vt_ref = refs[:3]
    pos = 3
    if cfg.decay:
        dq_ref, dk_ref = refs[pos:pos + 2]
        pos += 2
    if cfg.select:
        sel_ref = refs[pos]
        pos += 1
    if cfg.diff:
        g64_ref, gsub_ref, lam_ref = refs[pos:pos + 3]
        pos += 3
    o_ref, qm_s, m_s, acc_s = refs[pos:pos + 4]

    t = pl.program_id(1)
    i = qi_ref[t]
    j = kj_ref[t]
    tq = q_ref.shape[1]
    tk = k_ref.shape[1]

    @pl.when(j == 0)
    def _():
        for n, (off, lo, hi, _, _) in enumerate(cfg.vheads):
            qb = q_ref[0, :, off:off + MXU_WIDTH]
            qm_s[n] = jnp.where(_lane_mask(qb.shape, lo, hi), qb, jnp.zeros_like(qb))
        m_s[...] = jnp.full(m_s.shape, NEG_INF, jnp.float32)
        acc_s[...] = jnp.zeros_like(acc_s)

    def step(diag):
        if diag:
            key = lax.broadcasted_iota(jnp.int32, (tk, tq), 0)
            qry = lax.broadcasted_iota(jnp.int32, (tk, tq), 1)
            keep = key <= qry
        def scores(n):
            off = cfg.vheads[n][0]
            return _dot_nt(k_ref[0, :, off:off + MXU_WIDTH], qm_s[n])

        raw = {n: scores(n) for n in range(min(QK_LOOKAHEAD, len(cfg.vheads)))}
        for n, (off, lo, hi, c, h) in enumerate(cfg.vheads):
            s = raw.pop(n)
            if cfg.decay:
                s = (dq_ref[0, h:h + 1, :] - _tile_lanes(dk_ref[0, :, h * LANES:(h + 1) * LANES], tq)) + s
            if cfg.select:
                nbp = sel_ref.shape[1] // N_HEADS
                qpos = lax.broadcasted_iota(jnp.int32, (1, tq), 1)
                parts = []
                for kb in range(tk // MOBA_BLOCK):
                    rows = s[kb * MOBA_BLOCK:(kb + 1) * MOBA_BLOCK, :]
                    if not (diag and kb == tk // MOBA_BLOCK - 1):
                        bias = sel_ref[0, pl.ds(h * nbp + j * (tk // MOBA_BLOCK) + kb, 1), :]
                        if diag:
                            bias = jnp.where(qpos < (kb + 1) * MOBA_BLOCK, 0.0, bias)
                        rows = rows + bias
                    parts.append(rows)
                s = parts[0] if len(parts) == 1 else jnp.concatenate(parts, axis=0)
            if diag:
                s = jnp.where(keep, s, NEG_INF)
            m_prev = m_s[n]
            m_new = jnp.maximum(m_prev, jnp.max(s, axis=0, keepdims=True))
            alpha = jnp.exp2(m_prev - m_new)
            p = jnp.exp2(s - m_new)
            m_s[n] = m_new
            acc_s[n] = acc_s[n] * alpha + _dot(vt_ref[0, h * V_ROWS:(h + 1) * V_ROWS, :], p.astype(jnp.bfloat16))
            if n + QK_LOOKAHEAD < len(cfg.vheads):
                raw[n + QK_LOOKAHEAD] = scores(n + QK_LOOKAHEAD)

    if cfg.causal:
        pl.when(j < i)(functools.partial(step, False))
        pl.when(j == i)(functools.partial(step, True))
        last = j == i
    else:
        step(False)
        last = j == 0

    @pl.when(last)
    def _():
        outs = []
        for c in range(cfg.n_maps):
            heads = []
            for h in range(N_HEADS):
                acc = acc_s[c * N_HEADS + h]
                heads.append(acc[0:HEAD_DIM, :] / acc[ONES_ROW:ONES_ROW + 1, :])
            outs.append(jnp.concatenate(heads, axis=0).T)
        if cfg.diff:
            o = outs[0] - lam_ref[0:1, :] * outs[1]
            o = o * lax.rsqrt(_group_mean_sq(o, g64_ref[...]) + EPS) * gsub_ref[...]
        else:
            o = outs[0]
        o_ref[0] = o.astype(o_ref.dtype)


def _attn_call(cfg, q, k, v, extras, tq, tk):
    bsz, seq, wq = q.shape
    sk = k.shape[1]
    nq = seq // tq
    assert seq % tq == 0 and sk % tk == 0
    if cfg.causal:
        assert tq == tk and sk == seq
        pairs = [(i, j) for i in range(nq) for j in range(i + 1)]
    else:
        assert sk == tk
        pairs = [(i, 0) for i in range(nq)]
    qi = jnp.asarray(np.array([p[0] for p in pairs], np.int32))
    kj = jnp.asarray(np.array([p[1] for p in pairs], np.int32))
    n_vh = len(cfg.vheads)

    in_specs = [pl.BlockSpec((1, tq, wq), lambda b, t, qi, kj: (b, qi[t], 0)),
                pl.BlockSpec((1, tk, wq), lambda b, t, qi, kj: (b, kj[t], 0)),
                pl.BlockSpec((1, N_HEADS * V_ROWS, tk), lambda b, t, qi, kj: (b, 0, kj[t]))]
    args = [q, k, v]
    if cfg.decay:
        dcol, drow = extras
        in_specs += [pl.BlockSpec((1, SUBLANES, tq), lambda b, t, qi, kj: (b, 0, qi[t])),
                     pl.BlockSpec((1, tk, N_HEADS * LANES), lambda b, t, qi, kj: (b, kj[t], 0))]
        args += [drow, dcol]
    if cfg.select:
        (sel,) = extras
        in_specs += [pl.BlockSpec((1, sel.shape[1], tq), lambda b, t, qi, kj: (b, 0, qi[t]))]
        args += [sel]
    if cfg.diff:
        for c in extras:
            in_specs += [pl.BlockSpec(c.shape, lambda b, t, qi, kj: (0, 0))]
        args += list(extras)

    grid_spec = pltpu.PrefetchScalarGridSpec(
        num_scalar_prefetch=2,
        grid=(bsz, len(pairs)),
        in_specs=in_specs,
        out_specs=pl.BlockSpec((1, tq, GROUP_WIDTH), lambda b, t, qi, kj: (b, qi[t], 0)),
        scratch_shapes=[pltpu.VMEM((n_vh, tq, MXU_WIDTH), jnp.bfloat16),
                        pltpu.VMEM((n_vh, 1, tq), jnp.float32),
                        pltpu.VMEM((n_vh, V_ROWS, tq), jnp.float32)])
    return pl.pallas_call(
        functools.partial(_attn_body, cfg),
        grid_spec=grid_spec,
        out_shape=jax.ShapeDtypeStruct((bsz, seq, GROUP_WIDTH), jnp.bfloat16),
        compiler_params=pltpu.CompilerParams(dimension_semantics=("arbitrary", "arbitrary"),
                                             vmem_limit_bytes=VMEM_LIMIT_BYTES),
        name=cfg.name,
    )(qi, kj, *args)


def _ffn_body(x_ref, oa_ref, ob_ref, oc_ref, od_ref, oe_ref, wo_ref, fnorm_ref, wg_ref, wu_ref, cw_ref, cb_ref, wd_ref,
              out_ref, xnew_s, xn_s, tail_s, acc_s):
    i = pl.program_id(1)
    f = pl.program_id(2)
    tm = x_ref.shape[1]

    @pl.when(f == 0)
    def _():
        @pl.when(i == 0)
        def _():
            tail_s[...] = jnp.zeros_like(tail_s)

        @pl.when(i > 0)
        def _():
            tail_s[...] = xn_s[tm - TAIL_ROWS:tm, :]

        xnew = x_ref[0]
        for g, o_ref in enumerate((oa_ref, ob_ref, oc_ref, od_ref, oe_ref)):
            xnew = xnew + _dot(o_ref[0], wo_ref[g])
        xnew_s[...] = xnew
        xn = xnew * lax.rsqrt(jnp.mean(xnew * xnew, axis=-1, keepdims=True) + EPS) * fnorm_ref[...]
        xn_s[...] = xn.astype(xn_s.dtype)
        acc_s[...] = jnp.zeros_like(acc_s)

    wg = wg_ref[...]
    g0 = _dot(xn_s[...], wg)
    gt = _dot(tail_s[...], wg)
    u = _dot(xn_s[...], wu_ref[...])
    t1 = gt[TAIL_ROWS - 1:TAIL_ROWS, :]
    t2 = gt[TAIL_ROWS - 2:TAIL_ROWS - 1, :]
    row = lax.broadcasted_iota(jnp.int32, g0.shape, 0)
    g1 = jnp.where(row == 0, t1, pltpu.roll(g0, 1, 0))
    g2 = jnp.where(row == 0, t2, jnp.where(row == 1, t1, pltpu.roll(g0, 2, 0)))
    y = cb_ref[...] + cw_ref[0:1, :] * g2
    y = y + cw_ref[1:2, :] * g1
    y = y + cw_ref[2:3, :] * g0
    hmid = (y * (1.0 / (1.0 + jnp.exp(-y)))) * u
    acc_s[...] += _dot(hmid.astype(jnp.bfloat16), wd_ref[...])

    @pl.when(f == pl.num_programs(2) - 1)
    def _():
        out_ref[0] = xnew_s[...] + acc_s[...]


def _ffn_call(x, outs, wo, fnorm, wg, wu, cw, cb, wd, tm, tf):
    bsz, seq, d = x.shape
    dff = wg.shape[1]
    assert seq % tm == 0 and dff % tf == 0
    tok = lambda w: pl.BlockSpec((1, tm, w), lambda b, i, f: (b, i, 0))
    return pl.pallas_call(
        _ffn_body,
        grid=(bsz, seq // tm, dff // tf),
        in_specs=[tok(d)] + [tok(GROUP_WIDTH)] * 5 + [
            pl.BlockSpec(wo.shape, lambda b, i, f: (0, 0, 0)),
            pl.BlockSpec(fnorm.shape, lambda b, i, f: (0, 0)),
            pl.BlockSpec((d, tf), lambda b, i, f: (0, f)),
            pl.BlockSpec((d, tf), lambda b, i, f: (0, f)),
            pl.BlockSpec((SUBLANES, tf), lambda b, i, f: (0, f)),
            pl.BlockSpec((1, tf), lambda b, i, f: (0, f)),
            pl.BlockSpec((tf, d), lambda b, i, f: (f, 0))],
        out_specs=tok(d),
        out_shape=jax.ShapeDtypeStruct((bsz, seq, d), jnp.float32),
        scratch_shapes=[pltpu.VMEM((tm, d), jnp.float32), pltpu.VMEM((tm, d), jnp.bfloat16),
                        pltpu.VMEM((TAIL_ROWS, d), jnp.bfloat16), pltpu.VMEM((tm, d), jnp.float32)],
        compiler_params=pltpu.CompilerParams(dimension_semantics=("arbitrary", "arbitrary", "arbitrary"),
                                             vmem_limit_bytes=VMEM_LIMIT_BYTES),
        name="ffn",
    )(x, *outs, wo, fnorm, wg, wu, cw, cb, wd)


def _pad_row(v, width=MXU_WIDTH):
    v = v.astype(jnp.float32).reshape(-1)
    return jnp.pad(v, (0, width - v.shape[0]))


def _pair_gain(g):
    one = jnp.concatenate([g.astype(jnp.float32), g.astype(jnp.float32),
                           jnp.zeros((MXU_WIDTH - 2 * MLA_QK,), jnp.float32)])
    return one


def _rope_table(positions):
    pos = positions.astype(jnp.float32)[:, :, None]
    cols_c, cols_s = [], []
    for rot in (MLA_ROPE, ROT_MOBA, ROT_DIFF):
        inv = ROPE_THETA ** (-jnp.arange(0, rot, 2, dtype=jnp.float32) / rot)
        ang = pos * inv
        cols_c.append(jnp.cos(ang))
        cols_s.append(jnp.sin(ang))
    c = jnp.concatenate(cols_c, axis=-1)
    s = jnp.concatenate(cols_s, axis=-1)
    c_hi = c.astype(jnp.bfloat16)
    c_lo = (c - c_hi.astype(jnp.float32)).astype(jnp.bfloat16)
    s_hi = s.astype(jnp.bfloat16)
    s_lo = (s - s_hi.astype(jnp.float32)).astype(jnp.bfloat16)
    one = jnp.ones(pos.shape[:2] + (1,), jnp.bfloat16)
    pad = jnp.zeros(pos.shape[:2] + (TR_WIDTH - TR_ONE - 1,), jnp.bfloat16)
    return jnp.concatenate([c_hi, c_lo, s_hi, s_lo, one, pad], axis=-1)


def _pick_tile(n, pref):
    t = pref
    while n % t:
        t //= 2
    return t


def kernel(x, mem, positions, attn_norm, ffn_norm, mem_norm, w_in, mla_cq_norm, mla_ckv_norm, mla_w_uq, mla_w_ukv, mla_q_norm, mla_k_norm, fox_b_f, fox_q_norm, fox_k_norm, moba_q_norm, moba_k_norm, diff_lambda, diff_q_norm, diff_k_norm, diff_sub_norm, mem_w_kv, mem_q_norm, mem_k_norm, w_o, ffn_w_gate, ffn_w_up, ffn_conv_w, ffn_conv_b, ffn_w_down):
    bsz, seq, d = x.shape
    depth = w_in.shape[0]
    dff = ffn_w_gate.shape[2]
    bf = jnp.bfloat16
    f32 = jnp.float32

    in_idx = _np_in_index()
    uq_idx = _np_uq_index()
    ukvk_idx, ukvv_idx = _np_ukv_index()
    gpair = jnp.asarray(_np_group_matrix(_PAIR_GROUPS), bf)
    g64 = jnp.asarray(_np_group_matrix(_G64_GROUPS), bf)
    g32 = jnp.asarray(_np_group_matrix(_G32_GROUPS), bf)
    expand = jnp.asarray(_np_rope_expand_all(), bf)
    tril = jnp.asarray(np.tril(np.ones((MOBA_BLOCK, MOBA_BLOCK), np.float32)), bf)
    rep_np = np.zeros((LANES, N_HEADS * LANES), np.float32)
    for h in range(N_HEADS):
        rep_np[h, h * LANES:(h + 1) * LANES] = 1.0
    rep = jnp.asarray(rep_np, bf)
    vsel = jnp.asarray(_np_value_select(), bf)
    tr = _rope_table(positions)

    t_dense = _pick_tile(seq, 512)
    t_ffn = _pick_tile(seq, 512)
    tf = dff // 2 if (dff // 2) % LANES == 0 else dff

    for l in range(depth):
        win = jnp.concatenate([w_in[l], jnp.zeros((d, 1), f32)], axis=1)[:, in_idx].astype(bf)
        wuq = jnp.concatenate([mla_w_uq[l], jnp.zeros((MLA_Q_RANK, 1), f32)], axis=1)[:, uq_idx]
        wuq = jnp.pad(wuq, ((0, MXU_WIDTH - MLA_Q_RANK), (0, 0))).astype(bf)
        wukv = jnp.concatenate([mla_w_ukv[l], jnp.zeros((MLA_KV_RANK, 1), f32)], axis=1)
        wukvk = wukv[:, ukvk_idx].astype(bf)
        wukvv = wukv[:, ukvv_idx].astype(bf)
        rows = [None] * P_ROWS
        rows[P_CQ] = _pad_row(mla_cq_norm[l])
        rows[P_CKV] = _pad_row(mla_ckv_norm[l])
        rows[P_GQ] = _pair_gain(mla_q_norm[l]) * (LOG2E * MLA_QK ** -0.5)
        rows[P_GK] = _pair_gain(mla_k_norm[l])
        rows[P_FQ] = jnp.tile(fox_q_norm[l].astype(f32), N_HEADS) * (LOG2E * HEAD_DIM ** -0.5)
        rows[P_FK] = jnp.tile(fox_k_norm[l].astype(f32), N_HEADS)
        rows[P_FB] = _pad_row(fox_b_f[l])
        rows[P_MQ] = jnp.tile(moba_q_norm[l].astype(f32), N_HEADS)
        rows[P_MK] = jnp.tile(moba_k_norm[l].astype(f32), N_HEADS)
        rows[P_DQ] = jnp.tile(diff_q_norm[l].astype(f32), 2 * N_HEADS) * (LOG2E * DIFF_QK ** -0.5)
        rows[P_DK] = jnp.tile(diff_k_norm[l].astype(f32), 2 * N_HEADS)
        rows[P_EQ] = jnp.tile(mem_q_norm[l].astype(f32), N_HEADS) * (LOG2E * HEAD_DIM ** -0.5)
        par = jnp.stack([r if r is not None else jnp.zeros((MXU_WIDTH,), f32) for r in rows])

        (aq, ak, av, fq, fk, fv, fdcol, fdrow, mq, mk, mv, msel, dq, dk, dv, eq) = _prep_call(
            x, tr, attn_norm[l].reshape(1, d).astype(f32), win, wuq, wukvk, wukvv, gpair, g64, g32, expand, tril,
            rep, vsel, par)
        ek, ev = _memkv_call(mem, mem_norm[l].reshape(1, d).astype(f32), mem_w_kv[l].astype(bf), g64,
                             jnp.tile(mem_k_norm[l].astype(f32), N_HEADS).reshape(1, GROUP_WIDTH), vsel)

        lam_vec = diff_lambda[l].astype(f32)
        lam_init = 0.8 - 0.6 * math.exp(-0.3 * l)
        lam = (jnp.exp(jnp.sum(lam_vec[0] * lam_vec[1])) - jnp.exp(jnp.sum(lam_vec[2] * lam_vec[3])) + lam_init)
        lam_row = jnp.full((1, GROUP_WIDTH), 1.0, f32) * lam
        gsub = (jnp.tile(diff_sub_norm[l].astype(f32), N_HEADS) * (1.0 - lam_init)).reshape(1, GROUP_WIDTH)

        o_a = _attn_call(_CFG_MLA, aq, ak, av, (), t_dense, t_dense)
        o_b = _attn_call(_CFG_FOX, fq, fk, fv, (fdcol, fdrow), t_dense, t_dense)
        t_moba = max(t_dense, MOBA_BLOCK)
        o_c = _attn_call(_CFG_MOBA, mq, mk, mv, (msel,), t_moba, t_moba)
        o_d = _attn_call(_CFG_DIFF, dq, dk, dv, (g64, gsub, lam_row), t_dense, t_dense)
        o_e = _attn_call(_CFG_MEM, eq, ek, ev, (), t_dense, mem.shape[1])

        cw = jnp.pad(ffn_conv_w[l].astype(f32), ((0, SUBLANES - CONV_WIDTH), (0, 0)))
        x = _ffn_call(x, (o_a, o_b, o_c, o_d, o_e), w_o[l].reshape(5, GROUP_WIDTH, d).astype(bf),
                      ffn_norm[l].reshape(1, d).astype(f32), ffn_w_gate[l].astype(bf), ffn_w_up[l].astype(bf),
                      cw, ffn_conv_b[l].reshape(1, dff).astype(f32), ffn_w_down[l].astype(bf), t_ffn, tf)
    return x
```

```python
import functools
import math

import numpy as np
import jax
import jax.numpy as jnp
from jax import lax
from jax.experimental import pallas as pl
from jax.experimental.pallas import tpu as pltpu

N_HEADS = 4
HEAD_DIM = 64
GROUP_WIDTH = N_HEADS * HEAD_DIM
MLA_Q_RANK = 192
MLA_KV_RANK = 128
MLA_NOPE = 64
MLA_ROPE = 32
MLA_QK = MLA_NOPE + MLA_ROPE
DIFF_QK = HEAD_DIM // 2
ROPE_THETA = 500000.0
ROT_MOBA = HEAD_DIM // 4
ROT_DIFF = DIFF_QK // 4
MOBA_BLOCK = 256
MOBA_TOPK = 3
CONV_WIDTH = 3
EPS = 1e-6
NEG_INF = -1e30
LOG2E = math.log2(math.e)
REMOVED = -3e38

LANES = 128
SUBLANES = 8
MXU_WIDTH = 256
TAIL_ROWS = 16
VMEM_LIMIT_BYTES = 56 * 1024 * 1024

_SRC_CQ = 0
_SRC_CKV = _SRC_CQ + MLA_Q_RANK
_SRC_KR = _SRC_CKV + MLA_KV_RANK
_SRC_FOX = _SRC_KR + MLA_ROPE
_SRC_FOXF = _SRC_FOX + 3 * GROUP_WIDTH
_SRC_MOBA = _SRC_FOXF + N_HEADS
_SRC_DIFF = _SRC_MOBA + 3 * GROUP_WIDTH
_SRC_MEMQ = _SRC_DIFF + 3 * GROUP_WIDTH
_SRC_END = _SRC_MEMQ + GROUP_WIDTH

PK_CQ = 0
PK_CKV = 256
PK_KR = 384
PK_FQ, PK_FK, PK_FV = 896, 1152, 1408
PK_FF = 1664
PK_MQ, PK_MK, PK_MV = 1792, 2048, 2304
PK_DQ, PK_DK, PK_DV = 2560, 2816, 3072
PK_EQ = 3328
PK_END = 3584

PAIR_STRIDE = MLA_QK


def _pair_lane(h, d):
    return (h // 2) * MXU_WIDTH + (h % 2) * PAIR_STRIDE + d


N_FREQ = MLA_ROPE // 2 + ROT_MOBA // 2 + ROT_DIFF // 2
FREQ_BASE_MLA = 0
FREQ_BASE_MOBA = MLA_ROPE // 2
FREQ_BASE_DIFF = FREQ_BASE_MOBA + ROT_MOBA // 2
TR_ONE = 4 * N_FREQ
TR_WIDTH = LANES

(P_CQ, P_CKV, P_GQ, P_GK, P_FQ, P_FK, P_FB, P_MQ, P_MK, P_DQ, P_DK, P_EQ) = range(12)
P_ROWS = 16


def _np_in_index():
    idx = np.full((PK_END,), _SRC_END, np.int32)
    idx[PK_CQ:PK_CQ + MLA_Q_RANK] = np.arange(_SRC_CQ, _SRC_CQ + MLA_Q_RANK)
    idx[PK_CKV:PK_CKV + MLA_KV_RANK] = np.arange(_SRC_CKV, _SRC_CKV + MLA_KV_RANK)
    for h in range(N_HEADS):
        for d in range(MLA_ROPE):
            idx[PK_KR + _pair_lane(h, d)] = _SRC_KR + d
    idx[PK_FQ:PK_FQ + 3 * GROUP_WIDTH] = np.arange(_SRC_FOX, _SRC_FOX + 3 * GROUP_WIDTH)
    idx[PK_FF:PK_FF + N_HEADS] = np.arange(_SRC_FOXF, _SRC_FOXF + N_HEADS)
    idx[PK_MQ:PK_MQ + 3 * GROUP_WIDTH] = np.arange(_SRC_MOBA, _SRC_MOBA + 3 * GROUP_WIDTH)
    idx[PK_DQ:PK_DQ + 3 * GROUP_WIDTH] = np.arange(_SRC_DIFF, _SRC_DIFF + 3 * GROUP_WIDTH)
    idx[PK_EQ:PK_EQ + GROUP_WIDTH] = np.arange(_SRC_MEMQ, _SRC_MEMQ + GROUP_WIDTH)
    return idx


def _np_uq_index():
    idx = np.full((2 * MXU_WIDTH,), N_HEADS * MLA_QK, np.int32)
    for h in range(N_HEADS):
        for d in range(MLA_QK):
            idx[_pair_lane(h, d)] = h * MLA_QK + d
    return idx


def _np_ukv_index():
    zero = N_HEADS * (MLA_NOPE + HEAD_DIM)
    idx_k = np.full((2 * MXU_WIDTH,), zero, np.int32)
    idx_v = np.zeros((GROUP_WIDTH,), np.int32)
    for h in range(N_HEADS):
        for d in range(MLA_NOPE):
            idx_k[_pair_lane(h, MLA_ROPE + d)] = h * (MLA_NOPE + HEAD_DIM) + d
        for d in range(HEAD_DIM):
            idx_v[h * HEAD_DIM + d] = h * (MLA_NOPE + HEAD_DIM) + MLA_NOPE + d
    return idx_k, idx_v


def _np_group_matrix(groups):
    g = np.zeros((MXU_WIDTH, MXU_WIDTH), np.float32)
    for lo, size in groups:
        g[lo:lo + size, lo:lo + size] = 1.0 / size
    return g


_PAIR_GROUPS = [(0, MLA_ROPE), (MLA_ROPE, MLA_NOPE), (PAIR_STRIDE, MLA_ROPE), (PAIR_STRIDE + MLA_ROPE, MLA_NOPE)]
_G64_GROUPS = [(h * HEAD_DIM, HEAD_DIM) for h in range(N_HEADS)]
_G32_GROUPS = [(g * DIFF_QK, DIFF_QK) for g in range(2 * N_HEADS)]


def _np_rope_expand(regions, rot, base):
    half = rot // 2
    e = np.zeros((TR_WIDTH, 3 * MXU_WIDTH), np.float32)
    e[TR_ONE, 0:MXU_WIDTH] = 1.0
    for lo in regions:
        for r in range(half):
            f = base + r
            for lane, tab, sign in ((lo + r, 2, -1.0), (lo + half + r, 1, 1.0)):
                e[TR_ONE, lane] = 0.0
                e[f, lane] = 1.0
                e[N_FREQ + f, lane] = 1.0
                e[2 * N_FREQ + f, tab * MXU_WIDTH + lane] = sign
                e[3 * N_FREQ + f, tab * MXU_WIDTH + lane] = sign
    return e


def _np_rope_expand_all():
    return np.concatenate([
        _np_rope_expand([0, PAIR_STRIDE], MLA_ROPE, FREQ_BASE_MLA),
        _np_rope_expand([h * HEAD_DIM for h in range(N_HEADS)], ROT_MOBA, FREQ_BASE_MOBA),
        _np_rope_expand([g * DIFF_QK for g in range(2 * N_HEADS)], ROT_DIFF, FREQ_BASE_DIFF),
    ], axis=1)


def _dot(a, b):
    return jnp.dot(a, b, preferred_element_type=jnp.float32)


def _dot_nt(a, b):
    return lax.dot_general(a, b, (((1,), (1,)), ((), ())), preferred_element_type=jnp.float32)


def _split2(a):
    hi = a.astype(jnp.bfloat16)
    lo = (a - hi.astype(jnp.float32)).astype(jnp.bfloat16)
    return hi, lo


def _split3(a):
    hi = a.astype(jnp.bfloat16)
    r = a - hi.astype(jnp.float32)
    mid = r.astype(jnp.bfloat16)
    lo = (r - mid.astype(jnp.float32)).astype(jnp.bfloat16)
    return hi, mid, lo


def _group_mean_sq(a, g_bf16):
    return _dot((a * a).astype(jnp.bfloat16), g_bf16)


def _rope(x, tabs, half):
    w = x.shape[-1]
    c, s1, s2 = tabs[:, 0:w], tabs[:, w:2 * w], tabs[:, 2 * w:3 * w]
    return x * c + pltpu.roll(x, half, 1) * s1 + pltpu.roll(x, w - half, 1) * s2


def _lane_mask(shape, lo, hi):
    lane = lax.broadcasted_iota(jnp.int32, shape, len(shape) - 1)
    return (lane >= lo) & (lane < hi)


def _prep_body(x_ref, tr_ref, anorm_ref, win_ref, wuq_ref, wukvk_ref, wukvv_ref, gpair_ref, g64_ref, g32_ref,
               exp_ref, tril_ref, rep_ref, vsel_ref, par_ref,
               aq_ref, ak_ref, av_ref, fq_ref, fk_ref, fv_ref, fdcol_ref, fdrow_ref, mq_ref, mk_ref, mv_ref, msel_ref,
               dq_ref, dk_ref, dv_ref, eq_ref,
               kmean_s, carry_s):
    j = pl.program_id(1)
    tm = x_ref.shape[1]
    bf = jnp.bfloat16

    @pl.when(j == 0)
    def _():
        kmean_s[...] = jnp.zeros_like(kmean_s)
        carry_s[...] = jnp.zeros_like(carry_s)

    x = x_ref[0]
    xn = x * lax.rsqrt(jnp.mean(x * x, axis=-1, keepdims=True) + EPS) * anorm_ref[...]
    xb = xn.astype(bf)

    def proj(off, width):
        return _dot(xb, win_ref[:, off:off + width])

    def prow(r, width=MXU_WIDTH):
        return par_ref[r:r + 1, 0:width]

    gpair, g64, g32 = gpair_ref[...], g64_ref[...], g32_ref[...]
    vsel = vsel_ref[...]

    p_cq = proj(PK_CQ, MXU_WIDTH)
    p_ckv = proj(PK_CKV, MLA_KV_RANK)
    p_ff = proj(PK_FF, LANES)
    direct = {name: proj(off, MXU_WIDTH) for name, off in
              (("fq", PK_FQ), ("fk", PK_FK), ("mq", PK_MQ), ("mk", PK_MK), ("dq", PK_DQ), ("dk", PK_DK),
               ("eq", PK_EQ))}
    p_kr = proj(PK_KR, 2 * MXU_WIDTH)
    tabs = _dot(tr_ref[0], exp_ref[...])
    tab_mla = tabs[:, 0:3 * MXU_WIDTH]
    tab_moba = tabs[:, 3 * MXU_WIDTH:6 * MXU_WIDTH]
    tab_diff = tabs[:, 6 * MXU_WIDTH:9 * MXU_WIDTH]
    values = {name: proj(off, MXU_WIDTH).astype(bf) for name, off in
              (("fv", PK_FV), ("mv", PK_MV), ("dv", PK_DV))}

    cqn = p_cq * lax.rsqrt(jnp.sum(p_cq * p_cq, axis=-1, keepdims=True) * (1.0 / MLA_Q_RANK) + EPS) * prow(P_CQ)
    ckvn = p_ckv * lax.rsqrt(jnp.mean(p_ckv * p_ckv, axis=-1, keepdims=True) + EPS) * prow(P_CKV, MLA_KV_RANK)
    ckvb = ckvn.astype(bf)
    z = p_ff + prow(P_FB, LANES)
    log_f = jnp.minimum(z, 0.0) - jnp.log1p(jnp.exp(-jnp.abs(z)))
    log_f = jnp.where(_lane_mask(log_f.shape, 0, N_HEADS), log_f, 0.0)
    l1, l2, l3 = _split3(log_f)

    qa = _dot(cqn.astype(bf), wuq_ref[...])
    ka = p_kr + _dot(ckvb, wukvk_ref[...])
    values["av"] = _dot(ckvb, wukvv_ref[...]).astype(bf)
    gmat = {"fq": g64, "fk": g64, "mq": g64, "mk": g64, "dq": g32, "dk": g32, "eq": g64}
    ms = {name: _group_mean_sq(a, gmat[name]) for name, a in direct.items()}
    tril = tril_ref[...]
    dec = carry_s[...] + ((_dot(tril, l1) + _dot(tril, l2)) + _dot(tril, l3))
    carry_s[...] = dec[tm - 1:tm, :]
    for name, ref in (("fv", fv_ref), ("mv", mv_ref), ("dv", dv_ref), ("av", av_ref)):
        ref[0] = _values_t(values[name], vsel)

    gains = {"fq": P_FQ, "fk": P_FK, "mq": P_MQ, "mk": P_MK, "dq": P_DQ, "dk": P_DK, "eq": P_EQ}
    normed = {name: a * lax.rsqrt(ms[name] + EPS) * prow(gains[name]) for name, a in direct.items()}
    ms_qa = [_group_mean_sq(qa[:, p * MXU_WIDTH:(p + 1) * MXU_WIDTH], gpair) for p in range(2)]
    ms_ka = [_group_mean_sq(ka[:, p * MXU_WIDTH:(p + 1) * MXU_WIDTH], gpair) for p in range(2)]
    d1, d2, d3 = _split3(dec * LOG2E)
    rep = rep_ref[...]
    fdcol_ref[0] = (_dot(d1, rep) + _dot(d2, rep)) + _dot(d3, rep)
    row_sel = jnp.where(lax.broadcasted_iota(jnp.int32, (SUBLANES, LANES), 0)
                        == lax.broadcasted_iota(jnp.int32, (SUBLANES, LANES), 1), 1.0, 0.0).astype(bf)
    fdrow_ref[0] = (_dot_nt(row_sel, d1) + _dot_nt(row_sel, d2)) + _dot_nt(row_sel, d3)

    fq_ref[0] = normed["fq"].astype(bf)
    fk_ref[0] = normed["fk"].astype(bf)
    eq_ref[0] = normed["eq"].astype(bf)
    dq_ref[0] = _rope(normed["dq"], tab_diff, ROT_DIFF // 2).astype(bf)
    dk_ref[0] = _rope(normed["dk"], tab_diff, ROT_DIFF // 2).astype(bf)
    mq = _rope(normed["mq"], tab_moba, ROT_MOBA // 2)
    mk = _rope(normed["mk"], tab_moba, ROT_MOBA // 2)
    mq_ref[0] = (mq * (LOG2E * HEAD_DIM ** -0.5)).astype(bf)
    mk_ref[0] = mk.astype(bf)
    for p in range(2):
        sl = slice(p * MXU_WIDTH, (p + 1) * MXU_WIDTH)
        aq_ref[0, :, sl] = _rope(qa[:, sl] * lax.rsqrt(ms_qa[p] + EPS) * prow(P_GQ), tab_mla,
                                 MLA_ROPE // 2).astype(bf)
        ak_ref[0, :, sl] = _rope(ka[:, sl] * lax.rsqrt(ms_ka[p] + EPS) * prow(P_GK), tab_mla,
                                 MLA_ROPE // 2).astype(bf)

    kmean_s[pl.ds(j, 1), :] = jnp.mean(mk, axis=0, keepdims=True)
    km_hi, km_lo = _split2(kmean_s[...])
    nbp = kmean_s.shape[0]
    blk = lax.broadcasted_iota(jnp.int32, (nbp, tm), 0)
    past = blk < j
    for h in range(N_HEADS):
        q_hi, q_lo = _split2(jnp.where(_lane_mask(mq.shape, h * HEAD_DIM, (h + 1) * HEAD_DIM), mq, 0.0))
        gate = (_dot_nt(km_hi, q_hi) + _dot_nt(km_lo, q_hi)) + _dot_nt(km_hi, q_lo)
        work = jnp.where(past, gate, NEG_INF)
        sel = jnp.zeros((nbp, tm), jnp.bool_)
        for _ in range(MOBA_TOPK):
            mx = jnp.max(work, axis=0, keepdims=True)
            first = jnp.min(jnp.where(work == mx, blk, nbp), axis=0, keepdims=True)
            pick = blk == first
            sel = sel | pick
            work = jnp.where(pick, REMOVED, work)
        msel_ref[0, h * nbp:(h + 1) * nbp, :] = jnp.where(sel & past, 0.0, NEG_INF)


def _full_spec(shape):
    n = len(shape)
    return pl.BlockSpec(shape, lambda *_: (0,) * n)


def _moba_blocks_padded(seq):
    return -(-(seq // MOBA_BLOCK) // SUBLANES) * SUBLANES


def _prep_call(x, tr, anorm, win, wuq, wukvk, wukvv, gpair, g64, g32, expand, tril, rep, vsel, par):
    bsz, seq, d = x.shape
    tm = MOBA_BLOCK
    assert seq % tm == 0
    nbp = _moba_blocks_padded(seq)
    bf = jnp.bfloat16
    f32 = jnp.float32

    vt = -N_HEADS * V_ROWS
    widths = [(2 * MXU_WIDTH, bf), (2 * MXU_WIDTH, bf), (vt, bf),
              (GROUP_WIDTH, bf), (GROUP_WIDTH, bf), (vt, bf),
              (N_HEADS * LANES, f32), (-SUBLANES, f32),
              (GROUP_WIDTH, bf), (GROUP_WIDTH, bf), (vt, bf), (-N_HEADS * nbp, f32),
              (GROUP_WIDTH, bf), (GROUP_WIDTH, bf), (vt, bf),
              (GROUP_WIDTH, bf)]

    def tok(width):
        if width > 0:
            return pl.BlockSpec((1, tm, width), lambda b, j: (b, j, 0))
        return pl.BlockSpec((1, -width, tm), lambda b, j: (b, 0, j))

    def shape(width):
        return (bsz, seq, width) if width > 0 else (bsz, -width, seq)

    consts = [anorm, win, wuq, wukvk, wukvv, gpair, g64, g32, expand, tril, rep, vsel, par]
    return pl.pallas_call(
        _prep_body,
        grid=(bsz, seq // tm),
        in_specs=[tok(d), tok(TR_WIDTH)] + [_full_spec(c.shape) for c in consts],
        out_specs=[tok(w) for w, _ in widths],
        out_shape=[jax.ShapeDtypeStruct(shape(w), dt) for w, dt in widths],
        scratch_shapes=[pltpu.VMEM((nbp, GROUP_WIDTH), jnp.float32), pltpu.VMEM((1, LANES), jnp.float32)],
        compiler_params=pltpu.CompilerParams(dimension_semantics=("arbitrary", "arbitrary"),
                                             vmem_limit_bytes=VMEM_LIMIT_BYTES),
        name="prep",
    )(x, tr, *consts)


def _memkv_body(mem_ref, mnorm_ref, w_ref, g64_ref, gain_ref, vsel_ref, k_ref, v_ref):
    m = mem_ref[0]
    mn = m * lax.rsqrt(jnp.mean(m * m, axis=-1, keepdims=True) + EPS) * mnorm_ref[...]
    kv = _dot(mn.astype(jnp.bfloat16), w_ref[...])
    k = kv[:, 0:GROUP_WIDTH]
    k = k * lax.rsqrt(_group_mean_sq(k, g64_ref[...]) + EPS) * gain_ref[...]
    k_ref[0] = k.astype(jnp.bfloat16)
    v_ref[0] = _values_t(kv[:, GROUP_WIDTH:2 * GROUP_WIDTH].astype(jnp.bfloat16), vsel_ref[...])


def _memkv_call(mem, mnorm, w, g64, gain, vsel):
    bsz, mlen, d = mem.shape
    k_shape, vt_shape = (bsz, mlen, GROUP_WIDTH), (bsz, N_HEADS * V_ROWS, mlen)
    return pl.pallas_call(
        _memkv_body,
        grid=(bsz,),
        in_specs=[pl.BlockSpec((1, mlen, d), lambda b: (b, 0, 0)), _full_spec(mnorm.shape), _full_spec(w.shape),
                  _full_spec(g64.shape), _full_spec(gain.shape), _full_spec(vsel.shape)],
        out_specs=[pl.BlockSpec((1,) + s[1:], lambda b: (b, 0, 0)) for s in (k_shape, vt_shape)],
        out_shape=[jax.ShapeDtypeStruct(s, jnp.bfloat16) for s in (k_shape, vt_shape)],
        compiler_params=pltpu.CompilerParams(dimension_semantics=("arbitrary",), vmem_limit_bytes=VMEM_LIMIT_BYTES),
        name="mem_kv",
    )(mem, mnorm, w, g64, gain, vsel)


class _AttnCfg:
    def __init__(self, name, vheads, n_maps, causal, decay=False, select=False, diff=False):
        self.name = name
        self.vheads = vheads
        self.n_maps = n_maps
        self.causal = causal
        self.decay = decay
        self.select = select
        self.diff = diff


_PLAIN_VHEADS = [(0, h * HEAD_DIM, (h + 1) * HEAD_DIM, 0, h) for h in range(N_HEADS)]
_CFG_MLA = _AttnCfg("attn_mla", [((h // 2) * MXU_WIDTH, (h % 2) * PAIR_STRIDE, (h % 2) * PAIR_STRIDE + MLA_QK, 0, h)
                                 for h in range(N_HEADS)], 1, True)
_CFG_FOX = _AttnCfg("attn_fox", _PLAIN_VHEADS, 1, True, decay=True)
_CFG_MOBA = _AttnCfg("attn_moba", _PLAIN_VHEADS, 1, True, select=True)
_CFG_DIFF = _AttnCfg("attn_diff", [(0, h * HEAD_DIM + c * DIFF_QK, h * HEAD_DIM + (c + 1) * DIFF_QK, c, h)
                                   for c in range(2) for h in range(N_HEADS)], 2, True, diff=True)
_CFG_MEM = _AttnCfg("attn_mem", _PLAIN_VHEADS, 1, False)


ONES_ROW = HEAD_DIM
V_ROWS = HEAD_DIM + 16
QK_LOOKAHEAD = 8


def _np_value_select():
    sel = np.zeros((N_HEADS * V_ROWS, GROUP_WIDTH), np.float32)
    for h in range(N_HEADS):
        for d in range(HEAD_DIM):
            sel[h * V_ROWS + d, h * HEAD_DIM + d] = 1.0
    return sel


def _values_t(v, vsel):
    vt = _dot_nt(vsel, v)
    row = lax.broadcasted_iota(jnp.int32, vt.shape, 0)
    ones = row == ONES_ROW
    for h in range(1, N_HEADS):
        ones = ones | (row == h * V_ROWS + ONES_ROW)
    return jnp.where(ones, 1.0, vt).astype(jnp.bfloat16)


def _tile_lanes(x, width):
    return jnp.tile(x, (1, width // LANES)) if width != LANES else x


def _attn_body(cfg, qi_ref, kj_ref, *refs):
    refs = list(refs)
    q_ref, k_ref, vt_ref = refs[:3]
    pos = 3
    if cfg.decay:
        dq_ref, dk_ref = refs[pos:pos + 2]
        pos += 2
    if cfg.select:
        sel_ref = refs[pos]
        pos += 1
    if cfg.diff:
        g64_ref, gsub_ref, lam_ref = refs[pos:pos + 3]
        pos += 3
    o_ref, qm_s, m_s, acc_s = refs[pos:pos + 4]

    t = pl.program_id(1)
    i = qi_ref[t]
    j = kj_ref[t]
    tq = q_ref.shape[1]
    tk = k_ref.shape[1]

    @pl.when(j == 0)
    def _():
        for n, (off, lo, hi, _, _) in enumerate(cfg.vheads):
            qb = q_ref[0, :, off:off + MXU_WIDTH]
            qm_s[n] = jnp.where(_lane_mask(qb.shape, lo, hi), qb, jnp.zeros_like(qb))
        m_s[...] = jnp.full(m_s.shape, NEG_INF, jnp.float32)
        acc_s[...] = jnp.zeros_like(acc_s)

    def step(diag):
        if diag:
            key = lax.broadcasted_iota(jnp.int32, (tk, tq), 0)
            qry = lax.broadcasted_iota(jnp.int32, (tk, tq), 1)
            keep = key <= qry
        def scores(n):
            off = cfg.vheads[n][0]
            return _dot_nt(k_ref[0, :, off:off + MXU_WIDTH], qm_s[n])

        raw = {n: scores(n) for n in range(min(QK_LOOKAHEAD, len(cfg.vheads)))}
        for n, (off, lo, hi, c, h) in enumerate(cfg.vheads):
            s = raw.pop(n)
            if cfg.decay:
                s = (dq_ref[0, h:h + 1, :] - _tile_lanes(dk_ref[0, :, h * LANES:(h + 1) * LANES], tq)) + s
            if cfg.select:
                nbp = sel_ref.shape[1] // N_HEADS
                qpos = lax.broadcasted_iota(jnp.int32, (1, tq), 1)
                parts = []
                for kb in range(tk // MOBA_BLOCK):
                    rows = s[kb * MOBA_BLOCK:(kb + 1) * MOBA_BLOCK, :]
                    if not (diag and kb == tk // MOBA_BLOCK - 1):
                        bias = sel_ref[0, pl.ds(h * nbp + j * (tk // MOBA_BLOCK) + kb, 1), :]
                        if diag:
                            bias = jnp.where(qpos < (kb + 1) * MOBA_BLOCK, 0.0, bias)
                        rows = rows + bias
                    parts.append(rows)
                s = parts[0] if len(parts) == 1 else jnp.concatenate(parts, axis=0)
            if diag:
                s = jnp.where(keep, s, NEG_INF)
            m_prev = m_s[n]
            m_new = jnp.maximum(m_prev, jnp.max(s, axis=0, keepdims=True))
            alpha = jnp.exp2(m_prev - m_new)
            p = jnp.exp2(s - m_new)
            m_s[n] = m_new
            acc_s[n] = acc_s[n] * alpha + _dot(vt_ref[0, h * V_ROWS:(h + 1) * V_ROWS, :], p.astype(jnp.bfloat16))
            if n + QK_LOOKAHEAD < len(cfg.vheads):
                raw[n + QK_LOOKAHEAD] = scores(n + QK_LOOKAHEAD)

    if cfg.causal:
        pl.when(j < i)(functools.partial(step, False))
        pl.when(j == i)(functools.partial(step, True))
        last = j == i
    else:
        step(False)
        last = j == 0

    @pl.when(last)
    def _():
        outs = []
        for c in range(cfg.n_maps):
            heads = []
            for h in range(N_HEADS):
                acc = acc_s[c * N_HEADS + h]
                heads.append(acc[0:HEAD_DIM, :] / acc[ONES_ROW:ONES_ROW + 1, :])
            outs.append(jnp.concatenate(heads, axis=0).T)
        if cfg.diff:
            o = outs[0] - lam_ref[0:1, :] * outs[1]
            o = o * lax.rsqrt(_group_mean_sq(o, g64_ref[...]) + EPS) * gsub_ref[...]
        else:
            o = outs[0]
        o_ref[0] = o.astype(o_ref.dtype)


def _attn_call(cfg, q, k, v, extras, tq, tk):
    bsz, seq, wq = q.shape
    sk = k.shape[1]
    nq = seq // tq
    assert seq % tq == 0 and sk % tk == 0
    if cfg.causal:
        assert tq == tk and sk == seq
        pairs = [(i, j) for i in range(nq) for j in range(i + 1)]
    else:
        assert sk == tk
        pairs = [(i, 0) for i in range(nq)]
    qi = jnp.asarray(np.array([p[0] for p in pairs], np.int32))
    kj = jnp.asarray(np.array([p[1] for p in pairs], np.int32))
    n_vh = len(cfg.vheads)

    in_specs = [pl.BlockSpec((1, tq, wq), lambda b, t, qi, kj: (b, qi[t], 0)),
                pl.BlockSpec((1, tk, wq), lambda b, t, qi, kj: (b, kj[t], 0)),
                pl.BlockSpec((1, N_HEADS * V_ROWS, tk), lambda b, t, qi, kj: (b, 0, kj[t]))]
    args = [q, k, v]
    if cfg.decay:
        dcol, drow = extras
        in_specs += [pl.BlockSpec((1, SUBLANES, tq), lambda b, t, qi, kj: (b, 0, qi[t])),
                     pl.BlockSpec((1, tk, N_HEADS * LANES), lambda b, t, qi, kj: (b, kj[t], 0))]
        args += [drow, dcol]
    if cfg.select:
        (sel,) = extras
        in_specs += [pl.BlockSpec((1, sel.shape[1], tq), lambda b, t, qi, kj: (b, 0, qi[t]))]
        args += [sel]
    if cfg.diff:
        for c in extras:
            in_specs += [pl.BlockSpec(c.shape, lambda b, t, qi, kj: (0, 0))]
        args += list(extras)

    grid_spec = pltpu.PrefetchScalarGridSpec(
        num_scalar_prefetch=2,
        grid=(bsz, len(pairs)),
        in_specs=in_specs,
        out_specs=pl.BlockSpec((1, tq, GROUP_WIDTH), lambda b, t, qi, kj: (b, qi[t], 0)),
        scratch_shapes=[pltpu.VMEM((n_vh, tq, MXU_WIDTH), jnp.bfloat16),
                        pltpu.VMEM((n_vh, 1, tq), jnp.float32),
                        pltpu.VMEM((n_vh, V_ROWS, tq), jnp.float32)])
    return pl.pallas_call(
        functools.partial(_attn_body, cfg),
        grid_spec=grid_spec,
        out_shape=jax.ShapeDtypeStruct((bsz, seq, GROUP_WIDTH), jnp.bfloat16),
        compiler_params=pltpu.CompilerParams(dimension_semantics=("arbitrary", "arbitrary"),
                                             vmem_limit_bytes=VMEM_LIMIT_BYTES),
        name=cfg.name,
    )(qi, kj, *args)


def _ffn_body(x_ref, oa_ref, ob_ref, oc_ref, od_ref, oe_ref, wo_ref, fnorm_ref, wg_ref, wu_ref, cw_ref, cb_ref, wd_ref,
              out_ref, xnew_s, xn_s, tail_s, acc_s):
    i = pl.program_id(1)
    f = pl.program_id(2)
    tm = x_ref.shape[1]

    @pl.when(f == 0)
    def _():
        @pl.when(i == 0)
        def _():
            tail_s[...] = jnp.zeros_like(tail_s)

        @pl.when(i > 0)
        def _():
            tail_s[...] = xn_s[tm - TAIL_ROWS:tm, :]

        xnew = x_ref[0]
        for g, o_ref in enumerate((oa_ref, ob_ref, oc_ref, od_ref, oe_ref)):
            xnew = xnew + _dot(o_ref[0], wo_ref[g])
        xnew_s[...] = xnew
        xn = xnew * lax.rsqrt(jnp.mean(xnew * xnew, axis=-1, keepdims=True) + EPS) * fnorm_ref[...]
        xn_s[...] = xn.astype(xn_s.dtype)
        acc_s[...] = jnp.zeros_like(acc_s)

    wg = wg_ref[...]
    g0 = _dot(xn_s[...], wg)
    gt = _dot(tail_s[...], wg)
    u = _dot(xn_s[...], wu_ref[...])
    t1 = gt[TAIL_ROWS - 1:TAIL_ROWS, :]
    t2 = gt[TAIL_ROWS - 2:TAIL_ROWS - 1, :]
    row = lax.broadcasted_iota(jnp.int32, g0.shape, 0)
    g1 = jnp.where(row == 0, t1, pltpu.roll(g0, 1, 0))
    g2 = jnp.where(row == 0, t2, jnp.where(row == 1, t1, pltpu.roll(g0, 2, 0)))
    y = cb_ref[...] + cw_ref[0:1, :] * g2
    y = y + cw_ref[1:2, :] * g1
    y = y + cw_ref[2:3, :] * g0
    hmid = (y * (1.0 / (1.0 + jnp.exp(-y)))) * u
    acc_s[...] += _dot(hmid.astype(jnp.bfloat16), wd_ref[...])

    @pl.when(f == pl.num_programs(2) - 1)
    def _():
        out_ref[0] = xnew_s[...] + acc_s[...]


def _ffn_call(x, outs, wo, fnorm, wg, wu, cw, cb, wd, tm, tf):
    bsz, seq, d = x.shape
    dff = wg.shape[1]
    assert seq % tm == 0 and dff % tf == 0
    tok = lambda w: pl.BlockSpec((1, tm, w), lambda b, i, f: (b, i, 0))
    return pl.pallas_call(
        _ffn_body,
        grid=(bsz, seq // tm, dff // tf),
        in_specs=[tok(d)] + [tok(GROUP_WIDTH)] * 5 + [
            pl.BlockSpec(wo.shape, lambda b, i, f: (0, 0, 0)),
            pl.BlockSpec(fnorm.shape, lambda b, i, f: (0, 0)),
            pl.BlockSpec((d, tf), lambda b, i, f: (0, f)),
            pl.BlockSpec((d, tf), lambda b, i, f: (0, f)),
            pl.BlockSpec((SUBLANES, tf), lambda b, i, f: (0, f)),
            pl.BlockSpec((1, tf), lambda b, i, f: (0, f)),
            pl.BlockSpec((tf, d), lambda b, i, f: (f, 0))],
        out_specs=tok(d),
        out_shape=jax.ShapeDtypeStruct((bsz, seq, d), jnp.float32),
        scratch_shapes=[pltpu.VMEM((tm, d), jnp.float32), pltpu.VMEM((tm, d), jnp.bfloat16),
                        pltpu.VMEM((TAIL_ROWS, d), jnp.bfloat16), pltpu.VMEM((tm, d), jnp.float32)],
        compiler_params=pltpu.CompilerParams(dimension_semantics=("arbitrary", "arbitrary", "arbitrary"),
                                             vmem_limit_bytes=VMEM_LIMIT_BYTES),
        name="ffn",
    )(x, *outs, wo, fnorm, wg, wu, cw, cb, wd)


def _pad_row(v, width=MXU_WIDTH):
    v = v.astype(jnp.float32).reshape(-1)
    return jnp.pad(v, (0, width - v.shape[0]))


def _pair_gain(g):
    one = jnp.concatenate([g.astype(jnp.float32), g.astype(jnp.float32),
                           jnp.zeros((MXU_WIDTH - 2 * MLA_QK,), jnp.float32)])
    return one


def _rope_table(positions):
    pos = positions.astype(jnp.float32)[:, :, None]
    cols_c, cols_s = [], []
    for rot in (MLA_ROPE, ROT_MOBA, ROT_DIFF):
        inv = ROPE_THETA ** (-jnp.arange(0, rot, 2, dtype=jnp.float32) / rot)
        ang = pos * inv
        cols_c.append(jnp.cos(ang))
        cols_s.append(jnp.sin(ang))
    c = jnp.concatenate(cols_c, axis=-1)
    s = jnp.concatenate(cols_s, axis=-1)
    c_hi = c.astype(jnp.bfloat16)
    c_lo = (c - c_hi.astype(jnp.float32)).astype(jnp.bfloat16)
    s_hi = s.astype(jnp.bfloat16)
    s_lo = (s - s_hi.astype(jnp.float32)).astype(jnp.bfloat16)
    one = jnp.ones(pos.shape[:2] + (1,), jnp.bfloat16)
    pad = jnp.zeros(pos.shape[:2] + (TR_WIDTH - TR_ONE - 1,), jnp.bfloat16)
    return jnp.concatenate([c_hi, c_lo, s_hi, s_lo, one, pad], axis=-1)


def _pick_tile(n, pref):
    t = pref
    while n % t:
        t //= 2
    return t


def kernel(x, mem, positions, attn_norm, ffn_norm, mem_norm, w_in, mla_cq_norm, mla_ckv_norm, mla_w_uq, mla_w_ukv, mla_q_norm, mla_k_norm, fox_b_f, fox_q_norm, fox_k_norm, moba_q_norm, moba_k_norm, diff_lambda, diff_q_norm, diff_k_norm, diff_sub_norm, mem_w_kv, mem_q_norm, mem_k_norm, w_o, ffn_w_gate, ffn_w_up, ffn_conv_w, ffn_conv_b, ffn_w_down):
    bsz, seq, d = x.shape
    depth = w_in.shape[0]
    dff = ffn_w_gate.shape[2]
    bf = jnp.bfloat16
    f32 = jnp.float32

    in_idx = _np_in_index()
    uq_idx = _np_uq_index()
    ukvk_idx, ukvv_idx = _np_ukv_index()
    gpair = jnp.asarray(_np_group_matrix(_PAIR_GROUPS), bf)
    g64 = jnp.asarray(_np_group_matrix(_G64_GROUPS), bf)
    g32 = jnp.asarray(_np_group_matrix(_G32_GROUPS), bf)
    expand = jnp.asarray(_np_rope_expand_all(), bf)
    tril = jnp.asarray(np.tril(np.ones((MOBA_BLOCK, MOBA_BLOCK), np.float32)), bf)
    rep_np = np.zeros((LANES, N_HEADS * LANES), np.float32)
    for h in range(N_HEADS):
        rep_np[h, h * LANES:(h + 1) * LANES] = 1.0
    rep = jnp.asarray(rep_np, bf)
    vsel = jnp.asarray(_np_value_select(), bf)
    tr = _rope_table(positions)

    t_dense = _pick_tile(seq, 512)
    t_ffn = _pick_tile(seq, 512)
    tf = dff // 2 if (dff // 2) % LANES == 0 else dff

    for l in range(depth):
        win = jnp.concatenate([w_in[l], jnp.zeros((d, 1), f32)], axis=1)[:, in_idx].astype(bf)
        wuq = jnp.concatenate([mla_w_uq[l], jnp.zeros((MLA_Q_RANK, 1), f32)], axis=1)[:, uq_idx]
        wuq = jnp.pad(wuq, ((0, MXU_WIDTH - MLA_Q_RANK), (0, 0))).astype(bf)
        wukv = jnp.concatenate([mla_w_ukv[l], jnp.zeros((MLA_KV_RANK, 1), f32)], axis=1)
        wukvk = wukv[:, ukvk_idx].astype(bf)
        wukvv = wukv[:, ukvv_idx].astype(bf)
        rows = [None] * P_ROWS
        rows[P_CQ] = _pad_row(mla_cq_norm[l])
        rows[P_CKV] = _pad_row(mla_ckv_norm[l])
        rows[P_GQ] = _pair_gain(mla_q_norm[l]) * (LOG2E * MLA_QK ** -0.5)
        rows[P_GK] = _pair_gain(mla_k_norm[l])
        rows[P_FQ] = jnp.tile(fox_q_norm[l].astype(f32), N_HEADS) * (LOG2E * HEAD_DIM ** -0.5)
        rows[P_FK] = jnp.tile(fox_k_norm[l].astype(f32), N_HEADS)
        rows[P_FB] = _pad_row(fox_b_f[l])
        rows[P_MQ] = jnp.tile(moba_q_norm[l].astype(f32), N_HEADS)
        rows[P_MK] = jnp.tile(moba_k_norm[l].astype(f32), N_HEADS)
        rows[P_DQ] = jnp.tile(diff_q_norm[l].astype(f32), 2 * N_HEADS) * (LOG2E * DIFF_QK ** -0.5)
        rows[P_DK] = jnp.tile(diff_k_norm[l].astype(f32), 2 * N_HEADS)
        rows[P_EQ] = jnp.tile(mem_q_norm[l].astype(f32), N_HEADS) * (LOG2E * HEAD_DIM ** -0.5)
        par = jnp.stack([r if r is not None else jnp.zeros((MXU_WIDTH,), f32) for r in rows])

        (aq, ak, av, fq, fk, fv, fdcol, fdrow, mq, mk, mv, msel, dq, dk, dv, eq) = _prep_call(
            x, tr, attn_norm[l].reshape(1, d).astype(f32), win, wuq, wukvk, wukvv, gpair, g64, g32, expand, tril,
            rep, vsel, par)
        ek, ev = _memkv_call(mem, mem_norm[l].reshape(1, d).astype(f32), mem_w_kv[l].astype(bf), g64,
                             jnp.tile(mem_k_norm[l].astype(f32), N_HEADS).reshape(1, GROUP_WIDTH), vsel)

        lam_vec = diff_lambda[l].astype(f32)
        lam_init = 0.8 - 0.6 * math.exp(-0.3 * l)
        lam = (jnp.exp(jnp.sum(lam_vec[0] * lam_vec[1])) - jnp.exp(jnp.sum(lam_vec[2] * lam_vec[3])) + lam_init)
        lam_row = jnp.full((1, GROUP_WIDTH), 1.0, f32) * lam
        gsub = (jnp.tile(diff_sub_norm[l].astype(f32), N_HEADS) * (1.0 - lam_init)).reshape(1, GROUP_WIDTH)

        o_a = _attn_call(_CFG_MLA, aq, ak, av, (), t_dense, t_dense)
        o_b = _attn_call(_CFG_FOX, fq, fk, fv, (fdcol, fdrow), t_dense, t_dense)
        t_moba = max(t_dense, MOBA_BLOCK)
        o_c = _attn_call(_CFG_MOBA, mq, mk, mv, (msel,), t_moba, t_moba)
        o_d = _attn_call(_CFG_DIFF, dq, dk, dv, (g64, gsub, lam_row), t_dense, t_dense)
        o_e = _attn_call(_CFG_MEM, eq, ek, ev, (), t_dense, mem.shape[1])

        cw = jnp.pad(ffn_conv_w[l].astype(f32), ((0, SUBLANES - CONV_WIDTH), (0, 0)))
        x = _ffn_call(x, (o_a, o_b, o_c, o_d, o_e), w_o[l].reshape(5, GROUP_WIDTH, d).astype(bf),
                      ffn_norm[l].reshape(1, d).astype(f32), ffn_w_gate[l].astype(bf), ffn_w_up[l].astype(bf),
                      cw, ffn_conv_b[l].reshape(1, dff).astype(f32), ffn_w_down[l].astype(bf), t_ffn, tf)
    return x
```

```python
import functools
import math

import numpy as np
import jax
import jax.numpy as jnp
from jax import lax
from jax.experimental import pallas as pl
from jax.experimental.pallas import tpu as pltpu

N_HEADS = 4
HEAD_DIM = 64
GROUP_WIDTH = N_HEADS * HEAD_DIM
MLA_Q_RANK = 192
MLA_KV_RANK = 128
MLA_NOPE = 64
MLA_ROPE = 32
MLA_QK = MLA_NOPE + MLA_ROPE
DIFF_QK = HEAD_DIM // 2
ROPE_THETA = 500000.0
ROT_MOBA = HEAD_DIM // 4
ROT_DIFF = DIFF_QK // 4
MOBA_BLOCK = 256
MOBA_TOPK = 3
CONV_WIDTH = 3
EPS = 1e-6
NEG_INF = -1e30
LOG2E = math.log2(math.e)
REMOVED = -3e38

LANES = 128
SUBLANES = 8
MXU_WIDTH = 256
TAIL_ROWS = 16
VMEM_LIMIT_BYTES = 56 * 1024 * 1024

_SRC_CQ = 0
_SRC_CKV = _SRC_CQ + MLA_Q_RANK
_SRC_KR = _SRC_CKV + MLA_KV_RANK
_SRC_FOX = _SRC_KR + MLA_ROPE
_SRC_FOXF = _SRC_FOX + 3 * GROUP_WIDTH
_SRC_MOBA = _SRC_FOXF + N_HEADS
_SRC_DIFF = _SRC_MOBA + 3 * GROUP_WIDTH
_SRC_MEMQ = _SRC_DIFF + 3 * GROUP_WIDTH
_SRC_END = _SRC_MEMQ + GROUP_WIDTH

PK_CQ = 0
PK_CKV = 256
PK_KR = 384
PK_FQ, PK_FK, PK_FV = 896, 1152, 1408
PK_FF = 1664
PK_MQ, PK_MK, PK_MV = 1792, 2048, 2304
PK_DQ, PK_DK, PK_DV = 2560, 2816, 3072
PK_EQ = 3328
PK_END = 3584

PAIR_STRIDE = MLA_QK


def _pair_lane(h, d):
    return (h // 2) * MXU_WIDTH + (h % 2) * PAIR_STRIDE + d


N_FREQ = MLA_ROPE // 2 + ROT_MOBA // 2 + ROT_DIFF // 2
FREQ_BASE_MLA = 0
FREQ_BASE_MOBA = MLA_ROPE // 2
FREQ_BASE_DIFF = FREQ_BASE_MOBA + ROT_MOBA // 2
TR_ONE = 4 * N_FREQ
TR_WIDTH = LANES

(P_CQ, P_CKV, P_GQ, P_GK, P_FQ, P_FK, P_FB, P_MQ, P_MK, P_DQ, P_DK, P_EQ) = range(12)
P_ROWS = 16


def _np_in_index():
    idx = np.full((PK_END,), _SRC_END, np.int32)
    idx[PK_CQ:PK_CQ + MLA_Q_RANK] = np.arange(_SRC_CQ, _SRC_CQ + MLA_Q_RANK)
    idx[PK_CKV:PK_CKV + MLA_KV_RANK] = np.arange(_SRC_CKV, _SRC_CKV + MLA_KV_RANK)
    for h in range(N_HEADS):
        for d in range(MLA_ROPE):
            idx[PK_KR + _pair_lane(h, d)] = _SRC_KR + d
    idx[PK_FQ:PK_FQ + 3 * GROUP_WIDTH] = np.arange(_SRC_FOX, _SRC_FOX + 3 * GROUP_WIDTH)
    idx[PK_FF:PK_FF + N_HEADS] = np.arange(_SRC_FOXF, _SRC_FOXF + N_HEADS)
    idx[PK_MQ:PK_MQ + 3 * GROUP_WIDTH] = np.arange(_SRC_MOBA, _SRC_MOBA + 3 * GROUP_WIDTH)
    idx[PK_DQ:PK_DQ + 3 * GROUP_WIDTH] = np.arange(_SRC_DIFF, _SRC_DIFF + 3 * GROUP_WIDTH)
    idx[PK_EQ:PK_EQ + GROUP_WIDTH] = np.arange(_SRC_MEMQ, _SRC_MEMQ + GROUP_WIDTH)
    return idx


def _np_uq_index():
    idx = np.full((2 * MXU_WIDTH,), N_HEADS * MLA_QK, np.int32)
    for h in range(N_HEADS):
        for d in range(MLA_QK):
            idx[_pair_lane(h, d)] = h * MLA_QK + d
    return idx


def _np_ukv_index():
    zero = N_HEADS * (MLA_NOPE + HEAD_DIM)
    idx_k = np.full((2 * MXU_WIDTH,), zero, np.int32)
    idx_v = np.zeros((GROUP_WIDTH,), np.int32)
    for h in range(N_HEADS):
        for d in range(MLA_NOPE):
            idx_k[_pair_lane(h, MLA_ROPE + d)] = h * (MLA_NOPE + HEAD_DIM) + d
        for d in range(HEAD_DIM):
            idx_v[h * HEAD_DIM + d] = h * (MLA_NOPE + HEAD_DIM) + MLA_NOPE + d
    return idx_k, idx_v


def _np_group_matrix(groups):
    g = np.zeros((MXU_WIDTH, MXU_WIDTH), np.float32)
    for lo, size in groups:
        g[lo:lo + size, lo:lo + size] = 1.0 / size
    return g


_PAIR_GROUPS = [(0, MLA_ROPE), (MLA_ROPE, MLA_NOPE), (PAIR_STRIDE, MLA_ROPE), (PAIR_STRIDE + MLA_ROPE, MLA_NOPE)]
_G64_GROUPS = [(h * HEAD_DIM, HEAD_DIM) for h in range(N_HEADS)]
_G32_GROUPS = [(g * DIFF_QK, DIFF_QK) for g in range(2 * N_HEADS)]


def _np_rope_expand(regions, rot, base):
    half = rot // 2
    e = np.zeros((TR_WIDTH, 3 * MXU_WIDTH), np.float32)
    e[TR_ONE, 0:MXU_WIDTH] = 1.0
    for lo in regions:
        for r in range(half):
            f = base + r
            for lane, tab, sign in ((lo + r, 2, -1.0), (lo + half + r, 1, 1.0)):
                e[TR_ONE, lane] = 0.0
                e[f, lane] = 1.0
                e[N_FREQ + f, lane] = 1.0
                e[2 * N_FREQ + f, tab * MXU_WIDTH + lane] = sign
                e[3 * N_FREQ + f, tab * MXU_WIDTH + lane] = sign
    return e


def _np_rope_expand_all():
    return np.concatenate([
        _np_rope_expand([0, PAIR_STRIDE], MLA_ROPE, FREQ_BASE_MLA),
        _np_rope_expand([h * HEAD_DIM for h in range(N_HEADS)], ROT_MOBA, FREQ_BASE_MOBA),
        _np_rope_expand([g * DIFF_QK for g in range(2 * N_HEADS)], ROT_DIFF, FREQ_BASE_DIFF),
    ], axis=1)


def _dot(a, b):
    return jnp.dot(a, b, preferred_element_type=jnp.float32)


def _dot_nt(a, b):
    return lax.dot_general(a, b, (((1,), (1,)), ((), ())), preferred_element_type=jnp.float32)


def _split2(a):
    hi = a.astype(jnp.bfloat16)
    lo = (a - hi.astype(jnp.float32)).astype(jnp.bfloat16)
    return hi, lo


def _split3(a):
    hi = a.astype(jnp.bfloat16)
    r = a - hi.astype(jnp.float32)
    mid = r.astype(jnp.bfloat16)
    lo = (r - mid.astype(jnp.float32)).astype(jnp.bfloat16)
    return hi, mid, lo


def _group_mean_sq(a, g_bf16):
    return _dot((a * a).astype(jnp.bfloat16), g_bf16)


def _rope(x, tabs, half):
    w = x.shape[-1]
    c, s1, s2 = tabs[:, 0:w], tabs[:, w:2 * w], tabs[:, 2 * w:3 * w]
    return x * c + pltpu.roll(x, half, 1) * s1 + pltpu.roll(x, w - half, 1) * s2


def _lane_mask(shape, lo, hi):
    lane = lax.broadcasted_iota(jnp.int32, shape, len(shape) - 1)
    return (lane >= lo) & (lane < hi)


def _prep_body(x_ref, tr_ref, anorm_ref, win_ref, wuq_ref, wukvk_ref, wukvv_ref, gpair_ref, g64_ref, g32_ref,
               exp_ref, tril_ref, rep_ref, vsel_ref, par_ref,
               aq_ref, ak_ref, av_ref, fq_ref, fk_ref, fv_ref, fdcol_ref, fdrow_ref, mq_ref, mk_ref, mv_ref, msel_ref,
               dq_ref, dk_ref, dv_ref, eq_ref,
               kmean_s, carry_s):
    j = pl.program_id(1)
    tm = x_ref.shape[1]
    bf = jnp.bfloat16

    @pl.when(j == 0)
    def _():
        kmean_s[...] = jnp.zeros_like(kmean_s)
        carry_s[...] = jnp.zeros_like(carry_s)

    x = x_ref[0]
    xn = x * lax.rsqrt(jnp.mean(x * x, axis=-1, keepdims=True) + EPS) * anorm_ref[...]
    xb = xn.astype(bf)

    def proj(off, width):
        return _dot(xb, win_ref[:, off:off + width])

    def prow(r, width=MXU_WIDTH):
        return par_ref[r:r + 1, 0:width]

    gpair, g64, g32 = gpair_ref[...], g64_ref[...], g32_ref[...]
    vsel = vsel_ref[...]

    p_cq = proj(PK_CQ, MXU_WIDTH)
    p_ckv = proj(PK_CKV, MLA_KV_RANK)
    p_ff = proj(PK_FF, LANES)
    direct = {name: proj(off, MXU_WIDTH) for name, off in
              (("fq", PK_FQ), ("fk", PK_FK), ("mq", PK_MQ), ("mk", PK_MK), ("dq", PK_DQ), ("dk", PK_DK),
               ("eq", PK_EQ))}
    p_kr = proj(PK_KR, 2 * MXU_WIDTH)
    tabs = _dot(tr_ref[0], exp_ref[...])
    tab_mla = tabs[:, 0:3 * MXU_WIDTH]
    tab_moba = tabs[:, 3 * MXU_WIDTH:6 * MXU_WIDTH]
    tab_diff = tabs[:, 6 * MXU_WIDTH:9 * MXU_WIDTH]
    values = {name: proj(off, MXU_WIDTH).astype(bf) for name, off in
              (("fv", PK_FV), ("mv", PK_MV), ("dv", PK_DV))}

    cqn = p_cq * lax.rsqrt(jnp.sum(p_cq * p_cq, axis=-1, keepdims=True) * (1.0 / MLA_Q_RANK) + EPS) * prow(P_CQ)
    ckvn = p_ckv * lax.rsqrt(jnp.mean(p_ckv * p_ckv, axis=-1, keepdims=True) + EPS) * prow(P_CKV, MLA_KV_RANK)
    ckvb = ckvn.astype(bf)
    z = p_ff + prow(P_FB, LANES)
    log_f = jnp.minimum(z, 0.0) - jnp.log1p(jnp.exp(-jnp.abs(z)))
    log_f = jnp.where(_lane_mask(log_f.shape, 0, N_HEADS), log_f, 0.0)
    l1, l2, l3 = _split3(log_f)

    qa = _dot(cqn.astype(bf), wuq_ref[...])
    ka = p_kr + _dot(ckvb, wukvk_ref[...])
    values["av"] = _dot(ckvb, wukvv_ref[...]).astype(bf)
    gmat = {"fq": g64, "fk": g64, "mq": g64, "mk": g64, "dq": g32, "dk": g32, "eq": g64}
    ms = {name: _group_mean_sq(a, gmat[name]) for name, a in direct.items()}
    tril = tril_ref[...]
    dec = carry_s[...] + ((_dot(tril, l1) + _dot(tril, l2)) + _dot(tril, l3))
    carry_s[...] = dec[tm - 1:tm, :]
    for name, ref in (("fv", fv_ref), ("mv", mv_ref), ("dv", dv_ref), ("av", av_ref)):
        ref[0] = _values_t(values[name], vsel)

    gains = {"fq": P_FQ, "fk": P_FK, "mq": P_MQ, "mk": P_MK, "dq": P_DQ, "dk": P_DK, "eq": P_EQ}
    normed = {name: a * lax.rsqrt(ms[name] + EPS) * prow(gains[name]) for name, a in direct.items()}
    ms_qa = [_group_mean_sq(qa[:, p * MXU_WIDTH:(p + 1) * MXU_WIDTH], gpair) for p in range(2)]
    ms_ka = [_group_mean_sq(ka[:, p * MXU_WIDTH:(p + 1) * MXU_WIDTH], gpair) for p in range(2)]
    d1, d2, d3 = _split3(dec * LOG2E)
    rep = rep_ref[...]
    fdcol_ref[0] = (_dot(d1, rep) + _dot(d2, rep)) + _dot(d3, rep)
    row_sel = jnp.where(lax.broadcasted_iota(jnp.int32, (SUBLANES, LANES), 0)
                        == lax.broadcasted_iota(jnp.int32, (SUBLANES, LANES), 1), 1.0, 0.0).astype(bf)
    fdrow_ref[0] = (_dot_nt(row_sel, d1) + _dot_nt(row_sel, d2)) + _dot_nt(row_sel, d3)

    fq_ref[0] = normed["fq"].astype(bf)
    fk_ref[0] = normed["fk"].astype(bf)
    eq_ref[0] = normed["eq"].astype(bf)
    dq_ref[0] = _rope(normed["dq"], tab_diff, ROT_DIFF // 2).astype(bf)
    dk_ref[0] = _rope(normed["dk"], tab_diff, ROT_DIFF // 2).astype(bf)
    mq = _rope(normed["mq"], tab_moba, ROT_MOBA // 2)
    mk = _rope(normed["mk"], tab_moba, ROT_MOBA // 2)
    mq_ref[0] = (mq * (LOG2E * HEAD_DIM ** -0.5)).astype(bf)
    mk_ref[0] = mk.astype(bf)
    for p in range(2):
        sl = slice(p * MXU_WIDTH, (p + 1) * MXU_WIDTH)
        aq_ref[0, :, sl] = _rope(qa[:, sl] * lax.rsqrt(ms_qa[p] + EPS) * prow(P_GQ), tab_mla,
                                 MLA_ROPE // 2).astype(bf)
        ak_ref[0, :, sl] = _rope(ka[:, sl] * lax.rsqrt(ms_ka[p] + EPS) * prow(P_GK), tab_mla,
                                 MLA_ROPE // 2).astype(bf)

    kmean_s[pl.ds(j, 1), :] = jnp.mean(mk, axis=0, keepdims=True)
    km_hi, km_lo = _split2(kmean_s[...])
    nbp = kmean_s.shape[0]
    blk = lax.broadcasted_iota(jnp.int32, (nbp, tm), 0)
    past = blk < j
    for h in range(N_HEADS):
        q_hi, q_lo = _split2(jnp.where(_lane_mask(mq.shape, h * HEAD_DIM, (h + 1) * HEAD_DIM), mq, 0.0))
        gate = (_dot_nt(km_hi, q_hi) + _dot_nt(km_lo, q_hi)) + _dot_nt(km_hi, q_lo)
        work = jnp.where(past, gate, NEG_INF)
        sel = jnp.zeros((nbp, tm), jnp.bool_)
        for _ in range(MOBA_TOPK):
            mx = jnp.max(work, axis=0, keepdims=True)
            first = jnp.min(jnp.where(work == mx, blk, nbp), axis=0, keepdims=True)
            pick = blk == first
            sel = sel | pick
            work = jnp.where(pick, REMOVED, work)
        msel_ref[0, h * nbp:(h + 1) * nbp, :] = jnp.where(sel & past, 0.0, NEG_INF)


def _full_spec(shape):
    n = len(shape)
    return pl.BlockSpec(shape, lambda *_: (0,) * n)


def _moba_blocks_padded(seq):
    return -(-(seq // MOBA_BLOCK) // SUBLANES) * SUBLANES


def _prep_call(x, tr, anorm, win, wuq, wukvk, wukvv, gpair, g64, g32, expand, tril, rep, vsel, par):
    bsz, seq, d = x.shape
    tm = MOBA_BLOCK
    assert seq % tm == 0
    nbp = _moba_blocks_padded(seq)
    bf = jnp.bfloat16
    f32 = jnp.float32

    vt = -N_HEADS * V_ROWS
    widths = [(2 * MXU_WIDTH, bf), (2 * MXU_WIDTH, bf), (vt, bf),
              (GROUP_WIDTH, bf), (GROUP_WIDTH, bf), (vt, bf),
              (N_HEADS * LANES, f32), (-SUBLANES, f32),
              (GROUP_WIDTH, bf), (GROUP_WIDTH, bf), (vt, bf), (-N_HEADS * nbp, f32),
              (GROUP_WIDTH, bf), (GROUP_WIDTH, bf), (vt, bf),
              (GROUP_WIDTH, bf)]

    def tok(width):
        if width > 0:
            return pl.BlockSpec((1, tm, width), lambda b, j: (b, j, 0))
        return pl.BlockSpec((1, -width, tm), lambda b, j: (b, 0, j))

    def shape(width):
        return (bsz, seq, width) if width > 0 else (bsz, -width, seq)

    consts = [anorm, win, wuq, wukvk, wukvv, gpair, g64, g32, expand, tril, rep, vsel, par]
    return pl.pallas_call(
        _prep_body,
        grid=(bsz, seq // tm),
        in_specs=[tok(d), tok(TR_WIDTH)] + [_full_spec(c.shape) for c in consts],
        out_specs=[tok(w) for w, _ in widths],
        out_shape=[jax.ShapeDtypeStruct(shape(w), dt) for w, dt in widths],
        scratch_shapes=[pltpu.VMEM((nbp, GROUP_WIDTH), jnp.float32), pltpu.VMEM((1, LANES), jnp.float32)],
        compiler_params=pltpu.CompilerParams(dimension_semantics=("arbitrary", "arbitrary"),
                                             vmem_limit_bytes=VMEM_LIMIT_BYTES),
        name="prep",
    )(x, tr, *consts)


def _memkv_body(mem_ref, mnorm_ref, w_ref, g64_ref, gain_ref, vsel_ref, k_ref, v_ref):
    m = mem_ref[0]
    mn = m * lax.rsqrt(jnp.mean(m * m, axis=-1, keepdims=True) + EPS) * mnorm_ref[...]
    kv = _dot(mn.astype(jnp.bfloat16), w_ref[...])
    k = kv[:, 0:GROUP_WIDTH]
    k = k * lax.rsqrt(_group_mean_sq(k, g64_ref[...]) + EPS) * gain_ref[...]
    k_ref[0] = k.astype(jnp.bfloat16)
    v_ref[0] = _values_t(kv[:, GROUP_WIDTH:2 * GROUP_WIDTH].astype(jnp.bfloat16), vsel_ref[...])


def _memkv_call(mem, mnorm, w, g64, gain, vsel):
    bsz, mlen, d = mem.shape
    k_shape, vt_shape = (bsz, mlen, GROUP_WIDTH), (bsz, N_HEADS * V_ROWS, mlen)
    return pl.pallas_call(
        _memkv_body,
        grid=(bsz,),
        in_specs=[pl.BlockSpec((1, mlen, d), lambda b: (b, 0, 0)), _full_spec(mnorm.shape), _full_spec(w.shape),
                  _full_spec(g64.shape), _full_spec(gain.shape), _full_spec(vsel.shape)],
        out_specs=[pl.BlockSpec((1,) + s[1:], lambda b: (b, 0, 0)) for s in (k_shape, vt_shape)],
        out_shape=[jax.ShapeDtypeStruct(s, jnp.bfloat16) for s in (k_shape, vt_shape)],
        compiler_params=pltpu.CompilerParams(dimension_semantics=("arbitrary",), vmem_limit_bytes=VMEM_LIMIT_BYTES),
        name="mem_kv",
    )(mem, mnorm, w, g64, gain, vsel)


class _AttnCfg:
    def __init__(self, name, vheads, n_maps, causal, decay=False, select=False, diff=False):
        self.name = name
        self.vheads = vheads
        self.n_maps = n_maps
        self.causal = causal
        self.decay = decay
        self.select = select
        self.diff = diff


_PLAIN_VHEADS = [(0, h * HEAD_DIM, (h + 1) * HEAD_DIM, 0, h) for h in range(N_HEADS)]
_CFG_MLA = _AttnCfg("attn_mla", [((h // 2) * MXU_WIDTH, (h % 2) * PAIR_STRIDE, (h % 2) * PAIR_STRIDE + MLA_QK, 0, h)
                                 for h in range(N_HEADS)], 1, True)
_CFG_FOX = _AttnCfg("attn_fox", _PLAIN_VHEADS, 1, True, decay=True)
_CFG_MOBA = _AttnCfg("attn_moba", _PLAIN_VHEADS, 1, True, select=True)
_CFG_DIFF = _AttnCfg("attn_diff", [(0, h * HEAD_DIM + c * DIFF_QK, h * HEAD_DIM + (c + 1) * DIFF_QK, c, h)
                                   for c in range(2) for h in range(N_HEADS)], 2, True, diff=True)
_CFG_MEM = _AttnCfg("attn_mem", _PLAIN_VHEADS, 1, False)


ONES_ROW = HEAD_DIM
V_ROWS = HEAD_DIM + 16
QK_LOOKAHEAD = 4


def _np_value_select():
    sel = np.zeros((N_HEADS * V_ROWS, GROUP_WIDTH), np.float32)
    for h in range(N_HEADS):
        for d in range(HEAD_DIM):
            sel[h * V_ROWS + d, h * HEAD_DIM + d] = 1.0
    return sel


def _values_t(v, vsel):
    vt = _dot_nt(vsel, v)
    row = lax.broadcasted_iota(jnp.int32, vt.shape, 0)
    ones = row == ONES_ROW
    for h in range(1, N_HEADS):
        ones = ones | (row == h * V_ROWS + ONES_ROW)
    return jnp.where(ones, 1.0, vt).astype(jnp.bfloat16)


def _tile_lanes(x, width):
    return jnp.tile(x, (1, width // LANES)) if width != LANES else x


def _attn_body(cfg, qi_ref, kj_ref, *refs):
    refs = list(refs)
    q_ref, k_ref, vt_ref = refs[:3]
    pos = 3
    if cfg.decay:
        dq_ref, dk_ref = refs[pos:pos + 2]
        pos += 2
    if cfg.select:
        sel_ref = refs[pos]
        pos += 1
    if cfg.diff:
        g64_ref, gsub_ref, lam_ref = refs[pos:pos + 3]
        pos += 3
    o_ref, qm_s, m_s, acc_s = refs[pos:pos + 4]

    t = pl.program_id(1)
    i = qi_ref[t]
    j = kj_ref[t]
    tq = q_ref.shape[1]
    tk = k_ref.shape[1]

    @pl.when(j == 0)
    def _():
        for n, (off, lo, hi, _, _) in enumerate(cfg.vheads):
            qb = q_ref[0, :, off:off + MXU_WIDTH]
            qm_s[n] = jnp.where(_lane_mask(qb.shape, lo, hi), qb, jnp.zeros_like(qb))
        m_s[...] = jnp.full(m_s.shape, NEG_INF, jnp.float32)
        acc_s[...] = jnp.zeros_like(acc_s)

    def step(diag):
        if diag:
            key = lax.broadcasted_iota(jnp.int32, (tk, tq), 0)
            qry = lax.broadcasted_iota(jnp.int32, (tk, tq), 1)
            keep = key <= qry
        def scores(n):
            off = cfg.vheads[n][0]
            return _dot_nt(k_ref[0, :, off:off + MXU_WIDTH], qm_s[n])

        raw = {n: scores(n) for n in range(min(QK_LOOKAHEAD, len(cfg.vheads)))}
        for n, (off, lo, hi, c, h) in enumerate(cfg.vheads):
            s = raw.pop(n)
            if cfg.decay:
                s = (dq_ref[0, h:h + 1, :] - _tile_lanes(dk_ref[0, :, h * LANES:(h + 1) * LANES], tq)) + s
            if cfg.select:
                nbp = sel_ref.shape[1] // N_HEADS
                qpos = lax.broadcasted_iota(jnp.int32, (1, tq), 1)
                parts = []
                for kb in range(tk // MOBA_BLOCK):
                    rows = s[kb * MOBA_BLOCK:(kb + 1) * MOBA_BLOCK, :]
                    if not (diag and kb == tk // MOBA_BLOCK - 1):
                        bias = sel_ref[0, pl.ds(h * nbp + j * (tk // MOBA_BLOCK) + kb, 1), :]
                        if diag:
                            bias = jnp.where(qpos < (kb + 1) * MOBA_BLOCK, 0.0, bias)
                        rows = rows + bias
                    parts.append(rows)
                s = parts[0] if len(parts) == 1 else jnp.concatenate(parts, axis=0)
            if diag:
                s = jnp.where(keep, s, NEG_INF)
            m_prev = m_s[n]
            m_new = jnp.maximum(m_prev, jnp.max(s, axis=0, keepdims=True))
            alpha = jnp.exp2(m_prev - m_new)
            p = jnp.exp2(s - m_new)
            m_s[n] = m_new
            acc_s[n] = acc_s[n] * alpha + _dot(vt_ref[0, h * V_ROWS:(h + 1) * V_ROWS, :], p.astype(jnp.bfloat16))
            if n + QK_LOOKAHEAD < len(cfg.vheads):
                raw[n + QK_LOOKAHEAD] = scores(n + QK_LOOKAHEAD)

    if cfg.causal:
        pl.when(j < i)(functools.partial(step, False))
        pl.when(j == i)(functools.partial(step, True))
        last = j == i
    else:
        step(False)
        last = j == 0

    @pl.when(last)
    def _():
        outs = []
        for c in range(cfg.n_maps):
            heads = []
            for h in range(N_HEADS):
                acc = acc_s[c * N_HEADS + h]
                heads.append(acc[0:HEAD_DIM, :] / acc[ONES_ROW:ONES_ROW + 1, :])
            outs.append(jnp.concatenate(heads, axis=0).T)
        if cfg.diff:
            o = outs[0] - lam_ref[0:1, :] * outs[1]
            o = o * lax.rsqrt(_group_mean_sq(o, g64_ref[...]) + EPS) * gsub_ref[...]
        else:
            o = outs[0]
        o_ref[0] = o.astype(o_ref.dtype)


def _attn_call(cfg, q, k, v, extras, tq, tk):
    bsz, seq, wq = q.shape
    sk = k.shape[1]
    nq = seq // tq
    assert seq % tq == 0 and sk % tk == 0
    if cfg.causal:
        assert tq == tk and sk == seq
        pairs = [(i, j) for i in range(nq) for j in range(i + 1)]
    else:
        assert sk == tk
        pairs = [(i, 0) for i in range(nq)]
    qi = jnp.asarray(np.array([p[0] for p in pairs], np.int32))
    kj = jnp.asarray(np.array([p[1] for p in pairs], np.int32))
    n_vh = len(cfg.vheads)

    in_specs = [pl.BlockSpec((1, tq, wq), lambda b, t, qi, kj: (b, qi[t], 0)),
                pl.BlockSpec((1, tk, wq), lambda b, t, qi, kj: (b, kj[t], 0)),
                pl.BlockSpec((1, N_HEADS * V_ROWS, tk), lambda b, t, qi, kj: (b, 0, kj[t]))]
    args = [q, k, v]
    if cfg.decay:
        dcol, drow = extras
        in_specs += [pl.BlockSpec((1, SUBLANES, tq), lambda b, t, qi, kj: (b, 0, qi[t])),
                     pl.BlockSpec((1, tk, N_HEADS * LANES), lambda b, t, qi, kj: (b, kj[t], 0))]
        args += [drow, dcol]
    if cfg.select:
        (sel,) = extras
        in_specs += [pl.BlockSpec((1, sel.shape[1], tq), lambda b, t, qi, kj: (b, 0, qi[t]))]
        args += [sel]
    if cfg.diff:
        for c in extras:
            in_specs += [pl.BlockSpec(c.shape, lambda b, t, qi, kj: (0, 0))]
        args += list(extras)

    grid_spec = pltpu.PrefetchScalarGridSpec(
        num_scalar_prefetch=2,
        grid=(bsz, len(pairs)),
        in_specs=in_specs,
        out_specs=pl.BlockSpec((1, tq, GROUP_WIDTH), lambda b, t, qi, kj: (b, qi[t], 0)),
        scratch_shapes=[pltpu.VMEM((n_vh, tq, MXU_WIDTH), jnp.bfloat16),
                        pltpu.VMEM((n_vh, 1, tq), jnp.float32),
                        pltpu.VMEM((n_vh, V_ROWS, tq), jnp.float32)])
    return pl.pallas_call(
        functools.partial(_attn_body, cfg),
        grid_spec=grid_spec,
        out_shape=jax.ShapeDtypeStruct((bsz, seq, GROUP_WIDTH), jnp.bfloat16),
        compiler_params=pltpu.CompilerParams(dimension_semantics=("arbitrary", "arbitrary"),
                                             vmem_limit_bytes=VMEM_LIMIT_BYTES),
        name=cfg.name,
    )(qi, kj, *args)


def _ffn_body(x_ref, oa_ref, ob_ref, oc_ref, od_ref, oe_ref, wo_ref, fnorm_ref, wg_ref, wu_ref, cw_ref, cb_ref, wd_ref,
              out_ref, xnew_s, xn_s, acc_s):
    i = pl.program_id(1)
    f = pl.program_id(2)
    tm = x_ref.shape[1]

    @pl.when(f == 0)
    def _():
        @pl.when(i == 0)
        def _():
            xn_s[0:TAIL_ROWS, :] = jnp.zeros((TAIL_ROWS, xn_s.shape[1]), xn_s.dtype)

        @pl.when(i > 0)
        def _():
            xn_s[0:TAIL_ROWS, :] = xn_s[tm:tm + TAIL_ROWS, :]

        xnew = x_ref[0]
        for g, o_ref in enumerate((oa_ref, ob_ref, oc_ref, od_ref, oe_ref)):
            xnew = xnew + _dot(o_ref[0], wo_ref[g])
        xnew_s[...] = xnew
        xn = xnew * lax.rsqrt(jnp.mean(xnew * xnew, axis=-1, keepdims=True) + EPS) * fnorm_ref[...]
        xn_s[TAIL_ROWS:TAIL_ROWS + tm, :] = xn.astype(xn_s.dtype)
        acc_s[...] = jnp.zeros_like(acc_s)

    ge = _dot(xn_s[...], wg_ref[...])
    u = _dot(xn_s[TAIL_ROWS:TAIL_ROWS + tm, :], wu_ref[...])
    g0 = ge[TAIL_ROWS:TAIL_ROWS + tm, :]
    t1 = ge[TAIL_ROWS - 1:TAIL_ROWS, :]
    t2 = ge[TAIL_ROWS - 2:TAIL_ROWS - 1, :]
    row = lax.broadcasted_iota(jnp.int32, g0.shape, 0)
    g1 = jnp.where(row == 0, t1, pltpu.roll(g0, 1, 0))
    g2 = jnp.where(row == 0, t2, jnp.where(row == 1, t1, pltpu.roll(g0, 2, 0)))
    y = cb_ref[...] + cw_ref[0:1, :] * g2
    y = y + cw_ref[1:2, :] * g1
    y = y + cw_ref[2:3, :] * g0
    hmid = (y * (1.0 / (1.0 + jnp.exp(-y)))) * u
    acc_s[...] += _dot(hmid.astype(jnp.bfloat16), wd_ref[...])

    @pl.when(f == pl.num_programs(2) - 1)
    def _():
        out_ref[0] = xnew_s[...] + acc_s[...]


def _ffn_call(x, outs, wo, fnorm, wg, wu, cw, cb, wd, tm, tf):
    bsz, seq, d = x.shape
    dff = wg.shape[1]
    assert seq % tm == 0 and dff % tf == 0
    tok = lambda w: pl.BlockSpec((1, tm, w), lambda b, i, f: (b, i, 0))
    return pl.pallas_call(
        _ffn_body,
        grid=(bsz, seq // tm, dff // tf),
        in_specs=[tok(d)] + [tok(GROUP_WIDTH)] * 5 + [
            pl.BlockSpec(wo.shape, lambda b, i, f: (0, 0, 0)),
            pl.BlockSpec(fnorm.shape, lambda b, i, f: (0, 0)),
            pl.BlockSpec((d, tf), lambda b, i, f: (0, f)),
            pl.BlockSpec((d, tf), lambda b, i, f: (0, f)),
            pl.BlockSpec((SUBLANES, tf), lambda b, i, f: (0, f)),
            pl.BlockSpec((1, tf), lambda b, i, f: (0, f)),
            pl.BlockSpec((tf, d), lambda b, i, f: (f, 0))],
        out_specs=tok(d),
        out_shape=jax.ShapeDtypeStruct((bsz, seq, d), jnp.float32),
        scratch_shapes=[pltpu.VMEM((tm, d), jnp.float32), pltpu.VMEM((TAIL_ROWS + tm, d), jnp.bfloat16),
                        pltpu.VMEM((tm, d), jnp.float32)],
        compiler_params=pltpu.CompilerParams(dimension_semantics=("arbitrary", "arbitrary", "arbitrary"),
                                             vmem_limit_bytes=VMEM_LIMIT_BYTES),
        name="ffn",
    )(x, *outs, wo, fnorm, wg, wu, cw, cb, wd)


def _pad_row(v, width=MXU_WIDTH):
    v = v.astype(jnp.float32).reshape(-1)
    return jnp.pad(v, (0, width - v.shape[0]))


def _pair_gain(g):
    one = jnp.concatenate([g.astype(jnp.float32), g.astype(jnp.float32),
                           jnp.zeros((MXU_WIDTH - 2 * MLA_QK,), jnp.float32)])
    return one


def _rope_table(positions):
    pos = positions.astype(jnp.float32)[:, :, None]
    cols_c, cols_s = [], []
    for rot in (MLA_ROPE, ROT_MOBA, ROT_DIFF):
        inv = ROPE_THETA ** (-jnp.arange(0, rot, 2, dtype=jnp.float32) / rot)
        ang = pos * inv
        cols_c.append(jnp.cos(ang))
        cols_s.append(jnp.sin(ang))
    c = jnp.concatenate(cols_c, axis=-1)
    s = jnp.concatenate(cols_s, axis=-1)
    c_hi = c.astype(jnp.bfloat16)
    c_lo = (c - c_hi.astype(jnp.float32)).astype(jnp.bfloat16)
    s_hi = s.astype(jnp.bfloat16)
    s_lo = (s - s_hi.astype(jnp.float32)).astype(jnp.bfloat16)
    one = jnp.ones(pos.shape[:2] + (1,), jnp.bfloat16)
    pad = jnp.zeros(pos.shape[:2] + (TR_WIDTH - TR_ONE - 1,), jnp.bfloat16)
    return jnp.concatenate([c_hi, c_lo, s_hi, s_lo, one, pad], axis=-1)


def _pick_tile(n, pref):
    t = pref
    while n % t:
        t //= 2
    return t


def kernel(x, mem, positions, attn_norm, ffn_norm, mem_norm, w_in, mla_cq_norm, mla_ckv_norm, mla_w_uq, mla_w_ukv, mla_q_norm, mla_k_norm, fox_b_f, fox_q_norm, fox_k_norm, moba_q_norm, moba_k_norm, diff_lambda, diff_q_norm, diff_k_norm, diff_sub_norm, mem_w_kv, mem_q_norm, mem_k_norm, w_o, ffn_w_gate, ffn_w_up, ffn_conv_w, ffn_conv_b, ffn_w_down):
    bsz, seq, d = x.shape
    depth = w_in.shape[0]
    dff = ffn_w_gate.shape[2]
    bf = jnp.bfloat16
    f32 = jnp.float32

    in_idx = _np_in_index()
    uq_idx = _np_uq_index()
    ukvk_idx, ukvv_idx = _np_ukv_index()
    gpair = jnp.asarray(_np_group_matrix(_PAIR_GROUPS), bf)
    g64 = jnp.asarray(_np_group_matrix(_G64_GROUPS), bf)
    g32 = jnp.asarray(_np_group_matrix(_G32_GROUPS), bf)
    expand = jnp.asarray(_np_rope_expand_all(), bf)
    tril = jnp.asarray(np.tril(np.ones((MOBA_BLOCK, MOBA_BLOCK), np.float32)), bf)
    rep_np = np.zeros((LANES, N_HEADS * LANES), np.float32)
    for h in range(N_HEADS):
        rep_np[h, h * LANES:(h + 1) * LANES] = 1.0
    rep = jnp.asarray(rep_np, bf)
    vsel = jnp.asarray(_np_value_select(), bf)
    tr = _rope_table(positions)

    t_dense = _pick_tile(seq, 1024)
    t_ffn = _pick_tile(seq, 512)
    tf = dff // 2 if (dff // 2) % LANES == 0 else dff

    for l in range(depth):
        win = jnp.concatenate([w_in[l], jnp.zeros((d, 1), f32)], axis=1)[:, in_idx].astype(bf)
        wuq = jnp.concatenate([mla_w_uq[l], jnp.zeros((MLA_Q_RANK, 1), f32)], axis=1)[:, uq_idx]
        wuq = jnp.pad(wuq, ((0, MXU_WIDTH - MLA_Q_RANK), (0, 0))).astype(bf)
        wukv = jnp.concatenate([mla_w_ukv[l], jnp.zeros((MLA_KV_RANK, 1), f32)], axis=1)
        wukvk = wukv[:, ukvk_idx].astype(bf)
        wukvv = wukv[:, ukvv_idx].astype(bf)
        rows = [None] * P_ROWS
        rows[P_CQ] = _pad_row(mla_cq_norm[l])
        rows[P_CKV] = _pad_row(mla_ckv_norm[l])
        rows[P_GQ] = _pair_gain(mla_q_norm[l]) * (LOG2E * MLA_QK ** -0.5)
        rows[P_GK] = _pair_gain(mla_k_norm[l])
        rows[P_FQ] = jnp.tile(fox_q_norm[l].astype(f32), N_HEADS) * (LOG2E * HEAD_DIM ** -0.5)
        rows[P_FK] = jnp.tile(fox_k_norm[l].astype(f32), N_HEADS)
        rows[P_FB] = _pad_row(fox_b_f[l])
        rows[P_MQ] = jnp.tile(moba_q_norm[l].astype(f32), N_HEADS)
        rows[P_MK] = jnp.tile(moba_k_norm[l].astype(f32), N_HEADS)
        rows[P_DQ] = jnp.tile(diff_q_norm[l].astype(f32), 2 * N_HEADS) * (LOG2E * DIFF_QK ** -0.5)
        rows[P_DK] = jnp.tile(diff_k_norm[l].astype(f32), 2 * N_HEADS)
        rows[P_EQ] = jnp.tile(mem_q_norm[l].astype(f32), N_HEADS) * (LOG2E * HEAD_DIM ** -0.5)
        par = jnp.stack([r if r is not None else jnp.zeros((MXU_WIDTH,), f32) for r in rows])

        (aq, ak, av, fq, fk, fv, fdcol, fdrow, mq, mk, mv, msel, dq, dk, dv, eq) = _prep_call(
            x, tr, attn_norm[l].reshape(1, d).astype(f32), win, wuq, wukvk, wukvv, gpair, g64, g32, expand, tril,
            rep, vsel, par)
        ek, ev = _memkv_call(mem, mem_norm[l].reshape(1, d).astype(f32), mem_w_kv[l].astype(bf), g64,
                             jnp.tile(mem_k_norm[l].astype(f32), N_HEADS).reshape(1, GROUP_WIDTH), vsel)

        lam_vec = diff_lambda[l].astype(f32)
        lam_init = 0.8 - 0.6 * math.exp(-0.3 * l)
        lam = (jnp.exp(jnp.sum(lam_vec[0] * lam_vec[1])) - jnp.exp(jnp.sum(lam_vec[2] * lam_vec[3])) + lam_init)
        lam_row = jnp.full((1, GROUP_WIDTH), 1.0, f32) * lam
        gsub = (jnp.tile(diff_sub_norm[l].astype(f32), N_HEADS) * (1.0 - lam_init)).reshape(1, GROUP_WIDTH)

        o_a = _attn_call(_CFG_MLA, aq, ak, av, (), t_dense, t_dense)
        o_b = _attn_call(_CFG_FOX, fq, fk, fv, (fdcol, fdrow), t_dense, t_dense)
        t_moba = max(t_dense, MOBA_BLOCK)
        o_c = _attn_call(_CFG_MOBA, mq, mk, mv, (msel,), t_moba, t_moba)
        o_d = _attn_call(_CFG_DIFF, dq, dk, dv, (g64, gsub, lam_row), t_dense, t_dense)
        o_e = _attn_call(_CFG_MEM, eq, ek, ev, (), t_dense, mem.shape[1])

        cw = jnp.pad(ffn_conv_w[l].astype(f32), ((0, SUBLANES - CONV_WIDTH), (0, 0)))
        x = _ffn_call(x, (o_a, o_b, o_c, o_d, o_e), w_o[l].reshape(5, GROUP_WIDTH, d).astype(bf),
                      ffn_norm[l].reshape(1, d).astype(f32), ffn_w_gate[l].astype(bf), ffn_w_up[l].astype(bf),
                      cw, ffn_conv_b[l].reshape(1, dff).astype(f32), ffn_w_down[l].astype(bf), t_ffn, tf)
    return x
```

```python
import functools
import math

import numpy as np
import jax
import jax.numpy as jnp
from jax import lax
from jax.experimental import pallas as pl
from jax.experimental.pallas import tpu as pltpu

N_HEADS = 4
HEAD_DIM = 64
GROUP_WIDTH = N_HEADS * HEAD_DIM
MLA_Q_RANK = 192
MLA_KV_RANK = 128
MLA_NOPE = 64
MLA_ROPE = 32
MLA_QK = MLA_NOPE + MLA_ROPE
DIFF_QK = HEAD_DIM // 2
ROPE_THETA = 500000.0
ROT_MOBA = HEAD_DIM // 4
ROT_DIFF = DIFF_QK // 4
MOBA_BLOCK = 256
MOBA_TOPK = 3
CONV_WIDTH = 3
EPS = 1e-6
NEG_INF = -1e30
LOG2E = math.log2(math.e)
REMOVED = -3e38

LANES = 128
SUBLANES = 8
MXU_WIDTH = 256
TAIL_ROWS = 16
VMEM_LIMIT_BYTES = 56 * 1024 * 1024

_SRC_CQ = 0
_SRC_CKV = _SRC_CQ + MLA_Q_RANK
_SRC_KR = _SRC_CKV + MLA_KV_RANK
_SRC_FOX = _SRC_KR + MLA_ROPE
_SRC_FOXF = _SRC_FOX + 3 * GROUP_WIDTH
_SRC_MOBA = _SRC_FOXF + N_HEADS
_SRC_DIFF = _SRC_MOBA + 3 * GROUP_WIDTH
_SRC_MEMQ = _SRC_DIFF + 3 * GROUP_WIDTH
_SRC_END = _SRC_MEMQ + GROUP_WIDTH

PK_CQ = 0
PK_CKV = 256
PK_KR = 384
PK_FQ, PK_FK, PK_FV = 896, 1152, 1408
PK_FF = 1664
PK_MQ, PK_MK, PK_MV = 1792, 2048, 2304
PK_DQ, PK_DK, PK_DV = 2560, 2816, 3072
PK_EQ = 3328
PK_END = 3584

PAIR_STRIDE = MLA_QK


def _pair_lane(h, d):
    return (h // 2) * MXU_WIDTH + (h % 2) * PAIR_STRIDE + d


N_FREQ = MLA_ROPE // 2 + ROT_MOBA // 2 + ROT_DIFF // 2
FREQ_BASE_MLA = 0
FREQ_BASE_MOBA = MLA_ROPE // 2
FREQ_BASE_DIFF = FREQ_BASE_MOBA + ROT_MOBA // 2
TR_ONE = 4 * N_FREQ
TR_WIDTH = LANES

(P_CQ, P_CKV, P_GQ, P_GK, P_FQ, P_FK, P_FB, P_MQ, P_MK, P_DQ, P_DK, P_EQ) = range(12)
P_ROWS = 16


def _np_in_index():
    idx = np.full((PK_END,), _SRC_END, np.int32)
    idx[PK_CQ:PK_CQ + MLA_Q_RANK] = np.arange(_SRC_CQ, _SRC_CQ + MLA_Q_RANK)
    idx[PK_CKV:PK_CKV + MLA_KV_RANK] = np.arange(_SRC_CKV, _SRC_CKV + MLA_KV_RANK)
    for h in range(N_HEADS):
        for d in range(MLA_ROPE):
            idx[PK_KR + _pair_lane(h, d)] = _SRC_KR + d
    idx[PK_FQ:PK_FQ + 3 * GROUP_WIDTH] = np.arange(_SRC_FOX, _SRC_FOX + 3 * GROUP_WIDTH)
    idx[PK_FF:PK_FF + N_HEADS] = np.arange(_SRC_FOXF, _SRC_FOXF + N_HEADS)
    idx[PK_MQ:PK_MQ + 3 * GROUP_WIDTH] = np.arange(_SRC_MOBA, _SRC_MOBA + 3 * GROUP_WIDTH)
    idx[PK_DQ:PK_DQ + 3 * GROUP_WIDTH] = np.arange(_SRC_DIFF, _SRC_DIFF + 3 * GROUP_WIDTH)
    idx[PK_EQ:PK_EQ + GROUP_WIDTH] = np.arange(_SRC_MEMQ, _SRC_MEMQ + GROUP_WIDTH)
    return idx


def _np_uq_index():
    idx = np.full((2 * MXU_WIDTH,), N_HEADS * MLA_QK, np.int32)
    for h in range(N_HEADS):
        for d in range(MLA_QK):
            idx[_pair_lane(h, d)] = h * MLA_QK + d
    return idx


def _np_ukv_index():
    zero = N_HEADS * (MLA_NOPE + HEAD_DIM)
    idx_k = np.full((2 * MXU_WIDTH,), zero, np.int32)
    idx_v = np.zeros((GROUP_WIDTH,), np.int32)
    for h in range(N_HEADS):
        for d in range(MLA_NOPE):
            idx_k[_pair_lane(h, MLA_ROPE + d)] = h * (MLA_NOPE + HEAD_DIM) + d
        for d in range(HEAD_DIM):
            idx_v[h * HEAD_DIM + d] = h * (MLA_NOPE + HEAD_DIM) + MLA_NOPE + d
    return idx_k, idx_v


def _np_group_matrix(groups):
    g = np.zeros((MXU_WIDTH, MXU_WIDTH), np.float32)
    for lo, size in groups:
        g[lo:lo + size, lo:lo + size] = 1.0 / size
    return g


_PAIR_GROUPS = [(0, MLA_ROPE), (MLA_ROPE, MLA_NOPE), (PAIR_STRIDE, MLA_ROPE), (PAIR_STRIDE + MLA_ROPE, MLA_NOPE)]
_G64_GROUPS = [(h * HEAD_DIM, HEAD_DIM) for h in range(N_HEADS)]
_G32_GROUPS = [(g * DIFF_QK, DIFF_QK) for g in range(2 * N_HEADS)]


def _np_rope_expand(regions, rot, base):
    half = rot // 2
    e = np.zeros((TR_WIDTH, 3 * MXU_WIDTH), np.float32)
    e[TR_ONE, 0:MXU_WIDTH] = 1.0
    for lo in regions:
        for r in range(half):
            f = base + r
            for lane, tab, sign in ((lo + r, 2, -1.0), (lo + half + r, 1, 1.0)):
                e[TR_ONE, lane] = 0.0
                e[f, lane] = 1.0
                e[N_FREQ + f, lane] = 1.0
                e[2 * N_FREQ + f, tab * MXU_WIDTH + lane] = sign
                e[3 * N_FREQ + f, tab * MXU_WIDTH + lane] = sign
    return e


def _np_rope_expand_all():
    return np.concatenate([
        _np_rope_expand([0, PAIR_STRIDE], MLA_ROPE, FREQ_BASE_MLA),
        _np_rope_expand([h * HEAD_DIM for h in range(N_HEADS)], ROT_MOBA, FREQ_BASE_MOBA),
        _np_rope_expand([g * DIFF_QK for g in range(2 * N_HEADS)], ROT_DIFF, FREQ_BASE_DIFF),
    ], axis=1)


def _dot(a, b):
    return jnp.dot(a, b, preferred_element_type=jnp.float32)


def _dot_nt(a, b):
    return lax.dot_general(a, b, (((1,), (1,)), ((), ())), preferred_element_type=jnp.float32)


def _split2(a):
    hi = a.astype(jnp.bfloat16)
    lo = (a - hi.astype(jnp.float32)).astype(jnp.bfloat16)
    return hi, lo


def _split3(a):
    hi = a.astype(jnp.bfloat16)
    r = a - hi.astype(jnp.float32)
    mid = r.astype(jnp.bfloat16)
    lo = (r - mid.astype(jnp.float32)).astype(jnp.bfloat16)
    return hi, mid, lo


def _group_mean_sq(a, g_bf16):
    return _dot((a * a).astype(jnp.bfloat16), g_bf16)


def _rope(x, tabs, half):
    w = x.shape[-1]
    c, s1, s2 = tabs[:, 0:w], tabs[:, w:2 * w], tabs[:, 2 * w:3 * w]
    return x * c + pltpu.roll(x, half, 1) * s1 + pltpu.roll(x, w - half, 1) * s2


def _lane_mask(shape, lo, hi):
    lane = lax.broadcasted_iota(jnp.int32, shape, len(shape) - 1)
    return (lane >= lo) & (lane < hi)


def _prep_body(x_ref, tr_ref, anorm_ref, win_ref, wuq_ref, wukvk_ref, wukvv_ref, gpair_ref, g64_ref, g32_ref,
               exp_ref, tril_ref, rep_ref, vsel_ref, par_ref,
               aq_ref, ak_ref, av_ref, fq_ref, fk_ref, fv_ref, fdcol_ref, fdrow_ref, mq_ref, mk_ref, mv_ref, msel_ref,
               dq_ref, dk_ref, dv_ref, eq_ref,
               kmean_s, carry_s):
    j = pl.program_id(1)
    tm = x_ref.shape[1]
    bf = jnp.bfloat16

    @pl.when(j == 0)
    def _():
        kmean_s[...] = jnp.zeros_like(kmean_s)
        carry_s[...] = jnp.zeros_like(carry_s)

    x = x_ref[0]
    xn = x * lax.rsqrt(jnp.mean(x * x, axis=-1, keepdims=True) + EPS) * anorm_ref[...]
    xb = xn.astype(bf)

    def proj(off, width):
        return _dot(xb, win_ref[:, off:off + width])

    def prow(r, width=MXU_WIDTH):
        return par_ref[r:r + 1, 0:width]

    gpair, g64, g32 = gpair_ref[...], g64_ref[...], g32_ref[...]
    vsel = vsel_ref[...]

    p_cq = proj(PK_CQ, MXU_WIDTH)
    p_ckv = proj(PK_CKV, MLA_KV_RANK)
    p_ff = proj(PK_FF, LANES)
    direct = {name: proj(off, MXU_WIDTH) for name, off in
              (("fq", PK_FQ), ("fk", PK_FK), ("mq", PK_MQ), ("mk", PK_MK), ("dq", PK_DQ), ("dk", PK_DK),
               ("eq", PK_EQ))}
    p_kr = proj(PK_KR, 2 * MXU_WIDTH)
    tabs = _dot(tr_ref[0], exp_ref[...])
    tab_mla = tabs[:, 0:3 * MXU_WIDTH]
    tab_moba = tabs[:, 3 * MXU_WIDTH:6 * MXU_WIDTH]
    tab_diff = tabs[:, 6 * MXU_WIDTH:9 * MXU_WIDTH]
    values = {name: proj(off, MXU_WIDTH).astype(bf) for name, off in
              (("fv", PK_FV), ("mv", PK_MV), ("dv", PK_DV))}

    cqn = p_cq * lax.rsqrt(jnp.sum(p_cq * p_cq, axis=-1, keepdims=True) * (1.0 / MLA_Q_RANK) + EPS) * prow(P_CQ)
    ckvn = p_ckv * lax.rsqrt(jnp.mean(p_ckv * p_ckv, axis=-1, keepdims=True) + EPS) * prow(P_CKV, MLA_KV_RANK)
    ckvb = ckvn.astype(bf)
    z = p_ff + prow(P_FB, LANES)
    log_f = jnp.minimum(z, 0.0) - jnp.log1p(jnp.exp(-jnp.abs(z)))
    log_f = jnp.where(_lane_mask(log_f.shape, 0, N_HEADS), log_f, 0.0)
    l1, l2, l3 = _split3(log_f)

    qa = _dot(cqn.astype(bf), wuq_ref[...])
    ka = p_kr + _dot(ckvb, wukvk_ref[...])
    values["av"] = _dot(ckvb, wukvv_ref[...]).astype(bf)
    gmat = {"fq": g64, "fk": g64, "mq": g64, "mk": g64, "dq": g32, "dk": g32, "eq": g64}
    ms = {name: _group_mean_sq(a, gmat[name]) for name, a in direct.items()}
    tril = tril_ref[...]
    dec = carry_s[...] + ((_dot(tril, l1) + _dot(tril, l2)) + _dot(tril, l3))
    carry_s[...] = dec[tm - 1:tm, :]
    for name, ref in (("fv", fv_ref), ("mv", mv_ref), ("dv", dv_ref), ("av", av_ref)):
        ref[0] = _values_t(values[name], vsel)

    gains = {"fq": P_FQ, "fk": P_FK, "mq": P_MQ, "mk": P_MK, "dq": P_DQ, "dk": P_DK, "eq": P_EQ}
    normed = {name: a * lax.rsqrt(ms[name] + EPS) * prow(gains[name]) for name, a in direct.items()}
    ms_qa = [_group_mean_sq(qa[:, p * MXU_WIDTH:(p + 1) * MXU_WIDTH], gpair) for p in range(2)]
    ms_ka = [_group_mean_sq(ka[:, p * MXU_WIDTH:(p + 1) * MXU_WIDTH], gpair) for p in range(2)]
    d1, d2, d3 = _split3(dec * LOG2E)
    rep = rep_ref[...]
    fdcol_ref[0] = (_dot(d1, rep) + _dot(d2, rep)) + _dot(d3, rep)
    row_sel = jnp.where(lax.broadcasted_iota(jnp.int32, (SUBLANES, LANES), 0)
                        == lax.broadcasted_iota(jnp.int32, (SUBLANES, LANES), 1), 1.0, 0.0).astype(bf)
    fdrow_ref[0] = (_dot_nt(row_sel, d1) + _dot_nt(row_sel, d2)) + _dot_nt(row_sel, d3)

    fq_ref[0] = normed["fq"].astype(bf)
    fk_ref[0] = normed["fk"].astype(bf)
    eq_ref[0] = normed["eq"].astype(bf)
    dq_ref[0] = _rope(normed["dq"], tab_diff, ROT_DIFF // 2).astype(bf)
    dk_ref[0] = _rope(normed["dk"], tab_diff, ROT_DIFF // 2).astype(bf)
    mq = _rope(normed["mq"], tab_moba, ROT_MOBA // 2)
    mk = _rope(normed["mk"], tab_moba, ROT_MOBA // 2)
    mq_ref[0] = (mq * (LOG2E * HEAD_DIM ** -0.5)).astype(bf)
    mk_ref[0] = mk.astype(bf)
    for p in range(2):
        sl = slice(p * MXU_WIDTH, (p + 1) * MXU_WIDTH)
        aq_ref[0, :, sl] = _rope(qa[:, sl] * lax.rsqrt(ms_qa[p] + EPS) * prow(P_GQ), tab_mla,
                                 MLA_ROPE // 2).astype(bf)
        ak_ref[0, :, sl] = _rope(ka[:, sl] * lax.rsqrt(ms_ka[p] + EPS) * prow(P_GK), tab_mla,
                                 MLA_ROPE // 2).astype(bf)

    kmean_s[pl.ds(j, 1), :] = jnp.mean(mk, axis=0, keepdims=True)
    km_hi, km_lo = _split2(kmean_s[...])
    nbp = kmean_s.shape[0]
    blk = lax.broadcasted_iota(jnp.int32, (nbp, tm), 0)
    past = blk < j
    for h in range(N_HEADS):
        q_hi, q_lo = _split2(jnp.where(_lane_mask(mq.shape, h * HEAD_DIM, (h + 1) * HEAD_DIM), mq, 0.0))
        gate = (_dot_nt(km_hi, q_hi) + _dot_nt(km_lo, q_hi)) + _dot_nt(km_hi, q_lo)
        work = jnp.where(past, gate, NEG_INF)
        sel = jnp.zeros((nbp, tm), jnp.bool_)
        for _ in range(MOBA_TOPK):
            mx = jnp.max(work, axis=0, keepdims=True)
            first = jnp.min(jnp.where(work == mx, blk, nbp), axis=0, keepdims=True)
            pick = blk == first
            sel = sel | pick
            work = jnp.where(pick, REMOVED, work)
        msel_ref[0, h * nbp:(h + 1) * nbp, :] = jnp.where(sel & past, 0.0, NEG_INF)


def _full_spec(shape):
    n = len(shape)
    return pl.BlockSpec(shape, lambda *_: (0,) * n)


def _moba_blocks_padded(seq):
    return -(-(seq // MOBA_BLOCK) // SUBLANES) * SUBLANES


def _prep_call(x, tr, anorm, win, wuq, wukvk, wukvv, gpair, g64, g32, expand, tril, rep, vsel, par):
    bsz, seq, d = x.shape
    tm = MOBA_BLOCK
    assert seq % tm == 0
    nbp = _moba_blocks_padded(seq)
    bf = jnp.bfloat16
    f32 = jnp.float32

    vt = -N_HEADS * V_ROWS
    widths = [(2 * MXU_WIDTH, bf), (2 * MXU_WIDTH, bf), (vt, bf),
              (GROUP_WIDTH, bf), (GROUP_WIDTH, bf), (vt, bf),
              (N_HEADS * LANES, f32), (-SUBLANES, f32),
              (GROUP_WIDTH, bf), (GROUP_WIDTH, bf), (vt, bf), (-N_HEADS * nbp, f32),
              (GROUP_WIDTH, bf), (GROUP_WIDTH, bf), (vt, bf),
              (GROUP_WIDTH, bf)]

    def tok(width):
        if width > 0:
            return pl.BlockSpec((1, tm, width), lambda b, j: (b, j, 0))
        return pl.BlockSpec((1, -width, tm), lambda b, j: (b, 0, j))

    def shape(width):
        return (bsz, seq, width) if width > 0 else (bsz, -width, seq)

    consts = [anorm, win, wuq, wukvk, wukvv, gpair, g64, g32, expand, tril, rep, vsel, par]
    return pl.pallas_call(
        _prep_body,
        grid=(bsz, seq // tm),
        in_specs=[tok(d), tok(TR_WIDTH)] + [_full_spec(c.shape) for c in consts],
        out_specs=[tok(w) for w, _ in widths],
        out_shape=[jax.ShapeDtypeStruct(shape(w), dt) for w, dt in widths],
        scratch_shapes=[pltpu.VMEM((nbp, GROUP_WIDTH), jnp.float32), pltpu.VMEM((1, LANES), jnp.float32)],
        compiler_params=pltpu.CompilerParams(dimension_semantics=("arbitrary", "arbitrary"),
                                             vmem_limit_bytes=VMEM_LIMIT_BYTES),
        name="prep",
    )(x, tr, *consts)


def _memkv_body(mem_ref, mnorm_ref, w_ref, g64_ref, gain_ref, vsel_ref, k_ref, v_ref):
    m = mem_ref[0]
    mn = m * lax.rsqrt(jnp.mean(m * m, axis=-1, keepdims=True) + EPS) * mnorm_ref[...]
    kv = _dot(mn.astype(jnp.bfloat16), w_ref[...])
    k = kv[:, 0:GROUP_WIDTH]
    k = k * lax.rsqrt(_group_mean_sq(k, g64_ref[...]) + EPS) * gain_ref[...]
    k_ref[0] = k.astype(jnp.bfloat16)
    v_ref[0] = _values_t(kv[:, GROUP_WIDTH:2 * GROUP_WIDTH].astype(jnp.bfloat16), vsel_ref[...])


def _memkv_call(mem, mnorm, w, g64, gain, vsel):
    bsz, mlen, d = mem.shape
    k_shape, vt_shape = (bsz, mlen, GROUP_WIDTH), (bsz, N_HEADS * V_ROWS, mlen)
    return pl.pallas_call(
        _memkv_body,
        grid=(bsz,),
        in_specs=[pl.BlockSpec((1, mlen, d), lambda b: (b, 0, 0)), _full_spec(mnorm.shape), _full_spec(w.shape),
                  _full_spec(g64.shape), _full_spec(gain.shape), _full_spec(vsel.shape)],
        out_specs=[pl.BlockSpec((1,) + s[1:], lambda b: (b, 0, 0)) for s in (k_shape, vt_shape)],
        out_shape=[jax.ShapeDtypeStruct(s, jnp.bfloat16) for s in (k_shape, vt_shape)],
        compiler_params=pltpu.CompilerParams(dimension_semantics=("arbitrary",), vmem_limit_bytes=VMEM_LIMIT_BYTES),
        name="mem_kv",
    )(mem, mnorm, w, g64, gain, vsel)


class _AttnCfg:
    def __init__(self, name, vheads, n_maps, causal, decay=False, select=False, diff=False):
        self.name = name
        self.vheads = vheads
        self.n_maps = n_maps
        self.causal = causal
        self.decay = decay
        self.select = select
        self.diff = diff


_PLAIN_VHEADS = [(0, h * HEAD_DIM, (h + 1) * HEAD_DIM, 0, h) for h in range(N_HEADS)]
_CFG_MLA = _AttnCfg("attn_mla", [((h // 2) * MXU_WIDTH, (h % 2) * PAIR_STRIDE, (h % 2) * PAIR_STRIDE + MLA_QK, 0, h)
                                 for h in range(N_HEADS)], 1, True)
_CFG_FOX = _AttnCfg("attn_fox", _PLAIN_VHEADS, 1, True, decay=True)
_CFG_MOBA = _AttnCfg("attn_moba", _PLAIN_VHEADS, 1, True, select=True)
_CFG_DIFF = _AttnCfg("attn_diff", [(0, h * HEAD_DIM + c * DIFF_QK, h * HEAD_DIM + (c + 1) * DIFF_QK, c, h)
                                   for c in range(2) for h in range(N_HEADS)], 2, True, diff=True)
_CFG_MEM = _AttnCfg("attn_mem", _PLAIN_VHEADS, 1, False)


ONES_ROW = HEAD_DIM
V_ROWS = HEAD_DIM + 16
QK_LOOKAHEAD = 8
Q_SUB = 2 * MXU_WIDTH


def _np_value_select():
    sel = np.zeros((N_HEADS * V_ROWS, GROUP_WIDTH), np.float32)
    for h in range(N_HEADS):
        for d in range(HEAD_DIM):
            sel[h * V_ROWS + d, h * HEAD_DIM + d] = 1.0
    return sel


def _values_t(v, vsel):
    vt = _dot_nt(vsel, v)
    row = lax.broadcasted_iota(jnp.int32, vt.shape, 0)
    ones = row == ONES_ROW
    for h in range(1, N_HEADS):
        ones = ones | (row == h * V_ROWS + ONES_ROW)
    return jnp.where(ones, 1.0, vt).astype(jnp.bfloat16)


def _tile_lanes(x, width):
    return jnp.tile(x, (1, width // LANES)) if width != LANES else x


def _attn_body(cfg, qi_ref, kj_ref, *refs):
    refs = list(refs)
    q_ref, k_ref, vt_ref = refs[:3]
    pos = 3
    if cfg.decay:
        dq_ref, dk_ref = refs[pos:pos + 2]
        pos += 2
    if cfg.select:
        sel_ref = refs[pos]
        pos += 1
    if cfg.diff:
        g64_ref, gsub_ref, lam_ref = refs[pos:pos + 3]
        pos += 3
    o_ref, qm_s, m_s, acc_s = refs[pos:pos + 4]

    t = pl.program_id(1)
    i = qi_ref[t]
    j = kj_ref[t]
    tq = q_ref.shape[1]
    tk = k_ref.shape[1]

    @pl.when(j == 0)
    def _():
        for n, (off, lo, hi, _, _) in enumerate(cfg.vheads):
            qb = q_ref[0, :, off:off + MXU_WIDTH]
            qm_s[n] = jnp.where(_lane_mask(qb.shape, lo, hi), qb, jnp.zeros_like(qb))
        m_s[...] = jnp.full(m_s.shape, NEG_INF, jnp.float32)
        acc_s[...] = jnp.zeros_like(acc_s)

    def step(diag):
        qs = min(tq, Q_SUB)
        items = [(n, u) for n in range(len(cfg.vheads)) for u in range(tq // qs)]

        def n_keys(u):
            return (u + 1) * qs if diag else tk

        def scores(item):
            n, u = item
            off = cfg.vheads[n][0]
            return _dot_nt(k_ref[0, 0:n_keys(u), off:off + MXU_WIDTH],
                           qm_s[n, u * qs:(u + 1) * qs, :])

        raw = {it: scores(items[it]) for it in range(min(QK_LOOKAHEAD, len(items)))}
        for it, (n, u) in enumerate(items):
            _, _, _, c, h = cfg.vheads[n]
            nk = n_keys(u)
            cols = slice(u * qs, (u + 1) * qs)
            s = raw.pop(it)
            if cfg.decay:
                s = (dq_ref[0, h:h + 1, cols] - _tile_lanes(dk_ref[0, 0:nk, h * LANES:(h + 1) * LANES], qs)) + s
            if cfg.select:
                nbp = sel_ref.shape[1] // N_HEADS
                qpos = u * qs + lax.broadcasted_iota(jnp.int32, (1, qs), 1)
                parts = []
                for kb in range(nk // MOBA_BLOCK):
                    rows = s[kb * MOBA_BLOCK:(kb + 1) * MOBA_BLOCK, :]
                    if not (diag and kb == nk // MOBA_BLOCK - 1):
                        bias = sel_ref[0, pl.ds(h * nbp + j * (tk // MOBA_BLOCK) + kb, 1), cols]
                        if diag:
                            bias = jnp.where(qpos < (kb + 1) * MOBA_BLOCK, 0.0, bias)
                        rows = rows + bias
                    parts.append(rows)
                s = parts[0] if len(parts) == 1 else jnp.concatenate(parts, axis=0)
            if diag:
                key = lax.broadcasted_iota(jnp.int32, (nk, qs), 0)
                qry = u * qs + lax.broadcasted_iota(jnp.int32, (nk, qs), 1)
                s = jnp.where(key <= qry, s, NEG_INF)
            m_prev = m_s[n, :, cols]
            m_new = jnp.maximum(m_prev, jnp.max(s, axis=0, keepdims=True))
            alpha = jnp.exp2(m_prev - m_new)
            p = jnp.exp2(s - m_new)
            m_s[n, :, cols] = m_new
            acc_s[n, :, cols] = acc_s[n, :, cols] * alpha + _dot(vt_ref[0, h * V_ROWS:(h + 1) * V_ROWS, 0:nk],
                                                                 p.astype(jnp.bfloat16))
            if it + QK_LOOKAHEAD < len(items):
                raw[it + QK_LOOKAHEAD] = scores(items[it + QK_LOOKAHEAD])

    if cfg.causal:
        pl.when(j < i)(functools.partial(step, False))
        pl.when(j == i)(functools.partial(step, True))
        last = j == i
    else:
        step(False)
        last = j == 0

    @pl.when(last)
    def _():
        outs = []
        for c in range(cfg.n_maps):
            heads = []
            for h in range(N_HEADS):
                acc = acc_s[c * N_HEADS + h]
                heads.append(acc[0:HEAD_DIM, :] / acc[ONES_ROW:ONES_ROW + 1, :])
            outs.append(jnp.concatenate(heads, axis=0).T)
        if cfg.diff:
            o = outs[0] - lam_ref[0:1, :] * outs[1]
            o = o * lax.rsqrt(_group_mean_sq(o, g64_ref[...]) + EPS) * gsub_ref[...]
        else:
            o = outs[0]
        o_ref[0] = o.astype(o_ref.dtype)


def _attn_call(cfg, q, k, v, extras, tq, tk):
    bsz, seq, wq = q.shape
    sk = k.shape[1]
    nq = seq // tq
    assert seq % tq == 0 and sk % tk == 0
    if cfg.causal:
        assert tq == tk and sk == seq
        pairs = [(i, j) for i in range(nq) for j in range(i + 1)]
    else:
        assert sk == tk
        pairs = [(i, 0) for i in range(nq)]
    qi = jnp.asarray(np.array([p[0] for p in pairs], np.int32))
    kj = jnp.asarray(np.array([p[1] for p in pairs], np.int32))
    n_vh = len(cfg.vheads)

    in_specs = [pl.BlockSpec((1, tq, wq), lambda b, t, qi, kj: (b, qi[t], 0)),
                pl.BlockSpec((1, tk, wq), lambda b, t, qi, kj: (b, kj[t], 0)),
                pl.BlockSpec((1, N_HEADS * V_ROWS, tk), lambda b, t, qi, kj: (b, 0, kj[t]))]
    args = [q, k, v]
    if cfg.decay:
        dcol, drow = extras
        in_specs += [pl.BlockSpec((1, SUBLANES, tq), lambda b, t, qi, kj: (b, 0, qi[t])),
                     pl.BlockSpec((1, tk, N_HEADS * LANES), lambda b, t, qi, kj: (b, kj[t], 0))]
        args += [drow, dcol]
    if cfg.select:
        (sel,) = extras
        in_specs += [pl.BlockSpec((1, sel.shape[1], tq), lambda b, t, qi, kj: (b, 0, qi[t]))]
        args += [sel]
    if cfg.diff:
        for c in extras:
            in_specs += [pl.BlockSpec(c.shape, lambda b, t, qi, kj: (0, 0))]
        args += list(extras)

    grid_spec = pltpu.PrefetchScalarGridSpec(
        num_scalar_prefetch=2,
        grid=(bsz, len(pairs)),
        in_specs=in_specs,
        out_specs=pl.BlockSpec((1, tq, GROUP_WIDTH), lambda b, t, qi, kj: (b, qi[t], 0)),
        scratch_shapes=[pltpu.VMEM((n_vh, tq, MXU_WIDTH), jnp.bfloat16),
                        pltpu.VMEM((n_vh, 1, tq), jnp.float32),
                        pltpu.VMEM((n_vh, V_ROWS, tq), jnp.float32)])
    return pl.pallas_call(
        functools.partial(_attn_body, cfg),
        grid_spec=grid_spec,
        out_shape=jax.ShapeDtypeStruct((bsz, seq, GROUP_WIDTH), jnp.bfloat16),
        compiler_params=pltpu.CompilerParams(dimension_semantics=("arbitrary", "arbitrary"),
                                             vmem_limit_bytes=VMEM_LIMIT_BYTES),
        name=cfg.name,
    )(qi, kj, *args)


def _ffn_body(x_ref, oa_ref, ob_ref, oc_ref, od_ref, oe_ref, wo_ref, fnorm_ref, wg_ref, wu_ref, cw_ref, cb_ref, wd_ref,
              out_ref, xnew_s, xn_s, acc_s):
    i = pl.program_id(1)
    f = pl.program_id(2)
    tm = x_ref.shape[1]

    @pl.when(f == 0)
    def _():
        @pl.when(i == 0)
        def _():
            xn_s[0:TAIL_ROWS, :] = jnp.zeros((TAIL_ROWS, xn_s.shape[1]), xn_s.dtype)

        @pl.when(i > 0)
        def _():
            xn_s[0:TAIL_ROWS, :] = xn_s[tm:tm + TAIL_ROWS, :]

        xnew = x_ref[0]
        for g, o_ref in enumerate((oa_ref, ob_ref, oc_ref, od_ref, oe_ref)):
            xnew = xnew + _dot(o_ref[0], wo_ref[g])
        xnew_s[...] = xnew
        xn = xnew * lax.rsqrt(jnp.mean(xnew * xnew, axis=-1, keepdims=True) + EPS) * fnorm_ref[...]
        xn_s[TAIL_ROWS:TAIL_ROWS + tm, :] = xn.astype(xn_s.dtype)
        acc_s[...] = jnp.zeros_like(acc_s)

    ge = _dot(xn_s[...], wg_ref[...])
    u = _dot(xn_s[TAIL_ROWS:TAIL_ROWS + tm, :], wu_ref[...])
    g0 = ge[TAIL_ROWS:TAIL_ROWS + tm, :]
    t1 = ge[TAIL_ROWS - 1:TAIL_ROWS, :]
    t2 = ge[TAIL_ROWS - 2:TAIL_ROWS - 1, :]
    row = lax.broadcasted_iota(jnp.int32, g0.shape, 0)
    g1 = jnp.where(row == 0, t1, pltpu.roll(g0, 1, 0))
    g2 = jnp.where(row == 0, t2, jnp.where(row == 1, t1, pltpu.roll(g0, 2, 0)))
    y = cb_ref[...] + cw_ref[0:1, :] * g2
    y = y + cw_ref[1:2, :] * g1
    y = y + cw_ref[2:3, :] * g0
    hmid = (y * (1.0 / (1.0 + jnp.exp(-y)))) * u
    acc_s[...] += _dot(hmid.astype(jnp.bfloat16), wd_ref[...])

    @pl.when(f == pl.num_programs(2) - 1)
    def _():
        out_ref[0] = xnew_s[...] + acc_s[...]


def _ffn_call(x, outs, wo, fnorm, wg, wu, cw, cb, wd, tm, tf):
    bsz, seq, d = x.shape
    dff = wg.shape[1]
    assert seq % tm == 0 and dff % tf == 0
    tok = lambda w: pl.BlockSpec((1, tm, w), lambda b, i, f: (b, i, 0))
    return pl.pallas_call(
        _ffn_body,
        grid=(bsz, seq // tm, dff // tf),
        in_specs=[tok(d)] + [tok(GROUP_WIDTH)] * 5 + [
            pl.BlockSpec(wo.shape, lambda b, i, f: (0, 0, 0)),
            pl.BlockSpec(fnorm.shape, lambda b, i, f: (0, 0)),
            pl.BlockSpec((d, tf), lambda b, i, f: (0, f)),
            pl.BlockSpec((d, tf), lambda b, i, f: (0, f)),
            pl.BlockSpec((SUBLANES, tf), lambda b, i, f: (0, f)),
            pl.BlockSpec((1, tf), lambda b, i, f: (0, f)),
            pl.BlockSpec((tf, d), lambda b, i, f: (f, 0))],
        out_specs=tok(d),
        out_shape=jax.ShapeDtypeStruct((bsz, seq, d), jnp.float32),
        scratch_shapes=[pltpu.VMEM((tm, d), jnp.float32), pltpu.VMEM((TAIL_ROWS + tm, d), jnp.bfloat16),
                        pltpu.VMEM((tm, d), jnp.float32)],
        compiler_params=pltpu.CompilerParams(dimension_semantics=("arbitrary", "arbitrary", "arbitrary"),
                                             vmem_limit_bytes=VMEM_LIMIT_BYTES),
        name="ffn",
    )(x, *outs, wo, fnorm, wg, wu, cw, cb, wd)


def _pad_row(v, width=MXU_WIDTH):
    v = v.astype(jnp.float32).reshape(-1)
    return jnp.pad(v, (0, width - v.shape[0]))


def _pair_gain(g):
    one = jnp.concatenate([g.astype(jnp.float32), g.astype(jnp.float32),
                           jnp.zeros((MXU_WIDTH - 2 * MLA_QK,), jnp.float32)])
    return one


def _rope_table(positions):
    pos = positions.astype(jnp.float32)[:, :, None]
    cols_c, cols_s = [], []
    for rot in (MLA_ROPE, ROT_MOBA, ROT_DIFF):
        inv = ROPE_THETA ** (-jnp.arange(0, rot, 2, dtype=jnp.float32) / rot)
        ang = pos * inv
        cols_c.append(jnp.cos(ang))
        cols_s.append(jnp.sin(ang))
    c = jnp.concatenate(cols_c, axis=-1)
    s = jnp.concatenate(cols_s, axis=-1)
    c_hi = c.astype(jnp.bfloat16)
    c_lo = (c - c_hi.astype(jnp.float32)).astype(jnp.bfloat16)
    s_hi = s.astype(jnp.bfloat16)
    s_lo = (s - s_hi.astype(jnp.float32)).astype(jnp.bfloat16)
    one = jnp.ones(pos.shape[:2] + (1,), jnp.bfloat16)
    pad = jnp.zeros(pos.shape[:2] + (TR_WIDTH - TR_ONE - 1,), jnp.bfloat16)
    return jnp.concatenate([c_hi, c_lo, s_hi, s_lo, one, pad], axis=-1)


def _pick_tile(n, pref):
    t = pref
    while n % t:
        t //= 2
    return t


def kernel(x, mem, positions, attn_norm, ffn_norm, mem_norm, w_in, mla_cq_norm, mla_ckv_norm, mla_w_uq, mla_w_ukv, mla_q_norm, mla_k_norm, fox_b_f, fox_q_norm, fox_k_norm, moba_q_norm, moba_k_norm, diff_lambda, diff_q_norm, diff_k_norm, diff_sub_norm, mem_w_kv, mem_q_norm, mem_k_norm, w_o, ffn_w_gate, ffn_w_up, ffn_conv_w, ffn_conv_b, ffn_w_down):
    bsz, seq, d = x.shape
    depth = w_in.shape[0]
    dff = ffn_w_gate.shape[2]
    bf = jnp.bfloat16
    f32 = jnp.float32

    in_idx = _np_in_index()
    uq_idx = _np_uq_index()
    ukvk_idx, ukvv_idx = _np_ukv_index()
    gpair = jnp.asarray(_np_group_matrix(_PAIR_GROUPS), bf)
    g64 = jnp.asarray(_np_group_matrix(_G64_GROUPS), bf)
    g32 = jnp.asarray(_np_group_matrix(_G32_GROUPS), bf)
    expand = jnp.asarray(_np_rope_expand_all(), bf)
    tril = jnp.asarray(np.tril(np.ones((MOBA_BLOCK, MOBA_BLOCK), np.float32)), bf)
    rep_np = np.zeros((LANES, N_HEADS * LANES), np.float32)
    for h in range(N_HEADS):
        rep_np[h, h * LANES:(h + 1) * LANES] = 1.0
    rep = jnp.asarray(rep_np, bf)
    vsel = jnp.asarray(_np_value_select(), bf)
    tr = _rope_table(positions)

    t_dense = _pick_tile(seq, 1024)
    t_ffn = _pick_tile(seq, 512)
    tf = dff // 2 if (dff // 2) % LANES == 0 else dff

    for l in range(depth):
        win = jnp.concatenate([w_in[l], jnp.zeros((d, 1), f32)], axis=1)[:, in_idx].astype(bf)
        wuq = jnp.concatenate([mla_w_uq[l], jnp.zeros((MLA_Q_RANK, 1), f32)], axis=1)[:, uq_idx]
        wuq = jnp.pad(wuq, ((0, MXU_WIDTH - MLA_Q_RANK), (0, 0))).astype(bf)
        wukv = jnp.concatenate([mla_w_ukv[l], jnp.zeros((MLA_KV_RANK, 1), f32)], axis=1)
        wukvk = wukv[:, ukvk_idx].astype(bf)
        wukvv = wukv[:, ukvv_idx].astype(bf)
        rows = [None] * P_ROWS
        rows[P_CQ] = _pad_row(mla_cq_norm[l])
        rows[P_CKV] = _pad_row(mla_ckv_norm[l])
        rows[P_GQ] = _pair_gain(mla_q_norm[l]) * (LOG2E * MLA_QK ** -0.5)
        rows[P_GK] = _pair_gain(mla_k_norm[l])
        rows[P_FQ] = jnp.tile(fox_q_norm[l].astype(f32), N_HEADS) * (LOG2E * HEAD_DIM ** -0.5)
        rows[P_FK] = jnp.tile(fox_k_norm[l].astype(f32), N_HEADS)
        rows[P_FB] = _pad_row(fox_b_f[l])
        rows[P_MQ] = jnp.tile(moba_q_norm[l].astype(f32), N_HEADS)
        rows[P_MK] = jnp.tile(moba_k_norm[l].astype(f32), N_HEADS)
        rows[P_DQ] = jnp.tile(diff_q_norm[l].astype(f32), 2 * N_HEADS) * (LOG2E * DIFF_QK ** -0.5)
        rows[P_DK] = jnp.tile(diff_k_norm[l].astype(f32), 2 * N_HEADS)
        rows[P_EQ] = jnp.tile(mem_q_norm[l].astype(f32), N_HEADS) * (LOG2E * HEAD_DIM ** -0.5)
        par = jnp.stack([r if r is not None else jnp.zeros((MXU_WIDTH,), f32) for r in rows])

        (aq, ak, av, fq, fk, fv, fdcol, fdrow, mq, mk, mv, msel, dq, dk, dv, eq) = _prep_call(
            x, tr, attn_norm[l].reshape(1, d).astype(f32), win, wuq, wukvk, wukvv, gpair, g64, g32, expand, tril,
            rep, vsel, par)
        ek, ev = _memkv_call(mem, mem_norm[l].reshape(1, d).astype(f32), mem_w_kv[l].astype(bf), g64,
                             jnp.tile(mem_k_norm[l].astype(f32), N_HEADS).reshape(1, GROUP_WIDTH), vsel)

        lam_vec = diff_lambda[l].astype(f32)
        lam_init = 0.8 - 0.6 * math.exp(-0.3 * l)
        lam = (jnp.exp(jnp.sum(lam_vec[0] * lam_vec[1])) - jnp.exp(jnp.sum(lam_vec[2] * lam_vec[3])) + lam_init)
        lam_row = jnp.full((1, GROUP_WIDTH), 1.0, f32) * lam
        gsub = (jnp.tile(diff_sub_norm[l].astype(f32), N_HEADS) * (1.0 - lam_init)).reshape(1, GROUP_WIDTH)

        o_a = _attn_call(_CFG_MLA, aq, ak, av, (), t_dense, t_dense)
        o_b = _attn_call(_CFG_FOX, fq, fk, fv, (fdcol, fdrow), t_dense, t_dense)
        t_moba = max(t_dense, MOBA_BLOCK)
        o_c = _attn_call(_CFG_MOBA, mq, mk, mv, (msel,), t_moba, t_moba)
        o_d = _attn_call(_CFG_DIFF, dq, dk, dv, (g64, gsub, lam_row), t_dense, t_dense)
        o_e = _attn_call(_CFG_MEM, eq, ek, ev, (), t_dense, mem.shape[1])

        cw = jnp.pad(ffn_conv_w[l].astype(f32), ((0, SUBLANES - CONV_WIDTH), (0, 0)))
        x = _ffn_call(x, (o_a, o_b, o_c, o_d, o_e), w_o[l].reshape(5, GROUP_WIDTH, d).astype(bf),
                      ffn_norm[l].reshape(1, d).astype(f32), ffn_w_gate[l].astype(bf), ffn_w_up[l].astype(bf),
                      cw, ffn_conv_b[l].reshape(1, dff).astype(f32), ffn_w_down[l].astype(bf), t_ffn, tf)
    return x
```

```python
import functools
import math

import numpy as np
import jax
import jax.numpy as jnp
from jax import lax
from jax.experimental import pallas as pl
from jax.experimental.pallas import tpu as pltpu

N_HEADS = 4
HEAD_DIM = 64
GROUP_WIDTH = N_HEADS * HEAD_DIM
MLA_Q_RANK = 192
MLA_KV_RANK = 128
MLA_NOPE = 64
MLA_ROPE = 32
MLA_QK = MLA_NOPE + MLA_ROPE
DIFF_QK = HEAD_DIM // 2
ROPE_THETA = 500000.0
ROT_MOBA = HEAD_DIM // 4
ROT_DIFF = DIFF_QK // 4
MOBA_BLOCK = 256
MOBA_TOPK = 3
CONV_WIDTH = 3
EPS = 1e-6
NEG_INF = -1e30
LOG2E = math.log2(math.e)
REMOVED = -3e38

LANES = 128
SUBLANES = 8
MXU_WIDTH = 256
TAIL_ROWS = 16
VMEM_LIMIT_BYTES = 56 * 1024 * 1024

_SRC_CQ = 0
_SRC_CKV = _SRC_CQ + MLA_Q_RANK
_SRC_KR = _SRC_CKV + MLA_KV_RANK
_SRC_FOX = _SRC_KR + MLA_ROPE
_SRC_FOXF = _SRC_FOX + 3 * GROUP_WIDTH
_SRC_MOBA = _SRC_FOXF + N_HEADS
_SRC_DIFF = _SRC_MOBA + 3 * GROUP_WIDTH
_SRC_MEMQ = _SRC_DIFF + 3 * GROUP_WIDTH
_SRC_END = _SRC_MEMQ + GROUP_WIDTH

PK_CQ = 0
PK_CKV = 256
PK_KR = 384
PK_FQ, PK_FK, PK_FV = 896, 1152, 1408
PK_FF = 1664
PK_MQ, PK_MK, PK_MV = 1792, 2048, 2304
PK_DQ, PK_DK, PK_DV = 2560, 2816, 3072
PK_EQ = 3328
PK_END = 3584

PAIR_STRIDE = MLA_QK


def _pair_lane(h, d):
    return (h // 2) * MXU_WIDTH + (h % 2) * PAIR_STRIDE + d


N_FREQ = 32
FREQ_BASE_MLA = 0
FREQ_BASE_MOBA = MLA_ROPE // 2
FREQ_BASE_DIFF = FREQ_BASE_MOBA + ROT_MOBA // 2
TR_ONE = FREQ_BASE_DIFF + ROT_DIFF // 2
TR_WIDTH = 4 * N_FREQ
assert TR_ONE < N_FREQ and TR_WIDTH == LANES

(P_CQ, P_CKV, P_GQ, P_GK, P_FQ, P_FK, P_FB, P_MQ, P_MK, P_DQ, P_DK, P_EQ) = range(12)
P_ROWS = 16


def _np_in_index():
    idx = np.full((PK_END,), _SRC_END, np.int32)
    idx[PK_CQ:PK_CQ + MLA_Q_RANK] = np.arange(_SRC_CQ, _SRC_CQ + MLA_Q_RANK)
    idx[PK_CKV:PK_CKV + MLA_KV_RANK] = np.arange(_SRC_CKV, _SRC_CKV + MLA_KV_RANK)
    for h in range(N_HEADS):
        for d in range(MLA_ROPE):
            idx[PK_KR + _pair_lane(h, d)] = _SRC_KR + d
    idx[PK_FQ:PK_FQ + 3 * GROUP_WIDTH] = np.arange(_SRC_FOX, _SRC_FOX + 3 * GROUP_WIDTH)
    idx[PK_FF:PK_FF + N_HEADS] = np.arange(_SRC_FOXF, _SRC_FOXF + N_HEADS)
    idx[PK_MQ:PK_MQ + 3 * GROUP_WIDTH] = np.arange(_SRC_MOBA, _SRC_MOBA + 3 * GROUP_WIDTH)
    idx[PK_DQ:PK_DQ + 3 * GROUP_WIDTH] = np.arange(_SRC_DIFF, _SRC_DIFF + 3 * GROUP_WIDTH)
    idx[PK_EQ:PK_EQ + GROUP_WIDTH] = np.arange(_SRC_MEMQ, _SRC_MEMQ + GROUP_WIDTH)
    return idx


def _np_uq_index():
    idx = np.full((2 * MXU_WIDTH,), N_HEADS * MLA_QK, np.int32)
    for h in range(N_HEADS):
        for d in range(MLA_QK):
            idx[_pair_lane(h, d)] = h * MLA_QK + d
    return idx


def _np_ukv_index():
    zero = N_HEADS * (MLA_NOPE + HEAD_DIM)
    idx_k = np.full((2 * MXU_WIDTH,), zero, np.int32)
    idx_v = np.zeros((GROUP_WIDTH,), np.int32)
    for h in range(N_HEADS):
        for d in range(MLA_NOPE):
            idx_k[_pair_lane(h, MLA_ROPE + d)] = h * (MLA_NOPE + HEAD_DIM) + d
        for d in range(HEAD_DIM):
            idx_v[h * HEAD_DIM + d] = h * (MLA_NOPE + HEAD_DIM) + MLA_NOPE + d
    return idx_k, idx_v


def _np_group_matrix(groups):
    g = np.zeros((MXU_WIDTH, MXU_WIDTH), np.float32)
    for lo, size in groups:
        g[lo:lo + size, lo:lo + size] = 1.0 / size
    return g


_PAIR_GROUPS = [(0, MLA_ROPE), (MLA_ROPE, MLA_NOPE), (PAIR_STRIDE, MLA_ROPE), (PAIR_STRIDE + MLA_ROPE, MLA_NOPE)]
_G64_GROUPS = [(h * HEAD_DIM, HEAD_DIM) for h in range(N_HEADS)]
_G32_GROUPS = [(g * DIFF_QK, DIFF_QK) for g in range(2 * N_HEADS)]


def _np_rope_expand(regions, rot, base):
    half = rot // 2
    e = np.zeros((TR_WIDTH, 2 * MXU_WIDTH), np.float32)
    e[TR_ONE, 0:MXU_WIDTH] = 1.0
    for lo in regions:
        assert lo % rot == 0
        for r in range(half):
            f = base + r
            for lane, sign in ((lo + r, -1.0), (lo + half + r, 1.0)):
                e[TR_ONE, lane] = 0.0
                e[f, lane] = 1.0
                e[N_FREQ + f, lane] = 1.0
                e[2 * N_FREQ + f, MXU_WIDTH + lane] = sign
                e[3 * N_FREQ + f, MXU_WIDTH + lane] = sign
    return e


def _np_rope_expand_all():
    return np.concatenate([
        _np_rope_expand([0, PAIR_STRIDE], MLA_ROPE, FREQ_BASE_MLA),
        _np_rope_expand([h * HEAD_DIM for h in range(N_HEADS)], ROT_MOBA, FREQ_BASE_MOBA),
        _np_rope_expand([g * DIFF_QK for g in range(2 * N_HEADS)], ROT_DIFF, FREQ_BASE_DIFF),
    ], axis=1)


def _dot(a, b):
    return jnp.dot(a, b, preferred_element_type=jnp.float32)


def _dot_nt(a, b):
    return lax.dot_general(a, b, (((1,), (1,)), ((), ())), preferred_element_type=jnp.float32)


def _split2(a):
    hi = a.astype(jnp.bfloat16)
    lo = (a - hi.astype(jnp.float32)).astype(jnp.bfloat16)
    return hi, lo


def _split3(a):
    hi = a.astype(jnp.bfloat16)
    r = a - hi.astype(jnp.float32)
    mid = r.astype(jnp.bfloat16)
    lo = (r - mid.astype(jnp.float32)).astype(jnp.bfloat16)
    return hi, mid, lo


def _group_mean_sq(a, g_bf16):
    return _dot((a * a).astype(jnp.bfloat16), g_bf16)


def _rope(x, tabs, half):
    w = x.shape[-1]
    lane = lax.broadcasted_iota(jnp.int32, x.shape, 1)
    partner = jnp.where((lane & (2 * half - 1)) >= half, pltpu.roll(x, half, 1), pltpu.roll(x, w - half, 1))
    return x * tabs[:, 0:w] + partner * tabs[:, w:2 * w]


def _lane_mask(shape, lo, hi):
    lane = lax.broadcasted_iota(jnp.int32, shape, len(shape) - 1)
    return (lane >= lo) & (lane < hi)


def _prep_body(x_ref, tr_ref, anorm_ref, win_ref, wuq_ref, wukvk_ref, wukvv_ref, gpair_ref, g64_ref, g32_ref,
               exp_ref, tril_ref, vsel_ref, par_ref,
               aq_ref, ak_ref, av_ref, fq_ref, fk_ref, fv_ref, fdcol_ref, fdrow_ref, mq_ref, mk_ref, mv_ref, msel_ref,
               dq_ref, dk_ref, dv_ref, eq_ref,
               kmean_s, carry_s):
    j = pl.program_id(1)
    tm = x_ref.shape[1]
    bf = jnp.bfloat16

    @pl.when(j == 0)
    def _():
        kmean_s[...] = jnp.zeros_like(kmean_s)
        carry_s[...] = jnp.zeros_like(carry_s)

    x = x_ref[0]
    xn = x * lax.rsqrt(jnp.mean(x * x, axis=-1, keepdims=True) + EPS) * anorm_ref[...]
    xb = xn.astype(bf)

    def proj(off, width):
        return _dot(xb, win_ref[:, off:off + width])

    def prow(r, width=MXU_WIDTH):
        return par_ref[r:r + 1, 0:width]

    gpair, g64, g32 = gpair_ref[...], g64_ref[...], g32_ref[...]
    vsel = vsel_ref[...]

    tabs = _dot(tr_ref[0], exp_ref[...])
    tab_mla = tabs[:, 0:2 * MXU_WIDTH]
    tab_moba = tabs[:, 2 * MXU_WIDTH:4 * MXU_WIDTH]
    tab_diff = tabs[:, 4 * MXU_WIDTH:6 * MXU_WIDTH]
    direct = {name: proj(off, MXU_WIDTH) for name, off in (("mq", PK_MQ), ("mk", PK_MK))}
    p_cq = proj(PK_CQ, MXU_WIDTH)
    p_ckv = proj(PK_CKV, MLA_KV_RANK)
    p_ff = proj(PK_FF, LANES)
    direct.update({name: proj(off, MXU_WIDTH) for name, off in
                   (("fq", PK_FQ), ("fk", PK_FK), ("dq", PK_DQ), ("dk", PK_DK), ("eq", PK_EQ))})
    p_kr = proj(PK_KR, 2 * MXU_WIDTH)
    values = {name: proj(off, MXU_WIDTH).astype(bf) for name, off in
              (("fv", PK_FV), ("mv", PK_MV), ("dv", PK_DV))}

    cqn = p_cq * lax.rsqrt(jnp.sum(p_cq * p_cq, axis=-1, keepdims=True) * (1.0 / MLA_Q_RANK) + EPS) * prow(P_CQ)
    ckvn = p_ckv * lax.rsqrt(jnp.mean(p_ckv * p_ckv, axis=-1, keepdims=True) + EPS) * prow(P_CKV, MLA_KV_RANK)
    ckvb = ckvn.astype(bf)
    z = p_ff + prow(P_FB, LANES)
    log_f = jnp.minimum(z, 0.0) - jnp.log1p(jnp.exp(-jnp.abs(z)))
    log_f = jnp.where(_lane_mask(log_f.shape, 0, N_HEADS), log_f, 0.0)
    l1, l2, l3 = _split3(log_f)

    gmat = {"fq": g64, "fk": g64, "mq": g64, "mk": g64, "dq": g32, "dk": g32, "eq": g64}
    ms = {name: _group_mean_sq(a, gmat[name]) for name, a in direct.items()}
    qa = _dot(cqn.astype(bf), wuq_ref[...])
    ka = p_kr + _dot(ckvb, wukvk_ref[...])
    values["av"] = _dot(ckvb, wukvv_ref[...]).astype(bf)
    tril = tril_ref[...]
    dec = carry_s[...] + ((_dot(tril, l1) + _dot(tril, l2)) + _dot(tril, l3))
    carry_s[...] = dec[tm - 1:tm, :]

    gains = {"fq": P_FQ, "fk": P_FK, "mq": P_MQ, "mk": P_MK, "dq": P_DQ, "dk": P_DK, "eq": P_EQ}
    normed = {name: a * lax.rsqrt(ms[name] + EPS) * prow(gains[name]) for name, a in direct.items()}
    mq = _rope(normed["mq"], tab_moba, ROT_MOBA // 2)
    mk = _rope(normed["mk"], tab_moba, ROT_MOBA // 2)
    kmean_s[pl.ds(j, 1), :] = jnp.mean(mk, axis=0, keepdims=True)
    km_hi, km_lo = _split2(kmean_s[...])
    gates = []
    for h in range(N_HEADS):
        q_hi, q_lo = _split2(jnp.where(_lane_mask(mq.shape, h * HEAD_DIM, (h + 1) * HEAD_DIM), mq, 0.0))
        gates.append((_dot_nt(km_hi, q_hi) + _dot_nt(km_lo, q_hi)) + _dot_nt(km_hi, q_lo))
    ms_qa = [_group_mean_sq(qa[:, p * MXU_WIDTH:(p + 1) * MXU_WIDTH], gpair) for p in range(2)]
    ms_ka = [_group_mean_sq(ka[:, p * MXU_WIDTH:(p + 1) * MXU_WIDTH], gpair) for p in range(2)]
    dec2 = dec * LOG2E
    d1, d2, d3 = _split3(dec2)
    for h in range(N_HEADS):
        fdcol_ref[0, :, h * LANES:(h + 1) * LANES] = jnp.broadcast_to(dec2[:, h:h + 1], (tm, LANES))
    row_sel = jnp.where(lax.broadcasted_iota(jnp.int32, (SUBLANES, LANES), 0)
                        == lax.broadcasted_iota(jnp.int32, (SUBLANES, LANES), 1), 1.0, 0.0).astype(bf)
    fdrow_ref[0] = (_dot_nt(row_sel, d1) + _dot_nt(row_sel, d2)) + _dot_nt(row_sel, d3)
    for name, ref in (("fv", fv_ref), ("mv", mv_ref), ("dv", dv_ref), ("av", av_ref)):
        ref[0] = _values_t(values[name], vsel)

    fq_ref[0] = normed["fq"].astype(bf)
    fk_ref[0] = normed["fk"].astype(bf)
    eq_ref[0] = normed["eq"].astype(bf)
    dq_ref[0] = _rope(normed["dq"], tab_diff, ROT_DIFF // 2).astype(bf)
    dk_ref[0] = _rope(normed["dk"], tab_diff, ROT_DIFF // 2).astype(bf)
    mq_ref[0] = (mq * (LOG2E * HEAD_DIM ** -0.5)).astype(bf)
    mk_ref[0] = mk.astype(bf)
    for p in range(2):
        sl = slice(p * MXU_WIDTH, (p + 1) * MXU_WIDTH)
        aq_ref[0, :, sl] = _rope(qa[:, sl] * lax.rsqrt(ms_qa[p] + EPS) * prow(P_GQ), tab_mla,
                                 MLA_ROPE // 2).astype(bf)
        ak_ref[0, :, sl] = _rope(ka[:, sl] * lax.rsqrt(ms_ka[p] + EPS) * prow(P_GK), tab_mla,
                                 MLA_ROPE // 2).astype(bf)

    nbp = kmean_s.shape[0]
    blk = lax.broadcasted_iota(jnp.int32, (nbp, tm), 0)
    past = blk < j
    for h in range(N_HEADS):
        work = jnp.where(past, gates[h], NEG_INF)
        sel = jnp.zeros((nbp, tm), jnp.bool_)
        for _ in range(MOBA_TOPK):
            mx = jnp.max(work, axis=0, keepdims=True)
            first = jnp.min(jnp.where(work == mx, blk, nbp), axis=0, keepdims=True)
            pick = blk == first
            sel = sel | pick
            work = jnp.where(pick, REMOVED, work)
        msel_ref[0, h * nbp:(h + 1) * nbp, :] = jnp.where(sel & past, 0.0, NEG_INF)


def _full_spec(shape):
    n = len(shape)
    return pl.BlockSpec(shape, lambda *_: (0,) * n)


def _moba_blocks_padded(seq):
    return -(-(seq // MOBA_BLOCK) // SUBLANES) * SUBLANES


def _prep_call(x, tr, anorm, win, wuq, wukvk, wukvv, gpair, g64, g32, expand, tril, vsel, par):
    bsz, seq, d = x.shape
    tm = MOBA_BLOCK
    assert seq % tm == 0
    nbp = _moba_blocks_padded(seq)
    bf = jnp.bfloat16
    f32 = jnp.float32

    vt = -N_HEADS * V_ROWS
    widths = [(2 * MXU_WIDTH, bf), (2 * MXU_WIDTH, bf), (vt, bf),
              (GROUP_WIDTH, bf), (GROUP_WIDTH, bf), (vt, bf),
              (N_HEADS * LANES, f32), (-SUBLANES, f32),
              (GROUP_WIDTH, bf), (GROUP_WIDTH, bf), (vt, bf), (-N_HEADS * nbp, f32),
              (GROUP_WIDTH, bf), (GROUP_WIDTH, bf), (vt, bf),
              (GROUP_WIDTH, bf)]

    def tok(width):
        if width > 0:
            return pl.BlockSpec((1, tm, width), lambda b, j: (b, j, 0))
        return pl.BlockSpec((1, -width, tm), lambda b, j: (b, 0, j))

    def shape(width):
        return (bsz, seq, width) if width > 0 else (bsz, -width, seq)

    consts = [anorm, win, wuq, wukvk, wukvv, gpair, g64, g32, expand, tril, vsel, par]
    return pl.pallas_call(
        _prep_body,
        grid=(bsz, seq // tm),
        in_specs=[tok(d), tok(TR_WIDTH)] + [_full_spec(c.shape) for c in consts],
        out_specs=[tok(w) for w, _ in widths],
        out_shape=[jax.ShapeDtypeStruct(shape(w), dt) for w, dt in widths],
        scratch_shapes=[pltpu.VMEM((nbp, GROUP_WIDTH), jnp.float32), pltpu.VMEM((1, LANES), jnp.float32)],
        compiler_params=pltpu.CompilerParams(dimension_semantics=("arbitrary", "arbitrary"),
                                             vmem_limit_bytes=VMEM_LIMIT_BYTES),
        name="prep",
    )(x, tr, *consts)


def _memkv_body(mem_ref, mnorm_ref, w_ref, g64_ref, gain_ref, vsel_ref, k_ref, v_ref):
    m = mem_ref[0]
    mn = m * lax.rsqrt(jnp.mean(m * m, axis=-1, keepdims=True) + EPS) * mnorm_ref[...]
    kv = _dot(mn.astype(jnp.bfloat16), w_ref[...])
    k = kv[:, 0:GROUP_WIDTH]
    k = k * lax.rsqrt(_group_mean_sq(k, g64_ref[...]) + EPS) * gain_ref[...]
    k_ref[0] = k.astype(jnp.bfloat16)
    v_ref[0] = _values_t(kv[:, GROUP_WIDTH:2 * GROUP_WIDTH].astype(jnp.bfloat16), vsel_ref[...])


def _memkv_call(mem, mnorm, w, g64, gain, vsel):
    bsz, mlen, d = mem.shape
    k_shape, vt_shape = (bsz, mlen, GROUP_WIDTH), (bsz, N_HEADS * V_ROWS, mlen)
    return pl.pallas_call(
        _memkv_body,
        grid=(bsz,),
        in_specs=[pl.BlockSpec((1, mlen, d), lambda b: (b, 0, 0)), _full_spec(mnorm.shape), _full_spec(w.shape),
                  _full_spec(g64.shape), _full_spec(gain.shape), _full_spec(vsel.shape)],
        out_specs=[pl.BlockSpec((1,) + s[1:], lambda b: (b, 0, 0)) for s in (k_shape, vt_shape)],
        out_shape=[jax.ShapeDtypeStruct(s, jnp.bfloat16) for s in (k_shape, vt_shape)],
        compiler_params=pltpu.CompilerParams(dimension_semantics=("arbitrary",), vmem_limit_bytes=VMEM_LIMIT_BYTES),
        name="mem_kv",
    )(mem, mnorm, w, g64, gain, vsel)


class _AttnCfg:
    def __init__(self, name, vheads, n_maps, causal, decay=False, select=False, diff=False):
        self.name = name
        self.vheads = vheads
        self.n_maps = n_maps
        self.causal = causal
        self.decay = decay
        self.select = select
        self.diff = diff


_PLAIN_VHEADS = [(0, h * HEAD_DIM, (h + 1) * HEAD_DIM, 0, h) for h in range(N_HEADS)]
_CFG_MLA = _AttnCfg("attn_mla", [((h // 2) * MXU_WIDTH, (h % 2) * PAIR_STRIDE, (h % 2) * PAIR_STRIDE + MLA_QK, 0, h)
                                 for h in range(N_HEADS)], 1, True)
_CFG_FOX = _AttnCfg("attn_fox", _PLAIN_VHEADS, 1, True, decay=True)
_CFG_MOBA = _AttnCfg("attn_moba", _PLAIN_VHEADS, 1, True, select=True)
_CFG_DIFF = _AttnCfg("attn_diff", [(0, h * HEAD_DIM + c * DIFF_QK, h * HEAD_DIM + (c + 1) * DIFF_QK, c, h)
                                   for c in range(2) for h in range(N_HEADS)], 2, True, diff=True)
_CFG_MEM = _AttnCfg("attn_mem", _PLAIN_VHEADS, 1, False)


ONES_ROW = HEAD_DIM
V_ROWS = HEAD_DIM + 16
QK_LOOKAHEAD = 6
Q_SUB = 2 * MXU_WIDTH


def _np_value_select():
    sel = np.zeros((N_HEADS * V_ROWS, GROUP_WIDTH), np.float32)
    for h in range(N_HEADS):
        for d in range(HEAD_DIM):
            sel[h * V_ROWS + d, h * HEAD_DIM + d] = 1.0
    return sel


def _values_t(v, vsel):
    vt = _dot_nt(vsel, v)
    row = lax.broadcasted_iota(jnp.int32, vt.shape, 0)
    ones = row == ONES_ROW
    for h in range(1, N_HEADS):
        ones = ones | (row == h * V_ROWS + ONES_ROW)
    return jnp.where(ones, 1.0, vt).astype(jnp.bfloat16)


def _tile_lanes(x, width):
    return jnp.tile(x, (1, width // LANES)) if width != LANES else x


def _attn_body(cfg, qi_ref, kj_ref, *refs):
    refs = list(refs)
    q_ref, k_ref, vt_ref = refs[:3]
    pos = 3
    if cfg.decay:
        dq_ref, dk_ref = refs[pos:pos + 2]
        pos += 2
    if cfg.select:
        sel_ref = refs[pos]
        pos += 1
    if cfg.diff:
        g64_ref, gsub_ref, lam_ref = refs[pos:pos + 3]
        pos += 3
    o_ref, qm_s, m_s, acc_s = refs[pos:pos + 4]

    t = pl.program_id(1)
    i = qi_ref[t]
    j = kj_ref[t]
    tq = q_ref.shape[1]
    tk = k_ref.shape[1]

    @pl.when(j == 0)
    def _():
        for n, (off, lo, hi, _, _) in enumerate(cfg.vheads):
            qb = q_ref[0, :, off:off + MXU_WIDTH]
            qm_s[n] = jnp.where(_lane_mask(qb.shape, lo, hi), qb, jnp.zeros_like(qb))
        m_s[...] = jnp.full(m_s.shape, NEG_INF, jnp.float32)
        acc_s[...] = jnp.zeros_like(acc_s)

    def step(diag):
        qs = min(tq, Q_SUB)
        items = [(n, u) for n in range(len(cfg.vheads)) for u in range(tq // qs)]

        def n_keys(u):
            return (u + 1) * qs if diag else tk

        def scores(item):
            n, u = item
            off, _, _, _, h = cfg.vheads[n]
            nk = n_keys(u)
            cols = slice(u * qs, (u + 1) * qs)
            s = _dot_nt(k_ref[0, 0:nk, off:off + MXU_WIDTH], qm_s[n, cols, :])
            if cfg.decay:
                s = (dq_ref[0, h:h + 1, cols] - _tile_lanes(dk_ref[0, 0:nk, h * LANES:(h + 1) * LANES], qs)) + s
            if cfg.select:
                nbp = sel_ref.shape[1] // N_HEADS
                qpos = u * qs + lax.broadcasted_iota(jnp.int32, (1, qs), 1)
                parts = []
                for kb in range(nk // MOBA_BLOCK):
                    rows = s[kb * MOBA_BLOCK:(kb + 1) * MOBA_BLOCK, :]
                    if not (diag and kb == nk // MOBA_BLOCK - 1):
                        bias = sel_ref[0, pl.ds(h * nbp + j * (tk // MOBA_BLOCK) + kb, 1), cols]
                        if diag:
                            bias = jnp.where(qpos < (kb + 1) * MOBA_BLOCK, 0.0, bias)
                        rows = rows + bias
                    parts.append(rows)
                s = parts[0] if len(parts) == 1 else jnp.concatenate(parts, axis=0)
            if diag:
                key = lax.broadcasted_iota(jnp.int32, (nk, qs), 0)
                qry = u * qs + lax.broadcasted_iota(jnp.int32, (nk, qs), 1)
                s = jnp.where(key <= qry, s, NEG_INF)
            return s, jnp.max(s, axis=0, keepdims=True)

        raw = {it: scores(items[it]) for it in range(min(QK_LOOKAHEAD, len(items)))}
        for it, (n, u) in enumerate(items):
            h = cfg.vheads[n][4]
            nk = n_keys(u)
            cols = slice(u * qs, (u + 1) * qs)
            s, s_max = raw.pop(it)
            m_prev = m_s[n, :, cols]
            m_new = jnp.maximum(m_prev, s_max)
            alpha = jnp.exp2(m_prev - m_new)
            p = jnp.exp2(s - m_new)
            m_s[n, :, cols] = m_new
            acc_s[n, :, cols] = acc_s[n, :, cols] * alpha + _dot(vt_ref[0, h * V_ROWS:(h + 1) * V_ROWS, 0:nk],
                                                                 p.astype(jnp.bfloat16))
            if it + QK_LOOKAHEAD < len(items):
                raw[it + QK_LOOKAHEAD] = scores(items[it + QK_LOOKAHEAD])

    if cfg.causal:
        pl.when(j < i)(functools.partial(step, False))
        pl.when(j == i)(functools.partial(step, True))
        last = j == i
    else:
        step(False)
        last = j == 0

    @pl.when(last)
    def _():
        outs = []
        for c in range(cfg.n_maps):
            heads = []
            for h in range(N_HEADS):
                acc = acc_s[c * N_HEADS + h]
                heads.append(acc[0:HEAD_DIM, :] / acc[ONES_ROW:ONES_ROW + 1, :])
            outs.append(jnp.concatenate(heads, axis=0).T)
        if cfg.diff:
            o = outs[0] - lam_ref[0:1, :] * outs[1]
            o = o * lax.rsqrt(_group_mean_sq(o, g64_ref[...]) + EPS) * gsub_ref[...]
        else:
            o = outs[0]
        o_ref[0] = o.astype(o_ref.dtype)


def _attn_call(cfg, q, k, v, extras, tq, tk):
    bsz, seq, wq = q.shape
    sk = k.shape[1]
    nq = seq // tq
    assert seq % tq == 0 and sk % tk == 0
    if cfg.causal:
        assert tq == tk and sk == seq
        pairs = [(i, j) for i in range(nq) for j in range(i + 1)]
    else:
        assert sk == tk
        pairs = [(i, 0) for i in range(nq)]
    qi = jnp.asarray(np.array([p[0] for p in pairs], np.int32))
    kj = jnp.asarray(np.array([p[1] for p in pairs], np.int32))
    n_vh = len(cfg.vheads)

    in_specs = [pl.BlockSpec((1, tq, wq), lambda b, t, qi, kj: (b, qi[t], 0)),
                pl.BlockSpec((1, tk, wq), lambda b, t, qi, kj: (b, kj[t], 0)),
                pl.BlockSpec((1, N_HEADS * V_ROWS, tk), lambda b, t, qi, kj: (b, 0, kj[t]))]
    args = [q, k, v]
    if cfg.decay:
        dcol, drow = extras
        in_specs += [pl.BlockSpec((1, SUBLANES, tq), lambda b, t, qi, kj: (b, 0, qi[t])),
                     pl.BlockSpec((1, tk, N_HEADS * LANES), lambda b, t, qi, kj: (b, kj[t], 0))]
        args += [drow, dcol]
    if cfg.select:
        (sel,) = extras
        in_specs += [pl.BlockSpec((1, sel.shape[1], tq), lambda b, t, qi, kj: (b, 0, qi[t]))]
        args += [sel]
    if cfg.diff:
        for c in extras:
            in_specs += [pl.BlockSpec(c.shape, lambda b, t, qi, kj: (0, 0))]
        args += list(extras)

    grid_spec = pltpu.PrefetchScalarGridSpec(
        num_scalar_prefetch=2,
        grid=(bsz, len(pairs)),
        in_specs=in_specs,
        out_specs=pl.BlockSpec((1, tq, GROUP_WIDTH), lambda b, t, qi, kj: (b, qi[t], 0)),
        scratch_shapes=[pltpu.VMEM((n_vh, tq, MXU_WIDTH), jnp.bfloat16),
                        pltpu.VMEM((n_vh, 1, tq), jnp.float32),
                        pltpu.VMEM((n_vh, V_ROWS, tq), jnp.float32)])
    return pl.pallas_call(
        functools.partial(_attn_body, cfg),
        grid_spec=grid_spec,
        out_shape=jax.ShapeDtypeStruct((bsz, seq, GROUP_WIDTH), jnp.bfloat16),
        compiler_params=pltpu.CompilerParams(dimension_semantics=("arbitrary", "arbitrary"),
                                             vmem_limit_bytes=VMEM_LIMIT_BYTES),
        name=cfg.name,
    )(qi, kj, *args)


def _ffn_body(x_ref, oa_ref, ob_ref, oc_ref, od_ref, oe_ref, wo_ref, fnorm_ref, wg_ref, wu_ref, cw_ref, cb_ref, wd_ref,
              out_ref, xnew_s, xn_s, acc_s):
    i = pl.program_id(1)
    f = pl.program_id(2)
    tm = x_ref.shape[1]

    @pl.when(f == 0)
    def _():
        @pl.when(i == 0)
        def _():
            xn_s[0:TAIL_ROWS, :] = jnp.zeros((TAIL_ROWS, xn_s.shape[1]), xn_s.dtype)

        @pl.when(i > 0)
        def _():
            xn_s[0:TAIL_ROWS, :] = xn_s[tm:tm + TAIL_ROWS, :]

        xnew = x_ref[0]
        for g, o_ref in enumerate((oa_ref, ob_ref, oc_ref, od_ref, oe_ref)):
            xnew = xnew + _dot(o_ref[0], wo_ref[g])
        xnew_s[...] = xnew
        xn = xnew * lax.rsqrt(jnp.mean(xnew * xnew, axis=-1, keepdims=True) + EPS) * fnorm_ref[...]
        xn_s[TAIL_ROWS:TAIL_ROWS + tm, :] = xn.astype(xn_s.dtype)
        acc_s[...] = jnp.zeros_like(acc_s)

    ge = _dot(xn_s[...], wg_ref[...])
    u = _dot(xn_s[TAIL_ROWS:TAIL_ROWS + tm, :], wu_ref[...])
    g0 = ge[TAIL_ROWS:TAIL_ROWS + tm, :]
    t1 = ge[TAIL_ROWS - 1:TAIL_ROWS, :]
    t2 = ge[TAIL_ROWS - 2:TAIL_ROWS - 1, :]
    row = lax.broadcasted_iota(jnp.int32, g0.shape, 0)
    g1 = jnp.where(row == 0, t1, pltpu.roll(g0, 1, 0))
    g2 = jnp.where(row == 0, t2, jnp.where(row == 1, t1, pltpu.roll(g0, 2, 0)))
    y = cb_ref[...] + cw_ref[0:1, :] * g2
    y = y + cw_ref[1:2, :] * g1
    y = y + cw_ref[2:3, :] * g0
    hmid = (y * (1.0 / (1.0 + jnp.exp(-y)))) * u
    acc_s[...] += _dot(hmid.astype(jnp.bfloat16), wd_ref[...])

    @pl.when(f == pl.num_programs(2) - 1)
    def _():
        out_ref[0] = xnew_s[...] + acc_s[...]


def _ffn_call(x, outs, wo, fnorm, wg, wu, cw, cb, wd, tm, tf):
    bsz, seq, d = x.shape
    dff = wg.shape[1]
    assert seq % tm == 0 and dff % tf == 0
    tok = lambda w: pl.BlockSpec((1, tm, w), lambda b, i, f: (b, i, 0))
    return pl.pallas_call(
        _ffn_body,
        grid=(bsz, seq // tm, dff // tf),
        in_specs=[tok(d)] + [tok(GROUP_WIDTH)] * 5 + [
            pl.BlockSpec(wo.shape, lambda b, i, f: (0, 0, 0)),
            pl.BlockSpec(fnorm.shape, lambda b, i, f: (0, 0)),
            pl.BlockSpec((d, tf), lambda b, i, f: (0, f)),
            pl.BlockSpec((d, tf), lambda b, i, f: (0, f)),
            pl.BlockSpec((SUBLANES, tf), lambda b, i, f: (0, f)),
            pl.BlockSpec((1, tf), lambda b, i, f: (0, f)),
            pl.BlockSpec((tf, d), lambda b, i, f: (f, 0))],
        out_specs=tok(d),
        out_shape=jax.ShapeDtypeStruct((bsz, seq, d), jnp.float32),
        scratch_shapes=[pltpu.VMEM((tm, d), jnp.float32), pltpu.VMEM((TAIL_ROWS + tm, d), jnp.bfloat16),
                        pltpu.VMEM((tm, d), jnp.float32)],
        compiler_params=pltpu.CompilerParams(dimension_semantics=("arbitrary", "arbitrary", "arbitrary"),
                                             vmem_limit_bytes=VMEM_LIMIT_BYTES),
        name="ffn",
    )(x, *outs, wo, fnorm, wg, wu, cw, cb, wd)


def _pad_row(v, width=MXU_WIDTH):
    v = v.astype(jnp.float32).reshape(-1)
    return jnp.pad(v, (0, width - v.shape[0]))


def _pair_gain(g):
    one = jnp.concatenate([g.astype(jnp.float32), g.astype(jnp.float32),
                           jnp.zeros((MXU_WIDTH - 2 * MLA_QK,), jnp.float32)])
    return one


def _rope_table(positions):
    pos = positions.astype(jnp.float32)[:, :, None]
    inv = [ROPE_THETA ** (-jnp.arange(0, rot, 2, dtype=jnp.float32) / rot) for rot in (MLA_ROPE, ROT_MOBA, ROT_DIFF)]
    inv = jnp.concatenate(inv + [jnp.zeros((N_FREQ - TR_ONE,), jnp.float32)])
    ang = pos * inv
    c, s = jnp.cos(ang), jnp.sin(ang)
    c_hi = c.astype(jnp.bfloat16)
    c_lo = (c - c_hi.astype(jnp.float32)).astype(jnp.bfloat16)
    s_hi = s.astype(jnp.bfloat16)
    s_lo = (s - s_hi.astype(jnp.float32)).astype(jnp.bfloat16)
    return jnp.concatenate([c_hi, c_lo, s_hi, s_lo], axis=-1)


def _pick_tile(n, pref):
    t = pref
    while n % t:
        t //= 2
    return t


def kernel(x, mem, positions, attn_norm, ffn_norm, mem_norm, w_in, mla_cq_norm, mla_ckv_norm, mla_w_uq, mla_w_ukv, mla_q_norm, mla_k_norm, fox_b_f, fox_q_norm, fox_k_norm, moba_q_norm, moba_k_norm, diff_lambda, diff_q_norm, diff_k_norm, diff_sub_norm, mem_w_kv, mem_q_norm, mem_k_norm, w_o, ffn_w_gate, ffn_w_up, ffn_conv_w, ffn_conv_b, ffn_w_down):
    bsz, seq, d = x.shape
    depth = w_in.shape[0]
    dff = ffn_w_gate.shape[2]
    bf = jnp.bfloat16
    f32 = jnp.float32

    in_idx = _np_in_index()
    uq_idx = _np_uq_index()
    ukvk_idx, ukvv_idx = _np_ukv_index()
    gpair = jnp.asarray(_np_group_matrix(_PAIR_GROUPS), bf)
    g64 = jnp.asarray(_np_group_matrix(_G64_GROUPS), bf)
    g32 = jnp.asarray(_np_group_matrix(_G32_GROUPS), bf)
    expand = jnp.asarray(_np_rope_expand_all(), bf)
    tril = jnp.asarray(np.tril(np.ones((MOBA_BLOCK, MOBA_BLOCK), np.float32)), bf)
    vsel = jnp.asarray(_np_value_select(), bf)
    tr = _rope_table(positions)

    t_dense = _pick_tile(seq, 1024)
    t_ffn = _pick_tile(seq, 512)
    tf = dff // 2 if (dff // 2) % LANES == 0 else dff

    for l in range(depth):
        win = jnp.concatenate([w_in[l], jnp.zeros((d, 1), f32)], axis=1)[:, in_idx].astype(bf)
        wuq = jnp.concatenate([mla_w_uq[l], jnp.zeros((MLA_Q_RANK, 1), f32)], axis=1)[:, uq_idx]
        wuq = jnp.pad(wuq, ((0, MXU_WIDTH - MLA_Q_RANK), (0, 0))).astype(bf)
        wukv = jnp.concatenate([mla_w_ukv[l], jnp.zeros((MLA_KV_RANK, 1), f32)], axis=1)
        wukvk = wukv[:, ukvk_idx].astype(bf)
        wukvv = wukv[:, ukvv_idx].astype(bf)
        rows = [None] * P_ROWS
        rows[P_CQ] = _pad_row(mla_cq_norm[l])
        rows[P_CKV] = _pad_row(mla_ckv_norm[l])
        rows[P_GQ] = _pair_gain(mla_q_norm[l]) * (LOG2E * MLA_QK ** -0.5)
        rows[P_GK] = _pair_gain(mla_k_norm[l])
        rows[P_FQ] = jnp.tile(fox_q_norm[l].astype(f32), N_HEADS) * (LOG2E * HEAD_DIM ** -0.5)
        rows[P_FK] = jnp.tile(fox_k_norm[l].astype(f32), N_HEADS)
        rows[P_FB] = _pad_row(fox_b_f[l])
        rows[P_MQ] = jnp.tile(moba_q_norm[l].astype(f32), N_HEADS)
        rows[P_MK] = jnp.tile(moba_k_norm[l].astype(f32), N_HEADS)
        rows[P_DQ] = jnp.tile(diff_q_norm[l].astype(f32), 2 * N_HEADS) * (LOG2E * DIFF_QK ** -0.5)
        rows[P_DK] = jnp.tile(diff_k_norm[l].astype(f32), 2 * N_HEADS)
        rows[P_EQ] = jnp.tile(mem_q_norm[l].astype(f32), N_HEADS) * (LOG2E * HEAD_DIM ** -0.5)
        par = jnp.stack([r if r is not None else jnp.zeros((MXU_WIDTH,), f32) for r in rows])

        (aq, ak, av, fq, fk, fv, fdcol, fdrow, mq, mk, mv, msel, dq, dk, dv, eq) = _prep_call(
            x, tr, attn_norm[l].reshape(1, d).astype(f32), win, wuq, wukvk, wukvv, gpair, g64, g32, expand, tril,
            vsel, par)
        ek, ev = _memkv_call(mem, mem_norm[l].reshape(1, d).astype(f32), mem_w_kv[l].astype(bf), g64,
                             jnp.tile(mem_k_norm[l].astype(f32), N_HEADS).reshape(1, GROUP_WIDTH), vsel)

        lam_vec = diff_lambda[l].astype(f32)
        lam_init = 0.8 - 0.6 * math.exp(-0.3 * l)
        lam = (jnp.exp(jnp.sum(lam_vec[0] * lam_vec[1])) - jnp.exp(jnp.sum(lam_vec[2] * lam_vec[3])) + lam_init)
        lam_row = jnp.full((1, GROUP_WIDTH), 1.0, f32) * lam
        gsub = (jnp.tile(diff_sub_norm[l].astype(f32), N_HEADS) * (1.0 - lam_init)).reshape(1, GROUP_WIDTH)

        o_a = _attn_call(_CFG_MLA, aq, ak, av, (), t_dense, t_dense)
        o_b = _attn_call(_CFG_FOX, fq, fk, fv, (fdcol, fdrow), t_dense, t_dense)
        t_moba = max(t_dense, MOBA_BLOCK)
        o_c = _attn_call(_CFG_MOBA, mq, mk, mv, (msel,), t_moba, t_moba)
        o_d = _attn_call(_CFG_DIFF, dq, dk, dv, (g64, gsub, lam_row), t_dense, t_dense)
        o_e = _attn_call(_CFG_MEM, eq, ek, ev, (), t_dense, mem.shape[1])

        cw = jnp.pad(ffn_conv_w[l].astype(f32), ((0, SUBLANES - CONV_WIDTH), (0, 0)))
        x = _ffn_call(x, (o_a, o_b, o_c, o_d, o_e), w_o[l].reshape(5, GROUP_WIDTH, d).astype(bf),
                      ffn_norm[l].reshape(1, d).astype(f32), ffn_w_gate[l].astype(bf), ffn_w_up[l].astype(bf),
                      cw, ffn_conv_b[l].reshape(1, dff).astype(f32), ffn_w_down[l].astype(bf), t_ffn, tf)
    return x
```

```python
import functools
import math

import numpy as np
import jax
import jax.numpy as jnp
from jax import lax
from jax.experimental import pallas as pl
from jax.experimental.pallas import tpu as pltpu

N_HEADS = 4
HEAD_DIM = 64
GROUP_WIDTH = N_HEADS * HEAD_DIM
MLA_Q_RANK = 192
MLA_KV_RANK = 128
MLA_NOPE = 64
MLA_ROPE = 32
MLA_QK = MLA_NOPE + MLA_ROPE
DIFF_QK = HEAD_DIM // 2
ROPE_THETA = 500000.0
ROT_MOBA = HEAD_DIM // 4
ROT_DIFF = DIFF_QK // 4
MOBA_BLOCK = 256
MOBA_TOPK = 3
CONV_WIDTH = 3
EPS = 1e-6
NEG_INF = -1e30
LOG2E = math.log2(math.e)
REMOVED = -3e38

LANES = 128
SUBLANES = 8
MXU_WIDTH = 256
TAIL_ROWS = 16
VMEM_LIMIT_BYTES = 56 * 1024 * 1024

_SRC_CQ = 0
_SRC_CKV = _SRC_CQ + MLA_Q_RANK
_SRC_KR = _SRC_CKV + MLA_KV_RANK
_SRC_FOX = _SRC_KR + MLA_ROPE
_SRC_FOXF = _SRC_FOX + 3 * GROUP_WIDTH
_SRC_MOBA = _SRC_FOXF + N_HEADS
_SRC_DIFF = _SRC_MOBA + 3 * GROUP_WIDTH
_SRC_MEMQ = _SRC_DIFF + 3 * GROUP_WIDTH
_SRC_END = _SRC_MEMQ + GROUP_WIDTH

PK_CQ = 0
PK_CKV = 256
PK_KR = 384
PK_FQ, PK_FK, PK_FV = 896, 1152, 1408
PK_FF = 1664
PK_MQ, PK_MK, PK_MV = 1792, 2048, 2304
PK_DQ, PK_DK, PK_DV = 2560, 2816, 3072
PK_EQ = 3328
PK_END = 3584

PAIR_STRIDE = MLA_QK


def _pair_lane(h, d):
    return (h // 2) * MXU_WIDTH + (h % 2) * PAIR_STRIDE + d


N_FREQ = 32
FREQ_BASE_MLA = 0
FREQ_BASE_MOBA = MLA_ROPE // 2
FREQ_BASE_DIFF = FREQ_BASE_MOBA + ROT_MOBA // 2
TR_ONE = FREQ_BASE_DIFF + ROT_DIFF // 2
TR_WIDTH = 4 * N_FREQ
assert TR_ONE < N_FREQ and TR_WIDTH == LANES

(P_CQ, P_CKV, P_GQ, P_GK, P_FQ, P_FK, P_FB, P_MQ, P_MK, P_DQ, P_DK, P_EQ) = range(12)
P_ROWS = 16


def _np_in_index():
    idx = np.full((PK_END,), _SRC_END, np.int32)
    idx[PK_CQ:PK_CQ + MLA_Q_RANK] = np.arange(_SRC_CQ, _SRC_CQ + MLA_Q_RANK)
    idx[PK_CKV:PK_CKV + MLA_KV_RANK] = np.arange(_SRC_CKV, _SRC_CKV + MLA_KV_RANK)
    for h in range(N_HEADS):
        for d in range(MLA_ROPE):
            idx[PK_KR + _pair_lane(h, d)] = _SRC_KR + d
    idx[PK_FQ:PK_FQ + 3 * GROUP_WIDTH] = np.arange(_SRC_FOX, _SRC_FOX + 3 * GROUP_WIDTH)
    idx[PK_FF:PK_FF + N_HEADS] = np.arange(_SRC_FOXF, _SRC_FOXF + N_HEADS)
    idx[PK_MQ:PK_MQ + 3 * GROUP_WIDTH] = np.arange(_SRC_MOBA, _SRC_MOBA + 3 * GROUP_WIDTH)
    idx[PK_DQ:PK_DQ + 3 * GROUP_WIDTH] = np.arange(_SRC_DIFF, _SRC_DIFF + 3 * GROUP_WIDTH)
    idx[PK_EQ:PK_EQ + GROUP_WIDTH] = np.arange(_SRC_MEMQ, _SRC_MEMQ + GROUP_WIDTH)
    return idx


def _np_uq_index():
    idx = np.full((2 * MXU_WIDTH,), N_HEADS * MLA_QK, np.int32)
    for h in range(N_HEADS):
        for d in range(MLA_QK):
            idx[_pair_lane(h, d)] = h * MLA_QK + d
    return idx


def _np_ukv_index():
    zero = N_HEADS * (MLA_NOPE + HEAD_DIM)
    idx_k = np.full((2 * MXU_WIDTH,), zero, np.int32)
    idx_v = np.zeros((GROUP_WIDTH,), np.int32)
    for h in range(N_HEADS):
        for d in range(MLA_NOPE):
            idx_k[_pair_lane(h, MLA_ROPE + d)] = h * (MLA_NOPE + HEAD_DIM) + d
        for d in range(HEAD_DIM):
            idx_v[h * HEAD_DIM + d] = h * (MLA_NOPE + HEAD_DIM) + MLA_NOPE + d
    return idx_k, idx_v


def _np_group_matrix(groups):
    g = np.zeros((MXU_WIDTH, MXU_WIDTH), np.float32)
    for lo, size in groups:
        g[lo:lo + size, lo:lo + size] = 1.0 / size
    return g


_PAIR_GROUPS = [(0, MLA_ROPE), (MLA_ROPE, MLA_NOPE), (PAIR_STRIDE, MLA_ROPE), (PAIR_STRIDE + MLA_ROPE, MLA_NOPE)]
_G64_GROUPS = [(h * HEAD_DIM, HEAD_DIM) for h in range(N_HEADS)]
_G32_GROUPS = [(g * DIFF_QK, DIFF_QK) for g in range(2 * N_HEADS)]


def _np_rope_expand(regions, rot, base):
    half = rot // 2
    e = np.zeros((TR_WIDTH, 2 * MXU_WIDTH), np.float32)
    e[TR_ONE, 0:MXU_WIDTH] = 1.0
    for lo in regions:
        assert lo % rot == 0
        for r in range(half):
            f = base + r
            for lane, sign in ((lo + r, -1.0), (lo + half + r, 1.0)):
                e[TR_ONE, lane] = 0.0
                e[f, lane] = 1.0
                e[N_FREQ + f, lane] = 1.0
                e[2 * N_FREQ + f, MXU_WIDTH + lane] = sign
                e[3 * N_FREQ + f, MXU_WIDTH + lane] = sign
    return e


def _np_rope_expand_all():
    return np.concatenate([
        _np_rope_expand([0, PAIR_STRIDE], MLA_ROPE, FREQ_BASE_MLA),
        _np_rope_expand([h * HEAD_DIM for h in range(N_HEADS)], ROT_MOBA, FREQ_BASE_MOBA),
        _np_rope_expand([g * DIFF_QK for g in range(2 * N_HEADS)], ROT_DIFF, FREQ_BASE_DIFF),
    ], axis=1)


def _dot(a, b):
    return jnp.dot(a, b, preferred_element_type=jnp.float32)


def _dot_nt(a, b):
    return lax.dot_general(a, b, (((1,), (1,)), ((), ())), preferred_element_type=jnp.float32)


def _split2(a):
    hi = a.astype(jnp.bfloat16)
    lo = (a - hi.astype(jnp.float32)).astype(jnp.bfloat16)
    return hi, lo


def _split3(a):
    hi = a.astype(jnp.bfloat16)
    r = a - hi.astype(jnp.float32)
    mid = r.astype(jnp.bfloat16)
    lo = (r - mid.astype(jnp.float32)).astype(jnp.bfloat16)
    return hi, mid, lo


def _group_mean_sq(a, g_bf16):
    return _dot((a * a).astype(jnp.bfloat16), g_bf16)


def _rope(x, tabs, half):
    w = x.shape[-1]
    lane = lax.broadcasted_iota(jnp.int32, x.shape, 1)
    partner = jnp.where((lane & (2 * half - 1)) >= half, pltpu.roll(x, half, 1), pltpu.roll(x, w - half, 1))
    return x * tabs[:, 0:w] + partner * tabs[:, w:2 * w]


def _lane_mask(shape, lo, hi):
    lane = lax.broadcasted_iota(jnp.int32, shape, len(shape) - 1)
    return (lane >= lo) & (lane < hi)


def _prep_body(x_ref, tr_ref, anorm_ref, win_ref, wuq_ref, wukvk_ref, wukvv_ref, gpair_ref, g64_ref, g32_ref,
               exp_ref, tril_ref, vsel_ref, par_ref,
               aq_ref, ak_ref, av_ref, fq_ref, fk_ref, fv_ref, fdcol_ref, fdrow_ref, mq_ref, mk_ref, mv_ref, msel_ref,
               dq_ref, dk_ref, dv_ref, eq_ref,
               kmean_s, carry_s):
    j = pl.program_id(1)
    tm = x_ref.shape[1]
    bf = jnp.bfloat16

    @pl.when(j == 0)
    def _():
        kmean_s[...] = jnp.zeros_like(kmean_s)
        carry_s[...] = jnp.zeros_like(carry_s)

    x = x_ref[0]
    xn = x * lax.rsqrt(jnp.mean(x * x, axis=-1, keepdims=True) + EPS) * anorm_ref[...]
    xb = xn.astype(bf)

    def proj(off, width):
        return _dot(xb, win_ref[:, off:off + width])

    def prow(r, width=MXU_WIDTH):
        return par_ref[r:r + 1, 0:width]

    gpair, g64, g32 = gpair_ref[...], g64_ref[...], g32_ref[...]
    vsel = vsel_ref[...]

    tabs = _dot(tr_ref[0], exp_ref[...])
    tab_mla = tabs[:, 0:2 * MXU_WIDTH]
    tab_moba = tabs[:, 2 * MXU_WIDTH:4 * MXU_WIDTH]
    tab_diff = tabs[:, 4 * MXU_WIDTH:6 * MXU_WIDTH]
    direct = {name: proj(off, MXU_WIDTH) for name, off in (("mq", PK_MQ), ("mk", PK_MK))}
    p_cq = proj(PK_CQ, MXU_WIDTH)
    p_ckv = proj(PK_CKV, MLA_KV_RANK)
    p_ff = proj(PK_FF, LANES)
    direct.update({name: proj(off, MXU_WIDTH) for name, off in
                   (("fq", PK_FQ), ("fk", PK_FK), ("dq", PK_DQ), ("dk", PK_DK), ("eq", PK_EQ))})
    p_kr = proj(PK_KR, 2 * MXU_WIDTH)
    values = {name: proj(off, MXU_WIDTH).astype(bf) for name, off in
              (("fv", PK_FV), ("mv", PK_MV), ("dv", PK_DV))}

    cqn = p_cq * lax.rsqrt(jnp.sum(p_cq * p_cq, axis=-1, keepdims=True) * (1.0 / MLA_Q_RANK) + EPS) * prow(P_CQ)
    ckvn = p_ckv * lax.rsqrt(jnp.mean(p_ckv * p_ckv, axis=-1, keepdims=True) + EPS) * prow(P_CKV, MLA_KV_RANK)
    ckvb = ckvn.astype(bf)
    z = p_ff + prow(P_FB, LANES)
    log_f = jnp.minimum(z, 0.0) - jnp.log1p(jnp.exp(-jnp.abs(z)))
    log_f = jnp.where(_lane_mask(log_f.shape, 0, N_HEADS), log_f, 0.0)
    l1, l2, l3 = _split3(log_f)

    gmat = {"fq": g64, "fk": g64, "mq": g64, "mk": g64, "dq": g32, "dk": g32, "eq": g64}
    ms = {name: _group_mean_sq(a, gmat[name]) for name, a in direct.items()}
    qa = _dot(cqn.astype(bf), wuq_ref[...])
    ka = p_kr + _dot(ckvb, wukvk_ref[...])
    values["av"] = _dot(ckvb, wukvv_ref[...]).astype(bf)
    tril = tril_ref[...]
    dec = carry_s[...] + ((_dot(tril, l1) + _dot(tril, l2)) + _dot(tril, l3))
    carry_s[...] = dec[tm - 1:tm, :]

    gains = {"fq": P_FQ, "fk": P_FK, "mq": P_MQ, "mk": P_MK, "dq": P_DQ, "dk": P_DK, "eq": P_EQ}
    normed = {name: a * lax.rsqrt(ms[name] + EPS) * prow(gains[name]) for name, a in direct.items()}
    mq = _rope(normed["mq"], tab_moba, ROT_MOBA // 2)
    mk = _rope(normed["mk"], tab_moba, ROT_MOBA // 2)
    kmean_s[pl.ds(j, 1), :] = jnp.mean(mk, axis=0, keepdims=True)
    km_hi, km_lo = _split2(kmean_s[...])
    gates = []
    for h in range(N_HEADS):
        q_hi, q_lo = _split2(jnp.where(_lane_mask(mq.shape, h * HEAD_DIM, (h + 1) * HEAD_DIM), mq, 0.0))
        gates.append((_dot_nt(km_hi, q_hi) + _dot_nt(km_lo, q_hi)) + _dot_nt(km_hi, q_lo))
    ms_qa = [_group_mean_sq(qa[:, p * MXU_WIDTH:(p + 1) * MXU_WIDTH], gpair) for p in range(2)]
    ms_ka = [_group_mean_sq(ka[:, p * MXU_WIDTH:(p + 1) * MXU_WIDTH], gpair) for p in range(2)]
    dec2 = dec * LOG2E
    d1, d2, d3 = _split3(dec2)
    for h in range(N_HEADS):
        fdcol_ref[0, :, h * LANES:(h + 1) * LANES] = jnp.broadcast_to(dec2[:, h:h + 1], (tm, LANES))
    row_sel = jnp.where(lax.broadcasted_iota(jnp.int32, (SUBLANES, LANES), 0)
                        == lax.broadcasted_iota(jnp.int32, (SUBLANES, LANES), 1), 1.0, 0.0).astype(bf)
    fdrow_ref[0] = (_dot_nt(row_sel, d1) + _dot_nt(row_sel, d2)) + _dot_nt(row_sel, d3)
    for name, ref in (("fv", fv_ref), ("mv", mv_ref), ("dv", dv_ref), ("av", av_ref)):
        ref[0] = _values_t(values[name], vsel)

    fq_ref[0] = normed["fq"].astype(bf)
    fk_ref[0] = normed["fk"].astype(bf)
    eq_ref[0] = normed["eq"].astype(bf)
    dq_ref[0] = _rope(normed["dq"], tab_diff, ROT_DIFF // 2).astype(bf)
    dk_ref[0] = _rope(normed["dk"], tab_diff, ROT_DIFF // 2).astype(bf)
    mq_ref[0] = (mq * (LOG2E * HEAD_DIM ** -0.5)).astype(bf)
    mk_ref[0] = mk.astype(bf)
    for p in range(2):
        sl = slice(p * MXU_WIDTH, (p + 1) * MXU_WIDTH)
        aq_ref[0, :, sl] = _rope(qa[:, sl] * lax.rsqrt(ms_qa[p] + EPS) * prow(P_GQ), tab_mla,
                                 MLA_ROPE // 2).astype(bf)
        ak_ref[0, :, sl] = _rope(ka[:, sl] * lax.rsqrt(ms_ka[p] + EPS) * prow(P_GK), tab_mla,
                                 MLA_ROPE // 2).astype(bf)

    nbp = kmean_s.shape[0]
    blk = lax.broadcasted_iota(jnp.int32, (nbp, tm), 0)
    past = blk < j
    for h in range(N_HEADS):
        work = jnp.where(past, gates[h], NEG_INF)
        sel = jnp.zeros((nbp, tm), jnp.bool_)
        for _ in range(MOBA_TOPK):
            mx = jnp.max(work, axis=0, keepdims=True)
            first = jnp.min(jnp.where(work == mx, blk, nbp), axis=0, keepdims=True)
            pick = blk == first
            sel = sel | pick
            work = jnp.where(pick, REMOVED, work)
        msel_ref[0, h * nbp:(h + 1) * nbp, :] = jnp.where(sel & past, 0.0, NEG_INF)


def _full_spec(shape):
    n = len(shape)
    return pl.BlockSpec(shape, lambda *_: (0,) * n)


def _moba_blocks_padded(seq):
    return -(-(seq // MOBA_BLOCK) // SUBLANES) * SUBLANES


def _prep_call(x, tr, anorm, win, wuq, wukvk, wukvv, gpair, g64, g32, expand, tril, vsel, par):
    bsz, seq, d = x.shape
    tm = MOBA_BLOCK
    assert seq % tm == 0
    nbp = _moba_blocks_padded(seq)
    bf = jnp.bfloat16
    f32 = jnp.float32

    vt = -N_HEADS * V_ROWS
    widths = [(2 * MXU_WIDTH, bf), (2 * MXU_WIDTH, bf), (vt, bf),
              (GROUP_WIDTH, bf), (GROUP_WIDTH, bf), (vt, bf),
              (N_HEADS * LANES, f32), (-SUBLANES, f32),
              (GROUP_WIDTH, bf), (GROUP_WIDTH, bf), (vt, bf), (-N_HEADS * nbp, f32),
              (GROUP_WIDTH, bf), (GROUP_WIDTH, bf), (vt, bf),
              (GROUP_WIDTH, bf)]

    def tok(width):
        if width > 0:
            return pl.BlockSpec((1, tm, width), lambda b, j: (b, j, 0))
        return pl.BlockSpec((1, -width, tm), lambda b, j: (b, 0, j))

    def shape(width):
        return (bsz, seq, width) if width > 0 else (bsz, -width, seq)

    consts = [anorm, win, wuq, wukvk, wukvv, gpair, g64, g32, expand, tril, vsel, par]
    return pl.pallas_call(
        _prep_body,
        grid=(bsz, seq // tm),
        in_specs=[tok(d), tok(TR_WIDTH)] + [_full_spec(c.shape) for c in consts],
        out_specs=[tok(w) for w, _ in widths],
        out_shape=[jax.ShapeDtypeStruct(shape(w), dt) for w, dt in widths],
        scratch_shapes=[pltpu.VMEM((nbp, GROUP_WIDTH), jnp.float32), pltpu.VMEM((1, LANES), jnp.float32)],
        compiler_params=pltpu.CompilerParams(dimension_semantics=("arbitrary", "arbitrary"),
                                             vmem_limit_bytes=VMEM_LIMIT_BYTES),
        name="prep",
    )(x, tr, *consts)


def _memkv_body(mem_ref, mnorm_ref, w_ref, g64_ref, gain_ref, vsel_ref, k_ref, v_ref):
    m = mem_ref[0]
    mn = m * lax.rsqrt(jnp.mean(m * m, axis=-1, keepdims=True) + EPS) * mnorm_ref[...]
    kv = _dot(mn.astype(jnp.bfloat16), w_ref[...])
    k = kv[:, 0:GROUP_WIDTH]
    k = k * lax.rsqrt(_group_mean_sq(k, g64_ref[...]) + EPS) * gain_ref[...]
    k_ref[0] = k.astype(jnp.bfloat16)
    v_ref[0] = _values_t(kv[:, GROUP_WIDTH:2 * GROUP_WIDTH].astype(jnp.bfloat16), vsel_ref[...])


def _memkv_call(mem, mnorm, w, g64, gain, vsel):
    bsz, mlen, d = mem.shape
    k_shape, vt_shape = (bsz, mlen, GROUP_WIDTH), (bsz, N_HEADS * V_ROWS, mlen)
    return pl.pallas_call(
        _memkv_body,
        grid=(bsz,),
        in_specs=[pl.BlockSpec((1, mlen, d), lambda b: (b, 0, 0)), _full_spec(mnorm.shape), _full_spec(w.shape),
                  _full_spec(g64.shape), _full_spec(gain.shape), _full_spec(vsel.shape)],
        out_specs=[pl.BlockSpec((1,) + s[1:], lambda b: (b, 0, 0)) for s in (k_shape, vt_shape)],
        out_shape=[jax.ShapeDtypeStruct(s, jnp.bfloat16) for s in (k_shape, vt_shape)],
        compiler_params=pltpu.CompilerParams(dimension_semantics=("arbitrary",), vmem_limit_bytes=VMEM_LIMIT_BYTES),
        name="mem_kv",
    )(mem, mnorm, w, g64, gain, vsel)


class _AttnCfg:
    def __init__(self, name, vheads, n_maps, causal, decay=False, select=False, diff=False):
        self.name = name
        self.vheads = vheads
        self.n_maps = n_maps
        self.causal = causal
        self.decay = decay
        self.select = select
        self.diff = diff


_PLAIN_VHEADS = [(0, h * HEAD_DIM, (h + 1) * HEAD_DIM, 0, h) for h in range(N_HEADS)]
_CFG_MLA = _AttnCfg("attn_mla", [((h // 2) * MXU_WIDTH, (h % 2) * PAIR_STRIDE, (h % 2) * PAIR_STRIDE + MLA_QK, 0, h)
                                 for h in range(N_HEADS)], 1, True)
_CFG_FOX = _AttnCfg("attn_fox", _PLAIN_VHEADS, 1, True, decay=True)
_CFG_MOBA = _AttnCfg("attn_moba", _PLAIN_VHEADS, 1, True, select=True)
_CFG_DIFF = _AttnCfg("attn_diff", [(0, h * HEAD_DIM + c * DIFF_QK, h * HEAD_DIM + (c + 1) * DIFF_QK, c, h)
                                   for c in range(2) for h in range(N_HEADS)], 2, True, diff=True)
_CFG_MEM = _AttnCfg("attn_mem", _PLAIN_VHEADS, 1, False)


ONES_ROW = HEAD_DIM
V_ROWS = HEAD_DIM + 16
QK_LOOKAHEAD = 8
Q_SUB = MXU_WIDTH


def _np_value_select():
    sel = np.zeros((N_HEADS * V_ROWS, GROUP_WIDTH), np.float32)
    for h in range(N_HEADS):
        for d in range(HEAD_DIM):
            sel[h * V_ROWS + d, h * HEAD_DIM + d] = 1.0
    return sel


def _values_t(v, vsel):
    vt = _dot_nt(vsel, v)
    row = lax.broadcasted_iota(jnp.int32, vt.shape, 0)
    ones = row == ONES_ROW
    for h in range(1, N_HEADS):
        ones = ones | (row == h * V_ROWS + ONES_ROW)
    return jnp.where(ones, 1.0, vt).astype(jnp.bfloat16)


def _tile_lanes(x, width):
    return jnp.tile(x, (1, width // LANES)) if width != LANES else x


def _attn_body(cfg, qi_ref, kj_ref, *refs):
    refs = list(refs)
    q_ref, k_ref, vt_ref = refs[:3]
    pos = 3
    if cfg.decay:
        dq_ref, dk_ref = refs[pos:pos + 2]
        pos += 2
    if cfg.select:
        sel_ref = refs[pos]
        pos += 1
    if cfg.diff:
        g64_ref, gsub_ref, lam_ref = refs[pos:pos + 3]
        pos += 3
    o_ref, qm_s, m_s, acc_s = refs[pos:pos + 4]

    t = pl.program_id(1)
    i = qi_ref[t]
    j = kj_ref[t]
    tq = q_ref.shape[1]
    tk = k_ref.shape[1]

    @pl.when(j == 0)
    def _():
        for n, (off, lo, hi, _, _) in enumerate(cfg.vheads):
            qb = q_ref[0, :, off:off + MXU_WIDTH]
            qm_s[n] = jnp.where(_lane_mask(qb.shape, lo, hi), qb, jnp.zeros_like(qb))
        m_s[...] = jnp.full(m_s.shape, NEG_INF, jnp.float32)
        acc_s[...] = jnp.zeros_like(acc_s)

    def step(diag):
        qs = min(tq, Q_SUB)
        items = [(n, u) for n in range(len(cfg.vheads)) for u in range(tq // qs)]

        def n_keys(u):
            return (u + 1) * qs if diag else tk

        def scores(item):
            n, u = item
            off, _, _, _, h = cfg.vheads[n]
            nk = n_keys(u)
            cols = slice(u * qs, (u + 1) * qs)
            s = _dot_nt(k_ref[0, 0:nk, off:off + MXU_WIDTH], qm_s[n, cols, :])
            if cfg.decay:
                s = (dq_ref[0, h:h + 1, cols] - _tile_lanes(dk_ref[0, 0:nk, h * LANES:(h + 1) * LANES], qs)) + s
            if cfg.select:
                nbp = sel_ref.shape[1] // N_HEADS
                qpos = u * qs + lax.broadcasted_iota(jnp.int32, (1, qs), 1)
                parts = []
                for kb in range(nk // MOBA_BLOCK):
                    rows = s[kb * MOBA_BLOCK:(kb + 1) * MOBA_BLOCK, :]
                    if not (diag and kb == nk // MOBA_BLOCK - 1):
                        bias = sel_ref[0, pl.ds(h * nbp + j * (tk // MOBA_BLOCK) + kb, 1), cols]
                        if diag:
                            bias = jnp.where(qpos < (kb + 1) * MOBA_BLOCK, 0.0, bias)
                        rows = rows + bias
                    parts.append(rows)
                s = parts[0] if len(parts) == 1 else jnp.concatenate(parts, axis=0)
            if diag:
                key = lax.broadcasted_iota(jnp.int32, (nk, qs), 0)
                qry = u * qs + lax.broadcasted_iota(jnp.int32, (nk, qs), 1)
                s = jnp.where(key <= qry, s, NEG_INF)
            return s, jnp.max(s, axis=0, keepdims=True)

        raw = {it: scores(items[it]) for it in range(min(QK_LOOKAHEAD, len(items)))}
        for it, (n, u) in enumerate(items):
            h = cfg.vheads[n][4]
            nk = n_keys(u)
            cols = slice(u * qs, (u + 1) * qs)
            s, s_max = raw.pop(it)
            m_prev = m_s[n, :, cols]
            m_new = jnp.maximum(m_prev, s_max)
            alpha = jnp.exp2(m_prev - m_new)
            p = jnp.exp2(s - m_new)
            m_s[n, :, cols] = m_new
            acc_s[n, :, cols] = acc_s[n, :, cols] * alpha + _dot(vt_ref[0, h * V_ROWS:(h + 1) * V_ROWS, 0:nk],
                                                                 p.astype(jnp.bfloat16))
            if it + QK_LOOKAHEAD < len(items):
                raw[it + QK_LOOKAHEAD] = scores(items[it + QK_LOOKAHEAD])

    if cfg.causal:
        pl.when(j < i)(functools.partial(step, False))
        pl.when(j == i)(functools.partial(step, True))
        last = j == i
    else:
        step(False)
        last = j == 0

    @pl.when(last)
    def _():
        outs = []
        for c in range(cfg.n_maps):
            heads = []
            for h in range(N_HEADS):
                acc = acc_s[c * N_HEADS + h]
                heads.append(acc[0:HEAD_DIM, :] / acc[ONES_ROW:ONES_ROW + 1, :])
            outs.append(jnp.concatenate(heads, axis=0).T)
        if cfg.diff:
            o = outs[0] - lam_ref[0:1, :] * outs[1]
            o = o * lax.rsqrt(_group_mean_sq(o, g64_ref[...]) + EPS) * gsub_ref[...]
        else:
            o = outs[0]
        o_ref[0] = o.astype(o_ref.dtype)


def _attn_call(cfg, q, k, v, extras, tq, tk):
    bsz, seq, wq = q.shape
    sk = k.shape[1]
    nq = seq // tq
    assert seq % tq == 0 and sk % tk == 0
    if cfg.causal:
        assert tq == tk and sk == seq
        pairs = [(i, j) for i in range(nq) for j in range(i + 1)]
    else:
        assert sk == tk
        pairs = [(i, 0) for i in range(nq)]
    qi = jnp.asarray(np.array([p[0] for p in pairs], np.int32))
    kj = jnp.asarray(np.array([p[1] for p in pairs], np.int32))
    n_vh = len(cfg.vheads)

    in_specs = [pl.BlockSpec((1, tq, wq), lambda b, t, qi, kj: (b, qi[t], 0)),
                pl.BlockSpec((1, tk, wq), lambda b, t, qi, kj: (b, kj[t], 0)),
                pl.BlockSpec((1, N_HEADS * V_ROWS, tk), lambda b, t, qi, kj: (b, 0, kj[t]))]
    args = [q, k, v]
    if cfg.decay:
        dcol, drow = extras
        in_specs += [pl.BlockSpec((1, SUBLANES, tq), lambda b, t, qi, kj: (b, 0, qi[t])),
                     pl.BlockSpec((1, tk, N_HEADS * LANES), lambda b, t, qi, kj: (b, kj[t], 0))]
        args += [drow, dcol]
    if cfg.select:
        (sel,) = extras
        in_specs += [pl.BlockSpec((1, sel.shape[1], tq), lambda b, t, qi, kj: (b, 0, qi[t]))]
        args += [sel]
    if cfg.diff:
        for c in extras:
            in_specs += [pl.BlockSpec(c.shape, lambda b, t, qi, kj: (0, 0))]
        args += list(extras)

    grid_spec = pltpu.PrefetchScalarGridSpec(
        num_scalar_prefetch=2,
        grid=(bsz, len(pairs)),
        in_specs=in_specs,
        out_specs=pl.BlockSpec((1, tq, GROUP_WIDTH), lambda b, t, qi, kj: (b, qi[t], 0)),
        scratch_shapes=[pltpu.VMEM((n_vh, tq, MXU_WIDTH), jnp.bfloat16),
                        pltpu.VMEM((n_vh, 1, tq), jnp.float32),
                        pltpu.VMEM((n_vh, V_ROWS, tq), jnp.float32)])
    return pl.pallas_call(
        functools.partial(_attn_body, cfg),
        grid_spec=grid_spec,
        out_shape=jax.ShapeDtypeStruct((bsz, seq, GROUP_WIDTH), jnp.bfloat16),
        compiler_params=pltpu.CompilerParams(dimension_semantics=("arbitrary", "arbitrary"),
                                             vmem_limit_bytes=VMEM_LIMIT_BYTES),
        name=cfg.name,
    )(qi, kj, *args)


def _ffn_body(x_ref, oa_ref, ob_ref, oc_ref, od_ref, oe_ref, wo_ref, fnorm_ref, wg_ref, wu_ref, cw_ref, cb_ref, wd_ref,
              out_ref, xnew_s, xn_s, acc_s):
    i = pl.program_id(1)
    f = pl.program_id(2)
    tm = x_ref.shape[1]

    @pl.when(f == 0)
    def _():
        @pl.when(i == 0)
        def _():
            xn_s[0:TAIL_ROWS, :] = jnp.zeros((TAIL_ROWS, xn_s.shape[1]), xn_s.dtype)

        @pl.when(i > 0)
        def _():
            xn_s[0:TAIL_ROWS, :] = xn_s[tm:tm + TAIL_ROWS, :]

        xnew = x_ref[0]
        for g, o_ref in enumerate((oa_ref, ob_ref, oc_ref, od_ref, oe_ref)):
            xnew = xnew + _dot(o_ref[0], wo_ref[g])
        xnew_s[...] = xnew
        xn = xnew * lax.rsqrt(jnp.mean(xnew * xnew, axis=-1, keepdims=True) + EPS) * fnorm_ref[...]
        xn_s[TAIL_ROWS:TAIL_ROWS + tm, :] = xn.astype(xn_s.dtype)
        acc_s[...] = jnp.zeros_like(acc_s)

    ge = _dot(xn_s[...], wg_ref[...])
    u = _dot(xn_s[TAIL_ROWS:TAIL_ROWS + tm, :], wu_ref[...])
    g0 = ge[TAIL_ROWS:TAIL_ROWS + tm, :]
    t1 = ge[TAIL_ROWS - 1:TAIL_ROWS, :]
    t2 = ge[TAIL_ROWS - 2:TAIL_ROWS - 1, :]
    row = lax.broadcasted_iota(jnp.int32, g0.shape, 0)
    g1 = jnp.where(row == 0, t1, pltpu.roll(g0, 1, 0))
    g2 = jnp.where(row == 0, t2, jnp.where(row == 1, t1, pltpu.roll(g0, 2, 0)))
    y = cb_ref[...] + cw_ref[0:1, :] * g2
    y = y + cw_ref[1:2, :] * g1
    y = y + cw_ref[2:3, :] * g0
    hmid = (y * (1.0 / (1.0 + jnp.exp(-y)))) * u
    acc_s[...] += _dot(hmid.astype(jnp.bfloat16), wd_ref[...])

    @pl.when(f == pl.num_programs(2) - 1)
    def _():
        out_ref[0] = xnew_s[...] + acc_s[...]


def _ffn_call(x, outs, wo, fnorm, wg, wu, cw, cb, wd, tm, tf):
    bsz, seq, d = x.shape
    dff = wg.shape[1]
    assert seq % tm == 0 and dff % tf == 0
    tok = lambda w: pl.BlockSpec((1, tm, w), lambda b, i, f: (b, i, 0))
    return pl.pallas_call(
        _ffn_body,
        grid=(bsz, seq // tm, dff // tf),
        in_specs=[tok(d)] + [tok(GROUP_WIDTH)] * 5 + [
            pl.BlockSpec(wo.shape, lambda b, i, f: (0, 0, 0)),
            pl.BlockSpec(fnorm.shape, lambda b, i, f: (0, 0)),
            pl.BlockSpec((d, tf), lambda b, i, f: (0, f)),
            pl.BlockSpec((d, tf), lambda b, i, f: (0, f)),
            pl.BlockSpec((SUBLANES, tf), lambda b, i, f: (0, f)),
            pl.BlockSpec((1, tf), lambda b, i, f: (0, f)),
            pl.BlockSpec((tf, d), lambda b, i, f: (f, 0))],
        out_specs=tok(d),
        out_shape=jax.ShapeDtypeStruct((bsz, seq, d), jnp.float32),
        scratch_shapes=[pltpu.VMEM((tm, d), jnp.float32), pltpu.VMEM((TAIL_ROWS + tm, d), jnp.bfloat16),
                        pltpu.VMEM((tm, d), jnp.float32)],
        compiler_params=pltpu.CompilerParams(dimension_semantics=("arbitrary", "arbitrary", "arbitrary"),
                                             vmem_limit_bytes=VMEM_LIMIT_BYTES),
        name="ffn",
    )(x, *outs, wo, fnorm, wg, wu, cw, cb, wd)


def _pad_row(v, width=MXU_WIDTH):
    v = v.astype(jnp.float32).reshape(-1)
    return jnp.pad(v, (0, width - v.shape[0]))


def _pair_gain(g):
    one = jnp.concatenate([g.astype(jnp.float32), g.astype(jnp.float32),
                           jnp.zeros((MXU_WIDTH - 2 * MLA_QK,), jnp.float32)])
    return one


def _rope_table(positions):
    pos = positions.astype(jnp.float32)[:, :, None]
    inv = [ROPE_THETA ** (-jnp.arange(0, rot, 2, dtype=jnp.float32) / rot) for rot in (MLA_ROPE, ROT_MOBA, ROT_DIFF)]
    inv = jnp.concatenate(inv + [jnp.zeros((N_FREQ - TR_ONE,), jnp.float32)])
    ang = pos * inv
    c, s = jnp.cos(ang), jnp.sin(ang)
    c_hi = c.astype(jnp.bfloat16)
    c_lo = (c - c_hi.astype(jnp.float32)).astype(jnp.bfloat16)
    s_hi = s.astype(jnp.bfloat16)
    s_lo = (s - s_hi.astype(jnp.float32)).astype(jnp.bfloat16)
    return jnp.concatenate([c_hi, c_lo, s_hi, s_lo], axis=-1)


def _pick_tile(n, pref):
    t = pref
    while n % t:
        t //= 2
    return t


def kernel(x, mem, positions, attn_norm, ffn_norm, mem_norm, w_in, mla_cq_norm, mla_ckv_norm, mla_w_uq, mla_w_ukv, mla_q_norm, mla_k_norm, fox_b_f, fox_q_norm, fox_k_norm, moba_q_norm, moba_k_norm, diff_lambda, diff_q_norm, diff_k_norm, diff_sub_norm, mem_w_kv, mem_q_norm, mem_k_norm, w_o, ffn_w_gate, ffn_w_up, ffn_conv_w, ffn_conv_b, ffn_w_down):
    bsz, seq, d = x.shape
    depth = w_in.shape[0]
    dff = ffn_w_gate.shape[2]
    bf = jnp.bfloat16
    f32 = jnp.float32

    in_idx = _np_in_index()
    uq_idx = _np_uq_index()
    ukvk_idx, ukvv_idx = _np_ukv_index()
    gpair = jnp.asarray(_np_group_matrix(_PAIR_GROUPS), bf)
    g64 = jnp.asarray(_np_group_matrix(_G64_GROUPS), bf)
    g32 = jnp.asarray(_np_group_matrix(_G32_GROUPS), bf)
    expand = jnp.asarray(_np_rope_expand_all(), bf)
    tril = jnp.asarray(np.tril(np.ones((MOBA_BLOCK, MOBA_BLOCK), np.float32)), bf)
    vsel = jnp.asarray(_np_value_select(), bf)
    tr = _rope_table(positions)

    t_dense = _pick_tile(seq, 1024)
    t_ffn = _pick_tile(seq, 512)
    tf = dff // 2 if (dff // 2) % LANES == 0 else dff

    for l in range(depth):
        win = jnp.concatenate([w_in[l], jnp.zeros((d, 1), f32)], axis=1)[:, in_idx].astype(bf)
        wuq = jnp.concatenate([mla_w_uq[l], jnp.zeros((MLA_Q_RANK, 1), f32)], axis=1)[:, uq_idx]
        wuq = jnp.pad(wuq, ((0, MXU_WIDTH - MLA_Q_RANK), (0, 0))).astype(bf)
        wukv = jnp.concatenate([mla_w_ukv[l], jnp.zeros((MLA_KV_RANK, 1), f32)], axis=1)
        wukvk = wukv[:, ukvk_idx].astype(bf)
        wukvv = wukv[:, ukvv_idx].astype(bf)
        rows = [None] * P_ROWS
        rows[P_CQ] = _pad_row(mla_cq_norm[l])
        rows[P_CKV] = _pad_row(mla_ckv_norm[l])
        rows[P_GQ] = _pair_gain(mla_q_norm[l]) * (LOG2E * MLA_QK ** -0.5)
        rows[P_GK] = _pair_gain(mla_k_norm[l])
        rows[P_FQ] = jnp.tile(fox_q_norm[l].astype(f32), N_HEADS) * (LOG2E * HEAD_DIM ** -0.5)
        rows[P_FK] = jnp.tile(fox_k_norm[l].astype(f32), N_HEADS)
        rows[P_FB] = _pad_row(fox_b_f[l])
        rows[P_MQ] = jnp.tile(moba_q_norm[l].astype(f32), N_HEADS)
        rows[P_MK] = jnp.tile(moba_k_norm[l].astype(f32), N_HEADS)
        rows[P_DQ] = jnp.tile(diff_q_norm[l].astype(f32), 2 * N_HEADS) * (LOG2E * DIFF_QK ** -0.5)
        rows[P_DK] = jnp.tile(diff_k_norm[l].astype(f32), 2 * N_HEADS)
        rows[P_EQ] = jnp.tile(mem_q_norm[l].astype(f32), N_HEADS) * (LOG2E * HEAD_DIM ** -0.5)
        par = jnp.stack([r if r is not None else jnp.zeros((MXU_WIDTH,), f32) for r in rows])

        (aq, ak, av, fq, fk, fv, fdcol, fdrow, mq, mk, mv, msel, dq, dk, dv, eq) = _prep_call(
            x, tr, attn_norm[l].reshape(1, d).astype(f32), win, wuq, wukvk, wukvv, gpair, g64, g32, expand, tril,
            vsel, par)
        ek, ev = _memkv_call(mem, mem_norm[l].reshape(1, d).astype(f32), mem_w_kv[l].astype(bf), g64,
                             jnp.tile(mem_k_norm[l].astype(f32), N_HEADS).reshape(1, GROUP_WIDTH), vsel)

        lam_vec = diff_lambda[l].astype(f32)
        lam_init = 0.8 - 0.6 * math.exp(-0.3 * l)
        lam = (jnp.exp(jnp.sum(lam_vec[0] * lam_vec[1])) - jnp.exp(jnp.sum(lam_vec[2] * lam_vec[3])) + lam_init)
        lam_row = jnp.full((1, GROUP_WIDTH), 1.0, f32) * lam
        gsub = (jnp.tile(diff_sub_norm[l].astype(f32), N_HEADS) * (1.0 - lam_init)).reshape(1, GROUP_WIDTH)

        o_a = _attn_call(_CFG_MLA, aq, ak, av, (), t_dense, t_dense)
        o_b = _attn_call(_CFG_FOX, fq, fk, fv, (fdcol, fdrow), t_dense, t_dense)
        t_moba = max(t_dense, MOBA_BLOCK)
        o_c = _attn_call(_CFG_MOBA, mq, mk, mv, (msel,), t_moba, t_moba)
        o_d = _attn_call(_CFG_DIFF, dq, dk, dv, (g64, gsub, lam_row), t_dense, t_dense)
        o_e = _attn_call(_CFG_MEM, eq, ek, ev, (), t_dense, mem.shape[1])

        cw = jnp.pad(ffn_conv_w[l].astype(f32), ((0, SUBLANES - CONV_WIDTH), (0, 0)))
        x = _ffn_call(x, (o_a, o_b, o_c, o_d, o_e), w_o[l].reshape(5, GROUP_WIDTH, d).astype(bf),
                      ffn_norm[l].reshape(1, d).astype(f32), ffn_w_gate[l].astype(bf), ffn_w_up[l].astype(bf),
                      cw, ffn_conv_b[l].reshape(1, dff).astype(f32), ffn_w_down[l].astype(bf), t_ffn, tf)
    return x
```

```python
import functools
import math

import numpy as np
import jax
import jax.numpy as jnp
from jax import lax
from jax.experimental import pallas as pl
from jax.experimental.pallas import tpu as pltpu

N_HEADS = 4
HEAD_DIM = 64
GROUP_WIDTH = N_HEADS * HEAD_DIM
MLA_Q_RANK = 192
MLA_KV_RANK = 128
MLA_NOPE = 64
MLA_ROPE = 32
MLA_QK = MLA_NOPE + MLA_ROPE
DIFF_QK = HEAD_DIM // 2
ROPE_THETA = 500000.0
ROT_MOBA = HEAD_DIM // 4
ROT_DIFF = DIFF_QK // 4
MOBA_BLOCK = 256
MOBA_TOPK = 3
CONV_WIDTH = 3
EPS = 1e-6
NEG_INF = -1e30
LOG2E = math.log2(math.e)
REMOVED = -3e38

LANES = 128
SUBLANES = 8
MXU_WIDTH = 256
TAIL_ROWS = 16
VMEM_LIMIT_BYTES = 56 * 1024 * 1024

_SRC_CQ = 0
_SRC_CKV = _SRC_CQ + MLA_Q_RANK
_SRC_KR = _SRC_CKV + MLA_KV_RANK
_SRC_FOX = _SRC_KR + MLA_ROPE
_SRC_FOXF = _SRC_FOX + 3 * GROUP_WIDTH
_SRC_MOBA = _SRC_FOXF + N_HEADS
_SRC_DIFF = _SRC_MOBA + 3 * GROUP_WIDTH
_SRC_MEMQ = _SRC_DIFF + 3 * GROUP_WIDTH
_SRC_END = _SRC_MEMQ + GROUP_WIDTH

PK_CQ = 0
PK_CKV = 256
PK_KR = 384
PK_FQ, PK_FK, PK_FV = 896, 1152, 1408
PK_FF = 1664
PK_MQ, PK_MK, PK_MV = 1792, 2048, 2304
PK_DQ, PK_DK, PK_DV = 2560, 2816, 3072
PK_EQ = 3328
PK_END = 3584

PAIR_STRIDE = MLA_QK


def _pair_lane(h, d):
    return (h // 2) * MXU_WIDTH + (h % 2) * PAIR_STRIDE + d


N_FREQ = 32
FREQ_BASE_MLA = 0
FREQ_BASE_MOBA = MLA_ROPE // 2
FREQ_BASE_DIFF = FREQ_BASE_MOBA + ROT_MOBA // 2
TR_ONE = FREQ_BASE_DIFF + ROT_DIFF // 2
TR_WIDTH = 4 * N_FREQ
assert TR_ONE < N_FREQ and TR_WIDTH == LANES

(P_CQ, P_CKV, P_GQ, P_GK, P_FQ, P_FK, P_FB, P_MQ, P_MK, P_DQ, P_DK, P_EQ) = range(12)
P_ROWS = 16


def _np_in_index():
    idx = np.full((PK_END,), _SRC_END, np.int32)
    idx[PK_CQ:PK_CQ + MLA_Q_RANK] = np.arange(_SRC_CQ, _SRC_CQ + MLA_Q_RANK)
    idx[PK_CKV:PK_CKV + MLA_KV_RANK] = np.arange(_SRC_CKV, _SRC_CKV + MLA_KV_RANK)
    for h in range(N_HEADS):
        for d in range(MLA_ROPE):
            idx[PK_KR + _pair_lane(h, d)] = _SRC_KR + d
    idx[PK_FQ:PK_FQ + 3 * GROUP_WIDTH] = np.arange(_SRC_FOX, _SRC_FOX + 3 * GROUP_WIDTH)
    idx[PK_FF:PK_FF + N_HEADS] = np.arange(_SRC_FOXF, _SRC_FOXF + N_HEADS)
    idx[PK_MQ:PK_MQ + 3 * GROUP_WIDTH] = np.arange(_SRC_MOBA, _SRC_MOBA + 3 * GROUP_WIDTH)
    idx[PK_DQ:PK_DQ + 3 * GROUP_WIDTH] = np.arange(_SRC_DIFF, _SRC_DIFF + 3 * GROUP_WIDTH)
    idx[PK_EQ:PK_EQ + GROUP_WIDTH] = np.arange(_SRC_MEMQ, _SRC_MEMQ + GROUP_WIDTH)
    return idx


def _np_uq_index():
    idx = np.full((2 * MXU_WIDTH,), N_HEADS * MLA_QK, np.int32)
    for h in range(N_HEADS):
        for d in range(MLA_QK):
            idx[_pair_lane(h, d)] = h * MLA_QK + d
    return idx


def _np_ukv_index():
    zero = N_HEADS * (MLA_NOPE + HEAD_DIM)
    idx_k = np.full((2 * MXU_WIDTH,), zero, np.int32)
    idx_v = np.zeros((GROUP_WIDTH,), np.int32)
    for h in range(N_HEADS):
        for d in range(MLA_NOPE):
            idx_k[_pair_lane(h, MLA_ROPE + d)] = h * (MLA_NOPE + HEAD_DIM) + d
        for d in range(HEAD_DIM):
            idx_v[h * HEAD_DIM + d] = h * (MLA_NOPE + HEAD_DIM) + MLA_NOPE + d
    return idx_k, idx_v


def _np_group_matrix(groups):
    g = np.zeros((MXU_WIDTH, MXU_WIDTH), np.float32)
    for lo, size in groups:
        g[lo:lo + size, lo:lo + size] = 1.0 / size
    return g


_PAIR_GROUPS = [(0, MLA_ROPE), (MLA_ROPE, MLA_NOPE), (PAIR_STRIDE, MLA_ROPE), (PAIR_STRIDE + MLA_ROPE, MLA_NOPE)]
_G64_GROUPS = [(h * HEAD_DIM, HEAD_DIM) for h in range(N_HEADS)]
_G32_GROUPS = [(g * DIFF_QK, DIFF_QK) for g in range(2 * N_HEADS)]


def _np_rope_expand(regions, rot, base):
    half = rot // 2
    e = np.zeros((TR_WIDTH, 2 * MXU_WIDTH), np.float32)
    e[TR_ONE, 0:MXU_WIDTH] = 1.0
    for lo in regions:
        assert lo % rot == 0
        for r in range(half):
            f = base + r
            for lane, sign in ((lo + r, -1.0), (lo + half + r, 1.0)):
                e[TR_ONE, lane] = 0.0
                e[f, lane] = 1.0
                e[N_FREQ + f, lane] = 1.0
                e[2 * N_FREQ + f, MXU_WIDTH + lane] = sign
                e[3 * N_FREQ + f, MXU_WIDTH + lane] = sign
    return e


def _np_rope_expand_all():
    return np.concatenate([
        _np_rope_expand([0, PAIR_STRIDE], MLA_ROPE, FREQ_BASE_MLA),
        _np_rope_expand([h * HEAD_DIM for h in range(N_HEADS)], ROT_MOBA, FREQ_BASE_MOBA),
        _np_rope_expand([g * DIFF_QK for g in range(2 * N_HEADS)], ROT_DIFF, FREQ_BASE_DIFF),
    ], axis=1)


def _dot(a, b):
    return jnp.dot(a, b, preferred_element_type=jnp.float32)


def _dot_nt(a, b):
    return lax.dot_general(a, b, (((1,), (1,)), ((), ())), preferred_element_type=jnp.float32)


def _split2(a):
    hi = a.astype(jnp.bfloat16)
    lo = (a - hi.astype(jnp.float32)).astype(jnp.bfloat16)
    return hi, lo


def _split3(a):
    hi = a.astype(jnp.bfloat16)
    r = a - hi.astype(jnp.float32)
    mid = r.astype(jnp.bfloat16)
    lo = (r - mid.astype(jnp.float32)).astype(jnp.bfloat16)
    return hi, mid, lo


def _group_mean_sq(a, g_bf16):
    return _dot((a * a).astype(jnp.bfloat16), g_bf16)


def _rope(x, tabs, half):
    w = x.shape[-1]
    lane = lax.broadcasted_iota(jnp.int32, x.shape, 1)
    partner = jnp.where((lane & (2 * half - 1)) >= half, pltpu.roll(x, half, 1), pltpu.roll(x, w - half, 1))
    return x * tabs[:, 0:w] + partner * tabs[:, w:2 * w]


def _lane_mask(shape, lo, hi):
    lane = lax.broadcasted_iota(jnp.int32, shape, len(shape) - 1)
    return (lane >= lo) & (lane < hi)


def _prep_body(x_ref, tr_ref, anorm_ref, win_ref, wuq_ref, wukvk_ref, wukvv_ref, gpair_ref, g64_ref, g32_ref,
               exp_ref, tril_ref, vsel_ref, par_ref,
               aq_ref, ak_ref, av_ref, fq_ref, fk_ref, fv_ref, fdcol_ref, fdrow_ref, mq_ref, mk_ref, mv_ref, msel_ref,
               dq_ref, dk_ref, dv_ref, eq_ref,
               kmean_s, carry_s):
    j = pl.program_id(1)
    tm = x_ref.shape[1]
    bf = jnp.bfloat16

    @pl.when(j == 0)
    def _():
        kmean_s[...] = jnp.zeros_like(kmean_s)
        carry_s[...] = jnp.zeros_like(carry_s)

    x = x_ref[0]
    xn = x * lax.rsqrt(jnp.mean(x * x, axis=-1, keepdims=True) + EPS) * anorm_ref[...]
    xb = xn.astype(bf)

    def proj(off, width):
        return _dot(xb, win_ref[:, off:off + width])

    def prow(r, width=MXU_WIDTH):
        return par_ref[r:r + 1, 0:width]

    gpair, g64, g32 = gpair_ref[...], g64_ref[...], g32_ref[...]
    vsel = vsel_ref[...]

    tabs = _dot(tr_ref[0], exp_ref[...])
    tab_mla = tabs[:, 0:2 * MXU_WIDTH]
    tab_moba = tabs[:, 2 * MXU_WIDTH:4 * MXU_WIDTH]
    tab_diff = tabs[:, 4 * MXU_WIDTH:6 * MXU_WIDTH]
    direct = {name: proj(off, MXU_WIDTH) for name, off in (("mq", PK_MQ), ("mk", PK_MK))}
    p_cq = proj(PK_CQ, MXU_WIDTH)
    p_ckv = proj(PK_CKV, MLA_KV_RANK)
    p_ff = proj(PK_FF, LANES)
    direct.update({name: proj(off, MXU_WIDTH) for name, off in
                   (("fq", PK_FQ), ("fk", PK_FK), ("dq", PK_DQ), ("dk", PK_DK), ("eq", PK_EQ))})
    p_kr = proj(PK_KR, 2 * MXU_WIDTH)
    values = {name: proj(off, MXU_WIDTH).astype(bf) for name, off in
              (("fv", PK_FV), ("mv", PK_MV), ("dv", PK_DV))}

    cqn = p_cq * lax.rsqrt(jnp.sum(p_cq * p_cq, axis=-1, keepdims=True) * (1.0 / MLA_Q_RANK) + EPS) * prow(P_CQ)
    ckvn = p_ckv * lax.rsqrt(jnp.mean(p_ckv * p_ckv, axis=-1, keepdims=True) + EPS) * prow(P_CKV, MLA_KV_RANK)
    ckvb = ckvn.astype(bf)
    z = p_ff + prow(P_FB, LANES)
    log_f = jnp.minimum(z, 0.0) - jnp.log1p(jnp.exp(-jnp.abs(z)))
    log_f = jnp.where(_lane_mask(log_f.shape, 0, N_HEADS), log_f, 0.0)
    l1, l2, l3 = _split3(log_f)

    gmat = {"fq": g64, "fk": g64, "mq": g64, "mk": g64, "dq": g32, "dk": g32, "eq": g64}
    ms = {name: _group_mean_sq(a, gmat[name]) for name, a in direct.items()}
    qa = _dot(cqn.astype(bf), wuq_ref[...])
    ka = p_kr + _dot(ckvb, wukvk_ref[...])
    values["av"] = _dot(ckvb, wukvv_ref[...]).astype(bf)
    tril = tril_ref[...]
    dec = carry_s[...] + ((_dot(tril, l1) + _dot(tril, l2)) + _dot(tril, l3))
    carry_s[...] = dec[tm - 1:tm, :]

    gains = {"fq": P_FQ, "fk": P_FK, "mq": P_MQ, "mk": P_MK, "dq": P_DQ, "dk": P_DK, "eq": P_EQ}
    normed = {name: a * lax.rsqrt(ms[name] + EPS) * prow(gains[name]) for name, a in direct.items()}
    mq = _rope(normed["mq"], tab_moba, ROT_MOBA // 2)
    mk = _rope(normed["mk"], tab_moba, ROT_MOBA // 2)
    kmean_s[pl.ds(j, 1), :] = jnp.mean(mk, axis=0, keepdims=True)
    km_hi, km_lo = _split2(kmean_s[...])
    gates = []
    for h in range(N_HEADS):
        q_hi, q_lo = _split2(jnp.where(_lane_mask(mq.shape, h * HEAD_DIM, (h + 1) * HEAD_DIM), mq, 0.0))
        gates.append((_dot_nt(km_hi, q_hi) + _dot_nt(km_lo, q_hi)) + _dot_nt(km_hi, q_lo))
    ms_qa = [_group_mean_sq(qa[:, p * MXU_WIDTH:(p + 1) * MXU_WIDTH], gpair) for p in range(2)]
    ms_ka = [_group_mean_sq(ka[:, p * MXU_WIDTH:(p + 1) * MXU_WIDTH], gpair) for p in range(2)]
    dec2 = dec * LOG2E
    d1, d2, d3 = _split3(dec2)
    for h in range(N_HEADS):
        fdcol_ref[0, :, h * LANES:(h + 1) * LANES] = jnp.broadcast_to(dec2[:, h:h + 1], (tm, LANES))
    row_sel = jnp.where(lax.broadcasted_iota(jnp.int32, (SUBLANES, LANES), 0)
                        == lax.broadcasted_iota(jnp.int32, (SUBLANES, LANES), 1), 1.0, 0.0).astype(bf)
    fdrow_ref[0] = (_dot_nt(row_sel, d1) + _dot_nt(row_sel, d2)) + _dot_nt(row_sel, d3)
    for name, ref in (("fv", fv_ref), ("mv", mv_ref), ("dv", dv_ref), ("av", av_ref)):
        ref[0] = _values_t(values[name], vsel)

    fq_ref[0] = normed["fq"].astype(bf)
    fk_ref[0] = normed["fk"].astype(bf)
    eq_ref[0] = normed["eq"].astype(bf)
    dq_ref[0] = _rope(normed["dq"], tab_diff, ROT_DIFF // 2).astype(bf)
    dk_ref[0] = _rope(normed["dk"], tab_diff, ROT_DIFF // 2).astype(bf)
    mq_ref[0] = (mq * (LOG2E * HEAD_DIM ** -0.5)).astype(bf)
    mk_ref[0] = mk.astype(bf)
    for p in range(2):
        sl = slice(p * MXU_WIDTH, (p + 1) * MXU_WIDTH)
        aq_ref[0, :, sl] = _rope(qa[:, sl] * lax.rsqrt(ms_qa[p] + EPS) * prow(P_GQ), tab_mla,
                                 MLA_ROPE // 2).astype(bf)
        ak_ref[0, :, sl] = _rope(ka[:, sl] * lax.rsqrt(ms_ka[p] + EPS) * prow(P_GK), tab_mla,
                                 MLA_ROPE // 2).astype(bf)

    nbp = kmean_s.shape[0]
    blk = lax.broadcasted_iota(jnp.int32, (nbp, tm), 0)
    past = blk < j
    for h in range(N_HEADS):
        work = jnp.where(past, gates[h], NEG_INF)
        sel = jnp.zeros((nbp, tm), jnp.bool_)
        for _ in range(MOBA_TOPK):
            mx = jnp.max(work, axis=0, keepdims=True)
            first = jnp.min(jnp.where(work == mx, blk, nbp), axis=0, keepdims=True)
            pick = blk == first
            sel = sel | pick
            work = jnp.where(pick, REMOVED, work)
        msel_ref[0, h * nbp:(h + 1) * nbp, :] = jnp.where(sel & past, 0.0, NEG_INF)


def _const_spec(a, layer=None):
    if layer is None:
        n = a.ndim
        return pl.BlockSpec(a.shape, lambda *_: (0,) * n)
    n = a.ndim - 1
    return pl.BlockSpec((None,) + a.shape[1:], lambda *_: (layer,) + (0,) * n)


def _moba_blocks_padded(seq):
    return -(-(seq // MOBA_BLOCK) // SUBLANES) * SUBLANES


def _prep_call(layer, x, tr, anorm, win, wuq, wukvk, wukvv, gpair, g64, g32, expand, tril, vsel, par):
    bsz, seq, d = x.shape
    tm = MOBA_BLOCK
    assert seq % tm == 0
    nbp = _moba_blocks_padded(seq)
    bf = jnp.bfloat16
    f32 = jnp.float32

    vt = -N_HEADS * V_ROWS
    widths = [(2 * MXU_WIDTH, bf), (2 * MXU_WIDTH, bf), (vt, bf),
              (GROUP_WIDTH, bf), (GROUP_WIDTH, bf), (vt, bf),
              (N_HEADS * LANES, f32), (-SUBLANES, f32),
              (GROUP_WIDTH, bf), (GROUP_WIDTH, bf), (vt, bf), (-N_HEADS * nbp, f32),
              (GROUP_WIDTH, bf), (GROUP_WIDTH, bf), (vt, bf),
              (GROUP_WIDTH, bf)]

    def tok(width):
        if width > 0:
            return pl.BlockSpec((1, tm, width), lambda b, j: (b, j, 0))
        return pl.BlockSpec((1, -width, tm), lambda b, j: (b, 0, j))

    def shape(width):
        return (bsz, seq, width) if width > 0 else (bsz, -width, seq)

    consts = [anorm, win, wuq, wukvk, wukvv, gpair, g64, g32, expand, tril, vsel, par]
    layered = [True, True, True, True, True, False, False, False, False, False, False, True]
    return pl.pallas_call(
        _prep_body,
        grid=(bsz, seq // tm),
        in_specs=[tok(d), tok(TR_WIDTH)] + [_const_spec(c, layer if ly else None) for c, ly in zip(consts, layered)],
        out_specs=[tok(w) for w, _ in widths],
        out_shape=[jax.ShapeDtypeStruct(shape(w), dt) for w, dt in widths],
        scratch_shapes=[pltpu.VMEM((nbp, GROUP_WIDTH), jnp.float32), pltpu.VMEM((1, LANES), jnp.float32)],
        compiler_params=pltpu.CompilerParams(dimension_semantics=("arbitrary", "arbitrary"),
                                             vmem_limit_bytes=VMEM_LIMIT_BYTES),
        name="prep",
    )(x, tr, *consts)


def _memkv_body(mem_ref, mnorm_ref, w_ref, g64_ref, gain_ref, vsel_ref, k_ref, v_ref):
    m = mem_ref[0]
    mn = m * lax.rsqrt(jnp.mean(m * m, axis=-1, keepdims=True) + EPS) * mnorm_ref[...]
    kv = _dot(mn.astype(jnp.bfloat16), w_ref[...])
    k = kv[:, 0:GROUP_WIDTH]
    k = k * lax.rsqrt(_group_mean_sq(k, g64_ref[...]) + EPS) * gain_ref[...]
    k_ref[0] = k.astype(jnp.bfloat16)
    v_ref[0] = _values_t(kv[:, GROUP_WIDTH:2 * GROUP_WIDTH].astype(jnp.bfloat16), vsel_ref[...])


def _memkv_call(layer, mem, mnorm, w, g64, gain, vsel):
    bsz, mlen, d = mem.shape
    k_shape, vt_shape = (bsz, mlen, GROUP_WIDTH), (bsz, N_HEADS * V_ROWS, mlen)
    return pl.pallas_call(
        _memkv_body,
        grid=(bsz,),
        in_specs=[pl.BlockSpec((1, mlen, d), lambda b: (b, 0, 0)), _const_spec(mnorm, layer), _const_spec(w, layer),
                  _const_spec(g64), _const_spec(gain, layer), _const_spec(vsel)],
        out_specs=[pl.BlockSpec((1,) + s[1:], lambda b: (b, 0, 0)) for s in (k_shape, vt_shape)],
        out_shape=[jax.ShapeDtypeStruct(s, jnp.bfloat16) for s in (k_shape, vt_shape)],
        compiler_params=pltpu.CompilerParams(dimension_semantics=("arbitrary",), vmem_limit_bytes=VMEM_LIMIT_BYTES),
        name="mem_kv",
    )(mem, mnorm, w, g64, gain, vsel)


class _AttnCfg:
    def __init__(self, name, vheads, n_maps, causal, decay=False, select=False, diff=False):
        self.name = name
        self.vheads = vheads
        self.n_maps = n_maps
        self.causal = causal
        self.decay = decay
        self.select = select
        self.diff = diff


_PLAIN_VHEADS = [(0, h * HEAD_DIM, (h + 1) * HEAD_DIM, 0, h) for h in range(N_HEADS)]
_CFG_MLA = _AttnCfg("attn_mla", [((h // 2) * MXU_WIDTH, (h % 2) * PAIR_STRIDE, (h % 2) * PAIR_STRIDE + MLA_QK, 0, h)
                                 for h in range(N_HEADS)], 1, True)
_CFG_FOX = _AttnCfg("attn_fox", _PLAIN_VHEADS, 1, True, decay=True)
_CFG_MOBA = _AttnCfg("attn_moba", _PLAIN_VHEADS, 1, True, select=True)
_CFG_DIFF = _AttnCfg("attn_diff", [(0, h * HEAD_DIM + c * DIFF_QK, h * HEAD_DIM + (c + 1) * DIFF_QK, c, h)
                                   for c in range(2) for h in range(N_HEADS)], 2, True, diff=True)
_CFG_MEM = _AttnCfg("attn_mem", _PLAIN_VHEADS, 1, False)


ONES_ROW = HEAD_DIM
V_ROWS = HEAD_DIM + 16
QK_LOOKAHEAD = 8
Q_SUB = 2 * MXU_WIDTH


def _np_value_select():
    sel = np.zeros((N_HEADS * V_ROWS, GROUP_WIDTH), np.float32)
    for h in range(N_HEADS):
        for d in range(HEAD_DIM):
            sel[h * V_ROWS + d, h * HEAD_DIM + d] = 1.0
    return sel


def _values_t(v, vsel):
    vt = _dot_nt(vsel, v)
    row = lax.broadcasted_iota(jnp.int32, vt.shape, 0)
    ones = row == ONES_ROW
    for h in range(1, N_HEADS):
        ones = ones | (row == h * V_ROWS + ONES_ROW)
    return jnp.where(ones, 1.0, vt).astype(jnp.bfloat16)


def _tile_lanes(x, width):
    return jnp.tile(x, (1, width // LANES)) if width != LANES else x


def _attn_body(cfg, qi_ref, kj_ref, *refs):
    refs = list(refs)
    q_ref, k_ref, vt_ref = refs[:3]
    pos = 3
    if cfg.decay:
        dq_ref, dk_ref = refs[pos:pos + 2]
        pos += 2
    if cfg.select:
        sel_ref = refs[pos]
        pos += 1
    if cfg.diff:
        g64_ref, gsub_ref, lam_ref = refs[pos:pos + 3]
        pos += 3
    o_ref, qm_s, m_s, acc_s = refs[pos:pos + 4]

    t = pl.program_id(1)
    i = qi_ref[t]
    j = kj_ref[t]
    tq = q_ref.shape[1]
    tk = k_ref.shape[1]

    @pl.when(j == 0)
    def _():
        for n, (off, lo, hi, _, _) in enumerate(cfg.vheads):
            qb = q_ref[0, :, off:off + MXU_WIDTH]
            qm_s[n] = jnp.where(_lane_mask(qb.shape, lo, hi), qb, jnp.zeros_like(qb))
        m_s[...] = jnp.full(m_s.shape, NEG_INF, jnp.float32)
        acc_s[...] = jnp.zeros_like(acc_s)

    def step(diag):
        qs = min(tq, Q_SUB)
        items = [(n, u) for n in range(len(cfg.vheads)) for u in range(tq // qs)]

        def n_keys(u):
            return (u + 1) * qs if diag else tk

        def scores(item):
            n, u = item
            off, _, _, _, h = cfg.vheads[n]
            nk = n_keys(u)
            cols = slice(u * qs, (u + 1) * qs)
            s = _dot_nt(k_ref[0, 0:nk, off:off + MXU_WIDTH], qm_s[n, cols, :])
            if cfg.decay:
                s = (dq_ref[0, h:h + 1, cols] - _tile_lanes(dk_ref[0, 0:nk, h * LANES:(h + 1) * LANES], qs)) + s
            if cfg.select:
                nbp = sel_ref.shape[1] // N_HEADS
                qpos = u * qs + lax.broadcasted_iota(jnp.int32, (1, qs), 1)
                parts = []
                for kb in range(nk // MOBA_BLOCK):
                    rows = s[kb * MOBA_BLOCK:(kb + 1) * MOBA_BLOCK, :]
                    if not (diag and kb == nk // MOBA_BLOCK - 1):
                        bias = sel_ref[0, pl.ds(h * nbp + j * (tk // MOBA_BLOCK) + kb, 1), cols]
                        if diag:
                            bias = jnp.where(qpos < (kb + 1) * MOBA_BLOCK, 0.0, bias)
                        rows = rows + bias
                    parts.append(rows)
                s = parts[0] if len(parts) == 1 else jnp.concatenate(parts, axis=0)
            if diag:
                key = lax.broadcasted_iota(jnp.int32, (nk, qs), 0)
                qry = u * qs + lax.broadcasted_iota(jnp.int32, (nk, qs), 1)
                s = jnp.where(key <= qry, s, NEG_INF)
            return s, jnp.max(s, axis=0, keepdims=True)

        raw = {it: scores(items[it]) for it in range(min(QK_LOOKAHEAD, len(items)))}
        for it, (n, u) in enumerate(items):
            h = cfg.vheads[n][4]
            nk = n_keys(u)
            cols = slice(u * qs, (u + 1) * qs)
            s, s_max = raw.pop(it)
            m_prev = m_s[n, :, cols]
            m_new = jnp.maximum(m_prev, s_max)
            alpha = jnp.exp2(m_prev - m_new)
            p = jnp.exp2(s - m_new)
            m_s[n, :, cols] = m_new
            acc_s[n, :, cols] = acc_s[n, :, cols] * alpha + _dot(vt_ref[0, h * V_ROWS:(h + 1) * V_ROWS, 0:nk],
                                                                 p.astype(jnp.bfloat16))
            if it + QK_LOOKAHEAD < len(items):
                raw[it + QK_LOOKAHEAD] = scores(items[it + QK_LOOKAHEAD])

    if cfg.causal:
        pl.when(j < i)(functools.partial(step, False))
        pl.when(j == i)(functools.partial(step, True))
        last = j == i
    else:
        step(False)
        last = j == 0

    @pl.when(last)
    def _():
        outs = []
        for c in range(cfg.n_maps):
            heads = []
            for h in range(N_HEADS):
                acc = acc_s[c * N_HEADS + h]
                heads.append(acc[0:HEAD_DIM, :] / acc[ONES_ROW:ONES_ROW + 1, :])
            outs.append(jnp.concatenate(heads, axis=0).T)
        if cfg.diff:
            o = outs[0] - lam_ref[0:1, :] * outs[1]
            o = o * lax.rsqrt(_group_mean_sq(o, g64_ref[...]) + EPS) * gsub_ref[...]
        else:
            o = outs[0]
        o_ref[0] = o.astype(o_ref.dtype)


def _attn_call(cfg, q, k, v, extras, tq, tk):
    bsz, seq, wq = q.shape
    sk = k.shape[1]
    nq = seq // tq
    assert seq % tq == 0 and sk % tk == 0
    if cfg.causal:
        assert tq == tk and sk == seq
        pairs = [(i, j) for i in range(nq) for j in range(i + 1)]
    else:
        assert sk == tk
        pairs = [(i, 0) for i in range(nq)]
    qi = jnp.asarray(np.array([p[0] for p in pairs], np.int32))
    kj = jnp.asarray(np.array([p[1] for p in pairs], np.int32))
    n_vh = len(cfg.vheads)

    in_specs = [pl.BlockSpec((1, tq, wq), lambda b, t, qi, kj: (b, qi[t], 0)),
                pl.BlockSpec((1, tk, wq), lambda b, t, qi, kj: (b, kj[t], 0)),
                pl.BlockSpec((1, N_HEADS * V_ROWS, tk), lambda b, t, qi, kj: (b, 0, kj[t]))]
    args = [q, k, v]
    if cfg.decay:
        dcol, drow = extras
        in_specs += [pl.BlockSpec((1, SUBLANES, tq), lambda b, t, qi, kj: (b, 0, qi[t])),
                     pl.BlockSpec((1, tk, N_HEADS * LANES), lambda b, t, qi, kj: (b, kj[t], 0))]
        args += [drow, dcol]
    if cfg.select:
        (sel,) = extras
        in_specs += [pl.BlockSpec((1, sel.shape[1], tq), lambda b, t, qi, kj: (b, 0, qi[t]))]
        args += [sel]
    if cfg.diff:
        layer, g64, gsub, lam_row = extras
        in_specs += [_const_spec(g64), _const_spec(gsub, layer), _const_spec(lam_row, layer)]
        args += [g64, gsub, lam_row]

    grid_spec = pltpu.PrefetchScalarGridSpec(
        num_scalar_prefetch=2,
        grid=(bsz, len(pairs)),
        in_specs=in_specs,
        out_specs=pl.BlockSpec((1, tq, GROUP_WIDTH), lambda b, t, qi, kj: (b, qi[t], 0)),
        scratch_shapes=[pltpu.VMEM((n_vh, tq, MXU_WIDTH), jnp.bfloat16),
                        pltpu.VMEM((n_vh, 1, tq), jnp.float32),
                        pltpu.VMEM((n_vh, V_ROWS, tq), jnp.float32)])
    return pl.pallas_call(
        functools.partial(_attn_body, cfg),
        grid_spec=grid_spec,
        out_shape=jax.ShapeDtypeStruct((bsz, seq, GROUP_WIDTH), jnp.bfloat16),
        compiler_params=pltpu.CompilerParams(dimension_semantics=("arbitrary", "arbitrary"),
                                             vmem_limit_bytes=VMEM_LIMIT_BYTES),
        name=cfg.name,
    )(qi, kj, *args)


def _ffn_body(x_ref, oa_ref, ob_ref, oc_ref, od_ref, oe_ref, wo_ref, fnorm_ref, wg_ref, wu_ref, cw_ref, cb_ref, wd_ref,
              out_ref, xnew_s, xn_s, acc_s):
    i = pl.program_id(1)
    f = pl.program_id(2)
    tm = x_ref.shape[1]

    @pl.when(f == 0)
    def _():
        @pl.when(i == 0)
        def _():
            xn_s[0:TAIL_ROWS, :] = jnp.zeros((TAIL_ROWS, xn_s.shape[1]), xn_s.dtype)

        @pl.when(i > 0)
        def _():
            xn_s[0:TAIL_ROWS, :] = xn_s[tm:tm + TAIL_ROWS, :]

        xnew = x_ref[0]
        for g, o_ref in enumerate((oa_ref, ob_ref, oc_ref, od_ref, oe_ref)):
            xnew = xnew + _dot(o_ref[0], wo_ref[g])
        xnew_s[...] = xnew
        xn = xnew * lax.rsqrt(jnp.mean(xnew * xnew, axis=-1, keepdims=True) + EPS) * fnorm_ref[...]
        xn_s[TAIL_ROWS:TAIL_ROWS + tm, :] = xn.astype(xn_s.dtype)
        acc_s[...] = jnp.zeros_like(acc_s)

    ge = _dot(xn_s[...], wg_ref[...])
    u = _dot(xn_s[TAIL_ROWS:TAIL_ROWS + tm, :], wu_ref[...])
    g0 = ge[TAIL_ROWS:TAIL_ROWS + tm, :]
    t1 = ge[TAIL_ROWS - 1:TAIL_ROWS, :]
    t2 = ge[TAIL_ROWS - 2:TAIL_ROWS - 1, :]
    row = lax.broadcasted_iota(jnp.int32, g0.shape, 0)
    g1 = jnp.where(row == 0, t1, pltpu.roll(g0, 1, 0))
    g2 = jnp.where(row == 0, t2, jnp.where(row == 1, t1, pltpu.roll(g0, 2, 0)))
    y = cb_ref[...] + cw_ref[0:1, :] * g2
    y = y + cw_ref[1:2, :] * g1
    y = y + cw_ref[2:3, :] * g0
    hmid = (y * (1.0 / (1.0 + jnp.exp(-y)))) * u
    acc_s[...] += _dot(hmid.astype(jnp.bfloat16), wd_ref[...])

    @pl.when(f == pl.num_programs(2) - 1)
    def _():
        out_ref[0] = xnew_s[...] + acc_s[...]


def _ffn_call(layer, x, outs, wo, fnorm, wg, wu, cw, cb, wd, tm, tf):
    bsz, seq, d = x.shape
    dff = wg.shape[2]
    assert seq % tm == 0 and dff % tf == 0
    tok = lambda w: pl.BlockSpec((1, tm, w), lambda b, i, f: (b, i, 0))
    return pl.pallas_call(
        _ffn_body,
        grid=(bsz, seq // tm, dff // tf),
        in_specs=[tok(d)] + [tok(GROUP_WIDTH)] * 5 + [
            _const_spec(wo, layer),
            _const_spec(fnorm, layer),
            pl.BlockSpec((None, d, tf), lambda b, i, f: (layer, 0, f)),
            pl.BlockSpec((None, d, tf), lambda b, i, f: (layer, 0, f)),
            pl.BlockSpec((None, SUBLANES, tf), lambda b, i, f: (layer, 0, f)),
            pl.BlockSpec((None, 1, tf), lambda b, i, f: (layer, 0, f)),
            pl.BlockSpec((None, tf, d), lambda b, i, f: (layer, f, 0))],
        out_specs=tok(d),
        out_shape=jax.ShapeDtypeStruct((bsz, seq, d), jnp.float32),
        scratch_shapes=[pltpu.VMEM((tm, d), jnp.float32), pltpu.VMEM((TAIL_ROWS + tm, d), jnp.bfloat16),
                        pltpu.VMEM((tm, d), jnp.float32)],
        compiler_params=pltpu.CompilerParams(dimension_semantics=("arbitrary", "arbitrary", "arbitrary"),
                                             vmem_limit_bytes=VMEM_LIMIT_BYTES),
        name="ffn",
    )(x, *outs, wo, fnorm, wg, wu, cw, cb, wd)


def _pad_rows(v, width=MXU_WIDTH):
    return jnp.pad(v.astype(jnp.float32), ((0, 0), (0, width - v.shape[1])))


def _tile_rows(g, reps):
    return jnp.tile(g.astype(jnp.float32), (1, reps))


def _zero_col(w):
    return jnp.concatenate([w, jnp.zeros(w.shape[:-1] + (1,), w.dtype)], axis=-1)


def _rope_table(positions):
    pos = positions.astype(jnp.float32)[:, :, None]
    inv = [ROPE_THETA ** (-jnp.arange(0, rot, 2, dtype=jnp.float32) / rot) for rot in (MLA_ROPE, ROT_MOBA, ROT_DIFF)]
    inv = jnp.concatenate(inv + [jnp.zeros((N_FREQ - TR_ONE,), jnp.float32)])
    ang = pos * inv
    c, s = jnp.cos(ang), jnp.sin(ang)
    c_hi = c.astype(jnp.bfloat16)
    c_lo = (c - c_hi.astype(jnp.float32)).astype(jnp.bfloat16)
    s_hi = s.astype(jnp.bfloat16)
    s_lo = (s - s_hi.astype(jnp.float32)).astype(jnp.bfloat16)
    return jnp.concatenate([c_hi, c_lo, s_hi, s_lo], axis=-1)


def _pick_tile(n, pref):
    t = pref
    while n % t:
        t //= 2
    return t


def kernel(x, mem, positions, attn_norm, ffn_norm, mem_norm, w_in, mla_cq_norm, mla_ckv_norm, mla_w_uq, mla_w_ukv, mla_q_norm, mla_k_norm, fox_b_f, fox_q_norm, fox_k_norm, moba_q_norm, moba_k_norm, diff_lambda, diff_q_norm, diff_k_norm, diff_sub_norm, mem_w_kv, mem_q_norm, mem_k_norm, w_o, ffn_w_gate, ffn_w_up, ffn_conv_w, ffn_conv_b, ffn_w_down):
    bsz, seq, d = x.shape
    depth = w_in.shape[0]
    dff = ffn_w_gate.shape[2]
    bf = jnp.bfloat16
    f32 = jnp.float32

    in_idx = _np_in_index()
    uq_idx = _np_uq_index()
    ukvk_idx, ukvv_idx = _np_ukv_index()
    gpair = jnp.asarray(_np_group_matrix(_PAIR_GROUPS), bf)
    g64 = jnp.asarray(_np_group_matrix(_G64_GROUPS), bf)
    g32 = jnp.asarray(_np_group_matrix(_G32_GROUPS), bf)
    expand = jnp.asarray(_np_rope_expand_all(), bf)
    tril = jnp.asarray(np.tril(np.ones((MOBA_BLOCK, MOBA_BLOCK), np.float32)), bf)
    vsel = jnp.asarray(_np_value_select(), bf)
    tr = _rope_table(positions)

    t_dense = _pick_tile(seq, 1024)
    t_ffn = _pick_tile(seq, 512)
    tf = dff // 2 if (dff // 2) % LANES == 0 else dff

    win = jnp.take(_zero_col(w_in), in_idx, axis=2).astype(bf)
    wuq = jnp.take(_zero_col(mla_w_uq), uq_idx, axis=2)
    wuq = jnp.pad(wuq, ((0, 0), (0, MXU_WIDTH - MLA_Q_RANK), (0, 0))).astype(bf)
    wukv = _zero_col(mla_w_ukv)
    wukvk = jnp.take(wukv, ukvk_idx, axis=2).astype(bf)
    wukvv = jnp.take(wukv, ukvv_idx, axis=2).astype(bf)
    pair = lambda g: _pad_rows(_tile_rows(g, 2))
    rows = [jnp.zeros((depth, MXU_WIDTH), f32)] * P_ROWS
    rows[P_CQ] = _pad_rows(mla_cq_norm)
    rows[P_CKV] = _pad_rows(mla_ckv_norm)
    rows[P_GQ] = pair(mla_q_norm) * (LOG2E * MLA_QK ** -0.5)
    rows[P_GK] = pair(mla_k_norm)
    rows[P_FQ] = _tile_rows(fox_q_norm, N_HEADS) * (LOG2E * HEAD_DIM ** -0.5)
    rows[P_FK] = _tile_rows(fox_k_norm, N_HEADS)
    rows[P_FB] = _pad_rows(fox_b_f)
    rows[P_MQ] = _tile_rows(moba_q_norm, N_HEADS)
    rows[P_MK] = _tile_rows(moba_k_norm, N_HEADS)
    rows[P_DQ] = _tile_rows(diff_q_norm, 2 * N_HEADS) * (LOG2E * DIFF_QK ** -0.5)
    rows[P_DK] = _tile_rows(diff_k_norm, 2 * N_HEADS)
    rows[P_EQ] = _tile_rows(mem_q_norm, N_HEADS) * (LOG2E * HEAD_DIM ** -0.5)
    par = jnp.stack(rows, axis=1)
    anorm = attn_norm.astype(f32)[:, None, :]
    mnorm = mem_norm.astype(f32)[:, None, :]
    fnorm = ffn_norm.astype(f32)[:, None, :]
    wmem = mem_w_kv.astype(bf)
    mem_gain = _tile_rows(mem_k_norm, N_HEADS)[:, None, :]

    lam_init = jnp.asarray([0.8 - 0.6 * math.exp(-0.3 * l) for l in range(depth)], f32)
    lam_vec = diff_lambda.astype(f32)
    lam = (jnp.exp(jnp.sum(lam_vec[:, 0] * lam_vec[:, 1], axis=-1))
           - jnp.exp(jnp.sum(lam_vec[:, 2] * lam_vec[:, 3], axis=-1)) + lam_init)
    lam_row = jnp.broadcast_to(lam[:, None, None], (depth, 1, GROUP_WIDTH))
    gsub = (_tile_rows(diff_sub_norm, N_HEADS) * (1.0 - lam_init)[:, None])[:, None, :]

    wo = w_o.reshape(depth, 5, GROUP_WIDTH, d).astype(bf)
    wg, wu, wd = ffn_w_gate.astype(bf), ffn_w_up.astype(bf), ffn_w_down.astype(bf)
    cw = jnp.pad(ffn_conv_w.astype(f32), ((0, 0), (0, SUBLANES - CONV_WIDTH), (0, 0)))
    cb = ffn_conv_b.astype(f32)[:, None, :]

    for l in range(depth):
        (aq, ak, av, fq, fk, fv, fdcol, fdrow, mq, mk, mv, msel, dq, dk, dv, eq) = _prep_call(
            l, x, tr, anorm, win, wuq, wukvk, wukvv, gpair, g64, g32, expand, tril, vsel, par)
        ek, ev = _memkv_call(l, mem, mnorm, wmem, g64, mem_gain, vsel)

        o_a = _attn_call(_CFG_MLA, aq, ak, av, (), t_dense, t_dense)
        o_b = _attn_call(_CFG_FOX, fq, fk, fv, (fdcol, fdrow), t_dense, t_dense)
        t_moba = max(t_dense, MOBA_BLOCK)
        o_c = _attn_call(_CFG_MOBA, mq, mk, mv, (msel,), t_moba, t_moba)
        o_d = _attn_call(_CFG_DIFF, dq, dk, dv, (l, g64, gsub, lam_row), t_dense, t_dense)
        o_e = _attn_call(_CFG_MEM, eq, ek, ev, (), t_dense, mem.shape[1])

        x = _ffn_call(l, x, (o_a, o_b, o_c, o_d, o_e), wo, fnorm, wg, wu, cw, cb, wd, t_ffn, tf)
    return x
```

```python
import functools
import math

import numpy as np
import jax
import jax.numpy as jnp
from jax import lax
from jax.experimental import pallas as pl
from jax.experimental.pallas import tpu as pltpu

N_HEADS = 4
HEAD_DIM = 64
GROUP_WIDTH = N_HEADS * HEAD_DIM
MLA_Q_RANK = 192
MLA_KV_RANK = 128
MLA_NOPE = 64
MLA_ROPE = 32
MLA_QK = MLA_NOPE + MLA_ROPE
DIFF_QK = HEAD_DIM // 2
ROPE_THETA = 500000.0
ROT_MOBA = HEAD_DIM // 4
ROT_DIFF = DIFF_QK // 4
MOBA_BLOCK = 256
MOBA_TOPK = 3
CONV_WIDTH = 3
EPS = 1e-6
NEG_INF = -1e30
LOG2E = math.log2(math.e)
REMOVED = -3e38

LANES = 128
SUBLANES = 8
MXU_WIDTH = 256
TAIL_ROWS = 16
VMEM_LIMIT_BYTES = 56 * 1024 * 1024

_SRC_CQ = 0
_SRC_CKV = _SRC_CQ + MLA_Q_RANK
_SRC_KR = _SRC_CKV + MLA_KV_RANK
_SRC_FOX = _SRC_KR + MLA_ROPE
_SRC_FOXF = _SRC_FOX + 3 * GROUP_WIDTH
_SRC_MOBA = _SRC_FOXF + N_HEADS
_SRC_DIFF = _SRC_MOBA + 3 * GROUP_WIDTH
_SRC_MEMQ = _SRC_DIFF + 3 * GROUP_WIDTH
_SRC_END = _SRC_MEMQ + GROUP_WIDTH

PK_CQ = 0
PK_CKV = 256
PK_KR = 384
PK_FQ, PK_FK, PK_FV = 896, 1152, 1408
PK_FF = 1664
PK_MQ, PK_MK, PK_MV = 1792, 2048, 2304
PK_DQ, PK_DK, PK_DV = 2560, 2816, 3072
PK_EQ = 3328
PK_END = 3584

PAIR_STRIDE = MLA_QK


def _pair_lane(h, d):
    return (h // 2) * MXU_WIDTH + (h % 2) * PAIR_STRIDE + d


N_FREQ = 32
FREQ_BASE_MLA = 0
FREQ_BASE_MOBA = MLA_ROPE // 2
FREQ_BASE_DIFF = FREQ_BASE_MOBA + ROT_MOBA // 2
TR_ONE = FREQ_BASE_DIFF + ROT_DIFF // 2
TR_WIDTH = 4 * N_FREQ
assert TR_ONE < N_FREQ and TR_WIDTH == LANES

(P_CQ, P_CKV, P_GQ, P_GK, P_FQ, P_FK, P_FB, P_MQ, P_MK, P_DQ, P_DK, P_EQ) = range(12)
P_ROWS = 16


def _np_in_index():
    idx = np.full((PK_END,), _SRC_END, np.int32)
    idx[PK_CQ:PK_CQ + MLA_Q_RANK] = np.arange(_SRC_CQ, _SRC_CQ + MLA_Q_RANK)
    idx[PK_CKV:PK_CKV + MLA_KV_RANK] = np.arange(_SRC_CKV, _SRC_CKV + MLA_KV_RANK)
    for h in range(N_HEADS):
        for d in range(MLA_ROPE):
            idx[PK_KR + _pair_lane(h, d)] = _SRC_KR + d
    idx[PK_FQ:PK_FQ + 3 * GROUP_WIDTH] = np.arange(_SRC_FOX, _SRC_FOX + 3 * GROUP_WIDTH)
    idx[PK_FF:PK_FF + N_HEADS] = np.arange(_SRC_FOXF, _SRC_FOXF + N_HEADS)
    idx[PK_MQ:PK_MQ + 3 * GROUP_WIDTH] = np.arange(_SRC_MOBA, _SRC_MOBA + 3 * GROUP_WIDTH)
    idx[PK_DQ:PK_DQ + 3 * GROUP_WIDTH] = np.arange(_SRC_DIFF, _SRC_DIFF + 3 * GROUP_WIDTH)
    idx[PK_EQ:PK_EQ + GROUP_WIDTH] = np.arange(_SRC_MEMQ, _SRC_MEMQ + GROUP_WIDTH)
    return idx


def _np_uq_index():
    idx = np.full((2 * MXU_WIDTH,), N_HEADS * MLA_QK, np.int32)
    for h in range(N_HEADS):
        for d in range(MLA_QK):
            idx[_pair_lane(h, d)] = h * MLA_QK + d
    return idx


def _np_ukv_index():
    zero = N_HEADS * (MLA_NOPE + HEAD_DIM)
    idx_k = np.full((2 * MXU_WIDTH,), zero, np.int32)
    idx_v = np.zeros((GROUP_WIDTH,), np.int32)
    for h in range(N_HEADS):
        for d in range(MLA_NOPE):
            idx_k[_pair_lane(h, MLA_ROPE + d)] = h * (MLA_NOPE + HEAD_DIM) + d
        for d in range(HEAD_DIM):
            idx_v[h * HEAD_DIM + d] = h * (MLA_NOPE + HEAD_DIM) + MLA_NOPE + d
    return idx_k, idx_v


def _np_group_matrix(groups):
    g = np.zeros((MXU_WIDTH, MXU_WIDTH), np.float32)
    for lo, size in groups:
        g[lo:lo + size, lo:lo + size] = 1.0 / size
    return g


_PAIR_GROUPS = [(0, MLA_ROPE), (MLA_ROPE, MLA_NOPE), (PAIR_STRIDE, MLA_ROPE), (PAIR_STRIDE + MLA_ROPE, MLA_NOPE)]
_G64_GROUPS = [(h * HEAD_DIM, HEAD_DIM) for h in range(N_HEADS)]
_G32_GROUPS = [(g * DIFF_QK, DIFF_QK) for g in range(2 * N_HEADS)]


def _np_rope_expand(regions, rot, base):
    half = rot // 2
    e = np.zeros((TR_WIDTH, 2 * MXU_WIDTH), np.float32)
    e[TR_ONE, 0:MXU_WIDTH] = 1.0
    for lo in regions:
        assert lo % rot == 0
        for r in range(half):
            f = base + r
            for lane, sign in ((lo + r, -1.0), (lo + half + r, 1.0)):
                e[TR_ONE, lane] = 0.0
                e[f, lane] = 1.0
                e[N_FREQ + f, lane] = 1.0
                e[2 * N_FREQ + f, MXU_WIDTH + lane] = sign
                e[3 * N_FREQ + f, MXU_WIDTH + lane] = sign
    return e


def _np_rope_expand_all():
    return np.concatenate([
        _np_rope_expand([0, PAIR_STRIDE], MLA_ROPE, FREQ_BASE_MLA),
        _np_rope_expand([h * HEAD_DIM for h in range(N_HEADS)], ROT_MOBA, FREQ_BASE_MOBA),
        _np_rope_expand([g * DIFF_QK for g in range(2 * N_HEADS)], ROT_DIFF, FREQ_BASE_DIFF),
    ], axis=1)


def _dot(a, b):
    return jnp.dot(a, b, preferred_element_type=jnp.float32)


def _dot_nt(a, b):
    return lax.dot_general(a, b, (((1,), (1,)), ((), ())), preferred_element_type=jnp.float32)


def _split2(a):
    hi = a.astype(jnp.bfloat16)
    lo = (a - hi.astype(jnp.float32)).astype(jnp.bfloat16)
    return hi, lo


def _split3(a):
    hi = a.astype(jnp.bfloat16)
    r = a - hi.astype(jnp.float32)
    mid = r.astype(jnp.bfloat16)
    lo = (r - mid.astype(jnp.float32)).astype(jnp.bfloat16)
    return hi, mid, lo


def _group_mean_sq(a, g_bf16):
    return _dot((a * a).astype(jnp.bfloat16), g_bf16)


def _rope(x, tabs, half):
    w = x.shape[-1]
    lane = lax.broadcasted_iota(jnp.int32, x.shape, 1)
    partner = jnp.where((lane & (2 * half - 1)) >= half, pltpu.roll(x, half, 1), pltpu.roll(x, w - half, 1))
    return x * tabs[:, 0:w] + partner * tabs[:, w:2 * w]


def _lane_mask(shape, lo, hi):
    lane = lax.broadcasted_iota(jnp.int32, shape, len(shape) - 1)
    return (lane >= lo) & (lane < hi)


def _prep_body(x_ref, tr_ref, anorm_ref, win_ref, wuq_ref, wukvk_ref, wukvv_ref, gpair_ref, g64_ref, g32_ref,
               exp_ref, tril_ref, vsel_ref, par_ref,
               aq_ref, ak_ref, av_ref, fq_ref, fk_ref, fv_ref, fdcol_ref, fdrow_ref, mq_ref, mk_ref, mv_ref, msel_ref,
               dq_ref, dk_ref, dv_ref, eq_ref,
               kmean_s, carry_s):
    j = pl.program_id(1)
    tm = x_ref.shape[1]
    bf = jnp.bfloat16

    @pl.when(j == 0)
    def _():
        kmean_s[...] = jnp.zeros_like(kmean_s)
        carry_s[...] = jnp.zeros_like(carry_s)

    x = x_ref[0]
    xn = x * lax.rsqrt(jnp.mean(x * x, axis=-1, keepdims=True) + EPS) * anorm_ref[...]
    xb = xn.astype(bf)

    def proj(off, width):
        return _dot(xb, win_ref[:, off:off + width])

    def prow(r, width=MXU_WIDTH):
        return par_ref[r:r + 1, 0:width]

    gpair, g64, g32 = gpair_ref[...], g64_ref[...], g32_ref[...]
    vsel = vsel_ref[...]

    tabs = _dot(tr_ref[0], exp_ref[...])
    tab_mla = tabs[:, 0:2 * MXU_WIDTH]
    tab_moba = tabs[:, 2 * MXU_WIDTH:4 * MXU_WIDTH]
    tab_diff = tabs[:, 4 * MXU_WIDTH:6 * MXU_WIDTH]
    direct = {name: proj(off, MXU_WIDTH) for name, off in (("mq", PK_MQ), ("mk", PK_MK))}
    p_cq = proj(PK_CQ, MXU_WIDTH)
    p_ckv = proj(PK_CKV, MLA_KV_RANK)
    p_ff = proj(PK_FF, LANES)
    direct.update({name: proj(off, MXU_WIDTH) for name, off in
                   (("fq", PK_FQ), ("fk", PK_FK), ("dq", PK_DQ), ("dk", PK_DK), ("eq", PK_EQ))})
    p_kr = proj(PK_KR, 2 * MXU_WIDTH)
    values = {name: proj(off, MXU_WIDTH).astype(bf) for name, off in
              (("fv", PK_FV), ("mv", PK_MV), ("dv", PK_DV))}

    cqn = p_cq * lax.rsqrt(jnp.sum(p_cq * p_cq, axis=-1, keepdims=True) * (1.0 / MLA_Q_RANK) + EPS) * prow(P_CQ)
    ckvn = p_ckv * lax.rsqrt(jnp.mean(p_ckv * p_ckv, axis=-1, keepdims=True) + EPS) * prow(P_CKV, MLA_KV_RANK)
    ckvb = ckvn.astype(bf)
    z = p_ff + prow(P_FB, LANES)
    log_f = jnp.minimum(z, 0.0) - jnp.log1p(jnp.exp(-jnp.abs(z)))
    log_f = jnp.where(_lane_mask(log_f.shape, 0, N_HEADS), log_f, 0.0)
    l1, l2, l3 = _split3(log_f)

    gmat = {"fq": g64, "fk": g64, "mq": g64, "mk": g64, "dq": g32, "dk": g32, "eq": g64}
    ms = {name: _group_mean_sq(a, gmat[name]) for name, a in direct.items()}
    qa = _dot(cqn.astype(bf), wuq_ref[...])
    ka = p_kr + _dot(ckvb, wukvk_ref[...])
    values["av"] = _dot(ckvb, wukvv_ref[...]).astype(bf)
    tril = tril_ref[...]
    dec = carry_s[...] + ((_dot(tril, l1) + _dot(tril, l2)) + _dot(tril, l3))
    carry_s[...] = dec[tm - 1:tm, :]

    gains = {"fq": P_FQ, "fk": P_FK, "mq": P_MQ, "mk": P_MK, "dq": P_DQ, "dk": P_DK, "eq": P_EQ}
    normed = {name: a * lax.rsqrt(ms[name] + EPS) * prow(gains[name]) for name, a in direct.items()}
    mq = _rope(normed["mq"], tab_moba, ROT_MOBA // 2)
    mk = _rope(normed["mk"], tab_moba, ROT_MOBA // 2)
    kmean_s[pl.ds(j, 1), :] = jnp.mean(mk, axis=0, keepdims=True)
    km_hi, km_lo = _split2(kmean_s[...])
    gates = []
    for h in range(N_HEADS):
        q_hi, q_lo = _split2(jnp.where(_lane_mask(mq.shape, h * HEAD_DIM, (h + 1) * HEAD_DIM), mq, 0.0))
        gates.append((_dot_nt(km_hi, q_hi) + _dot_nt(km_lo, q_hi)) + _dot_nt(km_hi, q_lo))
    ms_qa = [_group_mean_sq(qa[:, p * MXU_WIDTH:(p + 1) * MXU_WIDTH], gpair) for p in range(2)]
    ms_ka = [_group_mean_sq(ka[:, p * MXU_WIDTH:(p + 1) * MXU_WIDTH], gpair) for p in range(2)]
    dec2 = dec * LOG2E
    d1, d2, d3 = _split3(dec2)
    for h in range(N_HEADS):
        fdcol_ref[0, :, h * LANES:(h + 1) * LANES] = jnp.broadcast_to(dec2[:, h:h + 1], (tm, LANES))
    row_sel = jnp.where(lax.broadcasted_iota(jnp.int32, (SUBLANES, LANES), 0)
                        == lax.broadcasted_iota(jnp.int32, (SUBLANES, LANES), 1), 1.0, 0.0).astype(bf)
    fdrow_ref[0] = (_dot_nt(row_sel, d1) + _dot_nt(row_sel, d2)) + _dot_nt(row_sel, d3)
    for name, ref in (("fv", fv_ref), ("mv", mv_ref), ("dv", dv_ref), ("av", av_ref)):
        ref[0] = _values_t(values[name], vsel)

    fq_ref[0] = normed["fq"].astype(bf)
    fk_ref[0] = normed["fk"].astype(bf)
    eq_ref[0] = normed["eq"].astype(bf)
    dq_ref[0] = _rope(normed["dq"], tab_diff, ROT_DIFF // 2).astype(bf)
    dk_ref[0] = _rope(normed["dk"], tab_diff, ROT_DIFF // 2).astype(bf)
    mq_ref[0] = (mq * (LOG2E * HEAD_DIM ** -0.5)).astype(bf)
    mk_ref[0] = mk.astype(bf)
    for p in range(2):
        sl = slice(p * MXU_WIDTH, (p + 1) * MXU_WIDTH)
        aq_ref[0, :, sl] = _rope(qa[:, sl] * lax.rsqrt(ms_qa[p] + EPS) * prow(P_GQ), tab_mla,
                                 MLA_ROPE // 2).astype(bf)
        ak_ref[0, :, sl] = _rope(ka[:, sl] * lax.rsqrt(ms_ka[p] + EPS) * prow(P_GK), tab_mla,
                                 MLA_ROPE // 2).astype(bf)

    nbp = kmean_s.shape[0]
    blk = lax.broadcasted_iota(jnp.int32, (nbp, tm), 0)
    past = blk < j
    for h in range(N_HEADS):
        work = jnp.where(past, gates[h], NEG_INF)
        sel = jnp.zeros((nbp, tm), jnp.bool_)
        for _ in range(MOBA_TOPK):
            mx = jnp.max(work, axis=0, keepdims=True)
            first = jnp.min(jnp.where(work == mx, blk, nbp), axis=0, keepdims=True)
            pick = blk == first
            sel = sel | pick
            work = jnp.where(pick, REMOVED, work)
        msel_ref[0, h * nbp:(h + 1) * nbp, :] = jnp.where(sel & past, 0.0, NEG_INF)


def _const_spec(a, layer=None):
    if layer is None:
        n = a.ndim
        return pl.BlockSpec(a.shape, lambda *_: (0,) * n)
    n = a.ndim - 1
    return pl.BlockSpec((None,) + a.shape[1:], lambda *_: (layer,) + (0,) * n)


def _moba_blocks_padded(seq):
    return -(-(seq // MOBA_BLOCK) // SUBLANES) * SUBLANES


def _prep_call(layer, x, tr, anorm, win, wuq, wukvk, wukvv, gpair, g64, g32, expand, tril, vsel, par):
    bsz, seq, d = x.shape
    tm = MOBA_BLOCK
    assert seq % tm == 0
    nbp = _moba_blocks_padded(seq)
    bf = jnp.bfloat16
    f32 = jnp.float32

    vt = -N_HEADS * V_ROWS
    widths = [(2 * MXU_WIDTH, bf), (2 * MXU_WIDTH, bf), (vt, bf),
              (GROUP_WIDTH, bf), (GROUP_WIDTH, bf), (vt, bf),
              (N_HEADS * LANES, f32), (-SUBLANES, f32),
              (GROUP_WIDTH, bf), (GROUP_WIDTH, bf), (vt, bf), (-N_HEADS * nbp, f32),
              (GROUP_WIDTH, bf), (GROUP_WIDTH, bf), (vt, bf),
              (GROUP_WIDTH, bf)]

    def tok(width):
        if width > 0:
            return pl.BlockSpec((1, tm, width), lambda b, j: (b, j, 0))
        return pl.BlockSpec((1, -width, tm), lambda b, j: (b, 0, j))

    def shape(width):
        return (bsz, seq, width) if width > 0 else (bsz, -width, seq)

    consts = [anorm, win, wuq, wukvk, wukvv, gpair, g64, g32, expand, tril, vsel, par]
    layered = [True, True, True, True, True, False, False, False, False, False, False, True]
    return pl.pallas_call(
        _prep_body,
        grid=(bsz, seq // tm),
        in_specs=[tok(d), tok(TR_WIDTH)] + [_const_spec(c, layer if ly else None) for c, ly in zip(consts, layered)],
        out_specs=[tok(w) for w, _ in widths],
        out_shape=[jax.ShapeDtypeStruct(shape(w), dt) for w, dt in widths],
        scratch_shapes=[pltpu.VMEM((nbp, GROUP_WIDTH), jnp.float32), pltpu.VMEM((1, LANES), jnp.float32)],
        compiler_params=pltpu.CompilerParams(dimension_semantics=("arbitrary", "arbitrary"),
                                             vmem_limit_bytes=VMEM_LIMIT_BYTES),
        name="prep",
    )(x, tr, *consts)


def _memkv_body(mem_ref, mnorm_ref, w_ref, g64_ref, gain_ref, vsel_ref, k_ref, v_ref):
    m = mem_ref[0]
    mn = m * lax.rsqrt(jnp.mean(m * m, axis=-1, keepdims=True) + EPS) * mnorm_ref[...]
    kv = _dot(mn.astype(jnp.bfloat16), w_ref[...])
    k = kv[:, 0:GROUP_WIDTH]
    k = k * lax.rsqrt(_group_mean_sq(k, g64_ref[...]) + EPS) * gain_ref[...]
    k_ref[0] = k.astype(jnp.bfloat16)
    v_ref[0] = _values_t(kv[:, GROUP_WIDTH:2 * GROUP_WIDTH].astype(jnp.bfloat16), vsel_ref[...])


def _memkv_call(layer, mem, mnorm, w, g64, gain, vsel):
    bsz, mlen, d = mem.shape
    k_shape, vt_shape = (bsz, mlen, GROUP_WIDTH), (bsz, N_HEADS * V_ROWS, mlen)
    return pl.pallas_call(
        _memkv_body,
        grid=(bsz,),
        in_specs=[pl.BlockSpec((1, mlen, d), lambda b: (b, 0, 0)), _const_spec(mnorm, layer), _const_spec(w, layer),
                  _const_spec(g64), _const_spec(gain, layer), _const_spec(vsel)],
        out_specs=[pl.BlockSpec((1,) + s[1:], lambda b: (b, 0, 0)) for s in (k_shape, vt_shape)],
        out_shape=[jax.ShapeDtypeStruct(s, jnp.bfloat16) for s in (k_shape, vt_shape)],
        compiler_params=pltpu.CompilerParams(dimension_semantics=("arbitrary",), vmem_limit_bytes=VMEM_LIMIT_BYTES),
        name="mem_kv",
    )(mem, mnorm, w, g64, gain, vsel)


class _AttnCfg:
    def __init__(self, name, vheads, n_maps, causal, decay=False, select=False, diff=False):
        self.name = name
        self.vheads = vheads
        self.n_maps = n_maps
        self.causal = causal
        self.decay = decay
        self.select = select
        self.diff = diff


_PLAIN_VHEADS = [(0, h * HEAD_DIM, (h + 1) * HEAD_DIM, 0, h) for h in range(N_HEADS)]
_CFG_MLA = _AttnCfg("attn_mla", [((h // 2) * MXU_WIDTH, (h % 2) * PAIR_STRIDE, (h % 2) * PAIR_STRIDE + MLA_QK, 0, h)
                                 for h in range(N_HEADS)], 1, True)
_CFG_FOX = _AttnCfg("attn_fox", _PLAIN_VHEADS, 1, True, decay=True)
_CFG_MOBA = _AttnCfg("attn_moba", _PLAIN_VHEADS, 1, True, select=True)
_CFG_DIFF = _AttnCfg("attn_diff", [(0, h * HEAD_DIM + c * DIFF_QK, h * HEAD_DIM + (c + 1) * DIFF_QK, c, h)
                                   for c in range(2) for h in range(N_HEADS)], 2, True, diff=True)
_CFG_MEM = _AttnCfg("attn_mem", _PLAIN_VHEADS, 1, False)


ONES_ROW = HEAD_DIM
V_ROWS = HEAD_DIM + 16
QK_LOOKAHEAD = 8
Q_SUB = 2 * MXU_WIDTH


def _np_value_select():
    sel = np.zeros((N_HEADS * V_ROWS, GROUP_WIDTH), np.float32)
    for h in range(N_HEADS):
        for d in range(HEAD_DIM):
            sel[h * V_ROWS + d, h * HEAD_DIM + d] = 1.0
    return sel


def _values_t(v, vsel):
    vt = _dot_nt(vsel, v)
    row = lax.broadcasted_iota(jnp.int32, vt.shape, 0)
    ones = row == ONES_ROW
    for h in range(1, N_HEADS):
        ones = ones | (row == h * V_ROWS + ONES_ROW)
    return jnp.where(ones, 1.0, vt).astype(jnp.bfloat16)


def _tile_lanes(x, width):
    return jnp.tile(x, (1, width // LANES)) if width != LANES else x


def _attn_body(cfg, qi_ref, kj_ref, *refs):
    refs = list(refs)
    q_ref, k_ref, vt_ref = refs[:3]
    pos = 3
    if cfg.decay:
        dq_ref, dk_ref = refs[pos:pos + 2]
        pos += 2
    if cfg.select:
        sel_ref = refs[pos]
        pos += 1
    if cfg.diff:
        g64_ref, gsub_ref, lam_ref = refs[pos:pos + 3]
        pos += 3
    o_ref, qm_s, m_s, acc_s = refs[pos:pos + 4]

    t = pl.program_id(1)
    i = qi_ref[t]
    j = kj_ref[t]
    tq = q_ref.shape[1]
    tk = k_ref.shape[1]

    @pl.when(j == 0)
    def _():
        for n, (off, lo, hi, _, _) in enumerate(cfg.vheads):
            qb = q_ref[0, :, off:off + MXU_WIDTH]
            qm_s[n] = jnp.where(_lane_mask(qb.shape, lo, hi), qb, jnp.zeros_like(qb))
        m_s[...] = jnp.full(m_s.shape, NEG_INF, jnp.float32)
        acc_s[...] = jnp.zeros_like(acc_s)

    def step(diag):
        qs = min(tq, Q_SUB)
        items = [(n, u) for n in range(len(cfg.vheads)) for u in range(tq // qs)]

        def n_keys(u):
            return (u + 1) * qs if diag else tk

        def scores(item):
            n, u = item
            off, _, _, _, h = cfg.vheads[n]
            nk = n_keys(u)
            cols = slice(u * qs, (u + 1) * qs)
            s = _dot_nt(k_ref[0, 0:nk, off:off + MXU_WIDTH], qm_s[n, cols, :])
            if cfg.decay:
                s = (dq_ref[0, h:h + 1, cols] - _tile_lanes(dk_ref[0, 0:nk, h * LANES:(h + 1) * LANES], qs)) + s
            if cfg.select:
                nbp = sel_ref.shape[1] // N_HEADS
                qpos = u * qs + lax.broadcasted_iota(jnp.int32, (1, qs), 1)
                parts = []
                for kb in range(nk // MOBA_BLOCK):
                    rows = s[kb * MOBA_BLOCK:(kb + 1) * MOBA_BLOCK, :]
                    if not (diag and kb == nk // MOBA_BLOCK - 1):
                        bias = sel_ref[0, pl.ds(h * nbp + j * (tk // MOBA_BLOCK) + kb, 1), cols]
                        if diag:
                            bias = jnp.where(qpos < (kb + 1) * MOBA_BLOCK, 0.0, bias)
                        rows = rows + bias
                    parts.append(rows)
                s = parts[0] if len(parts) == 1 else jnp.concatenate(parts, axis=0)
            if diag:
                key = lax.broadcasted_iota(jnp.int32, (nk, qs), 0)
                qry = u * qs + lax.broadcasted_iota(jnp.int32, (nk, qs), 1)
                s = jnp.where(key <= qry, s, NEG_INF)
            return s, jnp.max(s, axis=0, keepdims=True)

        raw = {it: scores(items[it]) for it in range(min(QK_LOOKAHEAD, len(items)))}
        for it, (n, u) in enumerate(items):
            h = cfg.vheads[n][4]
            nk = n_keys(u)
            cols = slice(u * qs, (u + 1) * qs)
            s, s_max = raw.pop(it)
            m_prev = m_s[n, :, cols]
            m_new = jnp.maximum(m_prev, s_max)
            alpha = jnp.exp2(m_prev - m_new)
            p = jnp.exp2(s - m_new)
            m_s[n, :, cols] = m_new
            acc_s[n, :, cols] = acc_s[n, :, cols] * alpha + _dot(vt_ref[0, h * V_ROWS:(h + 1) * V_ROWS, 0:nk],
                                                                 p.astype(jnp.bfloat16))
            if it + QK_LOOKAHEAD < len(items):
                raw[it + QK_LOOKAHEAD] = scores(items[it + QK_LOOKAHEAD])

    if cfg.causal:
        pl.when(j < i)(functools.partial(step, False))
        pl.when(j == i)(functools.partial(step, True))
        last = j == i
    else:
        step(False)
        last = j == 0

    @pl.when(last)
    def _():
        outs = []
        for c in range(cfg.n_maps):
            heads = []
            for h in range(N_HEADS):
                acc = acc_s[c * N_HEADS + h]
                heads.append(acc[0:HEAD_DIM, :] / acc[ONES_ROW:ONES_ROW + 1, :])
            outs.append(jnp.concatenate(heads, axis=0).T)
        if cfg.diff:
            o = outs[0] - lam_ref[0:1, :] * outs[1]
            o = o * lax.rsqrt(_group_mean_sq(o, g64_ref[...]) + EPS) * gsub_ref[...]
        else:
            o = outs[0]
        o_ref[0] = o.astype(o_ref.dtype)


def _attn_call(cfg, q, k, v, extras, tq, tk):
    bsz, seq, wq = q.shape
    sk = k.shape[1]
    nq = seq // tq
    assert seq % tq == 0 and sk % tk == 0
    if cfg.causal:
        assert tq == tk and sk == seq
        pairs = [(i, j) for i in range(nq) for j in range(i + 1)]
    else:
        assert sk == tk
        pairs = [(i, 0) for i in range(nq)]
    qi = jnp.asarray(np.array([p[0] for p in pairs], np.int32))
    kj = jnp.asarray(np.array([p[1] for p in pairs], np.int32))
    n_vh = len(cfg.vheads)

    in_specs = [pl.BlockSpec((1, tq, wq), lambda b, t, qi, kj: (b, qi[t], 0)),
                pl.BlockSpec((1, tk, wq), lambda b, t, qi, kj: (b, kj[t], 0)),
                pl.BlockSpec((1, N_HEADS * V_ROWS, tk), lambda b, t, qi, kj: (b, 0, kj[t]))]
    args = [q, k, v]
    if cfg.decay:
        dcol, drow = extras
        in_specs += [pl.BlockSpec((1, SUBLANES, tq), lambda b, t, qi, kj: (b, 0, qi[t])),
                     pl.BlockSpec((1, tk, N_HEADS * LANES), lambda b, t, qi, kj: (b, kj[t], 0))]
        args += [drow, dcol]
    if cfg.select:
        (sel,) = extras
        in_specs += [pl.BlockSpec((1, sel.shape[1], tq), lambda b, t, qi, kj: (b, 0, qi[t]))]
        args += [sel]
    if cfg.diff:
        layer, g64, gsub, lam_row = extras
        in_specs += [_const_spec(g64), _const_spec(gsub, layer), _const_spec(lam_row, layer)]
        args += [g64, gsub, lam_row]

    grid_spec = pltpu.PrefetchScalarGridSpec(
        num_scalar_prefetch=2,
        grid=(bsz, len(pairs)),
        in_specs=in_specs,
        out_specs=pl.BlockSpec((1, tq, GROUP_WIDTH), lambda b, t, qi, kj: (b, qi[t], 0)),
        scratch_shapes=[pltpu.VMEM((n_vh, tq, MXU_WIDTH), jnp.bfloat16),
                        pltpu.VMEM((n_vh, 1, tq), jnp.float32),
                        pltpu.VMEM((n_vh, V_ROWS, tq), jnp.float32)])
    return pl.pallas_call(
        functools.partial(_attn_body, cfg),
        grid_spec=grid_spec,
        out_shape=jax.ShapeDtypeStruct((bsz, seq, GROUP_WIDTH), jnp.bfloat16),
        compiler_params=pltpu.CompilerParams(dimension_semantics=("arbitrary", "arbitrary"),
                                             vmem_limit_bytes=VMEM_LIMIT_BYTES),
        name=cfg.name,
    )(qi, kj, *args)


def _ffn_body(nf, x_ref, oa_ref, ob_ref, oc_ref, od_ref, oe_ref, wo_ref, fnorm_ref, wg_ref, wu_ref, cw_ref, cb_ref,
              wd_ref, out_ref, xnew_s, xn_s, acc_s):
    i = pl.program_id(1)
    f = pl.program_id(2)
    tm = x_ref.shape[1]

    @pl.when(f == 0)
    def _():
        @pl.when(i == 0)
        def _():
            xn_s[0:TAIL_ROWS, :] = jnp.zeros((TAIL_ROWS, xn_s.shape[1]), xn_s.dtype)

        @pl.when(i > 0)
        def _():
            xn_s[0:TAIL_ROWS, :] = xn_s[tm:tm + TAIL_ROWS, :]

        mixed = jnp.concatenate([o_ref[0] for o_ref in (oa_ref, ob_ref, oc_ref, od_ref, oe_ref)], axis=1)
        xnew = x_ref[0] + _dot(mixed, wo_ref[...])
        xnew_s[...] = xnew
        xn = xnew * lax.rsqrt(jnp.mean(xnew * xnew, axis=-1, keepdims=True) + EPS) * fnorm_ref[...]
        xn_s[TAIL_ROWS:TAIL_ROWS + tm, :] = xn.astype(xn_s.dtype)

    def mlp_chunk():
        ge = _dot(xn_s[...], wg_ref[...])
        u = _dot(xn_s[TAIL_ROWS:TAIL_ROWS + tm, :], wu_ref[...])
        g0 = ge[TAIL_ROWS:TAIL_ROWS + tm, :]
        t1 = ge[TAIL_ROWS - 1:TAIL_ROWS, :]
        t2 = ge[TAIL_ROWS - 2:TAIL_ROWS - 1, :]
        row = lax.broadcasted_iota(jnp.int32, g0.shape, 0)
        g1 = jnp.where(row == 0, t1, pltpu.roll(g0, 1, 0))
        g2 = jnp.where(row == 0, t2, jnp.where(row == 1, t1, pltpu.roll(g0, 2, 0)))
        y = cb_ref[...] + cw_ref[0:1, :] * g2
        y = y + cw_ref[1:2, :] * g1
        y = y + cw_ref[2:3, :] * g0
        hmid = (y * (1.0 / (1.0 + jnp.exp(-y)))) * u
        return _dot(hmid.astype(jnp.bfloat16), wd_ref[...])

    @pl.when(f == 0)
    def _():
        acc_s[...] = mlp_chunk()

    if nf > 2:
        @pl.when((f > 0) & (f < nf - 1))
        def _():
            acc_s[...] += mlp_chunk()

    @pl.when(f == nf - 1)
    def _():
        out_ref[0] = xnew_s[...] + (acc_s[...] + mlp_chunk())


def _ffn_call(layer, x, outs, wo, fnorm, wg, wu, cw, cb, wd, tm, tf):
    bsz, seq, d = x.shape
    dff = wg.shape[2]
    n_tiles, nf = seq // tm, dff // tf
    assert seq % tm == 0 and dff % tf == 0 and nf >= 2
    tok = lambda w: pl.BlockSpec((1, tm, w), lambda b, i, f: (b, i, 0))
    return pl.pallas_call(
        functools.partial(_ffn_body, nf),
        grid=(bsz, n_tiles, nf),
        in_specs=[tok(d)] + [tok(GROUP_WIDTH)] * 5 + [
            _const_spec(wo, layer),
            _const_spec(fnorm, layer),
            pl.BlockSpec((None, d, tf), lambda b, i, f: (layer, 0, f)),
            pl.BlockSpec((None, d, tf), lambda b, i, f: (layer, 0, f)),
            pl.BlockSpec((None, SUBLANES, tf), lambda b, i, f: (layer, 0, f)),
            pl.BlockSpec((None, 1, tf), lambda b, i, f: (layer, 0, f)),
            pl.BlockSpec((None, tf, d), lambda b, i, f: (layer, f, 0))],
        out_specs=tok(d),
        out_shape=jax.ShapeDtypeStruct((bsz, seq, d), jnp.float32),
        scratch_shapes=[pltpu.VMEM((tm, d), jnp.float32), pltpu.VMEM((TAIL_ROWS + tm, d), jnp.bfloat16),
                        pltpu.VMEM((tm, d), jnp.float32)],
        compiler_params=pltpu.CompilerParams(dimension_semantics=("arbitrary", "arbitrary", "arbitrary"),
                                             vmem_limit_bytes=VMEM_LIMIT_BYTES),
        name="ffn",
    )(x, *outs, wo, fnorm, wg, wu, cw, cb, wd)


def _pad_rows(v, width=MXU_WIDTH):
    return jnp.pad(v.astype(jnp.float32), ((0, 0), (0, width - v.shape[1])))


def _tile_rows(g, reps):
    return jnp.tile(g.astype(jnp.float32), (1, reps))


def _pack_in_projection(w):
    idx = _np_in_index()
    pieces, start = [], 0
    while start < PK_END:
        stop = start + 1
        if idx[start] == _SRC_END:
            while stop < PK_END and idx[stop] == _SRC_END:
                stop += 1
            pieces.append(jnp.zeros(w.shape[:-1] + (stop - start,), w.dtype))
        else:
            while stop < PK_END and idx[stop] == idx[stop - 1] + 1:
                stop += 1
            pieces.append(w[..., int(idx[start]):int(idx[stop - 1]) + 1])
        start = stop
    return jnp.concatenate(pieces, axis=-1)


def _zero_col(w):
    return jnp.concatenate([w, jnp.zeros(w.shape[:-1] + (1,), w.dtype)], axis=-1)


def _rope_table(positions):
    pos = positions.astype(jnp.float32)[:, :, None]
    inv = [ROPE_THETA ** (-jnp.arange(0, rot, 2, dtype=jnp.float32) / rot) for rot in (MLA_ROPE, ROT_MOBA, ROT_DIFF)]
    inv = jnp.concatenate(inv + [jnp.zeros((N_FREQ - TR_ONE,), jnp.float32)])
    ang = pos * inv
    c, s = jnp.cos(ang), jnp.sin(ang)
    c_hi = c.astype(jnp.bfloat16)
    c_lo = (c - c_hi.astype(jnp.float32)).astype(jnp.bfloat16)
    s_hi = s.astype(jnp.bfloat16)
    s_lo = (s - s_hi.astype(jnp.float32)).astype(jnp.bfloat16)
    return jnp.concatenate([c_hi, c_lo, s_hi, s_lo], axis=-1)


def _pick_tile(n, pref):
    t = pref
    while n % t:
        t //= 2
    return t


def kernel(x, mem, positions, attn_norm, ffn_norm, mem_norm, w_in, mla_cq_norm, mla_ckv_norm, mla_w_uq, mla_w_ukv, mla_q_norm, mla_k_norm, fox_b_f, fox_q_norm, fox_k_norm, moba_q_norm, moba_k_norm, diff_lambda, diff_q_norm, diff_k_norm, diff_sub_norm, mem_w_kv, mem_q_norm, mem_k_norm, w_o, ffn_w_gate, ffn_w_up, ffn_conv_w, ffn_conv_b, ffn_w_down):
    bsz, seq, d = x.shape
    depth = w_in.shape[0]
    dff = ffn_w_gate.shape[2]
    bf = jnp.bfloat16
    f32 = jnp.float32

    uq_idx = _np_uq_index()
    ukvk_idx, ukvv_idx = _np_ukv_index()
    gpair = jnp.asarray(_np_group_matrix(_PAIR_GROUPS), bf)
    g64 = jnp.asarray(_np_group_matrix(_G64_GROUPS), bf)
    g32 = jnp.asarray(_np_group_matrix(_G32_GROUPS), bf)
    expand = jnp.asarray(_np_rope_expand_all(), bf)
    tril = jnp.asarray(np.tril(np.ones((MOBA_BLOCK, MOBA_BLOCK), np.float32)), bf)
    vsel = jnp.asarray(_np_value_select(), bf)
    tr = _rope_table(positions)

    t_dense = _pick_tile(seq, 1024)
    t_ffn = _pick_tile(seq, 512)
    tf = dff // 2 if (dff // 2) % LANES == 0 else dff

    win = _pack_in_projection(w_in.astype(bf))
    wuq = jnp.take(_zero_col(mla_w_uq), uq_idx, axis=2)
    wuq = jnp.pad(wuq, ((0, 0), (0, MXU_WIDTH - MLA_Q_RANK), (0, 0))).astype(bf)
    wukv = _zero_col(mla_w_ukv)
    wukvk = jnp.take(wukv, ukvk_idx, axis=2).astype(bf)
    wukvv = jnp.take(wukv, ukvv_idx, axis=2).astype(bf)
    pair = lambda g: _pad_rows(_tile_rows(g, 2))
    rows = [jnp.zeros((depth, MXU_WIDTH), f32)] * P_ROWS
    rows[P_CQ] = _pad_rows(mla_cq_norm)
    rows[P_CKV] = _pad_rows(mla_ckv_norm)
    rows[P_GQ] = pair(mla_q_norm) * (LOG2E * MLA_QK ** -0.5)
    rows[P_GK] = pair(mla_k_norm)
    rows[P_FQ] = _tile_rows(fox_q_norm, N_HEADS) * (LOG2E * HEAD_DIM ** -0.5)
    rows[P_FK] = _tile_rows(fox_k_norm, N_HEADS)
    rows[P_FB] = _pad_rows(fox_b_f)
    rows[P_MQ] = _tile_rows(moba_q_norm, N_HEADS)
    rows[P_MK] = _tile_rows(moba_k_norm, N_HEADS)
    rows[P_DQ] = _tile_rows(diff_q_norm, 2 * N_HEADS) * (LOG2E * DIFF_QK ** -0.5)
    rows[P_DK] = _tile_rows(diff_k_norm, 2 * N_HEADS)
    rows[P_EQ] = _tile_rows(mem_q_norm, N_HEADS) * (LOG2E * HEAD_DIM ** -0.5)
    par = jnp.stack(rows, axis=1)
    anorm = attn_norm.astype(f32)[:, None, :]
    mnorm = mem_norm.astype(f32)[:, None, :]
    fnorm = ffn_norm.astype(f32)[:, None, :]
    wmem = mem_w_kv.astype(bf)
    mem_gain = _tile_rows(mem_k_norm, N_HEADS)[:, None, :]

    lam_init = jnp.asarray([0.8 - 0.6 * math.exp(-0.3 * l) for l in range(depth)], f32)
    lam_vec = diff_lambda.astype(f32)
    lam = (jnp.exp(jnp.sum(lam_vec[:, 0] * lam_vec[:, 1], axis=-1))
           - jnp.exp(jnp.sum(lam_vec[:, 2] * lam_vec[:, 3], axis=-1)) + lam_init)
    lam_row = jnp.broadcast_to(lam[:, None, None], (depth, 1, GROUP_WIDTH))
    gsub = (_tile_rows(diff_sub_norm, N_HEADS) * (1.0 - lam_init)[:, None])[:, None, :]

    wo = w_o.astype(bf)
    wg, wu, wd = ffn_w_gate.astype(bf), ffn_w_up.astype(bf), ffn_w_down.astype(bf)
    cw = jnp.pad(ffn_conv_w.astype(f32), ((0, 0), (0, SUBLANES - CONV_WIDTH), (0, 0)))
    cb = ffn_conv_b.astype(f32)[:, None, :]

    for l in range(depth):
        (aq, ak, av, fq, fk, fv, fdcol, fdrow, mq, mk, mv, msel, dq, dk, dv, eq) = _prep_call(
            l, x, tr, anorm, win, wuq, wukvk, wukvv, gpair, g64, g32, expand, tril, vsel, par)
        ek, ev = _memkv_call(l, mem, mnorm, wmem, g64, mem_gain, vsel)

        o_a = _attn_call(_CFG_MLA, aq, ak, av, (), t_dense, t_dense)
        o_b = _attn_call(_CFG_FOX, fq, fk, fv, (fdcol, fdrow), t_dense, t_dense)
        t_moba = max(t_dense, MOBA_BLOCK)
        o_c = _attn_call(_CFG_MOBA, mq, mk, mv, (msel,), t_moba, t_moba)
        o_d = _attn_call(_CFG_DIFF, dq, dk, dv, (l, g64, gsub, lam_row), t_dense, t_dense)
        o_e = _attn_call(_CFG_MEM, eq, ek, ev, (), t_dense, mem.shape[1])

        x = _ffn_call(l, x, (o_a, o_b, o_c, o_d, o_e), wo, fnorm, wg, wu, cw, cb, wd, t_ffn, tf)
    return x
```

```python
import functools
import math

import numpy as np
import jax
import jax.numpy as jnp
from jax import lax
from jax.experimental import pallas as pl
from jax.experimental.pallas import tpu as pltpu

N_HEADS = 4
HEAD_DIM = 64
GROUP_WIDTH = N_HEADS * HEAD_DIM
MLA_Q_RANK = 192
MLA_KV_RANK = 128
MLA_NOPE = 64
MLA_ROPE = 32
MLA_QK = MLA_NOPE + MLA_ROPE
DIFF_QK = HEAD_DIM // 2
ROPE_THETA = 500000.0
ROT_MOBA = HEAD_DIM // 4
ROT_DIFF = DIFF_QK // 4
MOBA_BLOCK = 256
MOBA_TOPK = 3
CONV_WIDTH = 3
EPS = 1e-6
NEG_INF = -1e30
LOG2E = math.log2(math.e)
REMOVED = -3e38

LANES = 128
SUBLANES = 8
MXU_WIDTH = 256
TAIL_ROWS = 16
PREP_TILE = 512
VMEM_LIMIT_BYTES = 56 * 1024 * 1024

_SRC_CQ = 0
_SRC_CKV = _SRC_CQ + MLA_Q_RANK
_SRC_KR = _SRC_CKV + MLA_KV_RANK
_SRC_FOX = _SRC_KR + MLA_ROPE
_SRC_FOXF = _SRC_FOX + 3 * GROUP_WIDTH
_SRC_MOBA = _SRC_FOXF + N_HEADS
_SRC_DIFF = _SRC_MOBA + 3 * GROUP_WIDTH
_SRC_MEMQ = _SRC_DIFF + 3 * GROUP_WIDTH
_SRC_END = _SRC_MEMQ + GROUP_WIDTH

PK_CQ = 0
PK_CKV = 256
PK_KR = 384
PK_FQ, PK_FK, PK_FV = 896, 1152, 1408
PK_FF = 1664
PK_MQ, PK_MK, PK_MV = 1792, 2048, 2304
PK_DQ, PK_DK, PK_DV = 2560, 2816, 3072
PK_EQ = 3328
PK_END = 3584

PAIR_STRIDE = MLA_QK


def _pair_lane(h, d):
    return (h // 2) * MXU_WIDTH + (h % 2) * PAIR_STRIDE + d


N_FREQ = 32
FREQ_BASE_MLA = 0
FREQ_BASE_MOBA = MLA_ROPE // 2
FREQ_BASE_DIFF = FREQ_BASE_MOBA + ROT_MOBA // 2
TR_ONE = FREQ_BASE_DIFF + ROT_DIFF // 2
TR_WIDTH = 4 * N_FREQ
assert TR_ONE < N_FREQ and TR_WIDTH == LANES

(P_CQ, P_CKV, P_GQ, P_GK, P_FQ, P_FK, P_FB, P_MQ, P_MK, P_DQ, P_DK, P_EQ) = range(12)
P_ROWS = 16


def _np_in_index():
    idx = np.full((PK_END,), _SRC_END, np.int32)
    idx[PK_CQ:PK_CQ + MLA_Q_RANK] = np.arange(_SRC_CQ, _SRC_CQ + MLA_Q_RANK)
    idx[PK_CKV:PK_CKV + MLA_KV_RANK] = np.arange(_SRC_CKV, _SRC_CKV + MLA_KV_RANK)
    for h in range(N_HEADS):
        for d in range(MLA_ROPE):
            idx[PK_KR + _pair_lane(h, d)] = _SRC_KR + d
    idx[PK_FQ:PK_FQ + 3 * GROUP_WIDTH] = np.arange(_SRC_FOX, _SRC_FOX + 3 * GROUP_WIDTH)
    idx[PK_FF:PK_FF + N_HEADS] = np.arange(_SRC_FOXF, _SRC_FOXF + N_HEADS)
    idx[PK_MQ:PK_MQ + 3 * GROUP_WIDTH] = np.arange(_SRC_MOBA, _SRC_MOBA + 3 * GROUP_WIDTH)
    idx[PK_DQ:PK_DQ + 3 * GROUP_WIDTH] = np.arange(_SRC_DIFF, _SRC_DIFF + 3 * GROUP_WIDTH)
    idx[PK_EQ:PK_EQ + GROUP_WIDTH] = np.arange(_SRC_MEMQ, _SRC_MEMQ + GROUP_WIDTH)
    return idx


def _np_uq_index():
    idx = np.full((2 * MXU_WIDTH,), N_HEADS * MLA_QK, np.int32)
    for h in range(N_HEADS):
        for d in range(MLA_QK):
            idx[_pair_lane(h, d)] = h * MLA_QK + d
    return idx


def _np_ukv_index():
    zero = N_HEADS * (MLA_NOPE + HEAD_DIM)
    idx_k = np.full((2 * MXU_WIDTH,), zero, np.int32)
    idx_v = np.zeros((GROUP_WIDTH,), np.int32)
    for h in range(N_HEADS):
        for d in range(MLA_NOPE):
            idx_k[_pair_lane(h, MLA_ROPE + d)] = h * (MLA_NOPE + HEAD_DIM) + d
        for d in range(HEAD_DIM):
            idx_v[h * HEAD_DIM + d] = h * (MLA_NOPE + HEAD_DIM) + MLA_NOPE + d
    return idx_k, idx_v


def _np_group_matrix(groups):
    g = np.zeros((MXU_WIDTH, MXU_WIDTH), np.float32)
    for lo, size in groups:
        g[lo:lo + size, lo:lo + size] = 1.0 / size
    return g


_PAIR_GROUPS = [(0, MLA_ROPE), (MLA_ROPE, MLA_NOPE), (PAIR_STRIDE, MLA_ROPE), (PAIR_STRIDE + MLA_ROPE, MLA_NOPE)]
_G64_GROUPS = [(h * HEAD_DIM, HEAD_DIM) for h in range(N_HEADS)]
_G32_GROUPS = [(g * DIFF_QK, DIFF_QK) for g in range(2 * N_HEADS)]


def _np_rope_expand(regions, rot, base):
    half = rot // 2
    e = np.zeros((TR_WIDTH, 2 * MXU_WIDTH), np.float32)
    e[TR_ONE, 0:MXU_WIDTH] = 1.0
    for lo in regions:
        assert lo % rot == 0
        for r in range(half):
            f = base + r
            for lane, sign in ((lo + r, -1.0), (lo + half + r, 1.0)):
                e[TR_ONE, lane] = 0.0
                e[f, lane] = 1.0
                e[N_FREQ + f, lane] = 1.0
                e[2 * N_FREQ + f, MXU_WIDTH + lane] = sign
                e[3 * N_FREQ + f, MXU_WIDTH + lane] = sign
    return e


def _np_rope_expand_all():
    return np.concatenate([
        _np_rope_expand([0, PAIR_STRIDE], MLA_ROPE, FREQ_BASE_MLA),
        _np_rope_expand([h * HEAD_DIM for h in range(N_HEADS)], ROT_MOBA, FREQ_BASE_MOBA),
        _np_rope_expand([g * DIFF_QK for g in range(2 * N_HEADS)], ROT_DIFF, FREQ_BASE_DIFF),
    ], axis=1)


def _dot(a, b):
    return jnp.dot(a, b, preferred_element_type=jnp.float32)


def _dot_nt(a, b):
    return lax.dot_general(a, b, (((1,), (1,)), ((), ())), preferred_element_type=jnp.float32)


def _split2(a):
    hi = a.astype(jnp.bfloat16)
    lo = (a - hi.astype(jnp.float32)).astype(jnp.bfloat16)
    return hi, lo


def _split3(a):
    hi = a.astype(jnp.bfloat16)
    r = a - hi.astype(jnp.float32)
    mid = r.astype(jnp.bfloat16)
    lo = (r - mid.astype(jnp.float32)).astype(jnp.bfloat16)
    return hi, mid, lo


def _group_mean_sq(a, g_bf16):
    return _dot((a * a).astype(jnp.bfloat16), g_bf16)


def _rope(x, tabs, half):
    w = x.shape[-1]
    lane = lax.broadcasted_iota(jnp.int32, x.shape, 1)
    partner = jnp.where((lane & (2 * half - 1)) >= half, pltpu.roll(x, half, 1), pltpu.roll(x, w - half, 1))
    return x * tabs[:, 0:w] + partner * tabs[:, w:2 * w]


def _lane_mask(shape, lo, hi):
    lane = lax.broadcasted_iota(jnp.int32, shape, len(shape) - 1)
    return (lane >= lo) & (lane < hi)


def _prep_body(x_ref, tr_ref, anorm_ref, win_ref, wuq_ref, wukvk_ref, wukvv_ref, gpair_ref, g64_ref, g32_ref,
               exp_ref, tril_ref, vsel_ref, par_ref,
               aq_ref, ak_ref, av_ref, fq_ref, fk_ref, fv_ref, fdcol_ref, fdrow_ref, mq_ref, mk_ref, mv_ref, msel_ref,
               dq_ref, dk_ref, dv_ref, eq_ref,
               kmean_s, carry_s):
    j = pl.program_id(1)
    tm = x_ref.shape[1]
    bf = jnp.bfloat16

    @pl.when(j == 0)
    def _():
        kmean_s[...] = jnp.zeros_like(kmean_s)
        carry_s[...] = jnp.zeros_like(carry_s)

    x = x_ref[0]
    xn = x * lax.rsqrt(jnp.mean(x * x, axis=-1, keepdims=True) + EPS) * anorm_ref[...]
    xb = xn.astype(bf)

    def proj(off, width):
        return _dot(xb, win_ref[:, off:off + width])

    def prow(r, width=MXU_WIDTH):
        return par_ref[r:r + 1, 0:width]

    gpair, g64, g32 = gpair_ref[...], g64_ref[...], g32_ref[...]
    vsel = vsel_ref[...]

    tabs = _dot(tr_ref[0], exp_ref[...])
    tab_mla = tabs[:, 0:2 * MXU_WIDTH]
    tab_moba = tabs[:, 2 * MXU_WIDTH:4 * MXU_WIDTH]
    tab_diff = tabs[:, 4 * MXU_WIDTH:6 * MXU_WIDTH]
    def proj_run(lo, hi):
        wide = proj(lo, hi - lo)
        return lambda off, width: wide[:, off - lo:off - lo + width]

    run_m = proj_run(PK_MQ, PK_DQ)
    run_a = proj_run(PK_CQ, PK_FQ)
    run_f = proj_run(PK_FQ, PK_MQ)
    run_d = proj_run(PK_DQ, PK_END)
    direct = {"mq": run_m(PK_MQ, MXU_WIDTH), "mk": run_m(PK_MK, MXU_WIDTH),
              "fq": run_f(PK_FQ, MXU_WIDTH), "fk": run_f(PK_FK, MXU_WIDTH),
              "dq": run_d(PK_DQ, MXU_WIDTH), "dk": run_d(PK_DK, MXU_WIDTH), "eq": run_d(PK_EQ, MXU_WIDTH)}
    p_cq = run_a(PK_CQ, MXU_WIDTH)
    p_ckv = run_a(PK_CKV, MLA_KV_RANK)
    p_kr = run_a(PK_KR, 2 * MXU_WIDTH)
    p_ff = run_f(PK_FF, LANES)
    values = {"fv": run_f(PK_FV, MXU_WIDTH).astype(bf), "mv": run_m(PK_MV, MXU_WIDTH).astype(bf),
              "dv": run_d(PK_DV, MXU_WIDTH).astype(bf)}

    cqn = p_cq * lax.rsqrt(jnp.sum(p_cq * p_cq, axis=-1, keepdims=True) * (1.0 / MLA_Q_RANK) + EPS) * prow(P_CQ)
    ckvn = p_ckv * lax.rsqrt(jnp.mean(p_ckv * p_ckv, axis=-1, keepdims=True) + EPS) * prow(P_CKV, MLA_KV_RANK)
    ckvb = ckvn.astype(bf)
    z = p_ff + prow(P_FB, LANES)
    log_f = jnp.minimum(z, 0.0) - jnp.log1p(jnp.exp(-jnp.abs(z)))
    log_f = jnp.where(_lane_mask(log_f.shape, 0, N_HEADS), log_f, 0.0)
    l1, l2, l3 = _split3(log_f)

    gmat = {"fq": g64, "fk": g64, "mq": g64, "mk": g64, "dq": g32, "dk": g32, "eq": g64}
    ms = {name: _group_mean_sq(a, gmat[name]) for name, a in direct.items()}
    qa = _dot(cqn.astype(bf), wuq_ref[...])
    ka = p_kr + _dot(ckvb, wukvk_ref[...])
    values["av"] = _dot(ckvb, wukvv_ref[...]).astype(bf)
    tril = tril_ref[...]
    dec = carry_s[...] + ((_dot(tril, l1) + _dot(tril, l2)) + _dot(tril, l3))
    carry_s[...] = dec[tm - 1:tm, :]

    gains = {"fq": P_FQ, "fk": P_FK, "mq": P_MQ, "mk": P_MK, "dq": P_DQ, "dk": P_DK, "eq": P_EQ}
    normed = {name: a * lax.rsqrt(ms[name] + EPS) * prow(gains[name]) for name, a in direct.items()}
    mq = _rope(normed["mq"], tab_moba, ROT_MOBA // 2)
    mk = _rope(normed["mk"], tab_moba, ROT_MOBA // 2)
    n_sub = tm // MOBA_BLOCK
    for sb in range(n_sub):
        kmean_s[pl.ds(j * n_sub + sb, 1), :] = jnp.mean(mk[sb * MOBA_BLOCK:(sb + 1) * MOBA_BLOCK, :], axis=0,
                                                        keepdims=True)
    km_hi, km_lo = _split2(kmean_s[...])
    gates = []
    for h in range(N_HEADS):
        q_hi, q_lo = _split2(jnp.where(_lane_mask(mq.shape, h * HEAD_DIM, (h + 1) * HEAD_DIM), mq, 0.0))
        gates.append((_dot_nt(km_hi, q_hi) + _dot_nt(km_lo, q_hi)) + _dot_nt(km_hi, q_lo))
    ms_qa = [_group_mean_sq(qa[:, p * MXU_WIDTH:(p + 1) * MXU_WIDTH], gpair) for p in range(2)]
    ms_ka = [_group_mean_sq(ka[:, p * MXU_WIDTH:(p + 1) * MXU_WIDTH], gpair) for p in range(2)]
    dec2 = dec * LOG2E
    d1, d2, d3 = _split3(dec2)
    for h in range(N_HEADS):
        fdcol_ref[0, :, h * LANES:(h + 1) * LANES] = jnp.broadcast_to(dec2[:, h:h + 1], (tm, LANES))
    row_sel = jnp.where(lax.broadcasted_iota(jnp.int32, (SUBLANES, LANES), 0)
                        == lax.broadcasted_iota(jnp.int32, (SUBLANES, LANES), 1), 1.0, 0.0).astype(bf)
    fdrow_ref[0] = (_dot_nt(row_sel, d1) + _dot_nt(row_sel, d2)) + _dot_nt(row_sel, d3)
    for name, ref in (("fv", fv_ref), ("mv", mv_ref), ("dv", dv_ref), ("av", av_ref)):
        ref[0] = _values_t(values[name], vsel)

    fq_ref[0] = normed["fq"].astype(bf)
    fk_ref[0] = normed["fk"].astype(bf)
    eq_ref[0] = normed["eq"].astype(bf)
    dq_ref[0] = _rope(normed["dq"], tab_diff, ROT_DIFF // 2).astype(bf)
    dk_ref[0] = _rope(normed["dk"], tab_diff, ROT_DIFF // 2).astype(bf)
    mq_ref[0] = (mq * (LOG2E * HEAD_DIM ** -0.5)).astype(bf)
    mk_ref[0] = mk.astype(bf)
    for p in range(2):
        sl = slice(p * MXU_WIDTH, (p + 1) * MXU_WIDTH)
        aq_ref[0, :, sl] = _rope(qa[:, sl] * lax.rsqrt(ms_qa[p] + EPS) * prow(P_GQ), tab_mla,
                                 MLA_ROPE // 2).astype(bf)
        ak_ref[0, :, sl] = _rope(ka[:, sl] * lax.rsqrt(ms_ka[p] + EPS) * prow(P_GK), tab_mla,
                                 MLA_ROPE // 2).astype(bf)

    nbp = kmean_s.shape[0]
    blk = lax.broadcasted_iota(jnp.int32, (nbp, tm), 0)
    own_blk = j * n_sub + lax.broadcasted_iota(jnp.int32, (nbp, tm), 1) // MOBA_BLOCK
    past = blk < own_blk
    for h in range(N_HEADS):
        work = jnp.where(past, gates[h], NEG_INF)
        sel = jnp.zeros((nbp, tm), jnp.bool_)
        for _ in range(MOBA_TOPK):
            mx = jnp.max(work, axis=0, keepdims=True)
            first = jnp.min(jnp.where(work == mx, blk, nbp), axis=0, keepdims=True)
            pick = blk == first
            sel = sel | pick
            work = jnp.where(pick, REMOVED, work)
        msel_ref[0, h * nbp:(h + 1) * nbp, :] = jnp.where(sel & past, 0.0, NEG_INF)


def _const_spec(a, layer=None):
    if layer is None:
        n = a.ndim
        return pl.BlockSpec(a.shape, lambda *_: (0,) * n)
    n = a.ndim - 1
    return pl.BlockSpec((None,) + a.shape[1:], lambda *_: (layer,) + (0,) * n)


def _moba_blocks_padded(seq):
    return -(-(seq // MOBA_BLOCK) // SUBLANES) * SUBLANES


def _prep_call(layer, x, tr, anorm, win, wuq, wukvk, wukvv, gpair, g64, g32, expand, tril, vsel, par):
    bsz, seq, d = x.shape
    tm = tril.shape[0]
    assert seq % tm == 0 and tm % MOBA_BLOCK == 0
    nbp = _moba_blocks_padded(seq)
    bf = jnp.bfloat16
    f32 = jnp.float32

    vt = -N_HEADS * V_ROWS
    widths = [(2 * MXU_WIDTH, bf), (2 * MXU_WIDTH, bf), (vt, bf),
              (GROUP_WIDTH, bf), (GROUP_WIDTH, bf), (vt, bf),
              (N_HEADS * LANES, f32), (-SUBLANES, f32),
              (GROUP_WIDTH, bf), (GROUP_WIDTH, bf), (vt, bf), (-N_HEADS * nbp, f32),
              (GROUP_WIDTH, bf), (GROUP_WIDTH, bf), (vt, bf),
              (GROUP_WIDTH, bf)]

    def tok(width):
        if width > 0:
            return pl.BlockSpec((1, tm, width), lambda b, j: (b, j, 0))
        return pl.BlockSpec((1, -width, tm), lambda b, j: (b, 0, j))

    def shape(width):
        return (bsz, seq, width) if width > 0 else (bsz, -width, seq)

    consts = [anorm, win, wuq, wukvk, wukvv, gpair, g64, g32, expand, tril, vsel, par]
    layered = [True, True, True, True, True, False, False, False, False, False, False, True]
    return pl.pallas_call(
        _prep_body,
        grid=(bsz, seq // tm),
        in_specs=[tok(d), tok(TR_WIDTH)] + [_const_spec(c, layer if ly else None) for c, ly in zip(consts, layered)],
        out_specs=[tok(w) for w, _ in widths],
        out_shape=[jax.ShapeDtypeStruct(shape(w), dt) for w, dt in widths],
        scratch_shapes=[pltpu.VMEM((nbp, GROUP_WIDTH), jnp.float32), pltpu.VMEM((1, LANES), jnp.float32)],
        compiler_params=pltpu.CompilerParams(dimension_semantics=("arbitrary", "arbitrary"),
                                             vmem_limit_bytes=VMEM_LIMIT_BYTES),
        name="prep",
    )(x, tr, *consts)


def _memkv_body(mem_ref, mnorm_ref, w_ref, g64_ref, gain_ref, vsel_ref, k_ref, v_ref):
    m = mem_ref[0]
    mn = m * lax.rsqrt(jnp.mean(m * m, axis=-1, keepdims=True) + EPS) * mnorm_ref[...]
    kv = _dot(mn.astype(jnp.bfloat16), w_ref[...])
    k = kv[:, 0:GROUP_WIDTH]
    k = k * lax.rsqrt(_group_mean_sq(k, g64_ref[...]) + EPS) * gain_ref[...]
    k_ref[0] = k.astype(jnp.bfloat16)
    v_ref[0] = _values_t(kv[:, GROUP_WIDTH:2 * GROUP_WIDTH].astype(jnp.bfloat16), vsel_ref[...])


def _memkv_call(layer, mem, mnorm, w, g64, gain, vsel):
    bsz, mlen, d = mem.shape
    k_shape, vt_shape = (bsz, mlen, GROUP_WIDTH), (bsz, N_HEADS * V_ROWS, mlen)
    return pl.pallas_call(
        _memkv_body,
        grid=(bsz,),
        in_specs=[pl.BlockSpec((1, mlen, d), lambda b: (b, 0, 0)), _const_spec(mnorm, layer), _const_spec(w, layer),
                  _const_spec(g64), _const_spec(gain, layer), _const_spec(vsel)],
        out_specs=[pl.BlockSpec((1,) + s[1:], lambda b: (b, 0, 0)) for s in (k_shape, vt_shape)],
        out_shape=[jax.ShapeDtypeStruct(s, jnp.bfloat16) for s in (k_shape, vt_shape)],
        compiler_params=pltpu.CompilerParams(dimension_semantics=("arbitrary",), vmem_limit_bytes=VMEM_LIMIT_BYTES),
        name="mem_kv",
    )(mem, mnorm, w, g64, gain, vsel)


class _AttnCfg:
    def __init__(self, name, vheads, n_maps, causal, decay=False, select=False, diff=False):
        self.name = name
        self.vheads = vheads
        self.n_maps = n_maps
        self.causal = causal
        self.decay = decay
        self.select = select
        self.diff = diff


_PLAIN_VHEADS = [(0, h * HEAD_DIM, (h + 1) * HEAD_DIM, 0, h) for h in range(N_HEADS)]
_CFG_MLA = _AttnCfg("attn_mla", [((h // 2) * MXU_WIDTH, (h % 2) * PAIR_STRIDE, (h % 2) * PAIR_STRIDE + MLA_QK, 0, h)
                                 for h in range(N_HEADS)], 1, True)
_CFG_FOX = _AttnCfg("attn_fox", _PLAIN_VHEADS, 1, True, decay=True)
_CFG_MOBA = _AttnCfg("attn_moba", _PLAIN_VHEADS, 1, True, select=True)
_CFG_DIFF = _AttnCfg("attn_diff", [(0, h * HEAD_DIM + c * DIFF_QK, h * HEAD_DIM + (c + 1) * DIFF_QK, c, h)
                                   for c in range(2) for h in range(N_HEADS)], 2, True, diff=True)
_CFG_MEM = _AttnCfg("attn_mem", _PLAIN_VHEADS, 1, False)


ONES_ROW = HEAD_DIM
V_ROWS = HEAD_DIM + 16
QK_LOOKAHEAD = 8
Q_SUB = 2 * MXU_WIDTH


def _np_value_select():
    sel = np.zeros((N_HEADS * V_ROWS, GROUP_WIDTH), np.float32)
    for h in range(N_HEADS):
        for d in range(HEAD_DIM):
            sel[h * V_ROWS + d, h * HEAD_DIM + d] = 1.0
    return sel


def _values_t(v, vsel):
    vt = _dot_nt(vsel, v)
    row = lax.broadcasted_iota(jnp.int32, vt.shape, 0)
    ones = row == ONES_ROW
    for h in range(1, N_HEADS):
        ones = ones | (row == h * V_ROWS + ONES_ROW)
    return jnp.where(ones, 1.0, vt).astype(jnp.bfloat16)


def _tile_lanes(x, width):
    return jnp.tile(x, (1, width // LANES)) if width != LANES else x


def _attn_body(cfg, qi_ref, kj_ref, *refs):
    refs = list(refs)
    q_ref, k_ref, vt_ref = refs[:3]
    pos = 3
    if cfg.decay:
        dq_ref, dk_ref = refs[pos:pos + 2]
        pos += 2
    if cfg.select:
        sel_ref = refs[pos]
        pos += 1
    if cfg.diff:
        g64_ref, gsub_ref, lam_ref = refs[pos:pos + 3]
        pos += 3
    o_ref, qm_s, m_s, acc_s = refs[pos:pos + 4]

    t = pl.program_id(1)
    i = qi_ref[t]
    j = kj_ref[t]
    tq = q_ref.shape[1]
    tk = k_ref.shape[1]

    @pl.when(j == 0)
    def _():
        for n, (off, lo, hi, _, _) in enumerate(cfg.vheads):
            qb = q_ref[0, :, off:off + MXU_WIDTH]
            qm_s[n] = jnp.where(_lane_mask(qb.shape, lo, hi), qb, jnp.zeros_like(qb))
        m_s[...] = jnp.full(m_s.shape, NEG_INF, jnp.float32)
        acc_s[...] = jnp.zeros_like(acc_s)

    def step(diag):
        qs = min(tq, Q_SUB)
        items = [(n, u) for n in range(len(cfg.vheads)) for u in range(tq // qs)]

        def n_keys(u):
            return (u + 1) * qs if diag else tk

        def scores(item):
            n, u = item
            off, _, _, _, h = cfg.vheads[n]
            nk = n_keys(u)
            cols = slice(u * qs, (u + 1) * qs)
            s = _dot_nt(k_ref[0, 0:nk, off:off + MXU_WIDTH], qm_s[n, cols, :])
            if cfg.decay:
                s = (dq_ref[0, h:h + 1, cols] - _tile_lanes(dk_ref[0, 0:nk, h * LANES:(h + 1) * LANES], qs)) + s
            if cfg.select:
                nbp = sel_ref.shape[1] // N_HEADS
                qpos = u * qs + lax.broadcasted_iota(jnp.int32, (1, qs), 1)
                parts = []
                for kb in range(nk // MOBA_BLOCK):
                    rows = s[kb * MOBA_BLOCK:(kb + 1) * MOBA_BLOCK, :]
                    if not (diag and kb == nk // MOBA_BLOCK - 1):
                        bias = sel_ref[0, pl.ds(h * nbp + j * (tk // MOBA_BLOCK) + kb, 1), cols]
                        if diag:
                            bias = jnp.where(qpos < (kb + 1) * MOBA_BLOCK, 0.0, bias)
                        rows = rows + bias
                    parts.append(rows)
                s = parts[0] if len(parts) == 1 else jnp.concatenate(parts, axis=0)
            if diag:
                key = lax.broadcasted_iota(jnp.int32, (nk, qs), 0)
                qry = u * qs + lax.broadcasted_iota(jnp.int32, (nk, qs), 1)
                s = jnp.where(key <= qry, s, NEG_INF)
            return s, jnp.max(s, axis=0, keepdims=True)

        raw = {it: scores(items[it]) for it in range(min(QK_LOOKAHEAD, len(items)))}
        for it, (n, u) in enumerate(items):
            h = cfg.vheads[n][4]
            nk = n_keys(u)
            cols = slice(u * qs, (u + 1) * qs)
            s, s_max = raw.pop(it)
            m_prev = m_s[n, :, cols]
            m_new = jnp.maximum(m_prev, s_max)
            alpha = jnp.exp2(m_prev - m_new)
            p = jnp.exp2(s - m_new)
            m_s[n, :, cols] = m_new
            acc_s[n, :, cols] = acc_s[n, :, cols] * alpha + _dot(vt_ref[0, h * V_ROWS:(h + 1) * V_ROWS, 0:nk],
                                                                 p.astype(jnp.bfloat16))
            if it + QK_LOOKAHEAD < len(items):
                raw[it + QK_LOOKAHEAD] = scores(items[it + QK_LOOKAHEAD])

    if cfg.causal:
        pl.when(j < i)(functools.partial(step, False))
        pl.when(j == i)(functools.partial(step, True))
        last = j == i
    else:
        step(False)
        last = j == 0

    @pl.when(last)
    def _():
        outs = []
        for c in range(cfg.n_maps):
            heads = []
            for h in range(N_HEADS):
                acc = acc_s[c * N_HEADS + h]
                heads.append(acc[0:HEAD_DIM, :] / acc[ONES_ROW:ONES_ROW + 1, :])
            outs.append(jnp.concatenate(heads, axis=0).T)
        if cfg.diff:
            o = outs[0] - lam_ref[0:1, :] * outs[1]
            o = o * lax.rsqrt(_group_mean_sq(o, g64_ref[...]) + EPS) * gsub_ref[...]
        else:
            o = outs[0]
        o_ref[0] = o.astype(o_ref.dtype)


def _attn_call(cfg, q, k, v, extras, tq, tk):
    bsz, seq, wq = q.shape
    sk = k.shape[1]
    nq = seq // tq
    assert seq % tq == 0 and sk % tk == 0
    if cfg.causal:
        assert tq == tk and sk == seq
        pairs = [(i, j) for i in range(nq) for j in range(i + 1)]
    else:
        assert sk == tk
        pairs = [(i, 0) for i in range(nq)]
    qi = jnp.asarray(np.array([p[0] for p in pairs], np.int32))
    kj = jnp.asarray(np.array([p[1] for p in pairs], np.int32))
    n_vh = len(cfg.vheads)

    in_specs = [pl.BlockSpec((1, tq, wq), lambda b, t, qi, kj: (b, qi[t], 0)),
                pl.BlockSpec((1, tk, wq), lambda b, t, qi, kj: (b, kj[t], 0)),
                pl.BlockSpec((1, N_HEADS * V_ROWS, tk), lambda b, t, qi, kj: (b, 0, kj[t]))]
    args = [q, k, v]
    if cfg.decay:
        dcol, drow = extras
        in_specs += [pl.BlockSpec((1, SUBLANES, tq), lambda b, t, qi, kj: (b, 0, qi[t])),
                     pl.BlockSpec((1, tk, N_HEADS * LANES), lambda b, t, qi, kj: (b, kj[t], 0))]
        args += [drow, dcol]
    if cfg.select:
        (sel,) = extras
        in_specs += [pl.BlockSpec((1, sel.shape[1], tq), lambda b, t, qi, kj: (b, 0, qi[t]))]
        args += [sel]
    if cfg.diff:
        layer, g64, gsub, lam_row = extras
        in_specs += [_const_spec(g64), _const_spec(gsub, layer), _const_spec(lam_row, layer)]
        args += [g64, gsub, lam_row]

    grid_spec = pltpu.PrefetchScalarGridSpec(
        num_scalar_prefetch=2,
        grid=(bsz, len(pairs)),
        in_specs=in_specs,
        out_specs=pl.BlockSpec((1, tq, GROUP_WIDTH), lambda b, t, qi, kj: (b, qi[t], 0)),
        scratch_shapes=[pltpu.VMEM((n_vh, tq, MXU_WIDTH), jnp.bfloat16),
                        pltpu.VMEM((n_vh, 1, tq), jnp.float32),
                        pltpu.VMEM((n_vh, V_ROWS, tq), jnp.float32)])
    return pl.pallas_call(
        functools.partial(_attn_body, cfg),
        grid_spec=grid_spec,
        out_shape=jax.ShapeDtypeStruct((bsz, seq, GROUP_WIDTH), jnp.bfloat16),
        compiler_params=pltpu.CompilerParams(dimension_semantics=("arbitrary", "arbitrary"),
                                             vmem_limit_bytes=VMEM_LIMIT_BYTES),
        name=cfg.name,
    )(qi, kj, *args)


def _ffn_body(nf, x_ref, oa_ref, ob_ref, oc_ref, od_ref, oe_ref, wo_ref, fnorm_ref, wg_ref, wu_ref, cw_ref, cb_ref,
              wd_ref, out_ref, xnew_s, xn_s, acc_s):
    i = pl.program_id(1)
    f = pl.program_id(2)
    tm = x_ref.shape[1]

    @pl.when(f == 0)
    def _():
        @pl.when(i == 0)
        def _():
            xn_s[0:TAIL_ROWS, :] = jnp.zeros((TAIL_ROWS, xn_s.shape[1]), xn_s.dtype)

        @pl.when(i > 0)
        def _():
            xn_s[0:TAIL_ROWS, :] = xn_s[tm:tm + TAIL_ROWS, :]

        mixed = jnp.concatenate([o_ref[0] for o_ref in (oa_ref, ob_ref, oc_ref, od_ref, oe_ref)], axis=1)
        xnew = x_ref[0] + _dot(mixed, wo_ref[...])
        xnew_s[...] = xnew
        xn = xnew * lax.rsqrt(jnp.mean(xnew * xnew, axis=-1, keepdims=True) + EPS) * fnorm_ref[...]
        xn_s[TAIL_ROWS:TAIL_ROWS + tm, :] = xn.astype(xn_s.dtype)

    def mlp_chunk():
        ge = _dot(xn_s[...], wg_ref[...])
        u = _dot(xn_s[TAIL_ROWS:TAIL_ROWS + tm, :], wu_ref[...])
        g0 = ge[TAIL_ROWS:TAIL_ROWS + tm, :]
        t1 = ge[TAIL_ROWS - 1:TAIL_ROWS, :]
        t2 = ge[TAIL_ROWS - 2:TAIL_ROWS - 1, :]
        row = lax.broadcasted_iota(jnp.int32, g0.shape, 0)
        g1 = jnp.where(row == 0, t1, pltpu.roll(g0, 1, 0))
        g2 = jnp.where(row == 0, t2, jnp.where(row == 1, t1, pltpu.roll(g0, 2, 0)))
        y = cb_ref[...] + cw_ref[0:1, :] * g2
        y = y + cw_ref[1:2, :] * g1
        y = y + cw_ref[2:3, :] * g0
        hmid = (y * (1.0 / (1.0 + jnp.exp(-y)))) * u
        return _dot(hmid.astype(jnp.bfloat16), wd_ref[...])

    @pl.when(f == 0)
    def _():
        acc_s[...] = mlp_chunk()

    if nf > 2:
        @pl.when((f > 0) & (f < nf - 1))
        def _():
            acc_s[...] += mlp_chunk()

    @pl.when(f == nf - 1)
    def _():
        out_ref[0] = xnew_s[...] + (acc_s[...] + mlp_chunk())


def _ffn_call(layer, x, outs, wo, fnorm, wg, wu, cw, cb, wd, tm, tf):
    bsz, seq, d = x.shape
    dff = wg.shape[2]
    n_tiles, nf = seq // tm, dff // tf
    assert seq % tm == 0 and dff % tf == 0 and nf >= 2
    tok = lambda w: pl.BlockSpec((1, tm, w), lambda b, i, f: (b, i, 0))
    return pl.pallas_call(
        functools.partial(_ffn_body, nf),
        grid=(bsz, n_tiles, nf),
        in_specs=[tok(d)] + [tok(GROUP_WIDTH)] * 5 + [
            _const_spec(wo, layer),
            _const_spec(fnorm, layer),
            pl.BlockSpec((None, d, tf), lambda b, i, f: (layer, 0, f)),
            pl.BlockSpec((None, d, tf), lambda b, i, f: (layer, 0, f)),
            pl.BlockSpec((None, SUBLANES, tf), lambda b, i, f: (layer, 0, f)),
            pl.BlockSpec((None, 1, tf), lambda b, i, f: (layer, 0, f)),
            pl.BlockSpec((None, tf, d), lambda b, i, f: (layer, f, 0))],
        out_specs=tok(d),
        out_shape=jax.ShapeDtypeStruct((bsz, seq, d), jnp.float32),
        scratch_shapes=[pltpu.VMEM((tm, d), jnp.float32), pltpu.VMEM((TAIL_ROWS + tm, d), jnp.bfloat16),
                        pltpu.VMEM((tm, d), jnp.float32)],
        compiler_params=pltpu.CompilerParams(dimension_semantics=("arbitrary", "arbitrary", "arbitrary"),
                                             vmem_limit_bytes=VMEM_LIMIT_BYTES),
        name="ffn",
    )(x, *outs, wo, fnorm, wg, wu, cw, cb, wd)


def _pad_rows(v, width=MXU_WIDTH):
    return jnp.pad(v.astype(jnp.float32), ((0, 0), (0, width - v.shape[1])))


def _tile_rows(g, reps):
    return jnp.tile(g.astype(jnp.float32), (1, reps))


def _pack_in_projection(w):
    idx = _np_in_index()
    pieces, start = [], 0
    while start < PK_END:
        stop = start + 1
        if idx[start] == _SRC_END:
            while stop < PK_END and idx[stop] == _SRC_END:
                stop += 1
            pieces.append(jnp.zeros(w.shape[:-1] + (stop - start,), w.dtype))
        else:
            while stop < PK_END and idx[stop] == idx[stop - 1] + 1:
                stop += 1
            pieces.append(w[..., int(idx[start]):int(idx[stop - 1]) + 1])
        start = stop
    return jnp.concatenate(pieces, axis=-1)


def _zero_col(w):
    return jnp.concatenate([w, jnp.zeros(w.shape[:-1] + (1,), w.dtype)], axis=-1)


def _rope_table(positions):
    pos = positions.astype(jnp.float32)[:, :, None]
    inv = [ROPE_THETA ** (-jnp.arange(0, rot, 2, dtype=jnp.float32) / rot) for rot in (MLA_ROPE, ROT_MOBA, ROT_DIFF)]
    inv = jnp.concatenate(inv + [jnp.zeros((N_FREQ - TR_ONE,), jnp.float32)])
    ang = pos * inv
    c, s = jnp.cos(ang), jnp.sin(ang)
    c_hi = c.astype(jnp.bfloat16)
    c_lo = (c - c_hi.astype(jnp.float32)).astype(jnp.bfloat16)
    s_hi = s.astype(jnp.bfloat16)
    s_lo = (s - s_hi.astype(jnp.float32)).astype(jnp.bfloat16)
    return jnp.concatenate([c_hi, c_lo, s_hi, s_lo], axis=-1)


def _pick_tile(n, pref):
    t = pref
    while n % t:
        t //= 2
    return t


def kernel(x, mem, positions, attn_norm, ffn_norm, mem_norm, w_in, mla_cq_norm, mla_ckv_norm, mla_w_uq, mla_w_ukv, mla_q_norm, mla_k_norm, fox_b_f, fox_q_norm, fox_k_norm, moba_q_norm, moba_k_norm, diff_lambda, diff_q_norm, diff_k_norm, diff_sub_norm, mem_w_kv, mem_q_norm, mem_k_norm, w_o, ffn_w_gate, ffn_w_up, ffn_conv_w, ffn_conv_b, ffn_w_down):
    bsz, seq, d = x.shape
    depth = w_in.shape[0]
    dff = ffn_w_gate.shape[2]
    bf = jnp.bfloat16
    f32 = jnp.float32

    uq_idx = _np_uq_index()
    ukvk_idx, ukvv_idx = _np_ukv_index()
    gpair = jnp.asarray(_np_group_matrix(_PAIR_GROUPS), bf)
    g64 = jnp.asarray(_np_group_matrix(_G64_GROUPS), bf)
    g32 = jnp.asarray(_np_group_matrix(_G32_GROUPS), bf)
    expand = jnp.asarray(_np_rope_expand_all(), bf)
    t_prep = max(_pick_tile(seq, PREP_TILE), MOBA_BLOCK)
    tril = jnp.asarray(np.tril(np.ones((t_prep, t_prep), np.float32)), bf)
    vsel = jnp.asarray(_np_value_select(), bf)
    tr = _rope_table(positions)

    t_dense = _pick_tile(seq, 1024)
    t_ffn = _pick_tile(seq, 512)
    tf = dff // 2 if (dff // 2) % LANES == 0 else dff

    win = _pack_in_projection(w_in.astype(bf))
    wuq = jnp.take(_zero_col(mla_w_uq), uq_idx, axis=2)
    wuq = jnp.pad(wuq, ((0, 0), (0, MXU_WIDTH - MLA_Q_RANK), (0, 0))).astype(bf)
    wukv = _zero_col(mla_w_ukv)
    wukvk = jnp.take(wukv, ukvk_idx, axis=2).astype(bf)
    wukvv = jnp.take(wukv, ukvv_idx, axis=2).astype(bf)
    pair = lambda g: _pad_rows(_tile_rows(g, 2))
    rows = [jnp.zeros((depth, MXU_WIDTH), f32)] * P_ROWS
    rows[P_CQ] = _pad_rows(mla_cq_norm)
    rows[P_CKV] = _pad_rows(mla_ckv_norm)
    rows[P_GQ] = pair(mla_q_norm) * (LOG2E * MLA_QK ** -0.5)
    rows[P_GK] = pair(mla_k_norm)
    rows[P_FQ] = _tile_rows(fox_q_norm, N_HEADS) * (LOG2E * HEAD_DIM ** -0.5)
    rows[P_FK] = _tile_rows(fox_k_norm, N_HEADS)
    rows[P_FB] = _pad_rows(fox_b_f)
    rows[P_MQ] = _tile_rows(moba_q_norm, N_HEADS)
    rows[P_MK] = _tile_rows(moba_k_norm, N_HEADS)
    rows[P_DQ] = _tile_rows(diff_q_norm, 2 * N_HEADS) * (LOG2E * DIFF_QK ** -0.5)
    rows[P_DK] = _tile_rows(diff_k_norm, 2 * N_HEADS)
    rows[P_EQ] = _tile_rows(mem_q_norm, N_HEADS) * (LOG2E * HEAD_DIM ** -0.5)
    par = jnp.stack(rows, axis=1)
    anorm = attn_norm.astype(f32)[:, None, :]
    mnorm = mem_norm.astype(f32)[:, None, :]
    fnorm = ffn_norm.astype(f32)[:, None, :]
    wmem = mem_w_kv.astype(bf)
    mem_gain = _tile_rows(mem_k_norm, N_HEADS)[:, None, :]

    lam_init = jnp.asarray([0.8 - 0.6 * math.exp(-0.3 * l) for l in range(depth)], f32)
    lam_vec = diff_lambda.astype(f32)
    lam = (jnp.exp(jnp.sum(lam_vec[:, 0] * lam_vec[:, 1], axis=-1))
           - jnp.exp(jnp.sum(lam_vec[:, 2] * lam_vec[:, 3], axis=-1)) + lam_init)
    lam_row = jnp.broadcast_to(lam[:, None, None], (depth, 1, GROUP_WIDTH))
    gsub = (_tile_rows(diff_sub_norm, N_HEADS) * (1.0 - lam_init)[:, None])[:, None, :]

    wo = w_o.astype(bf)
    wg, wu, wd = ffn_w_gate.astype(bf), ffn_w_up.astype(bf), ffn_w_down.astype(bf)
    cw = jnp.pad(ffn_conv_w.astype(f32), ((0, 0), (0, SUBLANES - CONV_WIDTH), (0, 0)))
    cb = ffn_conv_b.astype(f32)[:, None, :]

    for l in range(depth):
        (aq, ak, av, fq, fk, fv, fdcol, fdrow, mq, mk, mv, msel, dq, dk, dv, eq) = _prep_call(
            l, x, tr, anorm, win, wuq, wukvk, wukvv, gpair, g64, g32, expand, tril, vsel, par)
        ek, ev = _memkv_call(l, mem, mnorm, wmem, g64, mem_gain, vsel)

        o_a = _attn_call(_CFG_MLA, aq, ak, av, (), t_dense, t_dense)
        o_b = _attn_call(_CFG_FOX, fq, fk, fv, (fdcol, fdrow), t_dense, t_dense)
        t_moba = max(t_dense, MOBA_BLOCK)
        o_c = _attn_call(_CFG_MOBA, mq, mk, mv, (msel,), t_moba, t_moba)
        o_d = _attn_call(_CFG_DIFF, dq, dk, dv, (l, g64, gsub, lam_row), t_dense, t_dense)
        o_e = _attn_call(_CFG_MEM, eq, ek, ev, (), t_dense, mem.shape[1])

        x = _ffn_call(l, x, (o_a, o_b, o_c, o_d, o_e), wo, fnorm, wg, wu, cw, cb, wd, t_ffn, tf)
    return x
```

```python
import functools
import math

import numpy as np
import jax
import jax.numpy as jnp
from jax import lax
from jax.experimental import pallas as pl
from jax.experimental.pallas import tpu as pltpu

N_HEADS = 4
HEAD_DIM = 64
GROUP_WIDTH = N_HEADS * HEAD_DIM
MLA_Q_RANK = 192
MLA_KV_RANK = 128
MLA_NOPE = 64
MLA_ROPE = 32
MLA_QK = MLA_NOPE + MLA_ROPE
DIFF_QK = HEAD_DIM // 2
ROPE_THETA = 500000.0
ROT_MOBA = HEAD_DIM // 4
ROT_DIFF = DIFF_QK // 4
MOBA_BLOCK = 256
MOBA_TOPK = 3
CONV_WIDTH = 3
EPS = 1e-6
NEG_INF = -1e30
LOG2E = math.log2(math.e)
REMOVED = -3e38

LANES = 128
SUBLANES = 8
MXU_WIDTH = 256
TAIL_ROWS = 16
PREP_TILE = 512
VMEM_LIMIT_BYTES = 56 * 1024 * 1024

_SRC_CQ = 0
_SRC_CKV = _SRC_CQ + MLA_Q_RANK
_SRC_KR = _SRC_CKV + MLA_KV_RANK
_SRC_FOX = _SRC_KR + MLA_ROPE
_SRC_FOXF = _SRC_FOX + 3 * GROUP_WIDTH
_SRC_MOBA = _SRC_FOXF + N_HEADS
_SRC_DIFF = _SRC_MOBA + 3 * GROUP_WIDTH
_SRC_MEMQ = _SRC_DIFF + 3 * GROUP_WIDTH
_SRC_END = _SRC_MEMQ + GROUP_WIDTH

PK_CQ = 0
PK_CKV = 256
PK_KR = 384
PK_FQ, PK_FK, PK_FV = 896, 1152, 1408
PK_FF = 1664
PK_MQ, PK_MK, PK_MV = 1792, 2048, 2304
PK_DQ, PK_DK, PK_DV = 2560, 2816, 3072
PK_EQ = 3328
PK_END = 3584

PAIR_STRIDE = MLA_QK


def _pair_lane(h, d):
    return (h // 2) * MXU_WIDTH + (h % 2) * PAIR_STRIDE + d


N_FREQ = 32
FREQ_BASE_MLA = 0
FREQ_BASE_MOBA = MLA_ROPE // 2
FREQ_BASE_DIFF = FREQ_BASE_MOBA + ROT_MOBA // 2
TR_ONE = FREQ_BASE_DIFF + ROT_DIFF // 2
TR_WIDTH = 4 * N_FREQ
assert TR_ONE < N_FREQ and TR_WIDTH == LANES

(P_CQ, P_CKV, P_GQ, P_GK, P_FQ, P_FK, P_FB, P_MQ, P_MK, P_DQ, P_DK, P_EQ) = range(12)
P_ROWS = 16


def _np_in_index():
    idx = np.full((PK_END,), _SRC_END, np.int32)
    idx[PK_CQ:PK_CQ + MLA_Q_RANK] = np.arange(_SRC_CQ, _SRC_CQ + MLA_Q_RANK)
    idx[PK_CKV:PK_CKV + MLA_KV_RANK] = np.arange(_SRC_CKV, _SRC_CKV + MLA_KV_RANK)
    for h in range(N_HEADS):
        for d in range(MLA_ROPE):
            idx[PK_KR + _pair_lane(h, d)] = _SRC_KR + d
    idx[PK_FQ:PK_FQ + 3 * GROUP_WIDTH] = np.arange(_SRC_FOX, _SRC_FOX + 3 * GROUP_WIDTH)
    idx[PK_FF:PK_FF + N_HEADS] = np.arange(_SRC_FOXF, _SRC_FOXF + N_HEADS)
    idx[PK_MQ:PK_MQ + 3 * GROUP_WIDTH] = np.arange(_SRC_MOBA, _SRC_MOBA + 3 * GROUP_WIDTH)
    idx[PK_DQ:PK_DQ + 3 * GROUP_WIDTH] = np.arange(_SRC_DIFF, _SRC_DIFF + 3 * GROUP_WIDTH)
    idx[PK_EQ:PK_EQ + GROUP_WIDTH] = np.arange(_SRC_MEMQ, _SRC_MEMQ + GROUP_WIDTH)
    return idx


def _np_uq_index():
    idx = np.full((2 * MXU_WIDTH,), N_HEADS * MLA_QK, np.int32)
    for h in range(N_HEADS):
        for d in range(MLA_QK):
            idx[_pair_lane(h, d)] = h * MLA_QK + d
    return idx


def _np_ukv_index():
    zero = N_HEADS * (MLA_NOPE + HEAD_DIM)
    idx_k = np.full((2 * MXU_WIDTH,), zero, np.int32)
    idx_v = np.zeros((GROUP_WIDTH,), np.int32)
    for h in range(N_HEADS):
        for d in range(MLA_NOPE):
            idx_k[_pair_lane(h, MLA_ROPE + d)] = h * (MLA_NOPE + HEAD_DIM) + d
        for d in range(HEAD_DIM):
            idx_v[h * HEAD_DIM + d] = h * (MLA_NOPE + HEAD_DIM) + MLA_NOPE + d
    return idx_k, idx_v


def _np_group_matrix(groups):
    g = np.zeros((MXU_WIDTH, MXU_WIDTH), np.float32)
    for lo, size in groups:
        g[lo:lo + size, lo:lo + size] = 1.0 / size
    return g


_PAIR_GROUPS = [(0, MLA_ROPE), (MLA_ROPE, MLA_NOPE), (PAIR_STRIDE, MLA_ROPE), (PAIR_STRIDE + MLA_ROPE, MLA_NOPE)]
_G64_GROUPS = [(h * HEAD_DIM, HEAD_DIM) for h in range(N_HEADS)]
_G32_GROUPS = [(g * DIFF_QK, DIFF_QK) for g in range(2 * N_HEADS)]


def _np_rope_expand(regions, rot, base):
    half = rot // 2
    e = np.zeros((TR_WIDTH, 2 * MXU_WIDTH), np.float32)
    e[TR_ONE, 0:MXU_WIDTH] = 1.0
    for lo in regions:
        assert lo % rot == 0
        for r in range(half):
            f = base + r
            for lane, sign in ((lo + r, -1.0), (lo + half + r, 1.0)):
                e[TR_ONE, lane] = 0.0
                e[f, lane] = 1.0
                e[N_FREQ + f, lane] = 1.0
                e[2 * N_FREQ + f, MXU_WIDTH + lane] = sign
                e[3 * N_FREQ + f, MXU_WIDTH + lane] = sign
    return e


def _np_rope_expand_all():
    return np.concatenate([
        _np_rope_expand([0, PAIR_STRIDE], MLA_ROPE, FREQ_BASE_MLA),
        _np_rope_expand([h * HEAD_DIM for h in range(N_HEADS)], ROT_MOBA, FREQ_BASE_MOBA),
        _np_rope_expand([g * DIFF_QK for g in range(2 * N_HEADS)], ROT_DIFF, FREQ_BASE_DIFF),
    ], axis=1)


def _dot(a, b):
    return jnp.dot(a, b, preferred_element_type=jnp.float32)


def _dot_nt(a, b):
    return lax.dot_general(a, b, (((1,), (1,)), ((), ())), preferred_element_type=jnp.float32)


def _split2(a):
    hi = a.astype(jnp.bfloat16)
    lo = (a - hi.astype(jnp.float32)).astype(jnp.bfloat16)
    return hi, lo


def _split3(a):
    hi = a.astype(jnp.bfloat16)
    r = a - hi.astype(jnp.float32)
    mid = r.astype(jnp.bfloat16)
    lo = (r - mid.astype(jnp.float32)).astype(jnp.bfloat16)
    return hi, mid, lo


def _group_mean_sq(a, g_bf16):
    return _dot((a * a).astype(jnp.bfloat16), g_bf16)


def _rope(x, tabs, half):
    w = x.shape[-1]
    lane = lax.broadcasted_iota(jnp.int32, x.shape, 1)
    partner = jnp.where((lane & (2 * half - 1)) >= half, pltpu.roll(x, half, 1), pltpu.roll(x, w - half, 1))
    return x * tabs[:, 0:w] + partner * tabs[:, w:2 * w]


def _lane_mask(shape, lo, hi):
    lane = lax.broadcasted_iota(jnp.int32, shape, len(shape) - 1)
    return (lane >= lo) & (lane < hi)


def _prep_body(x_ref, tr_ref, anorm_ref, win_ref, wuq_ref, wukvk_ref, wukvv_ref, gpair_ref, g64_ref, g32_ref,
               exp_ref, tril_ref, vsel_ref, par_ref,
               aq_ref, ak_ref, av_ref, fq_ref, fk_ref, fv_ref, fdcol_ref, fdrow_ref, mq_ref, mk_ref, mv_ref, msel_ref,
               dq_ref, dk_ref, dv_ref, eq_ref,
               kmean_s, carry_s):
    j = pl.program_id(1)
    tm = x_ref.shape[1]
    pr = MOBA_BLOCK
    n_parts = tm // pr
    nbp = kmean_s.shape[0]
    bf = jnp.bfloat16
    gpair, g64, g32 = gpair_ref[...], g64_ref[...], g32_ref[...]
    vsel = vsel_ref[...]
    tril = tril_ref[...]

    @pl.when(j == 0)
    def _():
        kmean_s[...] = jnp.zeros_like(kmean_s)
        carry_s[...] = jnp.zeros_like(carry_s)

    def prow(r, width=MXU_WIDTH):
        return par_ref[r:r + 1, 0:width]

    def project(st):
        rows = st["rows"]
        x = x_ref[0, rows, :]
        xb = (x * lax.rsqrt(jnp.mean(x * x, axis=-1, keepdims=True) + EPS) * anorm_ref[...]).astype(bf)
        tabs = _dot(tr_ref[0, rows, :], exp_ref[...])
        st["tab_mla"] = tabs[:, 0:2 * MXU_WIDTH]
        st["tab_moba"] = tabs[:, 2 * MXU_WIDTH:4 * MXU_WIDTH]
        st["tab_diff"] = tabs[:, 4 * MXU_WIDTH:6 * MXU_WIDTH]

        def run(lo, hi):
            wide = _dot(xb, win_ref[:, lo:hi])
            return lambda off, width: wide[:, off - lo:off - lo + width]

        run_m = run(PK_MQ, PK_DQ)
        run_a = run(PK_CQ, PK_FQ)
        run_f = run(PK_FQ, PK_MQ)
        run_d = run(PK_DQ, PK_END)
        st["direct"] = {"mq": run_m(PK_MQ, MXU_WIDTH), "mk": run_m(PK_MK, MXU_WIDTH),
                        "fq": run_f(PK_FQ, MXU_WIDTH), "fk": run_f(PK_FK, MXU_WIDTH),
                        "dq": run_d(PK_DQ, MXU_WIDTH), "dk": run_d(PK_DK, MXU_WIDTH),
                        "eq": run_d(PK_EQ, MXU_WIDTH)}
        st["values"] = {"fv": run_f(PK_FV, MXU_WIDTH).astype(bf), "mv": run_m(PK_MV, MXU_WIDTH).astype(bf),
                        "dv": run_d(PK_DV, MXU_WIDTH).astype(bf)}
        p_cq, p_ckv, st["p_kr"] = run_a(PK_CQ, MXU_WIDTH), run_a(PK_CKV, MLA_KV_RANK), run_a(PK_KR, 2 * MXU_WIDTH)
        cqn = p_cq * lax.rsqrt(jnp.sum(p_cq * p_cq, axis=-1, keepdims=True) * (1.0 / MLA_Q_RANK) + EPS) * prow(P_CQ)
        ckvn = p_ckv * lax.rsqrt(jnp.mean(p_ckv * p_ckv, axis=-1, keepdims=True) + EPS) * prow(P_CKV, MLA_KV_RANK)
        st["cqb"], st["ckvb"] = cqn.astype(bf), ckvn.astype(bf)
        z = run_f(PK_FF, LANES) + prow(P_FB, LANES)
        log_f = jnp.minimum(z, 0.0) - jnp.log1p(jnp.exp(-jnp.abs(z)))
        st["log_f"] = _split3(jnp.where(_lane_mask(log_f.shape, 0, N_HEADS), log_f, 0.0))

    def second(st):
        gmat = {"fq": g64, "fk": g64, "mq": g64, "mk": g64, "dq": g32, "dk": g32, "eq": g64}
        gains = {"fq": P_FQ, "fk": P_FK, "mq": P_MQ, "mk": P_MK, "dq": P_DQ, "dk": P_DK, "eq": P_EQ}
        ms = {name: _group_mean_sq(a, gmat[name]) for name, a in st["direct"].items()}
        st["qa"] = _dot(st["cqb"], wuq_ref[...])
        st["ka"] = st["p_kr"] + _dot(st["ckvb"], wukvk_ref[...])
        st["values"]["av"] = _dot(st["ckvb"], wukvv_ref[...]).astype(bf)
        l1, l2, l3 = st["log_f"]
        st["cumsum"] = (_dot(tril, l1) + _dot(tril, l2)) + _dot(tril, l3)
        st["normed"] = {name: a * lax.rsqrt(ms[name] + EPS) * prow(gains[name]) for name, a in st["direct"].items()}

    def third(st):
        rows = st["rows"]
        mq = _rope(st["normed"]["mq"], st["tab_moba"], ROT_MOBA // 2)
        mk = _rope(st["normed"]["mk"], st["tab_moba"], ROT_MOBA // 2)
        st["mq"], st["mk"] = mq, mk
        kmean_s[pl.ds(st["blk"], 1), :] = jnp.mean(mk, axis=0, keepdims=True)
        km_hi, km_lo = _split2(kmean_s[...])
        st["gates"] = []
        for h in range(N_HEADS):
            q_hi, q_lo = _split2(jnp.where(_lane_mask(mq.shape, h * HEAD_DIM, (h + 1) * HEAD_DIM), mq, 0.0))
            st["gates"].append((_dot_nt(km_hi, q_hi) + _dot_nt(km_lo, q_hi)) + _dot_nt(km_hi, q_lo))
        st["ms_qa"] = [_group_mean_sq(st["qa"][:, p * MXU_WIDTH:(p + 1) * MXU_WIDTH], gpair) for p in range(2)]
        st["ms_ka"] = [_group_mean_sq(st["ka"][:, p * MXU_WIDTH:(p + 1) * MXU_WIDTH], gpair) for p in range(2)]
        dec = carry_s[...] + st["cumsum"]
        carry_s[...] = dec[pr - 1:pr, :]
        dec2 = dec * LOG2E
        d1, d2, d3 = _split3(dec2)
        for h in range(N_HEADS):
            fdcol_ref[0, rows, h * LANES:(h + 1) * LANES] = jnp.broadcast_to(dec2[:, h:h + 1], (pr, LANES))
        row_sel = jnp.where(lax.broadcasted_iota(jnp.int32, (SUBLANES, LANES), 0)
                            == lax.broadcasted_iota(jnp.int32, (SUBLANES, LANES), 1), 1.0, 0.0).astype(bf)
        fdrow_ref[0, :, rows] = (_dot_nt(row_sel, d1) + _dot_nt(row_sel, d2)) + _dot_nt(row_sel, d3)
        for name, ref in (("fv", fv_ref), ("mv", mv_ref), ("dv", dv_ref), ("av", av_ref)):
            ref[0, :, rows] = _values_t(st["values"][name], vsel)

    def finish(st):
        rows = st["rows"]
        normed = st["normed"]
        fq_ref[0, rows, :] = normed["fq"].astype(bf)
        fk_ref[0, rows, :] = normed["fk"].astype(bf)
        eq_ref[0, rows, :] = normed["eq"].astype(bf)
        dq_ref[0, rows, :] = _rope(normed["dq"], st["tab_diff"], ROT_DIFF // 2).astype(bf)
        dk_ref[0, rows, :] = _rope(normed["dk"], st["tab_diff"], ROT_DIFF // 2).astype(bf)
        mq_ref[0, rows, :] = (st["mq"] * (LOG2E * HEAD_DIM ** -0.5)).astype(bf)
        mk_ref[0, rows, :] = st["mk"].astype(bf)
        for p in range(2):
            sl = slice(p * MXU_WIDTH, (p + 1) * MXU_WIDTH)
            aq_ref[0, rows, sl] = _rope(st["qa"][:, sl] * lax.rsqrt(st["ms_qa"][p] + EPS) * prow(P_GQ),
                                        st["tab_mla"], MLA_ROPE // 2).astype(bf)
            ak_ref[0, rows, sl] = _rope(st["ka"][:, sl] * lax.rsqrt(st["ms_ka"][p] + EPS) * prow(P_GK),
                                        st["tab_mla"], MLA_ROPE // 2).astype(bf)
        blk = lax.broadcasted_iota(jnp.int32, (nbp, pr), 0)
        past = blk < st["blk"]
        for h in range(N_HEADS):
            work = jnp.where(past, st["gates"][h], NEG_INF)
            sel = jnp.zeros((nbp, pr), jnp.bool_)
            for _ in range(MOBA_TOPK):
                mx = jnp.max(work, axis=0, keepdims=True)
                first = jnp.min(jnp.where(work == mx, blk, nbp), axis=0, keepdims=True)
                pick = blk == first
                sel = sel | pick
                work = jnp.where(pick, REMOVED, work)
            msel_ref[0, h * nbp:(h + 1) * nbp, rows] = jnp.where(sel & past, 0.0, NEG_INF)

    parts = [{"rows": slice(p * pr, (p + 1) * pr), "blk": j * n_parts + p} for p in range(n_parts)]
    for phase in (project, second, third, finish):
        for st in parts:
            phase(st)


def _const_spec(a, layer=None):
    if layer is None:
        n = a.ndim
        return pl.BlockSpec(a.shape, lambda *_: (0,) * n)
    n = a.ndim - 1
    return pl.BlockSpec((None,) + a.shape[1:], lambda *_: (layer,) + (0,) * n)


def _moba_blocks_padded(seq):
    return -(-(seq // MOBA_BLOCK) // SUBLANES) * SUBLANES


def _prep_call(layer, tm, x, tr, anorm, win, wuq, wukvk, wukvv, gpair, g64, g32, expand, tril, vsel, par):
    bsz, seq, d = x.shape
    assert seq % tm == 0 and tm % MOBA_BLOCK == 0 and tril.shape == (MOBA_BLOCK, MOBA_BLOCK)
    nbp = _moba_blocks_padded(seq)
    bf = jnp.bfloat16
    f32 = jnp.float32

    vt = -N_HEADS * V_ROWS
    widths = [(2 * MXU_WIDTH, bf), (2 * MXU_WIDTH, bf), (vt, bf),
              (GROUP_WIDTH, bf), (GROUP_WIDTH, bf), (vt, bf),
              (N_HEADS * LANES, f32), (-SUBLANES, f32),
              (GROUP_WIDTH, bf), (GROUP_WIDTH, bf), (vt, bf), (-N_HEADS * nbp, f32),
              (GROUP_WIDTH, bf), (GROUP_WIDTH, bf), (vt, bf),
              (GROUP_WIDTH, bf)]

    def tok(width):
        if width > 0:
            return pl.BlockSpec((1, tm, width), lambda b, j: (b, j, 0))
        return pl.BlockSpec((1, -width, tm), lambda b, j: (b, 0, j))

    def shape(width):
        return (bsz, seq, width) if width > 0 else (bsz, -width, seq)

    consts = [anorm, win, wuq, wukvk, wukvv, gpair, g64, g32, expand, tril, vsel, par]
    layered = [True, True, True, True, True, False, False, False, False, False, False, True]
    return pl.pallas_call(
        _prep_body,
        grid=(bsz, seq // tm),
        in_specs=[tok(d), tok(TR_WIDTH)] + [_const_spec(c, layer if ly else None) for c, ly in zip(consts, layered)],
        out_specs=[tok(w) for w, _ in widths],
        out_shape=[jax.ShapeDtypeStruct(shape(w), dt) for w, dt in widths],
        scratch_shapes=[pltpu.VMEM((nbp, GROUP_WIDTH), jnp.float32), pltpu.VMEM((1, LANES), jnp.float32)],
        compiler_params=pltpu.CompilerParams(dimension_semantics=("arbitrary", "arbitrary"),
                                             vmem_limit_bytes=VMEM_LIMIT_BYTES),
        name="prep",
    )(x, tr, *consts)


def _memkv_body(mem_ref, mnorm_ref, w_ref, g64_ref, gain_ref, vsel_ref, k_ref, v_ref):
    m = mem_ref[0]
    mn = m * lax.rsqrt(jnp.mean(m * m, axis=-1, keepdims=True) + EPS) * mnorm_ref[...]
    kv = _dot(mn.astype(jnp.bfloat16), w_ref[...])
    k = kv[:, 0:GROUP_WIDTH]
    k = k * lax.rsqrt(_group_mean_sq(k, g64_ref[...]) + EPS) * gain_ref[...]
    k_ref[0] = k.astype(jnp.bfloat16)
    v_ref[0] = _values_t(kv[:, GROUP_WIDTH:2 * GROUP_WIDTH].astype(jnp.bfloat16), vsel_ref[...])


def _memkv_call(layer, mem, mnorm, w, g64, gain, vsel):
    bsz, mlen, d = mem.shape
    k_shape, vt_shape = (bsz, mlen, GROUP_WIDTH), (bsz, N_HEADS * V_ROWS, mlen)
    return pl.pallas_call(
        _memkv_body,
        grid=(bsz,),
        in_specs=[pl.BlockSpec((1, mlen, d), lambda b: (b, 0, 0)), _const_spec(mnorm, layer), _const_spec(w, layer),
                  _const_spec(g64), _const_spec(gain, layer), _const_spec(vsel)],
        out_specs=[pl.BlockSpec((1,) + s[1:], lambda b: (b, 0, 0)) for s in (k_shape, vt_shape)],
        out_shape=[jax.ShapeDtypeStruct(s, jnp.bfloat16) for s in (k_shape, vt_shape)],
        compiler_params=pltpu.CompilerParams(dimension_semantics=("arbitrary",), vmem_limit_bytes=VMEM_LIMIT_BYTES),
        name="mem_kv",
    )(mem, mnorm, w, g64, gain, vsel)


class _AttnCfg:
    def __init__(self, name, vheads, n_maps, causal, decay=False, select=False, diff=False):
        self.name = name
        self.vheads = vheads
        self.n_maps = n_maps
        self.causal = causal
        self.decay = decay
        self.select = select
        self.diff = diff


_PLAIN_VHEADS = [(0, h * HEAD_DIM, (h + 1) * HEAD_DIM, 0, h) for h in range(N_HEADS)]
_CFG_MLA = _AttnCfg("attn_mla", [((h // 2) * MXU_WIDTH, (h % 2) * PAIR_STRIDE, (h % 2) * PAIR_STRIDE + MLA_QK, 0, h)
                                 for h in range(N_HEADS)], 1, True)
_CFG_FOX = _AttnCfg("attn_fox", _PLAIN_VHEADS, 1, True, decay=True)
_CFG_MOBA = _AttnCfg("attn_moba", _PLAIN_VHEADS, 1, True, select=True)
_CFG_DIFF = _AttnCfg("attn_diff", [(0, h * HEAD_DIM + c * DIFF_QK, h * HEAD_DIM + (c + 1) * DIFF_QK, c, h)
                                   for c in range(2) for h in range(N_HEADS)], 2, True, diff=True)
_CFG_MEM = _AttnCfg("attn_mem", _PLAIN_VHEADS, 1, False)


ONES_ROW = HEAD_DIM
V_ROWS = HEAD_DIM + 16
QK_LOOKAHEAD = 8
Q_SUB = 2 * MXU_WIDTH


def _np_value_select():
    sel = np.zeros((N_HEADS * V_ROWS, GROUP_WIDTH), np.float32)
    for h in range(N_HEADS):
        for d in range(HEAD_DIM):
            sel[h * V_ROWS + d, h * HEAD_DIM + d] = 1.0
    return sel


def _values_t(v, vsel):
    vt = _dot_nt(vsel, v)
    row = lax.broadcasted_iota(jnp.int32, vt.shape, 0)
    ones = row == ONES_ROW
    for h in range(1, N_HEADS):
        ones = ones | (row == h * V_ROWS + ONES_ROW)
    return jnp.where(ones, 1.0, vt).astype(jnp.bfloat16)


def _tile_lanes(x, width):
    return jnp.tile(x, (1, width // LANES)) if width != LANES else x


def _attn_body(cfg, qi_ref, kj_ref, *refs):
    refs = list(refs)
    q_ref, k_ref, vt_ref = refs[:3]
    pos = 3
    if cfg.decay:
        dq_ref, dk_ref = refs[pos:pos + 2]
        pos += 2
    if cfg.select:
        sel_ref = refs[pos]
        pos += 1
    if cfg.diff:
        g64_ref, gsub_ref, lam_ref = refs[pos:pos + 3]
        pos += 3
    o_ref, qm_s, m_s, acc_s = refs[pos:pos + 4]

    t = pl.program_id(1)
    i = qi_ref[t]
    j = kj_ref[t]
    tq = q_ref.shape[1]
    tk = k_ref.shape[1]

    @pl.when(j == 0)
    def _():
        for n, (off, lo, hi, _, _) in enumerate(cfg.vheads):
            qb = q_ref[0, :, off:off + MXU_WIDTH]
            qm_s[n] = jnp.where(_lane_mask(qb.shape, lo, hi), qb, jnp.zeros_like(qb))
        m_s[...] = jnp.full(m_s.shape, NEG_INF, jnp.float32)
        acc_s[...] = jnp.zeros_like(acc_s)

    def step(diag):
        qs = min(tq, Q_SUB)
        items = [(n, u) for n in range(len(cfg.vheads)) for u in range(tq // qs)]

        def n_keys(u):
            return (u + 1) * qs if diag else tk

        def scores(item):
            n, u = item
            off, _, _, _, h = cfg.vheads[n]
            nk = n_keys(u)
            cols = slice(u * qs, (u + 1) * qs)
            s = _dot_nt(k_ref[0, 0:nk, off:off + MXU_WIDTH], qm_s[n, cols, :])
            if cfg.decay:
                s = (dq_ref[0, h:h + 1, cols] - _tile_lanes(dk_ref[0, 0:nk, h * LANES:(h + 1) * LANES], qs)) + s
            if cfg.select:
                nbp = sel_ref.shape[1] // N_HEADS
                qpos = u * qs + lax.broadcasted_iota(jnp.int32, (1, qs), 1)
                parts = []
                for kb in range(nk // MOBA_BLOCK):
                    rows = s[kb * MOBA_BLOCK:(kb + 1) * MOBA_BLOCK, :]
                    if not (diag and kb == nk // MOBA_BLOCK - 1):
                        bias = sel_ref[0, pl.ds(h * nbp + j * (tk // MOBA_BLOCK) + kb, 1), cols]
                        if diag:
                            bias = jnp.where(qpos < (kb + 1) * MOBA_BLOCK, 0.0, bias)
                        rows = rows + bias
                    parts.append(rows)
                s = parts[0] if len(parts) == 1 else jnp.concatenate(parts, axis=0)
            if diag:
                key = lax.broadcasted_iota(jnp.int32, (nk, qs), 0)
                qry = u * qs + lax.broadcasted_iota(jnp.int32, (nk, qs), 1)
                s = jnp.where(key <= qry, s, NEG_INF)
            return s, jnp.max(s, axis=0, keepdims=True)

        raw = {it: scores(items[it]) for it in range(min(QK_LOOKAHEAD, len(items)))}
        for it, (n, u) in enumerate(items):
            h = cfg.vheads[n][4]
            nk = n_keys(u)
            cols = slice(u * qs, (u + 1) * qs)
            s, s_max = raw.pop(it)
            m_prev = m_s[n, :, cols]
            m_new = jnp.maximum(m_prev, s_max)
            alpha = jnp.exp2(m_prev - m_new)
            p = jnp.exp2(s - m_new)
            m_s[n, :, cols] = m_new
            acc_s[n, :, cols] = acc_s[n, :, cols] * alpha + _dot(vt_ref[0, h * V_ROWS:(h + 1) * V_ROWS, 0:nk],
                                                                 p.astype(jnp.bfloat16))
            if it + QK_LOOKAHEAD < len(items):
                raw[it + QK_LOOKAHEAD] = scores(items[it + QK_LOOKAHEAD])

    if cfg.causal:
        pl.when(j < i)(functools.partial(step, False))
        pl.when(j == i)(functools.partial(step, True))
        last = j == i
    else:
        step(False)
        last = j == 0

    @pl.when(last)
    def _():
        outs = []
        for c in range(cfg.n_maps):
            heads = []
            for h in range(N_HEADS):
                acc = acc_s[c * N_HEADS + h]
                heads.append(acc[0:HEAD_DIM, :] / acc[ONES_ROW:ONES_ROW + 1, :])
            outs.append(jnp.concatenate(heads, axis=0).T)
        if cfg.diff:
            o = outs[0] - lam_ref[0:1, :] * outs[1]
            o = o * lax.rsqrt(_group_mean_sq(o, g64_ref[...]) + EPS) * gsub_ref[...]
        else:
            o = outs[0]
        o_ref[0] = o.astype(o_ref.dtype)


def _attn_call(cfg, q, k, v, extras, tq, tk):
    bsz, seq, wq = q.shape
    sk = k.shape[1]
    nq = seq // tq
    assert seq % tq == 0 and sk % tk == 0
    if cfg.causal:
        assert tq == tk and sk == seq
        pairs = [(i, j) for i in range(nq) for j in range(i + 1)]
    else:
        assert sk == tk
        pairs = [(i, 0) for i in range(nq)]
    qi = jnp.asarray(np.array([p[0] for p in pairs], np.int32))
    kj = jnp.asarray(np.array([p[1] for p in pairs], np.int32))
    n_vh = len(cfg.vheads)

    in_specs = [pl.BlockSpec((1, tq, wq), lambda b, t, qi, kj: (b, qi[t], 0)),
                pl.BlockSpec((1, tk, wq), lambda b, t, qi, kj: (b, kj[t], 0)),
                pl.BlockSpec((1, N_HEADS * V_ROWS, tk), lambda b, t, qi, kj: (b, 0, kj[t]))]
    args = [q, k, v]
    if cfg.decay:
        dcol, drow = extras
        in_specs += [pl.BlockSpec((1, SUBLANES, tq), lambda b, t, qi, kj: (b, 0, qi[t])),
                     pl.BlockSpec((1, tk, N_HEADS * LANES), lambda b, t, qi, kj: (b, kj[t], 0))]
        args += [drow, dcol]
    if cfg.select:
        (sel,) = extras
        in_specs += [pl.BlockSpec((1, sel.shape[1], tq), lambda b, t, qi, kj: (b, 0, qi[t]))]
        args += [sel]
    if cfg.diff:
        layer, g64, gsub, lam_row = extras
        in_specs += [_const_spec(g64), _const_spec(gsub, layer), _const_spec(lam_row, layer)]
        args += [g64, gsub, lam_row]

    grid_spec = pltpu.PrefetchScalarGridSpec(
        num_scalar_prefetch=2,
        grid=(bsz, len(pairs)),
        in_specs=in_specs,
        out_specs=pl.BlockSpec((1, tq, GROUP_WIDTH), lambda b, t, qi, kj: (b, qi[t], 0)),
        scratch_shapes=[pltpu.VMEM((n_vh, tq, MXU_WIDTH), jnp.bfloat16),
                        pltpu.VMEM((n_vh, 1, tq), jnp.float32),
                        pltpu.VMEM((n_vh, V_ROWS, tq), jnp.float32)])
    return pl.pallas_call(
        functools.partial(_attn_body, cfg),
        grid_spec=grid_spec,
        out_shape=jax.ShapeDtypeStruct((bsz, seq, GROUP_WIDTH), jnp.bfloat16),
        compiler_params=pltpu.CompilerParams(dimension_semantics=("arbitrary", "arbitrary"),
                                             vmem_limit_bytes=VMEM_LIMIT_BYTES),
        name=cfg.name,
    )(qi, kj, *args)


def _ffn_body(nf, x_ref, oa_ref, ob_ref, oc_ref, od_ref, oe_ref, wo_ref, fnorm_ref, wg_ref, wu_ref, cw_ref, cb_ref,
              wd_ref, out_ref, xnew_s, xn_s, acc_s):
    i = pl.program_id(1)
    f = pl.program_id(2)
    tm = x_ref.shape[1]

    @pl.when(f == 0)
    def _():
        @pl.when(i == 0)
        def _():
            xn_s[0:TAIL_ROWS, :] = jnp.zeros((TAIL_ROWS, xn_s.shape[1]), xn_s.dtype)

        @pl.when(i > 0)
        def _():
            xn_s[0:TAIL_ROWS, :] = xn_s[tm:tm + TAIL_ROWS, :]

        mixed = jnp.concatenate([o_ref[0] for o_ref in (oa_ref, ob_ref, oc_ref, od_ref, oe_ref)], axis=1)
        xnew = x_ref[0] + _dot(mixed, wo_ref[...])
        xnew_s[...] = xnew
        xn = xnew * lax.rsqrt(jnp.mean(xnew * xnew, axis=-1, keepdims=True) + EPS) * fnorm_ref[...]
        xn_s[TAIL_ROWS:TAIL_ROWS + tm, :] = xn.astype(xn_s.dtype)

    def mlp_chunk():
        ge = _dot(xn_s[...], wg_ref[...])
        u = _dot(xn_s[TAIL_ROWS:TAIL_ROWS + tm, :], wu_ref[...])
        g0 = ge[TAIL_ROWS:TAIL_ROWS + tm, :]
        t1 = ge[TAIL_ROWS - 1:TAIL_ROWS, :]
        t2 = ge[TAIL_ROWS - 2:TAIL_ROWS - 1, :]
        row = lax.broadcasted_iota(jnp.int32, g0.shape, 0)
        g1 = jnp.where(row == 0, t1, pltpu.roll(g0, 1, 0))
        g2 = jnp.where(row == 0, t2, jnp.where(row == 1, t1, pltpu.roll(g0, 2, 0)))
        y = cb_ref[...] + cw_ref[0:1, :] * g2
        y = y + cw_ref[1:2, :] * g1
        y = y + cw_ref[2:3, :] * g0
        hmid = (y * (1.0 / (1.0 + jnp.exp(-y)))) * u
        return _dot(hmid.astype(jnp.bfloat16), wd_ref[...])

    @pl.when(f == 0)
    def _():
        acc_s[...] = mlp_chunk()

    if nf > 2:
        @pl.when((f > 0) & (f < nf - 1))
        def _():
            acc_s[...] += mlp_chunk()

    @pl.when(f == nf - 1)
    def _():
        out_ref[0] = xnew_s[...] + (acc_s[...] + mlp_chunk())


def _ffn_call(layer, x, outs, wo, fnorm, wg, wu, cw, cb, wd, tm, tf):
    bsz, seq, d = x.shape
    dff = wg.shape[2]
    n_tiles, nf = seq // tm, dff // tf
    assert seq % tm == 0 and dff % tf == 0 and nf >= 2
    tok = lambda w: pl.BlockSpec((1, tm, w), lambda b, i, f: (b, i, 0))
    return pl.pallas_call(
        functools.partial(_ffn_body, nf),
        grid=(bsz, n_tiles, nf),
        in_specs=[tok(d)] + [tok(GROUP_WIDTH)] * 5 + [
            _const_spec(wo, layer),
            _const_spec(fnorm, layer),
            pl.BlockSpec((None, d, tf), lambda b, i, f: (layer, 0, f)),
            pl.BlockSpec((None, d, tf), lambda b, i, f: (layer, 0, f)),
            pl.BlockSpec((None, SUBLANES, tf), lambda b, i, f: (layer, 0, f)),
            pl.BlockSpec((None, 1, tf), lambda b, i, f: (layer, 0, f)),
            pl.BlockSpec((None, tf, d), lambda b, i, f: (layer, f, 0))],
        out_specs=tok(d),
        out_shape=jax.ShapeDtypeStruct((bsz, seq, d), jnp.float32),
        scratch_shapes=[pltpu.VMEM((tm, d), jnp.float32), pltpu.VMEM((TAIL_ROWS + tm, d), jnp.bfloat16),
                        pltpu.VMEM((tm, d), jnp.float32)],
        compiler_params=pltpu.CompilerParams(dimension_semantics=("arbitrary", "arbitrary", "arbitrary"),
                                             vmem_limit_bytes=VMEM_LIMIT_BYTES),
        name="ffn",
    )(x, *outs, wo, fnorm, wg, wu, cw, cb, wd)


def _pad_rows(v, width=MXU_WIDTH):
    return jnp.pad(v.astype(jnp.float32), ((0, 0), (0, width - v.shape[1])))


def _tile_rows(g, reps):
    return jnp.tile(g.astype(jnp.float32), (1, reps))


def _pack_in_projection(w):
    idx = _np_in_index()
    pieces, start = [], 0
    while start < PK_END:
        stop = start + 1
        if idx[start] == _SRC_END:
            while stop < PK_END and idx[stop] == _SRC_END:
                stop += 1
            pieces.append(jnp.zeros(w.shape[:-1] + (stop - start,), w.dtype))
        else:
            while stop < PK_END and idx[stop] == idx[stop - 1] + 1:
                stop += 1
            pieces.append(w[..., int(idx[start]):int(idx[stop - 1]) + 1])
        start = stop
    return jnp.concatenate(pieces, axis=-1)


def _zero_col(w):
    return jnp.concatenate([w, jnp.zeros(w.shape[:-1] + (1,), w.dtype)], axis=-1)


def _rope_table(positions):
    pos = positions.astype(jnp.float32)[:, :, None]
    inv = [ROPE_THETA ** (-jnp.arange(0, rot, 2, dtype=jnp.float32) / rot) for rot in (MLA_ROPE, ROT_MOBA, ROT_DIFF)]
    inv = jnp.concatenate(inv + [jnp.zeros((N_FREQ - TR_ONE,), jnp.float32)])
    ang = pos * inv
    c, s = jnp.cos(ang), jnp.sin(ang)
    c_hi = c.astype(jnp.bfloat16)
    c_lo = (c - c_hi.astype(jnp.float32)).astype(jnp.bfloat16)
    s_hi = s.astype(jnp.bfloat16)
    s_lo = (s - s_hi.astype(jnp.float32)).astype(jnp.bfloat16)
    return jnp.concatenate([c_hi, c_lo, s_hi, s_lo], axis=-1)


def _pick_tile(n, pref):
    t = pref
    while n % t:
        t //= 2
    return t


def kernel(x, mem, positions, attn_norm, ffn_norm, mem_norm, w_in, mla_cq_norm, mla_ckv_norm, mla_w_uq, mla_w_ukv, mla_q_norm, mla_k_norm, fox_b_f, fox_q_norm, fox_k_norm, moba_q_norm, moba_k_norm, diff_lambda, diff_q_norm, diff_k_norm, diff_sub_norm, mem_w_kv, mem_q_norm, mem_k_norm, w_o, ffn_w_gate, ffn_w_up, ffn_conv_w, ffn_conv_b, ffn_w_down):
    bsz, seq, d = x.shape
    depth = w_in.shape[0]
    dff = ffn_w_gate.shape[2]
    bf = jnp.bfloat16
    f32 = jnp.float32

    uq_idx = _np_uq_index()
    ukvk_idx, ukvv_idx = _np_ukv_index()
    gpair = jnp.asarray(_np_group_matrix(_PAIR_GROUPS), bf)
    g64 = jnp.asarray(_np_group_matrix(_G64_GROUPS), bf)
    g32 = jnp.asarray(_np_group_matrix(_G32_GROUPS), bf)
    expand = jnp.asarray(_np_rope_expand_all(), bf)
    t_prep = max(_pick_tile(seq, PREP_TILE), MOBA_BLOCK)
    tril = jnp.asarray(np.tril(np.ones((MOBA_BLOCK, MOBA_BLOCK), np.float32)), bf)
    vsel = jnp.asarray(_np_value_select(), bf)
    tr = _rope_table(positions)

    t_dense = _pick_tile(seq, 1024)
    t_ffn = _pick_tile(seq, 512)
    tf = dff // 2 if (dff // 2) % LANES == 0 else dff

    win = _pack_in_projection(w_in.astype(bf))
    wuq = jnp.take(_zero_col(mla_w_uq), uq_idx, axis=2)
    wuq = jnp.pad(wuq, ((0, 0), (0, MXU_WIDTH - MLA_Q_RANK), (0, 0))).astype(bf)
    wukv = _zero_col(mla_w_ukv)
    wukvk = jnp.take(wukv, ukvk_idx, axis=2).astype(bf)
    wukvv = jnp.take(wukv, ukvv_idx, axis=2).astype(bf)
    pair = lambda g: _pad_rows(_tile_rows(g, 2))
    rows = [jnp.zeros((depth, MXU_WIDTH), f32)] * P_ROWS
    rows[P_CQ] = _pad_rows(mla_cq_norm)
    rows[P_CKV] = _pad_rows(mla_ckv_norm)
    rows[P_GQ] = pair(mla_q_norm) * (LOG2E * MLA_QK ** -0.5)
    rows[P_GK] = pair(mla_k_norm)
    rows[P_FQ] = _tile_rows(fox_q_norm, N_HEADS) * (LOG2E * HEAD_DIM ** -0.5)
    rows[P_FK] = _tile_rows(fox_k_norm, N_HEADS)
    rows[P_FB] = _pad_rows(fox_b_f)
    rows[P_MQ] = _tile_rows(moba_q_norm, N_HEADS)
    rows[P_MK] = _tile_rows(moba_k_norm, N_HEADS)
    rows[P_DQ] = _tile_rows(diff_q_norm, 2 * N_HEADS) * (LOG2E * DIFF_QK ** -0.5)
    rows[P_DK] = _tile_rows(diff_k_norm, 2 * N_HEADS)
    rows[P_EQ] = _tile_rows(mem_q_norm, N_HEADS) * (LOG2E * HEAD_DIM ** -0.5)
    par = jnp.stack(rows, axis=1)
    anorm = attn_norm.astype(f32)[:, None, :]
    mnorm = mem_norm.astype(f32)[:, None, :]
    fnorm = ffn_norm.astype(f32)[:, None, :]
    wmem = mem_w_kv.astype(bf)
    mem_gain = _tile_rows(mem_k_norm, N_HEADS)[:, None, :]

    lam_init = jnp.asarray([0.8 - 0.6 * math.exp(-0.3 * l) for l in range(depth)], f32)
    lam_vec = diff_lambda.astype(f32)
    lam = (jnp.exp(jnp.sum(lam_vec[:, 0] * lam_vec[:, 1], axis=-1))
           - jnp.exp(jnp.sum(lam_vec[:, 2] * lam_vec[:, 3], axis=-1)) + lam_init)
    lam_row = jnp.broadcast_to(lam[:, None, None], (depth, 1, GROUP_WIDTH))
    gsub = (_tile_rows(diff_sub_norm, N_HEADS) * (1.0 - lam_init)[:, None])[:, None, :]

    wo = w_o.astype(bf)
    wg, wu, wd = ffn_w_gate.astype(bf), ffn_w_up.astype(bf), ffn_w_down.astype(bf)
    cw = jnp.pad(ffn_conv_w.astype(f32), ((0, 0), (0, SUBLANES - CONV_WIDTH), (0, 0)))
    cb = ffn_conv_b.astype(f32)[:, None, :]

    for l in range(depth):
        (aq, ak, av, fq, fk, fv, fdcol, fdrow, mq, mk, mv, msel, dq, dk, dv, eq) = _prep_call(
            l, t_prep, x, tr, anorm, win, wuq, wukvk, wukvv, gpair, g64, g32, expand, tril, vsel, par)
        ek, ev = _memkv_call(l, mem, mnorm, wmem, g64, mem_gain, vsel)

        o_a = _attn_call(_CFG_MLA, aq, ak, av, (), t_dense, t_dense)
        o_b = _attn_call(_CFG_FOX, fq, fk, fv, (fdcol, fdrow), t_dense, t_dense)
        t_moba = max(t_dense, MOBA_BLOCK)
        o_c = _attn_call(_CFG_MOBA, mq, mk, mv, (msel,), t_moba, t_moba)
        o_d = _attn_call(_CFG_DIFF, dq, dk, dv, (l, g64, gsub, lam_row), t_dense, t_dense)
        o_e = _attn_call(_CFG_MEM, eq, ek, ev, (), t_dense, mem.shape[1])

        x = _ffn_call(l, x, (o_a, o_b, o_c, o_d, o_e), wo, fnorm, wg, wu, cw, cb, wd, t_ffn, tf)
    return x
```

```python
import functools
import math

import numpy as np
import jax
import jax.numpy as jnp
from jax import lax
from jax.experimental import pallas as pl
from jax.experimental.pallas import tpu as pltpu

N_HEADS = 4
HEAD_DIM = 64
GROUP_WIDTH = N_HEADS * HEAD_DIM
MLA_Q_RANK = 192
MLA_KV_RANK = 128
MLA_NOPE = 64
MLA_ROPE = 32
MLA_QK = MLA_NOPE + MLA_ROPE
DIFF_QK = HEAD_DIM // 2
ROPE_THETA = 500000.0
ROT_MOBA = HEAD_DIM // 4
ROT_DIFF = DIFF_QK // 4
MOBA_BLOCK = 256
MOBA_TOPK = 3
CONV_WIDTH = 3
EPS = 1e-6
NEG_INF = -1e30
LOG2E = math.log2(math.e)
REMOVED = -3e38

LANES = 128
SUBLANES = 8
MXU_WIDTH = 256
TAIL_ROWS = 16
PREP_TILE = 512
VMEM_LIMIT_BYTES = 56 * 1024 * 1024

_SRC_CQ = 0
_SRC_CKV = _SRC_CQ + MLA_Q_RANK
_SRC_KR = _SRC_CKV + MLA_KV_RANK
_SRC_FOX = _SRC_KR + MLA_ROPE
_SRC_FOXF = _SRC_FOX + 3 * GROUP_WIDTH
_SRC_MOBA = _SRC_FOXF + N_HEADS
_SRC_DIFF = _SRC_MOBA + 3 * GROUP_WIDTH
_SRC_MEMQ = _SRC_DIFF + 3 * GROUP_WIDTH
_SRC_END = _SRC_MEMQ + GROUP_WIDTH

PK_CQ = 0
PK_CKV = 256
PK_KR = 384
PK_FQ, PK_FK, PK_FV = 896, 1152, 1408
PK_FF = 1664
PK_MQ, PK_MK, PK_MV = 1792, 2048, 2304
PK_DQ, PK_DK, PK_DV = 2560, 2816, 3072
PK_EQ = 3328
PK_END = 3584

PAIR_STRIDE = MLA_QK


def _pair_lane(h, d):
    return (h // 2) * MXU_WIDTH + (h % 2) * PAIR_STRIDE + d


N_FREQ = 32
FREQ_BASE_MLA = 0
FREQ_BASE_MOBA = MLA_ROPE // 2
FREQ_BASE_DIFF = FREQ_BASE_MOBA + ROT_MOBA // 2
TR_ONE = FREQ_BASE_DIFF + ROT_DIFF // 2
TR_WIDTH = 4 * N_FREQ
assert TR_ONE < N_FREQ and TR_WIDTH == LANES

(P_CQ, P_CKV, P_GQ, P_GK, P_FQ, P_FK, P_FB, P_MQ, P_MK, P_DQ, P_DK, P_EQ) = range(12)
P_ROWS = 16


def _np_in_index():
    idx = np.full((PK_END,), _SRC_END, np.int32)
    idx[PK_CQ:PK_CQ + MLA_Q_RANK] = np.arange(_SRC_CQ, _SRC_CQ + MLA_Q_RANK)
    idx[PK_CKV:PK_CKV + MLA_KV_RANK] = np.arange(_SRC_CKV, _SRC_CKV + MLA_KV_RANK)
    for h in range(N_HEADS):
        for d in range(MLA_ROPE):
            idx[PK_KR + _pair_lane(h, d)] = _SRC_KR + d
    idx[PK_FQ:PK_FQ + 3 * GROUP_WIDTH] = np.arange(_SRC_FOX, _SRC_FOX + 3 * GROUP_WIDTH)
    idx[PK_FF:PK_FF + N_HEADS] = np.arange(_SRC_FOXF, _SRC_FOXF + N_HEADS)
    idx[PK_MQ:PK_MQ + 3 * GROUP_WIDTH] = np.arange(_SRC_MOBA, _SRC_MOBA + 3 * GROUP_WIDTH)
    idx[PK_DQ:PK_DQ + 3 * GROUP_WIDTH] = np.arange(_SRC_DIFF, _SRC_DIFF + 3 * GROUP_WIDTH)
    idx[PK_EQ:PK_EQ + GROUP_WIDTH] = np.arange(_SRC_MEMQ, _SRC_MEMQ + GROUP_WIDTH)
    return idx


def _np_uq_index():
    idx = np.full((2 * MXU_WIDTH,), N_HEADS * MLA_QK, np.int32)
    for h in range(N_HEADS):
        for d in range(MLA_QK):
            idx[_pair_lane(h, d)] = h * MLA_QK + d
    return idx


def _np_ukv_index():
    zero = N_HEADS * (MLA_NOPE + HEAD_DIM)
    idx_k = np.full((2 * MXU_WIDTH,), zero, np.int32)
    idx_v = np.zeros((GROUP_WIDTH,), np.int32)
    for h in range(N_HEADS):
        for d in range(MLA_NOPE):
            idx_k[_pair_lane(h, MLA_ROPE + d)] = h * (MLA_NOPE + HEAD_DIM) + d
        for d in range(HEAD_DIM):
            idx_v[h * HEAD_DIM + d] = h * (MLA_NOPE + HEAD_DIM) + MLA_NOPE + d
    return idx_k, idx_v


def _np_group_matrix(groups):
    g = np.zeros((MXU_WIDTH, MXU_WIDTH), np.float32)
    for lo, size in groups:
        g[lo:lo + size, lo:lo + size] = 1.0 / size
    return g


_PAIR_GROUPS = [(0, MLA_ROPE), (MLA_ROPE, MLA_NOPE), (PAIR_STRIDE, MLA_ROPE), (PAIR_STRIDE + MLA_ROPE, MLA_NOPE)]
_G64_GROUPS = [(h * HEAD_DIM, HEAD_DIM) for h in range(N_HEADS)]
_G32_GROUPS = [(g * DIFF_QK, DIFF_QK) for g in range(2 * N_HEADS)]


def _np_rope_expand(regions, rot, base):
    half = rot // 2
    e = np.zeros((TR_WIDTH, 2 * MXU_WIDTH), np.float32)
    e[TR_ONE, 0:MXU_WIDTH] = 1.0
    for lo in regions:
        assert lo % rot == 0
        for r in range(half):
            f = base + r
            for lane, sign in ((lo + r, -1.0), (lo + half + r, 1.0)):
                e[TR_ONE, lane] = 0.0
                e[f, lane] = 1.0
                e[N_FREQ + f, lane] = 1.0
                e[2 * N_FREQ + f, MXU_WIDTH + lane] = sign
                e[3 * N_FREQ + f, MXU_WIDTH + lane] = sign
    return e


def _np_rope_expand_all():
    return np.concatenate([
        _np_rope_expand([0, PAIR_STRIDE], MLA_ROPE, FREQ_BASE_MLA),
        _np_rope_expand([h * HEAD_DIM for h in range(N_HEADS)], ROT_MOBA, FREQ_BASE_MOBA),
        _np_rope_expand([g * DIFF_QK for g in range(2 * N_HEADS)], ROT_DIFF, FREQ_BASE_DIFF),
    ], axis=1)


def _dot(a, b):
    return jnp.dot(a, b, preferred_element_type=jnp.float32)


def _dot_nt(a, b):
    return lax.dot_general(a, b, (((1,), (1,)), ((), ())), preferred_element_type=jnp.float32)


def _split2(a):
    hi = a.astype(jnp.bfloat16)
    lo = (a - hi.astype(jnp.float32)).astype(jnp.bfloat16)
    return hi, lo


def _split3(a):
    hi = a.astype(jnp.bfloat16)
    r = a - hi.astype(jnp.float32)
    mid = r.astype(jnp.bfloat16)
    lo = (r - mid.astype(jnp.float32)).astype(jnp.bfloat16)
    return hi, mid, lo


def _group_mean_sq(a, g_bf16):
    return _dot((a * a).astype(jnp.bfloat16), g_bf16)


def _rope(x, tabs, half):
    w = x.shape[-1]
    lane = lax.broadcasted_iota(jnp.int32, x.shape, 1)
    partner = jnp.where((lane & (2 * half - 1)) >= half, pltpu.roll(x, half, 1), pltpu.roll(x, w - half, 1))
    return x * tabs[:, 0:w] + partner * tabs[:, w:2 * w]


def _lane_mask(shape, lo, hi):
    lane = lax.broadcasted_iota(jnp.int32, shape, len(shape) - 1)
    return (lane >= lo) & (lane < hi)


def _prep_body(x_ref, tr_ref, anorm_ref, win_ref, wuq_ref, wukvk_ref, wukvv_ref, gpair_ref, g64_ref, g32_ref,
               exp_ref, tril_ref, vsel_ref, par_ref,
               aq_ref, ak_ref, av_ref, fq_ref, fk_ref, fv_ref, fdcol_ref, fdrow_ref, mq_ref, mk_ref, mv_ref, msel_ref,
               dq_ref, dk_ref, dv_ref, eq_ref,
               kmean_s, carry_s):
    j = pl.program_id(1)
    tm = x_ref.shape[1]
    pr = MOBA_BLOCK
    n_parts = tm // pr
    nbp = kmean_s.shape[0]
    bf = jnp.bfloat16
    gpair, g64, g32 = gpair_ref[...], g64_ref[...], g32_ref[...]
    vsel = vsel_ref[...]
    tril = tril_ref[...]

    @pl.when(j == 0)
    def _():
        kmean_s[...] = jnp.zeros_like(kmean_s)
        carry_s[...] = jnp.zeros_like(carry_s)

    def prow(r, width=MXU_WIDTH):
        return par_ref[r:r + 1, 0:width]

    def project(st):
        rows = st["rows"]
        x = x_ref[0, rows, :]
        xb = (x * lax.rsqrt(jnp.mean(x * x, axis=-1, keepdims=True) + EPS) * anorm_ref[...]).astype(bf)
        tabs = _dot(tr_ref[0, rows, :], exp_ref[...])
        st["tab_mla"] = tabs[:, 0:2 * MXU_WIDTH]
        st["tab_moba"] = tabs[:, 2 * MXU_WIDTH:4 * MXU_WIDTH]
        st["tab_diff"] = tabs[:, 4 * MXU_WIDTH:6 * MXU_WIDTH]

        def run(lo, hi):
            wide = _dot(xb, win_ref[:, lo:hi])
            return lambda off, width: wide[:, off - lo:off - lo + width]

        run_m = run(PK_MQ, PK_DQ)
        run_a = run(PK_CQ, PK_FQ)
        run_f = run(PK_FQ, PK_MQ)
        run_d = run(PK_DQ, PK_END)
        st["direct"] = {"mq": run_m(PK_MQ, MXU_WIDTH), "mk": run_m(PK_MK, MXU_WIDTH),
                        "fq": run_f(PK_FQ, MXU_WIDTH), "fk": run_f(PK_FK, MXU_WIDTH),
                        "dq": run_d(PK_DQ, MXU_WIDTH), "dk": run_d(PK_DK, MXU_WIDTH),
                        "eq": run_d(PK_EQ, MXU_WIDTH)}
        st["values"] = {"fv": run_f(PK_FV, MXU_WIDTH).astype(bf), "mv": run_m(PK_MV, MXU_WIDTH).astype(bf),
                        "dv": run_d(PK_DV, MXU_WIDTH).astype(bf)}
        p_cq, p_ckv, st["p_kr"] = run_a(PK_CQ, MXU_WIDTH), run_a(PK_CKV, MLA_KV_RANK), run_a(PK_KR, 2 * MXU_WIDTH)
        cqn = p_cq * lax.rsqrt(jnp.sum(p_cq * p_cq, axis=-1, keepdims=True) * (1.0 / MLA_Q_RANK) + EPS) * prow(P_CQ)
        ckvn = p_ckv * lax.rsqrt(jnp.mean(p_ckv * p_ckv, axis=-1, keepdims=True) + EPS) * prow(P_CKV, MLA_KV_RANK)
        st["cqb"], st["ckvb"] = cqn.astype(bf), ckvn.astype(bf)
        z = run_f(PK_FF, LANES) + prow(P_FB, LANES)
        log_f = jnp.minimum(z, 0.0) - jnp.log1p(jnp.exp(-jnp.abs(z)))
        st["log_f"] = _split3(jnp.where(_lane_mask(log_f.shape, 0, N_HEADS), log_f, 0.0))

    def second(st):
        gmat = {"fq": g64, "fk": g64, "mq": g64, "mk": g64, "dq": g32, "dk": g32, "eq": g64}
        gains = {"fq": P_FQ, "fk": P_FK, "mq": P_MQ, "mk": P_MK, "dq": P_DQ, "dk": P_DK, "eq": P_EQ}
        ms = {name: _group_mean_sq(a, gmat[name]) for name, a in st["direct"].items()}
        st["qa"] = _dot(st["cqb"], wuq_ref[...])
        st["ka"] = st["p_kr"] + _dot(st["ckvb"], wukvk_ref[...])
        st["values"]["av"] = _dot(st["ckvb"], wukvv_ref[...]).astype(bf)
        l1, l2, l3 = st["log_f"]
        st["cumsum"] = (_dot(tril, l1) + _dot(tril, l2)) + _dot(tril, l3)
        st["normed"] = {name: a * lax.rsqrt(ms[name] + EPS) * prow(gains[name]) for name, a in st["direct"].items()}

    def third(st):
        rows = st["rows"]
        mq = _rope(st["normed"]["mq"], st["tab_moba"], ROT_MOBA // 2)
        mk = _rope(st["normed"]["mk"], st["tab_moba"], ROT_MOBA // 2)
        st["mq"], st["mk"] = mq, mk
        kmean_s[pl.ds(st["blk"], 1), :] = jnp.mean(mk, axis=0, keepdims=True)
        km_hi, km_lo = _split2(kmean_s[...])
        st["gates"] = []
        for h in range(N_HEADS):
            q_hi, q_lo = _split2(jnp.where(_lane_mask(mq.shape, h * HEAD_DIM, (h + 1) * HEAD_DIM), mq, 0.0))
            st["gates"].append((_dot_nt(km_hi, q_hi) + _dot_nt(km_lo, q_hi)) + _dot_nt(km_hi, q_lo))
        st["ms_qa"] = [_group_mean_sq(st["qa"][:, p * MXU_WIDTH:(p + 1) * MXU_WIDTH], gpair) for p in range(2)]
        st["ms_ka"] = [_group_mean_sq(st["ka"][:, p * MXU_WIDTH:(p + 1) * MXU_WIDTH], gpair) for p in range(2)]
        dec = carry_s[...] + st["cumsum"]
        carry_s[...] = dec[pr - 1:pr, :]
        dec2 = dec * LOG2E
        d1, d2, d3 = _split3(dec2)
        for h in range(N_HEADS):
            fdcol_ref[0, rows, h * LANES:(h + 1) * LANES] = jnp.broadcast_to(dec2[:, h:h + 1], (pr, LANES))
        row_sel = jnp.where(lax.broadcasted_iota(jnp.int32, (SUBLANES, LANES), 0)
                            == lax.broadcasted_iota(jnp.int32, (SUBLANES, LANES), 1), 1.0, 0.0).astype(bf)
        fdrow_ref[0, :, rows] = (_dot_nt(row_sel, d1) + _dot_nt(row_sel, d2)) + _dot_nt(row_sel, d3)
        for name, ref in (("fv", fv_ref), ("mv", mv_ref), ("dv", dv_ref), ("av", av_ref)):
            ref[0, :, rows] = _values_t(st["values"][name], vsel)

    def finish(st):
        rows = st["rows"]
        normed = st["normed"]
        fq_ref[0, rows, :] = normed["fq"].astype(bf)
        fk_ref[0, rows, :] = normed["fk"].astype(bf)
        eq_ref[0, rows, :] = normed["eq"].astype(bf)
        dq_ref[0, rows, :] = _rope(normed["dq"], st["tab_diff"], ROT_DIFF // 2).astype(bf)
        dk_ref[0, rows, :] = _rope(normed["dk"], st["tab_diff"], ROT_DIFF // 2).astype(bf)
        mq_ref[0, rows, :] = (st["mq"] * (LOG2E * HEAD_DIM ** -0.5)).astype(bf)
        mk_ref[0, rows, :] = st["mk"].astype(bf)
        for p in range(2):
            sl = slice(p * MXU_WIDTH, (p + 1) * MXU_WIDTH)
            aq_ref[0, rows, sl] = _rope(st["qa"][:, sl] * lax.rsqrt(st["ms_qa"][p] + EPS) * prow(P_GQ),
                                        st["tab_mla"], MLA_ROPE // 2).astype(bf)
            ak_ref[0, rows, sl] = _rope(st["ka"][:, sl] * lax.rsqrt(st["ms_ka"][p] + EPS) * prow(P_GK),
                                        st["tab_mla"], MLA_ROPE // 2).astype(bf)
        blk = lax.broadcasted_iota(jnp.int32, (nbp, pr), 0)
        past = blk < st["blk"]
        for h in range(N_HEADS):
            work = jnp.where(past, st["gates"][h], NEG_INF)
            sel = jnp.zeros((nbp, pr), jnp.bool_)
            for _ in range(MOBA_TOPK):
                mx = jnp.max(work, axis=0, keepdims=True)
                first = jnp.min(jnp.where(work == mx, blk, nbp), axis=0, keepdims=True)
                pick = blk == first
                sel = sel | pick
                work = jnp.where(pick, REMOVED, work)
            msel_ref[0, h * nbp:(h + 1) * nbp, rows] = jnp.where(sel & past, 0.0, NEG_INF)

    parts = [{"rows": slice(p * pr, (p + 1) * pr), "blk": j * n_parts + p} for p in range(n_parts)]
    for phase in (project, second, third, finish):
        for st in parts:
            phase(st)


def _const_spec(a, layer=None):
    if layer is None:
        n = a.ndim
        return pl.BlockSpec(a.shape, lambda *_: (0,) * n)
    n = a.ndim - 1
    return pl.BlockSpec((None,) + a.shape[1:], lambda *_: (layer,) + (0,) * n)


def _moba_blocks_padded(seq):
    return -(-(seq // MOBA_BLOCK) // SUBLANES) * SUBLANES


def _prep_call(layer, tm, x, tr, anorm, win, wuq, wukvk, wukvv, gpair, g64, g32, expand, tril, vsel, par):
    bsz, seq, d = x.shape
    assert seq % tm == 0 and tm % MOBA_BLOCK == 0 and tril.shape == (MOBA_BLOCK, MOBA_BLOCK)
    nbp = _moba_blocks_padded(seq)
    bf = jnp.bfloat16
    f32 = jnp.float32

    vt = -N_HEADS * V_ROWS
    widths = [(2 * MXU_WIDTH, bf), (2 * MXU_WIDTH, bf), (vt, bf),
              (GROUP_WIDTH, bf), (GROUP_WIDTH, bf), (vt, bf),
              (N_HEADS * LANES, f32), (-SUBLANES, f32),
              (GROUP_WIDTH, bf), (GROUP_WIDTH, bf), (vt, bf), (-N_HEADS * nbp, f32),
              (GROUP_WIDTH, bf), (GROUP_WIDTH, bf), (vt, bf),
              (GROUP_WIDTH, bf)]

    def tok(width):
        if width > 0:
            return pl.BlockSpec((1, tm, width), lambda b, j: (b, j, 0))
        return pl.BlockSpec((1, -width, tm), lambda b, j: (b, 0, j))

    def shape(width):
        return (bsz, seq, width) if width > 0 else (bsz, -width, seq)

    consts = [anorm, win, wuq, wukvk, wukvv, gpair, g64, g32, expand, tril, vsel, par]
    layered = [True, True, True, True, True, False, False, False, False, False, False, True]
    return pl.pallas_call(
        _prep_body,
        grid=(bsz, seq // tm),
        in_specs=[tok(d), tok(TR_WIDTH)] + [_const_spec(c, layer if ly else None) for c, ly in zip(consts, layered)],
        out_specs=[tok(w) for w, _ in widths],
        out_shape=[jax.ShapeDtypeStruct(shape(w), dt) for w, dt in widths],
        scratch_shapes=[pltpu.VMEM((nbp, GROUP_WIDTH), jnp.float32), pltpu.VMEM((1, LANES), jnp.float32)],
        compiler_params=pltpu.CompilerParams(dimension_semantics=("arbitrary", "arbitrary"),
                                             vmem_limit_bytes=VMEM_LIMIT_BYTES),
        name="prep",
    )(x, tr, *consts)


def _memkv_body(mem_ref, mnorm_ref, w_ref, g64_ref, gain_ref, vsel_ref, k_ref, v_ref):
    m = mem_ref[0]
    mn = m * lax.rsqrt(jnp.mean(m * m, axis=-1, keepdims=True) + EPS) * mnorm_ref[...]
    kv = _dot(mn.astype(jnp.bfloat16), w_ref[...])
    k = kv[:, 0:GROUP_WIDTH]
    k = k * lax.rsqrt(_group_mean_sq(k, g64_ref[...]) + EPS) * gain_ref[...]
    k_ref[0] = k.astype(jnp.bfloat16)
    v_ref[0] = _values_t(kv[:, GROUP_WIDTH:2 * GROUP_WIDTH].astype(jnp.bfloat16), vsel_ref[...])


def _memkv_call(layer, mem, mnorm, w, g64, gain, vsel):
    bsz, mlen, d = mem.shape
    k_shape, vt_shape = (bsz, mlen, GROUP_WIDTH), (bsz, N_HEADS * V_ROWS, mlen)
    return pl.pallas_call(
        _memkv_body,
        grid=(bsz,),
        in_specs=[pl.BlockSpec((1, mlen, d), lambda b: (b, 0, 0)), _const_spec(mnorm, layer), _const_spec(w, layer),
                  _const_spec(g64), _const_spec(gain, layer), _const_spec(vsel)],
        out_specs=[pl.BlockSpec((1,) + s[1:], lambda b: (b, 0, 0)) for s in (k_shape, vt_shape)],
        out_shape=[jax.ShapeDtypeStruct(s, jnp.bfloat16) for s in (k_shape, vt_shape)],
        compiler_params=pltpu.CompilerParams(dimension_semantics=("arbitrary",), vmem_limit_bytes=VMEM_LIMIT_BYTES),
        name="mem_kv",
    )(mem, mnorm, w, g64, gain, vsel)


class _AttnCfg:
    def __init__(self, name, vheads, n_maps, causal, decay=False, select=False, diff=False):
        self.name = name
        self.vheads = vheads
        self.n_maps = n_maps
        self.causal = causal
        self.decay = decay
        self.select = select
        self.diff = diff


_PLAIN_VHEADS = [(0, h * HEAD_DIM, (h + 1) * HEAD_DIM, 0, h) for h in range(N_HEADS)]
_CFG_MLA = _AttnCfg("attn_mla", [((h // 2) * MXU_WIDTH, (h % 2) * PAIR_STRIDE, (h % 2) * PAIR_STRIDE + MLA_QK, 0, h)
                                 for h in range(N_HEADS)], 1, True)
_CFG_FOX = _AttnCfg("attn_fox", _PLAIN_VHEADS, 1, True, decay=True)
_CFG_MOBA = _AttnCfg("attn_moba", _PLAIN_VHEADS, 1, True, select=True)
_CFG_DIFF = _AttnCfg("attn_diff", [(0, h * HEAD_DIM + c * DIFF_QK, h * HEAD_DIM + (c + 1) * DIFF_QK, c, h)
                                   for c in range(2) for h in range(N_HEADS)], 2, True, diff=True)
_CFG_MEM = _AttnCfg("attn_mem", _PLAIN_VHEADS, 1, False)


ONES_ROW = HEAD_DIM
V_ROWS = HEAD_DIM + 16
QK_LOOKAHEAD = 4
Q_SUB = 2 * MXU_WIDTH


def _np_value_select():
    sel = np.zeros((N_HEADS * V_ROWS, GROUP_WIDTH), np.float32)
    for h in range(N_HEADS):
        for d in range(HEAD_DIM):
            sel[h * V_ROWS + d, h * HEAD_DIM + d] = 1.0
    return sel


def _values_t(v, vsel):
    vt = _dot_nt(vsel, v)
    row = lax.broadcasted_iota(jnp.int32, vt.shape, 0)
    ones = row == ONES_ROW
    for h in range(1, N_HEADS):
        ones = ones | (row == h * V_ROWS + ONES_ROW)
    return jnp.where(ones, 1.0, vt).astype(jnp.bfloat16)


def _tile_lanes(x, width):
    return jnp.tile(x, (1, width // LANES)) if width != LANES else x


def _attn_body(cfg, qi_ref, kj_ref, *refs):
    refs = list(refs)
    q_ref, k_ref, vt_ref = refs[:3]
    pos = 3
    if cfg.decay:
        dq_ref, dk_ref = refs[pos:pos + 2]
        pos += 2
    if cfg.select:
        sel_ref = refs[pos]
        pos += 1
    if cfg.diff:
        g64_ref, gsub_ref, lam_ref = refs[pos:pos + 3]
        pos += 3
    o_ref, qm_s, m_s, acc_s = refs[pos:pos + 4]

    t = pl.program_id(1)
    i = qi_ref[t]
    j = kj_ref[t]
    tq = q_ref.shape[1]
    tk = k_ref.shape[1]

    @pl.when(j == 0)
    def _():
        for n, (off, lo, hi, _, _) in enumerate(cfg.vheads):
            qb = q_ref[0, :, off:off + MXU_WIDTH]
            qm_s[n] = jnp.where(_lane_mask(qb.shape, lo, hi), qb, jnp.zeros_like(qb))
        m_s[...] = jnp.full(m_s.shape, NEG_INF, jnp.float32)
        acc_s[...] = jnp.zeros_like(acc_s)

    def step(diag):
        qs = min(tq, Q_SUB)
        items = [(n, u) for n in range(len(cfg.vheads)) for u in range(tq // qs)]

        def n_keys(u):
            return (u + 1) * qs if diag else tk

        def scores(item):
            n, u = item
            off, _, _, _, h = cfg.vheads[n]
            nk = n_keys(u)
            cols = slice(u * qs, (u + 1) * qs)
            s = _dot_nt(k_ref[0, 0:nk, off:off + MXU_WIDTH], qm_s[n, cols, :])
            if cfg.decay:
                s = (dq_ref[0, h:h + 1, cols] - _tile_lanes(dk_ref[0, 0:nk, h * LANES:(h + 1) * LANES], qs)) + s
            if cfg.select:
                nbp = sel_ref.shape[1] // N_HEADS
                qpos = u * qs + lax.broadcasted_iota(jnp.int32, (1, qs), 1)
                parts = []
                for kb in range(nk // MOBA_BLOCK):
                    rows = s[kb * MOBA_BLOCK:(kb + 1) * MOBA_BLOCK, :]
                    if not (diag and kb == nk // MOBA_BLOCK - 1):
                        bias = sel_ref[0, pl.ds(h * nbp + j * (tk // MOBA_BLOCK) + kb, 1), cols]
                        if diag:
                            bias = jnp.where(qpos < (kb + 1) * MOBA_BLOCK, 0.0, bias)
                        rows = rows + bias
                    parts.append(rows)
                s = parts[0] if len(parts) == 1 else jnp.concatenate(parts, axis=0)
            if diag:
                key = lax.broadcasted_iota(jnp.int32, (nk, qs), 0)
                qry = u * qs + lax.broadcasted_iota(jnp.int32, (nk, qs), 1)
                s = jnp.where(key <= qry, s, NEG_INF)
            return s, jnp.max(s, axis=0, keepdims=True)

        raw = {it: scores(items[it]) for it in range(min(QK_LOOKAHEAD, len(items)))}
        for it, (n, u) in enumerate(items):
            h = cfg.vheads[n][4]
            nk = n_keys(u)
            cols = slice(u * qs, (u + 1) * qs)
            s, s_max = raw.pop(it)
            m_prev = m_s[n, :, cols]
            m_new = jnp.maximum(m_prev, s_max)
            alpha = jnp.exp2(m_prev - m_new)
            p = jnp.exp2(s - m_new)
            m_s[n, :, cols] = m_new
            acc_s[n, :, cols] = acc_s[n, :, cols] * alpha + _dot(vt_ref[0, h * V_ROWS:(h + 1) * V_ROWS, 0:nk],
                                                                 p.astype(jnp.bfloat16))
            if it + QK_LOOKAHEAD < len(items):
                raw[it + QK_LOOKAHEAD] = scores(items[it + QK_LOOKAHEAD])

    if cfg.causal:
        pl.when(j < i)(functools.partial(step, False))
        pl.when(j == i)(functools.partial(step, True))
        last = j == i
    else:
        step(False)
        last = j == 0

    @pl.when(last)
    def _():
        outs = []
        for c in range(cfg.n_maps):
            heads = []
            for h in range(N_HEADS):
                acc = acc_s[c * N_HEADS + h]
                heads.append(acc[0:HEAD_DIM, :] / acc[ONES_ROW:ONES_ROW + 1, :])
            outs.append(jnp.concatenate(heads, axis=0).T)
        if cfg.diff:
            o = outs[0] - lam_ref[0:1, :] * outs[1]
            o = o * lax.rsqrt(_group_mean_sq(o, g64_ref[...]) + EPS) * gsub_ref[...]
        else:
            o = outs[0]
        o_ref[0] = o.astype(o_ref.dtype)


def _attn_call(cfg, q, k, v, extras, tq, tk):
    bsz, seq, wq = q.shape
    sk = k.shape[1]
    nq = seq // tq
    assert seq % tq == 0 and sk % tk == 0
    if cfg.causal:
        assert tq == tk and sk == seq
        pairs = [(i, j) for i in range(nq) for j in range(i + 1)]
    else:
        assert sk == tk
        pairs = [(i, 0) for i in range(nq)]
    qi = jnp.asarray(np.array([p[0] for p in pairs], np.int32))
    kj = jnp.asarray(np.array([p[1] for p in pairs], np.int32))
    n_vh = len(cfg.vheads)

    in_specs = [pl.BlockSpec((1, tq, wq), lambda b, t, qi, kj: (b, qi[t], 0)),
                pl.BlockSpec((1, tk, wq), lambda b, t, qi, kj: (b, kj[t], 0)),
                pl.BlockSpec((1, N_HEADS * V_ROWS, tk), lambda b, t, qi, kj: (b, 0, kj[t]))]
    args = [q, k, v]
    if cfg.decay:
        dcol, drow = extras
        in_specs += [pl.BlockSpec((1, SUBLANES, tq), lambda b, t, qi, kj: (b, 0, qi[t])),
                     pl.BlockSpec((1, tk, N_HEADS * LANES), lambda b, t, qi, kj: (b, kj[t], 0))]
        args += [drow, dcol]
    if cfg.select:
        (sel,) = extras
        in_specs += [pl.BlockSpec((1, sel.shape[1], tq), lambda b, t, qi, kj: (b, 0, qi[t]))]
        args += [sel]
    if cfg.diff:
        layer, g64, gsub, lam_row = extras
        in_specs += [_const_spec(g64), _const_spec(gsub, layer), _const_spec(lam_row, layer)]
        args += [g64, gsub, lam_row]

    grid_spec = pltpu.PrefetchScalarGridSpec(
        num_scalar_prefetch=2,
        grid=(bsz, len(pairs)),
        in_specs=in_specs,
        out_specs=pl.BlockSpec((1, tq, GROUP_WIDTH), lambda b, t, qi, kj: (b, qi[t], 0)),
        scratch_shapes=[pltpu.VMEM((n_vh, tq, MXU_WIDTH), jnp.bfloat16),
                        pltpu.VMEM((n_vh, 1, tq), jnp.float32),
                        pltpu.VMEM((n_vh, V_ROWS, tq), jnp.float32)])
    return pl.pallas_call(
        functools.partial(_attn_body, cfg),
        grid_spec=grid_spec,
        out_shape=jax.ShapeDtypeStruct((bsz, seq, GROUP_WIDTH), jnp.bfloat16),
        compiler_params=pltpu.CompilerParams(dimension_semantics=("arbitrary", "arbitrary"),
                                             vmem_limit_bytes=VMEM_LIMIT_BYTES),
        name=cfg.name,
    )(qi, kj, *args)


def _ffn_body(nf, x_ref, oa_ref, ob_ref, oc_ref, od_ref, oe_ref, wo_ref, fnorm_ref, wg_ref, wu_ref, cw_ref, cb_ref,
              wd_ref, out_ref, xnew_s, xn_s, acc_s):
    i = pl.program_id(1)
    f = pl.program_id(2)
    tm = x_ref.shape[1]

    @pl.when(f == 0)
    def _():
        @pl.when(i == 0)
        def _():
            xn_s[0:TAIL_ROWS, :] = jnp.zeros((TAIL_ROWS, xn_s.shape[1]), xn_s.dtype)

        @pl.when(i > 0)
        def _():
            xn_s[0:TAIL_ROWS, :] = xn_s[tm:tm + TAIL_ROWS, :]

        mixed = jnp.concatenate([o_ref[0] for o_ref in (oa_ref, ob_ref, oc_ref, od_ref, oe_ref)], axis=1)
        xnew = x_ref[0] + _dot(mixed, wo_ref[...])
        xnew_s[...] = xnew
        xn = xnew * lax.rsqrt(jnp.mean(xnew * xnew, axis=-1, keepdims=True) + EPS) * fnorm_ref[...]
        xn_s[TAIL_ROWS:TAIL_ROWS + tm, :] = xn.astype(xn_s.dtype)

    def mlp_chunk():
        ge = _dot(xn_s[...], wg_ref[...])
        u = _dot(xn_s[TAIL_ROWS:TAIL_ROWS + tm, :], wu_ref[...])
        g0 = ge[TAIL_ROWS:TAIL_ROWS + tm, :]
        t1 = ge[TAIL_ROWS - 1:TAIL_ROWS, :]
        t2 = ge[TAIL_ROWS - 2:TAIL_ROWS - 1, :]
        row = lax.broadcasted_iota(jnp.int32, g0.shape, 0)
        g1 = jnp.where(row == 0, t1, pltpu.roll(g0, 1, 0))
        g2 = jnp.where(row == 0, t2, jnp.where(row == 1, t1, pltpu.roll(g0, 2, 0)))
        y = cb_ref[...] + cw_ref[0:1, :] * g2
        y = y + cw_ref[1:2, :] * g1
        y = y + cw_ref[2:3, :] * g0
        hmid = (y * (1.0 / (1.0 + jnp.exp(-y)))) * u
        return _dot(hmid.astype(jnp.bfloat16), wd_ref[...])

    @pl.when(f == 0)
    def _():
        acc_s[...] = mlp_chunk()

    if nf > 2:
        @pl.when((f > 0) & (f < nf - 1))
        def _():
            acc_s[...] += mlp_chunk()

    @pl.when(f == nf - 1)
    def _():
        out_ref[0] = xnew_s[...] + (acc_s[...] + mlp_chunk())


def _ffn_call(layer, x, outs, wo, fnorm, wg, wu, cw, cb, wd, tm, tf):
    bsz, seq, d = x.shape
    dff = wg.shape[2]
    n_tiles, nf = seq // tm, dff // tf
    assert seq % tm == 0 and dff % tf == 0 and nf >= 2
    tok = lambda w: pl.BlockSpec((1, tm, w), lambda b, i, f: (b, i, 0))
    return pl.pallas_call(
        functools.partial(_ffn_body, nf),
        grid=(bsz, n_tiles, nf),
        in_specs=[tok(d)] + [tok(GROUP_WIDTH)] * 5 + [
            _const_spec(wo, layer),
            _const_spec(fnorm, layer),
            pl.BlockSpec((None, d, tf), lambda b, i, f: (layer, 0, f)),
            pl.BlockSpec((None, d, tf), lambda b, i, f: (layer, 0, f)),
            pl.BlockSpec((None, SUBLANES, tf), lambda b, i, f: (layer, 0, f)),
            pl.BlockSpec((None, 1, tf), lambda b, i, f: (layer, 0, f)),
            pl.BlockSpec((None, tf, d), lambda b, i, f: (layer, f, 0))],
        out_specs=tok(d),
        out_shape=jax.ShapeDtypeStruct((bsz, seq, d), jnp.float32),
        scratch_shapes=[pltpu.VMEM((tm, d), jnp.float32), pltpu.VMEM((TAIL_ROWS + tm, d), jnp.bfloat16),
                        pltpu.VMEM((tm, d), jnp.float32)],
        compiler_params=pltpu.CompilerParams(dimension_semantics=("arbitrary", "arbitrary", "arbitrary"),
                                             vmem_limit_bytes=VMEM_LIMIT_BYTES),
        name="ffn",
    )(x, *outs, wo, fnorm, wg, wu, cw, cb, wd)


def _pad_rows(v, width=MXU_WIDTH):
    return jnp.pad(v.astype(jnp.float32), ((0, 0), (0, width - v.shape[1])))


def _tile_rows(g, reps):
    return jnp.tile(g.astype(jnp.float32), (1, reps))


def _pack_in_projection(w):
    idx = _np_in_index()
    pieces, start = [], 0
    while start < PK_END:
        stop = start + 1
        if idx[start] == _SRC_END:
            while stop < PK_END and idx[stop] == _SRC_END:
                stop += 1
            pieces.append(jnp.zeros(w.shape[:-1] + (stop - start,), w.dtype))
        else:
            while stop < PK_END and idx[stop] == idx[stop - 1] + 1:
                stop += 1
            pieces.append(w[..., int(idx[start]):int(idx[stop - 1]) + 1])
        start = stop
    return jnp.concatenate(pieces, axis=-1)


def _zero_col(w):
    return jnp.concatenate([w, jnp.zeros(w.shape[:-1] + (1,), w.dtype)], axis=-1)


def _rope_table(positions):
    pos = positions.astype(jnp.float32)[:, :, None]
    inv = [ROPE_THETA ** (-jnp.arange(0, rot, 2, dtype=jnp.float32) / rot) for rot in (MLA_ROPE, ROT_MOBA, ROT_DIFF)]
    inv = jnp.concatenate(inv + [jnp.zeros((N_FREQ - TR_ONE,), jnp.float32)])
    ang = pos * inv
    c, s = jnp.cos(ang), jnp.sin(ang)
    c_hi = c.astype(jnp.bfloat16)
    c_lo = (c - c_hi.astype(jnp.float32)).astype(jnp.bfloat16)
    s_hi = s.astype(jnp.bfloat16)
    s_lo = (s - s_hi.astype(jnp.float32)).astype(jnp.bfloat16)
    return jnp.concatenate([c_hi, c_lo, s_hi, s_lo], axis=-1)


def _pick_tile(n, pref):
    t = pref
    while n % t:
        t //= 2
    return t


def kernel(x, mem, positions, attn_norm, ffn_norm, mem_norm, w_in, mla_cq_norm, mla_ckv_norm, mla_w_uq, mla_w_ukv, mla_q_norm, mla_k_norm, fox_b_f, fox_q_norm, fox_k_norm, moba_q_norm, moba_k_norm, diff_lambda, diff_q_norm, diff_k_norm, diff_sub_norm, mem_w_kv, mem_q_norm, mem_k_norm, w_o, ffn_w_gate, ffn_w_up, ffn_conv_w, ffn_conv_b, ffn_w_down):
    bsz, seq, d = x.shape
    depth = w_in.shape[0]
    dff = ffn_w_gate.shape[2]
    bf = jnp.bfloat16
    f32 = jnp.float32

    uq_idx = _np_uq_index()
    ukvk_idx, ukvv_idx = _np_ukv_index()
    gpair = jnp.asarray(_np_group_matrix(_PAIR_GROUPS), bf)
    g64 = jnp.asarray(_np_group_matrix(_G64_GROUPS), bf)
    g32 = jnp.asarray(_np_group_matrix(_G32_GROUPS), bf)
    expand = jnp.asarray(_np_rope_expand_all(), bf)
    t_prep = max(_pick_tile(seq, PREP_TILE), MOBA_BLOCK)
    tril = jnp.asarray(np.tril(np.ones((MOBA_BLOCK, MOBA_BLOCK), np.float32)), bf)
    vsel = jnp.asarray(_np_value_select(), bf)
    tr = _rope_table(positions)

    t_dense = _pick_tile(seq, 2048)
    t_ffn = _pick_tile(seq, 512)
    tf = dff // 2 if (dff // 2) % LANES == 0 else dff

    win = _pack_in_projection(w_in.astype(bf))
    wuq = jnp.take(_zero_col(mla_w_uq), uq_idx, axis=2)
    wuq = jnp.pad(wuq, ((0, 0), (0, MXU_WIDTH - MLA_Q_RANK), (0, 0))).astype(bf)
    wukv = _zero_col(mla_w_ukv)
    wukvk = jnp.take(wukv, ukvk_idx, axis=2).astype(bf)
    wukvv = jnp.take(wukv, ukvv_idx, axis=2).astype(bf)
    pair = lambda g: _pad_rows(_tile_rows(g, 2))
    rows = [jnp.zeros((depth, MXU_WIDTH), f32)] * P_ROWS
    rows[P_CQ] = _pad_rows(mla_cq_norm)
    rows[P_CKV] = _pad_rows(mla_ckv_norm)
    rows[P_GQ] = pair(mla_q_norm) * (LOG2E * MLA_QK ** -0.5)
    rows[P_GK] = pair(mla_k_norm)
    rows[P_FQ] = _tile_rows(fox_q_norm, N_HEADS) * (LOG2E * HEAD_DIM ** -0.5)
    rows[P_FK] = _tile_rows(fox_k_norm, N_HEADS)
    rows[P_FB] = _pad_rows(fox_b_f)
    rows[P_MQ] = _tile_rows(moba_q_norm, N_HEADS)
    rows[P_MK] = _tile_rows(moba_k_norm, N_HEADS)
    rows[P_DQ] = _tile_rows(diff_q_norm, 2 * N_HEADS) * (LOG2E * DIFF_QK ** -0.5)
    rows[P_DK] = _tile_rows(diff_k_norm, 2 * N_HEADS)
    rows[P_EQ] = _tile_rows(mem_q_norm, N_HEADS) * (LOG2E * HEAD_DIM ** -0.5)
    par = jnp.stack(rows, axis=1)
    anorm = attn_norm.astype(f32)[:, None, :]
    mnorm = mem_norm.astype(f32)[:, None, :]
    fnorm = ffn_norm.astype(f32)[:, None, :]
    wmem = mem_w_kv.astype(bf)
    mem_gain = _tile_rows(mem_k_norm, N_HEADS)[:, None, :]

    lam_init = jnp.asarray([0.8 - 0.6 * math.exp(-0.3 * l) for l in range(depth)], f32)
    lam_vec = diff_lambda.astype(f32)
    lam = (jnp.exp(jnp.sum(lam_vec[:, 0] * lam_vec[:, 1], axis=-1))
           - jnp.exp(jnp.sum(lam_vec[:, 2] * lam_vec[:, 3], axis=-1)) + lam_init)
    lam_row = jnp.broadcast_to(lam[:, None, None], (depth, 1, GROUP_WIDTH))
    gsub = (_tile_rows(diff_sub_norm, N_HEADS) * (1.0 - lam_init)[:, None])[:, None, :]

    wo = w_o.astype(bf)
    wg, wu, wd = ffn_w_gate.astype(bf), ffn_w_up.astype(bf), ffn_w_down.astype(bf)
    cw = jnp.pad(ffn_conv_w.astype(f32), ((0, 0), (0, SUBLANES - CONV_WIDTH), (0, 0)))
    cb = ffn_conv_b.astype(f32)[:, None, :]

    for l in range(depth):
        (aq, ak, av, fq, fk, fv, fdcol, fdrow, mq, mk, mv, msel, dq, dk, dv, eq) = _prep_call(
            l, t_prep, x, tr, anorm, win, wuq, wukvk, wukvv, gpair, g64, g32, expand, tril, vsel, par)
        ek, ev = _memkv_call(l, mem, mnorm, wmem, g64, mem_gain, vsel)

        o_a = _attn_call(_CFG_MLA, aq, ak, av, (), t_dense, t_dense)
        o_b = _attn_call(_CFG_FOX, fq, fk, fv, (fdcol, fdrow), t_dense, t_dense)
        t_moba = max(t_dense, MOBA_BLOCK)
        o_c = _attn_call(_CFG_MOBA, mq, mk, mv, (msel,), t_moba, t_moba)
        o_d = _attn_call(_CFG_DIFF, dq, dk, dv, (l, g64, gsub, lam_row), t_dense, t_dense)
        o_e = _attn_call(_CFG_MEM, eq, ek, ev, (), t_dense, mem.shape[1])

        x = _ffn_call(l, x, (o_a, o_b, o_c, o_d, o_e), wo, fnorm, wg, wu, cw, cb, wd, t_ffn, tf)
    return x
```

```python
import functools
import math

import numpy as np
import jax
import jax.numpy as jnp
from jax import lax
from jax.experimental import pallas as pl
from jax.experimental.pallas import tpu as pltpu

N_HEADS = 4
HEAD_DIM = 64
GROUP_WIDTH = N_HEADS * HEAD_DIM
MLA_Q_RANK = 192
MLA_KV_RANK = 128
MLA_NOPE = 64
MLA_ROPE = 32
MLA_QK = MLA_NOPE + MLA_ROPE
DIFF_QK = HEAD_DIM // 2
ROPE_THETA = 500000.0
ROT_MOBA = HEAD_DIM // 4
ROT_DIFF = DIFF_QK // 4
MOBA_BLOCK = 256
MOBA_TOPK = 3
CONV_WIDTH = 3
EPS = 1e-6
NEG_INF = -1e30
LOG2E = math.log2(math.e)
REMOVED = -3e38

LANES = 128
SUBLANES = 8
MXU_WIDTH = 256
TAIL_ROWS = 16
PREP_TILE = 512
VMEM_LIMIT_BYTES = 56 * 1024 * 1024

_SRC_CQ = 0
_SRC_CKV = _SRC_CQ + MLA_Q_RANK
_SRC_KR = _SRC_CKV + MLA_KV_RANK
_SRC_FOX = _SRC_KR + MLA_ROPE
_SRC_FOXF = _SRC_FOX + 3 * GROUP_WIDTH
_SRC_MOBA = _SRC_FOXF + N_HEADS
_SRC_DIFF = _SRC_MOBA + 3 * GROUP_WIDTH
_SRC_MEMQ = _SRC_DIFF + 3 * GROUP_WIDTH
_SRC_END = _SRC_MEMQ + GROUP_WIDTH

PK_CQ = 0
PK_CKV = 256
PK_KR = 384
PK_FQ, PK_FK, PK_FV = 896, 1152, 1408
PK_FF = 1664
PK_MQ, PK_MK, PK_MV = 1792, 2048, 2304
PK_DQ, PK_DK, PK_DV = 2560, 2816, 3072
PK_EQ = 3328
PK_END = 3584

PAIR_STRIDE = MLA_QK


def _pair_lane(h, d):
    return (h // 2) * MXU_WIDTH + (h % 2) * PAIR_STRIDE + d


N_FREQ = 32
FREQ_BASE_MLA = 0
FREQ_BASE_MOBA = MLA_ROPE // 2
FREQ_BASE_DIFF = FREQ_BASE_MOBA + ROT_MOBA // 2
TR_ONE = FREQ_BASE_DIFF + ROT_DIFF // 2
TR_WIDTH = 4 * N_FREQ
assert TR_ONE < N_FREQ and TR_WIDTH == LANES

(P_CQ, P_CKV, P_GQ, P_GK, P_FQ, P_FK, P_FB, P_MQ, P_MK, P_DQ, P_DK, P_EQ) = range(12)
P_ROWS = 16


def _np_in_index():
    idx = np.full((PK_END,), _SRC_END, np.int32)
    idx[PK_CQ:PK_CQ + MLA_Q_RANK] = np.arange(_SRC_CQ, _SRC_CQ + MLA_Q_RANK)
    idx[PK_CKV:PK_CKV + MLA_KV_RANK] = np.arange(_SRC_CKV, _SRC_CKV + MLA_KV_RANK)
    for h in range(N_HEADS):
        for d in range(MLA_ROPE):
            idx[PK_KR + _pair_lane(h, d)] = _SRC_KR + d
    idx[PK_FQ:PK_FQ + 3 * GROUP_WIDTH] = np.arange(_SRC_FOX, _SRC_FOX + 3 * GROUP_WIDTH)
    idx[PK_FF:PK_FF + N_HEADS] = np.arange(_SRC_FOXF, _SRC_FOXF + N_HEADS)
    idx[PK_MQ:PK_MQ + 3 * GROUP_WIDTH] = np.arange(_SRC_MOBA, _SRC_MOBA + 3 * GROUP_WIDTH)
    idx[PK_DQ:PK_DQ + 3 * GROUP_WIDTH] = np.arange(_SRC_DIFF, _SRC_DIFF + 3 * GROUP_WIDTH)
    idx[PK_EQ:PK_EQ + GROUP_WIDTH] = np.arange(_SRC_MEMQ, _SRC_MEMQ + GROUP_WIDTH)
    return idx


def _np_uq_index():
    idx = np.full((2 * MXU_WIDTH,), N_HEADS * MLA_QK, np.int32)
    for h in range(N_HEADS):
        for d in range(MLA_QK):
            idx[_pair_lane(h, d)] = h * MLA_QK + d
    return idx


def _np_ukv_index():
    zero = N_HEADS * (MLA_NOPE + HEAD_DIM)
    idx_k = np.full((2 * MXU_WIDTH,), zero, np.int32)
    idx_v = np.zeros((GROUP_WIDTH,), np.int32)
    for h in range(N_HEADS):
        for d in range(MLA_NOPE):
            idx_k[_pair_lane(h, MLA_ROPE + d)] = h * (MLA_NOPE + HEAD_DIM) + d
        for d in range(HEAD_DIM):
            idx_v[h * HEAD_DIM + d] = h * (MLA_NOPE + HEAD_DIM) + MLA_NOPE + d
    return idx_k, idx_v


def _np_group_matrix(groups):
    g = np.zeros((MXU_WIDTH, MXU_WIDTH), np.float32)
    for lo, size in groups:
        g[lo:lo + size, lo:lo + size] = 1.0 / size
    return g


_PAIR_GROUPS = [(0, MLA_ROPE), (MLA_ROPE, MLA_NOPE), (PAIR_STRIDE, MLA_ROPE), (PAIR_STRIDE + MLA_ROPE, MLA_NOPE)]
_G64_GROUPS = [(h * HEAD_DIM, HEAD_DIM) for h in range(N_HEADS)]
_G32_GROUPS = [(g * DIFF_QK, DIFF_QK) for g in range(2 * N_HEADS)]


def _np_rope_expand(regions, rot, base):
    half = rot // 2
    e = np.zeros((TR_WIDTH, 2 * MXU_WIDTH), np.float32)
    e[TR_ONE, 0:MXU_WIDTH] = 1.0
    for lo in regions:
        assert lo % rot == 0
        for r in range(half):
            f = base + r
            for lane, sign in ((lo + r, -1.0), (lo + half + r, 1.0)):
                e[TR_ONE, lane] = 0.0
                e[f, lane] = 1.0
                e[N_FREQ + f, lane] = 1.0
                e[2 * N_FREQ + f, MXU_WIDTH + lane] = sign
                e[3 * N_FREQ + f, MXU_WIDTH + lane] = sign
    return e


def _np_rope_expand_all():
    return np.concatenate([
        _np_rope_expand([0, PAIR_STRIDE], MLA_ROPE, FREQ_BASE_MLA),
        _np_rope_expand([h * HEAD_DIM for h in range(N_HEADS)], ROT_MOBA, FREQ_BASE_MOBA),
        _np_rope_expand([g * DIFF_QK for g in range(2 * N_HEADS)], ROT_DIFF, FREQ_BASE_DIFF),
    ], axis=1)


def _dot(a, b):
    return jnp.dot(a, b, preferred_element_type=jnp.float32)


def _dot_nt(a, b):
    return lax.dot_general(a, b, (((1,), (1,)), ((), ())), preferred_element_type=jnp.float32)


def _split2(a):
    hi = a.astype(jnp.bfloat16)
    lo = (a - hi.astype(jnp.float32)).astype(jnp.bfloat16)
    return hi, lo


def _split3(a):
    hi = a.astype(jnp.bfloat16)
    r = a - hi.astype(jnp.float32)
    mid = r.astype(jnp.bfloat16)
    lo = (r - mid.astype(jnp.float32)).astype(jnp.bfloat16)
    return hi, mid, lo


def _group_mean_sq(a, g_bf16):
    return _dot((a * a).astype(jnp.bfloat16), g_bf16)


def _rope(x, tabs, half):
    w = x.shape[-1]
    lane = lax.broadcasted_iota(jnp.int32, x.shape, 1)
    partner = jnp.where((lane & (2 * half - 1)) >= half, pltpu.roll(x, half, 1), pltpu.roll(x, w - half, 1))
    return x * tabs[:, 0:w] + partner * tabs[:, w:2 * w]


def _lane_mask(shape, lo, hi):
    lane = lax.broadcasted_iota(jnp.int32, shape, len(shape) - 1)
    return (lane >= lo) & (lane < hi)


def _prep_body(x_ref, tr_ref, anorm_ref, win_ref, wuq_ref, wukvk_ref, wukvv_ref, gpair_ref, g64_ref, g32_ref,
               exp_ref, tril_ref, vsel_ref, par_ref,
               aq_ref, ak_ref, av_ref, fq_ref, fk_ref, fv_ref, fdcol_ref, fdrow_ref, mq_ref, mk_ref, mv_ref, msel_ref,
               dq_ref, dk_ref, dv_ref, eq_ref,
               kmean_s, carry_s):
    j = pl.program_id(1)
    tm = x_ref.shape[1]
    pr = MOBA_BLOCK
    n_parts = tm // pr
    nbp = kmean_s.shape[0]
    bf = jnp.bfloat16
    gpair, g64, g32 = gpair_ref[...], g64_ref[...], g32_ref[...]
    vsel = vsel_ref[...]
    tril = tril_ref[...]

    @pl.when(j == 0)
    def _():
        kmean_s[...] = jnp.zeros_like(kmean_s)
        carry_s[...] = jnp.zeros_like(carry_s)

    def prow(r, width=MXU_WIDTH):
        return par_ref[r:r + 1, 0:width]

    def project(st):
        rows = st["rows"]
        x = x_ref[0, rows, :]
        xb = (x * lax.rsqrt(jnp.mean(x * x, axis=-1, keepdims=True) + EPS) * anorm_ref[...]).astype(bf)
        tabs = _dot(tr_ref[0, rows, :], exp_ref[...])
        st["tab_mla"] = tabs[:, 0:2 * MXU_WIDTH]
        st["tab_moba"] = tabs[:, 2 * MXU_WIDTH:4 * MXU_WIDTH]
        st["tab_diff"] = tabs[:, 4 * MXU_WIDTH:6 * MXU_WIDTH]

        def run(lo, hi):
            wide = _dot(xb, win_ref[:, lo:hi])
            return lambda off, width: wide[:, off - lo:off - lo + width]

        run_m = run(PK_MQ, PK_DQ)
        run_a = run(PK_CQ, PK_FQ)
        run_f = run(PK_FQ, PK_MQ)
        run_d = run(PK_DQ, PK_END)
        st["direct"] = {"mq": run_m(PK_MQ, MXU_WIDTH), "mk": run_m(PK_MK, MXU_WIDTH),
                        "fq": run_f(PK_FQ, MXU_WIDTH), "fk": run_f(PK_FK, MXU_WIDTH),
                        "dq": run_d(PK_DQ, MXU_WIDTH), "dk": run_d(PK_DK, MXU_WIDTH),
                        "eq": run_d(PK_EQ, MXU_WIDTH)}
        st["values"] = {"fv": run_f(PK_FV, MXU_WIDTH).astype(bf), "mv": run_m(PK_MV, MXU_WIDTH).astype(bf),
                        "dv": run_d(PK_DV, MXU_WIDTH).astype(bf)}
        p_cq, p_ckv, st["p_kr"] = run_a(PK_CQ, MXU_WIDTH), run_a(PK_CKV, MLA_KV_RANK), run_a(PK_KR, 2 * MXU_WIDTH)
        cqn = p_cq * lax.rsqrt(jnp.sum(p_cq * p_cq, axis=-1, keepdims=True) * (1.0 / MLA_Q_RANK) + EPS) * prow(P_CQ)
        ckvn = p_ckv * lax.rsqrt(jnp.mean(p_ckv * p_ckv, axis=-1, keepdims=True) + EPS) * prow(P_CKV, MLA_KV_RANK)
        st["cqb"], st["ckvb"] = cqn.astype(bf), ckvn.astype(bf)
        z = run_f(PK_FF, LANES) + prow(P_FB, LANES)
        log_f = jnp.minimum(z, 0.0) - jnp.log1p(jnp.exp(-jnp.abs(z)))
        st["log_f"] = _split3(jnp.where(_lane_mask(log_f.shape, 0, N_HEADS), log_f, 0.0))

    def second(st):
        gmat = {"fq": g64, "fk": g64, "mq": g64, "mk": g64, "dq": g32, "dk": g32, "eq": g64}
        gains = {"fq": P_FQ, "fk": P_FK, "mq": P_MQ, "mk": P_MK, "dq": P_DQ, "dk": P_DK, "eq": P_EQ}
        ms = {name: _group_mean_sq(a, gmat[name]) for name, a in st["direct"].items()}
        st["qa"] = _dot(st["cqb"], wuq_ref[...])
        st["ka"] = st["p_kr"] + _dot(st["ckvb"], wukvk_ref[...])
        st["values"]["av"] = _dot(st["ckvb"], wukvv_ref[...]).astype(bf)
        l1, l2, l3 = st["log_f"]
        st["cumsum"] = (_dot(tril, l1) + _dot(tril, l2)) + _dot(tril, l3)
        st["normed"] = {name: a * lax.rsqrt(ms[name] + EPS) * prow(gains[name]) for name, a in st["direct"].items()}

    def third(st):
        rows = st["rows"]
        mq = _rope(st["normed"]["mq"], st["tab_moba"], ROT_MOBA // 2)
        mk = _rope(st["normed"]["mk"], st["tab_moba"], ROT_MOBA // 2)
        st["mq"], st["mk"] = mq, mk
        kmean_s[pl.ds(st["blk"], 1), :] = jnp.mean(mk, axis=0, keepdims=True)
        km_hi, km_lo = _split2(kmean_s[...])
        st["gates"] = []
        for h in range(N_HEADS):
            q_hi, q_lo = _split2(jnp.where(_lane_mask(mq.shape, h * HEAD_DIM, (h + 1) * HEAD_DIM), mq, 0.0))
            st["gates"].append((_dot_nt(km_hi, q_hi) + _dot_nt(km_lo, q_hi)) + _dot_nt(km_hi, q_lo))
        st["ms_qa"] = [_group_mean_sq(st["qa"][:, p * MXU_WIDTH:(p + 1) * MXU_WIDTH], gpair) for p in range(2)]
        st["ms_ka"] = [_group_mean_sq(st["ka"][:, p * MXU_WIDTH:(p + 1) * MXU_WIDTH], gpair) for p in range(2)]
        dec = carry_s[...] + st["cumsum"]
        carry_s[...] = dec[pr - 1:pr, :]
        dec2 = dec * LOG2E
        d1, d2, d3 = _split3(dec2)
        for h in range(N_HEADS):
            fdcol_ref[0, rows, h * LANES:(h + 1) * LANES] = jnp.broadcast_to(dec2[:, h:h + 1], (pr, LANES))
        row_sel = jnp.where(lax.broadcasted_iota(jnp.int32, (SUBLANES, LANES), 0)
                            == lax.broadcasted_iota(jnp.int32, (SUBLANES, LANES), 1), 1.0, 0.0).astype(bf)
        fdrow_ref[0, :, rows] = (_dot_nt(row_sel, d1) + _dot_nt(row_sel, d2)) + _dot_nt(row_sel, d3)
        for name, ref in (("fv", fv_ref), ("mv", mv_ref), ("dv", dv_ref), ("av", av_ref)):
            ref[0, :, rows] = _values_t(st["values"][name], vsel)

    def finish(st):
        rows = st["rows"]
        normed = st["normed"]
        fq_ref[0, rows, :] = normed["fq"].astype(bf)
        fk_ref[0, rows, :] = normed["fk"].astype(bf)
        eq_ref[0, rows, :] = normed["eq"].astype(bf)
        dq_ref[0, rows, :] = _rope(normed["dq"], st["tab_diff"], ROT_DIFF // 2).astype(bf)
        dk_ref[0, rows, :] = _rope(normed["dk"], st["tab_diff"], ROT_DIFF // 2).astype(bf)
        mq_ref[0, rows, :] = (st["mq"] * (LOG2E * HEAD_DIM ** -0.5)).astype(bf)
        mk_ref[0, rows, :] = st["mk"].astype(bf)
        for p in range(2):
            sl = slice(p * MXU_WIDTH, (p + 1) * MXU_WIDTH)
            aq_ref[0, rows, sl] = _rope(st["qa"][:, sl] * lax.rsqrt(st["ms_qa"][p] + EPS) * prow(P_GQ),
                                        st["tab_mla"], MLA_ROPE // 2).astype(bf)
            ak_ref[0, rows, sl] = _rope(st["ka"][:, sl] * lax.rsqrt(st["ms_ka"][p] + EPS) * prow(P_GK),
                                        st["tab_mla"], MLA_ROPE // 2).astype(bf)
        blk = lax.broadcasted_iota(jnp.int32, (nbp, pr), 0)
        past = blk < st["blk"]
        for h in range(N_HEADS):
            work = jnp.where(past, st["gates"][h], NEG_INF)
            sel = jnp.zeros((nbp, pr), jnp.bool_)
            for _ in range(MOBA_TOPK):
                mx = jnp.max(work, axis=0, keepdims=True)
                first = jnp.min(jnp.where(work == mx, blk, nbp), axis=0, keepdims=True)
                pick = blk == first
                sel = sel | pick
                work = jnp.where(pick, REMOVED, work)
            msel_ref[0, h * nbp:(h + 1) * nbp, rows] = jnp.where(sel & past, 0.0, NEG_INF)

    parts = [{"rows": slice(p * pr, (p + 1) * pr), "blk": j * n_parts + p} for p in range(n_parts)]
    for phase in (project, second, third, finish):
        for st in parts:
            phase(st)


def _const_spec(a, layer=None):
    if layer is None:
        n = a.ndim
        return pl.BlockSpec(a.shape, lambda *_: (0,) * n)
    n = a.ndim - 1
    return pl.BlockSpec((None,) + a.shape[1:], lambda *_: (layer,) + (0,) * n)


def _moba_blocks_padded(seq):
    return -(-(seq // MOBA_BLOCK) // SUBLANES) * SUBLANES


def _prep_call(layer, tm, x, tr, anorm, win, wuq, wukvk, wukvv, gpair, g64, g32, expand, tril, vsel, par):
    bsz, seq, d = x.shape
    assert seq % tm == 0 and tm % MOBA_BLOCK == 0 and tril.shape == (MOBA_BLOCK, MOBA_BLOCK)
    nbp = _moba_blocks_padded(seq)
    bf = jnp.bfloat16
    f32 = jnp.float32

    vt = -N_HEADS * V_ROWS
    widths = [(2 * MXU_WIDTH, bf), (2 * MXU_WIDTH, bf), (vt, bf),
              (GROUP_WIDTH, bf), (GROUP_WIDTH, bf), (vt, bf),
              (N_HEADS * LANES, f32), (-SUBLANES, f32),
              (GROUP_WIDTH, bf), (GROUP_WIDTH, bf), (vt, bf), (-N_HEADS * nbp, f32),
              (GROUP_WIDTH, bf), (GROUP_WIDTH, bf), (vt, bf),
              (GROUP_WIDTH, bf)]

    def tok(width):
        if width > 0:
            return pl.BlockSpec((1, tm, width), lambda b, j: (b, j, 0))
        return pl.BlockSpec((1, -width, tm), lambda b, j: (b, 0, j))

    def shape(width):
        return (bsz, seq, width) if width > 0 else (bsz, -width, seq)

    consts = [anorm, win, wuq, wukvk, wukvv, gpair, g64, g32, expand, tril, vsel, par]
    layered = [True, True, True, True, True, False, False, False, False, False, False, True]
    return pl.pallas_call(
        _prep_body,
        grid=(bsz, seq // tm),
        in_specs=[tok(d), tok(TR_WIDTH)] + [_const_spec(c, layer if ly else None) for c, ly in zip(consts, layered)],
        out_specs=[tok(w) for w, _ in widths],
        out_shape=[jax.ShapeDtypeStruct(shape(w), dt) for w, dt in widths],
        scratch_shapes=[pltpu.VMEM((nbp, GROUP_WIDTH), jnp.float32), pltpu.VMEM((1, LANES), jnp.float32)],
        compiler_params=pltpu.CompilerParams(dimension_semantics=("arbitrary", "arbitrary"),
                                             vmem_limit_bytes=VMEM_LIMIT_BYTES),
        name="prep",
    )(x, tr, *consts)


def _memkv_body(mem_ref, mnorm_ref, w_ref, g64_ref, gain_ref, vsel_ref, k_ref, v_ref):
    m = mem_ref[0]
    mn = m * lax.rsqrt(jnp.mean(m * m, axis=-1, keepdims=True) + EPS) * mnorm_ref[...]
    kv = _dot(mn.astype(jnp.bfloat16), w_ref[...])
    k = kv[:, 0:GROUP_WIDTH]
    k = k * lax.rsqrt(_group_mean_sq(k, g64_ref[...]) + EPS) * gain_ref[...]
    k_ref[0] = k.astype(jnp.bfloat16)
    v_ref[0] = _values_t(kv[:, GROUP_WIDTH:2 * GROUP_WIDTH].astype(jnp.bfloat16), vsel_ref[...])


def _memkv_call(layer, mem, mnorm, w, g64, gain, vsel):
    bsz, mlen, d = mem.shape
    k_shape, vt_shape = (bsz, mlen, GROUP_WIDTH), (bsz, N_HEADS * V_ROWS, mlen)
    return pl.pallas_call(
        _memkv_body,
        grid=(bsz,),
        in_specs=[pl.BlockSpec((1, mlen, d), lambda b: (b, 0, 0)), _const_spec(mnorm, layer), _const_spec(w, layer),
                  _const_spec(g64), _const_spec(gain, layer), _const_spec(vsel)],
        out_specs=[pl.BlockSpec((1,) + s[1:], lambda b: (b, 0, 0)) for s in (k_shape, vt_shape)],
        out_shape=[jax.ShapeDtypeStruct(s, jnp.bfloat16) for s in (k_shape, vt_shape)],
        compiler_params=pltpu.CompilerParams(dimension_semantics=("arbitrary",), vmem_limit_bytes=VMEM_LIMIT_BYTES),
        name="mem_kv",
    )(mem, mnorm, w, g64, gain, vsel)


class _AttnCfg:
    def __init__(self, name, vheads, n_maps, causal, decay=False, select=False, diff=False, tile=1024, lookahead=8):
        self.name = name
        self.vheads = vheads
        self.n_maps = n_maps
        self.causal = causal
        self.decay = decay
        self.select = select
        self.diff = diff
        self.tile = tile
        self.lookahead = lookahead


_PLAIN_VHEADS = [(0, h * HEAD_DIM, (h + 1) * HEAD_DIM, 0, h) for h in range(N_HEADS)]
_CFG_MLA = _AttnCfg("attn_mla", [((h // 2) * MXU_WIDTH, (h % 2) * PAIR_STRIDE, (h % 2) * PAIR_STRIDE + MLA_QK, 0, h)
                                 for h in range(N_HEADS)], 1, True)
_CFG_FOX = _AttnCfg("attn_fox", _PLAIN_VHEADS, 1, True, decay=True)
_CFG_MOBA = _AttnCfg("attn_moba", _PLAIN_VHEADS, 1, True, select=True, tile=2048, lookahead=4)
_CFG_DIFF = _AttnCfg("attn_diff", [(0, h * HEAD_DIM + c * DIFF_QK, h * HEAD_DIM + (c + 1) * DIFF_QK, c, h)
                                   for c in range(2) for h in range(N_HEADS)], 2, True, diff=True)
_CFG_MEM = _AttnCfg("attn_mem", _PLAIN_VHEADS, 1, False)


ONES_ROW = HEAD_DIM
V_ROWS = HEAD_DIM + 16
Q_SUB = 2 * MXU_WIDTH


def _np_value_select():
    sel = np.zeros((N_HEADS * V_ROWS, GROUP_WIDTH), np.float32)
    for h in range(N_HEADS):
        for d in range(HEAD_DIM):
            sel[h * V_ROWS + d, h * HEAD_DIM + d] = 1.0
    return sel


def _values_t(v, vsel):
    vt = _dot_nt(vsel, v)
    row = lax.broadcasted_iota(jnp.int32, vt.shape, 0)
    ones = row == ONES_ROW
    for h in range(1, N_HEADS):
        ones = ones | (row == h * V_ROWS + ONES_ROW)
    return jnp.where(ones, 1.0, vt).astype(jnp.bfloat16)


def _tile_lanes(x, width):
    return jnp.tile(x, (1, width // LANES)) if width != LANES else x


def _attn_body(cfg, qi_ref, kj_ref, *refs):
    refs = list(refs)
    q_ref, k_ref, vt_ref = refs[:3]
    pos = 3
    if cfg.decay:
        dq_ref, dk_ref = refs[pos:pos + 2]
        pos += 2
    if cfg.select:
        sel_ref = refs[pos]
        pos += 1
    if cfg.diff:
        g64_ref, gsub_ref, lam_ref = refs[pos:pos + 3]
        pos += 3
    o_ref, qm_s, m_s, acc_s = refs[pos:pos + 4]

    t = pl.program_id(1)
    i = qi_ref[t]
    j = kj_ref[t]
    tq = q_ref.shape[1]
    tk = k_ref.shape[1]

    @pl.when(j == 0)
    def _():
        for n, (off, lo, hi, _, _) in enumerate(cfg.vheads):
            qb = q_ref[0, :, off:off + MXU_WIDTH]
            qm_s[n] = jnp.where(_lane_mask(qb.shape, lo, hi), qb, jnp.zeros_like(qb))
        m_s[...] = jnp.full(m_s.shape, NEG_INF, jnp.float32)
        acc_s[...] = jnp.zeros_like(acc_s)

    def step(diag):
        qs = min(tq, Q_SUB)
        items = [(n, u) for n in range(len(cfg.vheads)) for u in range(tq // qs)]

        def n_keys(u):
            return (u + 1) * qs if diag else tk

        def scores(item):
            n, u = item
            off, _, _, _, h = cfg.vheads[n]
            nk = n_keys(u)
            cols = slice(u * qs, (u + 1) * qs)
            s = _dot_nt(k_ref[0, 0:nk, off:off + MXU_WIDTH], qm_s[n, cols, :])
            if cfg.decay:
                s = (dq_ref[0, h:h + 1, cols] - _tile_lanes(dk_ref[0, 0:nk, h * LANES:(h + 1) * LANES], qs)) + s
            if cfg.select:
                nbp = sel_ref.shape[1] // N_HEADS
                qpos = u * qs + lax.broadcasted_iota(jnp.int32, (1, qs), 1)
                parts = []
                for kb in range(nk // MOBA_BLOCK):
                    rows = s[kb * MOBA_BLOCK:(kb + 1) * MOBA_BLOCK, :]
                    if not (diag and kb == nk // MOBA_BLOCK - 1):
                        bias = sel_ref[0, pl.ds(h * nbp + j * (tk // MOBA_BLOCK) + kb, 1), cols]
                        if diag:
                            bias = jnp.where(qpos < (kb + 1) * MOBA_BLOCK, 0.0, bias)
                        rows = rows + bias
                    parts.append(rows)
                s = parts[0] if len(parts) == 1 else jnp.concatenate(parts, axis=0)
            if diag:
                key = lax.broadcasted_iota(jnp.int32, (nk, qs), 0)
                qry = u * qs + lax.broadcasted_iota(jnp.int32, (nk, qs), 1)
                s = jnp.where(key <= qry, s, NEG_INF)
            return s, jnp.max(s, axis=0, keepdims=True)

        raw = {it: scores(items[it]) for it in range(min(cfg.lookahead, len(items)))}
        for it, (n, u) in enumerate(items):
            h = cfg.vheads[n][4]
            nk = n_keys(u)
            cols = slice(u * qs, (u + 1) * qs)
            s, s_max = raw.pop(it)
            m_prev = m_s[n, :, cols]
            m_new = jnp.maximum(m_prev, s_max)
            alpha = jnp.exp2(m_prev - m_new)
            p = jnp.exp2(s - m_new)
            m_s[n, :, cols] = m_new
            acc_s[n, :, cols] = acc_s[n, :, cols] * alpha + _dot(vt_ref[0, h * V_ROWS:(h + 1) * V_ROWS, 0:nk],
                                                                 p.astype(jnp.bfloat16))
            if it + cfg.lookahead < len(items):
                raw[it + cfg.lookahead] = scores(items[it + cfg.lookahead])

    if cfg.causal:
        pl.when(j < i)(functools.partial(step, False))
        pl.when(j == i)(functools.partial(step, True))
        last = j == i
    else:
        step(False)
        last = j == 0

    @pl.when(last)
    def _():
        outs = []
        for c in range(cfg.n_maps):
            heads = []
            for h in range(N_HEADS):
                acc = acc_s[c * N_HEADS + h]
                heads.append(acc[0:HEAD_DIM, :] / acc[ONES_ROW:ONES_ROW + 1, :])
            outs.append(jnp.concatenate(heads, axis=0).T)
        if cfg.diff:
            o = outs[0] - lam_ref[0:1, :] * outs[1]
            o = o * lax.rsqrt(_group_mean_sq(o, g64_ref[...]) + EPS) * gsub_ref[...]
        else:
            o = outs[0]
        o_ref[0] = o.astype(o_ref.dtype)


def _attn_call(cfg, q, k, v, extras, tq, tk):
    bsz, seq, wq = q.shape
    sk = k.shape[1]
    nq = seq // tq
    assert seq % tq == 0 and sk % tk == 0
    if cfg.causal:
        assert tq == tk and sk == seq
        pairs = [(i, j) for i in range(nq) for j in range(i + 1)]
    else:
        assert sk == tk
        pairs = [(i, 0) for i in range(nq)]
    qi = jnp.asarray(np.array([p[0] for p in pairs], np.int32))
    kj = jnp.asarray(np.array([p[1] for p in pairs], np.int32))
    n_vh = len(cfg.vheads)

    in_specs = [pl.BlockSpec((1, tq, wq), lambda b, t, qi, kj: (b, qi[t], 0)),
                pl.BlockSpec((1, tk, wq), lambda b, t, qi, kj: (b, kj[t], 0)),
                pl.BlockSpec((1, N_HEADS * V_ROWS, tk), lambda b, t, qi, kj: (b, 0, kj[t]))]
    args = [q, k, v]
    if cfg.decay:
        dcol, drow = extras
        in_specs += [pl.BlockSpec((1, SUBLANES, tq), lambda b, t, qi, kj: (b, 0, qi[t])),
                     pl.BlockSpec((1, tk, N_HEADS * LANES), lambda b, t, qi, kj: (b, kj[t], 0))]
        args += [drow, dcol]
    if cfg.select:
        (sel,) = extras
        in_specs += [pl.BlockSpec((1, sel.shape[1], tq), lambda b, t, qi, kj: (b, 0, qi[t]))]
        args += [sel]
    if cfg.diff:
        layer, g64, gsub, lam_row = extras
        in_specs += [_const_spec(g64), _const_spec(gsub, layer), _const_spec(lam_row, layer)]
        args += [g64, gsub, lam_row]

    grid_spec = pltpu.PrefetchScalarGridSpec(
        num_scalar_prefetch=2,
        grid=(bsz, len(pairs)),
        in_specs=in_specs,
        out_specs=pl.BlockSpec((1, tq, GROUP_WIDTH), lambda b, t, qi, kj: (b, qi[t], 0)),
        scratch_shapes=[pltpu.VMEM((n_vh, tq, MXU_WIDTH), jnp.bfloat16),
                        pltpu.VMEM((n_vh, 1, tq), jnp.float32),
                        pltpu.VMEM((n_vh, V_ROWS, tq), jnp.float32)])
    return pl.pallas_call(
        functools.partial(_attn_body, cfg),
        grid_spec=grid_spec,
        out_shape=jax.ShapeDtypeStruct((bsz, seq, GROUP_WIDTH), jnp.bfloat16),
        compiler_params=pltpu.CompilerParams(dimension_semantics=("arbitrary", "arbitrary"),
                                             vmem_limit_bytes=VMEM_LIMIT_BYTES),
        name=cfg.name,
    )(qi, kj, *args)


def _ffn_body(nf, x_ref, oa_ref, ob_ref, oc_ref, od_ref, oe_ref, wo_ref, fnorm_ref, wg_ref, wu_ref, cw_ref, cb_ref,
              wd_ref, out_ref, xnew_s, xn_s, acc_s):
    i = pl.program_id(1)
    f = pl.program_id(2)
    tm = x_ref.shape[1]

    @pl.when(f == 0)
    def _():
        @pl.when(i == 0)
        def _():
            xn_s[0:TAIL_ROWS, :] = jnp.zeros((TAIL_ROWS, xn_s.shape[1]), xn_s.dtype)

        @pl.when(i > 0)
        def _():
            xn_s[0:TAIL_ROWS, :] = xn_s[tm:tm + TAIL_ROWS, :]

        mixed = jnp.concatenate([o_ref[0] for o_ref in (oa_ref, ob_ref, oc_ref, od_ref, oe_ref)], axis=1)
        xnew = x_ref[0] + _dot(mixed, wo_ref[...])
        xnew_s[...] = xnew
        xn = xnew * lax.rsqrt(jnp.mean(xnew * xnew, axis=-1, keepdims=True) + EPS) * fnorm_ref[...]
        xn_s[TAIL_ROWS:TAIL_ROWS + tm, :] = xn.astype(xn_s.dtype)

    def mlp_chunk():
        ge = _dot(xn_s[...], wg_ref[...])
        u = _dot(xn_s[TAIL_ROWS:TAIL_ROWS + tm, :], wu_ref[...])
        g0 = ge[TAIL_ROWS:TAIL_ROWS + tm, :]
        t1 = ge[TAIL_ROWS - 1:TAIL_ROWS, :]
        t2 = ge[TAIL_ROWS - 2:TAIL_ROWS - 1, :]
        row = lax.broadcasted_iota(jnp.int32, g0.shape, 0)
        g1 = jnp.where(row == 0, t1, pltpu.roll(g0, 1, 0))
        g2 = jnp.where(row == 0, t2, jnp.where(row == 1, t1, pltpu.roll(g0, 2, 0)))
        y = cb_ref[...] + cw_ref[0:1, :] * g2
        y = y + cw_ref[1:2, :] * g1
        y = y + cw_ref[2:3, :] * g0
        hmid = (y * (1.0 / (1.0 + jnp.exp(-y)))) * u
        return _dot(hmid.astype(jnp.bfloat16), wd_ref[...])

    @pl.when(f == 0)
    def _():
        acc_s[...] = mlp_chunk()

    if nf > 2:
        @pl.when((f > 0) & (f < nf - 1))
        def _():
            acc_s[...] += mlp_chunk()

    @pl.when(f == nf - 1)
    def _():
        out_ref[0] = xnew_s[...] + (acc_s[...] + mlp_chunk())


def _ffn_call(layer, x, outs, wo, fnorm, wg, wu, cw, cb, wd, tm, tf):
    bsz, seq, d = x.shape
    dff = wg.shape[2]
    n_tiles, nf = seq // tm, dff // tf
    assert seq % tm == 0 and dff % tf == 0 and nf >= 2
    tok = lambda w: pl.BlockSpec((1, tm, w), lambda b, i, f: (b, i, 0))
    return pl.pallas_call(
        functools.partial(_ffn_body, nf),
        grid=(bsz, n_tiles, nf),
        in_specs=[tok(d)] + [tok(GROUP_WIDTH)] * 5 + [
            _const_spec(wo, layer),
            _const_spec(fnorm, layer),
            pl.BlockSpec((None, d, tf), lambda b, i, f: (layer, 0, f)),
            pl.BlockSpec((None, d, tf), lambda b, i, f: (layer, 0, f)),
            pl.BlockSpec((None, SUBLANES, tf), lambda b, i, f: (layer, 0, f)),
            pl.BlockSpec((None, 1, tf), lambda b, i, f: (layer, 0, f)),
            pl.BlockSpec((None, tf, d), lambda b, i, f: (layer, f, 0))],
        out_specs=tok(d),
        out_shape=jax.ShapeDtypeStruct((bsz, seq, d), jnp.float32),
        scratch_shapes=[pltpu.VMEM((tm, d), jnp.float32), pltpu.VMEM((TAIL_ROWS + tm, d), jnp.bfloat16),
                        pltpu.VMEM((tm, d), jnp.float32)],
        compiler_params=pltpu.CompilerParams(dimension_semantics=("arbitrary", "arbitrary", "arbitrary"),
                                             vmem_limit_bytes=VMEM_LIMIT_BYTES),
        name="ffn",
    )(x, *outs, wo, fnorm, wg, wu, cw, cb, wd)


def _pad_rows(v, width=MXU_WIDTH):
    return jnp.pad(v.astype(jnp.float32), ((0, 0), (0, width - v.shape[1])))


def _tile_rows(g, reps):
    return jnp.tile(g.astype(jnp.float32), (1, reps))


def _pack_in_projection(w):
    idx = _np_in_index()
    pieces, start = [], 0
    while start < PK_END:
        stop = start + 1
        if idx[start] == _SRC_END:
            while stop < PK_END and idx[stop] == _SRC_END:
                stop += 1
            pieces.append(jnp.zeros(w.shape[:-1] + (stop - start,), w.dtype))
        else:
            while stop < PK_END and idx[stop] == idx[stop - 1] + 1:
                stop += 1
            pieces.append(w[..., int(idx[start]):int(idx[stop - 1]) + 1])
        start = stop
    return jnp.concatenate(pieces, axis=-1)


def _zero_col(w):
    return jnp.concatenate([w, jnp.zeros(w.shape[:-1] + (1,), w.dtype)], axis=-1)


def _rope_table(positions):
    pos = positions.astype(jnp.float32)[:, :, None]
    inv = [ROPE_THETA ** (-jnp.arange(0, rot, 2, dtype=jnp.float32) / rot) for rot in (MLA_ROPE, ROT_MOBA, ROT_DIFF)]
    inv = jnp.concatenate(inv + [jnp.zeros((N_FREQ - TR_ONE,), jnp.float32)])
    ang = pos * inv
    c, s = jnp.cos(ang), jnp.sin(ang)
    c_hi = c.astype(jnp.bfloat16)
    c_lo = (c - c_hi.astype(jnp.float32)).astype(jnp.bfloat16)
    s_hi = s.astype(jnp.bfloat16)
    s_lo = (s - s_hi.astype(jnp.float32)).astype(jnp.bfloat16)
    return jnp.concatenate([c_hi, c_lo, s_hi, s_lo], axis=-1)


def _pick_tile(n, pref):
    t = pref
    while n % t:
        t //= 2
    return t


def kernel(x, mem, positions, attn_norm, ffn_norm, mem_norm, w_in, mla_cq_norm, mla_ckv_norm, mla_w_uq, mla_w_ukv, mla_q_norm, mla_k_norm, fox_b_f, fox_q_norm, fox_k_norm, moba_q_norm, moba_k_norm, diff_lambda, diff_q_norm, diff_k_norm, diff_sub_norm, mem_w_kv, mem_q_norm, mem_k_norm, w_o, ffn_w_gate, ffn_w_up, ffn_conv_w, ffn_conv_b, ffn_w_down):
    bsz, seq, d = x.shape
    depth = w_in.shape[0]
    dff = ffn_w_gate.shape[2]
    bf = jnp.bfloat16
    f32 = jnp.float32

    uq_idx = _np_uq_index()
    ukvk_idx, ukvv_idx = _np_ukv_index()
    gpair = jnp.asarray(_np_group_matrix(_PAIR_GROUPS), bf)
    g64 = jnp.asarray(_np_group_matrix(_G64_GROUPS), bf)
    g32 = jnp.asarray(_np_group_matrix(_G32_GROUPS), bf)
    expand = jnp.asarray(_np_rope_expand_all(), bf)
    t_prep = max(_pick_tile(seq, PREP_TILE), MOBA_BLOCK)
    tril = jnp.asarray(np.tril(np.ones((MOBA_BLOCK, MOBA_BLOCK), np.float32)), bf)
    vsel = jnp.asarray(_np_value_select(), bf)
    tr = _rope_table(positions)

    tile = lambda cfg: max(_pick_tile(seq, cfg.tile), MOBA_BLOCK) if cfg.select else _pick_tile(seq, cfg.tile)
    t_ffn = _pick_tile(seq, 512)
    tf = dff // 2 if (dff // 2) % LANES == 0 else dff

    win = _pack_in_projection(w_in.astype(bf))
    wuq = jnp.take(_zero_col(mla_w_uq), uq_idx, axis=2)
    wuq = jnp.pad(wuq, ((0, 0), (0, MXU_WIDTH - MLA_Q_RANK), (0, 0))).astype(bf)
    wukv = _zero_col(mla_w_ukv)
    wukvk = jnp.take(wukv, ukvk_idx, axis=2).astype(bf)
    wukvv = jnp.take(wukv, ukvv_idx, axis=2).astype(bf)
    pair = lambda g: _pad_rows(_tile_rows(g, 2))
    rows = [jnp.zeros((depth, MXU_WIDTH), f32)] * P_ROWS
    rows[P_CQ] = _pad_rows(mla_cq_norm)
    rows[P_CKV] = _pad_rows(mla_ckv_norm)
    rows[P_GQ] = pair(mla_q_norm) * (LOG2E * MLA_QK ** -0.5)
    rows[P_GK] = pair(mla_k_norm)
    rows[P_FQ] = _tile_rows(fox_q_norm, N_HEADS) * (LOG2E * HEAD_DIM ** -0.5)
    rows[P_FK] = _tile_rows(fox_k_norm, N_HEADS)
    rows[P_FB] = _pad_rows(fox_b_f)
    rows[P_MQ] = _tile_rows(moba_q_norm, N_HEADS)
    rows[P_MK] = _tile_rows(moba_k_norm, N_HEADS)
    rows[P_DQ] = _tile_rows(diff_q_norm, 2 * N_HEADS) * (LOG2E * DIFF_QK ** -0.5)
    rows[P_DK] = _tile_rows(diff_k_norm, 2 * N_HEADS)
    rows[P_EQ] = _tile_rows(mem_q_norm, N_HEADS) * (LOG2E * HEAD_DIM ** -0.5)
    par = jnp.stack(rows, axis=1)
    anorm = attn_norm.astype(f32)[:, None, :]
    mnorm = mem_norm.astype(f32)[:, None, :]
    fnorm = ffn_norm.astype(f32)[:, None, :]
    wmem = mem_w_kv.astype(bf)
    mem_gain = _tile_rows(mem_k_norm, N_HEADS)[:, None, :]

    lam_init = jnp.asarray([0.8 - 0.6 * math.exp(-0.3 * l) for l in range(depth)], f32)
    lam_vec = diff_lambda.astype(f32)
    lam = (jnp.exp(jnp.sum(lam_vec[:, 0] * lam_vec[:, 1], axis=-1))
           - jnp.exp(jnp.sum(lam_vec[:, 2] * lam_vec[:, 3], axis=-1)) + lam_init)
    lam_row = jnp.broadcast_to(lam[:, None, None], (depth, 1, GROUP_WIDTH))
    gsub = (_tile_rows(diff_sub_norm, N_HEADS) * (1.0 - lam_init)[:, None])[:, None, :]

    wo = w_o.astype(bf)
    wg, wu, wd = ffn_w_gate.astype(bf), ffn_w_up.astype(bf), ffn_w_down.astype(bf)
    cw = jnp.pad(ffn_conv_w.astype(f32), ((0, 0), (0, SUBLANES - CONV_WIDTH), (0, 0)))
    cb = ffn_conv_b.astype(f32)[:, None, :]

    for l in range(depth):
        (aq, ak, av, fq, fk, fv, fdcol, fdrow, mq, mk, mv, msel, dq, dk, dv, eq) = _prep_call(
            l, t_prep, x, tr, anorm, win, wuq, wukvk, wukvv, gpair, g64, g32, expand, tril, vsel, par)
        ek, ev = _memkv_call(l, mem, mnorm, wmem, g64, mem_gain, vsel)

        o_a = _attn_call(_CFG_MLA, aq, ak, av, (), tile(_CFG_MLA), tile(_CFG_MLA))
        o_b = _attn_call(_CFG_FOX, fq, fk, fv, (fdcol, fdrow), tile(_CFG_FOX), tile(_CFG_FOX))
        o_c = _attn_call(_CFG_MOBA, mq, mk, mv, (msel,), tile(_CFG_MOBA), tile(_CFG_MOBA))
        o_d = _attn_call(_CFG_DIFF, dq, dk, dv, (l, g64, gsub, lam_row), tile(_CFG_DIFF), tile(_CFG_DIFF))
        o_e = _attn_call(_CFG_MEM, eq, ek, ev, (), tile(_CFG_MEM), mem.shape[1])

        x = _ffn_call(l, x, (o_a, o_b, o_c, o_d, o_e), wo, fnorm, wg, wu, cw, cb, wd, t_ffn, tf)
    return x
```

```python
import functools
import math

import numpy as np
import jax
import jax.numpy as jnp
from jax import lax
from jax.experimental import pallas as pl
from jax.experimental.pallas import tpu as pltpu

N_HEADS = 4
HEAD_DIM = 64
GROUP_WIDTH = N_HEADS * HEAD_DIM
MLA_Q_RANK = 192
MLA_KV_RANK = 128
MLA_NOPE = 64
MLA_ROPE = 32
MLA_QK = MLA_NOPE + MLA_ROPE
DIFF_QK = HEAD_DIM // 2
ROPE_THETA = 500000.0
ROT_MOBA = HEAD_DIM // 4
ROT_DIFF = DIFF_QK // 4
MOBA_BLOCK = 256
MOBA_TOPK = 3
CONV_WIDTH = 3
EPS = 1e-6
NEG_INF = -1e30
LOG2E = math.log2(math.e)
REMOVED = -3e38

LANES = 128
SUBLANES = 8
MXU_WIDTH = 256
TAIL_ROWS = 16
PREP_TILE = 512
VMEM_LIMIT_BYTES = 56 * 1024 * 1024

_SRC_CQ = 0
_SRC_CKV = _SRC_CQ + MLA_Q_RANK
_SRC_KR = _SRC_CKV + MLA_KV_RANK
_SRC_FOX = _SRC_KR + MLA_ROPE
_SRC_FOXF = _SRC_FOX + 3 * GROUP_WIDTH
_SRC_MOBA = _SRC_FOXF + N_HEADS
_SRC_DIFF = _SRC_MOBA + 3 * GROUP_WIDTH
_SRC_MEMQ = _SRC_DIFF + 3 * GROUP_WIDTH
_SRC_END = _SRC_MEMQ + GROUP_WIDTH

PK_CQ = 0
PK_CKV = 256
PK_KR = 384
PK_FQ, PK_FK, PK_FV = 896, 1152, 1408
PK_FF = 1664
PK_MQ, PK_MK, PK_MV = 1792, 2048, 2304
PK_DQ, PK_DK, PK_DV = 2560, 2816, 3072
PK_EQ = 3328
PK_END = 3584

PAIR_STRIDE = MLA_QK


def _pair_lane(h, d):
    return (h // 2) * MXU_WIDTH + (h % 2) * PAIR_STRIDE + d


N_FREQ = 32
FREQ_BASE_MLA = 0
FREQ_BASE_MOBA = MLA_ROPE // 2
FREQ_BASE_DIFF = FREQ_BASE_MOBA + ROT_MOBA // 2
TR_ONE = FREQ_BASE_DIFF + ROT_DIFF // 2
TR_WIDTH = 4 * N_FREQ
assert TR_ONE < N_FREQ and TR_WIDTH == LANES

(P_CQ, P_CKV, P_GQ, P_GK, P_FQ, P_FK, P_FB, P_MQ, P_MK, P_DQ, P_DK, P_EQ) = range(12)
P_ROWS = 16


def _np_in_index():
    idx = np.full((PK_END,), _SRC_END, np.int32)
    idx[PK_CQ:PK_CQ + MLA_Q_RANK] = np.arange(_SRC_CQ, _SRC_CQ + MLA_Q_RANK)
    idx[PK_CKV:PK_CKV + MLA_KV_RANK] = np.arange(_SRC_CKV, _SRC_CKV + MLA_KV_RANK)
    for h in range(N_HEADS):
        for d in range(MLA_ROPE):
            idx[PK_KR + _pair_lane(h, d)] = _SRC_KR + d
    idx[PK_FQ:PK_FQ + 3 * GROUP_WIDTH] = np.arange(_SRC_FOX, _SRC_FOX + 3 * GROUP_WIDTH)
    idx[PK_FF:PK_FF + N_HEADS] = np.arange(_SRC_FOXF, _SRC_FOXF + N_HEADS)
    idx[PK_MQ:PK_MQ + 3 * GROUP_WIDTH] = np.arange(_SRC_MOBA, _SRC_MOBA + 3 * GROUP_WIDTH)
    idx[PK_DQ:PK_DQ + 3 * GROUP_WIDTH] = np.arange(_SRC_DIFF, _SRC_DIFF + 3 * GROUP_WIDTH)
    idx[PK_EQ:PK_EQ + GROUP_WIDTH] = np.arange(_SRC_MEMQ, _SRC_MEMQ + GROUP_WIDTH)
    return idx


def _np_uq_index():
    idx = np.full((2 * MXU_WIDTH,), N_HEADS * MLA_QK, np.int32)
    for h in range(N_HEADS):
        for d in range(MLA_QK):
            idx[_pair_lane(h, d)] = h * MLA_QK + d
    return idx


def _np_ukv_index():
    zero = N_HEADS * (MLA_NOPE + HEAD_DIM)
    idx_k = np.full((2 * MXU_WIDTH,), zero, np.int32)
    idx_v = np.zeros((GROUP_WIDTH,), np.int32)
    for h in range(N_HEADS):
        for d in range(MLA_NOPE):
            idx_k[_pair_lane(h, MLA_ROPE + d)] = h * (MLA_NOPE + HEAD_DIM) + d
        for d in range(HEAD_DIM):
            idx_v[h * HEAD_DIM + d] = h * (MLA_NOPE + HEAD_DIM) + MLA_NOPE + d
    return idx_k, idx_v


def _np_group_matrix(groups):
    g = np.zeros((MXU_WIDTH, MXU_WIDTH), np.float32)
    for lo, size in groups:
        g[lo:lo + size, lo:lo + size] = 1.0 / size
    return g


_PAIR_GROUPS = [(0, MLA_ROPE), (MLA_ROPE, MLA_NOPE), (PAIR_STRIDE, MLA_ROPE), (PAIR_STRIDE + MLA_ROPE, MLA_NOPE)]
_G64_GROUPS = [(h * HEAD_DIM, HEAD_DIM) for h in range(N_HEADS)]
_G32_GROUPS = [(g * DIFF_QK, DIFF_QK) for g in range(2 * N_HEADS)]


def _np_rope_expand(regions, rot, base):
    half = rot // 2
    e = np.zeros((TR_WIDTH, 2 * MXU_WIDTH), np.float32)
    e[TR_ONE, 0:MXU_WIDTH] = 1.0
    for lo in regions:
        assert lo % rot == 0
        for r in range(half):
            f = base + r
            for lane, sign in ((lo + r, -1.0), (lo + half + r, 1.0)):
                e[TR_ONE, lane] = 0.0
                e[f, lane] = 1.0
                e[N_FREQ + f, lane] = 1.0
                e[2 * N_FREQ + f, MXU_WIDTH + lane] = sign
                e[3 * N_FREQ + f, MXU_WIDTH + lane] = sign
    return e


def _np_rope_expand_all():
    return np.concatenate([
        _np_rope_expand([0, PAIR_STRIDE], MLA_ROPE, FREQ_BASE_MLA),
        _np_rope_expand([h * HEAD_DIM for h in range(N_HEADS)], ROT_MOBA, FREQ_BASE_MOBA),
        _np_rope_expand([g * DIFF_QK for g in range(2 * N_HEADS)], ROT_DIFF, FREQ_BASE_DIFF),
    ], axis=1)


def _dot(a, b):
    return jnp.dot(a, b, preferred_element_type=jnp.float32)


def _dot_nt(a, b):
    return lax.dot_general(a, b, (((1,), (1,)), ((), ())), preferred_element_type=jnp.float32)


def _split2(a):
    hi = a.astype(jnp.bfloat16)
    lo = (a - hi.astype(jnp.float32)).astype(jnp.bfloat16)
    return hi, lo


def _split3(a):
    hi = a.astype(jnp.bfloat16)
    r = a - hi.astype(jnp.float32)
    mid = r.astype(jnp.bfloat16)
    lo = (r - mid.astype(jnp.float32)).astype(jnp.bfloat16)
    return hi, mid, lo


def _group_mean_sq(a, g_bf16):
    return _dot((a * a).astype(jnp.bfloat16), g_bf16)


def _rope(x, tabs, half):
    w = x.shape[-1]
    lane = lax.broadcasted_iota(jnp.int32, x.shape, 1)
    partner = jnp.where((lane & (2 * half - 1)) >= half, pltpu.roll(x, half, 1), pltpu.roll(x, w - half, 1))
    return x * tabs[:, 0:w] + partner * tabs[:, w:2 * w]


def _lane_mask(shape, lo, hi):
    lane = lax.broadcasted_iota(jnp.int32, shape, len(shape) - 1)
    return (lane >= lo) & (lane < hi)


def _prep_body(x_ref, tr_ref, anorm_ref, win_ref, wuq_ref, wukvk_ref, wukvv_ref, gpair_ref, g64_ref, g32_ref,
               exp_ref, tril_ref, vsel_ref, par_ref,
               aq_ref, ak_ref, av_ref, fq_ref, fk_ref, fv_ref, fdcol_ref, fdrow_ref, mq_ref, mk_ref, mv_ref, msel_ref,
               dq_ref, dk_ref, dv_ref, eq_ref,
               kmean_s, carry_s):
    j = pl.program_id(1)
    tm = x_ref.shape[1]
    pr = MOBA_BLOCK
    n_parts = tm // pr
    nbp = kmean_s.shape[0]
    bf = jnp.bfloat16
    gpair, g64, g32 = gpair_ref[...], g64_ref[...], g32_ref[...]
    vsel = vsel_ref[...]
    tril = tril_ref[...]

    @pl.when(j == 0)
    def _():
        kmean_s[...] = jnp.zeros_like(kmean_s)
        carry_s[...] = jnp.zeros_like(carry_s)

    def prow(r, width=MXU_WIDTH):
        return par_ref[r:r + 1, 0:width]

    def project(st):
        rows = st["rows"]
        x = x_ref[0, rows, :]
        xb = (x * lax.rsqrt(jnp.mean(x * x, axis=-1, keepdims=True) + EPS) * anorm_ref[...]).astype(bf)
        tabs = _dot(tr_ref[0, rows, :], exp_ref[...])
        st["tab_mla"] = tabs[:, 0:2 * MXU_WIDTH]
        st["tab_moba"] = tabs[:, 2 * MXU_WIDTH:4 * MXU_WIDTH]
        st["tab_diff"] = tabs[:, 4 * MXU_WIDTH:6 * MXU_WIDTH]

        def run(lo, hi):
            wide = _dot(xb, win_ref[:, lo:hi])
            return lambda off, width: wide[:, off - lo:off - lo + width]

        run_m = run(PK_MQ, PK_DQ)
        run_a = run(PK_CQ, PK_FQ)
        run_f = run(PK_FQ, PK_MQ)
        run_d = run(PK_DQ, PK_END)
        st["direct"] = {"mq": run_m(PK_MQ, MXU_WIDTH), "mk": run_m(PK_MK, MXU_WIDTH),
                        "fq": run_f(PK_FQ, MXU_WIDTH), "fk": run_f(PK_FK, MXU_WIDTH),
                        "dq": run_d(PK_DQ, MXU_WIDTH), "dk": run_d(PK_DK, MXU_WIDTH),
                        "eq": run_d(PK_EQ, MXU_WIDTH)}
        st["values"] = {"fv": run_f(PK_FV, MXU_WIDTH).astype(bf), "mv": run_m(PK_MV, MXU_WIDTH).astype(bf),
                        "dv": run_d(PK_DV, MXU_WIDTH).astype(bf)}
        p_cq, p_ckv, st["p_kr"] = run_a(PK_CQ, MXU_WIDTH), run_a(PK_CKV, MLA_KV_RANK), run_a(PK_KR, 2 * MXU_WIDTH)
        cqn = p_cq * lax.rsqrt(jnp.sum(p_cq * p_cq, axis=-1, keepdims=True) * (1.0 / MLA_Q_RANK) + EPS) * prow(P_CQ)
        ckvn = p_ckv * lax.rsqrt(jnp.mean(p_ckv * p_ckv, axis=-1, keepdims=True) + EPS) * prow(P_CKV, MLA_KV_RANK)
        st["cqb"], st["ckvb"] = cqn.astype(bf), ckvn.astype(bf)
        z = run_f(PK_FF, LANES) + prow(P_FB, LANES)
        log_f = jnp.minimum(z, 0.0) - jnp.log1p(jnp.exp(-jnp.abs(z)))
        st["log_f"] = _split3(jnp.where(_lane_mask(log_f.shape, 0, N_HEADS), log_f, 0.0))

    def second(st):
        gmat = {"fq": g64, "fk": g64, "mq": g64, "mk": g64, "dq": g32, "dk": g32, "eq": g64}
        gains = {"fq": P_FQ, "fk": P_FK, "mq": P_MQ, "mk": P_MK, "dq": P_DQ, "dk": P_DK, "eq": P_EQ}
        ms = {name: _group_mean_sq(a, gmat[name]) for name, a in st["direct"].items()}
        st["qa"] = _dot(st["cqb"], wuq_ref[...])
        st["ka"] = st["p_kr"] + _dot(st["ckvb"], wukvk_ref[...])
        st["values"]["av"] = _dot(st["ckvb"], wukvv_ref[...]).astype(bf)
        l1, l2, l3 = st["log_f"]
        st["cumsum"] = (_dot(tril, l1) + _dot(tril, l2)) + _dot(tril, l3)
        st["normed"] = {name: a * lax.rsqrt(ms[name] + EPS) * prow(gains[name]) for name, a in st["direct"].items()}

    def third(st):
        rows = st["rows"]
        mq = _rope(st["normed"]["mq"], st["tab_moba"], ROT_MOBA // 2)
        mk = _rope(st["normed"]["mk"], st["tab_moba"], ROT_MOBA // 2)
        st["mq"], st["mk"] = mq, mk
        kmean_s[pl.ds(st["blk"], 1), :] = jnp.mean(mk, axis=0, keepdims=True)
        km_hi, km_lo = _split2(kmean_s[...])
        st["gates"] = []
        for h in range(N_HEADS):
            q_hi, q_lo = _split2(jnp.where(_lane_mask(mq.shape, h * HEAD_DIM, (h + 1) * HEAD_DIM), mq, 0.0))
            st["gates"].append((_dot_nt(km_hi, q_hi) + _dot_nt(km_lo, q_hi)) + _dot_nt(km_hi, q_lo))
        st["ms_qa"] = [_group_mean_sq(st["qa"][:, p * MXU_WIDTH:(p + 1) * MXU_WIDTH], gpair) for p in range(2)]
        st["ms_ka"] = [_group_mean_sq(st["ka"][:, p * MXU_WIDTH:(p + 1) * MXU_WIDTH], gpair) for p in range(2)]
        dec = carry_s[...] + st["cumsum"]
        carry_s[...] = dec[pr - 1:pr, :]
        dec2 = dec * LOG2E
        d1, d2, d3 = _split3(dec2)
        for h in range(N_HEADS):
            fdcol_ref[0, rows, h * LANES:(h + 1) * LANES] = jnp.broadcast_to(dec2[:, h:h + 1], (pr, LANES))
        row_sel = jnp.where(lax.broadcasted_iota(jnp.int32, (SUBLANES, LANES), 0)
                            == lax.broadcasted_iota(jnp.int32, (SUBLANES, LANES), 1), 1.0, 0.0).astype(bf)
        fdrow_ref[0, :, rows] = (_dot_nt(row_sel, d1) + _dot_nt(row_sel, d2)) + _dot_nt(row_sel, d3)
        for name, ref in (("fv", fv_ref), ("mv", mv_ref), ("dv", dv_ref), ("av", av_ref)):
            ref[0, :, rows] = _values_t(st["values"][name], vsel)

    def finish(st):
        rows = st["rows"]
        normed = st["normed"]
        fq_ref[0, rows, :] = normed["fq"].astype(bf)
        fk_ref[0, rows, :] = normed["fk"].astype(bf)
        eq_ref[0, rows, :] = normed["eq"].astype(bf)
        dq_ref[0, rows, :] = _rope(normed["dq"], st["tab_diff"], ROT_DIFF // 2).astype(bf)
        dk_ref[0, rows, :] = _rope(normed["dk"], st["tab_diff"], ROT_DIFF // 2).astype(bf)
        mq_ref[0, rows, :] = (st["mq"] * (LOG2E * HEAD_DIM ** -0.5)).astype(bf)
        mk_ref[0, rows, :] = st["mk"].astype(bf)
        for p in range(2):
            sl = slice(p * MXU_WIDTH, (p + 1) * MXU_WIDTH)
            aq_ref[0, rows, sl] = _rope(st["qa"][:, sl] * lax.rsqrt(st["ms_qa"][p] + EPS) * prow(P_GQ),
                                        st["tab_mla"], MLA_ROPE // 2).astype(bf)
            ak_ref[0, rows, sl] = _rope(st["ka"][:, sl] * lax.rsqrt(st["ms_ka"][p] + EPS) * prow(P_GK),
                                        st["tab_mla"], MLA_ROPE // 2).astype(bf)
        blk = lax.broadcasted_iota(jnp.int32, (nbp, pr), 0)
        past = blk < st["blk"]
        for h in range(N_HEADS):
            work = jnp.where(past, st["gates"][h], NEG_INF)
            sel = jnp.zeros((nbp, pr), jnp.bool_)
            for _ in range(MOBA_TOPK):
                mx = jnp.max(work, axis=0, keepdims=True)
                first = jnp.min(jnp.where(work == mx, blk, nbp), axis=0, keepdims=True)
                pick = blk == first
                sel = sel | pick
                work = jnp.where(pick, REMOVED, work)
            msel_ref[0, h * nbp:(h + 1) * nbp, rows] = jnp.where(sel & past, 0.0, NEG_INF)

    parts = [{"rows": slice(p * pr, (p + 1) * pr), "blk": j * n_parts + p} for p in range(n_parts)]
    for phase in (project, second, third, finish):
        for st in parts:
            phase(st)


def _const_spec(a, layer=None):
    if layer is None:
        n = a.ndim
        return pl.BlockSpec(a.shape, lambda *_: (0,) * n)
    n = a.ndim - 1
    return pl.BlockSpec((None,) + a.shape[1:], lambda *_: (layer,) + (0,) * n)


def _moba_blocks_padded(seq):
    return -(-(seq // MOBA_BLOCK) // SUBLANES) * SUBLANES


def _prep_call(layer, tm, x, tr, anorm, win, wuq, wukvk, wukvv, gpair, g64, g32, expand, tril, vsel, par):
    bsz, seq, d = x.shape
    assert seq % tm == 0 and tm % MOBA_BLOCK == 0 and tril.shape == (MOBA_BLOCK, MOBA_BLOCK)
    nbp = _moba_blocks_padded(seq)
    bf = jnp.bfloat16
    f32 = jnp.float32

    vt = -N_HEADS * V_ROWS
    widths = [(2 * MXU_WIDTH, bf), (2 * MXU_WIDTH, bf), (vt, bf),
              (GROUP_WIDTH, bf), (GROUP_WIDTH, bf), (vt, bf),
              (N_HEADS * LANES, f32), (-SUBLANES, f32),
              (GROUP_WIDTH, bf), (GROUP_WIDTH, bf), (vt, bf), (-N_HEADS * nbp, f32),
              (GROUP_WIDTH, bf), (GROUP_WIDTH, bf), (vt, bf),
              (GROUP_WIDTH, bf)]

    def tok(width):
        if width > 0:
            return pl.BlockSpec((1, tm, width), lambda b, j: (b, j, 0))
        return pl.BlockSpec((1, -width, tm), lambda b, j: (b, 0, j))

    def shape(width):
        return (bsz, seq, width) if width > 0 else (bsz, -width, seq)

    consts = [anorm, win, wuq, wukvk, wukvv, gpair, g64, g32, expand, tril, vsel, par]
    layered = [True, True, True, True, True, False, False, False, False, False, False, True]
    return pl.pallas_call(
        _prep_body,
        grid=(bsz, seq // tm),
        in_specs=[tok(d), tok(TR_WIDTH)] + [_const_spec(c, layer if ly else None) for c, ly in zip(consts, layered)],
        out_specs=[tok(w) for w, _ in widths],
        out_shape=[jax.ShapeDtypeStruct(shape(w), dt) for w, dt in widths],
        scratch_shapes=[pltpu.VMEM((nbp, GROUP_WIDTH), jnp.float32), pltpu.VMEM((1, LANES), jnp.float32)],
        compiler_params=pltpu.CompilerParams(dimension_semantics=("arbitrary", "arbitrary"),
                                             vmem_limit_bytes=VMEM_LIMIT_BYTES),
        name="prep",
    )(x, tr, *consts)


def _memkv_body(mem_ref, mnorm_ref, w_ref, g64_ref, gain_ref, vsel_ref, k_ref, v_ref):
    m = mem_ref[0]
    mn = m * lax.rsqrt(jnp.mean(m * m, axis=-1, keepdims=True) + EPS) * mnorm_ref[...]
    kv = _dot(mn.astype(jnp.bfloat16), w_ref[...])
    k = kv[:, 0:GROUP_WIDTH]
    k = k * lax.rsqrt(_group_mean_sq(k, g64_ref[...]) + EPS) * gain_ref[...]
    k_ref[0] = k.astype(jnp.bfloat16)
    v_ref[0] = _values_t(kv[:, GROUP_WIDTH:2 * GROUP_WIDTH].astype(jnp.bfloat16), vsel_ref[...])


def _memkv_call(layer, mem, mnorm, w, g64, gain, vsel):
    bsz, mlen, d = mem.shape
    k_shape, vt_shape = (bsz, mlen, GROUP_WIDTH), (bsz, N_HEADS * V_ROWS, mlen)
    return pl.pallas_call(
        _memkv_body,
        grid=(bsz,),
        in_specs=[pl.BlockSpec((1, mlen, d), lambda b: (b, 0, 0)), _const_spec(mnorm, layer), _const_spec(w, layer),
                  _const_spec(g64), _const_spec(gain, layer), _const_spec(vsel)],
        out_specs=[pl.BlockSpec((1,) + s[1:], lambda b: (b, 0, 0)) for s in (k_shape, vt_shape)],
        out_shape=[jax.ShapeDtypeStruct(s, jnp.bfloat16) for s in (k_shape, vt_shape)],
        compiler_params=pltpu.CompilerParams(dimension_semantics=("arbitrary",), vmem_limit_bytes=VMEM_LIMIT_BYTES),
        name="mem_kv",
    )(mem, mnorm, w, g64, gain, vsel)


Q_SUB = 2 * MXU_WIDTH


class _AttnCfg:
    def __init__(self, name, vheads, n_maps, causal, decay=False, select=False, diff=False, tile=1024, lookahead=8,
                 q_sub=Q_SUB):
        self.name = name
        self.vheads = vheads
        self.n_maps = n_maps
        self.causal = causal
        self.decay = decay
        self.select = select
        self.diff = diff
        self.tile = tile
        self.lookahead = lookahead
        self.q_sub = q_sub


_PLAIN_VHEADS = [(0, h * HEAD_DIM, (h + 1) * HEAD_DIM, 0, h) for h in range(N_HEADS)]
_CFG_MLA = _AttnCfg("attn_mla", [((h // 2) * MXU_WIDTH, (h % 2) * PAIR_STRIDE, (h % 2) * PAIR_STRIDE + MLA_QK, 0, h)
                                 for h in range(N_HEADS)], 1, True)
_CFG_FOX = _AttnCfg("attn_fox", _PLAIN_VHEADS, 1, True, decay=True)
_CFG_MOBA = _AttnCfg("attn_moba", _PLAIN_VHEADS, 1, True, select=True, tile=2048, lookahead=4)
_CFG_DIFF = _AttnCfg("attn_diff", [(0, h * HEAD_DIM + c * DIFF_QK, h * HEAD_DIM + (c + 1) * DIFF_QK, c, h)
                                   for c in range(2) for h in range(N_HEADS)], 2, True, diff=True, q_sub=MXU_WIDTH)
_CFG_MEM = _AttnCfg("attn_mem", _PLAIN_VHEADS, 1, False, tile=2048)


ONES_ROW = HEAD_DIM
V_ROWS = HEAD_DIM + 16


def _np_value_select():
    sel = np.zeros((N_HEADS * V_ROWS, GROUP_WIDTH), np.float32)
    for h in range(N_HEADS):
        for d in range(HEAD_DIM):
            sel[h * V_ROWS + d, h * HEAD_DIM + d] = 1.0
    return sel


def _values_t(v, vsel):
    vt = _dot_nt(vsel, v)
    row = lax.broadcasted_iota(jnp.int32, vt.shape, 0)
    ones = row == ONES_ROW
    for h in range(1, N_HEADS):
        ones = ones | (row == h * V_ROWS + ONES_ROW)
    return jnp.where(ones, 1.0, vt).astype(jnp.bfloat16)


def _tile_lanes(x, width):
    return jnp.tile(x, (1, width // LANES)) if width != LANES else x


def _attn_body(cfg, qi_ref, kj_ref, *refs):
    refs = list(refs)
    q_ref, k_ref, vt_ref = refs[:3]
    pos = 3
    if cfg.decay:
        dq_ref, dk_ref = refs[pos:pos + 2]
        pos += 2
    if cfg.select:
        sel_ref = refs[pos]
        pos += 1
    if cfg.diff:
        g64_ref, gsub_ref, lam_ref = refs[pos:pos + 3]
        pos += 3
    o_ref, qm_s, m_s, acc_s = refs[pos:pos + 4]

    t = pl.program_id(1)
    i = qi_ref[t]
    j = kj_ref[t]
    tq = q_ref.shape[1]
    tk = k_ref.shape[1]

    @pl.when(j == 0)
    def _():
        for n, (off, lo, hi, _, _) in enumerate(cfg.vheads):
            qb = q_ref[0, :, off:off + MXU_WIDTH]
            qm_s[n] = jnp.where(_lane_mask(qb.shape, lo, hi), qb, jnp.zeros_like(qb))
        m_s[...] = jnp.full(m_s.shape, NEG_INF, jnp.float32)
        acc_s[...] = jnp.zeros_like(acc_s)

    def step(diag):
        qs = min(tq, cfg.q_sub)
        items = [(n, u) for n in range(len(cfg.vheads)) for u in range(tq // qs)]

        def n_keys(u):
            return (u + 1) * qs if diag else tk

        def scores(item):
            n, u = item
            off, _, _, _, h = cfg.vheads[n]
            nk = n_keys(u)
            cols = slice(u * qs, (u + 1) * qs)
            s = _dot_nt(k_ref[0, 0:nk, off:off + MXU_WIDTH], qm_s[n, cols, :])
            if cfg.decay:
                s = (dq_ref[0, h:h + 1, cols] - _tile_lanes(dk_ref[0, 0:nk, h * LANES:(h + 1) * LANES], qs)) + s
            if cfg.select:
                nbp = sel_ref.shape[1] // N_HEADS
                qpos = u * qs + lax.broadcasted_iota(jnp.int32, (1, qs), 1)
                parts = []
                for kb in range(nk // MOBA_BLOCK):
                    rows = s[kb * MOBA_BLOCK:(kb + 1) * MOBA_BLOCK, :]
                    if not (diag and kb == nk // MOBA_BLOCK - 1):
                        bias = sel_ref[0, pl.ds(h * nbp + j * (tk // MOBA_BLOCK) + kb, 1), cols]
                        if diag:
                            bias = jnp.where(qpos < (kb + 1) * MOBA_BLOCK, 0.0, bias)
                        rows = rows + bias
                    parts.append(rows)
                s = parts[0] if len(parts) == 1 else jnp.concatenate(parts, axis=0)
            if diag:
                key = lax.broadcasted_iota(jnp.int32, (nk, qs), 0)
                qry = u * qs + lax.broadcasted_iota(jnp.int32, (nk, qs), 1)
                s = jnp.where(key <= qry, s, NEG_INF)
            return s, jnp.max(s, axis=0, keepdims=True)

        raw = {it: scores(items[it]) for it in range(min(cfg.lookahead, len(items)))}
        for it, (n, u) in enumerate(items):
            h = cfg.vheads[n][4]
            nk = n_keys(u)
            cols = slice(u * qs, (u + 1) * qs)
            s, s_max = raw.pop(it)
            m_prev = m_s[n, :, cols]
            m_new = jnp.maximum(m_prev, s_max)
            alpha = jnp.exp2(m_prev - m_new)
            p = jnp.exp2(s - m_new)
            m_s[n, :, cols] = m_new
            acc_s[n, :, cols] = acc_s[n, :, cols] * alpha + _dot(vt_ref[0, h * V_ROWS:(h + 1) * V_ROWS, 0:nk],
                                                                 p.astype(jnp.bfloat16))
            if it + cfg.lookahead < len(items):
                raw[it + cfg.lookahead] = scores(items[it + cfg.lookahead])

    if cfg.causal:
        pl.when(j < i)(functools.partial(step, False))
        pl.when(j == i)(functools.partial(step, True))
        last = j == i
    else:
        step(False)
        last = j == 0

    @pl.when(last)
    def _():
        outs = []
        for c in range(cfg.n_maps):
            heads = []
            for h in range(N_HEADS):
                acc = acc_s[c * N_HEADS + h]
                heads.append(acc[0:HEAD_DIM, :] / acc[ONES_ROW:ONES_ROW + 1, :])
            outs.append(jnp.concatenate(heads, axis=0).T)
        if cfg.diff:
            o = outs[0] - lam_ref[0:1, :] * outs[1]
            o = o * lax.rsqrt(_group_mean_sq(o, g64_ref[...]) + EPS) * gsub_ref[...]
        else:
            o = outs[0]
        o_ref[0] = o.astype(o_ref.dtype)


def _attn_call(cfg, q, k, v, extras, tq, tk):
    bsz, seq, wq = q.shape
    sk = k.shape[1]
    nq = seq // tq
    assert seq % tq == 0 and sk % tk == 0
    if cfg.causal:
        assert tq == tk and sk == seq
        pairs = [(i, j) for i in range(nq) for j in range(i + 1)]
    else:
        assert sk == tk
        pairs = [(i, 0) for i in range(nq)]
    qi = jnp.asarray(np.array([p[0] for p in pairs], np.int32))
    kj = jnp.asarray(np.array([p[1] for p in pairs], np.int32))
    n_vh = len(cfg.vheads)

    in_specs = [pl.BlockSpec((1, tq, wq), lambda b, t, qi, kj: (b, qi[t], 0)),
                pl.BlockSpec((1, tk, wq), lambda b, t, qi, kj: (b, kj[t], 0)),
                pl.BlockSpec((1, N_HEADS * V_ROWS, tk), lambda b, t, qi, kj: (b, 0, kj[t]))]
    args = [q, k, v]
    if cfg.decay:
        dcol, drow = extras
        in_specs += [pl.BlockSpec((1, SUBLANES, tq), lambda b, t, qi, kj: (b, 0, qi[t])),
                     pl.BlockSpec((1, tk, N_HEADS * LANES), lambda b, t, qi, kj: (b, kj[t], 0))]
        args += [drow, dcol]
    if cfg.select:
        (sel,) = extras
        in_specs += [pl.BlockSpec((1, sel.shape[1], tq), lambda b, t, qi, kj: (b, 0, qi[t]))]
        args += [sel]
    if cfg.diff:
        layer, g64, gsub, lam_row = extras
        in_specs += [_const_spec(g64), _const_spec(gsub, layer), _const_spec(lam_row, layer)]
        args += [g64, gsub, lam_row]

    grid_spec = pltpu.PrefetchScalarGridSpec(
        num_scalar_prefetch=2,
        grid=(bsz, len(pairs)),
        in_specs=in_specs,
        out_specs=pl.BlockSpec((1, tq, GROUP_WIDTH), lambda b, t, qi, kj: (b, qi[t], 0)),
        scratch_shapes=[pltpu.VMEM((n_vh, tq, MXU_WIDTH), jnp.bfloat16),
                        pltpu.VMEM((n_vh, 1, tq), jnp.float32),
                        pltpu.VMEM((n_vh, V_ROWS, tq), jnp.float32)])
    return pl.pallas_call(
        functools.partial(_attn_body, cfg),
        grid_spec=grid_spec,
        out_shape=jax.ShapeDtypeStruct((bsz, seq, GROUP_WIDTH), jnp.bfloat16),
        compiler_params=pltpu.CompilerParams(dimension_semantics=("arbitrary", "arbitrary"),
                                             vmem_limit_bytes=VMEM_LIMIT_BYTES),
        name=cfg.name,
    )(qi, kj, *args)


def _ffn_body(nf, x_ref, oa_ref, ob_ref, oc_ref, od_ref, oe_ref, wo_ref, fnorm_ref, wg_ref, wu_ref, cw_ref, cb_ref,
              wd_ref, out_ref, xnew_s, xn_s, acc_s):
    i = pl.program_id(1)
    f = pl.program_id(2)
    tm = x_ref.shape[1]

    @pl.when(f == 0)
    def _():
        @pl.when(i == 0)
        def _():
            xn_s[0:TAIL_ROWS, :] = jnp.zeros((TAIL_ROWS, xn_s.shape[1]), xn_s.dtype)

        @pl.when(i > 0)
        def _():
            xn_s[0:TAIL_ROWS, :] = xn_s[tm:tm + TAIL_ROWS, :]

        mixed = jnp.concatenate([o_ref[0] for o_ref in (oa_ref, ob_ref, oc_ref, od_ref, oe_ref)], axis=1)
        xnew = x_ref[0] + _dot(mixed, wo_ref[...])
        xnew_s[...] = xnew
        xn = xnew * lax.rsqrt(jnp.mean(xnew * xnew, axis=-1, keepdims=True) + EPS) * fnorm_ref[...]
        xn_s[TAIL_ROWS:TAIL_ROWS + tm, :] = xn.astype(xn_s.dtype)

    def mlp_chunk():
        ge = _dot(xn_s[...], wg_ref[...])
        u = _dot(xn_s[TAIL_ROWS:TAIL_ROWS + tm, :], wu_ref[...])
        g0 = ge[TAIL_ROWS:TAIL_ROWS + tm, :]
        t1 = ge[TAIL_ROWS - 1:TAIL_ROWS, :]
        t2 = ge[TAIL_ROWS - 2:TAIL_ROWS - 1, :]
        row = lax.broadcasted_iota(jnp.int32, g0.shape, 0)
        g1 = jnp.where(row == 0, t1, pltpu.roll(g0, 1, 0))
        g2 = jnp.where(row == 0, t2, jnp.where(row == 1, t1, pltpu.roll(g0, 2, 0)))
        y = cb_ref[...] + cw_ref[0:1, :] * g2
        y = y + cw_ref[1:2, :] * g1
        y = y + cw_ref[2:3, :] * g0
        hmid = (y * (1.0 / (1.0 + jnp.exp(-y)))) * u
        return _dot(hmid.astype(jnp.bfloat16), wd_ref[...])

    @pl.when(f == 0)
    def _():
        acc_s[...] = mlp_chunk()

    if nf > 2:
        @pl.when((f > 0) & (f < nf - 1))
        def _():
            acc_s[...] += mlp_chunk()

    @pl.when(f == nf - 1)
    def _():
        out_ref[0] = xnew_s[...] + (acc_s[...] + mlp_chunk())


def _ffn_call(layer, x, outs, wo, fnorm, wg, wu, cw, cb, wd, tm, tf):
    bsz, seq, d = x.shape
    dff = wg.shape[2]
    n_tiles, nf = seq // tm, dff // tf
    assert seq % tm == 0 and dff % tf == 0 and nf >= 2
    tok = lambda w: pl.BlockSpec((1, tm, w), lambda b, i, f: (b, i, 0))
    return pl.pallas_call(
        functools.partial(_ffn_body, nf),
        grid=(bsz, n_tiles, nf),
        in_specs=[tok(d)] + [tok(GROUP_WIDTH)] * 5 + [
            _const_spec(wo, layer),
            _const_spec(fnorm, layer),
            pl.BlockSpec((None, d, tf), lambda b, i, f: (layer, 0, f)),
            pl.BlockSpec((None, d, tf), lambda b, i, f: (layer, 0, f)),
            pl.BlockSpec((None, SUBLANES, tf), lambda b, i, f: (layer, 0, f)),
            pl.BlockSpec((None, 1, tf), lambda b, i, f: (layer, 0, f)),
            pl.BlockSpec((None, tf, d), lambda b, i, f: (layer, f, 0))],
        out_specs=tok(d),
        out_shape=jax.ShapeDtypeStruct((bsz, seq, d), jnp.float32),
        scratch_shapes=[pltpu.VMEM((tm, d), jnp.float32), pltpu.VMEM((TAIL_ROWS + tm, d), jnp.bfloat16),
                        pltpu.VMEM((tm, d), jnp.float32)],
        compiler_params=pltpu.CompilerParams(dimension_semantics=("arbitrary", "arbitrary", "arbitrary"),
                                             vmem_limit_bytes=VMEM_LIMIT_BYTES),
        name="ffn",
    )(x, *outs, wo, fnorm, wg, wu, cw, cb, wd)


def _pad_rows(v, width=MXU_WIDTH):
    return jnp.pad(v.astype(jnp.float32), ((0, 0), (0, width - v.shape[1])))


def _tile_rows(g, reps):
    return jnp.tile(g.astype(jnp.float32), (1, reps))


def _pack_in_projection(w):
    idx = _np_in_index()
    pieces, start = [], 0
    while start < PK_END:
        stop = start + 1
        if idx[start] == _SRC_END:
            while stop < PK_END and idx[stop] == _SRC_END:
                stop += 1
            pieces.append(jnp.zeros(w.shape[:-1] + (stop - start,), w.dtype))
        else:
            while stop < PK_END and idx[stop] == idx[stop - 1] + 1:
                stop += 1
            pieces.append(w[..., int(idx[start]):int(idx[stop - 1]) + 1])
        start = stop
    return jnp.concatenate(pieces, axis=-1)


def _zero_col(w):
    return jnp.concatenate([w, jnp.zeros(w.shape[:-1] + (1,), w.dtype)], axis=-1)


def _rope_table(positions):
    pos = positions.astype(jnp.float32)[:, :, None]
    inv = [ROPE_THETA ** (-jnp.arange(0, rot, 2, dtype=jnp.float32) / rot) for rot in (MLA_ROPE, ROT_MOBA, ROT_DIFF)]
    inv = jnp.concatenate(inv + [jnp.zeros((N_FREQ - TR_ONE,), jnp.float32)])
    ang = pos * inv
    c, s = jnp.cos(ang), jnp.sin(ang)
    c_hi = c.astype(jnp.bfloat16)
    c_lo = (c - c_hi.astype(jnp.float32)).astype(jnp.bfloat16)
    s_hi = s.astype(jnp.bfloat16)
    s_lo = (s - s_hi.astype(jnp.float32)).astype(jnp.bfloat16)
    return jnp.concatenate([c_hi, c_lo, s_hi, s_lo], axis=-1)


def _pick_tile(n, pref):
    t = pref
    while n % t:
        t //= 2
    return t


def kernel(x, mem, positions, attn_norm, ffn_norm, mem_norm, w_in, mla_cq_norm, mla_ckv_norm, mla_w_uq, mla_w_ukv, mla_q_norm, mla_k_norm, fox_b_f, fox_q_norm, fox_k_norm, moba_q_norm, moba_k_norm, diff_lambda, diff_q_norm, diff_k_norm, diff_sub_norm, mem_w_kv, mem_q_norm, mem_k_norm, w_o, ffn_w_gate, ffn_w_up, ffn_conv_w, ffn_conv_b, ffn_w_down):
    bsz, seq, d = x.shape
    depth = w_in.shape[0]
    dff = ffn_w_gate.shape[2]
    bf = jnp.bfloat16
    f32 = jnp.float32

    uq_idx = _np_uq_index()
    ukvk_idx, ukvv_idx = _np_ukv_index()
    gpair = jnp.asarray(_np_group_matrix(_PAIR_GROUPS), bf)
    g64 = jnp.asarray(_np_group_matrix(_G64_GROUPS), bf)
    g32 = jnp.asarray(_np_group_matrix(_G32_GROUPS), bf)
    expand = jnp.asarray(_np_rope_expand_all(), bf)
    t_prep = max(_pick_tile(seq, PREP_TILE), MOBA_BLOCK)
    tril = jnp.asarray(np.tril(np.ones((MOBA_BLOCK, MOBA_BLOCK), np.float32)), bf)
    vsel = jnp.asarray(_np_value_select(), bf)
    tr = _rope_table(positions)

    tile = lambda cfg: max(_pick_tile(seq, cfg.tile), MOBA_BLOCK) if cfg.select else _pick_tile(seq, cfg.tile)
    t_ffn = _pick_tile(seq, 512)
    tf = dff // 2 if (dff // 2) % LANES == 0 else dff

    win = _pack_in_projection(w_in.astype(bf))
    wuq = jnp.take(_zero_col(mla_w_uq), uq_idx, axis=2)
    wuq = jnp.pad(wuq, ((0, 0), (0, MXU_WIDTH - MLA_Q_RANK), (0, 0))).astype(bf)
    wukv = _zero_col(mla_w_ukv)
    wukvk = jnp.take(wukv, ukvk_idx, axis=2).astype(bf)
    wukvv = jnp.take(wukv, ukvv_idx, axis=2).astype(bf)
    pair = lambda g: _pad_rows(_tile_rows(g, 2))
    rows = [jnp.zeros((depth, MXU_WIDTH), f32)] * P_ROWS
    rows[P_CQ] = _pad_rows(mla_cq_norm)
    rows[P_CKV] = _pad_rows(mla_ckv_norm)
    rows[P_GQ] = pair(mla_q_norm) * (LOG2E * MLA_QK ** -0.5)
    rows[P_GK] = pair(mla_k_norm)
    rows[P_FQ] = _tile_rows(fox_q_norm, N_HEADS) * (LOG2E * HEAD_DIM ** -0.5)
    rows[P_FK] = _tile_rows(fox_k_norm, N_HEADS)
    rows[P_FB] = _pad_rows(fox_b_f)
    rows[P_MQ] = _tile_rows(moba_q_norm, N_HEADS)
    rows[P_MK] = _tile_rows(moba_k_norm, N_HEADS)
    rows[P_DQ] = _tile_rows(diff_q_norm, 2 * N_HEADS) * (LOG2E * DIFF_QK ** -0.5)
    rows[P_DK] = _tile_rows(diff_k_norm, 2 * N_HEADS)
    rows[P_EQ] = _tile_rows(mem_q_norm, N_HEADS) * (LOG2E * HEAD_DIM ** -0.5)
    par = jnp.stack(rows, axis=1)
    anorm = attn_norm.astype(f32)[:, None, :]
    mnorm = mem_norm.astype(f32)[:, None, :]
    fnorm = ffn_norm.astype(f32)[:, None, :]
    wmem = mem_w_kv.astype(bf)
    mem_gain = _tile_rows(mem_k_norm, N_HEADS)[:, None, :]

    lam_init = jnp.asarray([0.8 - 0.6 * math.exp(-0.3 * l) for l in range(depth)], f32)
    lam_vec = diff_lambda.astype(f32)
    lam = (jnp.exp(jnp.sum(lam_vec[:, 0] * lam_vec[:, 1], axis=-1))
           - jnp.exp(jnp.sum(lam_vec[:, 2] * lam_vec[:, 3], axis=-1)) + lam_init)
    lam_row = jnp.broadcast_to(lam[:, None, None], (depth, 1, GROUP_WIDTH))
    gsub = (_tile_rows(diff_sub_norm, N_HEADS) * (1.0 - lam_init)[:, None])[:, None, :]

    wo = w_o.astype(bf)
    wg, wu, wd = ffn_w_gate.astype(bf), ffn_w_up.astype(bf), ffn_w_down.astype(bf)
    cw = jnp.pad(ffn_conv_w.astype(f32), ((0, 0), (0, SUBLANES - CONV_WIDTH), (0, 0)))
    cb = ffn_conv_b.astype(f32)[:, None, :]

    for l in range(depth):
        (aq, ak, av, fq, fk, fv, fdcol, fdrow, mq, mk, mv, msel, dq, dk, dv, eq) = _prep_call(
            l, t_prep, x, tr, anorm, win, wuq, wukvk, wukvv, gpair, g64, g32, expand, tril, vsel, par)
        ek, ev = _memkv_call(l, mem, mnorm, wmem, g64, mem_gain, vsel)

        o_a = _attn_call(_CFG_MLA, aq, ak, av, (), tile(_CFG_MLA), tile(_CFG_MLA))
        o_b = _attn_call(_CFG_FOX, fq, fk, fv, (fdcol, fdrow), tile(_CFG_FOX), tile(_CFG_FOX))
        o_c = _attn_call(_CFG_MOBA, mq, mk, mv, (msel,), tile(_CFG_MOBA), tile(_CFG_MOBA))
        o_d = _attn_call(_CFG_DIFF, dq, dk, dv, (l, g64, gsub, lam_row), tile(_CFG_DIFF), tile(_CFG_DIFF))
        o_e = _attn_call(_CFG_MEM, eq, ek, ev, (), tile(_CFG_MEM), mem.shape[1])

        x = _ffn_call(l, x, (o_a, o_b, o_c, o_d, o_e), wo, fnorm, wg, wu, cw, cb, wd, t_ffn, tf)
    return x
```

```python
import functools
import math

import numpy as np
import jax
import jax.numpy as jnp
from jax import lax
from jax.experimental import pallas as pl
from jax.experimental.pallas import tpu as pltpu

N_HEADS = 4
HEAD_DIM = 64
GROUP_WIDTH = N_HEADS * HEAD_DIM
MLA_Q_RANK = 192
MLA_KV_RANK = 128
MLA_NOPE = 64
MLA_ROPE = 32
MLA_QK = MLA_NOPE + MLA_ROPE
DIFF_QK = HEAD_DIM // 2
ROPE_THETA = 500000.0
ROT_MOBA = HEAD_DIM // 4
ROT_DIFF = DIFF_QK // 4
MOBA_BLOCK = 256
MOBA_TOPK = 3
CONV_WIDTH = 3
EPS = 1e-6
NEG_INF = -1e30
LOG2E = math.log2(math.e)
REMOVED = -3e38

LANES = 128
SUBLANES = 8
MXU_WIDTH = 256
TAIL_ROWS = 16
PREP_TILE = 512
VMEM_LIMIT_BYTES = 56 * 1024 * 1024

_SRC_CQ = 0
_SRC_CKV = _SRC_CQ + MLA_Q_RANK
_SRC_KR = _SRC_CKV + MLA_KV_RANK
_SRC_FOX = _SRC_KR + MLA_ROPE
_SRC_FOXF = _SRC_FOX + 3 * GROUP_WIDTH
_SRC_MOBA = _SRC_FOXF + N_HEADS
_SRC_DIFF = _SRC_MOBA + 3 * GROUP_WIDTH
_SRC_MEMQ = _SRC_DIFF + 3 * GROUP_WIDTH
_SRC_END = _SRC_MEMQ + GROUP_WIDTH

PK_CQ = 0
PK_CKV = 256
PK_KR = 384
PK_FQ, PK_FK, PK_FV = 896, 1152, 1408
PK_FF = 1664
PK_MQ, PK_MK, PK_MV = 1792, 2048, 2304
PK_DQ, PK_DK, PK_DV = 2560, 2816, 3072
PK_EQ = 3328
PK_END = 3584

PAIR_STRIDE = MLA_QK


def _pair_lane(h, d):
    return (h // 2) * MXU_WIDTH + (h % 2) * PAIR_STRIDE + d


N_FREQ = 32
FREQ_BASE_MLA = 0
FREQ_BASE_MOBA = MLA_ROPE // 2
FREQ_BASE_DIFF = FREQ_BASE_MOBA + ROT_MOBA // 2
TR_ONE = FREQ_BASE_DIFF + ROT_DIFF // 2
TR_WIDTH = 4 * N_FREQ
assert TR_ONE < N_FREQ and TR_WIDTH == LANES

(P_CQ, P_CKV, P_GQ, P_GK, P_FQ, P_FK, P_FB, P_MQ, P_MK, P_DQ, P_DK, P_EQ) = range(12)
P_ROWS = 16


def _np_in_index():
    idx = np.full((PK_END,), _SRC_END, np.int32)
    idx[PK_CQ:PK_CQ + MLA_Q_RANK] = np.arange(_SRC_CQ, _SRC_CQ + MLA_Q_RANK)
    idx[PK_CKV:PK_CKV + MLA_KV_RANK] = np.arange(_SRC_CKV, _SRC_CKV + MLA_KV_RANK)
    for h in range(N_HEADS):
        for d in range(MLA_ROPE):
            idx[PK_KR + _pair_lane(h, d)] = _SRC_KR + d
    idx[PK_FQ:PK_FQ + 3 * GROUP_WIDTH] = np.arange(_SRC_FOX, _SRC_FOX + 3 * GROUP_WIDTH)
    idx[PK_FF:PK_FF + N_HEADS] = np.arange(_SRC_FOXF, _SRC_FOXF + N_HEADS)
    idx[PK_MQ:PK_MQ + 3 * GROUP_WIDTH] = np.arange(_SRC_MOBA, _SRC_MOBA + 3 * GROUP_WIDTH)
    idx[PK_DQ:PK_DQ + 3 * GROUP_WIDTH] = np.arange(_SRC_DIFF, _SRC_DIFF + 3 * GROUP_WIDTH)
    idx[PK_EQ:PK_EQ + GROUP_WIDTH] = np.arange(_SRC_MEMQ, _SRC_MEMQ + GROUP_WIDTH)
    return idx


def _np_uq_index():
    idx = np.full((2 * MXU_WIDTH,), N_HEADS * MLA_QK, np.int32)
    for h in range(N_HEADS):
        for d in range(MLA_QK):
            idx[_pair_lane(h, d)] = h * MLA_QK + d
    return idx


def _np_ukv_index():
    zero = N_HEADS * (MLA_NOPE + HEAD_DIM)
    idx_k = np.full((2 * MXU_WIDTH,), zero, np.int32)
    idx_v = np.zeros((GROUP_WIDTH,), np.int32)
    for h in range(N_HEADS):
        for d in range(MLA_NOPE):
            idx_k[_pair_lane(h, MLA_ROPE + d)] = h * (MLA_NOPE + HEAD_DIM) + d
        for d in range(HEAD_DIM):
            idx_v[h * HEAD_DIM + d] = h * (MLA_NOPE + HEAD_DIM) + MLA_NOPE + d
    return idx_k, idx_v


def _np_group_matrix(groups):
    g = np.zeros((MXU_WIDTH, MXU_WIDTH), np.float32)
    for lo, size in groups:
        g[lo:lo + size, lo:lo + size] = 1.0 / size
    return g


_PAIR_GROUPS = [(0, MLA_ROPE), (MLA_ROPE, MLA_NOPE), (PAIR_STRIDE, MLA_ROPE), (PAIR_STRIDE + MLA_ROPE, MLA_NOPE)]
_G64_GROUPS = [(h * HEAD_DIM, HEAD_DIM) for h in range(N_HEADS)]
_G32_GROUPS = [(g * DIFF_QK, DIFF_QK) for g in range(2 * N_HEADS)]


def _np_rope_expand(regions, rot, base):
    half = rot // 2
    e = np.zeros((TR_WIDTH, 2 * MXU_WIDTH), np.float32)
    e[TR_ONE, 0:MXU_WIDTH] = 1.0
    for lo in regions:
        assert lo % rot == 0
        for r in range(half):
            f = base + r
            for lane, sign in ((lo + r, -1.0), (lo + half + r, 1.0)):
                e[TR_ONE, lane] = 0.0
                e[f, lane] = 1.0
                e[N_FREQ + f, lane] = 1.0
                e[2 * N_FREQ + f, MXU_WIDTH + lane] = sign
                e[3 * N_FREQ + f, MXU_WIDTH + lane] = sign
    return e


def _np_rope_expand_all():
    return np.concatenate([
        _np_rope_expand([0, PAIR_STRIDE], MLA_ROPE, FREQ_BASE_MLA),
        _np_rope_expand([h * HEAD_DIM for h in range(N_HEADS)], ROT_MOBA, FREQ_BASE_MOBA),
        _np_rope_expand([g * DIFF_QK for g in range(2 * N_HEADS)], ROT_DIFF, FREQ_BASE_DIFF),
    ], axis=1)


def _dot(a, b):
    return jnp.dot(a, b, preferred_element_type=jnp.float32)


def _dot_nt(a, b):
    return lax.dot_general(a, b, (((1,), (1,)), ((), ())), preferred_element_type=jnp.float32)


def _split2(a):
    hi = a.astype(jnp.bfloat16)
    lo = (a - hi.astype(jnp.float32)).astype(jnp.bfloat16)
    return hi, lo


def _split3(a):
    hi = a.astype(jnp.bfloat16)
    r = a - hi.astype(jnp.float32)
    mid = r.astype(jnp.bfloat16)
    lo = (r - mid.astype(jnp.float32)).astype(jnp.bfloat16)
    return hi, mid, lo


def _group_mean_sq(a, g_bf16):
    return _dot((a * a).astype(jnp.bfloat16), g_bf16)


def _rope(x, tabs, half):
    w = x.shape[-1]
    lane = lax.broadcasted_iota(jnp.int32, x.shape, 1)
    partner = jnp.where((lane & (2 * half - 1)) >= half, pltpu.roll(x, half, 1), pltpu.roll(x, w - half, 1))
    return x * tabs[:, 0:w] + partner * tabs[:, w:2 * w]


def _lane_mask(shape, lo, hi):
    lane = lax.broadcasted_iota(jnp.int32, shape, len(shape) - 1)
    return (lane >= lo) & (lane < hi)


def _prep_body(x_ref, tr_ref, anorm_ref, win_ref, wuq_ref, wukvk_ref, wukvv_ref, gpair_ref, g64_ref, g32_ref,
               exp_ref, tril_ref, vsel_ref, par_ref,
               aq_ref, ak_ref, av_ref, fq_ref, fk_ref, fv_ref, fdcol_ref, fdrow_ref, mq_ref, mk_ref, mv_ref, msel_ref,
               dq_ref, dk_ref, dv_ref, eq_ref,
               kmean_s, carry_s):
    j = pl.program_id(1)
    tm = x_ref.shape[1]
    pr = MOBA_BLOCK
    n_parts = tm // pr
    nbp = kmean_s.shape[0]
    bf = jnp.bfloat16
    gpair, g64, g32 = gpair_ref[...], g64_ref[...], g32_ref[...]
    vsel = vsel_ref[...]
    tril = tril_ref[...]

    @pl.when(j == 0)
    def _():
        kmean_s[...] = jnp.zeros_like(kmean_s)
        carry_s[...] = jnp.zeros_like(carry_s)

    def prow(r, width=MXU_WIDTH):
        return par_ref[r:r + 1, 0:width]

    def project(st):
        rows = st["rows"]
        x = x_ref[0, rows, :]
        xb = (x * lax.rsqrt(jnp.mean(x * x, axis=-1, keepdims=True) + EPS) * anorm_ref[...]).astype(bf)
        tabs = _dot(tr_ref[0, rows, :], exp_ref[...])
        st["tab_mla"] = tabs[:, 0:2 * MXU_WIDTH]
        st["tab_moba"] = tabs[:, 2 * MXU_WIDTH:4 * MXU_WIDTH]
        st["tab_diff"] = tabs[:, 4 * MXU_WIDTH:6 * MXU_WIDTH]

        def run(lo, hi):
            wide = _dot(xb, win_ref[:, lo:hi])
            return lambda off, width: wide[:, off - lo:off - lo + width]

        run_m = run(PK_MQ, PK_DQ)
        run_a = run(PK_CQ, PK_FQ)
        run_f = run(PK_FQ, PK_MQ)
        run_d = run(PK_DQ, PK_END)
        st["direct"] = {"mq": run_m(PK_MQ, MXU_WIDTH), "mk": run_m(PK_MK, MXU_WIDTH),
                        "fq": run_f(PK_FQ, MXU_WIDTH), "fk": run_f(PK_FK, MXU_WIDTH),
                        "dq": run_d(PK_DQ, MXU_WIDTH), "dk": run_d(PK_DK, MXU_WIDTH),
                        "eq": run_d(PK_EQ, MXU_WIDTH)}
        st["values"] = {"fv": run_f(PK_FV, MXU_WIDTH).astype(bf), "mv": run_m(PK_MV, MXU_WIDTH).astype(bf),
                        "dv": run_d(PK_DV, MXU_WIDTH).astype(bf)}
        p_cq, p_ckv, st["p_kr"] = run_a(PK_CQ, MXU_WIDTH), run_a(PK_CKV, MLA_KV_RANK), run_a(PK_KR, 2 * MXU_WIDTH)
        cqn = p_cq * lax.rsqrt(jnp.sum(p_cq * p_cq, axis=-1, keepdims=True) * (1.0 / MLA_Q_RANK) + EPS) * prow(P_CQ)
        ckvn = p_ckv * lax.rsqrt(jnp.mean(p_ckv * p_ckv, axis=-1, keepdims=True) + EPS) * prow(P_CKV, MLA_KV_RANK)
        st["cqb"], st["ckvb"] = cqn.astype(bf), ckvn.astype(bf)
        z = run_f(PK_FF, LANES) + prow(P_FB, LANES)
        log_f = jnp.minimum(z, 0.0) - jnp.log1p(jnp.exp(-jnp.abs(z)))
        st["log_f"] = _split3(jnp.where(_lane_mask(log_f.shape, 0, N_HEADS), log_f, 0.0))

    def second(st):
        gmat = {"fq": g64, "fk": g64, "mq": g64, "mk": g64, "dq": g32, "dk": g32, "eq": g64}
        gains = {"fq": P_FQ, "fk": P_FK, "mq": P_MQ, "mk": P_MK, "dq": P_DQ, "dk": P_DK, "eq": P_EQ}
        ms = {name: _group_mean_sq(a, gmat[name]) for name, a in st["direct"].items()}
        st["qa"] = _dot(st["cqb"], wuq_ref[...])
        st["ka"] = st["p_kr"] + _dot(st["ckvb"], wukvk_ref[...])
        st["values"]["av"] = _dot(st["ckvb"], wukvv_ref[...]).astype(bf)
        l1, l2, l3 = st["log_f"]
        st["cumsum"] = (_dot(tril, l1) + _dot(tril, l2)) + _dot(tril, l3)
        st["normed"] = {name: a * lax.rsqrt(ms[name] + EPS) * prow(gains[name]) for name, a in st["direct"].items()}

    def third(st):
        rows = st["rows"]
        mq = _rope(st["normed"]["mq"], st["tab_moba"], ROT_MOBA // 2)
        mk = _rope(st["normed"]["mk"], st["tab_moba"], ROT_MOBA // 2)
        st["mq"], st["mk"] = mq, mk
        kmean_s[pl.ds(st["blk"], 1), :] = jnp.mean(mk, axis=0, keepdims=True)
        km_hi, km_lo = _split2(kmean_s[...])
        st["gates"] = []
        for h in range(N_HEADS):
            q_hi, q_lo = _split2(jnp.where(_lane_mask(mq.shape, h * HEAD_DIM, (h + 1) * HEAD_DIM), mq, 0.0))
            st["gates"].append((_dot_nt(km_hi, q_hi) + _dot_nt(km_lo, q_hi)) + _dot_nt(km_hi, q_lo))
        st["ms_qa"] = [_group_mean_sq(st["qa"][:, p * MXU_WIDTH:(p + 1) * MXU_WIDTH], gpair) for p in range(2)]
        st["ms_ka"] = [_group_mean_sq(st["ka"][:, p * MXU_WIDTH:(p + 1) * MXU_WIDTH], gpair) for p in range(2)]
        dec = carry_s[...] + st["cumsum"]
        carry_s[...] = dec[pr - 1:pr, :]
        dec2 = dec * LOG2E
        d1, d2, d3 = _split3(dec2)
        for h in range(N_HEADS):
            fdcol_ref[0, rows, h * LANES:(h + 1) * LANES] = jnp.broadcast_to(dec2[:, h:h + 1], (pr, LANES))
        row_sel = jnp.where(lax.broadcasted_iota(jnp.int32, (SUBLANES, LANES), 0)
                            == lax.broadcasted_iota(jnp.int32, (SUBLANES, LANES), 1), 1.0, 0.0).astype(bf)
        fdrow_ref[0, :, rows] = (_dot_nt(row_sel, d1) + _dot_nt(row_sel, d2)) + _dot_nt(row_sel, d3)
        for name, ref in (("fv", fv_ref), ("mv", mv_ref), ("dv", dv_ref), ("av", av_ref)):
            ref[0, :, rows] = _values_t(st["values"][name], vsel)

    def finish(st):
        rows = st["rows"]
        normed = st["normed"]
        fq_ref[0, rows, :] = normed["fq"].astype(bf)
        fk_ref[0, rows, :] = normed["fk"].astype(bf)
        eq_ref[0, rows, :] = normed["eq"].astype(bf)
        dq_ref[0, rows, :] = _rope(normed["dq"], st["tab_diff"], ROT_DIFF // 2).astype(bf)
        dk_ref[0, rows, :] = _rope(normed["dk"], st["tab_diff"], ROT_DIFF // 2).astype(bf)
        mq_ref[0, rows, :] = (st["mq"] * (LOG2E * HEAD_DIM ** -0.5)).astype(bf)
        mk_ref[0, rows, :] = st["mk"].astype(bf)
        for p in range(2):
            sl = slice(p * MXU_WIDTH, (p + 1) * MXU_WIDTH)
            aq_ref[0, rows, sl] = _rope(st["qa"][:, sl] * lax.rsqrt(st["ms_qa"][p] + EPS) * prow(P_GQ),
                                        st["tab_mla"], MLA_ROPE // 2).astype(bf)
            ak_ref[0, rows, sl] = _rope(st["ka"][:, sl] * lax.rsqrt(st["ms_ka"][p] + EPS) * prow(P_GK),
                                        st["tab_mla"], MLA_ROPE // 2).astype(bf)
        blk = lax.broadcasted_iota(jnp.int32, (nbp, pr), 0)
        past = blk < st["blk"]
        for h in range(N_HEADS):
            work = jnp.where(past, st["gates"][h], NEG_INF)
            sel = jnp.zeros((nbp, pr), jnp.bool_)
            for _ in range(MOBA_TOPK):
                mx = jnp.max(work, axis=0, keepdims=True)
                first = jnp.min(jnp.where(work == mx, blk, nbp), axis=0, keepdims=True)
                pick = blk == first
                sel = sel | pick
                work = jnp.where(pick, REMOVED, work)
            msel_ref[0, h * nbp:(h + 1) * nbp, rows] = jnp.where(sel & past, 0.0, NEG_INF)

    parts = [{"rows": slice(p * pr, (p + 1) * pr), "blk": j * n_parts + p} for p in range(n_parts)]
    for phase in (project, second, third, finish):
        for st in parts:
            phase(st)


def _const_spec(a, layer=None):
    if layer is None:
        n = a.ndim
        return pl.BlockSpec(a.shape, lambda *_: (0,) * n)
    n = a.ndim - 1
    return pl.BlockSpec((None,) + a.shape[1:], lambda *_: (layer,) + (0,) * n)


def _moba_blocks_padded(seq):
    return -(-(seq // MOBA_BLOCK) // SUBLANES) * SUBLANES


def _prep_call(layer, tm, x, tr, anorm, win, wuq, wukvk, wukvv, gpair, g64, g32, expand, tril, vsel, par):
    bsz, seq, d = x.shape
    assert seq % tm == 0 and tm % MOBA_BLOCK == 0 and tril.shape == (MOBA_BLOCK, MOBA_BLOCK)
    nbp = _moba_blocks_padded(seq)
    bf = jnp.bfloat16
    f32 = jnp.float32

    vt = -N_HEADS * V_ROWS
    widths = [(2 * MXU_WIDTH, bf), (2 * MXU_WIDTH, bf), (vt, bf),
              (GROUP_WIDTH, bf), (GROUP_WIDTH, bf), (vt, bf),
              (N_HEADS * LANES, f32), (-SUBLANES, f32),
              (GROUP_WIDTH, bf), (GROUP_WIDTH, bf), (vt, bf), (-N_HEADS * nbp, f32),
              (GROUP_WIDTH, bf), (GROUP_WIDTH, bf), (vt, bf),
              (GROUP_WIDTH, bf)]

    def tok(width):
        if width > 0:
            return pl.BlockSpec((1, tm, width), lambda b, j: (b, j, 0))
        return pl.BlockSpec((1, -width, tm), lambda b, j: (b, 0, j))

    def shape(width):
        return (bsz, seq, width) if width > 0 else (bsz, -width, seq)

    consts = [anorm, win, wuq, wukvk, wukvv, gpair, g64, g32, expand, tril, vsel, par]
    layered = [True, True, True, True, True, False, False, False, False, False, False, True]
    return pl.pallas_call(
        _prep_body,
        grid=(bsz, seq // tm),
        in_specs=[tok(d), tok(TR_WIDTH)] + [_const_spec(c, layer if ly else None) for c, ly in zip(consts, layered)],
        out_specs=[tok(w) for w, _ in widths],
        out_shape=[jax.ShapeDtypeStruct(shape(w), dt) for w, dt in widths],
        scratch_shapes=[pltpu.VMEM((nbp, GROUP_WIDTH), jnp.float32), pltpu.VMEM((1, LANES), jnp.float32)],
        compiler_params=pltpu.CompilerParams(dimension_semantics=("arbitrary", "arbitrary"),
                                             vmem_limit_bytes=VMEM_LIMIT_BYTES),
        name="prep",
    )(x, tr, *consts)


def _memkv_body(mem_ref, mnorm_ref, w_ref, g64_ref, gain_ref, vsel_ref, k_ref, v_ref):
    m = mem_ref[0]
    mn = m * lax.rsqrt(jnp.mean(m * m, axis=-1, keepdims=True) + EPS) * mnorm_ref[...]
    kv = _dot(mn.astype(jnp.bfloat16), w_ref[...])
    k = kv[:, 0:GROUP_WIDTH]
    k = k * lax.rsqrt(_group_mean_sq(k, g64_ref[...]) + EPS) * gain_ref[...]
    k_ref[0] = k.astype(jnp.bfloat16)
    v_ref[0] = _values_t(kv[:, GROUP_WIDTH:2 * GROUP_WIDTH].astype(jnp.bfloat16), vsel_ref[...])


def _memkv_call(layer, mem, mnorm, w, g64, gain, vsel):
    bsz, mlen, d = mem.shape
    k_shape, vt_shape = (bsz, mlen, GROUP_WIDTH), (bsz, N_HEADS * V_ROWS, mlen)
    return pl.pallas_call(
        _memkv_body,
        grid=(bsz,),
        in_specs=[pl.BlockSpec((1, mlen, d), lambda b: (b, 0, 0)), _const_spec(mnorm, layer), _const_spec(w, layer),
                  _const_spec(g64), _const_spec(gain, layer), _const_spec(vsel)],
        out_specs=[pl.BlockSpec((1,) + s[1:], lambda b: (b, 0, 0)) for s in (k_shape, vt_shape)],
        out_shape=[jax.ShapeDtypeStruct(s, jnp.bfloat16) for s in (k_shape, vt_shape)],
        compiler_params=pltpu.CompilerParams(dimension_semantics=("arbitrary",), vmem_limit_bytes=VMEM_LIMIT_BYTES),
        name="mem_kv",
    )(mem, mnorm, w, g64, gain, vsel)


Q_SUB = 2 * MXU_WIDTH


class _AttnCfg:
    def __init__(self, name, vheads, n_maps, causal, decay=False, select=False, diff=False, tile=1024, lookahead=8,
                 q_sub=Q_SUB):
        self.name = name
        self.vheads = vheads
        self.n_maps = n_maps
        self.causal = causal
        self.decay = decay
        self.select = select
        self.diff = diff
        self.tile = tile
        self.lookahead = lookahead
        self.q_sub = q_sub


_PLAIN_VHEADS = [(0, h * HEAD_DIM, (h + 1) * HEAD_DIM, 0, h) for h in range(N_HEADS)]
_CFG_MLA = _AttnCfg("attn_mla", [((h // 2) * MXU_WIDTH, (h % 2) * PAIR_STRIDE, (h % 2) * PAIR_STRIDE + MLA_QK, 0, h)
                                 for h in range(N_HEADS)], 1, True)
_CFG_FOX = _AttnCfg("attn_fox", _PLAIN_VHEADS, 1, True, decay=True, tile=2048, lookahead=4)
_CFG_MOBA = _AttnCfg("attn_moba", _PLAIN_VHEADS, 1, True, select=True, tile=2048, lookahead=6)
_CFG_DIFF = _AttnCfg("attn_diff", [(0, h * HEAD_DIM + c * DIFF_QK, h * HEAD_DIM + (c + 1) * DIFF_QK, c, h)
                                   for c in range(2) for h in range(N_HEADS)], 2, True, diff=True, q_sub=MXU_WIDTH)
_CFG_MEM = _AttnCfg("attn_mem", _PLAIN_VHEADS, 1, False, tile=2048)


ONES_ROW = HEAD_DIM
V_ROWS = HEAD_DIM + 16


def _np_value_select():
    sel = np.zeros((N_HEADS * V_ROWS, GROUP_WIDTH), np.float32)
    for h in range(N_HEADS):
        for d in range(HEAD_DIM):
            sel[h * V_ROWS + d, h * HEAD_DIM + d] = 1.0
    return sel


def _values_t(v, vsel):
    vt = _dot_nt(vsel, v)
    row = lax.broadcasted_iota(jnp.int32, vt.shape, 0)
    ones = row == ONES_ROW
    for h in range(1, N_HEADS):
        ones = ones | (row == h * V_ROWS + ONES_ROW)
    return jnp.where(ones, 1.0, vt).astype(jnp.bfloat16)


def _tile_lanes(x, width):
    return jnp.tile(x, (1, width // LANES)) if width != LANES else x


def _attn_body(cfg, qi_ref, kj_ref, *refs):
    refs = list(refs)
    q_ref, k_ref, vt_ref = refs[:3]
    pos = 3
    if cfg.decay:
        dq_ref, dk_ref = refs[pos:pos + 2]
        pos += 2
    if cfg.select:
        sel_ref = refs[pos]
        pos += 1
    if cfg.diff:
        g64_ref, gsub_ref, lam_ref = refs[pos:pos + 3]
        pos += 3
    o_ref, qm_s, m_s, acc_s = refs[pos:pos + 4]

    t = pl.program_id(1)
    i = qi_ref[t]
    j = kj_ref[t]
    tq = q_ref.shape[1]
    tk = k_ref.shape[1]

    @pl.when(j == 0)
    def _():
        for n, (off, lo, hi, _, _) in enumerate(cfg.vheads):
            qb = q_ref[0, :, off:off + MXU_WIDTH]
            qm_s[n] = jnp.where(_lane_mask(qb.shape, lo, hi), qb, jnp.zeros_like(qb))
        m_s[...] = jnp.full(m_s.shape, NEG_INF, jnp.float32)
        acc_s[...] = jnp.zeros_like(acc_s)

    def step(diag):
        qs = min(tq, cfg.q_sub)
        items = [(n, u) for n in range(len(cfg.vheads)) for u in range(tq // qs)]

        def n_keys(u):
            return (u + 1) * qs if diag else tk

        def scores(item):
            n, u = item
            off, _, _, _, h = cfg.vheads[n]
            nk = n_keys(u)
            cols = slice(u * qs, (u + 1) * qs)
            s = _dot_nt(k_ref[0, 0:nk, off:off + MXU_WIDTH], qm_s[n, cols, :])
            if cfg.decay:
                s = (dq_ref[0, h:h + 1, cols] - _tile_lanes(dk_ref[0, 0:nk, h * LANES:(h + 1) * LANES], qs)) + s
            if cfg.select:
                nbp = sel_ref.shape[1] // N_HEADS
                qpos = u * qs + lax.broadcasted_iota(jnp.int32, (1, qs), 1)
                parts = []
                for kb in range(nk // MOBA_BLOCK):
                    rows = s[kb * MOBA_BLOCK:(kb + 1) * MOBA_BLOCK, :]
                    if not (diag and kb == nk // MOBA_BLOCK - 1):
                        bias = sel_ref[0, pl.ds(h * nbp + j * (tk // MOBA_BLOCK) + kb, 1), cols]
                        if diag:
                            bias = jnp.where(qpos < (kb + 1) * MOBA_BLOCK, 0.0, bias)
                        rows = rows + bias
                    parts.append(rows)
                s = parts[0] if len(parts) == 1 else jnp.concatenate(parts, axis=0)
            if diag:
                key = lax.broadcasted_iota(jnp.int32, (nk, qs), 0)
                qry = u * qs + lax.broadcasted_iota(jnp.int32, (nk, qs), 1)
                s = jnp.where(key <= qry, s, NEG_INF)
            return s, jnp.max(s, axis=0, keepdims=True)

        raw = {it: scores(items[it]) for it in range(min(cfg.lookahead, len(items)))}
        for it, (n, u) in enumerate(items):
            h = cfg.vheads[n][4]
            nk = n_keys(u)
            cols = slice(u * qs, (u + 1) * qs)
            s, s_max = raw.pop(it)
            m_prev = m_s[n, :, cols]
            m_new = jnp.maximum(m_prev, s_max)
            alpha = jnp.exp2(m_prev - m_new)
            p = jnp.exp2(s - m_new)
            m_s[n, :, cols] = m_new
            acc_s[n, :, cols] = acc_s[n, :, cols] * alpha + _dot(vt_ref[0, h * V_ROWS:(h + 1) * V_ROWS, 0:nk],
                                                                 p.astype(jnp.bfloat16))
            if it + cfg.lookahead < len(items):
                raw[it + cfg.lookahead] = scores(items[it + cfg.lookahead])

    if cfg.causal:
        pl.when(j < i)(functools.partial(step, False))
        pl.when(j == i)(functools.partial(step, True))
        last = j == i
    else:
        step(False)
        last = j == 0

    @pl.when(last)
    def _():
        outs = []
        for c in range(cfg.n_maps):
            heads = []
            for h in range(N_HEADS):
                acc = acc_s[c * N_HEADS + h]
                heads.append(acc[0:HEAD_DIM, :] / acc[ONES_ROW:ONES_ROW + 1, :])
            outs.append(jnp.concatenate(heads, axis=0).T)
        if cfg.diff:
            o = outs[0] - lam_ref[0:1, :] * outs[1]
            o = o * lax.rsqrt(_group_mean_sq(o, g64_ref[...]) + EPS) * gsub_ref[...]
        else:
            o = outs[0]
        o_ref[0] = o.astype(o_ref.dtype)


def _attn_call(cfg, q, k, v, extras, tq, tk):
    bsz, seq, wq = q.shape
    sk = k.shape[1]
    nq = seq // tq
    assert seq % tq == 0 and sk % tk == 0
    if cfg.causal:
        assert tq == tk and sk == seq
        pairs = [(i, j) for i in range(nq) for j in range(i + 1)]
    else:
        assert sk == tk
        pairs = [(i, 0) for i in range(nq)]
    qi = jnp.asarray(np.array([p[0] for p in pairs], np.int32))
    kj = jnp.asarray(np.array([p[1] for p in pairs], np.int32))
    n_vh = len(cfg.vheads)

    in_specs = [pl.BlockSpec((1, tq, wq), lambda b, t, qi, kj: (b, qi[t], 0)),
                pl.BlockSpec((1, tk, wq), lambda b, t, qi, kj: (b, kj[t], 0)),
                pl.BlockSpec((1, N_HEADS * V_ROWS, tk), lambda b, t, qi, kj: (b, 0, kj[t]))]
    args = [q, k, v]
    if cfg.decay:
        dcol, drow = extras
        in_specs += [pl.BlockSpec((1, SUBLANES, tq), lambda b, t, qi, kj: (b, 0, qi[t])),
                     pl.BlockSpec((1, tk, N_HEADS * LANES), lambda b, t, qi, kj: (b, kj[t], 0))]
        args += [drow, dcol]
    if cfg.select:
        (sel,) = extras
        in_specs += [pl.BlockSpec((1, sel.shape[1], tq), lambda b, t, qi, kj: (b, 0, qi[t]))]
        args += [sel]
    if cfg.diff:
        layer, g64, gsub, lam_row = extras
        in_specs += [_const_spec(g64), _const_spec(gsub, layer), _const_spec(lam_row, layer)]
        args += [g64, gsub, lam_row]

    grid_spec = pltpu.PrefetchScalarGridSpec(
        num_scalar_prefetch=2,
        grid=(bsz, len(pairs)),
        in_specs=in_specs,
        out_specs=pl.BlockSpec((1, tq, GROUP_WIDTH), lambda b, t, qi, kj: (b, qi[t], 0)),
        scratch_shapes=[pltpu.VMEM((n_vh, tq, MXU_WIDTH), jnp.bfloat16),
                        pltpu.VMEM((n_vh, 1, tq), jnp.float32),
                        pltpu.VMEM((n_vh, V_ROWS, tq), jnp.float32)])
    return pl.pallas_call(
        functools.partial(_attn_body, cfg),
        grid_spec=grid_spec,
        out_shape=jax.ShapeDtypeStruct((bsz, seq, GROUP_WIDTH), jnp.bfloat16),
        compiler_params=pltpu.CompilerParams(dimension_semantics=("arbitrary", "arbitrary"),
                                             vmem_limit_bytes=VMEM_LIMIT_BYTES),
        name=cfg.name,
    )(qi, kj, *args)


def _ffn_body(nf, x_ref, oa_ref, ob_ref, oc_ref, od_ref, oe_ref, wo_ref, fnorm_ref, wg_ref, wu_ref, cw_ref, cb_ref,
              wd_ref, out_ref, xnew_s, xn_s, acc_s):
    i = pl.program_id(1)
    f = pl.program_id(2)
    tm = x_ref.shape[1]

    @pl.when(f == 0)
    def _():
        @pl.when(i == 0)
        def _():
            xn_s[0:TAIL_ROWS, :] = jnp.zeros((TAIL_ROWS, xn_s.shape[1]), xn_s.dtype)

        @pl.when(i > 0)
        def _():
            xn_s[0:TAIL_ROWS, :] = xn_s[tm:tm + TAIL_ROWS, :]

        mixed = jnp.concatenate([o_ref[0] for o_ref in (oa_ref, ob_ref, oc_ref, od_ref, oe_ref)], axis=1)
        xnew = x_ref[0] + _dot(mixed, wo_ref[...])
        xnew_s[...] = xnew
        xn = xnew * lax.rsqrt(jnp.mean(xnew * xnew, axis=-1, keepdims=True) + EPS) * fnorm_ref[...]
        xn_s[TAIL_ROWS:TAIL_ROWS + tm, :] = xn.astype(xn_s.dtype)

    def mlp_chunk():
        ge = _dot(xn_s[...], wg_ref[...])
        u = _dot(xn_s[TAIL_ROWS:TAIL_ROWS + tm, :], wu_ref[...])
        g0 = ge[TAIL_ROWS:TAIL_ROWS + tm, :]
        t1 = ge[TAIL_ROWS - 1:TAIL_ROWS, :]
        t2 = ge[TAIL_ROWS - 2:TAIL_ROWS - 1, :]
        row = lax.broadcasted_iota(jnp.int32, g0.shape, 0)
        g1 = jnp.where(row == 0, t1, pltpu.roll(g0, 1, 0))
        g2 = jnp.where(row == 0, t2, jnp.where(row == 1, t1, pltpu.roll(g0, 2, 0)))
        y = cb_ref[...] + cw_ref[0:1, :] * g2
        y = y + cw_ref[1:2, :] * g1
        y = y + cw_ref[2:3, :] * g0
        hmid = (y * (1.0 / (1.0 + jnp.exp(-y)))) * u
        return _dot(hmid.astype(jnp.bfloat16), wd_ref[...])

    @pl.when(f == 0)
    def _():
        acc_s[...] = mlp_chunk()

    if nf > 2:
        @pl.when((f > 0) & (f < nf - 1))
        def _():
            acc_s[...] += mlp_chunk()

    @pl.when(f == nf - 1)
    def _():
        out_ref[0] = xnew_s[...] + (acc_s[...] + mlp_chunk())


def _ffn_call(layer, x, outs, wo, fnorm, wg, wu, cw, cb, wd, tm, tf):
    bsz, seq, d = x.shape
    dff = wg.shape[2]
    n_tiles, nf = seq // tm, dff // tf
    assert seq % tm == 0 and dff % tf == 0 and nf >= 2
    tok = lambda w: pl.BlockSpec((1, tm, w), lambda b, i, f: (b, i, 0))
    return pl.pallas_call(
        functools.partial(_ffn_body, nf),
        grid=(bsz, n_tiles, nf),
        in_specs=[tok(d)] + [tok(GROUP_WIDTH)] * 5 + [
            _const_spec(wo, layer),
            _const_spec(fnorm, layer),
            pl.BlockSpec((None, d, tf), lambda b, i, f: (layer, 0, f)),
            pl.BlockSpec((None, d, tf), lambda b, i, f: (layer, 0, f)),
            pl.BlockSpec((None, SUBLANES, tf), lambda b, i, f: (layer, 0, f)),
            pl.BlockSpec((None, 1, tf), lambda b, i, f: (layer, 0, f)),
            pl.BlockSpec((None, tf, d), lambda b, i, f: (layer, f, 0))],
        out_specs=tok(d),
        out_shape=jax.ShapeDtypeStruct((bsz, seq, d), jnp.float32),
        scratch_shapes=[pltpu.VMEM((tm, d), jnp.float32), pltpu.VMEM((TAIL_ROWS + tm, d), jnp.bfloat16),
                        pltpu.VMEM((tm, d), jnp.float32)],
        compiler_params=pltpu.CompilerParams(dimension_semantics=("arbitrary", "arbitrary", "arbitrary"),
                                             vmem_limit_bytes=VMEM_LIMIT_BYTES),
        name="ffn",
    )(x, *outs, wo, fnorm, wg, wu, cw, cb, wd)


def _pad_rows(v, width=MXU_WIDTH):
    return jnp.pad(v.astype(jnp.float32), ((0, 0), (0, width - v.shape[1])))


def _tile_rows(g, reps):
    return jnp.tile(g.astype(jnp.float32), (1, reps))


def _pack_in_projection(w):
    idx = _np_in_index()
    pieces, start = [], 0
    while start < PK_END:
        stop = start + 1
        if idx[start] == _SRC_END:
            while stop < PK_END and idx[stop] == _SRC_END:
                stop += 1
            pieces.append(jnp.zeros(w.shape[:-1] + (stop - start,), w.dtype))
        else:
            while stop < PK_END and idx[stop] == idx[stop - 1] + 1:
                stop += 1
            pieces.append(w[..., int(idx[start]):int(idx[stop - 1]) + 1])
        start = stop
    return jnp.concatenate(pieces, axis=-1)


def _zero_col(w):
    return jnp.concatenate([w, jnp.zeros(w.shape[:-1] + (1,), w.dtype)], axis=-1)


def _rope_table(positions):
    pos = positions.astype(jnp.float32)[:, :, None]
    inv = [ROPE_THETA ** (-jnp.arange(0, rot, 2, dtype=jnp.float32) / rot) for rot in (MLA_ROPE, ROT_MOBA, ROT_DIFF)]
    inv = jnp.concatenate(inv + [jnp.zeros((N_FREQ - TR_ONE,), jnp.float32)])
    ang = pos * inv
    c, s = jnp.cos(ang), jnp.sin(ang)
    c_hi = c.astype(jnp.bfloat16)
    c_lo = (c - c_hi.astype(jnp.float32)).astype(jnp.bfloat16)
    s_hi = s.astype(jnp.bfloat16)
    s_lo = (s - s_hi.astype(jnp.float32)).astype(jnp.bfloat16)
    return jnp.concatenate([c_hi, c_lo, s_hi, s_lo], axis=-1)


def _pick_tile(n, pref):
    t = pref
    while n % t:
        t //= 2
    return t


def kernel(x, mem, positions, attn_norm, ffn_norm, mem_norm, w_in, mla_cq_norm, mla_ckv_norm, mla_w_uq, mla_w_ukv, mla_q_norm, mla_k_norm, fox_b_f, fox_q_norm, fox_k_norm, moba_q_norm, moba_k_norm, diff_lambda, diff_q_norm, diff_k_norm, diff_sub_norm, mem_w_kv, mem_q_norm, mem_k_norm, w_o, ffn_w_gate, ffn_w_up, ffn_conv_w, ffn_conv_b, ffn_w_down):
    bsz, seq, d = x.shape
    depth = w_in.shape[0]
    dff = ffn_w_gate.shape[2]
    bf = jnp.bfloat16
    f32 = jnp.float32

    uq_idx = _np_uq_index()
    ukvk_idx, ukvv_idx = _np_ukv_index()
    gpair = jnp.asarray(_np_group_matrix(_PAIR_GROUPS), bf)
    g64 = jnp.asarray(_np_group_matrix(_G64_GROUPS), bf)
    g32 = jnp.asarray(_np_group_matrix(_G32_GROUPS), bf)
    expand = jnp.asarray(_np_rope_expand_all(), bf)
    t_prep = max(_pick_tile(seq, PREP_TILE), MOBA_BLOCK)
    tril = jnp.asarray(np.tril(np.ones((MOBA_BLOCK, MOBA_BLOCK), np.float32)), bf)
    vsel = jnp.asarray(_np_value_select(), bf)
    tr = _rope_table(positions)

    tile = lambda cfg: max(_pick_tile(seq, cfg.tile), MOBA_BLOCK) if cfg.select else _pick_tile(seq, cfg.tile)
    t_ffn = _pick_tile(seq, 512)
    tf = dff // 2 if (dff // 2) % LANES == 0 else dff

    win = _pack_in_projection(w_in.astype(bf))
    wuq = jnp.take(_zero_col(mla_w_uq), uq_idx, axis=2)
    wuq = jnp.pad(wuq, ((0, 0), (0, MXU_WIDTH - MLA_Q_RANK), (0, 0))).astype(bf)
    wukv = _zero_col(mla_w_ukv)
    wukvk = jnp.take(wukv, ukvk_idx, axis=2).astype(bf)
    wukvv = jnp.take(wukv, ukvv_idx, axis=2).astype(bf)
    pair = lambda g: _pad_rows(_tile_rows(g, 2))
    rows = [jnp.zeros((depth, MXU_WIDTH), f32)] * P_ROWS
    rows[P_CQ] = _pad_rows(mla_cq_norm)
    rows[P_CKV] = _pad_rows(mla_ckv_norm)
    rows[P_GQ] = pair(mla_q_norm) * (LOG2E * MLA_QK ** -0.5)
    rows[P_GK] = pair(mla_k_norm)
    rows[P_FQ] = _tile_rows(fox_q_norm, N_HEADS) * (LOG2E * HEAD_DIM ** -0.5)
    rows[P_FK] = _tile_rows(fox_k_norm, N_HEADS)
    rows[P_FB] = _pad_rows(fox_b_f)
    rows[P_MQ] = _tile_rows(moba_q_norm, N_HEADS)
    rows[P_MK] = _tile_rows(moba_k_norm, N_HEADS)
    rows[P_DQ] = _tile_rows(diff_q_norm, 2 * N_HEADS) * (LOG2E * DIFF_QK ** -0.5)
    rows[P_DK] = _tile_rows(diff_k_norm, 2 * N_HEADS)
    rows[P_EQ] = _tile_rows(mem_q_norm, N_HEADS) * (LOG2E * HEAD_DIM ** -0.5)
    par = jnp.stack(rows, axis=1)
    anorm = attn_norm.astype(f32)[:, None, :]
    mnorm = mem_norm.astype(f32)[:, None, :]
    fnorm = ffn_norm.astype(f32)[:, None, :]
    wmem = mem_w_kv.astype(bf)
    mem_gain = _tile_rows(mem_k_norm, N_HEADS)[:, None, :]

    lam_init = jnp.asarray([0.8 - 0.6 * math.exp(-0.3 * l) for l in range(depth)], f32)
    lam_vec = diff_lambda.astype(f32)
    lam = (jnp.exp(jnp.sum(lam_vec[:, 0] * lam_vec[:, 1], axis=-1))
           - jnp.exp(jnp.sum(lam_vec[:, 2] * lam_vec[:, 3], axis=-1)) + lam_init)
    lam_row = jnp.broadcast_to(lam[:, None, None], (depth, 1, GROUP_WIDTH))
    gsub = (_tile_rows(diff_sub_norm, N_HEADS) * (1.0 - lam_init)[:, None])[:, None, :]

    wo = w_o.astype(bf)
    wg, wu, wd = ffn_w_gate.astype(bf), ffn_w_up.astype(bf), ffn_w_down.astype(bf)
    cw = jnp.pad(ffn_conv_w.astype(f32), ((0, 0), (0, SUBLANES - CONV_WIDTH), (0, 0)))
    cb = ffn_conv_b.astype(f32)[:, None, :]

    for l in range(depth):
        (aq, ak, av, fq, fk, fv, fdcol, fdrow, mq, mk, mv, msel, dq, dk, dv, eq) = _prep_call(
            l, t_prep, x, tr, anorm, win, wuq, wukvk, wukvv, gpair, g64, g32, expand, tril, vsel, par)
        ek, ev = _memkv_call(l, mem, mnorm, wmem, g64, mem_gain, vsel)

        o_a = _attn_call(_CFG_MLA, aq, ak, av, (), tile(_CFG_MLA), tile(_CFG_MLA))
        o_b = _attn_call(_CFG_FOX, fq, fk, fv, (fdcol, fdrow), tile(_CFG_FOX), tile(_CFG_FOX))
        o_c = _attn_call(_CFG_MOBA, mq, mk, mv, (msel,), tile(_CFG_MOBA), tile(_CFG_MOBA))
        o_d = _attn_call(_CFG_DIFF, dq, dk, dv, (l, g64, gsub, lam_row), tile(_CFG_DIFF), tile(_CFG_DIFF))
        o_e = _attn_call(_CFG_MEM, eq, ek, ev, (), tile(_CFG_MEM), mem.shape[1])

        x = _ffn_call(l, x, (o_a, o_b, o_c, o_d, o_e), wo, fnorm, wg, wu, cw, cb, wd, t_ffn, tf)
    return x
```

```python
import functools
import math

import numpy as np
import jax
import jax.numpy as jnp
from jax import lax
from jax.experimental import pallas as pl
from jax.experimental.pallas import tpu as pltpu

N_HEADS = 4
HEAD_DIM = 64
GROUP_WIDTH = N_HEADS * HEAD_DIM
MLA_Q_RANK = 192
MLA_KV_RANK = 128
MLA_NOPE = 64
MLA_ROPE = 32
MLA_QK = MLA_NOPE + MLA_ROPE
DIFF_QK = HEAD_DIM // 2
ROPE_THETA = 500000.0
ROT_MOBA = HEAD_DIM // 4
ROT_DIFF = DIFF_QK // 4
MOBA_BLOCK = 256
MOBA_TOPK = 3
CONV_WIDTH = 3
EPS = 1e-6
NEG_INF = -1e30
LOG2E = math.log2(math.e)
REMOVED = -3e38

LANES = 128
SUBLANES = 8
MXU_WIDTH = 256
TAIL_ROWS = 16
PREP_TILE = 512
VMEM_LIMIT_BYTES = 56 * 1024 * 1024

_SRC_CQ = 0
_SRC_CKV = _SRC_CQ + MLA_Q_RANK
_SRC_KR = _SRC_CKV + MLA_KV_RANK
_SRC_FOX = _SRC_KR + MLA_ROPE
_SRC_FOXF = _SRC_FOX + 3 * GROUP_WIDTH
_SRC_MOBA = _SRC_FOXF + N_HEADS
_SRC_DIFF = _SRC_MOBA + 3 * GROUP_WIDTH
_SRC_MEMQ = _SRC_DIFF + 3 * GROUP_WIDTH
_SRC_END = _SRC_MEMQ + GROUP_WIDTH

PK_CQ = 0
PK_CKV = 256
PK_KR = 384
PK_FQ, PK_FK, PK_FV = 896, 1152, 1408
PK_FF = 1664
PK_MQ, PK_MK, PK_MV = 1792, 2048, 2304
PK_DQ, PK_DK, PK_DV = 2560, 2816, 3072
PK_EQ = 3328
PK_END = 3584

PAIR_STRIDE = MLA_QK


def _pair_lane(h, d):
    return (h // 2) * MXU_WIDTH + (h % 2) * PAIR_STRIDE + d


N_FREQ = 32
FREQ_BASE_MLA = 0
FREQ_BASE_MOBA = MLA_ROPE // 2
FREQ_BASE_DIFF = FREQ_BASE_MOBA + ROT_MOBA // 2
TR_ONE = FREQ_BASE_DIFF + ROT_DIFF // 2
TR_WIDTH = 4 * N_FREQ
assert TR_ONE < N_FREQ and TR_WIDTH == LANES

(P_CQ, P_CKV, P_GQ, P_GK, P_FQ, P_FK, P_FB, P_MQ, P_MK, P_DQ, P_DK, P_EQ) = range(12)
P_ROWS = 16


def _np_in_index():
    idx = np.full((PK_END,), _SRC_END, np.int32)
    idx[PK_CQ:PK_CQ + MLA_Q_RANK] = np.arange(_SRC_CQ, _SRC_CQ + MLA_Q_RANK)
    idx[PK_CKV:PK_CKV + MLA_KV_RANK] = np.arange(_SRC_CKV, _SRC_CKV + MLA_KV_RANK)
    for h in range(N_HEADS):
        for d in range(MLA_ROPE):
            idx[PK_KR + _pair_lane(h, d)] = _SRC_KR + d
    idx[PK_FQ:PK_FQ + 3 * GROUP_WIDTH] = np.arange(_SRC_FOX, _SRC_FOX + 3 * GROUP_WIDTH)
    idx[PK_FF:PK_FF + N_HEADS] = np.arange(_SRC_FOXF, _SRC_FOXF + N_HEADS)
    idx[PK_MQ:PK_MQ + 3 * GROUP_WIDTH] = np.arange(_SRC_MOBA, _SRC_MOBA + 3 * GROUP_WIDTH)
    idx[PK_DQ:PK_DQ + 3 * GROUP_WIDTH] = np.arange(_SRC_DIFF, _SRC_DIFF + 3 * GROUP_WIDTH)
    idx[PK_EQ:PK_EQ + GROUP_WIDTH] = np.arange(_SRC_MEMQ, _SRC_MEMQ + GROUP_WIDTH)
    return idx


def _np_uq_index():
    idx = np.full((2 * MXU_WIDTH,), N_HEADS * MLA_QK, np.int32)
    for h in range(N_HEADS):
        for d in range(MLA_QK):
            idx[_pair_lane(h, d)] = h * MLA_QK + d
    return idx


def _np_ukv_index():
    zero = N_HEADS * (MLA_NOPE + HEAD_DIM)
    idx_k = np.full((2 * MXU_WIDTH,), zero, np.int32)
    idx_v = np.zeros((GROUP_WIDTH,), np.int32)
    for h in range(N_HEADS):
        for d in range(MLA_NOPE):
            idx_k[_pair_lane(h, MLA_ROPE + d)] = h * (MLA_NOPE + HEAD_DIM) + d
        for d in range(HEAD_DIM):
            idx_v[h * HEAD_DIM + d] = h * (MLA_NOPE + HEAD_DIM) + MLA_NOPE + d
    return idx_k, idx_v


def _np_group_matrix(groups):
    g = np.zeros((MXU_WIDTH, MXU_WIDTH), np.float32)
    for lo, size in groups:
        g[lo:lo + size, lo:lo + size] = 1.0 / size
    return g


_PAIR_GROUPS = [(0, MLA_ROPE), (MLA_ROPE, MLA_NOPE), (PAIR_STRIDE, MLA_ROPE), (PAIR_STRIDE + MLA_ROPE, MLA_NOPE)]
_G64_GROUPS = [(h * HEAD_DIM, HEAD_DIM) for h in range(N_HEADS)]
_G32_GROUPS = [(g * DIFF_QK, DIFF_QK) for g in range(2 * N_HEADS)]


def _np_rope_expand(regions, rot, base):
    half = rot // 2
    e = np.zeros((TR_WIDTH, 2 * MXU_WIDTH), np.float32)
    e[TR_ONE, 0:MXU_WIDTH] = 1.0
    for lo in regions:
        assert lo % rot == 0
        for r in range(half):
            f = base + r
            for lane, sign in ((lo + r, -1.0), (lo + half + r, 1.0)):
                e[TR_ONE, lane] = 0.0
                e[f, lane] = 1.0
                e[N_FREQ + f, lane] = 1.0
                e[2 * N_FREQ + f, MXU_WIDTH + lane] = sign
                e[3 * N_FREQ + f, MXU_WIDTH + lane] = sign
    return e


def _np_rope_expand_all():
    return np.concatenate([
        _np_rope_expand([0, PAIR_STRIDE], MLA_ROPE, FREQ_BASE_MLA),
        _np_rope_expand([h * HEAD_DIM for h in range(N_HEADS)], ROT_MOBA, FREQ_BASE_MOBA),
        _np_rope_expand([g * DIFF_QK for g in range(2 * N_HEADS)], ROT_DIFF, FREQ_BASE_DIFF),
    ], axis=1)


def _dot(a, b):
    return jnp.dot(a, b, preferred_element_type=jnp.float32)


def _dot_nt(a, b):
    return lax.dot_general(a, b, (((1,), (1,)), ((), ())), preferred_element_type=jnp.float32)


def _split2(a):
    hi = a.astype(jnp.bfloat16)
    lo = (a - hi.astype(jnp.float32)).astype(jnp.bfloat16)
    return hi, lo


def _split3(a):
    hi = a.astype(jnp.bfloat16)
    r = a - hi.astype(jnp.float32)
    mid = r.astype(jnp.bfloat16)
    lo = (r - mid.astype(jnp.float32)).astype(jnp.bfloat16)
    return hi, mid, lo


def _group_mean_sq(a, g_bf16):
    return _dot((a * a).astype(jnp.bfloat16), g_bf16)


def _rope(x, tabs, half):
    w = x.shape[-1]
    lane = lax.broadcasted_iota(jnp.int32, x.shape, 1)
    partner = jnp.where((lane & (2 * half - 1)) >= half, pltpu.roll(x, half, 1), pltpu.roll(x, w - half, 1))
    return x * tabs[:, 0:w] + partner * tabs[:, w:2 * w]


def _lane_mask(shape, lo, hi):
    lane = lax.broadcasted_iota(jnp.int32, shape, len(shape) - 1)
    return (lane >= lo) & (lane < hi)


def _prep_body(x_ref, tr_ref, anorm_ref, win_ref, wuq_ref, wukvk_ref, wukvv_ref, gpair_ref, g64_ref, g32_ref,
               exp_ref, tril_ref, vsel_ref, par_ref,
               aq_ref, ak_ref, av_ref, fq_ref, fk_ref, fv_ref, fdcol_ref, fdrow_ref, mq_ref, mk_ref, mv_ref, msel_ref,
               dq_ref, dk_ref, dv_ref, eq_ref,
               kmean_s, carry_s):
    j = pl.program_id(1)
    tm = x_ref.shape[1]
    pr = MOBA_BLOCK
    n_parts = tm // pr
    nbp = kmean_s.shape[0]
    bf = jnp.bfloat16
    gpair, g64, g32 = gpair_ref[...], g64_ref[...], g32_ref[...]
    vsel = vsel_ref[...]
    tril = tril_ref[...]

    @pl.when(j == 0)
    def _():
        kmean_s[...] = jnp.zeros_like(kmean_s)
        carry_s[...] = jnp.zeros_like(carry_s)

    def prow(r, width=MXU_WIDTH):
        return par_ref[r:r + 1, 0:width]

    def project(st):
        rows = st["rows"]
        x = x_ref[0, rows, :]
        xb = (x * lax.rsqrt(jnp.mean(x * x, axis=-1, keepdims=True) + EPS) * anorm_ref[...]).astype(bf)
        tabs = _dot(tr_ref[0, rows, :], exp_ref[...])
        st["tab_mla"] = tabs[:, 0:2 * MXU_WIDTH]
        st["tab_moba"] = tabs[:, 2 * MXU_WIDTH:4 * MXU_WIDTH]
        st["tab_diff"] = tabs[:, 4 * MXU_WIDTH:6 * MXU_WIDTH]

        def run(lo, hi):
            wide = _dot(xb, win_ref[:, lo:hi])
            return lambda off, width: wide[:, off - lo:off - lo + width]

        run_m = run(PK_MQ, PK_DQ)
        run_a = run(PK_CQ, PK_FQ)
        run_f = run(PK_FQ, PK_MQ)
        run_d = run(PK_DQ, PK_END)
        st["direct"] = {"mq": run_m(PK_MQ, MXU_WIDTH), "mk": run_m(PK_MK, MXU_WIDTH),
                        "fq": run_f(PK_FQ, MXU_WIDTH), "fk": run_f(PK_FK, MXU_WIDTH),
                        "dq": run_d(PK_DQ, MXU_WIDTH), "dk": run_d(PK_DK, MXU_WIDTH),
                        "eq": run_d(PK_EQ, MXU_WIDTH)}
        st["values"] = {"fv": run_f(PK_FV, MXU_WIDTH).astype(bf), "mv": run_m(PK_MV, MXU_WIDTH).astype(bf),
                        "dv": run_d(PK_DV, MXU_WIDTH).astype(bf)}
        p_cq, p_ckv, st["p_kr"] = run_a(PK_CQ, MXU_WIDTH), run_a(PK_CKV, MLA_KV_RANK), run_a(PK_KR, 2 * MXU_WIDTH)
        cqn = p_cq * lax.rsqrt(jnp.sum(p_cq * p_cq, axis=-1, keepdims=True) * (1.0 / MLA_Q_RANK) + EPS) * prow(P_CQ)
        ckvn = p_ckv * lax.rsqrt(jnp.mean(p_ckv * p_ckv, axis=-1, keepdims=True) + EPS) * prow(P_CKV, MLA_KV_RANK)
        st["cqb"], st["ckvb"] = cqn.astype(bf), ckvn.astype(bf)
        z = run_f(PK_FF, LANES) + prow(P_FB, LANES)
        log_f = jnp.minimum(z, 0.0) - jnp.log1p(jnp.exp(-jnp.abs(z)))
        st["log_f"] = _split3(jnp.where(_lane_mask(log_f.shape, 0, N_HEADS), log_f, 0.0))

    def second(st):
        gmat = {"fq": g64, "fk": g64, "mq": g64, "mk": g64, "dq": g32, "dk": g32, "eq": g64}
        gains = {"fq": P_FQ, "fk": P_FK, "mq": P_MQ, "mk": P_MK, "dq": P_DQ, "dk": P_DK, "eq": P_EQ}
        ms = {name: _group_mean_sq(a, gmat[name]) for name, a in st["direct"].items()}
        st["qa"] = _dot(st["cqb"], wuq_ref[...])
        st["ka"] = st["p_kr"] + _dot(st["ckvb"], wukvk_ref[...])
        st["values"]["av"] = _dot(st["ckvb"], wukvv_ref[...]).astype(bf)
        l1, l2, l3 = st["log_f"]
        st["cumsum"] = (_dot(tril, l1) + _dot(tril, l2)) + _dot(tril, l3)
        st["normed"] = {name: a * lax.rsqrt(ms[name] + EPS) * prow(gains[name]) for name, a in st["direct"].items()}

    def third(st):
        rows = st["rows"]
        mq = _rope(st["normed"]["mq"], st["tab_moba"], ROT_MOBA // 2)
        mk = _rope(st["normed"]["mk"], st["tab_moba"], ROT_MOBA // 2)
        st["mq"], st["mk"] = mq, mk
        kmean_s[pl.ds(st["blk"], 1), :] = jnp.mean(mk, axis=0, keepdims=True)
        km_hi, km_lo = _split2(kmean_s[...])
        st["gates"] = []
        for h in range(N_HEADS):
            q_hi, q_lo = _split2(jnp.where(_lane_mask(mq.shape, h * HEAD_DIM, (h + 1) * HEAD_DIM), mq, 0.0))
            st["gates"].append((_dot_nt(km_hi, q_hi) + _dot_nt(km_lo, q_hi)) + _dot_nt(km_hi, q_lo))
        st["ms_qa"] = [_group_mean_sq(st["qa"][:, p * MXU_WIDTH:(p + 1) * MXU_WIDTH], gpair) for p in range(2)]
        st["ms_ka"] = [_group_mean_sq(st["ka"][:, p * MXU_WIDTH:(p + 1) * MXU_WIDTH], gpair) for p in range(2)]
        dec = carry_s[...] + st["cumsum"]
        carry_s[...] = dec[pr - 1:pr, :]
        dec2 = dec * LOG2E
        d1, d2, d3 = _split3(dec2)
        for h in range(N_HEADS):
            fdcol_ref[0, rows, h * LANES:(h + 1) * LANES] = jnp.broadcast_to(dec2[:, h:h + 1], (pr, LANES))
        row_sel = jnp.where(lax.broadcasted_iota(jnp.int32, (SUBLANES, LANES), 0)
                            == lax.broadcasted_iota(jnp.int32, (SUBLANES, LANES), 1), 1.0, 0.0).astype(bf)
        fdrow_ref[0, :, rows] = (_dot_nt(row_sel, d1) + _dot_nt(row_sel, d2)) + _dot_nt(row_sel, d3)
        for name, ref in (("fv", fv_ref), ("mv", mv_ref), ("dv", dv_ref), ("av", av_ref)):
            ref[0, :, rows] = _values_t(st["values"][name], vsel)

    def finish(st):
        rows = st["rows"]
        normed = st["normed"]
        fq_ref[0, rows, :] = normed["fq"].astype(bf)
        fk_ref[0, rows, :] = normed["fk"].astype(bf)
        eq_ref[0, rows, :] = normed["eq"].astype(bf)
        dq_ref[0, rows, :] = _rope(normed["dq"], st["tab_diff"], ROT_DIFF // 2).astype(bf)
        dk_ref[0, rows, :] = _rope(normed["dk"], st["tab_diff"], ROT_DIFF // 2).astype(bf)
        mq_ref[0, rows, :] = (st["mq"] * (LOG2E * HEAD_DIM ** -0.5)).astype(bf)
        mk_ref[0, rows, :] = st["mk"].astype(bf)
        for p in range(2):
            sl = slice(p * MXU_WIDTH, (p + 1) * MXU_WIDTH)
            aq_ref[0, rows, sl] = _rope(st["qa"][:, sl] * lax.rsqrt(st["ms_qa"][p] + EPS) * prow(P_GQ),
                                        st["tab_mla"], MLA_ROPE // 2).astype(bf)
            ak_ref[0, rows, sl] = _rope(st["ka"][:, sl] * lax.rsqrt(st["ms_ka"][p] + EPS) * prow(P_GK),
                                        st["tab_mla"], MLA_ROPE // 2).astype(bf)
        blk = lax.broadcasted_iota(jnp.int32, (nbp, pr), 0)
        past = blk < st["blk"]
        for h in range(N_HEADS):
            work = jnp.where(past, st["gates"][h], NEG_INF)
            sel = jnp.zeros((nbp, pr), jnp.bool_)
            for _ in range(MOBA_TOPK):
                mx = jnp.max(work, axis=0, keepdims=True)
                first = jnp.min(jnp.where(work == mx, blk, nbp), axis=0, keepdims=True)
                pick = blk == first
                sel = sel | pick
                work = jnp.where(pick, REMOVED, work)
            msel_ref[0, h * nbp:(h + 1) * nbp, rows] = jnp.where(sel & past, 0.0, NEG_INF)

    parts = [{"rows": slice(p * pr, (p + 1) * pr), "blk": j * n_parts + p} for p in range(n_parts)]
    for phase in (project, second, third, finish):
        for st in parts:
            phase(st)


def _const_spec(a, layer=None):
    if layer is None:
        n = a.ndim
        return pl.BlockSpec(a.shape, lambda *_: (0,) * n)
    n = a.ndim - 1
    return pl.BlockSpec((None,) + a.shape[1:], lambda *_: (layer,) + (0,) * n)


def _moba_blocks_padded(seq):
    return -(-(seq // MOBA_BLOCK) // SUBLANES) * SUBLANES


def _prep_call(layer, tm, x, tr, anorm, win, wuq, wukvk, wukvv, gpair, g64, g32, expand, tril, vsel, par):
    bsz, seq, d = x.shape
    assert seq % tm == 0 and tm % MOBA_BLOCK == 0 and tril.shape == (MOBA_BLOCK, MOBA_BLOCK)
    nbp = _moba_blocks_padded(seq)
    bf = jnp.bfloat16
    f32 = jnp.float32

    vt = -N_HEADS * V_ROWS
    widths = [(2 * MXU_WIDTH, bf), (2 * MXU_WIDTH, bf), (vt, bf),
              (GROUP_WIDTH, bf), (GROUP_WIDTH, bf), (vt, bf),
              (N_HEADS * LANES, f32), (-SUBLANES, f32),
              (GROUP_WIDTH, bf), (GROUP_WIDTH, bf), (vt, bf), (-N_HEADS * nbp, f32),
              (GROUP_WIDTH, bf), (GROUP_WIDTH, bf), (vt, bf),
              (GROUP_WIDTH, bf)]

    def tok(width):
        if width > 0:
            return pl.BlockSpec((1, tm, width), lambda b, j: (b, j, 0))
        return pl.BlockSpec((1, -width, tm), lambda b, j: (b, 0, j))

    def shape(width):
        return (bsz, seq, width) if width > 0 else (bsz, -width, seq)

    consts = [anorm, win, wuq, wukvk, wukvv, gpair, g64, g32, expand, tril, vsel, par]
    layered = [True, True, True, True, True, False, False, False, False, False, False, True]
    return pl.pallas_call(
        _prep_body,
        grid=(bsz, seq // tm),
        in_specs=[tok(d), tok(TR_WIDTH)] + [_const_spec(c, layer if ly else None) for c, ly in zip(consts, layered)],
        out_specs=[tok(w) for w, _ in widths],
        out_shape=[jax.ShapeDtypeStruct(shape(w), dt) for w, dt in widths],
        scratch_shapes=[pltpu.VMEM((nbp, GROUP_WIDTH), jnp.float32), pltpu.VMEM((1, LANES), jnp.float32)],
        compiler_params=pltpu.CompilerParams(dimension_semantics=("arbitrary", "arbitrary"),
                                             vmem_limit_bytes=VMEM_LIMIT_BYTES),
        name="prep",
    )(x, tr, *consts)


def _memkv_body(mem_ref, mnorm_ref, w_ref, g64_ref, gain_ref, vsel_ref, k_ref, v_ref):
    m = mem_ref[0]
    mn = m * lax.rsqrt(jnp.mean(m * m, axis=-1, keepdims=True) + EPS) * mnorm_ref[...]
    kv = _dot(mn.astype(jnp.bfloat16), w_ref[...])
    k = kv[:, 0:GROUP_WIDTH]
    k = k * lax.rsqrt(_group_mean_sq(k, g64_ref[...]) + EPS) * gain_ref[...]
    k_ref[0] = k.astype(jnp.bfloat16)
    v_ref[0] = _values_t(kv[:, GROUP_WIDTH:2 * GROUP_WIDTH].astype(jnp.bfloat16), vsel_ref[...])


def _memkv_call(layer, mem, mnorm, w, g64, gain, vsel):
    bsz, mlen, d = mem.shape
    k_shape, vt_shape = (bsz, mlen, GROUP_WIDTH), (bsz, N_HEADS * V_ROWS, mlen)
    return pl.pallas_call(
        _memkv_body,
        grid=(bsz,),
        in_specs=[pl.BlockSpec((1, mlen, d), lambda b: (b, 0, 0)), _const_spec(mnorm, layer), _const_spec(w, layer),
                  _const_spec(g64), _const_spec(gain, layer), _const_spec(vsel)],
        out_specs=[pl.BlockSpec((1,) + s[1:], lambda b: (b, 0, 0)) for s in (k_shape, vt_shape)],
        out_shape=[jax.ShapeDtypeStruct(s, jnp.bfloat16) for s in (k_shape, vt_shape)],
        compiler_params=pltpu.CompilerParams(dimension_semantics=("arbitrary",), vmem_limit_bytes=VMEM_LIMIT_BYTES),
        name="mem_kv",
    )(mem, mnorm, w, g64, gain, vsel)


Q_SUB = 2 * MXU_WIDTH


class _AttnCfg:
    def __init__(self, name, vheads, n_maps, causal, decay=False, select=False, diff=False, tile=1024, lookahead=8,
                 q_sub=Q_SUB):
        self.name = name
        self.vheads = vheads
        self.n_maps = n_maps
        self.causal = causal
        self.decay = decay
        self.select = select
        self.diff = diff
        self.tile = tile
        self.lookahead = lookahead
        self.q_sub = q_sub


_PLAIN_VHEADS = [(0, h * HEAD_DIM, (h + 1) * HEAD_DIM, 0, h) for h in range(N_HEADS)]
_CFG_MLA = _AttnCfg("attn_mla", [((h // 2) * MXU_WIDTH, (h % 2) * PAIR_STRIDE, (h % 2) * PAIR_STRIDE + MLA_QK, 0, h)
                                 for h in range(N_HEADS)], 1, True)
_CFG_FOX = _AttnCfg("attn_fox", _PLAIN_VHEADS, 1, True, decay=True, tile=2048, lookahead=4)
_CFG_MOBA = _AttnCfg("attn_moba", _PLAIN_VHEADS, 1, True, select=True, tile=2048, lookahead=6)
_CFG_DIFF = _AttnCfg("attn_diff", [(0, h * HEAD_DIM + c * DIFF_QK, h * HEAD_DIM + (c + 1) * DIFF_QK, c, h)
                                   for c in range(2) for h in range(N_HEADS)], 2, True, diff=True, q_sub=MXU_WIDTH,
                     lookahead=16)
_CFG_MEM = _AttnCfg("attn_mem", _PLAIN_VHEADS, 1, False, tile=2048)


ONES_ROW = HEAD_DIM
V_ROWS = HEAD_DIM + 16


def _np_value_select():
    sel = np.zeros((N_HEADS * V_ROWS, GROUP_WIDTH), np.float32)
    for h in range(N_HEADS):
        for d in range(HEAD_DIM):
            sel[h * V_ROWS + d, h * HEAD_DIM + d] = 1.0
    return sel


def _values_t(v, vsel):
    vt = _dot_nt(vsel, v)
    row = lax.broadcasted_iota(jnp.int32, vt.shape, 0)
    ones = row == ONES_ROW
    for h in range(1, N_HEADS):
        ones = ones | (row == h * V_ROWS + ONES_ROW)
    return jnp.where(ones, 1.0, vt).astype(jnp.bfloat16)


def _tile_lanes(x, width):
    return jnp.tile(x, (1, width // LANES)) if width != LANES else x


def _attn_body(cfg, qi_ref, kj_ref, *refs):
    refs = list(refs)
    q_ref, k_ref, vt_ref = refs[:3]
    pos = 3
    if cfg.decay:
        dq_ref, dk_ref = refs[pos:pos + 2]
        pos += 2
    if cfg.select:
        sel_ref = refs[pos]
        pos += 1
    if cfg.diff:
        g64_ref, gsub_ref, lam_ref = refs[pos:pos + 3]
        pos += 3
    o_ref, qm_s, m_s, acc_s = refs[pos:pos + 4]

    t = pl.program_id(1)
    i = qi_ref[t]
    j = kj_ref[t]
    tq = q_ref.shape[1]
    tk = k_ref.shape[1]

    @pl.when(j == 0)
    def _():
        for n, (off, lo, hi, _, _) in enumerate(cfg.vheads):
            qb = q_ref[0, :, off:off + MXU_WIDTH]
            qm_s[n] = jnp.where(_lane_mask(qb.shape, lo, hi), qb, jnp.zeros_like(qb))
        m_s[...] = jnp.full(m_s.shape, NEG_INF, jnp.float32)
        acc_s[...] = jnp.zeros_like(acc_s)

    def step(diag):
        qs = min(tq, cfg.q_sub)
        items = [(n, u) for n in range(len(cfg.vheads)) for u in range(tq // qs)]

        def n_keys(u):
            return (u + 1) * qs if diag else tk

        def scores(item):
            n, u = item
            off, _, _, _, h = cfg.vheads[n]
            nk = n_keys(u)
            cols = slice(u * qs, (u + 1) * qs)
            s = _dot_nt(k_ref[0, 0:nk, off:off + MXU_WIDTH], qm_s[n, cols, :])
            if cfg.decay:
                s = (dq_ref[0, h:h + 1, cols] - _tile_lanes(dk_ref[0, 0:nk, h * LANES:(h + 1) * LANES], qs)) + s
            if cfg.select:
                nbp = sel_ref.shape[1] // N_HEADS
                qpos = u * qs + lax.broadcasted_iota(jnp.int32, (1, qs), 1)
                parts = []
                for kb in range(nk // MOBA_BLOCK):
                    rows = s[kb * MOBA_BLOCK:(kb + 1) * MOBA_BLOCK, :]
                    if not (diag and kb == nk // MOBA_BLOCK - 1):
                        bias = sel_ref[0, pl.ds(h * nbp + j * (tk // MOBA_BLOCK) + kb, 1), cols]
                        if diag:
                            bias = jnp.where(qpos < (kb + 1) * MOBA_BLOCK, 0.0, bias)
                        rows = rows + bias
                    parts.append(rows)
                s = parts[0] if len(parts) == 1 else jnp.concatenate(parts, axis=0)
            if diag:
                key = lax.broadcasted_iota(jnp.int32, (nk, qs), 0)
                qry = u * qs + lax.broadcasted_iota(jnp.int32, (nk, qs), 1)
                s = jnp.where(key <= qry, s, NEG_INF)
            return s, jnp.max(s, axis=0, keepdims=True)

        raw = {it: scores(items[it]) for it in range(min(cfg.lookahead, len(items)))}
        for it, (n, u) in enumerate(items):
            h = cfg.vheads[n][4]
            nk = n_keys(u)
            cols = slice(u * qs, (u + 1) * qs)
            s, s_max = raw.pop(it)
            m_prev = m_s[n, :, cols]
            m_new = jnp.maximum(m_prev, s_max)
            alpha = jnp.exp2(m_prev - m_new)
            p = jnp.exp2(s - m_new)
            m_s[n, :, cols] = m_new
            acc_s[n, :, cols] = acc_s[n, :, cols] * alpha + _dot(vt_ref[0, h * V_ROWS:(h + 1) * V_ROWS, 0:nk],
                                                                 p.astype(jnp.bfloat16))
            if it + cfg.lookahead < len(items):
                raw[it + cfg.lookahead] = scores(items[it + cfg.lookahead])

    if cfg.causal:
        pl.when(j < i)(functools.partial(step, False))
        pl.when(j == i)(functools.partial(step, True))
        last = j == i
    else:
        step(False)
        last = j == 0

    @pl.when(last)
    def _():
        outs = []
        for c in range(cfg.n_maps):
            heads = []
            for h in range(N_HEADS):
                acc = acc_s[c * N_HEADS + h]
                heads.append(acc[0:HEAD_DIM, :] / acc[ONES_ROW:ONES_ROW + 1, :])
            outs.append(jnp.concatenate(heads, axis=0).T)
        if cfg.diff:
            o = outs[0] - lam_ref[0:1, :] * outs[1]
            o = o * lax.rsqrt(_group_mean_sq(o, g64_ref[...]) + EPS) * gsub_ref[...]
        else:
            o = outs[0]
        o_ref[0] = o.astype(o_ref.dtype)


def _attn_call(cfg, q, k, v, extras, tq, tk):
    bsz, seq, wq = q.shape
    sk = k.shape[1]
    nq = seq // tq
    assert seq % tq == 0 and sk % tk == 0
    if cfg.causal:
        assert tq == tk and sk == seq
        pairs = [(i, j) for i in range(nq) for j in range(i + 1)]
    else:
        assert sk == tk
        pairs = [(i, 0) for i in range(nq)]
    qi = jnp.asarray(np.array([p[0] for p in pairs], np.int32))
    kj = jnp.asarray(np.array([p[1] for p in pairs], np.int32))
    n_vh = len(cfg.vheads)

    in_specs = [pl.BlockSpec((1, tq, wq), lambda b, t, qi, kj: (b, qi[t], 0)),
                pl.BlockSpec((1, tk, wq), lambda b, t, qi, kj: (b, kj[t], 0)),
                pl.BlockSpec((1, N_HEADS * V_ROWS, tk), lambda b, t, qi, kj: (b, 0, kj[t]))]
    args = [q, k, v]
    if cfg.decay:
        dcol, drow = extras
        in_specs += [pl.BlockSpec((1, SUBLANES, tq), lambda b, t, qi, kj: (b, 0, qi[t])),
                     pl.BlockSpec((1, tk, N_HEADS * LANES), lambda b, t, qi, kj: (b, kj[t], 0))]
        args += [drow, dcol]
    if cfg.select:
        (sel,) = extras
        in_specs += [pl.BlockSpec((1, sel.shape[1], tq), lambda b, t, qi, kj: (b, 0, qi[t]))]
        args += [sel]
    if cfg.diff:
        layer, g64, gsub, lam_row = extras
        in_specs += [_const_spec(g64), _const_spec(gsub, layer), _const_spec(lam_row, layer)]
        args += [g64, gsub, lam_row]

    grid_spec = pltpu.PrefetchScalarGridSpec(
        num_scalar_prefetch=2,
        grid=(bsz, len(pairs)),
        in_specs=in_specs,
        out_specs=pl.BlockSpec((1, tq, GROUP_WIDTH), lambda b, t, qi, kj: (b, qi[t], 0)),
        scratch_shapes=[pltpu.VMEM((n_vh, tq, MXU_WIDTH), jnp.bfloat16),
                        pltpu.VMEM((n_vh, 1, tq), jnp.float32),
                        pltpu.VMEM((n_vh, V_ROWS, tq), jnp.float32)])
    return pl.pallas_call(
        functools.partial(_attn_body, cfg),
        grid_spec=grid_spec,
        out_shape=jax.ShapeDtypeStruct((bsz, seq, GROUP_WIDTH), jnp.bfloat16),
        compiler_params=pltpu.CompilerParams(dimension_semantics=("arbitrary", "arbitrary"),
                                             vmem_limit_bytes=VMEM_LIMIT_BYTES),
        name=cfg.name,
    )(qi, kj, *args)


def _ffn_body(nf, x_ref, oa_ref, ob_ref, oc_ref, od_ref, oe_ref, wo_ref, fnorm_ref, wg_ref, wu_ref, cw_ref, cb_ref,
              wd_ref, out_ref, xnew_s, xn_s, acc_s):
    i = pl.program_id(1)
    f = pl.program_id(2)
    tm = x_ref.shape[1]

    @pl.when(f == 0)
    def _():
        @pl.when(i == 0)
        def _():
            xn_s[0:TAIL_ROWS, :] = jnp.zeros((TAIL_ROWS, xn_s.shape[1]), xn_s.dtype)

        @pl.when(i > 0)
        def _():
            xn_s[0:TAIL_ROWS, :] = xn_s[tm:tm + TAIL_ROWS, :]

        mixed = jnp.concatenate([o_ref[0] for o_ref in (oa_ref, ob_ref, oc_ref, od_ref, oe_ref)], axis=1)
        xnew = x_ref[0] + _dot(mixed, wo_ref[...])
        xnew_s[...] = xnew
        xn = xnew * lax.rsqrt(jnp.mean(xnew * xnew, axis=-1, keepdims=True) + EPS) * fnorm_ref[...]
        xn_s[TAIL_ROWS:TAIL_ROWS + tm, :] = xn.astype(xn_s.dtype)

    def mlp_chunk():
        ge = _dot(xn_s[...], wg_ref[...])
        u = _dot(xn_s[TAIL_ROWS:TAIL_ROWS + tm, :], wu_ref[...])
        g0 = ge[TAIL_ROWS:TAIL_ROWS + tm, :]
        t1 = ge[TAIL_ROWS - 1:TAIL_ROWS, :]
        t2 = ge[TAIL_ROWS - 2:TAIL_ROWS - 1, :]
        row = lax.broadcasted_iota(jnp.int32, g0.shape, 0)
        g1 = jnp.where(row == 0, t1, pltpu.roll(g0, 1, 0))
        g2 = jnp.where(row == 0, t2, jnp.where(row == 1, t1, pltpu.roll(g0, 2, 0)))
        y = cb_ref[...] + cw_ref[0:1, :] * g2
        y = y + cw_ref[1:2, :] * g1
        y = y + cw_ref[2:3, :] * g0
        hmid = (y * (1.0 / (1.0 + jnp.exp(-y)))) * u
        return _dot(hmid.astype(jnp.bfloat16), wd_ref[...])

    @pl.when(f == 0)
    def _():
        acc_s[...] = mlp_chunk()

    if nf > 2:
        @pl.when((f > 0) & (f < nf - 1))
        def _():
            acc_s[...] += mlp_chunk()

    @pl.when(f == nf - 1)
    def _():
        out_ref[0] = xnew_s[...] + (acc_s[...] + mlp_chunk())


def _ffn_call(layer, x, outs, wo, fnorm, wg, wu, cw, cb, wd, tm, tf):
    bsz, seq, d = x.shape
    dff = wg.shape[2]
    n_tiles, nf = seq // tm, dff // tf
    assert seq % tm == 0 and dff % tf == 0 and nf >= 2
    tok = lambda w: pl.BlockSpec((1, tm, w), lambda b, i, f: (b, i, 0))
    return pl.pallas_call(
        functools.partial(_ffn_body, nf),
        grid=(bsz, n_tiles, nf),
        in_specs=[tok(d)] + [tok(GROUP_WIDTH)] * 5 + [
            _const_spec(wo, layer),
            _const_spec(fnorm, layer),
            pl.BlockSpec((None, d, tf), lambda b, i, f: (layer, 0, f)),
            pl.BlockSpec((None, d, tf), lambda b, i, f: (layer, 0, f)),
            pl.BlockSpec((None, SUBLANES, tf), lambda b, i, f: (layer, 0, f)),
            pl.BlockSpec((None, 1, tf), lambda b, i, f: (layer, 0, f)),
            pl.BlockSpec((None, tf, d), lambda b, i, f: (layer, f, 0))],
        out_specs=tok(d),
        out_shape=jax.ShapeDtypeStruct((bsz, seq, d), jnp.float32),
        scratch_shapes=[pltpu.VMEM((tm, d), jnp.float32), pltpu.VMEM((TAIL_ROWS + tm, d), jnp.bfloat16),
                        pltpu.VMEM((tm, d), jnp.float32)],
        compiler_params=pltpu.CompilerParams(dimension_semantics=("arbitrary", "arbitrary", "arbitrary"),
                                             vmem_limit_bytes=VMEM_LIMIT_BYTES),
        name="ffn",
    )(x, *outs, wo, fnorm, wg, wu, cw, cb, wd)


def _pad_rows(v, width=MXU_WIDTH):
    return jnp.pad(v.astype(jnp.float32), ((0, 0), (0, width - v.shape[1])))


def _tile_rows(g, reps):
    return jnp.tile(g.astype(jnp.float32), (1, reps))


def _pack_in_projection(w):
    idx = _np_in_index()
    pieces, start = [], 0
    while start < PK_END:
        stop = start + 1
        if idx[start] == _SRC_END:
            while stop < PK_END and idx[stop] == _SRC_END:
                stop += 1
            pieces.append(jnp.zeros(w.shape[:-1] + (stop - start,), w.dtype))
        else:
            while stop < PK_END and idx[stop] == idx[stop - 1] + 1:
                stop += 1
            pieces.append(w[..., int(idx[start]):int(idx[stop - 1]) + 1])
        start = stop
    return jnp.concatenate(pieces, axis=-1)


def _zero_col(w):
    return jnp.concatenate([w, jnp.zeros(w.shape[:-1] + (1,), w.dtype)], axis=-1)


def _rope_table(positions):
    pos = positions.astype(jnp.float32)[:, :, None]
    inv = [ROPE_THETA ** (-jnp.arange(0, rot, 2, dtype=jnp.float32) / rot) for rot in (MLA_ROPE, ROT_MOBA, ROT_DIFF)]
    inv = jnp.concatenate(inv + [jnp.zeros((N_FREQ - TR_ONE,), jnp.float32)])
    ang = pos * inv
    c, s = jnp.cos(ang), jnp.sin(ang)
    c_hi = c.astype(jnp.bfloat16)
    c_lo = (c - c_hi.astype(jnp.float32)).astype(jnp.bfloat16)
    s_hi = s.astype(jnp.bfloat16)
    s_lo = (s - s_hi.astype(jnp.float32)).astype(jnp.bfloat16)
    return jnp.concatenate([c_hi, c_lo, s_hi, s_lo], axis=-1)


def _pick_tile(n, pref):
    t = pref
    while n % t:
        t //= 2
    return t


def kernel(x, mem, positions, attn_norm, ffn_norm, mem_norm, w_in, mla_cq_norm, mla_ckv_norm, mla_w_uq, mla_w_ukv, mla_q_norm, mla_k_norm, fox_b_f, fox_q_norm, fox_k_norm, moba_q_norm, moba_k_norm, diff_lambda, diff_q_norm, diff_k_norm, diff_sub_norm, mem_w_kv, mem_q_norm, mem_k_norm, w_o, ffn_w_gate, ffn_w_up, ffn_conv_w, ffn_conv_b, ffn_w_down):
    bsz, seq, d = x.shape
    depth = w_in.shape[0]
    dff = ffn_w_gate.shape[2]
    bf = jnp.bfloat16
    f32 = jnp.float32

    uq_idx = _np_uq_index()
    ukvk_idx, ukvv_idx = _np_ukv_index()
    gpair = jnp.asarray(_np_group_matrix(_PAIR_GROUPS), bf)
    g64 = jnp.asarray(_np_group_matrix(_G64_GROUPS), bf)
    g32 = jnp.asarray(_np_group_matrix(_G32_GROUPS), bf)
    expand = jnp.asarray(_np_rope_expand_all(), bf)
    t_prep = max(_pick_tile(seq, PREP_TILE), MOBA_BLOCK)
    tril = jnp.asarray(np.tril(np.ones((MOBA_BLOCK, MOBA_BLOCK), np.float32)), bf)
    vsel = jnp.asarray(_np_value_select(), bf)
    tr = _rope_table(positions)

    tile = lambda cfg: max(_pick_tile(seq, cfg.tile), MOBA_BLOCK) if cfg.select else _pick_tile(seq, cfg.tile)
    t_ffn = _pick_tile(seq, 512)
    tf = dff // 2 if (dff // 2) % LANES == 0 else dff

    win = _pack_in_projection(w_in.astype(bf))
    wuq = jnp.take(_zero_col(mla_w_uq), uq_idx, axis=2)
    wuq = jnp.pad(wuq, ((0, 0), (0, MXU_WIDTH - MLA_Q_RANK), (0, 0))).astype(bf)
    wukv = _zero_col(mla_w_ukv)
    wukvk = jnp.take(wukv, ukvk_idx, axis=2).astype(bf)
    wukvv = jnp.take(wukv, ukvv_idx, axis=2).astype(bf)
    pair = lambda g: _pad_rows(_tile_rows(g, 2))
    rows = [jnp.zeros((depth, MXU_WIDTH), f32)] * P_ROWS
    rows[P_CQ] = _pad_rows(mla_cq_norm)
    rows[P_CKV] = _pad_rows(mla_ckv_norm)
    rows[P_GQ] = pair(mla_q_norm) * (LOG2E * MLA_QK ** -0.5)
    rows[P_GK] = pair(mla_k_norm)
    rows[P_FQ] = _tile_rows(fox_q_norm, N_HEADS) * (LOG2E * HEAD_DIM ** -0.5)
    rows[P_FK] = _tile_rows(fox_k_norm, N_HEADS)
    rows[P_FB] = _pad_rows(fox_b_f)
    rows[P_MQ] = _tile_rows(moba_q_norm, N_HEADS)
    rows[P_MK] = _tile_rows(moba_k_norm, N_HEADS)
    rows[P_DQ] = _tile_rows(diff_q_norm, 2 * N_HEADS) * (LOG2E * DIFF_QK ** -0.5)
    rows[P_DK] = _tile_rows(diff_k_norm, 2 * N_HEADS)
    rows[P_EQ] = _tile_rows(mem_q_norm, N_HEADS) * (LOG2E * HEAD_DIM ** -0.5)
    par = jnp.stack(rows, axis=1)
    anorm = attn_norm.astype(f32)[:, None, :]
    mnorm = mem_norm.astype(f32)[:, None, :]
    fnorm = ffn_norm.astype(f32)[:, None, :]
    wmem = mem_w_kv.astype(bf)
    mem_gain = _tile_rows(mem_k_norm, N_HEADS)[:, None, :]

    lam_init = jnp.asarray([0.8 - 0.6 * math.exp(-0.3 * l) for l in range(depth)], f32)
    lam_vec = diff_lambda.astype(f32)
    lam = (jnp.exp(jnp.sum(lam_vec[:, 0] * lam_vec[:, 1], axis=-1))
           - jnp.exp(jnp.sum(lam_vec[:, 2] * lam_vec[:, 3], axis=-1)) + lam_init)
    lam_row = jnp.broadcast_to(lam[:, None, None], (depth, 1, GROUP_WIDTH))
    gsub = (_tile_rows(diff_sub_norm, N_HEADS) * (1.0 - lam_init)[:, None])[:, None, :]

    wo = w_o.astype(bf)
    wg, wu, wd = ffn_w_gate.astype(bf), ffn_w_up.astype(bf), ffn_w_down.astype(bf)
    cw = jnp.pad(ffn_conv_w.astype(f32), ((0, 0), (0, SUBLANES - CONV_WIDTH), (0, 0)))
    cb = ffn_conv_b.astype(f32)[:, None, :]

    for l in range(depth):
        (aq, ak, av, fq, fk, fv, fdcol, fdrow, mq, mk, mv, msel, dq, dk, dv, eq) = _prep_call(
            l, t_prep, x, tr, anorm, win, wuq, wukvk, wukvv, gpair, g64, g32, expand, tril, vsel, par)
        ek, ev = _memkv_call(l, mem, mnorm, wmem, g64, mem_gain, vsel)

        o_a = _attn_call(_CFG_MLA, aq, ak, av, (), tile(_CFG_MLA), tile(_CFG_MLA))
        o_b = _attn_call(_CFG_FOX, fq, fk, fv, (fdcol, fdrow), tile(_CFG_FOX), tile(_CFG_FOX))
        o_c = _attn_call(_CFG_MOBA, mq, mk, mv, (msel,), tile(_CFG_MOBA), tile(_CFG_MOBA))
        o_d = _attn_call(_CFG_DIFF, dq, dk, dv, (l, g64, gsub, lam_row), tile(_CFG_DIFF), tile(_CFG_DIFF))
        o_e = _attn_call(_CFG_MEM, eq, ek, ev, (), tile(_CFG_MEM), mem.shape[1])

        x = _ffn_call(l, x, (o_a, o_b, o_c, o_d, o_e), wo, fnorm, wg, wu, cw, cb, wd, t_ffn, tf)
    return x
```

```python
import functools
import math

import numpy as np
import jax
import jax.numpy as jnp
from jax import lax
from jax.experimental import pallas as pl
from jax.experimental.pallas import tpu as pltpu

N_HEADS = 4
HEAD_DIM = 64
GROUP_WIDTH = N_HEADS * HEAD_DIM
MLA_Q_RANK = 192
MLA_KV_RANK = 128
MLA_NOPE = 64
MLA_ROPE = 32
MLA_QK = MLA_NOPE + MLA_ROPE
DIFF_QK = HEAD_DIM // 2
ROPE_THETA = 500000.0
ROT_MOBA = HEAD_DIM // 4
ROT_DIFF = DIFF_QK // 4
MOBA_BLOCK = 256
MOBA_TOPK = 3
CONV_WIDTH = 3
EPS = 1e-6
NEG_INF = -1e30
LOG2E = math.log2(math.e)
REMOVED = -3e38

LANES = 128
SUBLANES = 8
MXU_WIDTH = 256
TAIL_ROWS = 16
PREP_TILE = 512
VMEM_LIMIT_BYTES = 56 * 1024 * 1024

_SRC_CQ = 0
_SRC_CKV = _SRC_CQ + MLA_Q_RANK
_SRC_KR = _SRC_CKV + MLA_KV_RANK
_SRC_FOX = _SRC_KR + MLA_ROPE
_SRC_FOXF = _SRC_FOX + 3 * GROUP_WIDTH
_SRC_MOBA = _SRC_FOXF + N_HEADS
_SRC_DIFF = _SRC_MOBA + 3 * GROUP_WIDTH
_SRC_MEMQ = _SRC_DIFF + 3 * GROUP_WIDTH
_SRC_END = _SRC_MEMQ + GROUP_WIDTH

PK_CQ = 0
PK_CKV = 256
PK_KR = 384
PK_FQ, PK_FK, PK_FV = 896, 1152, 1408
PK_FF = 1664
PK_MQ, PK_MK, PK_MV = 1792, 2048, 2304
PK_DQ, PK_DK, PK_DV = 2560, 2816, 3072
PK_EQ = 3328
PK_END = 3584

PAIR_STRIDE = MLA_QK


def _pair_lane(h, d):
    return (h // 2) * MXU_WIDTH + (h % 2) * PAIR_STRIDE + d


N_FREQ = 32
FREQ_BASE_MLA = 0
FREQ_BASE_MOBA = MLA_ROPE // 2
FREQ_BASE_DIFF = FREQ_BASE_MOBA + ROT_MOBA // 2
TR_ONE = FREQ_BASE_DIFF + ROT_DIFF // 2
TR_WIDTH = 4 * N_FREQ
assert TR_ONE < N_FREQ and TR_WIDTH == LANES

(P_CQ, P_CKV, P_GQ, P_GK, P_FQ, P_FK, P_FB, P_MQ, P_MK, P_DQ, P_DK, P_EQ) = range(12)
P_ROWS = 16


def _np_in_index():
    idx = np.full((PK_END,), _SRC_END, np.int32)
    idx[PK_CQ:PK_CQ + MLA_Q_RANK] = np.arange(_SRC_CQ, _SRC_CQ + MLA_Q_RANK)
    idx[PK_CKV:PK_CKV + MLA_KV_RANK] = np.arange(_SRC_CKV, _SRC_CKV + MLA_KV_RANK)
    for h in range(N_HEADS):
        for d in range(MLA_ROPE):
            idx[PK_KR + _pair_lane(h, d)] = _SRC_KR + d
    idx[PK_FQ:PK_FQ + 3 * GROUP_WIDTH] = np.arange(_SRC_FOX, _SRC_FOX + 3 * GROUP_WIDTH)
    idx[PK_FF:PK_FF + N_HEADS] = np.arange(_SRC_FOXF, _SRC_FOXF + N_HEADS)
    idx[PK_MQ:PK_MQ + 3 * GROUP_WIDTH] = np.arange(_SRC_MOBA, _SRC_MOBA + 3 * GROUP_WIDTH)
    idx[PK_DQ:PK_DQ + 3 * GROUP_WIDTH] = np.arange(_SRC_DIFF, _SRC_DIFF + 3 * GROUP_WIDTH)
    idx[PK_EQ:PK_EQ + GROUP_WIDTH] = np.arange(_SRC_MEMQ, _SRC_MEMQ + GROUP_WIDTH)
    return idx


def _np_uq_index():
    idx = np.full((2 * MXU_WIDTH,), N_HEADS * MLA_QK, np.int32)
    for h in range(N_HEADS):
        for d in range(MLA_QK):
            idx[_pair_lane(h, d)] = h * MLA_QK + d
    return idx


def _np_ukv_index():
    zero = N_HEADS * (MLA_NOPE + HEAD_DIM)
    idx_k = np.full((2 * MXU_WIDTH,), zero, np.int32)
    idx_v = np.zeros((GROUP_WIDTH,), np.int32)
    for h in range(N_HEADS):
        for d in range(MLA_NOPE):
            idx_k[_pair_lane(h, MLA_ROPE + d)] = h * (MLA_NOPE + HEAD_DIM) + d
        for d in range(HEAD_DIM):
            idx_v[h * HEAD_DIM + d] = h * (MLA_NOPE + HEAD_DIM) + MLA_NOPE + d
    return idx_k, idx_v


def _np_group_matrix(groups):
    g = np.zeros((MXU_WIDTH, MXU_WIDTH), np.float32)
    for lo, size in groups:
        g[lo:lo + size, lo:lo + size] = 1.0 / size
    return g


_PAIR_GROUPS = [(0, MLA_ROPE), (MLA_ROPE, MLA_NOPE), (PAIR_STRIDE, MLA_ROPE), (PAIR_STRIDE + MLA_ROPE, MLA_NOPE)]
_G64_GROUPS = [(h * HEAD_DIM, HEAD_DIM) for h in range(N_HEADS)]
_G32_GROUPS = [(g * DIFF_QK, DIFF_QK) for g in range(2 * N_HEADS)]


def _np_rope_expand(regions, rot, base):
    half = rot // 2
    e = np.zeros((TR_WIDTH, 2 * MXU_WIDTH), np.float32)
    e[TR_ONE, 0:MXU_WIDTH] = 1.0
    for lo in regions:
        assert lo % rot == 0
        for r in range(half):
            f = base + r
            for lane, sign in ((lo + r, -1.0), (lo + half + r, 1.0)):
                e[TR_ONE, lane] = 0.0
                e[f, lane] = 1.0
                e[N_FREQ + f, lane] = 1.0
                e[2 * N_FREQ + f, MXU_WIDTH + lane] = sign
                e[3 * N_FREQ + f, MXU_WIDTH + lane] = sign
    return e


def _np_rope_expand_all():
    return np.concatenate([
        _np_rope_expand([0, PAIR_STRIDE], MLA_ROPE, FREQ_BASE_MLA),
        _np_rope_expand([h * HEAD_DIM for h in range(N_HEADS)], ROT_MOBA, FREQ_BASE_MOBA),
        _np_rope_expand([g * DIFF_QK for g in range(2 * N_HEADS)], ROT_DIFF, FREQ_BASE_DIFF),
    ], axis=1)


def _dot(a, b):
    return jnp.dot(a, b, preferred_element_type=jnp.float32)


def _dot_nt(a, b):
    return lax.dot_general(a, b, (((1,), (1,)), ((), ())), preferred_element_type=jnp.float32)


def _split2(a):
    hi = a.astype(jnp.bfloat16)
    lo = (a - hi.astype(jnp.float32)).astype(jnp.bfloat16)
    return hi, lo


def _split3(a):
    hi = a.astype(jnp.bfloat16)
    r = a - hi.astype(jnp.float32)
    mid = r.astype(jnp.bfloat16)
    lo = (r - mid.astype(jnp.float32)).astype(jnp.bfloat16)
    return hi, mid, lo


def _group_mean_sq(a, g_bf16):
    return _dot((a * a).astype(jnp.bfloat16), g_bf16)


def _rope(x, tabs, half):
    w = x.shape[-1]
    lane = lax.broadcasted_iota(jnp.int32, x.shape, 1)
    partner = jnp.where((lane & (2 * half - 1)) >= half, pltpu.roll(x, half, 1), pltpu.roll(x, w - half, 1))
    return x * tabs[:, 0:w] + partner * tabs[:, w:2 * w]


def _lane_mask(shape, lo, hi):
    lane = lax.broadcasted_iota(jnp.int32, shape, len(shape) - 1)
    return (lane >= lo) & (lane < hi)


def _prep_body(x_ref, tr_ref, anorm_ref, win_ref, wuq_ref, wukvk_ref, wukvv_ref, gpair_ref, g64_ref, g32_ref,
               exp_ref, tril_ref, vsel_ref, par_ref,
               aq_ref, ak_ref, av_ref, fq_ref, fk_ref, fv_ref, fdcol_ref, fdrow_ref, mq_ref, mk_ref, mv_ref, msel_ref,
               dq_ref, dk_ref, dv_ref, eq_ref,
               kmean_s, carry_s):
    j = pl.program_id(1)
    tm = x_ref.shape[1]
    pr = MOBA_BLOCK
    n_parts = tm // pr
    nbp = kmean_s.shape[0]
    bf = jnp.bfloat16
    gpair, g64, g32 = gpair_ref[...], g64_ref[...], g32_ref[...]
    vsel = vsel_ref[...]
    tril = tril_ref[...]

    @pl.when(j == 0)
    def _():
        kmean_s[...] = jnp.zeros_like(kmean_s)
        carry_s[...] = jnp.zeros_like(carry_s)

    def prow(r, width=MXU_WIDTH):
        return par_ref[r:r + 1, 0:width]

    def project(st):
        rows = st["rows"]
        x = x_ref[0, rows, :]
        xb = (x * lax.rsqrt(jnp.mean(x * x, axis=-1, keepdims=True) + EPS) * anorm_ref[...]).astype(bf)
        tabs = _dot(tr_ref[0, rows, :], exp_ref[...])
        st["tab_mla"] = tabs[:, 0:2 * MXU_WIDTH]
        st["tab_moba"] = tabs[:, 2 * MXU_WIDTH:4 * MXU_WIDTH]
        st["tab_diff"] = tabs[:, 4 * MXU_WIDTH:6 * MXU_WIDTH]

        def run(lo, hi):
            wide = _dot(xb, win_ref[:, lo:hi])
            return lambda off, width: wide[:, off - lo:off - lo + width]

        run_m = run(PK_MQ, PK_DQ)
        run_a = run(PK_CQ, PK_FQ)
        run_f = run(PK_FQ, PK_MQ)
        run_d = run(PK_DQ, PK_END)
        st["direct"] = {"mq": run_m(PK_MQ, MXU_WIDTH), "mk": run_m(PK_MK, MXU_WIDTH),
                        "fq": run_f(PK_FQ, MXU_WIDTH), "fk": run_f(PK_FK, MXU_WIDTH),
                        "dq": run_d(PK_DQ, MXU_WIDTH), "dk": run_d(PK_DK, MXU_WIDTH),
                        "eq": run_d(PK_EQ, MXU_WIDTH)}
        st["values"] = {"fv": run_f(PK_FV, MXU_WIDTH).astype(bf), "mv": run_m(PK_MV, MXU_WIDTH).astype(bf),
                        "dv": run_d(PK_DV, MXU_WIDTH).astype(bf)}
        p_cq, p_ckv, st["p_kr"] = run_a(PK_CQ, MXU_WIDTH), run_a(PK_CKV, MLA_KV_RANK), run_a(PK_KR, 2 * MXU_WIDTH)
        cqn = p_cq * lax.rsqrt(jnp.sum(p_cq * p_cq, axis=-1, keepdims=True) * (1.0 / MLA_Q_RANK) + EPS) * prow(P_CQ)
        ckvn = p_ckv * lax.rsqrt(jnp.mean(p_ckv * p_ckv, axis=-1, keepdims=True) + EPS) * prow(P_CKV, MLA_KV_RANK)
        st["cqb"], st["ckvb"] = cqn.astype(bf), ckvn.astype(bf)
        z = run_f(PK_FF, LANES) + prow(P_FB, LANES)
        log_f = jnp.minimum(z, 0.0) - jnp.log1p(jnp.exp(-jnp.abs(z)))
        st["log_f"] = _split3(jnp.where(_lane_mask(log_f.shape, 0, N_HEADS), log_f, 0.0))

    def second(st):
        gmat = {"fq": g64, "fk": g64, "mq": g64, "mk": g64, "dq": g32, "dk": g32, "eq": g64}
        gains = {"fq": P_FQ, "fk": P_FK, "mq": P_MQ, "mk": P_MK, "dq": P_DQ, "dk": P_DK, "eq": P_EQ}
        ms = {name: _group_mean_sq(a, gmat[name]) for name, a in st["direct"].items()}
        st["qa"] = _dot(st["cqb"], wuq_ref[...])
        st["ka"] = st["p_kr"] + _dot(st["ckvb"], wukvk_ref[...])
        st["values"]["av"] = _dot(st["ckvb"], wukvv_ref[...]).astype(bf)
        l1, l2, l3 = st["log_f"]
        st["cumsum"] = (_dot(tril, l1) + _dot(tril, l2)) + _dot(tril, l3)
        st["normed"] = {name: a * lax.rsqrt(ms[name] + EPS) * prow(gains[name]) for name, a in st["direct"].items()}

    def third(st):
        rows = st["rows"]
        mq = _rope(st["normed"]["mq"], st["tab_moba"], ROT_MOBA // 2)
        mk = _rope(st["normed"]["mk"], st["tab_moba"], ROT_MOBA // 2)
        st["mq"], st["mk"] = mq, mk
        kmean_s[pl.ds(st["blk"], 1), :] = jnp.mean(mk, axis=0, keepdims=True)
        km_hi, km_lo = _split2(kmean_s[...])
        st["gates"] = []
        for h in range(N_HEADS):
            q_hi, q_lo = _split2(jnp.where(_lane_mask(mq.shape, h * HEAD_DIM, (h + 1) * HEAD_DIM), mq, 0.0))
            st["gates"].append((_dot_nt(km_hi, q_hi) + _dot_nt(km_lo, q_hi)) + _dot_nt(km_hi, q_lo))
        st["ms_qa"] = [_group_mean_sq(st["qa"][:, p * MXU_WIDTH:(p + 1) * MXU_WIDTH], gpair) for p in range(2)]
        st["ms_ka"] = [_group_mean_sq(st["ka"][:, p * MXU_WIDTH:(p + 1) * MXU_WIDTH], gpair) for p in range(2)]
        dec = carry_s[...] + st["cumsum"]
        carry_s[...] = dec[pr - 1:pr, :]
        dec2 = dec * LOG2E
        d1, d2, d3 = _split3(dec2)
        for h in range(N_HEADS):
            fdcol_ref[0, rows, h * LANES:(h + 1) * LANES] = jnp.broadcast_to(dec2[:, h:h + 1], (pr, LANES))
        row_sel = jnp.where(lax.broadcasted_iota(jnp.int32, (SUBLANES, LANES), 0)
                            == lax.broadcasted_iota(jnp.int32, (SUBLANES, LANES), 1), 1.0, 0.0).astype(bf)
        fdrow_ref[0, :, rows] = (_dot_nt(row_sel, d1) + _dot_nt(row_sel, d2)) + _dot_nt(row_sel, d3)
        for name, ref in (("fv", fv_ref), ("mv", mv_ref), ("dv", dv_ref), ("av", av_ref)):
            ref[0, :, rows] = _values_t(st["values"][name], vsel)

    def finish(st):
        rows = st["rows"]
        normed = st["normed"]
        fq_ref[0, rows, :] = normed["fq"].astype(bf)
        fk_ref[0, rows, :] = normed["fk"].astype(bf)
        eq_ref[0, rows, :] = normed["eq"].astype(bf)
        dq_ref[0, rows, :] = _rope(normed["dq"], st["tab_diff"], ROT_DIFF // 2).astype(bf)
        dk_ref[0, rows, :] = _rope(normed["dk"], st["tab_diff"], ROT_DIFF // 2).astype(bf)
        mq_ref[0, rows, :] = (st["mq"] * (LOG2E * HEAD_DIM ** -0.5)).astype(bf)
        mk_ref[0, rows, :] = st["mk"].astype(bf)
        for p in range(2):
            sl = slice(p * MXU_WIDTH, (p + 1) * MXU_WIDTH)
            aq_ref[0, rows, sl] = _rope(st["qa"][:, sl] * lax.rsqrt(st["ms_qa"][p] + EPS) * prow(P_GQ),
                                        st["tab_mla"], MLA_ROPE // 2).astype(bf)
            ak_ref[0, rows, sl] = _rope(st["ka"][:, sl] * lax.rsqrt(st["ms_ka"][p] + EPS) * prow(P_GK),
                                        st["tab_mla"], MLA_ROPE // 2).astype(bf)
        blk = lax.broadcasted_iota(jnp.int32, (nbp, pr), 0)
        past = blk < st["blk"]
        for h in range(N_HEADS):
            work = jnp.where(past, st["gates"][h], NEG_INF)
            sel = jnp.zeros((nbp, pr), jnp.bool_)
            for _ in range(MOBA_TOPK):
                mx = jnp.max(work, axis=0, keepdims=True)
                first = jnp.min(jnp.where(work == mx, blk, nbp), axis=0, keepdims=True)
                pick = blk == first
                sel = sel | pick
                work = jnp.where(pick, REMOVED, work)
            msel_ref[0, h * nbp:(h + 1) * nbp, rows] = jnp.where(sel & past, 0.0, NEG_INF)

    parts = [{"rows": slice(p * pr, (p + 1) * pr), "blk": j * n_parts + p} for p in range(n_parts)]
    for phase in (project, second, third, finish):
        for st in parts:
            phase(st)


def _const_spec(a, layer=None):
    if layer is None:
        n = a.ndim
        return pl.BlockSpec(a.shape, lambda *_: (0,) * n)
    n = a.ndim - 1
    return pl.BlockSpec((None,) + a.shape[1:], lambda *_: (layer,) + (0,) * n)


def _moba_blocks_padded(seq):
    return -(-(seq // MOBA_BLOCK) // SUBLANES) * SUBLANES


def _prep_call(layer, tm, x, tr, anorm, win, wuq, wukvk, wukvv, gpair, g64, g32, expand, tril, vsel, par):
    bsz, seq, d = x.shape
    assert seq % tm == 0 and tm % MOBA_BLOCK == 0 and tril.shape == (MOBA_BLOCK, MOBA_BLOCK)
    nbp = _moba_blocks_padded(seq)
    bf = jnp.bfloat16
    f32 = jnp.float32

    vt = -N_HEADS * V_ROWS
    widths = [(2 * MXU_WIDTH, bf), (2 * MXU_WIDTH, bf), (vt, bf),
              (GROUP_WIDTH, bf), (GROUP_WIDTH, bf), (vt, bf),
              (N_HEADS * LANES, f32), (-SUBLANES, f32),
              (GROUP_WIDTH, bf), (GROUP_WIDTH, bf), (vt, bf), (-N_HEADS * nbp, f32),
              (GROUP_WIDTH, bf), (GROUP_WIDTH, bf), (vt, bf),
              (GROUP_WIDTH, bf)]

    def tok(width):
        if width > 0:
            return pl.BlockSpec((1, tm, width), lambda b, j: (b, j, 0))
        return pl.BlockSpec((1, -width, tm), lambda b, j: (b, 0, j))

    def shape(width):
        return (bsz, seq, width) if width > 0 else (bsz, -width, seq)

    consts = [anorm, win, wuq, wukvk, wukvv, gpair, g64, g32, expand, tril, vsel, par]
    layered = [True, True, True, True, True, False, False, False, False, False, False, True]
    return pl.pallas_call(
        _prep_body,
        grid=(bsz, seq // tm),
        in_specs=[tok(d), tok(TR_WIDTH)] + [_const_spec(c, layer if ly else None) for c, ly in zip(consts, layered)],
        out_specs=[tok(w) for w, _ in widths],
        out_shape=[jax.ShapeDtypeStruct(shape(w), dt) for w, dt in widths],
        scratch_shapes=[pltpu.VMEM((nbp, GROUP_WIDTH), jnp.float32), pltpu.VMEM((1, LANES), jnp.float32)],
        compiler_params=pltpu.CompilerParams(dimension_semantics=("arbitrary", "arbitrary"),
                                             vmem_limit_bytes=VMEM_LIMIT_BYTES),
        name="prep",
    )(x, tr, *consts)


def _memkv_body(mem_ref, mnorm_ref, w_ref, g64_ref, gain_ref, vsel_ref, k_ref, v_ref):
    m = mem_ref[0]
    mn = m * lax.rsqrt(jnp.mean(m * m, axis=-1, keepdims=True) + EPS) * mnorm_ref[...]
    kv = _dot(mn.astype(jnp.bfloat16), w_ref[...])
    k = kv[:, 0:GROUP_WIDTH]
    k = k * lax.rsqrt(_group_mean_sq(k, g64_ref[...]) + EPS) * gain_ref[...]
    k_ref[0] = k.astype(jnp.bfloat16)
    v_ref[0] = _values_t(kv[:, GROUP_WIDTH:2 * GROUP_WIDTH].astype(jnp.bfloat16), vsel_ref[...])


def _memkv_call(layer, mem, mnorm, w, g64, gain, vsel):
    bsz, mlen, d = mem.shape
    k_shape, vt_shape = (bsz, mlen, GROUP_WIDTH), (bsz, N_HEADS * V_ROWS, mlen)
    return pl.pallas_call(
        _memkv_body,
        grid=(bsz,),
        in_specs=[pl.BlockSpec((1, mlen, d), lambda b: (b, 0, 0)), _const_spec(mnorm, layer), _const_spec(w, layer),
                  _const_spec(g64), _const_spec(gain, layer), _const_spec(vsel)],
        out_specs=[pl.BlockSpec((1,) + s[1:], lambda b: (b, 0, 0)) for s in (k_shape, vt_shape)],
        out_shape=[jax.ShapeDtypeStruct(s, jnp.bfloat16) for s in (k_shape, vt_shape)],
        compiler_params=pltpu.CompilerParams(dimension_semantics=("arbitrary",), vmem_limit_bytes=VMEM_LIMIT_BYTES),
        name="mem_kv",
    )(mem, mnorm, w, g64, gain, vsel)


Q_SUB = 2 * MXU_WIDTH


class _AttnCfg:
    def __init__(self, name, vheads, n_maps, causal, decay=False, select=False, diff=False, tile=1024, lookahead=8,
                 q_sub=Q_SUB):
        self.name = name
        self.vheads = vheads
        self.n_maps = n_maps
        self.causal = causal
        self.decay = decay
        self.select = select
        self.diff = diff
        self.tile = tile
        self.lookahead = lookahead
        self.q_sub = q_sub


_PLAIN_VHEADS = [(0, h * HEAD_DIM, (h + 1) * HEAD_DIM, 0, h) for h in range(N_HEADS)]
_CFG_MLA = _AttnCfg("attn_mla", [((h // 2) * MXU_WIDTH, (h % 2) * PAIR_STRIDE, (h % 2) * PAIR_STRIDE + MLA_QK, 0, h)
                                 for h in range(N_HEADS)], 1, True, q_sub=MXU_WIDTH, lookahead=16)
_CFG_FOX = _AttnCfg("attn_fox", _PLAIN_VHEADS, 1, True, decay=True, tile=2048, lookahead=4)
_CFG_MOBA = _AttnCfg("attn_moba", _PLAIN_VHEADS, 1, True, select=True, tile=2048, lookahead=6)
_CFG_DIFF = _AttnCfg("attn_diff", [(0, h * HEAD_DIM + c * DIFF_QK, h * HEAD_DIM + (c + 1) * DIFF_QK, c, h)
                                   for c in range(2) for h in range(N_HEADS)], 2, True, diff=True, q_sub=MXU_WIDTH,
                     lookahead=16)
_CFG_MEM = _AttnCfg("attn_mem", _PLAIN_VHEADS, 1, False, tile=2048)


ONES_ROW = HEAD_DIM
V_ROWS = HEAD_DIM + 16


def _np_value_select():
    sel = np.zeros((N_HEADS * V_ROWS, GROUP_WIDTH), np.float32)
    for h in range(N_HEADS):
        for d in range(HEAD_DIM):
            sel[h * V_ROWS + d, h * HEAD_DIM + d] = 1.0
    return sel


def _values_t(v, vsel):
    vt = _dot_nt(vsel, v)
    row = lax.broadcasted_iota(jnp.int32, vt.shape, 0)
    ones = row == ONES_ROW
    for h in range(1, N_HEADS):
        ones = ones | (row == h * V_ROWS + ONES_ROW)
    return jnp.where(ones, 1.0, vt).astype(jnp.bfloat16)


def _tile_lanes(x, width):
    return jnp.tile(x, (1, width // LANES)) if width != LANES else x


def _attn_body(cfg, qi_ref, kj_ref, *refs):
    refs = list(refs)
    q_ref, k_ref, vt_ref = refs[:3]
    pos = 3
    if cfg.decay:
        dq_ref, dk_ref = refs[pos:pos + 2]
        pos += 2
    if cfg.select:
        sel_ref = refs[pos]
        pos += 1
    if cfg.diff:
        g64_ref, gsub_ref, lam_ref = refs[pos:pos + 3]
        pos += 3
    o_ref, qm_s, m_s, acc_s = refs[pos:pos + 4]

    t = pl.program_id(1)
    i = qi_ref[t]
    j = kj_ref[t]
    tq = q_ref.shape[1]
    tk = k_ref.shape[1]

    @pl.when(j == 0)
    def _():
        for n, (off, lo, hi, _, _) in enumerate(cfg.vheads):
            qb = q_ref[0, :, off:off + MXU_WIDTH]
            qm_s[n] = jnp.where(_lane_mask(qb.shape, lo, hi), qb, jnp.zeros_like(qb))
        m_s[...] = jnp.full(m_s.shape, NEG_INF, jnp.float32)
        acc_s[...] = jnp.zeros_like(acc_s)

    def step(diag):
        qs = min(tq, cfg.q_sub)
        items = [(n, u) for n in range(len(cfg.vheads)) for u in range(tq // qs)]

        def n_keys(u):
            return (u + 1) * qs if diag else tk

        def scores(item):
            n, u = item
            off, _, _, _, h = cfg.vheads[n]
            nk = n_keys(u)
            cols = slice(u * qs, (u + 1) * qs)
            s = _dot_nt(k_ref[0, 0:nk, off:off + MXU_WIDTH], qm_s[n, cols, :])
            if cfg.decay:
                s = (dq_ref[0, h:h + 1, cols] - _tile_lanes(dk_ref[0, 0:nk, h * LANES:(h + 1) * LANES], qs)) + s
            if cfg.select:
                nbp = sel_ref.shape[1] // N_HEADS
                qpos = u * qs + lax.broadcasted_iota(jnp.int32, (1, qs), 1)
                parts = []
                for kb in range(nk // MOBA_BLOCK):
                    rows = s[kb * MOBA_BLOCK:(kb + 1) * MOBA_BLOCK, :]
                    if not (diag and kb == nk // MOBA_BLOCK - 1):
                        bias = sel_ref[0, pl.ds(h * nbp + j * (tk // MOBA_BLOCK) + kb, 1), cols]
                        if diag:
                            bias = jnp.where(qpos < (kb + 1) * MOBA_BLOCK, 0.0, bias)
                        rows = rows + bias
                    parts.append(rows)
                s = parts[0] if len(parts) == 1 else jnp.concatenate(parts, axis=0)
            if diag:
                key = lax.broadcasted_iota(jnp.int32, (nk, qs), 0)
                qry = u * qs + lax.broadcasted_iota(jnp.int32, (nk, qs), 1)
                s = jnp.where(key <= qry, s, NEG_INF)
            return s, jnp.max(s, axis=0, keepdims=True)

        raw = {it: scores(items[it]) for it in range(min(cfg.lookahead, len(items)))}
        for it, (n, u) in enumerate(items):
            h = cfg.vheads[n][4]
            nk = n_keys(u)
            cols = slice(u * qs, (u + 1) * qs)
            s, s_max = raw.pop(it)
            m_prev = m_s[n, :, cols]
            m_new = jnp.maximum(m_prev, s_max)
            alpha = jnp.exp2(m_prev - m_new)
            p = jnp.exp2(s - m_new)
            m_s[n, :, cols] = m_new
            acc_s[n, :, cols] = acc_s[n, :, cols] * alpha + _dot(vt_ref[0, h * V_ROWS:(h + 1) * V_ROWS, 0:nk],
                                                                 p.astype(jnp.bfloat16))
            if it + cfg.lookahead < len(items):
                raw[it + cfg.lookahead] = scores(items[it + cfg.lookahead])

    if cfg.causal:
        pl.when(j < i)(functools.partial(step, False))
        pl.when(j == i)(functools.partial(step, True))
        last = j == i
    else:
        step(False)
        last = j == 0

    @pl.when(last)
    def _():
        outs = []
        for c in range(cfg.n_maps):
            heads = []
            for h in range(N_HEADS):
                acc = acc_s[c * N_HEADS + h]
                heads.append(acc[0:HEAD_DIM, :] / acc[ONES_ROW:ONES_ROW + 1, :])
            outs.append(jnp.concatenate(heads, axis=0).T)
        if cfg.diff:
            o = outs[0] - lam_ref[0:1, :] * outs[1]
            o = o * lax.rsqrt(_group_mean_sq(o, g64_ref[...]) + EPS) * gsub_ref[...]
        else:
            o = outs[0]
        o_ref[0] = o.astype(o_ref.dtype)


def _attn_call(cfg, q, k, v, extras, tq, tk):
    bsz, seq, wq = q.shape
    sk = k.shape[1]
    nq = seq // tq
    assert seq % tq == 0 and sk % tk == 0
    if cfg.causal:
        assert tq == tk and sk == seq
        pairs = [(i, j) for i in range(nq) for j in range(i + 1)]
    else:
        assert sk == tk
        pairs = [(i, 0) for i in range(nq)]
    qi = jnp.asarray(np.array([p[0] for p in pairs], np.int32))
    kj = jnp.asarray(np.array([p[1] for p in pairs], np.int32))
    n_vh = len(cfg.vheads)

    in_specs = [pl.BlockSpec((1, tq, wq), lambda b, t, qi, kj: (b, qi[t], 0)),
                pl.BlockSpec((1, tk, wq), lambda b, t, qi, kj: (b, kj[t], 0)),
                pl.BlockSpec((1, N_HEADS * V_ROWS, tk), lambda b, t, qi, kj: (b, 0, kj[t]))]
    args = [q, k, v]
    if cfg.decay:
        dcol, drow = extras
        in_specs += [pl.BlockSpec((1, SUBLANES, tq), lambda b, t, qi, kj: (b, 0, qi[t])),
                     pl.BlockSpec((1, tk, N_HEADS * LANES), lambda b, t, qi, kj: (b, kj[t], 0))]
        args += [drow, dcol]
    if cfg.select:
        (sel,) = extras
        in_specs += [pl.BlockSpec((1, sel.shape[1], tq), lambda b, t, qi, kj: (b, 0, qi[t]))]
        args += [sel]
    if cfg.diff:
        layer, g64, gsub, lam_row = extras
        in_specs += [_const_spec(g64), _const_spec(gsub, layer), _const_spec(lam_row, layer)]
        args += [g64, gsub, lam_row]

    grid_spec = pltpu.PrefetchScalarGridSpec(
        num_scalar_prefetch=2,
        grid=(bsz, len(pairs)),
        in_specs=in_specs,
        out_specs=pl.BlockSpec((1, tq, GROUP_WIDTH), lambda b, t, qi, kj: (b, qi[t], 0)),
        scratch_shapes=[pltpu.VMEM((n_vh, tq, MXU_WIDTH), jnp.bfloat16),
                        pltpu.VMEM((n_vh, 1, tq), jnp.float32),
                        pltpu.VMEM((n_vh, V_ROWS, tq), jnp.float32)])
    return pl.pallas_call(
        functools.partial(_attn_body, cfg),
        grid_spec=grid_spec,
        out_shape=jax.ShapeDtypeStruct((bsz, seq, GROUP_WIDTH), jnp.bfloat16),
        compiler_params=pltpu.CompilerParams(dimension_semantics=("arbitrary", "arbitrary"),
                                             vmem_limit_bytes=VMEM_LIMIT_BYTES),
        name=cfg.name,
    )(qi, kj, *args)


def _ffn_body(nf, x_ref, oa_ref, ob_ref, oc_ref, od_ref, oe_ref, wo_ref, fnorm_ref, wg_ref, wu_ref, cw_ref, cb_ref,
              wd_ref, out_ref, xnew_s, xn_s, acc_s):
    i = pl.program_id(1)
    f = pl.program_id(2)
    tm = x_ref.shape[1]

    @pl.when(f == 0)
    def _():
        @pl.when(i == 0)
        def _():
            xn_s[0:TAIL_ROWS, :] = jnp.zeros((TAIL_ROWS, xn_s.shape[1]), xn_s.dtype)

        @pl.when(i > 0)
        def _():
            xn_s[0:TAIL_ROWS, :] = xn_s[tm:tm + TAIL_ROWS, :]

        mixed = jnp.concatenate([o_ref[0] for o_ref in (oa_ref, ob_ref, oc_ref, od_ref, oe_ref)], axis=1)
        xnew = x_ref[0] + _dot(mixed, wo_ref[...])
        xnew_s[...] = xnew
        xn = xnew * lax.rsqrt(jnp.mean(xnew * xnew, axis=-1, keepdims=True) + EPS) * fnorm_ref[...]
        xn_s[TAIL_ROWS:TAIL_ROWS + tm, :] = xn.astype(xn_s.dtype)

    def mlp_chunk():
        ge = _dot(xn_s[...], wg_ref[...])
        u = _dot(xn_s[TAIL_ROWS:TAIL_ROWS + tm, :], wu_ref[...])
        g0 = ge[TAIL_ROWS:TAIL_ROWS + tm, :]
        t1 = ge[TAIL_ROWS - 1:TAIL_ROWS, :]
        t2 = ge[TAIL_ROWS - 2:TAIL_ROWS - 1, :]
        row = lax.broadcasted_iota(jnp.int32, g0.shape, 0)
        g1 = jnp.where(row == 0, t1, pltpu.roll(g0, 1, 0))
        g2 = jnp.where(row == 0, t2, jnp.where(row == 1, t1, pltpu.roll(g0, 2, 0)))
        y = cb_ref[...] + cw_ref[0:1, :] * g2
        y = y + cw_ref[1:2, :] * g1
        y = y + cw_ref[2:3, :] * g0
        hmid = (y * (1.0 / (1.0 + jnp.exp(-y)))) * u
        return _dot(hmid.astype(jnp.bfloat16), wd_ref[...])

    @pl.when(f == 0)
    def _():
        acc_s[...] = mlp_chunk()

    if nf > 2:
        @pl.when((f > 0) & (f < nf - 1))
        def _():
            acc_s[...] += mlp_chunk()

    @pl.when(f == nf - 1)
    def _():
        out_ref[0] = xnew_s[...] + (acc_s[...] + mlp_chunk())


def _ffn_call(layer, x, outs, wo, fnorm, wg, wu, cw, cb, wd, tm, tf):
    bsz, seq, d = x.shape
    dff = wg.shape[2]
    n_tiles, nf = seq // tm, dff // tf
    assert seq % tm == 0 and dff % tf == 0 and nf >= 2
    tok = lambda w: pl.BlockSpec((1, tm, w), lambda b, i, f: (b, i, 0))
    return pl.pallas_call(
        functools.partial(_ffn_body, nf),
        grid=(bsz, n_tiles, nf),
        in_specs=[tok(d)] + [tok(GROUP_WIDTH)] * 5 + [
            _const_spec(wo, layer),
            _const_spec(fnorm, layer),
            pl.BlockSpec((None, d, tf), lambda b, i, f: (layer, 0, f)),
            pl.BlockSpec((None, d, tf), lambda b, i, f: (layer, 0, f)),
            pl.BlockSpec((None, SUBLANES, tf), lambda b, i, f: (layer, 0, f)),
            pl.BlockSpec((None, 1, tf), lambda b, i, f: (layer, 0, f)),
            pl.BlockSpec((None, tf, d), lambda b, i, f: (layer, f, 0))],
        out_specs=tok(d),
        out_shape=jax.ShapeDtypeStruct((bsz, seq, d), jnp.float32),
        scratch_shapes=[pltpu.VMEM((tm, d), jnp.float32), pltpu.VMEM((TAIL_ROWS + tm, d), jnp.bfloat16),
                        pltpu.VMEM((tm, d), jnp.float32)],
        compiler_params=pltpu.CompilerParams(dimension_semantics=("arbitrary", "arbitrary", "arbitrary"),
                                             vmem_limit_bytes=VMEM_LIMIT_BYTES),
        name="ffn",
    )(x, *outs, wo, fnorm, wg, wu, cw, cb, wd)


def _pad_rows(v, width=MXU_WIDTH):
    return jnp.pad(v.astype(jnp.float32), ((0, 0), (0, width - v.shape[1])))


def _tile_rows(g, reps):
    return jnp.tile(g.astype(jnp.float32), (1, reps))


def _pack_in_projection(w):
    idx = _np_in_index()
    pieces, start = [], 0
    while start < PK_END:
        stop = start + 1
        if idx[start] == _SRC_END:
            while stop < PK_END and idx[stop] == _SRC_END:
                stop += 1
            pieces.append(jnp.zeros(w.shape[:-1] + (stop - start,), w.dtype))
        else:
            while stop < PK_END and idx[stop] == idx[stop - 1] + 1:
                stop += 1
            pieces.append(w[..., int(idx[start]):int(idx[stop - 1]) + 1])
        start = stop
    return jnp.concatenate(pieces, axis=-1)


def _zero_col(w):
    return jnp.concatenate([w, jnp.zeros(w.shape[:-1] + (1,), w.dtype)], axis=-1)


def _rope_table(positions):
    pos = positions.astype(jnp.float32)[:, :, None]
    inv = [ROPE_THETA ** (-jnp.arange(0, rot, 2, dtype=jnp.float32) / rot) for rot in (MLA_ROPE, ROT_MOBA, ROT_DIFF)]
    inv = jnp.concatenate(inv + [jnp.zeros((N_FREQ - TR_ONE,), jnp.float32)])
    ang = pos * inv
    c, s = jnp.cos(ang), jnp.sin(ang)
    c_hi = c.astype(jnp.bfloat16)
    c_lo = (c - c_hi.astype(jnp.float32)).astype(jnp.bfloat16)
    s_hi = s.astype(jnp.bfloat16)
    s_lo = (s - s_hi.astype(jnp.float32)).astype(jnp.bfloat16)
    return jnp.concatenate([c_hi, c_lo, s_hi, s_lo], axis=-1)


def _pick_tile(n, pref):
    t = pref
    while n % t:
        t //= 2
    return t


def kernel(x, mem, positions, attn_norm, ffn_norm, mem_norm, w_in, mla_cq_norm, mla_ckv_norm, mla_w_uq, mla_w_ukv, mla_q_norm, mla_k_norm, fox_b_f, fox_q_norm, fox_k_norm, moba_q_norm, moba_k_norm, diff_lambda, diff_q_norm, diff_k_norm, diff_sub_norm, mem_w_kv, mem_q_norm, mem_k_norm, w_o, ffn_w_gate, ffn_w_up, ffn_conv_w, ffn_conv_b, ffn_w_down):
    bsz, seq, d = x.shape
    depth = w_in.shape[0]
    dff = ffn_w_gate.shape[2]
    bf = jnp.bfloat16
    f32 = jnp.float32

    uq_idx = _np_uq_index()
    ukvk_idx, ukvv_idx = _np_ukv_index()
    gpair = jnp.asarray(_np_group_matrix(_PAIR_GROUPS), bf)
    g64 = jnp.asarray(_np_group_matrix(_G64_GROUPS), bf)
    g32 = jnp.asarray(_np_group_matrix(_G32_GROUPS), bf)
    expand = jnp.asarray(_np_rope_expand_all(), bf)
    t_prep = max(_pick_tile(seq, PREP_TILE), MOBA_BLOCK)
    tril = jnp.asarray(np.tril(np.ones((MOBA_BLOCK, MOBA_BLOCK), np.float32)), bf)
    vsel = jnp.asarray(_np_value_select(), bf)
    tr = _rope_table(positions)

    tile = lambda cfg: max(_pick_tile(seq, cfg.tile), MOBA_BLOCK) if cfg.select else _pick_tile(seq, cfg.tile)
    t_ffn = _pick_tile(seq, 512)
    tf = dff // 2 if (dff // 2) % LANES == 0 else dff

    win = _pack_in_projection(w_in.astype(bf))
    wuq = jnp.take(_zero_col(mla_w_uq), uq_idx, axis=2)
    wuq = jnp.pad(wuq, ((0, 0), (0, MXU_WIDTH - MLA_Q_RANK), (0, 0))).astype(bf)
    wukv = _zero_col(mla_w_ukv)
    wukvk = jnp.take(wukv, ukvk_idx, axis=2).astype(bf)
    wukvv = jnp.take(wukv, ukvv_idx, axis=2).astype(bf)
    pair = lambda g: _pad_rows(_tile_rows(g, 2))
    rows = [jnp.zeros((depth, MXU_WIDTH), f32)] * P_ROWS
    rows[P_CQ] = _pad_rows(mla_cq_norm)
    rows[P_CKV] = _pad_rows(mla_ckv_norm)
    rows[P_GQ] = pair(mla_q_norm) * (LOG2E * MLA_QK ** -0.5)
    rows[P_GK] = pair(mla_k_norm)
    rows[P_FQ] = _tile_rows(fox_q_norm, N_HEADS) * (LOG2E * HEAD_DIM ** -0.5)
    rows[P_FK] = _tile_rows(fox_k_norm, N_HEADS)
    rows[P_FB] = _pad_rows(fox_b_f)
    rows[P_MQ] = _tile_rows(moba_q_norm, N_HEADS)
    rows[P_MK] = _tile_rows(moba_k_norm, N_HEADS)
    rows[P_DQ] = _tile_rows(diff_q_norm, 2 * N_HEADS) * (LOG2E * DIFF_QK ** -0.5)
    rows[P_DK] = _tile_rows(diff_k_norm, 2 * N_HEADS)
    rows[P_EQ] = _tile_rows(mem_q_norm, N_HEADS) * (LOG2E * HEAD_DIM ** -0.5)
    par = jnp.stack(rows, axis=1)
    anorm = attn_norm.astype(f32)[:, None, :]
    mnorm = mem_norm.astype(f32)[:, None, :]
    fnorm = ffn_norm.astype(f32)[:, None, :]
    wmem = mem_w_kv.astype(bf)
    mem_gain = _tile_rows(mem_k_norm, N_HEADS)[:, None, :]

    lam_init = jnp.asarray([0.8 - 0.6 * math.exp(-0.3 * l) for l in range(depth)], f32)
    lam_vec = diff_lambda.astype(f32)
    lam = (jnp.exp(jnp.sum(lam_vec[:, 0] * lam_vec[:, 1], axis=-1))
           - jnp.exp(jnp.sum(lam_vec[:, 2] * lam_vec[:, 3], axis=-1)) + lam_init)
    lam_row = jnp.broadcast_to(lam[:, None, None], (depth, 1, GROUP_WIDTH))
    gsub = (_tile_rows(diff_sub_norm, N_HEADS) * (1.0 - lam_init)[:, None])[:, None, :]

    wo = w_o.astype(bf)
    wg, wu, wd = ffn_w_gate.astype(bf), ffn_w_up.astype(bf), ffn_w_down.astype(bf)
    cw = jnp.pad(ffn_conv_w.astype(f32), ((0, 0), (0, SUBLANES - CONV_WIDTH), (0, 0)))
    cb = ffn_conv_b.astype(f32)[:, None, :]

    for l in range(depth):
        (aq, ak, av, fq, fk, fv, fdcol, fdrow, mq, mk, mv, msel, dq, dk, dv, eq) = _prep_call(
            l, t_prep, x, tr, anorm, win, wuq, wukvk, wukvv, gpair, g64, g32, expand, tril, vsel, par)
        ek, ev = _memkv_call(l, mem, mnorm, wmem, g64, mem_gain, vsel)

        o_a = _attn_call(_CFG_MLA, aq, ak, av, (), tile(_CFG_MLA), tile(_CFG_MLA))
        o_b = _attn_call(_CFG_FOX, fq, fk, fv, (fdcol, fdrow), tile(_CFG_FOX), tile(_CFG_FOX))
        o_c = _attn_call(_CFG_MOBA, mq, mk, mv, (msel,), tile(_CFG_MOBA), tile(_CFG_MOBA))
        o_d = _attn_call(_CFG_DIFF, dq, dk, dv, (l, g64, gsub, lam_row), tile(_CFG_DIFF), tile(_CFG_DIFF))
        o_e = _attn_call(_CFG_MEM, eq, ek, ev, (), tile(_CFG_MEM), mem.shape[1])

        x = _ffn_call(l, x, (o_a, o_b, o_c, o_d, o_e), wo, fnorm, wg, wu, cw, cb, wd, t_ffn, tf)
    return x
```

```python
import functools
import math

import numpy as np
import jax
import jax.numpy as jnp
from jax import lax
from jax.experimental import pallas as pl
from jax.experimental.pallas import tpu as pltpu

N_HEADS = 4
HEAD_DIM = 64
GROUP_WIDTH = N_HEADS * HEAD_DIM
MLA_Q_RANK = 192
MLA_KV_RANK = 128
MLA_NOPE = 64
MLA_ROPE = 32
MLA_QK = MLA_NOPE + MLA_ROPE
DIFF_QK = HEAD_DIM // 2
ROPE_THETA = 500000.0
ROT_MOBA = HEAD_DIM // 4
ROT_DIFF = DIFF_QK // 4
MOBA_BLOCK = 256
MOBA_TOPK = 3
CONV_WIDTH = 3
EPS = 1e-6
NEG_INF = -1e30
LOG2E = math.log2(math.e)
REMOVED = -3e38

LANES = 128
SUBLANES = 8
MXU_WIDTH = 256
TAIL_ROWS = 16
PREP_TILE = 512
VMEM_LIMIT_BYTES = 56 * 1024 * 1024

_SRC_CQ = 0
_SRC_CKV = _SRC_CQ + MLA_Q_RANK
_SRC_KR = _SRC_CKV + MLA_KV_RANK
_SRC_FOX = _SRC_KR + MLA_ROPE
_SRC_FOXF = _SRC_FOX + 3 * GROUP_WIDTH
_SRC_MOBA = _SRC_FOXF + N_HEADS
_SRC_DIFF = _SRC_MOBA + 3 * GROUP_WIDTH
_SRC_MEMQ = _SRC_DIFF + 3 * GROUP_WIDTH
_SRC_END = _SRC_MEMQ + GROUP_WIDTH

PK_CQ = 0
PK_CKV = 256
PK_KR = 384
PK_FQ, PK_FK, PK_FV = 896, 1152, 1408
PK_FF = 1664
PK_MQ, PK_MK, PK_MV = 1792, 2048, 2304
PK_DQ, PK_DK, PK_DV = 2560, 2816, 3072
PK_EQ = 3328
PK_END = 3584

PAIR_STRIDE = MLA_QK


def _pair_lane(h, d):
    return (h // 2) * MXU_WIDTH + (h % 2) * PAIR_STRIDE + d


N_FREQ = 32
FREQ_BASE_MLA = 0
FREQ_BASE_MOBA = MLA_ROPE // 2
FREQ_BASE_DIFF = FREQ_BASE_MOBA + ROT_MOBA // 2
TR_ONE = FREQ_BASE_DIFF + ROT_DIFF // 2
TR_WIDTH = 4 * N_FREQ
assert TR_ONE < N_FREQ and TR_WIDTH == LANES

(P_CQ, P_CKV, P_GQ, P_GK, P_FQ, P_FK, P_FB, P_MQ, P_MK, P_DQ, P_DK, P_EQ) = range(12)
P_ROWS = 16


def _np_in_index():
    idx = np.full((PK_END,), _SRC_END, np.int32)
    idx[PK_CQ:PK_CQ + MLA_Q_RANK] = np.arange(_SRC_CQ, _SRC_CQ + MLA_Q_RANK)
    idx[PK_CKV:PK_CKV + MLA_KV_RANK] = np.arange(_SRC_CKV, _SRC_CKV + MLA_KV_RANK)
    for h in range(N_HEADS):
        for d in range(MLA_ROPE):
            idx[PK_KR + _pair_lane(h, d)] = _SRC_KR + d
    idx[PK_FQ:PK_FQ + 3 * GROUP_WIDTH] = np.arange(_SRC_FOX, _SRC_FOX + 3 * GROUP_WIDTH)
    idx[PK_FF:PK_FF + N_HEADS] = np.arange(_SRC_FOXF, _SRC_FOXF + N_HEADS)
    idx[PK_MQ:PK_MQ + 3 * GROUP_WIDTH] = np.arange(_SRC_MOBA, _SRC_MOBA + 3 * GROUP_WIDTH)
    idx[PK_DQ:PK_DQ + 3 * GROUP_WIDTH] = np.arange(_SRC_DIFF, _SRC_DIFF + 3 * GROUP_WIDTH)
    idx[PK_EQ:PK_EQ + GROUP_WIDTH] = np.arange(_SRC_MEMQ, _SRC_MEMQ + GROUP_WIDTH)
    return idx


def _np_uq_index():
    idx = np.full((2 * MXU_WIDTH,), N_HEADS * MLA_QK, np.int32)
    for h in range(N_HEADS):
        for d in range(MLA_QK):
            idx[_pair_lane(h, d)] = h * MLA_QK + d
    return idx


def _np_ukv_index():
    zero = N_HEADS * (MLA_NOPE + HEAD_DIM)
    idx_k = np.full((2 * MXU_WIDTH,), zero, np.int32)
    idx_v = np.zeros((GROUP_WIDTH,), np.int32)
    for h in range(N_HEADS):
        for d in range(MLA_NOPE):
            idx_k[_pair_lane(h, MLA_ROPE + d)] = h * (MLA_NOPE + HEAD_DIM) + d
        for d in range(HEAD_DIM):
            idx_v[h * HEAD_DIM + d] = h * (MLA_NOPE + HEAD_DIM) + MLA_NOPE + d
    return idx_k, idx_v


def _np_group_matrix(groups):
    g = np.zeros((MXU_WIDTH, MXU_WIDTH), np.float32)
    for lo, size in groups:
        g[lo:lo + size, lo:lo + size] = 1.0 / size
    return g


_PAIR_GROUPS = [(0, MLA_ROPE), (MLA_ROPE, MLA_NOPE), (PAIR_STRIDE, MLA_ROPE), (PAIR_STRIDE + MLA_ROPE, MLA_NOPE)]
_G64_GROUPS = [(h * HEAD_DIM, HEAD_DIM) for h in range(N_HEADS)]
_G32_GROUPS = [(g * DIFF_QK, DIFF_QK) for g in range(2 * N_HEADS)]


def _np_rope_expand(regions, rot, base):
    half = rot // 2
    e = np.zeros((TR_WIDTH, 2 * MXU_WIDTH), np.float32)
    e[TR_ONE, 0:MXU_WIDTH] = 1.0
    for lo in regions:
        assert lo % rot == 0
        for r in range(half):
            f = base + r
            for lane, sign in ((lo + r, -1.0), (lo + half + r, 1.0)):
                e[TR_ONE, lane] = 0.0
                e[f, lane] = 1.0
                e[N_FREQ + f, lane] = 1.0
                e[2 * N_FREQ + f, MXU_WIDTH + lane] = sign
                e[3 * N_FREQ + f, MXU_WIDTH + lane] = sign
    return e


def _np_rope_expand_all():
    return np.concatenate([
        _np_rope_expand([0, PAIR_STRIDE], MLA_ROPE, FREQ_BASE_MLA),
        _np_rope_expand([h * HEAD_DIM for h in range(N_HEADS)], ROT_MOBA, FREQ_BASE_MOBA),
        _np_rope_expand([g * DIFF_QK for g in range(2 * N_HEADS)], ROT_DIFF, FREQ_BASE_DIFF),
    ], axis=1)


def _dot(a, b):
    return jnp.dot(a, b, preferred_element_type=jnp.float32)


def _dot_nt(a, b):
    return lax.dot_general(a, b, (((1,), (1,)), ((), ())), preferred_element_type=jnp.float32)


def _split2(a):
    hi = a.astype(jnp.bfloat16)
    lo = (a - hi.astype(jnp.float32)).astype(jnp.bfloat16)
    return hi, lo


def _split3(a):
    hi = a.astype(jnp.bfloat16)
    r = a - hi.astype(jnp.float32)
    mid = r.astype(jnp.bfloat16)
    lo = (r - mid.astype(jnp.float32)).astype(jnp.bfloat16)
    return hi, mid, lo


def _group_mean_sq(a, g_bf16):
    return _dot((a * a).astype(jnp.bfloat16), g_bf16)


def _rope(x, tabs, half):
    w = x.shape[-1]
    lane = lax.broadcasted_iota(jnp.int32, x.shape, 1)
    partner = jnp.where((lane & (2 * half - 1)) >= half, pltpu.roll(x, half, 1), pltpu.roll(x, w - half, 1))
    return x * tabs[:, 0:w] + partner * tabs[:, w:2 * w]


def _lane_mask(shape, lo, hi):
    lane = lax.broadcasted_iota(jnp.int32, shape, len(shape) - 1)
    return (lane >= lo) & (lane < hi)


def _prep_body(x_ref, tr_ref, anorm_ref, win_ref, wuq_ref, wukvk_ref, wukvv_ref, gpair_ref, g64_ref, g32_ref,
               exp_ref, tril_ref, vsel_ref, par_ref,
               aq_ref, ak_ref, av_ref, fq_ref, fk_ref, fv_ref, fdcol_ref, fdrow_ref, mq_ref, mk_ref, mv_ref, msel_ref,
               dq_ref, dk_ref, dv_ref, eq_ref,
               kmean_s, carry_s):
    j = pl.program_id(1)
    tm = x_ref.shape[1]
    pr = MOBA_BLOCK
    n_parts = tm // pr
    nbp = kmean_s.shape[0]
    bf = jnp.bfloat16
    gpair, g64, g32 = gpair_ref[...], g64_ref[...], g32_ref[...]
    vsel = vsel_ref[...]
    tril = tril_ref[...]

    @pl.when(j == 0)
    def _():
        kmean_s[...] = jnp.zeros_like(kmean_s)
        carry_s[...] = jnp.zeros_like(carry_s)

    def prow(r, width=MXU_WIDTH):
        return par_ref[r:r + 1, 0:width]

    def project(st):
        rows = st["rows"]
        x = x_ref[0, rows, :]
        xb = (x * lax.rsqrt(jnp.mean(x * x, axis=-1, keepdims=True) + EPS) * anorm_ref[...]).astype(bf)
        tabs = _dot(tr_ref[0, rows, :], exp_ref[...])
        st["tab_mla"] = tabs[:, 0:2 * MXU_WIDTH]
        st["tab_moba"] = tabs[:, 2 * MXU_WIDTH:4 * MXU_WIDTH]
        st["tab_diff"] = tabs[:, 4 * MXU_WIDTH:6 * MXU_WIDTH]

        def run(lo, hi):
            wide = _dot(xb, win_ref[:, lo:hi])
            return lambda off, width: wide[:, off - lo:off - lo + width]

        run_m = run(PK_MQ, PK_DQ)
        run_a = run(PK_CQ, PK_FQ)
        run_f = run(PK_FQ, PK_MQ)
        run_d = run(PK_DQ, PK_END)
        st["direct"] = {"mq": run_m(PK_MQ, MXU_WIDTH), "mk": run_m(PK_MK, MXU_WIDTH),
                        "fq": run_f(PK_FQ, MXU_WIDTH), "fk": run_f(PK_FK, MXU_WIDTH),
                        "dq": run_d(PK_DQ, MXU_WIDTH), "dk": run_d(PK_DK, MXU_WIDTH),
                        "eq": run_d(PK_EQ, MXU_WIDTH)}
        st["values"] = {"fv": run_f(PK_FV, MXU_WIDTH).astype(bf), "mv": run_m(PK_MV, MXU_WIDTH).astype(bf),
                        "dv": run_d(PK_DV, MXU_WIDTH).astype(bf)}
        p_cq, p_ckv, st["p_kr"] = run_a(PK_CQ, MXU_WIDTH), run_a(PK_CKV, MLA_KV_RANK), run_a(PK_KR, 2 * MXU_WIDTH)
        cqn = p_cq * lax.rsqrt(jnp.sum(p_cq * p_cq, axis=-1, keepdims=True) * (1.0 / MLA_Q_RANK) + EPS) * prow(P_CQ)
        ckvn = p_ckv * lax.rsqrt(jnp.mean(p_ckv * p_ckv, axis=-1, keepdims=True) + EPS) * prow(P_CKV, MLA_KV_RANK)
        st["cqb"], st["ckvb"] = cqn.astype(bf), ckvn.astype(bf)
        z = run_f(PK_FF, LANES) + prow(P_FB, LANES)
        log_f = jnp.minimum(z, 0.0) - jnp.log1p(jnp.exp(-jnp.abs(z)))
        st["log_f"] = _split3(jnp.where(_lane_mask(log_f.shape, 0, N_HEADS), log_f, 0.0))

    def second(st):
        gmat = {"fq": g64, "fk": g64, "mq": g64, "mk": g64, "dq": g32, "dk": g32, "eq": g64}
        gains = {"fq": P_FQ, "fk": P_FK, "mq": P_MQ, "mk": P_MK, "dq": P_DQ, "dk": P_DK, "eq": P_EQ}
        ms = {name: _group_mean_sq(a, gmat[name]) for name, a in st["direct"].items()}
        st["qa"] = _dot(st["cqb"], wuq_ref[...])
        st["ka"] = st["p_kr"] + _dot(st["ckvb"], wukvk_ref[...])
        st["values"]["av"] = _dot(st["ckvb"], wukvv_ref[...]).astype(bf)
        l1, l2, l3 = st["log_f"]
        st["cumsum"] = (_dot(tril, l1) + _dot(tril, l2)) + _dot(tril, l3)
        st["normed"] = {name: a * lax.rsqrt(ms[name] + EPS) * prow(gains[name]) for name, a in st["direct"].items()}

    def third(st):
        rows = st["rows"]
        mq = _rope(st["normed"]["mq"], st["tab_moba"], ROT_MOBA // 2)
        mk = _rope(st["normed"]["mk"], st["tab_moba"], ROT_MOBA // 2)
        st["mq"], st["mk"] = mq, mk
        kmean_s[pl.ds(st["blk"], 1), :] = jnp.mean(mk, axis=0, keepdims=True)
        km_hi, km_lo = _split2(kmean_s[...])
        st["gates"] = []
        for h in range(N_HEADS):
            q_hi, q_lo = _split2(jnp.where(_lane_mask(mq.shape, h * HEAD_DIM, (h + 1) * HEAD_DIM), mq, 0.0))
            st["gates"].append((_dot_nt(km_hi, q_hi) + _dot_nt(km_lo, q_hi)) + _dot_nt(km_hi, q_lo))
        st["ms_qa"] = [_group_mean_sq(st["qa"][:, p * MXU_WIDTH:(p + 1) * MXU_WIDTH], gpair) for p in range(2)]
        st["ms_ka"] = [_group_mean_sq(st["ka"][:, p * MXU_WIDTH:(p + 1) * MXU_WIDTH], gpair) for p in range(2)]
        dec = carry_s[...] + st["cumsum"]
        carry_s[...] = dec[pr - 1:pr, :]
        dec2 = dec * LOG2E
        d1, d2, d3 = _split3(dec2)
        for h in range(N_HEADS):
            fdcol_ref[0, rows, h * LANES:(h + 1) * LANES] = jnp.broadcast_to(dec2[:, h:h + 1], (pr, LANES))
        row_sel = jnp.where(lax.broadcasted_iota(jnp.int32, (SUBLANES, LANES), 0)
                            == lax.broadcasted_iota(jnp.int32, (SUBLANES, LANES), 1), 1.0, 0.0).astype(bf)
        fdrow_ref[0, :, rows] = (_dot_nt(row_sel, d1) + _dot_nt(row_sel, d2)) + _dot_nt(row_sel, d3)
        for name, ref in (("fv", fv_ref), ("mv", mv_ref), ("dv", dv_ref), ("av", av_ref)):
            ref[0, :, rows] = _values_t(st["values"][name], vsel)

    def finish(st):
        rows = st["rows"]
        normed = st["normed"]
        fq_ref[0, rows, :] = normed["fq"].astype(bf)
        fk_ref[0, rows, :] = normed["fk"].astype(bf)
        eq_ref[0, rows, :] = normed["eq"].astype(bf)
        dq_ref[0, rows, :] = _rope(normed["dq"], st["tab_diff"], ROT_DIFF // 2).astype(bf)
        dk_ref[0, rows, :] = _rope(normed["dk"], st["tab_diff"], ROT_DIFF // 2).astype(bf)
        mq_ref[0, rows, :] = (st["mq"] * (LOG2E * HEAD_DIM ** -0.5)).astype(bf)
        mk_ref[0, rows, :] = st["mk"].astype(bf)
        for p in range(2):
            sl = slice(p * MXU_WIDTH, (p + 1) * MXU_WIDTH)
            aq_ref[0, rows, sl] = _rope(st["qa"][:, sl] * lax.rsqrt(st["ms_qa"][p] + EPS) * prow(P_GQ),
                                        st["tab_mla"], MLA_ROPE // 2).astype(bf)
            ak_ref[0, rows, sl] = _rope(st["ka"][:, sl] * lax.rsqrt(st["ms_ka"][p] + EPS) * prow(P_GK),
                                        st["tab_mla"], MLA_ROPE // 2).astype(bf)
        blk = lax.broadcasted_iota(jnp.int32, (nbp, pr), 0)
        past = blk < st["blk"]
        for h in range(N_HEADS):
            work = jnp.where(past, st["gates"][h], NEG_INF)
            sel = jnp.zeros((nbp, pr), jnp.bool_)
            for _ in range(MOBA_TOPK):
                mx = jnp.max(work, axis=0, keepdims=True)
                first = jnp.min(jnp.where(work == mx, blk, nbp), axis=0, keepdims=True)
                pick = blk == first
                sel = sel | pick
                work = jnp.where(pick, REMOVED, work)
            msel_ref[0, h * nbp:(h + 1) * nbp, rows] = jnp.where(sel & past, 0.0, NEG_INF)

    parts = [{"rows": slice(p * pr, (p + 1) * pr), "blk": j * n_parts + p} for p in range(n_parts)]
    for phase in (project, second, third, finish):
        for st in parts:
            phase(st)


def _const_spec(a, layer=None):
    if layer is None:
        n = a.ndim
        return pl.BlockSpec(a.shape, lambda *_: (0,) * n)
    n = a.ndim - 1
    return pl.BlockSpec((None,) + a.shape[1:], lambda *_: (layer,) + (0,) * n)


def _moba_blocks_padded(seq):
    return -(-(seq // MOBA_BLOCK) // SUBLANES) * SUBLANES


def _prep_call(layer, tm, x, tr, anorm, win, wuq, wukvk, wukvv, gpair, g64, g32, expand, tril, vsel, par):
    bsz, seq, d = x.shape
    assert seq % tm == 0 and tm % MOBA_BLOCK == 0 and tril.shape == (MOBA_BLOCK, MOBA_BLOCK)
    nbp = _moba_blocks_padded(seq)
    bf = jnp.bfloat16
    f32 = jnp.float32

    vt = -N_HEADS * V_ROWS
    widths = [(2 * MXU_WIDTH, bf), (2 * MXU_WIDTH, bf), (vt, bf),
              (GROUP_WIDTH, bf), (GROUP_WIDTH, bf), (vt, bf),
              (N_HEADS * LANES, f32), (-SUBLANES, f32),
              (GROUP_WIDTH, bf), (GROUP_WIDTH, bf), (vt, bf), (-N_HEADS * nbp, f32),
              (GROUP_WIDTH, bf), (GROUP_WIDTH, bf), (vt, bf),
              (GROUP_WIDTH, bf)]

    def tok(width):
        if width > 0:
            return pl.BlockSpec((1, tm, width), lambda b, j: (b, j, 0))
        return pl.BlockSpec((1, -width, tm), lambda b, j: (b, 0, j))

    def shape(width):
        return (bsz, seq, width) if width > 0 else (bsz, -width, seq)

    consts = [anorm, win, wuq, wukvk, wukvv, gpair, g64, g32, expand, tril, vsel, par]
    layered = [True, True, True, True, True, False, False, False, False, False, False, True]
    return pl.pallas_call(
        _prep_body,
        grid=(bsz, seq // tm),
        in_specs=[tok(d), tok(TR_WIDTH)] + [_const_spec(c, layer if ly else None) for c, ly in zip(consts, layered)],
        out_specs=[tok(w) for w, _ in widths],
        out_shape=[jax.ShapeDtypeStruct(shape(w), dt) for w, dt in widths],
        scratch_shapes=[pltpu.VMEM((nbp, GROUP_WIDTH), jnp.float32), pltpu.VMEM((1, LANES), jnp.float32)],
        compiler_params=pltpu.CompilerParams(dimension_semantics=("arbitrary", "arbitrary"),
                                             vmem_limit_bytes=VMEM_LIMIT_BYTES),
        name="prep",
    )(x, tr, *consts)


def _memkv_body(mem_ref, mnorm_ref, w_ref, g64_ref, gain_ref, vsel_ref, k_ref, v_ref):
    m = mem_ref[0]
    mn = m * lax.rsqrt(jnp.mean(m * m, axis=-1, keepdims=True) + EPS) * mnorm_ref[...]
    kv = _dot(mn.astype(jnp.bfloat16), w_ref[...])
    k = kv[:, 0:GROUP_WIDTH]
    k = k * lax.rsqrt(_group_mean_sq(k, g64_ref[...]) + EPS) * gain_ref[...]
    k_ref[0] = k.astype(jnp.bfloat16)
    v_ref[0] = _values_t(kv[:, GROUP_WIDTH:2 * GROUP_WIDTH].astype(jnp.bfloat16), vsel_ref[...])


def _memkv_call(layer, mem, mnorm, w, g64, gain, vsel):
    bsz, mlen, d = mem.shape
    k_shape, vt_shape = (bsz, mlen, GROUP_WIDTH), (bsz, N_HEADS * V_ROWS, mlen)
    return pl.pallas_call(
        _memkv_body,
        grid=(bsz,),
        in_specs=[pl.BlockSpec((1, mlen, d), lambda b: (b, 0, 0)), _const_spec(mnorm, layer), _const_spec(w, layer),
                  _const_spec(g64), _const_spec(gain, layer), _const_spec(vsel)],
        out_specs=[pl.BlockSpec((1,) + s[1:], lambda b: (b, 0, 0)) for s in (k_shape, vt_shape)],
        out_shape=[jax.ShapeDtypeStruct(s, jnp.bfloat16) for s in (k_shape, vt_shape)],
        compiler_params=pltpu.CompilerParams(dimension_semantics=("arbitrary",), vmem_limit_bytes=VMEM_LIMIT_BYTES),
        name="mem_kv",
    )(mem, mnorm, w, g64, gain, vsel)


Q_SUB = 2 * MXU_WIDTH


class _AttnCfg:
    def __init__(self, name, vheads, n_maps, causal, decay=False, select=False, diff=False, tile=1024, lookahead=8,
                 q_sub=Q_SUB):
        self.name = name
        self.vheads = vheads
        self.n_maps = n_maps
        self.causal = causal
        self.decay = decay
        self.select = select
        self.diff = diff
        self.tile = tile
        self.lookahead = lookahead
        self.q_sub = q_sub


_PLAIN_VHEADS = [(0, h * HEAD_DIM, (h + 1) * HEAD_DIM, 0, h) for h in range(N_HEADS)]
_CFG_MLA = _AttnCfg("attn_mla", [((h // 2) * MXU_WIDTH, (h % 2) * PAIR_STRIDE, (h % 2) * PAIR_STRIDE + MLA_QK, 0, h)
                                 for h in range(N_HEADS)], 1, True, q_sub=MXU_WIDTH, lookahead=16)
_CFG_FOX = _AttnCfg("attn_fox", _PLAIN_VHEADS, 1, True, decay=True, q_sub=MXU_WIDTH, lookahead=16)
_CFG_MOBA = _AttnCfg("attn_moba", _PLAIN_VHEADS, 1, True, select=True, q_sub=MXU_WIDTH, lookahead=16)
_CFG_DIFF = _AttnCfg("attn_diff", [(0, h * HEAD_DIM + c * DIFF_QK, h * HEAD_DIM + (c + 1) * DIFF_QK, c, h)
                                   for c in range(2) for h in range(N_HEADS)], 2, True, diff=True, q_sub=MXU_WIDTH,
                     lookahead=16)
_CFG_MEM = _AttnCfg("attn_mem", _PLAIN_VHEADS, 1, False, tile=2048)


ONES_ROW = HEAD_DIM
V_ROWS = HEAD_DIM + 16


def _np_value_select():
    sel = np.zeros((N_HEADS * V_ROWS, GROUP_WIDTH), np.float32)
    for h in range(N_HEADS):
        for d in range(HEAD_DIM):
            sel[h * V_ROWS + d, h * HEAD_DIM + d] = 1.0
    return sel


def _values_t(v, vsel):
    vt = _dot_nt(vsel, v)
    row = lax.broadcasted_iota(jnp.int32, vt.shape, 0)
    ones = row == ONES_ROW
    for h in range(1, N_HEADS):
        ones = ones | (row == h * V_ROWS + ONES_ROW)
    return jnp.where(ones, 1.0, vt).astype(jnp.bfloat16)


def _tile_lanes(x, width):
    return jnp.tile(x, (1, width // LANES)) if width != LANES else x


def _attn_body(cfg, qi_ref, kj_ref, *refs):
    refs = list(refs)
    q_ref, k_ref, vt_ref = refs[:3]
    pos = 3
    if cfg.decay:
        dq_ref, dk_ref = refs[pos:pos + 2]
        pos += 2
    if cfg.select:
        sel_ref = refs[pos]
        pos += 1
    if cfg.diff:
        g64_ref, gsub_ref, lam_ref = refs[pos:pos + 3]
        pos += 3
    o_ref, qm_s, m_s, acc_s = refs[pos:pos + 4]

    t = pl.program_id(1)
    i = qi_ref[t]
    j = kj_ref[t]
    tq = q_ref.shape[1]
    tk = k_ref.shape[1]

    @pl.when(j == 0)
    def _():
        for n, (off, lo, hi, _, _) in enumerate(cfg.vheads):
            qb = q_ref[0, :, off:off + MXU_WIDTH]
            qm_s[n] = jnp.where(_lane_mask(qb.shape, lo, hi), qb, jnp.zeros_like(qb))
        m_s[...] = jnp.full(m_s.shape, NEG_INF, jnp.float32)
        acc_s[...] = jnp.zeros_like(acc_s)

    def step(diag):
        qs = min(tq, cfg.q_sub)
        items = [(n, u) for n in range(len(cfg.vheads)) for u in range(tq // qs)]

        def n_keys(u):
            return (u + 1) * qs if diag else tk

        def scores(item):
            n, u = item
            off, _, _, _, h = cfg.vheads[n]
            nk = n_keys(u)
            cols = slice(u * qs, (u + 1) * qs)
            s = _dot_nt(k_ref[0, 0:nk, off:off + MXU_WIDTH], qm_s[n, cols, :])
            if cfg.decay:
                s = (dq_ref[0, h:h + 1, cols] - _tile_lanes(dk_ref[0, 0:nk, h * LANES:(h + 1) * LANES], qs)) + s
            if cfg.select:
                nbp = sel_ref.shape[1] // N_HEADS
                qpos = u * qs + lax.broadcasted_iota(jnp.int32, (1, qs), 1)
                parts = []
                for kb in range(nk // MOBA_BLOCK):
                    rows = s[kb * MOBA_BLOCK:(kb + 1) * MOBA_BLOCK, :]
                    if not (diag and kb == nk // MOBA_BLOCK - 1):
                        bias = sel_ref[0, pl.ds(h * nbp + j * (tk // MOBA_BLOCK) + kb, 1), cols]
                        if diag:
                            bias = jnp.where(qpos < (kb + 1) * MOBA_BLOCK, 0.0, bias)
                        rows = rows + bias
                    parts.append(rows)
                s = parts[0] if len(parts) == 1 else jnp.concatenate(parts, axis=0)
            if diag:
                key = lax.broadcasted_iota(jnp.int32, (nk, qs), 0)
                qry = u * qs + lax.broadcasted_iota(jnp.int32, (nk, qs), 1)
                s = jnp.where(key <= qry, s, NEG_INF)
            return s, jnp.max(s, axis=0, keepdims=True)

        raw = {it: scores(items[it]) for it in range(min(cfg.lookahead, len(items)))}
        for it, (n, u) in enumerate(items):
            h = cfg.vheads[n][4]
            nk = n_keys(u)
            cols = slice(u * qs, (u + 1) * qs)
            s, s_max = raw.pop(it)
            m_prev = m_s[n, :, cols]
            m_new = jnp.maximum(m_prev, s_max)
            alpha = jnp.exp2(m_prev - m_new)
            p = jnp.exp2(s - m_new)
            m_s[n, :, cols] = m_new
            acc_s[n, :, cols] = acc_s[n, :, cols] * alpha + _dot(vt_ref[0, h * V_ROWS:(h + 1) * V_ROWS, 0:nk],
                                                                 p.astype(jnp.bfloat16))
            if it + cfg.lookahead < len(items):
                raw[it + cfg.lookahead] = scores(items[it + cfg.lookahead])

    if cfg.causal:
        pl.when(j < i)(functools.partial(step, False))
        pl.when(j == i)(functools.partial(step, True))
        last = j == i
    else:
        step(False)
        last = j == 0

    @pl.when(last)
    def _():
        outs = []
        for c in range(cfg.n_maps):
            heads = []
            for h in range(N_HEADS):
                acc = acc_s[c * N_HEADS + h]
                heads.append(acc[0:HEAD_DIM, :] / acc[ONES_ROW:ONES_ROW + 1, :])
            outs.append(jnp.concatenate(heads, axis=0).T)
        if cfg.diff:
            o = outs[0] - lam_ref[0:1, :] * outs[1]
            o = o * lax.rsqrt(_group_mean_sq(o, g64_ref[...]) + EPS) * gsub_ref[...]
        else:
            o = outs[0]
        o_ref[0] = o.astype(o_ref.dtype)


def _attn_call(cfg, q, k, v, extras, tq, tk):
    bsz, seq, wq = q.shape
    sk = k.shape[1]
    nq = seq // tq
    assert seq % tq == 0 and sk % tk == 0
    if cfg.causal:
        assert tq == tk and sk == seq
        pairs = [(i, j) for i in range(nq) for j in range(i + 1)]
    else:
        assert sk == tk
        pairs = [(i, 0) for i in range(nq)]
    qi = jnp.asarray(np.array([p[0] for p in pairs], np.int32))
    kj = jnp.asarray(np.array([p[1] for p in pairs], np.int32))
    n_vh = len(cfg.vheads)

    in_specs = [pl.BlockSpec((1, tq, wq), lambda b, t, qi, kj: (b, qi[t], 0)),
                pl.BlockSpec((1, tk, wq), lambda b, t, qi, kj: (b, kj[t], 0)),
                pl.BlockSpec((1, N_HEADS * V_ROWS, tk), lambda b, t, qi, kj: (b, 0, kj[t]))]
    args = [q, k, v]
    if cfg.decay:
        dcol, drow = extras
        in_specs += [pl.BlockSpec((1, SUBLANES, tq), lambda b, t, qi, kj: (b, 0, qi[t])),
                     pl.BlockSpec((1, tk, N_HEADS * LANES), lambda b, t, qi, kj: (b, kj[t], 0))]
        args += [drow, dcol]
    if cfg.select:
        (sel,) = extras
        in_specs += [pl.BlockSpec((1, sel.shape[1], tq), lambda b, t, qi, kj: (b, 0, qi[t]))]
        args += [sel]
    if cfg.diff:
        layer, g64, gsub, lam_row = extras
        in_specs += [_const_spec(g64), _const_spec(gsub, layer), _const_spec(lam_row, layer)]
        args += [g64, gsub, lam_row]

    grid_spec = pltpu.PrefetchScalarGridSpec(
        num_scalar_prefetch=2,
        grid=(bsz, len(pairs)),
        in_specs=in_specs,
        out_specs=pl.BlockSpec((1, tq, GROUP_WIDTH), lambda b, t, qi, kj: (b, qi[t], 0)),
        scratch_shapes=[pltpu.VMEM((n_vh, tq, MXU_WIDTH), jnp.bfloat16),
                        pltpu.VMEM((n_vh, 1, tq), jnp.float32),
                        pltpu.VMEM((n_vh, V_ROWS, tq), jnp.float32)])
    return pl.pallas_call(
        functools.partial(_attn_body, cfg),
        grid_spec=grid_spec,
        out_shape=jax.ShapeDtypeStruct((bsz, seq, GROUP_WIDTH), jnp.bfloat16),
        compiler_params=pltpu.CompilerParams(dimension_semantics=("arbitrary", "arbitrary"),
                                             vmem_limit_bytes=VMEM_LIMIT_BYTES),
        name=cfg.name,
    )(qi, kj, *args)


def _ffn_body(nf, x_ref, oa_ref, ob_ref, oc_ref, od_ref, oe_ref, wo_ref, fnorm_ref, wg_ref, wu_ref, cw_ref, cb_ref,
              wd_ref, out_ref, xnew_s, xn_s, acc_s):
    i = pl.program_id(1)
    f = pl.program_id(2)
    tm = x_ref.shape[1]

    @pl.when(f == 0)
    def _():
        @pl.when(i == 0)
        def _():
            xn_s[0:TAIL_ROWS, :] = jnp.zeros((TAIL_ROWS, xn_s.shape[1]), xn_s.dtype)

        @pl.when(i > 0)
        def _():
            xn_s[0:TAIL_ROWS, :] = xn_s[tm:tm + TAIL_ROWS, :]

        mixed = jnp.concatenate([o_ref[0] for o_ref in (oa_ref, ob_ref, oc_ref, od_ref, oe_ref)], axis=1)
        xnew = x_ref[0] + _dot(mixed, wo_ref[...])
        xnew_s[...] = xnew
        xn = xnew * lax.rsqrt(jnp.mean(xnew * xnew, axis=-1, keepdims=True) + EPS) * fnorm_ref[...]
        xn_s[TAIL_ROWS:TAIL_ROWS + tm, :] = xn.astype(xn_s.dtype)

    def mlp_chunk():
        ge = _dot(xn_s[...], wg_ref[...])
        u = _dot(xn_s[TAIL_ROWS:TAIL_ROWS + tm, :], wu_ref[...])
        g0 = ge[TAIL_ROWS:TAIL_ROWS + tm, :]
        t1 = ge[TAIL_ROWS - 1:TAIL_ROWS, :]
        t2 = ge[TAIL_ROWS - 2:TAIL_ROWS - 1, :]
        row = lax.broadcasted_iota(jnp.int32, g0.shape, 0)
        g1 = jnp.where(row == 0, t1, pltpu.roll(g0, 1, 0))
        g2 = jnp.where(row == 0, t2, jnp.where(row == 1, t1, pltpu.roll(g0, 2, 0)))
        y = cb_ref[...] + cw_ref[0:1, :] * g2
        y = y + cw_ref[1:2, :] * g1
        y = y + cw_ref[2:3, :] * g0
        hmid = (y * (1.0 / (1.0 + jnp.exp(-y)))) * u
        return _dot(hmid.astype(jnp.bfloat16), wd_ref[...])

    @pl.when(f == 0)
    def _():
        acc_s[...] = mlp_chunk()

    if nf > 2:
        @pl.when((f > 0) & (f < nf - 1))
        def _():
            acc_s[...] += mlp_chunk()

    @pl.when(f == nf - 1)
    def _():
        out_ref[0] = xnew_s[...] + (acc_s[...] + mlp_chunk())


def _ffn_call(layer, x, outs, wo, fnorm, wg, wu, cw, cb, wd, tm, tf):
    bsz, seq, d = x.shape
    dff = wg.shape[2]
    n_tiles, nf = seq // tm, dff // tf
    assert seq % tm == 0 and dff % tf == 0 and nf >= 2
    tok = lambda w: pl.BlockSpec((1, tm, w), lambda b, i, f: (b, i, 0))
    return pl.pallas_call(
        functools.partial(_ffn_body, nf),
        grid=(bsz, n_tiles, nf),
        in_specs=[tok(d)] + [tok(GROUP_WIDTH)] * 5 + [
            _const_spec(wo, layer),
            _const_spec(fnorm, layer),
            pl.BlockSpec((None, d, tf), lambda b, i, f: (layer, 0, f)),
            pl.BlockSpec((None, d, tf), lambda b, i, f: (layer, 0, f)),
            pl.BlockSpec((None, SUBLANES, tf), lambda b, i, f: (layer, 0, f)),
            pl.BlockSpec((None, 1, tf), lambda b, i, f: (layer, 0, f)),
            pl.BlockSpec((None, tf, d), lambda b, i, f: (layer, f, 0))],
        out_specs=tok(d),
        out_shape=jax.ShapeDtypeStruct((bsz, seq, d), jnp.float32),
        scratch_shapes=[pltpu.VMEM((tm, d), jnp.float32), pltpu.VMEM((TAIL_ROWS + tm, d), jnp.bfloat16),
                        pltpu.VMEM((tm, d), jnp.float32)],
        compiler_params=pltpu.CompilerParams(dimension_semantics=("arbitrary", "arbitrary", "arbitrary"),
                                             vmem_limit_bytes=VMEM_LIMIT_BYTES),
        name="ffn",
    )(x, *outs, wo, fnorm, wg, wu, cw, cb, wd)


def _pad_rows(v, width=MXU_WIDTH):
    return jnp.pad(v.astype(jnp.float32), ((0, 0), (0, width - v.shape[1])))


def _tile_rows(g, reps):
    return jnp.tile(g.astype(jnp.float32), (1, reps))


def _pack_in_projection(w):
    idx = _np_in_index()
    pieces, start = [], 0
    while start < PK_END:
        stop = start + 1
        if idx[start] == _SRC_END:
            while stop < PK_END and idx[stop] == _SRC_END:
                stop += 1
            pieces.append(jnp.zeros(w.shape[:-1] + (stop - start,), w.dtype))
        else:
            while stop < PK_END and idx[stop] == idx[stop - 1] + 1:
                stop += 1
            pieces.append(w[..., int(idx[start]):int(idx[stop - 1]) + 1])
        start = stop
    return jnp.concatenate(pieces, axis=-1)


def _zero_col(w):
    return jnp.concatenate([w, jnp.zeros(w.shape[:-1] + (1,), w.dtype)], axis=-1)


def _rope_table(positions):
    pos = positions.astype(jnp.float32)[:, :, None]
    inv = [ROPE_THETA ** (-jnp.arange(0, rot, 2, dtype=jnp.float32) / rot) for rot in (MLA_ROPE, ROT_MOBA, ROT_DIFF)]
    inv = jnp.concatenate(inv + [jnp.zeros((N_FREQ - TR_ONE,), jnp.float32)])
    ang = pos * inv
    c, s = jnp.cos(ang), jnp.sin(ang)
    c_hi = c.astype(jnp.bfloat16)
    c_lo = (c - c_hi.astype(jnp.float32)).astype(jnp.bfloat16)
    s_hi = s.astype(jnp.bfloat16)
    s_lo = (s - s_hi.astype(jnp.float32)).astype(jnp.bfloat16)
    return jnp.concatenate([c_hi, c_lo, s_hi, s_lo], axis=-1)


def _pick_tile(n, pref):
    t = pref
    while n % t:
        t //= 2
    return t


def kernel(x, mem, positions, attn_norm, ffn_norm, mem_norm, w_in, mla_cq_norm, mla_ckv_norm, mla_w_uq, mla_w_ukv, mla_q_norm, mla_k_norm, fox_b_f, fox_q_norm, fox_k_norm, moba_q_norm, moba_k_norm, diff_lambda, diff_q_norm, diff_k_norm, diff_sub_norm, mem_w_kv, mem_q_norm, mem_k_norm, w_o, ffn_w_gate, ffn_w_up, ffn_conv_w, ffn_conv_b, ffn_w_down):
    bsz, seq, d = x.shape
    depth = w_in.shape[0]
    dff = ffn_w_gate.shape[2]
    bf = jnp.bfloat16
    f32 = jnp.float32

    uq_idx = _np_uq_index()
    ukvk_idx, ukvv_idx = _np_ukv_index()
    gpair = jnp.asarray(_np_group_matrix(_PAIR_GROUPS), bf)
    g64 = jnp.asarray(_np_group_matrix(_G64_GROUPS), bf)
    g32 = jnp.asarray(_np_group_matrix(_G32_GROUPS), bf)
    expand = jnp.asarray(_np_rope_expand_all(), bf)
    t_prep = max(_pick_tile(seq, PREP_TILE), MOBA_BLOCK)
    tril = jnp.asarray(np.tril(np.ones((MOBA_BLOCK, MOBA_BLOCK), np.float32)), bf)
    vsel = jnp.asarray(_np_value_select(), bf)
    tr = _rope_table(positions)

    tile = lambda cfg: max(_pick_tile(seq, cfg.tile), MOBA_BLOCK) if cfg.select else _pick_tile(seq, cfg.tile)
    t_ffn = _pick_tile(seq, 512)
    tf = dff // 2 if (dff // 2) % LANES == 0 else dff

    win = _pack_in_projection(w_in.astype(bf))
    wuq = jnp.take(_zero_col(mla_w_uq), uq_idx, axis=2)
    wuq = jnp.pad(wuq, ((0, 0), (0, MXU_WIDTH - MLA_Q_RANK), (0, 0))).astype(bf)
    wukv = _zero_col(mla_w_ukv)
    wukvk = jnp.take(wukv, ukvk_idx, axis=2).astype(bf)
    wukvv = jnp.take(wukv, ukvv_idx, axis=2).astype(bf)
    pair = lambda g: _pad_rows(_tile_rows(g, 2))
    rows = [jnp.zeros((depth, MXU_WIDTH), f32)] * P_ROWS
    rows[P_CQ] = _pad_rows(mla_cq_norm)
    rows[P_CKV] = _pad_rows(mla_ckv_norm)
    rows[P_GQ] = pair(mla_q_norm) * (LOG2E * MLA_QK ** -0.5)
    rows[P_GK] = pair(mla_k_norm)
    rows[P_FQ] = _tile_rows(fox_q_norm, N_HEADS) * (LOG2E * HEAD_DIM ** -0.5)
    rows[P_FK] = _tile_rows(fox_k_norm, N_HEADS)
    rows[P_FB] = _pad_rows(fox_b_f)
    rows[P_MQ] = _tile_rows(moba_q_norm, N_HEADS)
    rows[P_MK] = _tile_rows(moba_k_norm, N_HEADS)
    rows[P_DQ] = _tile_rows(diff_q_norm, 2 * N_HEADS) * (LOG2E * DIFF_QK ** -0.5)
    rows[P_DK] = _tile_rows(diff_k_norm, 2 * N_HEADS)
    rows[P_EQ] = _tile_rows(mem_q_norm, N_HEADS) * (LOG2E * HEAD_DIM ** -0.5)
    par = jnp.stack(rows, axis=1)
    anorm = attn_norm.astype(f32)[:, None, :]
    mnorm = mem_norm.astype(f32)[:, None, :]
    fnorm = ffn_norm.astype(f32)[:, None, :]
    wmem = mem_w_kv.astype(bf)
    mem_gain = _tile_rows(mem_k_norm, N_HEADS)[:, None, :]

    lam_init = jnp.asarray([0.8 - 0.6 * math.exp(-0.3 * l) for l in range(depth)], f32)
    lam_vec = diff_lambda.astype(f32)
    lam = (jnp.exp(jnp.sum(lam_vec[:, 0] * lam_vec[:, 1], axis=-1))
           - jnp.exp(jnp.sum(lam_vec[:, 2] * lam_vec[:, 3], axis=-1)) + lam_init)
    lam_row = jnp.broadcast_to(lam[:, None, None], (depth, 1, GROUP_WIDTH))
    gsub = (_tile_rows(diff_sub_norm, N_HEADS) * (1.0 - lam_init)[:, None])[:, None, :]

    wo = w_o.astype(bf)
    wg, wu, wd = ffn_w_gate.astype(bf), ffn_w_up.astype(bf), ffn_w_down.astype(bf)
    cw = jnp.pad(ffn_conv_w.astype(f32), ((0, 0), (0, SUBLANES - CONV_WIDTH), (0, 0)))
    cb = ffn_conv_b.astype(f32)[:, None, :]

    for l in range(depth):
        (aq, ak, av, fq, fk, fv, fdcol, fdrow, mq, mk, mv, msel, dq, dk, dv, eq) = _prep_call(
            l, t_prep, x, tr, anorm, win, wuq, wukvk, wukvv, gpair, g64, g32, expand, tril, vsel, par)
        ek, ev = _memkv_call(l, mem, mnorm, wmem, g64, mem_gain, vsel)

        o_a = _attn_call(_CFG_MLA, aq, ak, av, (), tile(_CFG_MLA), tile(_CFG_MLA))
        o_b = _attn_call(_CFG_FOX, fq, fk, fv, (fdcol, fdrow), tile(_CFG_FOX), tile(_CFG_FOX))
        o_c = _attn_call(_CFG_MOBA, mq, mk, mv, (msel,), tile(_CFG_MOBA), tile(_CFG_MOBA))
        o_d = _attn_call(_CFG_DIFF, dq, dk, dv, (l, g64, gsub, lam_row), tile(_CFG_DIFF), tile(_CFG_DIFF))
        o_e = _attn_call(_CFG_MEM, eq, ek, ev, (), tile(_CFG_MEM), mem.shape[1])

        x = _ffn_call(l, x, (o_a, o_b, o_c, o_d, o_e), wo, fnorm, wg, wu, cw, cb, wd, t_ffn, tf)
    return x
```

```python
import functools
import math

import numpy as np
import jax
import jax.numpy as jnp
from jax import lax
from jax.experimental import pallas as pl
from jax.experimental.pallas import tpu as pltpu

N_HEADS = 4
HEAD_DIM = 64
GROUP_WIDTH = N_HEADS * HEAD_DIM
MLA_Q_RANK = 192
MLA_KV_RANK = 128
MLA_NOPE = 64
MLA_ROPE = 32
MLA_QK = MLA_NOPE + MLA_ROPE
DIFF_QK = HEAD_DIM // 2
ROPE_THETA = 500000.0
ROT_MOBA = HEAD_DIM // 4
ROT_DIFF = DIFF_QK // 4
MOBA_BLOCK = 256
MOBA_TOPK = 3
CONV_WIDTH = 3
EPS = 1e-6
NEG_INF = -1e30
LOG2E = math.log2(math.e)
REMOVED = -3e38

LANES = 128
SUBLANES = 8
MXU_WIDTH = 256
TAIL_ROWS = 16
PREP_TILE = 512
VMEM_LIMIT_BYTES = 56 * 1024 * 1024

_SRC_CQ = 0
_SRC_CKV = _SRC_CQ + MLA_Q_RANK
_SRC_KR = _SRC_CKV + MLA_KV_RANK
_SRC_FOX = _SRC_KR + MLA_ROPE
_SRC_FOXF = _SRC_FOX + 3 * GROUP_WIDTH
_SRC_MOBA = _SRC_FOXF + N_HEADS
_SRC_DIFF = _SRC_MOBA + 3 * GROUP_WIDTH
_SRC_MEMQ = _SRC_DIFF + 3 * GROUP_WIDTH
_SRC_END = _SRC_MEMQ + GROUP_WIDTH

PK_CQ = 0
PK_CKV = 256
PK_KR = 384
PK_FQ, PK_FK, PK_FV = 896, 1152, 1408
PK_FF = 1664
PK_MQ, PK_MK, PK_MV = 1792, 2048, 2304
PK_DQ, PK_DK, PK_DV = 2560, 2816, 3072
PK_EQ = 3328
PK_END = 3584

PAIR_STRIDE = MLA_QK


def _pair_lane(h, d):
    return (h // 2) * MXU_WIDTH + (h % 2) * PAIR_STRIDE + d


N_FREQ = 32
FREQ_BASE_MLA = 0
FREQ_BASE_MOBA = MLA_ROPE // 2
FREQ_BASE_DIFF = FREQ_BASE_MOBA + ROT_MOBA // 2
TR_ONE = FREQ_BASE_DIFF + ROT_DIFF // 2
TR_WIDTH = 4 * N_FREQ
assert TR_ONE < N_FREQ and TR_WIDTH == LANES

(P_CQ, P_CKV, P_GQ, P_GK, P_FQ, P_FK, P_FB, P_MQ, P_MK, P_DQ, P_DK, P_EQ) = range(12)
P_ROWS = 16


def _np_in_index():
    idx = np.full((PK_END,), _SRC_END, np.int32)
    idx[PK_CQ:PK_CQ + MLA_Q_RANK] = np.arange(_SRC_CQ, _SRC_CQ + MLA_Q_RANK)
    idx[PK_CKV:PK_CKV + MLA_KV_RANK] = np.arange(_SRC_CKV, _SRC_CKV + MLA_KV_RANK)
    for h in range(N_HEADS):
        for d in range(MLA_ROPE):
            idx[PK_KR + _pair_lane(h, d)] = _SRC_KR + d
    idx[PK_FQ:PK_FQ + 3 * GROUP_WIDTH] = np.arange(_SRC_FOX, _SRC_FOX + 3 * GROUP_WIDTH)
    idx[PK_FF:PK_FF + N_HEADS] = np.arange(_SRC_FOXF, _SRC_FOXF + N_HEADS)
    idx[PK_MQ:PK_MQ + 3 * GROUP_WIDTH] = np.arange(_SRC_MOBA, _SRC_MOBA + 3 * GROUP_WIDTH)
    idx[PK_DQ:PK_DQ + 3 * GROUP_WIDTH] = np.arange(_SRC_DIFF, _SRC_DIFF + 3 * GROUP_WIDTH)
    idx[PK_EQ:PK_EQ + GROUP_WIDTH] = np.arange(_SRC_MEMQ, _SRC_MEMQ + GROUP_WIDTH)
    return idx


def _np_uq_index():
    idx = np.full((2 * MXU_WIDTH,), N_HEADS * MLA_QK, np.int32)
    for h in range(N_HEADS):
        for d in range(MLA_QK):
            idx[_pair_lane(h, d)] = h * MLA_QK + d
    return idx


def _np_ukv_index():
    zero = N_HEADS * (MLA_NOPE + HEAD_DIM)
    idx_k = np.full((2 * MXU_WIDTH,), zero, np.int32)
    idx_v = np.zeros((GROUP_WIDTH,), np.int32)
    for h in range(N_HEADS):
        for d in range(MLA_NOPE):
            idx_k[_pair_lane(h, MLA_ROPE + d)] = h * (MLA_NOPE + HEAD_DIM) + d
        for d in range(HEAD_DIM):
            idx_v[h * HEAD_DIM + d] = h * (MLA_NOPE + HEAD_DIM) + MLA_NOPE + d
    return idx_k, idx_v


def _np_group_matrix(groups):
    g = np.zeros((MXU_WIDTH, MXU_WIDTH), np.float32)
    for lo, size in groups:
        g[lo:lo + size, lo:lo + size] = 1.0 / size
    return g


_PAIR_GROUPS = [(0, MLA_ROPE), (MLA_ROPE, MLA_NOPE), (PAIR_STRIDE, MLA_ROPE), (PAIR_STRIDE + MLA_ROPE, MLA_NOPE)]
_G64_GROUPS = [(h * HEAD_DIM, HEAD_DIM) for h in range(N_HEADS)]
_G32_GROUPS = [(g * DIFF_QK, DIFF_QK) for g in range(2 * N_HEADS)]


def _np_rope_expand(regions, rot, base):
    half = rot // 2
    e = np.zeros((TR_WIDTH, 2 * MXU_WIDTH), np.float32)
    e[TR_ONE, 0:MXU_WIDTH] = 1.0
    for lo in regions:
        assert lo % rot == 0
        for r in range(half):
            f = base + r
            for lane, sign in ((lo + r, -1.0), (lo + half + r, 1.0)):
                e[TR_ONE, lane] = 0.0
                e[f, lane] = 1.0
                e[N_FREQ + f, lane] = 1.0
                e[2 * N_FREQ + f, MXU_WIDTH + lane] = sign
                e[3 * N_FREQ + f, MXU_WIDTH + lane] = sign
    return e


def _np_rope_expand_all():
    return np.concatenate([
        _np_rope_expand([0, PAIR_STRIDE], MLA_ROPE, FREQ_BASE_MLA),
        _np_rope_expand([h * HEAD_DIM for h in range(N_HEADS)], ROT_MOBA, FREQ_BASE_MOBA),
        _np_rope_expand([g * DIFF_QK for g in range(2 * N_HEADS)], ROT_DIFF, FREQ_BASE_DIFF),
    ], axis=1)


def _dot(a, b):
    return jnp.dot(a, b, preferred_element_type=jnp.float32)


def _dot_nt(a, b):
    return lax.dot_general(a, b, (((1,), (1,)), ((), ())), preferred_element_type=jnp.float32)


def _split2(a):
    hi = a.astype(jnp.bfloat16)
    lo = (a - hi.astype(jnp.float32)).astype(jnp.bfloat16)
    return hi, lo


def _split3(a):
    hi = a.astype(jnp.bfloat16)
    r = a - hi.astype(jnp.float32)
    mid = r.astype(jnp.bfloat16)
    lo = (r - mid.astype(jnp.float32)).astype(jnp.bfloat16)
    return hi, mid, lo


def _group_mean_sq(a, g_bf16):
    return _dot((a * a).astype(jnp.bfloat16), g_bf16)


def _rope(x, tabs, half):
    w = x.shape[-1]
    lane = lax.broadcasted_iota(jnp.int32, x.shape, 1)
    partner = jnp.where((lane & (2 * half - 1)) >= half, pltpu.roll(x, half, 1), pltpu.roll(x, w - half, 1))
    return x * tabs[:, 0:w] + partner * tabs[:, w:2 * w]


def _lane_mask(shape, lo, hi):
    lane = lax.broadcasted_iota(jnp.int32, shape, len(shape) - 1)
    return (lane >= lo) & (lane < hi)


def _prep_body(x_ref, tr_ref, anorm_ref, win_ref, wuq_ref, wukvk_ref, wukvv_ref, gpair_ref, g64_ref, g32_ref,
               exp_ref, tril_ref, vsel_ref, par_ref,
               aq_ref, ak_ref, av_ref, fq_ref, fk_ref, fv_ref, fdcol_ref, fdrow_ref, mq_ref, mk_ref, mv_ref, msel_ref,
               dq_ref, dk_ref, dv_ref, eq_ref,
               kmean_s, carry_s):
    j = pl.program_id(1)
    tm = x_ref.shape[1]
    pr = MOBA_BLOCK
    n_parts = tm // pr
    nbp = kmean_s.shape[0]
    bf = jnp.bfloat16
    gpair, g64, g32 = gpair_ref[...], g64_ref[...], g32_ref[...]
    vsel = vsel_ref[...]
    tril = tril_ref[...]

    @pl.when(j == 0)
    def _():
        kmean_s[...] = jnp.zeros_like(kmean_s)
        carry_s[...] = jnp.zeros_like(carry_s)

    def prow(r, width=MXU_WIDTH):
        return par_ref[r:r + 1, 0:width]

    def project(st):
        rows = st["rows"]
        x = x_ref[0, rows, :]
        xb = (x * lax.rsqrt(jnp.mean(x * x, axis=-1, keepdims=True) + EPS) * anorm_ref[...]).astype(bf)
        tabs = _dot(tr_ref[0, rows, :], exp_ref[...])
        st["tab_mla"] = tabs[:, 0:2 * MXU_WIDTH]
        st["tab_moba"] = tabs[:, 2 * MXU_WIDTH:4 * MXU_WIDTH]
        st["tab_diff"] = tabs[:, 4 * MXU_WIDTH:6 * MXU_WIDTH]

        def run(lo, hi):
            wide = _dot(xb, win_ref[:, lo:hi])
            return lambda off, width: wide[:, off - lo:off - lo + width]

        run_m = run(PK_MQ, PK_DQ)
        run_a = run(PK_CQ, PK_FQ)
        run_f = run(PK_FQ, PK_MQ)
        run_d = run(PK_DQ, PK_END)
        st["direct"] = {"mq": run_m(PK_MQ, MXU_WIDTH), "mk": run_m(PK_MK, MXU_WIDTH),
                        "fq": run_f(PK_FQ, MXU_WIDTH), "fk": run_f(PK_FK, MXU_WIDTH),
                        "dq": run_d(PK_DQ, MXU_WIDTH), "dk": run_d(PK_DK, MXU_WIDTH),
                        "eq": run_d(PK_EQ, MXU_WIDTH)}
        st["values"] = {"fv": run_f(PK_FV, MXU_WIDTH).astype(bf), "mv": run_m(PK_MV, MXU_WIDTH).astype(bf),
                        "dv": run_d(PK_DV, MXU_WIDTH).astype(bf)}
        p_cq, p_ckv, st["p_kr"] = run_a(PK_CQ, MXU_WIDTH), run_a(PK_CKV, MLA_KV_RANK), run_a(PK_KR, 2 * MXU_WIDTH)
        cqn = p_cq * lax.rsqrt(jnp.sum(p_cq * p_cq, axis=-1, keepdims=True) * (1.0 / MLA_Q_RANK) + EPS) * prow(P_CQ)
        ckvn = p_ckv * lax.rsqrt(jnp.mean(p_ckv * p_ckv, axis=-1, keepdims=True) + EPS) * prow(P_CKV, MLA_KV_RANK)
        st["cqb"], st["ckvb"] = cqn.astype(bf), ckvn.astype(bf)
        z = run_f(PK_FF, LANES) + prow(P_FB, LANES)
        log_f = jnp.minimum(z, 0.0) - jnp.log1p(jnp.exp(-jnp.abs(z)))
        st["log_f"] = _split3(jnp.where(_lane_mask(log_f.shape, 0, N_HEADS), log_f, 0.0))

    def second(st):
        gmat = {"fq": g64, "fk": g64, "mq": g64, "mk": g64, "dq": g32, "dk": g32, "eq": g64}
        gains = {"fq": P_FQ, "fk": P_FK, "mq": P_MQ, "mk": P_MK, "dq": P_DQ, "dk": P_DK, "eq": P_EQ}
        ms = {name: _group_mean_sq(a, gmat[name]) for name, a in st["direct"].items()}
        st["qa"] = _dot(st["cqb"], wuq_ref[...])
        st["ka"] = st["p_kr"] + _dot(st["ckvb"], wukvk_ref[...])
        st["values"]["av"] = _dot(st["ckvb"], wukvv_ref[...]).astype(bf)
        l1, l2, l3 = st["log_f"]
        st["cumsum"] = (_dot(tril, l1) + _dot(tril, l2)) + _dot(tril, l3)
        st["normed"] = {name: a * lax.rsqrt(ms[name] + EPS) * prow(gains[name]) for name, a in st["direct"].items()}

    def third(st):
        rows = st["rows"]
        mq = _rope(st["normed"]["mq"], st["tab_moba"], ROT_MOBA // 2)
        mk = _rope(st["normed"]["mk"], st["tab_moba"], ROT_MOBA // 2)
        st["mq"], st["mk"] = mq, mk
        kmean_s[pl.ds(st["blk"], 1), :] = jnp.mean(mk, axis=0, keepdims=True)
        km_hi, km_lo = _split2(kmean_s[...])
        st["gates"] = []
        for h in range(N_HEADS):
            q_hi, q_lo = _split2(jnp.where(_lane_mask(mq.shape, h * HEAD_DIM, (h + 1) * HEAD_DIM), mq, 0.0))
            st["gates"].append((_dot_nt(km_hi, q_hi) + _dot_nt(km_lo, q_hi)) + _dot_nt(km_hi, q_lo))
        st["ms_qa"] = [_group_mean_sq(st["qa"][:, p * MXU_WIDTH:(p + 1) * MXU_WIDTH], gpair) for p in range(2)]
        st["ms_ka"] = [_group_mean_sq(st["ka"][:, p * MXU_WIDTH:(p + 1) * MXU_WIDTH], gpair) for p in range(2)]
        dec = carry_s[...] + st["cumsum"]
        carry_s[...] = dec[pr - 1:pr, :]
        dec2 = dec * LOG2E
        d1, d2, d3 = _split3(dec2)
        for h in range(N_HEADS):
            fdcol_ref[0, rows, h * LANES:(h + 1) * LANES] = jnp.broadcast_to(dec2[:, h:h + 1], (pr, LANES))
        row_sel = jnp.where(lax.broadcasted_iota(jnp.int32, (SUBLANES, LANES), 0)
                            == lax.broadcasted_iota(jnp.int32, (SUBLANES, LANES), 1), 1.0, 0.0).astype(bf)
        fdrow_ref[0, :, rows] = (_dot_nt(row_sel, d1) + _dot_nt(row_sel, d2)) + _dot_nt(row_sel, d3)
        for name, ref in (("fv", fv_ref), ("mv", mv_ref), ("dv", dv_ref), ("av", av_ref)):
            ref[0, :, rows] = _values_t(st["values"][name], vsel)

    def finish(st):
        rows = st["rows"]
        normed = st["normed"]
        fq_ref[0, rows, :] = normed["fq"].astype(bf)
        fk_ref[0, rows, :] = normed["fk"].astype(bf)
        eq_ref[0, rows, :] = normed["eq"].astype(bf)
        dq_ref[0, rows, :] = _rope(normed["dq"], st["tab_diff"], ROT_DIFF // 2).astype(bf)
        dk_ref[0, rows, :] = _rope(normed["dk"], st["tab_diff"], ROT_DIFF // 2).astype(bf)
        mq_ref[0, rows, :] = (st["mq"] * (LOG2E * HEAD_DIM ** -0.5)).astype(bf)
        mk_ref[0, rows, :] = st["mk"].astype(bf)
        for p in range(2):
            sl = slice(p * MXU_WIDTH, (p + 1) * MXU_WIDTH)
            aq_ref[0, rows, sl] = _rope(st["qa"][:, sl] * lax.rsqrt(st["ms_qa"][p] + EPS) * prow(P_GQ),
                                        st["tab_mla"], MLA_ROPE // 2).astype(bf)
            ak_ref[0, rows, sl] = _rope(st["ka"][:, sl] * lax.rsqrt(st["ms_ka"][p] + EPS) * prow(P_GK),
                                        st["tab_mla"], MLA_ROPE // 2).astype(bf)
        blk = lax.broadcasted_iota(jnp.int32, (nbp, pr), 0)
        past = blk < st["blk"]
        for h in range(N_HEADS):
            work = jnp.where(past, st["gates"][h], NEG_INF)
            sel = jnp.zeros((nbp, pr), jnp.bool_)
            for _ in range(MOBA_TOPK):
                mx = jnp.max(work, axis=0, keepdims=True)
                first = jnp.min(jnp.where(work == mx, blk, nbp), axis=0, keepdims=True)
                pick = blk == first
                sel = sel | pick
                work = jnp.where(pick, REMOVED, work)
            msel_ref[0, h * nbp:(h + 1) * nbp, rows] = jnp.where(sel & past, 0.0, NEG_INF)

    parts = [{"rows": slice(p * pr, (p + 1) * pr), "blk": j * n_parts + p} for p in range(n_parts)]
    for phase in (project, second, third, finish):
        for st in parts:
            phase(st)


def _const_spec(a, layer=None):
    if layer is None:
        n = a.ndim
        return pl.BlockSpec(a.shape, lambda *_: (0,) * n)
    n = a.ndim - 1
    return pl.BlockSpec((None,) + a.shape[1:], lambda *_: (layer,) + (0,) * n)


def _moba_blocks_padded(seq):
    return -(-(seq // MOBA_BLOCK) // SUBLANES) * SUBLANES


def _prep_call(layer, tm, x, tr, anorm, win, wuq, wukvk, wukvv, gpair, g64, g32, expand, tril, vsel, par):
    bsz, seq, d = x.shape
    assert seq % tm == 0 and tm % MOBA_BLOCK == 0 and tril.shape == (MOBA_BLOCK, MOBA_BLOCK)
    nbp = _moba_blocks_padded(seq)
    bf = jnp.bfloat16
    f32 = jnp.float32

    vt = -N_HEADS * V_ROWS
    widths = [(2 * MXU_WIDTH, bf), (2 * MXU_WIDTH, bf), (vt, bf),
              (GROUP_WIDTH, bf), (GROUP_WIDTH, bf), (vt, bf),
              (N_HEADS * LANES, f32), (-SUBLANES, f32),
              (GROUP_WIDTH, bf), (GROUP_WIDTH, bf), (vt, bf), (-N_HEADS * nbp, f32),
              (GROUP_WIDTH, bf), (GROUP_WIDTH, bf), (vt, bf),
              (GROUP_WIDTH, bf)]

    def tok(width):
        if width > 0:
            return pl.BlockSpec((1, tm, width), lambda b, j: (b, j, 0))
        return pl.BlockSpec((1, -width, tm), lambda b, j: (b, 0, j))

    def shape(width):
        return (bsz, seq, width) if width > 0 else (bsz, -width, seq)

    consts = [anorm, win, wuq, wukvk, wukvv, gpair, g64, g32, expand, tril, vsel, par]
    layered = [True, True, True, True, True, False, False, False, False, False, False, True]
    return pl.pallas_call(
        _prep_body,
        grid=(bsz, seq // tm),
        in_specs=[tok(d), tok(TR_WIDTH)] + [_const_spec(c, layer if ly else None) for c, ly in zip(consts, layered)],
        out_specs=[tok(w) for w, _ in widths],
        out_shape=[jax.ShapeDtypeStruct(shape(w), dt) for w, dt in widths],
        scratch_shapes=[pltpu.VMEM((nbp, GROUP_WIDTH), jnp.float32), pltpu.VMEM((1, LANES), jnp.float32)],
        compiler_params=pltpu.CompilerParams(dimension_semantics=("arbitrary", "arbitrary"),
                                             vmem_limit_bytes=VMEM_LIMIT_BYTES),
        name="prep",
    )(x, tr, *consts)


def _memkv_body(mem_ref, mnorm_ref, w_ref, g64_ref, gain_ref, vsel_ref, k_ref, v_ref):
    m = mem_ref[0]
    mn = m * lax.rsqrt(jnp.mean(m * m, axis=-1, keepdims=True) + EPS) * mnorm_ref[...]
    kv = _dot(mn.astype(jnp.bfloat16), w_ref[...])
    k = kv[:, 0:GROUP_WIDTH]
    k = k * lax.rsqrt(_group_mean_sq(k, g64_ref[...]) + EPS) * gain_ref[...]
    k_ref[0] = k.astype(jnp.bfloat16)
    v_ref[0] = _values_t(kv[:, GROUP_WIDTH:2 * GROUP_WIDTH].astype(jnp.bfloat16), vsel_ref[...])


def _memkv_call(layer, mem, mnorm, w, g64, gain, vsel):
    bsz, mlen, d = mem.shape
    k_shape, vt_shape = (bsz, mlen, GROUP_WIDTH), (bsz, N_HEADS * V_ROWS, mlen)
    return pl.pallas_call(
        _memkv_body,
        grid=(bsz,),
        in_specs=[pl.BlockSpec((1, mlen, d), lambda b: (b, 0, 0)), _const_spec(mnorm, layer), _const_spec(w, layer),
                  _const_spec(g64), _const_spec(gain, layer), _const_spec(vsel)],
        out_specs=[pl.BlockSpec((1,) + s[1:], lambda b: (b, 0, 0)) for s in (k_shape, vt_shape)],
        out_shape=[jax.ShapeDtypeStruct(s, jnp.bfloat16) for s in (k_shape, vt_shape)],
        compiler_params=pltpu.CompilerParams(dimension_semantics=("arbitrary",), vmem_limit_bytes=VMEM_LIMIT_BYTES),
        name="mem_kv",
    )(mem, mnorm, w, g64, gain, vsel)


Q_SUB = 2 * MXU_WIDTH


class _AttnCfg:
    def __init__(self, name, vheads, n_maps, causal, decay=False, select=False, diff=False, tile=1024, lookahead=8,
                 q_sub=Q_SUB):
        self.name = name
        self.vheads = vheads
        self.n_maps = n_maps
        self.causal = causal
        self.decay = decay
        self.select = select
        self.diff = diff
        self.tile = tile
        self.lookahead = lookahead
        self.q_sub = q_sub


_PLAIN_VHEADS = [(0, h * HEAD_DIM, (h + 1) * HEAD_DIM, 0, h) for h in range(N_HEADS)]
_CFG_MLA = _AttnCfg("attn_mla", [((h // 2) * MXU_WIDTH, (h % 2) * PAIR_STRIDE, (h % 2) * PAIR_STRIDE + MLA_QK, 0, h)
                                 for h in range(N_HEADS)], 1, True, q_sub=MXU_WIDTH, lookahead=16)
_CFG_FOX = _AttnCfg("attn_fox", _PLAIN_VHEADS, 1, True, decay=True, tile=2048, lookahead=4)
_CFG_MOBA = _AttnCfg("attn_moba", _PLAIN_VHEADS, 1, True, select=True, tile=2048, lookahead=6)
_CFG_DIFF = _AttnCfg("attn_diff", [(0, h * HEAD_DIM + c * DIFF_QK, h * HEAD_DIM + (c + 1) * DIFF_QK, c, h)
                                   for c in range(2) for h in range(N_HEADS)], 2, True, diff=True, q_sub=MXU_WIDTH,
                     lookahead=32)
_CFG_MEM = _AttnCfg("attn_mem", _PLAIN_VHEADS, 1, False, tile=2048)


ONES_ROW = HEAD_DIM
V_ROWS = HEAD_DIM + 16


def _np_value_select():
    sel = np.zeros((N_HEADS * V_ROWS, GROUP_WIDTH), np.float32)
    for h in range(N_HEADS):
        for d in range(HEAD_DIM):
            sel[h * V_ROWS + d, h * HEAD_DIM + d] = 1.0
    return sel


def _values_t(v, vsel):
    vt = _dot_nt(vsel, v)
    row = lax.broadcasted_iota(jnp.int32, vt.shape, 0)
    ones = row == ONES_ROW
    for h in range(1, N_HEADS):
        ones = ones | (row == h * V_ROWS + ONES_ROW)
    return jnp.where(ones, 1.0, vt).astype(jnp.bfloat16)


def _tile_lanes(x, width):
    return jnp.tile(x, (1, width // LANES)) if width != LANES else x


def _attn_body(cfg, qi_ref, kj_ref, *refs):
    refs = list(refs)
    q_ref, k_ref, vt_ref = refs[:3]
    pos = 3
    if cfg.decay:
        dq_ref, dk_ref = refs[pos:pos + 2]
        pos += 2
    if cfg.select:
        sel_ref = refs[pos]
        pos += 1
    if cfg.diff:
        g64_ref, gsub_ref, lam_ref = refs[pos:pos + 3]
        pos += 3
    o_ref, qm_s, m_s, acc_s = refs[pos:pos + 4]

    t = pl.program_id(1)
    i = qi_ref[t]
    j = kj_ref[t]
    tq = q_ref.shape[1]
    tk = k_ref.shape[1]

    @pl.when(j == 0)
    def _():
        for n, (off, lo, hi, _, _) in enumerate(cfg.vheads):
            qb = q_ref[0, :, off:off + MXU_WIDTH]
            qm_s[n] = jnp.where(_lane_mask(qb.shape, lo, hi), qb, jnp.zeros_like(qb))
        m_s[...] = jnp.full(m_s.shape, NEG_INF, jnp.float32)
        acc_s[...] = jnp.zeros_like(acc_s)

    def step(diag):
        qs = min(tq, cfg.q_sub)
        items = [(n, u) for n in range(len(cfg.vheads)) for u in range(tq // qs)]

        def n_keys(u):
            return (u + 1) * qs if diag else tk

        def scores(item):
            n, u = item
            off, _, _, _, h = cfg.vheads[n]
            nk = n_keys(u)
            cols = slice(u * qs, (u + 1) * qs)
            s = _dot_nt(k_ref[0, 0:nk, off:off + MXU_WIDTH], qm_s[n, cols, :])
            if cfg.decay:
                s = (dq_ref[0, h:h + 1, cols] - _tile_lanes(dk_ref[0, 0:nk, h * LANES:(h + 1) * LANES], qs)) + s
            if cfg.select:
                nbp = sel_ref.shape[1] // N_HEADS
                qpos = u * qs + lax.broadcasted_iota(jnp.int32, (1, qs), 1)
                parts = []
                for kb in range(nk // MOBA_BLOCK):
                    rows = s[kb * MOBA_BLOCK:(kb + 1) * MOBA_BLOCK, :]
                    if not (diag and kb == nk // MOBA_BLOCK - 1):
                        bias = sel_ref[0, pl.ds(h * nbp + j * (tk // MOBA_BLOCK) + kb, 1), cols]
                        if diag:
                            bias = jnp.where(qpos < (kb + 1) * MOBA_BLOCK, 0.0, bias)
                        rows = rows + bias
                    parts.append(rows)
                s = parts[0] if len(parts) == 1 else jnp.concatenate(parts, axis=0)
            if diag:
                key = lax.broadcasted_iota(jnp.int32, (nk, qs), 0)
                qry = u * qs + lax.broadcasted_iota(jnp.int32, (nk, qs), 1)
                s = jnp.where(key <= qry, s, NEG_INF)
            return s, jnp.max(s, axis=0, keepdims=True)

        raw = {it: scores(items[it]) for it in range(min(cfg.lookahead, len(items)))}
        for it, (n, u) in enumerate(items):
            h = cfg.vheads[n][4]
            nk = n_keys(u)
            cols = slice(u * qs, (u + 1) * qs)
            s, s_max = raw.pop(it)
            m_prev = m_s[n, :, cols]
            m_new = jnp.maximum(m_prev, s_max)
            alpha = jnp.exp2(m_prev - m_new)
            p = jnp.exp2(s - m_new)
            m_s[n, :, cols] = m_new
            acc_s[n, :, cols] = acc_s[n, :, cols] * alpha + _dot(vt_ref[0, h * V_ROWS:(h + 1) * V_ROWS, 0:nk],
                                                                 p.astype(jnp.bfloat16))
            if it + cfg.lookahead < len(items):
                raw[it + cfg.lookahead] = scores(items[it + cfg.lookahead])

    if cfg.causal:
        pl.when(j < i)(functools.partial(step, False))
        pl.when(j == i)(functools.partial(step, True))
        last = j == i
    else:
        step(False)
        last = j == 0

    @pl.when(last)
    def _():
        outs = []
        for c in range(cfg.n_maps):
            heads = []
            for h in range(N_HEADS):
                acc = acc_s[c * N_HEADS + h]
                heads.append(acc[0:HEAD_DIM, :] / acc[ONES_ROW:ONES_ROW + 1, :])
            outs.append(jnp.concatenate(heads, axis=0).T)
        if cfg.diff:
            o = outs[0] - lam_ref[0:1, :] * outs[1]
            o = o * lax.rsqrt(_group_mean_sq(o, g64_ref[...]) + EPS) * gsub_ref[...]
        else:
            o = outs[0]
        o_ref[0] = o.astype(o_ref.dtype)


def _attn_call(cfg, q, k, v, extras, tq, tk):
    bsz, seq, wq = q.shape
    sk = k.shape[1]
    nq = seq // tq
    assert seq % tq == 0 and sk % tk == 0
    if cfg.causal:
        assert tq == tk and sk == seq
        pairs = [(i, j) for i in range(nq) for j in range(i + 1)]
    else:
        assert sk == tk
        pairs = [(i, 0) for i in range(nq)]
    qi = jnp.asarray(np.array([p[0] for p in pairs], np.int32))
    kj = jnp.asarray(np.array([p[1] for p in pairs], np.int32))
    n_vh = len(cfg.vheads)

    in_specs = [pl.BlockSpec((1, tq, wq), lambda b, t, qi, kj: (b, qi[t], 0)),
                pl.BlockSpec((1, tk, wq), lambda b, t, qi, kj: (b, kj[t], 0)),
                pl.BlockSpec((1, N_HEADS * V_ROWS, tk), lambda b, t, qi, kj: (b, 0, kj[t]))]
    args = [q, k, v]
    if cfg.decay:
        dcol, drow = extras
        in_specs += [pl.BlockSpec((1, SUBLANES, tq), lambda b, t, qi, kj: (b, 0, qi[t])),
                     pl.BlockSpec((1, tk, N_HEADS * LANES), lambda b, t, qi, kj: (b, kj[t], 0))]
        args += [drow, dcol]
    if cfg.select:
        (sel,) = extras
        in_specs += [pl.BlockSpec((1, sel.shape[1], tq), lambda b, t, qi, kj: (b, 0, qi[t]))]
        args += [sel]
    if cfg.diff:
        layer, g64, gsub, lam_row = extras
        in_specs += [_const_spec(g64), _const_spec(gsub, layer), _const_spec(lam_row, layer)]
        args += [g64, gsub, lam_row]

    grid_spec = pltpu.PrefetchScalarGridSpec(
        num_scalar_prefetch=2,
        grid=(bsz, len(pairs)),
        in_specs=in_specs,
        out_specs=pl.BlockSpec((1, tq, GROUP_WIDTH), lambda b, t, qi, kj: (b, qi[t], 0)),
        scratch_shapes=[pltpu.VMEM((n_vh, tq, MXU_WIDTH), jnp.bfloat16),
                        pltpu.VMEM((n_vh, 1, tq), jnp.float32),
                        pltpu.VMEM((n_vh, V_ROWS, tq), jnp.float32)])
    return pl.pallas_call(
        functools.partial(_attn_body, cfg),
        grid_spec=grid_spec,
        out_shape=jax.ShapeDtypeStruct((bsz, seq, GROUP_WIDTH), jnp.bfloat16),
        compiler_params=pltpu.CompilerParams(dimension_semantics=("arbitrary", "arbitrary"),
                                             vmem_limit_bytes=VMEM_LIMIT_BYTES),
        name=cfg.name,
    )(qi, kj, *args)


def _ffn_body(nf, x_ref, oa_ref, ob_ref, oc_ref, od_ref, oe_ref, wo_ref, fnorm_ref, wg_ref, wu_ref, cw_ref, cb_ref,
              wd_ref, out_ref, xnew_s, xn_s, acc_s):
    i = pl.program_id(1)
    f = pl.program_id(2)
    tm = x_ref.shape[1]

    @pl.when(f == 0)
    def _():
        @pl.when(i == 0)
        def _():
            xn_s[0:TAIL_ROWS, :] = jnp.zeros((TAIL_ROWS, xn_s.shape[1]), xn_s.dtype)

        @pl.when(i > 0)
        def _():
            xn_s[0:TAIL_ROWS, :] = xn_s[tm:tm + TAIL_ROWS, :]

        mixed = jnp.concatenate([o_ref[0] for o_ref in (oa_ref, ob_ref, oc_ref, od_ref, oe_ref)], axis=1)
        xnew = x_ref[0] + _dot(mixed, wo_ref[...])
        xnew_s[...] = xnew
        xn = xnew * lax.rsqrt(jnp.mean(xnew * xnew, axis=-1, keepdims=True) + EPS) * fnorm_ref[...]
        xn_s[TAIL_ROWS:TAIL_ROWS + tm, :] = xn.astype(xn_s.dtype)

    def mlp_chunk():
        ge = _dot(xn_s[...], wg_ref[...])
        u = _dot(xn_s[TAIL_ROWS:TAIL_ROWS + tm, :], wu_ref[...])
        g0 = ge[TAIL_ROWS:TAIL_ROWS + tm, :]
        t1 = ge[TAIL_ROWS - 1:TAIL_ROWS, :]
        t2 = ge[TAIL_ROWS - 2:TAIL_ROWS - 1, :]
        row = lax.broadcasted_iota(jnp.int32, g0.shape, 0)
        g1 = jnp.where(row == 0, t1, pltpu.roll(g0, 1, 0))
        g2 = jnp.where(row == 0, t2, jnp.where(row == 1, t1, pltpu.roll(g0, 2, 0)))
        y = cb_ref[...] + cw_ref[0:1, :] * g2
        y = y + cw_ref[1:2, :] * g1
        y = y + cw_ref[2:3, :] * g0
        hmid = (y * (1.0 / (1.0 + jnp.exp(-y)))) * u
        return _dot(hmid.astype(jnp.bfloat16), wd_ref[...])

    @pl.when(f == 0)
    def _():
        acc_s[...] = mlp_chunk()

    if nf > 2:
        @pl.when((f > 0) & (f < nf - 1))
        def _():
            acc_s[...] += mlp_chunk()

    @pl.when(f == nf - 1)
    def _():
        out_ref[0] = xnew_s[...] + (acc_s[...] + mlp_chunk())


def _ffn_call(layer, x, outs, wo, fnorm, wg, wu, cw, cb, wd, tm, tf):
    bsz, seq, d = x.shape
    dff = wg.shape[2]
    n_tiles, nf = seq // tm, dff // tf
    assert seq % tm == 0 and dff % tf == 0 and nf >= 2
    tok = lambda w: pl.BlockSpec((1, tm, w), lambda b, i, f: (b, i, 0))
    return pl.pallas_call(
        functools.partial(_ffn_body, nf),
        grid=(bsz, n_tiles, nf),
        in_specs=[tok(d)] + [tok(GROUP_WIDTH)] * 5 + [
            _const_spec(wo, layer),
            _const_spec(fnorm, layer),
            pl.BlockSpec((None, d, tf), lambda b, i, f: (layer, 0, f)),
            pl.BlockSpec((None, d, tf), lambda b, i, f: (layer, 0, f)),
            pl.BlockSpec((None, SUBLANES, tf), lambda b, i, f: (layer, 0, f)),
            pl.BlockSpec((None, 1, tf), lambda b, i, f: (layer, 0, f)),
            pl.BlockSpec((None, tf, d), lambda b, i, f: (layer, f, 0))],
        out_specs=tok(d),
        out_shape=jax.ShapeDtypeStruct((bsz, seq, d), jnp.float32),
        scratch_shapes=[pltpu.VMEM((tm, d), jnp.float32), pltpu.VMEM((TAIL_ROWS + tm, d), jnp.bfloat16),
                        pltpu.VMEM((tm, d), jnp.float32)],
        compiler_params=pltpu.CompilerParams(dimension_semantics=("arbitrary", "arbitrary", "arbitrary"),
                                             vmem_limit_bytes=VMEM_LIMIT_BYTES),
        name="ffn",
    )(x, *outs, wo, fnorm, wg, wu, cw, cb, wd)


def _pad_rows(v, width=MXU_WIDTH):
    return jnp.pad(v.astype(jnp.float32), ((0, 0), (0, width - v.shape[1])))


def _tile_rows(g, reps):
    return jnp.tile(g.astype(jnp.float32), (1, reps))


def _pack_in_projection(w):
    idx = _np_in_index()
    pieces, start = [], 0
    while start < PK_END:
        stop = start + 1
        if idx[start] == _SRC_END:
            while stop < PK_END and idx[stop] == _SRC_END:
                stop += 1
            pieces.append(jnp.zeros(w.shape[:-1] + (stop - start,), w.dtype))
        else:
            while stop < PK_END and idx[stop] == idx[stop - 1] + 1:
                stop += 1
            pieces.append(w[..., int(idx[start]):int(idx[stop - 1]) + 1])
        start = stop
    return jnp.concatenate(pieces, axis=-1)


def _zero_col(w):
    return jnp.concatenate([w, jnp.zeros(w.shape[:-1] + (1,), w.dtype)], axis=-1)


def _rope_table(positions):
    pos = positions.astype(jnp.float32)[:, :, None]
    inv = [ROPE_THETA ** (-jnp.arange(0, rot, 2, dtype=jnp.float32) / rot) for rot in (MLA_ROPE, ROT_MOBA, ROT_DIFF)]
    inv = jnp.concatenate(inv + [jnp.zeros((N_FREQ - TR_ONE,), jnp.float32)])
    ang = pos * inv
    c, s = jnp.cos(ang), jnp.sin(ang)
    c_hi = c.astype(jnp.bfloat16)
    c_lo = (c - c_hi.astype(jnp.float32)).astype(jnp.bfloat16)
    s_hi = s.astype(jnp.bfloat16)
    s_lo = (s - s_hi.astype(jnp.float32)).astype(jnp.bfloat16)
    return jnp.concatenate([c_hi, c_lo, s_hi, s_lo], axis=-1)


def _pick_tile(n, pref):
    t = pref
    while n % t:
        t //= 2
    return t


def kernel(x, mem, positions, attn_norm, ffn_norm, mem_norm, w_in, mla_cq_norm, mla_ckv_norm, mla_w_uq, mla_w_ukv, mla_q_norm, mla_k_norm, fox_b_f, fox_q_norm, fox_k_norm, moba_q_norm, moba_k_norm, diff_lambda, diff_q_norm, diff_k_norm, diff_sub_norm, mem_w_kv, mem_q_norm, mem_k_norm, w_o, ffn_w_gate, ffn_w_up, ffn_conv_w, ffn_conv_b, ffn_w_down):
    bsz, seq, d = x.shape
    depth = w_in.shape[0]
    dff = ffn_w_gate.shape[2]
    bf = jnp.bfloat16
    f32 = jnp.float32

    uq_idx = _np_uq_index()
    ukvk_idx, ukvv_idx = _np_ukv_index()
    gpair = jnp.asarray(_np_group_matrix(_PAIR_GROUPS), bf)
    g64 = jnp.asarray(_np_group_matrix(_G64_GROUPS), bf)
    g32 = jnp.asarray(_np_group_matrix(_G32_GROUPS), bf)
    expand = jnp.asarray(_np_rope_expand_all(), bf)
    t_prep = max(_pick_tile(seq, PREP_TILE), MOBA_BLOCK)
    tril = jnp.asarray(np.tril(np.ones((MOBA_BLOCK, MOBA_BLOCK), np.float32)), bf)
    vsel = jnp.asarray(_np_value_select(), bf)
    tr = _rope_table(positions)

    tile = lambda cfg: max(_pick_tile(seq, cfg.tile), MOBA_BLOCK) if cfg.select else _pick_tile(seq, cfg.tile)
    t_ffn = _pick_tile(seq, 512)
    tf = dff // 2 if (dff // 2) % LANES == 0 else dff

    win = _pack_in_projection(w_in.astype(bf))
    wuq = jnp.take(_zero_col(mla_w_uq), uq_idx, axis=2)
    wuq = jnp.pad(wuq, ((0, 0), (0, MXU_WIDTH - MLA_Q_RANK), (0, 0))).astype(bf)
    wukv = _zero_col(mla_w_ukv)
    wukvk = jnp.take(wukv, ukvk_idx, axis=2).astype(bf)
    wukvv = jnp.take(wukv, ukvv_idx, axis=2).astype(bf)
    pair = lambda g: _pad_rows(_tile_rows(g, 2))
    rows = [jnp.zeros((depth, MXU_WIDTH), f32)] * P_ROWS
    rows[P_CQ] = _pad_rows(mla_cq_norm)
    rows[P_CKV] = _pad_rows(mla_ckv_norm)
    rows[P_GQ] = pair(mla_q_norm) * (LOG2E * MLA_QK ** -0.5)
    rows[P_GK] = pair(mla_k_norm)
    rows[P_FQ] = _tile_rows(fox_q_norm, N_HEADS) * (LOG2E * HEAD_DIM ** -0.5)
    rows[P_FK] = _tile_rows(fox_k_norm, N_HEADS)
    rows[P_FB] = _pad_rows(fox_b_f)
    rows[P_MQ] = _tile_rows(moba_q_norm, N_HEADS)
    rows[P_MK] = _tile_rows(moba_k_norm, N_HEADS)
    rows[P_DQ] = _tile_rows(diff_q_norm, 2 * N_HEADS) * (LOG2E * DIFF_QK ** -0.5)
    rows[P_DK] = _tile_rows(diff_k_norm, 2 * N_HEADS)
    rows[P_EQ] = _tile_rows(mem_q_norm, N_HEADS) * (LOG2E * HEAD_DIM ** -0.5)
    par = jnp.stack(rows, axis=1)
    anorm = attn_norm.astype(f32)[:, None, :]
    mnorm = mem_norm.astype(f32)[:, None, :]
    fnorm = ffn_norm.astype(f32)[:, None, :]
    wmem = mem_w_kv.astype(bf)
    mem_gain = _tile_rows(mem_k_norm, N_HEADS)[:, None, :]

    lam_init = jnp.asarray([0.8 - 0.6 * math.exp(-0.3 * l) for l in range(depth)], f32)
    lam_vec = diff_lambda.astype(f32)
    lam = (jnp.exp(jnp.sum(lam_vec[:, 0] * lam_vec[:, 1], axis=-1))
           - jnp.exp(jnp.sum(lam_vec[:, 2] * lam_vec[:, 3], axis=-1)) + lam_init)
    lam_row = jnp.broadcast_to(lam[:, None, None], (depth, 1, GROUP_WIDTH))
    gsub = (_tile_rows(diff_sub_norm, N_HEADS) * (1.0 - lam_init)[:, None])[:, None, :]

    wo = w_o.astype(bf)
    wg, wu, wd = ffn_w_gate.astype(bf), ffn_w_up.astype(bf), ffn_w_down.astype(bf)
    cw = jnp.pad(ffn_conv_w.astype(f32), ((0, 0), (0, SUBLANES - CONV_WIDTH), (0, 0)))
    cb = ffn_conv_b.astype(f32)[:, None, :]

    for l in range(depth):
        (aq, ak, av, fq, fk, fv, fdcol, fdrow, mq, mk, mv, msel, dq, dk, dv, eq) = _prep_call(
            l, t_prep, x, tr, anorm, win, wuq, wukvk, wukvv, gpair, g64, g32, expand, tril, vsel, par)
        ek, ev = _memkv_call(l, mem, mnorm, wmem, g64, mem_gain, vsel)

        o_a = _attn_call(_CFG_MLA, aq, ak, av, (), tile(_CFG_MLA), tile(_CFG_MLA))
        o_b = _attn_call(_CFG_FOX, fq, fk, fv, (fdcol, fdrow), tile(_CFG_FOX), tile(_CFG_FOX))
        o_c = _attn_call(_CFG_MOBA, mq, mk, mv, (msel,), tile(_CFG_MOBA), tile(_CFG_MOBA))
        o_d = _attn_call(_CFG_DIFF, dq, dk, dv, (l, g64, gsub, lam_row), tile(_CFG_DIFF), tile(_CFG_DIFF))
        o_e = _attn_call(_CFG_MEM, eq, ek, ev, (), tile(_CFG_MEM), mem.shape[1])

        x = _ffn_call(l, x, (o_a, o_b, o_c, o_d, o_e), wo, fnorm, wg, wu, cw, cb, wd, t_ffn, tf)
    return x
```

```python
import functools
import math

import numpy as np
import jax
import jax.numpy as jnp
from jax import lax
from jax.experimental import pallas as pl
from jax.experimental.pallas import tpu as pltpu

N_HEADS = 4
HEAD_DIM = 64
GROUP_WIDTH = N_HEADS * HEAD_DIM
MLA_Q_RANK = 192
MLA_KV_RANK = 128
MLA_NOPE = 64
MLA_ROPE = 32
MLA_QK = MLA_NOPE + MLA_ROPE
DIFF_QK = HEAD_DIM // 2
ROPE_THETA = 500000.0
ROT_MOBA = HEAD_DIM // 4
ROT_DIFF = DIFF_QK // 4
MOBA_BLOCK = 256
MOBA_TOPK = 3
CONV_WIDTH = 3
EPS = 1e-6
NEG_INF = -1e30
LOG2E = math.log2(math.e)
REMOVED = -3e38

LANES = 128
SUBLANES = 8
MXU_WIDTH = 256
TAIL_ROWS = 16
PREP_TILE = 512
VMEM_LIMIT_BYTES = 56 * 1024 * 1024

_SRC_CQ = 0
_SRC_CKV = _SRC_CQ + MLA_Q_RANK
_SRC_KR = _SRC_CKV + MLA_KV_RANK
_SRC_FOX = _SRC_KR + MLA_ROPE
_SRC_FOXF = _SRC_FOX + 3 * GROUP_WIDTH
_SRC_MOBA = _SRC_FOXF + N_HEADS
_SRC_DIFF = _SRC_MOBA + 3 * GROUP_WIDTH
_SRC_MEMQ = _SRC_DIFF + 3 * GROUP_WIDTH
_SRC_END = _SRC_MEMQ + GROUP_WIDTH

PK_CQ = 0
PK_CKV = 256
PK_KR = 384
PK_FQ, PK_FK, PK_FV = 896, 1152, 1408
PK_FF = 1664
PK_MQ, PK_MK, PK_MV = 1792, 2048, 2304
PK_DQ, PK_DK, PK_DV = 2560, 2816, 3072
PK_EQ = 3328
PK_END = 3584

PAIR_STRIDE = MLA_QK


def _pair_lane(h, d):
    return (h // 2) * MXU_WIDTH + (h % 2) * PAIR_STRIDE + d


N_FREQ = 32
FREQ_BASE_MLA = 0
FREQ_BASE_MOBA = MLA_ROPE // 2
FREQ_BASE_DIFF = FREQ_BASE_MOBA + ROT_MOBA // 2
TR_ONE = FREQ_BASE_DIFF + ROT_DIFF // 2
TR_WIDTH = 4 * N_FREQ
assert TR_ONE < N_FREQ and TR_WIDTH == LANES

(P_CQ, P_CKV, P_GQ, P_GK, P_FQ, P_FK, P_FB, P_MQ, P_MK, P_DQ, P_DK, P_EQ) = range(12)
P_ROWS = 16


def _np_in_index():
    idx = np.full((PK_END,), _SRC_END, np.int32)
    idx[PK_CQ:PK_CQ + MLA_Q_RANK] = np.arange(_SRC_CQ, _SRC_CQ + MLA_Q_RANK)
    idx[PK_CKV:PK_CKV + MLA_KV_RANK] = np.arange(_SRC_CKV, _SRC_CKV + MLA_KV_RANK)
    for h in range(N_HEADS):
        for d in range(MLA_ROPE):
            idx[PK_KR + _pair_lane(h, d)] = _SRC_KR + d
    idx[PK_FQ:PK_FQ + 3 * GROUP_WIDTH] = np.arange(_SRC_FOX, _SRC_FOX + 3 * GROUP_WIDTH)
    idx[PK_FF:PK_FF + N_HEADS] = np.arange(_SRC_FOXF, _SRC_FOXF + N_HEADS)
    idx[PK_MQ:PK_MQ + 3 * GROUP_WIDTH] = np.arange(_SRC_MOBA, _SRC_MOBA + 3 * GROUP_WIDTH)
    idx[PK_DQ:PK_DQ + 3 * GROUP_WIDTH] = np.arange(_SRC_DIFF, _SRC_DIFF + 3 * GROUP_WIDTH)
    idx[PK_EQ:PK_EQ + GROUP_WIDTH] = np.arange(_SRC_MEMQ, _SRC_MEMQ + GROUP_WIDTH)
    return idx


def _np_uq_index():
    idx = np.full((2 * MXU_WIDTH,), N_HEADS * MLA_QK, np.int32)
    for h in range(N_HEADS):
        for d in range(MLA_QK):
            idx[_pair_lane(h, d)] = h * MLA_QK + d
    return idx


def _np_ukv_index():
    zero = N_HEADS * (MLA_NOPE + HEAD_DIM)
    idx_k = np.full((2 * MXU_WIDTH,), zero, np.int32)
    idx_v = np.zeros((GROUP_WIDTH,), np.int32)
    for h in range(N_HEADS):
        for d in range(MLA_NOPE):
            idx_k[_pair_lane(h, MLA_ROPE + d)] = h * (MLA_NOPE + HEAD_DIM) + d
        for d in range(HEAD_DIM):
            idx_v[h * HEAD_DIM + d] = h * (MLA_NOPE + HEAD_DIM) + MLA_NOPE + d
    return idx_k, idx_v


def _np_group_matrix(groups):
    g = np.zeros((MXU_WIDTH, MXU_WIDTH), np.float32)
    for lo, size in groups:
        g[lo:lo + size, lo:lo + size] = 1.0 / size
    return g


_PAIR_GROUPS = [(0, MLA_ROPE), (MLA_ROPE, MLA_NOPE), (PAIR_STRIDE, MLA_ROPE), (PAIR_STRIDE + MLA_ROPE, MLA_NOPE)]
_G64_GROUPS = [(h * HEAD_DIM, HEAD_DIM) for h in range(N_HEADS)]
_G32_GROUPS = [(g * DIFF_QK, DIFF_QK) for g in range(2 * N_HEADS)]


def _np_rope_expand(regions, rot, base):
    half = rot // 2
    e = np.zeros((TR_WIDTH, 2 * MXU_WIDTH), np.float32)
    e[TR_ONE, 0:MXU_WIDTH] = 1.0
    for lo in regions:
        assert lo % rot == 0
        for r in range(half):
            f = base + r
            for lane, sign in ((lo + r, -1.0), (lo + half + r, 1.0)):
                e[TR_ONE, lane] = 0.0
                e[f, lane] = 1.0
                e[N_FREQ + f, lane] = 1.0
                e[2 * N_FREQ + f, MXU_WIDTH + lane] = sign
                e[3 * N_FREQ + f, MXU_WIDTH + lane] = sign
    return e


def _np_rope_expand_all():
    return np.concatenate([
        _np_rope_expand([0, PAIR_STRIDE], MLA_ROPE, FREQ_BASE_MLA),
        _np_rope_expand([h * HEAD_DIM for h in range(N_HEADS)], ROT_MOBA, FREQ_BASE_MOBA),
        _np_rope_expand([g * DIFF_QK for g in range(2 * N_HEADS)], ROT_DIFF, FREQ_BASE_DIFF),
    ], axis=1)


def _dot(a, b):
    return jnp.dot(a, b, preferred_element_type=jnp.float32)


def _dot_nt(a, b):
    return lax.dot_general(a, b, (((1,), (1,)), ((), ())), preferred_element_type=jnp.float32)


def _split2(a):
    hi = a.astype(jnp.bfloat16)
    lo = (a - hi.astype(jnp.float32)).astype(jnp.bfloat16)
    return hi, lo


def _split3(a):
    hi = a.astype(jnp.bfloat16)
    r = a - hi.astype(jnp.float32)
    mid = r.astype(jnp.bfloat16)
    lo = (r - mid.astype(jnp.float32)).astype(jnp.bfloat16)
    return hi, mid, lo


def _group_mean_sq(a, g_bf16):
    return _dot((a * a).astype(jnp.bfloat16), g_bf16)


def _rope(x, tabs, half):
    w = x.shape[-1]
    lane = lax.broadcasted_iota(jnp.int32, x.shape, 1)
    partner = jnp.where((lane & (2 * half - 1)) >= half, pltpu.roll(x, half, 1), pltpu.roll(x, w - half, 1))
    return x * tabs[:, 0:w] + partner * tabs[:, w:2 * w]


def _lane_mask(shape, lo, hi):
    lane = lax.broadcasted_iota(jnp.int32, shape, len(shape) - 1)
    return (lane >= lo) & (lane < hi)


def _prep_body(x_ref, tr_ref, anorm_ref, win_ref, wuq_ref, wukvk_ref, wukvv_ref, gpair_ref, g64_ref, g32_ref,
               exp_ref, tril_ref, vsel_ref, par_ref,
               aq_ref, ak_ref, av_ref, fq_ref, fk_ref, fv_ref, fdcol_ref, fdrow_ref, mq_ref, mk_ref, mv_ref, msel_ref,
               dq_ref, dk_ref, dv_ref, eq_ref,
               kmean_s, carry_s):
    j = pl.program_id(1)
    tm = x_ref.shape[1]
    pr = MOBA_BLOCK
    n_parts = tm // pr
    nbp = kmean_s.shape[0]
    bf = jnp.bfloat16
    gpair, g64, g32 = gpair_ref[...], g64_ref[...], g32_ref[...]
    vsel = vsel_ref[...]
    tril = tril_ref[...]

    @pl.when(j == 0)
    def _():
        kmean_s[...] = jnp.zeros_like(kmean_s)
        carry_s[...] = jnp.zeros_like(carry_s)

    def prow(r, width=MXU_WIDTH):
        return par_ref[r:r + 1, 0:width]

    def project(st):
        rows = st["rows"]
        x = x_ref[0, rows, :]
        xb = (x * lax.rsqrt(jnp.mean(x * x, axis=-1, keepdims=True) + EPS) * anorm_ref[...]).astype(bf)
        tabs = _dot(tr_ref[0, rows, :], exp_ref[...])
        st["tab_mla"] = tabs[:, 0:2 * MXU_WIDTH]
        st["tab_moba"] = tabs[:, 2 * MXU_WIDTH:4 * MXU_WIDTH]
        st["tab_diff"] = tabs[:, 4 * MXU_WIDTH:6 * MXU_WIDTH]

        def run(lo, hi):
            wide = _dot(xb, win_ref[:, lo:hi])
            return lambda off, width: wide[:, off - lo:off - lo + width]

        run_m = run(PK_MQ, PK_DQ)
        run_a = run(PK_CQ, PK_FQ)
        run_f = run(PK_FQ, PK_MQ)
        run_d = run(PK_DQ, PK_END)
        st["direct"] = {"mq": run_m(PK_MQ, MXU_WIDTH), "mk": run_m(PK_MK, MXU_WIDTH),
                        "fq": run_f(PK_FQ, MXU_WIDTH), "fk": run_f(PK_FK, MXU_WIDTH),
                        "dq": run_d(PK_DQ, MXU_WIDTH), "dk": run_d(PK_DK, MXU_WIDTH),
                        "eq": run_d(PK_EQ, MXU_WIDTH)}
        st["values"] = {"fv": run_f(PK_FV, MXU_WIDTH).astype(bf), "mv": run_m(PK_MV, MXU_WIDTH).astype(bf),
                        "dv": run_d(PK_DV, MXU_WIDTH).astype(bf)}
        p_cq, p_ckv, st["p_kr"] = run_a(PK_CQ, MXU_WIDTH), run_a(PK_CKV, MLA_KV_RANK), run_a(PK_KR, 2 * MXU_WIDTH)
        cqn = p_cq * lax.rsqrt(jnp.sum(p_cq * p_cq, axis=-1, keepdims=True) * (1.0 / MLA_Q_RANK) + EPS) * prow(P_CQ)
        ckvn = p_ckv * lax.rsqrt(jnp.mean(p_ckv * p_ckv, axis=-1, keepdims=True) + EPS) * prow(P_CKV, MLA_KV_RANK)
        st["cqb"], st["ckvb"] = cqn.astype(bf), ckvn.astype(bf)
        z = run_f(PK_FF, LANES) + prow(P_FB, LANES)
        log_f = jnp.minimum(z, 0.0) - jnp.log1p(jnp.exp(-jnp.abs(z)))
        st["log_f"] = _split3(jnp.where(_lane_mask(log_f.shape, 0, N_HEADS), log_f, 0.0))

    def second(st):
        gmat = {"fq": g64, "fk": g64, "mq": g64, "mk": g64, "dq": g32, "dk": g32, "eq": g64}
        gains = {"fq": P_FQ, "fk": P_FK, "mq": P_MQ, "mk": P_MK, "dq": P_DQ, "dk": P_DK, "eq": P_EQ}
        ms = {name: _group_mean_sq(a, gmat[name]) for name, a in st["direct"].items()}
        st["qa"] = _dot(st["cqb"], wuq_ref[...])
        st["ka"] = st["p_kr"] + _dot(st["ckvb"], wukvk_ref[...])
        st["values"]["av"] = _dot(st["ckvb"], wukvv_ref[...]).astype(bf)
        l1, l2, l3 = st["log_f"]
        st["cumsum"] = (_dot(tril, l1) + _dot(tril, l2)) + _dot(tril, l3)
        st["normed"] = {name: a * lax.rsqrt(ms[name] + EPS) * prow(gains[name]) for name, a in st["direct"].items()}

    def third(st):
        rows = st["rows"]
        mq = _rope(st["normed"]["mq"], st["tab_moba"], ROT_MOBA // 2)
        mk = _rope(st["normed"]["mk"], st["tab_moba"], ROT_MOBA // 2)
        st["mq"], st["mk"] = mq, mk
        kmean_s[pl.ds(st["blk"], 1), :] = jnp.mean(mk, axis=0, keepdims=True)
        km_hi, km_lo = _split2(kmean_s[...])
        st["gates"] = []
        for h in range(N_HEADS):
            q_hi, q_lo = _split2(jnp.where(_lane_mask(mq.shape, h * HEAD_DIM, (h + 1) * HEAD_DIM), mq, 0.0))
            st["gates"].append((_dot_nt(km_hi, q_hi) + _dot_nt(km_lo, q_hi)) + _dot_nt(km_hi, q_lo))
        st["ms_qa"] = [_group_mean_sq(st["qa"][:, p * MXU_WIDTH:(p + 1) * MXU_WIDTH], gpair) for p in range(2)]
        st["ms_ka"] = [_group_mean_sq(st["ka"][:, p * MXU_WIDTH:(p + 1) * MXU_WIDTH], gpair) for p in range(2)]
        dec = carry_s[...] + st["cumsum"]
        carry_s[...] = dec[pr - 1:pr, :]
        dec2 = dec * LOG2E
        d1, d2, d3 = _split3(dec2)
        for h in range(N_HEADS):
            fdcol_ref[0, rows, h * LANES:(h + 1) * LANES] = jnp.broadcast_to(dec2[:, h:h + 1], (pr, LANES))
        row_sel = jnp.where(lax.broadcasted_iota(jnp.int32, (SUBLANES, LANES), 0)
                            == lax.broadcasted_iota(jnp.int32, (SUBLANES, LANES), 1), 1.0, 0.0).astype(bf)
        fdrow_ref[0, :, rows] = (_dot_nt(row_sel, d1) + _dot_nt(row_sel, d2)) + _dot_nt(row_sel, d3)
        for name, ref in (("fv", fv_ref), ("mv", mv_ref), ("dv", dv_ref), ("av", av_ref)):
            ref[0, :, rows] = _values_t(st["values"][name], vsel)

    def finish(st):
        rows = st["rows"]
        normed = st["normed"]
        fq_ref[0, rows, :] = normed["fq"].astype(bf)
        fk_ref[0, rows, :] = normed["fk"].astype(bf)
        eq_ref[0, rows, :] = normed["eq"].astype(bf)
        dq_ref[0, rows, :] = _rope(normed["dq"], st["tab_diff"], ROT_DIFF // 2).astype(bf)
        dk_ref[0, rows, :] = _rope(normed["dk"], st["tab_diff"], ROT_DIFF // 2).astype(bf)
        mq_ref[0, rows, :] = (st["mq"] * (LOG2E * HEAD_DIM ** -0.5)).astype(bf)
        mk_ref[0, rows, :] = st["mk"].astype(bf)
        for p in range(2):
            sl = slice(p * MXU_WIDTH, (p + 1) * MXU_WIDTH)
            aq_ref[0, rows, sl] = _rope(st["qa"][:, sl] * lax.rsqrt(st["ms_qa"][p] + EPS) * prow(P_GQ),
                                        st["tab_mla"], MLA_ROPE // 2).astype(bf)
            ak_ref[0, rows, sl] = _rope(st["ka"][:, sl] * lax.rsqrt(st["ms_ka"][p] + EPS) * prow(P_GK),
                                        st["tab_mla"], MLA_ROPE // 2).astype(bf)
        blk = lax.broadcasted_iota(jnp.int32, (nbp, pr), 0)
        past = blk < st["blk"]
        for h in range(N_HEADS):
            work = jnp.where(past, st["gates"][h], NEG_INF)
            sel = jnp.zeros((nbp, pr), jnp.bool_)
            for _ in range(MOBA_TOPK):
                mx = jnp.max(work, axis=0, keepdims=True)
                first = jnp.min(jnp.where(work == mx, blk, nbp), axis=0, keepdims=True)
                pick = blk == first
                sel = sel | pick
                work = jnp.where(pick, REMOVED, work)
            msel_ref[0, h * nbp:(h + 1) * nbp, rows] = jnp.where(sel & past, 0.0, NEG_INF)

    parts = [{"rows": slice(p * pr, (p + 1) * pr), "blk": j * n_parts + p} for p in range(n_parts)]
    for phase in (project, second, third, finish):
        for st in parts:
            phase(st)


def _const_spec(a, layer=None):
    if layer is None:
        n = a.ndim
        return pl.BlockSpec(a.shape, lambda *_: (0,) * n)
    n = a.ndim - 1
    return pl.BlockSpec((None,) + a.shape[1:], lambda *_: (layer,) + (0,) * n)


def _moba_blocks_padded(seq):
    return -(-(seq // MOBA_BLOCK) // SUBLANES) * SUBLANES


def _prep_call(layer, tm, x, tr, anorm, win, wuq, wukvk, wukvv, gpair, g64, g32, expand, tril, vsel, par):
    bsz, seq, d = x.shape
    assert seq % tm == 0 and tm % MOBA_BLOCK == 0 and tril.shape == (MOBA_BLOCK, MOBA_BLOCK)
    nbp = _moba_blocks_padded(seq)
    bf = jnp.bfloat16
    f32 = jnp.float32

    vt = -N_HEADS * V_ROWS
    widths = [(2 * MXU_WIDTH, bf), (2 * MXU_WIDTH, bf), (vt, bf),
              (GROUP_WIDTH, bf), (GROUP_WIDTH, bf), (vt, bf),
              (N_HEADS * LANES, f32), (-SUBLANES, f32),
              (GROUP_WIDTH, bf), (GROUP_WIDTH, bf), (vt, bf), (-N_HEADS * nbp, f32),
              (GROUP_WIDTH, bf), (GROUP_WIDTH, bf), (vt, bf),
              (GROUP_WIDTH, bf)]

    def tok(width):
        if width > 0:
            return pl.BlockSpec((1, tm, width), lambda b, j: (b, j, 0))
        return pl.BlockSpec((1, -width, tm), lambda b, j: (b, 0, j))

    def shape(width):
        return (bsz, seq, width) if width > 0 else (bsz, -width, seq)

    consts = [anorm, win, wuq, wukvk, wukvv, gpair, g64, g32, expand, tril, vsel, par]
    layered = [True, True, True, True, True, False, False, False, False, False, False, True]
    return pl.pallas_call(
        _prep_body,
        grid=(bsz, seq // tm),
        in_specs=[tok(d), tok(TR_WIDTH)] + [_const_spec(c, layer if ly else None) for c, ly in zip(consts, layered)],
        out_specs=[tok(w) for w, _ in widths],
        out_shape=[jax.ShapeDtypeStruct(shape(w), dt) for w, dt in widths],
        scratch_shapes=[pltpu.VMEM((nbp, GROUP_WIDTH), jnp.float32), pltpu.VMEM((1, LANES), jnp.float32)],
        compiler_params=pltpu.CompilerParams(dimension_semantics=("arbitrary", "arbitrary"),
                                             vmem_limit_bytes=VMEM_LIMIT_BYTES),
        name="prep",
    )(x, tr, *consts)


def _memkv_body(mem_ref, mnorm_ref, w_ref, g64_ref, gain_ref, vsel_ref, k_ref, v_ref):
    m = mem_ref[0]
    mn = m * lax.rsqrt(jnp.mean(m * m, axis=-1, keepdims=True) + EPS) * mnorm_ref[...]
    kv = _dot(mn.astype(jnp.bfloat16), w_ref[...])
    k = kv[:, 0:GROUP_WIDTH]
    k = k * lax.rsqrt(_group_mean_sq(k, g64_ref[...]) + EPS) * gain_ref[...]
    k_ref[0] = k.astype(jnp.bfloat16)
    v_ref[0] = _values_t(kv[:, GROUP_WIDTH:2 * GROUP_WIDTH].astype(jnp.bfloat16), vsel_ref[...])


def _memkv_call(layer, mem, mnorm, w, g64, gain, vsel):
    bsz, mlen, d = mem.shape
    k_shape, vt_shape = (bsz, mlen, GROUP_WIDTH), (bsz, N_HEADS * V_ROWS, mlen)
    return pl.pallas_call(
        _memkv_body,
        grid=(bsz,),
        in_specs=[pl.BlockSpec((1, mlen, d), lambda b: (b, 0, 0)), _const_spec(mnorm, layer), _const_spec(w, layer),
                  _const_spec(g64), _const_spec(gain, layer), _const_spec(vsel)],
        out_specs=[pl.BlockSpec((1,) + s[1:], lambda b: (b, 0, 0)) for s in (k_shape, vt_shape)],
        out_shape=[jax.ShapeDtypeStruct(s, jnp.bfloat16) for s in (k_shape, vt_shape)],
        compiler_params=pltpu.CompilerParams(dimension_semantics=("arbitrary",), vmem_limit_bytes=VMEM_LIMIT_BYTES),
        name="mem_kv",
    )(mem, mnorm, w, g64, gain, vsel)


Q_SUB = 2 * MXU_WIDTH


class _AttnCfg:
    def __init__(self, name, vheads, n_maps, causal, decay=False, select=False, diff=False, tile=1024, lookahead=8,
                 q_sub=Q_SUB):
        self.name = name
        self.vheads = vheads
        self.n_maps = n_maps
        self.causal = causal
        self.decay = decay
        self.select = select
        self.diff = diff
        self.tile = tile
        self.lookahead = lookahead
        self.q_sub = q_sub


_PLAIN_VHEADS = [(0, h * HEAD_DIM, (h + 1) * HEAD_DIM, 0, h) for h in range(N_HEADS)]
_CFG_MLA = _AttnCfg("attn_mla", [((h // 2) * MXU_WIDTH, (h % 2) * PAIR_STRIDE, (h % 2) * PAIR_STRIDE + MLA_QK, 0, h)
                                 for h in range(N_HEADS)], 1, True, q_sub=MXU_WIDTH, lookahead=16)
_CFG_FOX = _AttnCfg("attn_fox", _PLAIN_VHEADS, 1, True, decay=True, tile=2048, lookahead=6)
_CFG_MOBA = _AttnCfg("attn_moba", _PLAIN_VHEADS, 1, True, select=True, tile=2048, lookahead=8)
_CFG_DIFF = _AttnCfg("attn_diff", [(0, h * HEAD_DIM + c * DIFF_QK, h * HEAD_DIM + (c + 1) * DIFF_QK, c, h)
                                   for c in range(2) for h in range(N_HEADS)], 2, True, diff=True, q_sub=MXU_WIDTH,
                     lookahead=32)
_CFG_MEM = _AttnCfg("attn_mem", _PLAIN_VHEADS, 1, False, tile=2048)


ONES_ROW = HEAD_DIM
V_ROWS = HEAD_DIM + 16


def _np_value_select():
    sel = np.zeros((N_HEADS * V_ROWS, GROUP_WIDTH), np.float32)
    for h in range(N_HEADS):
        for d in range(HEAD_DIM):
            sel[h * V_ROWS + d, h * HEAD_DIM + d] = 1.0
    return sel


def _values_t(v, vsel):
    vt = _dot_nt(vsel, v)
    row = lax.broadcasted_iota(jnp.int32, vt.shape, 0)
    ones = row == ONES_ROW
    for h in range(1, N_HEADS):
        ones = ones | (row == h * V_ROWS + ONES_ROW)
    return jnp.where(ones, 1.0, vt).astype(jnp.bfloat16)


def _tile_lanes(x, width):
    return jnp.tile(x, (1, width // LANES)) if width != LANES else x


def _attn_body(cfg, qi_ref, kj_ref, *refs):
    refs = list(refs)
    q_ref, k_ref, vt_ref = refs[:3]
    pos = 3
    if cfg.decay:
        dq_ref, dk_ref = refs[pos:pos + 2]
        pos += 2
    if cfg.select:
        sel_ref = refs[pos]
        pos += 1
    if cfg.diff:
        g64_ref, gsub_ref, lam_ref = refs[pos:pos + 3]
        pos += 3
    o_ref, qm_s, m_s, acc_s = refs[pos:pos + 4]

    t = pl.program_id(1)
    i = qi_ref[t]
    j = kj_ref[t]
    tq = q_ref.shape[1]
    tk = k_ref.shape[1]

    @pl.when(j == 0)
    def _():
        for n, (off, lo, hi, _, _) in enumerate(cfg.vheads):
            qb = q_ref[0, :, off:off + MXU_WIDTH]
            qm_s[n] = jnp.where(_lane_mask(qb.shape, lo, hi), qb, jnp.zeros_like(qb))
        m_s[...] = jnp.full(m_s.shape, NEG_INF, jnp.float32)
        acc_s[...] = jnp.zeros_like(acc_s)

    def step(diag):
        qs = min(tq, cfg.q_sub)
        items = [(n, u) for n in range(len(cfg.vheads)) for u in range(tq // qs)]

        def n_keys(u):
            return (u + 1) * qs if diag else tk

        def scores(item):
            n, u = item
            off, _, _, _, h = cfg.vheads[n]
            nk = n_keys(u)
            cols = slice(u * qs, (u + 1) * qs)
            s = _dot_nt(k_ref[0, 0:nk, off:off + MXU_WIDTH], qm_s[n, cols, :])
            if cfg.decay:
                s = (dq_ref[0, h:h + 1, cols] - _tile_lanes(dk_ref[0, 0:nk, h * LANES:(h + 1) * LANES], qs)) + s
            if cfg.select:
                nbp = sel_ref.shape[1] // N_HEADS
                qpos = u * qs + lax.broadcasted_iota(jnp.int32, (1, qs), 1)
                parts = []
                for kb in range(nk // MOBA_BLOCK):
                    rows = s[kb * MOBA_BLOCK:(kb + 1) * MOBA_BLOCK, :]
                    if not (diag and kb == nk // MOBA_BLOCK - 1):
                        bias = sel_ref[0, pl.ds(h * nbp + j * (tk // MOBA_BLOCK) + kb, 1), cols]
                        if diag:
                            bias = jnp.where(qpos < (kb + 1) * MOBA_BLOCK, 0.0, bias)
                        rows = rows + bias
                    parts.append(rows)
                s = parts[0] if len(parts) == 1 else jnp.concatenate(parts, axis=0)
            if diag:
                key = lax.broadcasted_iota(jnp.int32, (nk, qs), 0)
                qry = u * qs + lax.broadcasted_iota(jnp.int32, (nk, qs), 1)
                s = jnp.where(key <= qry, s, NEG_INF)
            return s, jnp.max(s, axis=0, keepdims=True)

        raw = {it: scores(items[it]) for it in range(min(cfg.lookahead, len(items)))}
        for it, (n, u) in enumerate(items):
            h = cfg.vheads[n][4]
            nk = n_keys(u)
            cols = slice(u * qs, (u + 1) * qs)
            s, s_max = raw.pop(it)
            m_prev = m_s[n, :, cols]
            m_new = jnp.maximum(m_prev, s_max)
            alpha = jnp.exp2(m_prev - m_new)
            p = jnp.exp2(s - m_new)
            m_s[n, :, cols] = m_new
            acc_s[n, :, cols] = acc_s[n, :, cols] * alpha + _dot(vt_ref[0, h * V_ROWS:(h + 1) * V_ROWS, 0:nk],
                                                                 p.astype(jnp.bfloat16))
            if it + cfg.lookahead < len(items):
                raw[it + cfg.lookahead] = scores(items[it + cfg.lookahead])

    if cfg.causal:
        pl.when(j < i)(functools.partial(step, False))
        pl.when(j == i)(functools.partial(step, True))
        last = j == i
    else:
        step(False)
        last = j == 0

    @pl.when(last)
    def _():
        outs = []
        for c in range(cfg.n_maps):
            heads = []
            for h in range(N_HEADS):
                acc = acc_s[c * N_HEADS + h]
                heads.append(acc[0:HEAD_DIM, :] / acc[ONES_ROW:ONES_ROW + 1, :])
            outs.append(jnp.concatenate(heads, axis=0).T)
        if cfg.diff:
            o = outs[0] - lam_ref[0:1, :] * outs[1]
            o = o * lax.rsqrt(_group_mean_sq(o, g64_ref[...]) + EPS) * gsub_ref[...]
        else:
            o = outs[0]
        o_ref[0] = o.astype(o_ref.dtype)


def _attn_call(cfg, q, k, v, extras, tq, tk):
    bsz, seq, wq = q.shape
    sk = k.shape[1]
    nq = seq // tq
    assert seq % tq == 0 and sk % tk == 0
    if cfg.causal:
        assert tq == tk and sk == seq
        pairs = [(i, j) for i in range(nq) for j in range(i + 1)]
    else:
        assert sk == tk
        pairs = [(i, 0) for i in range(nq)]
    qi = jnp.asarray(np.array([p[0] for p in pairs], np.int32))
    kj = jnp.asarray(np.array([p[1] for p in pairs], np.int32))
    n_vh = len(cfg.vheads)

    in_specs = [pl.BlockSpec((1, tq, wq), lambda b, t, qi, kj: (b, qi[t], 0)),
                pl.BlockSpec((1, tk, wq), lambda b, t, qi, kj: (b, kj[t], 0)),
                pl.BlockSpec((1, N_HEADS * V_ROWS, tk), lambda b, t, qi, kj: (b, 0, kj[t]))]
    args = [q, k, v]
    if cfg.decay:
        dcol, drow = extras
        in_specs += [pl.BlockSpec((1, SUBLANES, tq), lambda b, t, qi, kj: (b, 0, qi[t])),
                     pl.BlockSpec((1, tk, N_HEADS * LANES), lambda b, t, qi, kj: (b, kj[t], 0))]
        args += [drow, dcol]
    if cfg.select:
        (sel,) = extras
        in_specs += [pl.BlockSpec((1, sel.shape[1], tq), lambda b, t, qi, kj: (b, 0, qi[t]))]
        args += [sel]
    if cfg.diff:
        layer, g64, gsub, lam_row = extras
        in_specs += [_const_spec(g64), _const_spec(gsub, layer), _const_spec(lam_row, layer)]
        args += [g64, gsub, lam_row]

    grid_spec = pltpu.PrefetchScalarGridSpec(
        num_scalar_prefetch=2,
        grid=(bsz, len(pairs)),
        in_specs=in_specs,
        out_specs=pl.BlockSpec((1, tq, GROUP_WIDTH), lambda b, t, qi, kj: (b, qi[t], 0)),
        scratch_shapes=[pltpu.VMEM((n_vh, tq, MXU_WIDTH), jnp.bfloat16),
                        pltpu.VMEM((n_vh, 1, tq), jnp.float32),
                        pltpu.VMEM((n_vh, V_ROWS, tq), jnp.float32)])
    return pl.pallas_call(
        functools.partial(_attn_body, cfg),
        grid_spec=grid_spec,
        out_shape=jax.ShapeDtypeStruct((bsz, seq, GROUP_WIDTH), jnp.bfloat16),
        compiler_params=pltpu.CompilerParams(dimension_semantics=("arbitrary", "arbitrary"),
                                             vmem_limit_bytes=VMEM_LIMIT_BYTES),
        name=cfg.name,
    )(qi, kj, *args)


def _ffn_body(nf, x_ref, oa_ref, ob_ref, oc_ref, od_ref, oe_ref, wo_ref, fnorm_ref, wg_ref, wu_ref, cw_ref, cb_ref,
              wd_ref, out_ref, xnew_s, xn_s, acc_s):
    i = pl.program_id(1)
    f = pl.program_id(2)
    tm = x_ref.shape[1]

    @pl.when(f == 0)
    def _():
        @pl.when(i == 0)
        def _():
            xn_s[0:TAIL_ROWS, :] = jnp.zeros((TAIL_ROWS, xn_s.shape[1]), xn_s.dtype)

        @pl.when(i > 0)
        def _():
            xn_s[0:TAIL_ROWS, :] = xn_s[tm:tm + TAIL_ROWS, :]

        mixed = jnp.concatenate([o_ref[0] for o_ref in (oa_ref, ob_ref, oc_ref, od_ref, oe_ref)], axis=1)
        xnew = x_ref[0] + _dot(mixed, wo_ref[...])
        xnew_s[...] = xnew
        xn = xnew * lax.rsqrt(jnp.mean(xnew * xnew, axis=-1, keepdims=True) + EPS) * fnorm_ref[...]
        xn_s[TAIL_ROWS:TAIL_ROWS + tm, :] = xn.astype(xn_s.dtype)

    def mlp_chunk():
        ge = _dot(xn_s[...], wg_ref[...])
        u = _dot(xn_s[TAIL_ROWS:TAIL_ROWS + tm, :], wu_ref[...])
        g0 = ge[TAIL_ROWS:TAIL_ROWS + tm, :]
        t1 = ge[TAIL_ROWS - 1:TAIL_ROWS, :]
        t2 = ge[TAIL_ROWS - 2:TAIL_ROWS - 1, :]
        row = lax.broadcasted_iota(jnp.int32, g0.shape, 0)
        g1 = jnp.where(row == 0, t1, pltpu.roll(g0, 1, 0))
        g2 = jnp.where(row == 0, t2, jnp.where(row == 1, t1, pltpu.roll(g0, 2, 0)))
        y = cb_ref[...] + cw_ref[0:1, :] * g2
        y = y + cw_ref[1:2, :] * g1
        y = y + cw_ref[2:3, :] * g0
        hmid = (y * (1.0 / (1.0 + jnp.exp(-y)))) * u
        return _dot(hmid.astype(jnp.bfloat16), wd_ref[...])

    @pl.when(f == 0)
    def _():
        acc_s[...] = mlp_chunk()

    if nf > 2:
        @pl.when((f > 0) & (f < nf - 1))
        def _():
            acc_s[...] += mlp_chunk()

    @pl.when(f == nf - 1)
    def _():
        out_ref[0] = xnew_s[...] + (acc_s[...] + mlp_chunk())


def _ffn_call(layer, x, outs, wo, fnorm, wg, wu, cw, cb, wd, tm, tf):
    bsz, seq, d = x.shape
    dff = wg.shape[2]
    n_tiles, nf = seq // tm, dff // tf
    assert seq % tm == 0 and dff % tf == 0 and nf >= 2
    tok = lambda w: pl.BlockSpec((1, tm, w), lambda b, i, f: (b, i, 0))
    return pl.pallas_call(
        functools.partial(_ffn_body, nf),
        grid=(bsz, n_tiles, nf),
        in_specs=[tok(d)] + [tok(GROUP_WIDTH)] * 5 + [
            _const_spec(wo, layer),
            _const_spec(fnorm, layer),
            pl.BlockSpec((None, d, tf), lambda b, i, f: (layer, 0, f)),
            pl.BlockSpec((None, d, tf), lambda b, i, f: (layer, 0, f)),
            pl.BlockSpec((None, SUBLANES, tf), lambda b, i, f: (layer, 0, f)),
            pl.BlockSpec((None, 1, tf), lambda b, i, f: (layer, 0, f)),
            pl.BlockSpec((None, tf, d), lambda b, i, f: (layer, f, 0))],
        out_specs=tok(d),
        out_shape=jax.ShapeDtypeStruct((bsz, seq, d), jnp.float32),
        scratch_shapes=[pltpu.VMEM((tm, d), jnp.float32), pltpu.VMEM((TAIL_ROWS + tm, d), jnp.bfloat16),
                        pltpu.VMEM((tm, d), jnp.float32)],
        compiler_params=pltpu.CompilerParams(dimension_semantics=("arbitrary", "arbitrary", "arbitrary"),
                                             vmem_limit_bytes=VMEM_LIMIT_BYTES),
        name="ffn",
    )(x, *outs, wo, fnorm, wg, wu, cw, cb, wd)


def _pad_rows(v, width=MXU_WIDTH):
    return jnp.pad(v.astype(jnp.float32), ((0, 0), (0, width - v.shape[1])))


def _tile_rows(g, reps):
    return jnp.tile(g.astype(jnp.float32), (1, reps))


def _pack_in_projection(w):
    idx = _np_in_index()
    pieces, start = [], 0
    while start < PK_END:
        stop = start + 1
        if idx[start] == _SRC_END:
            while stop < PK_END and idx[stop] == _SRC_END:
                stop += 1
            pieces.append(jnp.zeros(w.shape[:-1] + (stop - start,), w.dtype))
        else:
            while stop < PK_END and idx[stop] == idx[stop - 1] + 1:
                stop += 1
            pieces.append(w[..., int(idx[start]):int(idx[stop - 1]) + 1])
        start = stop
    return jnp.concatenate(pieces, axis=-1)


def _zero_col(w):
    return jnp.concatenate([w, jnp.zeros(w.shape[:-1] + (1,), w.dtype)], axis=-1)


def _rope_table(positions):
    pos = positions.astype(jnp.float32)[:, :, None]
    inv = [ROPE_THETA ** (-jnp.arange(0, rot, 2, dtype=jnp.float32) / rot) for rot in (MLA_ROPE, ROT_MOBA, ROT_DIFF)]
    inv = jnp.concatenate(inv + [jnp.zeros((N_FREQ - TR_ONE,), jnp.float32)])
    ang = pos * inv
    c, s = jnp.cos(ang), jnp.sin(ang)
    c_hi = c.astype(jnp.bfloat16)
    c_lo = (c - c_hi.astype(jnp.float32)).astype(jnp.bfloat16)
    s_hi = s.astype(jnp.bfloat16)
    s_lo = (s - s_hi.astype(jnp.float32)).astype(jnp.bfloat16)
    return jnp.concatenate([c_hi, c_lo, s_hi, s_lo], axis=-1)


def _pick_tile(n, pref):
    t = pref
    while n % t:
        t //= 2
    return t


def kernel(x, mem, positions, attn_norm, ffn_norm, mem_norm, w_in, mla_cq_norm, mla_ckv_norm, mla_w_uq, mla_w_ukv, mla_q_norm, mla_k_norm, fox_b_f, fox_q_norm, fox_k_norm, moba_q_norm, moba_k_norm, diff_lambda, diff_q_norm, diff_k_norm, diff_sub_norm, mem_w_kv, mem_q_norm, mem_k_norm, w_o, ffn_w_gate, ffn_w_up, ffn_conv_w, ffn_conv_b, ffn_w_down):
    bsz, seq, d = x.shape
    depth = w_in.shape[0]
    dff = ffn_w_gate.shape[2]
    bf = jnp.bfloat16
    f32 = jnp.float32

    uq_idx = _np_uq_index()
    ukvk_idx, ukvv_idx = _np_ukv_index()
    gpair = jnp.asarray(_np_group_matrix(_PAIR_GROUPS), bf)
    g64 = jnp.asarray(_np_group_matrix(_G64_GROUPS), bf)
    g32 = jnp.asarray(_np_group_matrix(_G32_GROUPS), bf)
    expand = jnp.asarray(_np_rope_expand_all(), bf)
    t_prep = max(_pick_tile(seq, PREP_TILE), MOBA_BLOCK)
    tril = jnp.asarray(np.tril(np.ones((MOBA_BLOCK, MOBA_BLOCK), np.float32)), bf)
    vsel = jnp.asarray(_np_value_select(), bf)
    tr = _rope_table(positions)

    tile = lambda cfg: max(_pick_tile(seq, cfg.tile), MOBA_BLOCK) if cfg.select else _pick_tile(seq, cfg.tile)
    t_ffn = _pick_tile(seq, 512)
    tf = dff // 2 if (dff // 2) % LANES == 0 else dff

    win = _pack_in_projection(w_in.astype(bf))
    wuq = jnp.take(_zero_col(mla_w_uq), uq_idx, axis=2)
    wuq = jnp.pad(wuq, ((0, 0), (0, MXU_WIDTH - MLA_Q_RANK), (0, 0))).astype(bf)
    wukv = _zero_col(mla_w_ukv)
    wukvk = jnp.take(wukv, ukvk_idx, axis=2).astype(bf)
    wukvv = jnp.take(wukv, ukvv_idx, axis=2).astype(bf)
    pair = lambda g: _pad_rows(_tile_rows(g, 2))
    rows = [jnp.zeros((depth, MXU_WIDTH), f32)] * P_ROWS
    rows[P_CQ] = _pad_rows(mla_cq_norm)
    rows[P_CKV] = _pad_rows(mla_ckv_norm)
    rows[P_GQ] = pair(mla_q_norm) * (LOG2E * MLA_QK ** -0.5)
    rows[P_GK] = pair(mla_k_norm)
    rows[P_FQ] = _tile_rows(fox_q_norm, N_HEADS) * (LOG2E * HEAD_DIM ** -0.5)
    rows[P_FK] = _tile_rows(fox_k_norm, N_HEADS)
    rows[P_FB] = _pad_rows(fox_b_f)
    rows[P_MQ] = _tile_rows(moba_q_norm, N_HEADS)
    rows[P_MK] = _tile_rows(moba_k_norm, N_HEADS)
    rows[P_DQ] = _tile_rows(diff_q_norm, 2 * N_HEADS) * (LOG2E * DIFF_QK ** -0.5)
    rows[P_DK] = _tile_rows(diff_k_norm, 2 * N_HEADS)
    rows[P_EQ] = _tile_rows(mem_q_norm, N_HEADS) * (LOG2E * HEAD_DIM ** -0.5)
    par = jnp.stack(rows, axis=1)
    anorm = attn_norm.astype(f32)[:, None, :]
    mnorm = mem_norm.astype(f32)[:, None, :]
    fnorm = ffn_norm.astype(f32)[:, None, :]
    wmem = mem_w_kv.astype(bf)
    mem_gain = _tile_rows(mem_k_norm, N_HEADS)[:, None, :]

    lam_init = jnp.asarray([0.8 - 0.6 * math.exp(-0.3 * l) for l in range(depth)], f32)
    lam_vec = diff_lambda.astype(f32)
    lam = (jnp.exp(jnp.sum(lam_vec[:, 0] * lam_vec[:, 1], axis=-1))
           - jnp.exp(jnp.sum(lam_vec[:, 2] * lam_vec[:, 3], axis=-1)) + lam_init)
    lam_row = jnp.broadcast_to(lam[:, None, None], (depth, 1, GROUP_WIDTH))
    gsub = (_tile_rows(diff_sub_norm, N_HEADS) * (1.0 - lam_init)[:, None])[:, None, :]

    wo = w_o.astype(bf)
    wg, wu, wd = ffn_w_gate.astype(bf), ffn_w_up.astype(bf), ffn_w_down.astype(bf)
    cw = jnp.pad(ffn_conv_w.astype(f32), ((0, 0), (0, SUBLANES - CONV_WIDTH), (0, 0)))
    cb = ffn_conv_b.astype(f32)[:, None, :]

    for l in range(depth):
        (aq, ak, av, fq, fk, fv, fdcol, fdrow, mq, mk, mv, msel, dq, dk, dv, eq) = _prep_call(
            l, t_prep, x, tr, anorm, win, wuq, wukvk, wukvv, gpair, g64, g32, expand, tril, vsel, par)
        ek, ev = _memkv_call(l, mem, mnorm, wmem, g64, mem_gain, vsel)

        o_a = _attn_call(_CFG_MLA, aq, ak, av, (), tile(_CFG_MLA), tile(_CFG_MLA))
        o_b = _attn_call(_CFG_FOX, fq, fk, fv, (fdcol, fdrow), tile(_CFG_FOX), tile(_CFG_FOX))
        o_c = _attn_call(_CFG_MOBA, mq, mk, mv, (msel,), tile(_CFG_MOBA), tile(_CFG_MOBA))
        o_d = _attn_call(_CFG_DIFF, dq, dk, dv, (l, g64, gsub, lam_row), tile(_CFG_DIFF), tile(_CFG_DIFF))
        o_e = _attn_call(_CFG_MEM, eq, ek, ev, (), tile(_CFG_MEM), mem.shape[1])

        x = _ffn_call(l, x, (o_a, o_b, o_c, o_d, o_e), wo, fnorm, wg, wu, cw, cb, wd, t_ffn, tf)
    return x
```

```python
import functools
import math

import numpy as np
import jax
import jax.numpy as jnp
from jax import lax
from jax.experimental import pallas as pl
from jax.experimental.pallas import tpu as pltpu

N_HEADS = 4
HEAD_DIM = 64
GROUP_WIDTH = N_HEADS * HEAD_DIM
MLA_Q_RANK = 192
MLA_KV_RANK = 128
MLA_NOPE = 64
MLA_ROPE = 32
MLA_QK = MLA_NOPE + MLA_ROPE
DIFF_QK = HEAD_DIM // 2
ROPE_THETA = 500000.0
ROT_MOBA = HEAD_DIM // 4
ROT_DIFF = DIFF_QK // 4
MOBA_BLOCK = 256
MOBA_TOPK = 3
CONV_WIDTH = 3
EPS = 1e-6
NEG_INF = -1e30
LOG2E = math.log2(math.e)
REMOVED = -3e38

LANES = 128
SUBLANES = 8
MXU_WIDTH = 256
TAIL_ROWS = 16
PREP_TILE = 512
VMEM_LIMIT_BYTES = 56 * 1024 * 1024

_SRC_CQ = 0
_SRC_CKV = _SRC_CQ + MLA_Q_RANK
_SRC_KR = _SRC_CKV + MLA_KV_RANK
_SRC_FOX = _SRC_KR + MLA_ROPE
_SRC_FOXF = _SRC_FOX + 3 * GROUP_WIDTH
_SRC_MOBA = _SRC_FOXF + N_HEADS
_SRC_DIFF = _SRC_MOBA + 3 * GROUP_WIDTH
_SRC_MEMQ = _SRC_DIFF + 3 * GROUP_WIDTH
_SRC_END = _SRC_MEMQ + GROUP_WIDTH

PK_CQ = 0
PK_CKV = 256
PK_KR = 384
PK_FQ, PK_FK, PK_FV = 896, 1152, 1408
PK_FF = 1664
PK_MQ, PK_MK, PK_MV = 1792, 2048, 2304
PK_DQ, PK_DK, PK_DV = 2560, 2816, 3072
PK_EQ = 3328
PK_END = 3584

PAIR_STRIDE = MLA_QK


def _pair_lane(h, d):
    return (h // 2) * MXU_WIDTH + (h % 2) * PAIR_STRIDE + d


N_FREQ = 32
FREQ_BASE_MLA = 0
FREQ_BASE_MOBA = MLA_ROPE // 2
FREQ_BASE_DIFF = FREQ_BASE_MOBA + ROT_MOBA // 2
TR_ONE = FREQ_BASE_DIFF + ROT_DIFF // 2
TR_WIDTH = 4 * N_FREQ
assert TR_ONE < N_FREQ and TR_WIDTH == LANES

(P_CQ, P_CKV, P_GQ, P_GK, P_FQ, P_FK, P_FB, P_MQ, P_MK, P_DQ, P_DK, P_EQ) = range(12)
P_ROWS = 16


def _np_in_index():
    idx = np.full((PK_END,), _SRC_END, np.int32)
    idx[PK_CQ:PK_CQ + MLA_Q_RANK] = np.arange(_SRC_CQ, _SRC_CQ + MLA_Q_RANK)
    idx[PK_CKV:PK_CKV + MLA_KV_RANK] = np.arange(_SRC_CKV, _SRC_CKV + MLA_KV_RANK)
    for h in range(N_HEADS):
        for d in range(MLA_ROPE):
            idx[PK_KR + _pair_lane(h, d)] = _SRC_KR + d
    idx[PK_FQ:PK_FQ + 3 * GROUP_WIDTH] = np.arange(_SRC_FOX, _SRC_FOX + 3 * GROUP_WIDTH)
    idx[PK_FF:PK_FF + N_HEADS] = np.arange(_SRC_FOXF, _SRC_FOXF + N_HEADS)
    idx[PK_MQ:PK_MQ + 3 * GROUP_WIDTH] = np.arange(_SRC_MOBA, _SRC_MOBA + 3 * GROUP_WIDTH)
    idx[PK_DQ:PK_DQ + 3 * GROUP_WIDTH] = np.arange(_SRC_DIFF, _SRC_DIFF + 3 * GROUP_WIDTH)
    idx[PK_EQ:PK_EQ + GROUP_WIDTH] = np.arange(_SRC_MEMQ, _SRC_MEMQ + GROUP_WIDTH)
    return idx


def _np_uq_index():
    idx = np.full((2 * MXU_WIDTH,), N_HEADS * MLA_QK, np.int32)
    for h in range(N_HEADS):
        for d in range(MLA_QK):
            idx[_pair_lane(h, d)] = h * MLA_QK + d
    return idx


def _np_ukv_index():
    zero = N_HEADS * (MLA_NOPE + HEAD_DIM)
    idx_k = np.full((2 * MXU_WIDTH,), zero, np.int32)
    idx_v = np.zeros((GROUP_WIDTH,), np.int32)
    for h in range(N_HEADS):
        for d in range(MLA_NOPE):
            idx_k[_pair_lane(h, MLA_ROPE + d)] = h * (MLA_NOPE + HEAD_DIM) + d
        for d in range(HEAD_DIM):
            idx_v[h * HEAD_DIM + d] = h * (MLA_NOPE + HEAD_DIM) + MLA_NOPE + d
    return idx_k, idx_v


def _np_group_matrix(groups):
    g = np.zeros((MXU_WIDTH, MXU_WIDTH), np.float32)
    for lo, size in groups:
        g[lo:lo + size, lo:lo + size] = 1.0 / size
    return g


_PAIR_GROUPS = [(0, MLA_ROPE), (MLA_ROPE, MLA_NOPE), (PAIR_STRIDE, MLA_ROPE), (PAIR_STRIDE + MLA_ROPE, MLA_NOPE)]
_G64_GROUPS = [(h * HEAD_DIM, HEAD_DIM) for h in range(N_HEADS)]
_G32_GROUPS = [(g * DIFF_QK, DIFF_QK) for g in range(2 * N_HEADS)]


def _np_rope_expand(regions, rot, base):
    half = rot // 2
    e = np.zeros((TR_WIDTH, 2 * MXU_WIDTH), np.float32)
    e[TR_ONE, 0:MXU_WIDTH] = 1.0
    for lo in regions:
        assert lo % rot == 0
        for r in range(half):
            f = base + r
            for lane, sign in ((lo + r, -1.0), (lo + half + r, 1.0)):
                e[TR_ONE, lane] = 0.0
                e[f, lane] = 1.0
                e[N_FREQ + f, lane] = 1.0
                e[2 * N_FREQ + f, MXU_WIDTH + lane] = sign
                e[3 * N_FREQ + f, MXU_WIDTH + lane] = sign
    return e


def _np_rope_expand_all():
    return np.concatenate([
        _np_rope_expand([0, PAIR_STRIDE], MLA_ROPE, FREQ_BASE_MLA),
        _np_rope_expand([h * HEAD_DIM for h in range(N_HEADS)], ROT_MOBA, FREQ_BASE_MOBA),
        _np_rope_expand([g * DIFF_QK for g in range(2 * N_HEADS)], ROT_DIFF, FREQ_BASE_DIFF),
    ], axis=1)


def _dot(a, b):
    return jnp.dot(a, b, preferred_element_type=jnp.float32)


def _dot_nt(a, b):
    return lax.dot_general(a, b, (((1,), (1,)), ((), ())), preferred_element_type=jnp.float32)


def _split2(a):
    hi = a.astype(jnp.bfloat16)
    lo = (a - hi.astype(jnp.float32)).astype(jnp.bfloat16)
    return hi, lo


def _split3(a):
    hi = a.astype(jnp.bfloat16)
    r = a - hi.astype(jnp.float32)
    mid = r.astype(jnp.bfloat16)
    lo = (r - mid.astype(jnp.float32)).astype(jnp.bfloat16)
    return hi, mid, lo


def _group_mean_sq(a, g_bf16):
    return _dot((a * a).astype(jnp.bfloat16), g_bf16)


def _rope(x, tabs, half):
    w = x.shape[-1]
    lane = lax.broadcasted_iota(jnp.int32, x.shape, 1)
    partner = jnp.where((lane & (2 * half - 1)) >= half, pltpu.roll(x, half, 1), pltpu.roll(x, w - half, 1))
    return x * tabs[:, 0:w] + partner * tabs[:, w:2 * w]


def _lane_mask(shape, lo, hi):
    lane = lax.broadcasted_iota(jnp.int32, shape, len(shape) - 1)
    return (lane >= lo) & (lane < hi)


def _prep_body(x_ref, tr_ref, anorm_ref, win_ref, wuq_ref, wukvk_ref, wukvv_ref, gpair_ref, g64_ref, g32_ref,
               exp_ref, tril_ref, vsel_ref, par_ref,
               aq_ref, ak_ref, av_ref, fq_ref, fk_ref, fv_ref, fdcol_ref, fdrow_ref, mq_ref, mk_ref, mv_ref, msel_ref,
               dq_ref, dk_ref, dv_ref, eq_ref,
               kmean_s, carry_s):
    j = pl.program_id(1)
    tm = x_ref.shape[1]
    pr = MOBA_BLOCK
    n_parts = tm // pr
    nbp = kmean_s.shape[0]
    bf = jnp.bfloat16
    gpair, g64, g32 = gpair_ref[...], g64_ref[...], g32_ref[...]
    vsel = vsel_ref[...]
    tril = tril_ref[...]

    @pl.when(j == 0)
    def _():
        kmean_s[...] = jnp.zeros_like(kmean_s)
        carry_s[...] = jnp.zeros_like(carry_s)

    def prow(r, width=MXU_WIDTH):
        return par_ref[r:r + 1, 0:width]

    def project(st):
        rows = st["rows"]
        x = x_ref[0, rows, :]
        xb = (x * lax.rsqrt(jnp.mean(x * x, axis=-1, keepdims=True) + EPS) * anorm_ref[...]).astype(bf)
        tabs = _dot(tr_ref[0, rows, :], exp_ref[...])
        st["tab_mla"] = tabs[:, 0:2 * MXU_WIDTH]
        st["tab_moba"] = tabs[:, 2 * MXU_WIDTH:4 * MXU_WIDTH]
        st["tab_diff"] = tabs[:, 4 * MXU_WIDTH:6 * MXU_WIDTH]

        def run(lo, hi):
            wide = _dot(xb, win_ref[:, lo:hi])
            return lambda off, width: wide[:, off - lo:off - lo + width]

        run_m = run(PK_MQ, PK_DQ)
        run_a = run(PK_CQ, PK_FQ)
        run_f = run(PK_FQ, PK_MQ)
        run_d = run(PK_DQ, PK_END)
        st["direct"] = {"mq": run_m(PK_MQ, MXU_WIDTH), "mk": run_m(PK_MK, MXU_WIDTH),
                        "fq": run_f(PK_FQ, MXU_WIDTH), "fk": run_f(PK_FK, MXU_WIDTH),
                        "dq": run_d(PK_DQ, MXU_WIDTH), "dk": run_d(PK_DK, MXU_WIDTH),
                        "eq": run_d(PK_EQ, MXU_WIDTH)}
        st["values"] = {"fv": run_f(PK_FV, MXU_WIDTH).astype(bf), "mv": run_m(PK_MV, MXU_WIDTH).astype(bf),
                        "dv": run_d(PK_DV, MXU_WIDTH).astype(bf)}
        p_cq, p_ckv, st["p_kr"] = run_a(PK_CQ, MXU_WIDTH), run_a(PK_CKV, MLA_KV_RANK), run_a(PK_KR, 2 * MXU_WIDTH)
        cqn = p_cq * lax.rsqrt(jnp.sum(p_cq * p_cq, axis=-1, keepdims=True) * (1.0 / MLA_Q_RANK) + EPS) * prow(P_CQ)
        ckvn = p_ckv * lax.rsqrt(jnp.mean(p_ckv * p_ckv, axis=-1, keepdims=True) + EPS) * prow(P_CKV, MLA_KV_RANK)
        st["cqb"], st["ckvb"] = cqn.astype(bf), ckvn.astype(bf)
        z = run_f(PK_FF, LANES) + prow(P_FB, LANES)
        log_f = jnp.minimum(z, 0.0) - jnp.log1p(jnp.exp(-jnp.abs(z)))
        st["log_f"] = _split3(jnp.where(_lane_mask(log_f.shape, 0, N_HEADS), log_f, 0.0))

    def second(st):
        gmat = {"fq": g64, "fk": g64, "mq": g64, "mk": g64, "dq": g32, "dk": g32, "eq": g64}
        gains = {"fq": P_FQ, "fk": P_FK, "mq": P_MQ, "mk": P_MK, "dq": P_DQ, "dk": P_DK, "eq": P_EQ}
        ms = {name: _group_mean_sq(a, gmat[name]) for name, a in st["direct"].items()}
        st["qa"] = _dot(st["cqb"], wuq_ref[...])
        st["ka"] = st["p_kr"] + _dot(st["ckvb"], wukvk_ref[...])
        st["values"]["av"] = _dot(st["ckvb"], wukvv_ref[...]).astype(bf)
        l1, l2, l3 = st["log_f"]
        st["cumsum"] = (_dot(tril, l1) + _dot(tril, l2)) + _dot(tril, l3)
        st["normed"] = {name: a * lax.rsqrt(ms[name] + EPS) * prow(gains[name]) for name, a in st["direct"].items()}

    def third(st):
        rows = st["rows"]
        mq = _rope(st["normed"]["mq"], st["tab_moba"], ROT_MOBA // 2)
        mk = _rope(st["normed"]["mk"], st["tab_moba"], ROT_MOBA // 2)
        st["mq"], st["mk"] = mq, mk
        kmean_s[pl.ds(st["blk"], 1), :] = jnp.mean(mk, axis=0, keepdims=True)
        km_hi, km_lo = _split2(kmean_s[...])
        st["gates"] = []
        for h in range(N_HEADS):
            q_hi, q_lo = _split2(jnp.where(_lane_mask(mq.shape, h * HEAD_DIM, (h + 1) * HEAD_DIM), mq, 0.0))
            st["gates"].append((_dot_nt(km_hi, q_hi) + _dot_nt(km_lo, q_hi)) + _dot_nt(km_hi, q_lo))
        st["ms_qa"] = [_group_mean_sq(st["qa"][:, p * MXU_WIDTH:(p + 1) * MXU_WIDTH], gpair) for p in range(2)]
        st["ms_ka"] = [_group_mean_sq(st["ka"][:, p * MXU_WIDTH:(p + 1) * MXU_WIDTH], gpair) for p in range(2)]
        dec = carry_s[...] + st["cumsum"]
        carry_s[...] = dec[pr - 1:pr, :]
        dec2 = dec * LOG2E
        d1, d2, d3 = _split3(dec2)
        for h in range(N_HEADS):
            fdcol_ref[0, rows, h * LANES:(h + 1) * LANES] = jnp.broadcast_to(dec2[:, h:h + 1], (pr, LANES))
        row_sel = jnp.where(lax.broadcasted_iota(jnp.int32, (SUBLANES, LANES), 0)
                            == lax.broadcasted_iota(jnp.int32, (SUBLANES, LANES), 1), 1.0, 0.0).astype(bf)
        fdrow_ref[0, :, rows] = (_dot_nt(row_sel, d1) + _dot_nt(row_sel, d2)) + _dot_nt(row_sel, d3)
        for name, ref in (("fv", fv_ref), ("mv", mv_ref), ("dv", dv_ref), ("av", av_ref)):
            ref[0, :, rows] = _values_t(st["values"][name], vsel)

    def finish(st):
        rows = st["rows"]
        normed = st["normed"]
        fq_ref[0, rows, :] = normed["fq"].astype(bf)
        fk_ref[0, rows, :] = normed["fk"].astype(bf)
        eq_ref[0, rows, :] = normed["eq"].astype(bf)
        dq_ref[0, rows, :] = _rope(normed["dq"], st["tab_diff"], ROT_DIFF // 2).astype(bf)
        dk_ref[0, rows, :] = _rope(normed["dk"], st["tab_diff"], ROT_DIFF // 2).astype(bf)
        mq_ref[0, rows, :] = (st["mq"] * (LOG2E * HEAD_DIM ** -0.5)).astype(bf)
        mk_ref[0, rows, :] = st["mk"].astype(bf)
        for p in range(2):
            sl = slice(p * MXU_WIDTH, (p + 1) * MXU_WIDTH)
            aq_ref[0, rows, sl] = _rope(st["qa"][:, sl] * lax.rsqrt(st["ms_qa"][p] + EPS) * prow(P_GQ),
                                        st["tab_mla"], MLA_ROPE // 2).astype(bf)
            ak_ref[0, rows, sl] = _rope(st["ka"][:, sl] * lax.rsqrt(st["ms_ka"][p] + EPS) * prow(P_GK),
                                        st["tab_mla"], MLA_ROPE // 2).astype(bf)
        blk = lax.broadcasted_iota(jnp.int32, (nbp, pr), 0)
        past = blk < st["blk"]
        for h in range(N_HEADS):
            work = jnp.where(past, st["gates"][h], NEG_INF)
            sel = jnp.zeros((nbp, pr), jnp.bool_)
            for _ in range(MOBA_TOPK):
                mx = jnp.max(work, axis=0, keepdims=True)
                first = jnp.min(jnp.where(work == mx, blk, nbp), axis=0, keepdims=True)
                pick = blk == first
                sel = sel | pick
                work = jnp.where(pick, REMOVED, work)
            msel_ref[0, h * nbp:(h + 1) * nbp, rows] = jnp.where(sel & past, 0.0, NEG_INF)

    parts = [{"rows": slice(p * pr, (p + 1) * pr), "blk": j * n_parts + p} for p in range(n_parts)]
    for phase in (project, second, third, finish):
        for st in parts:
            phase(st)


def _const_spec(a, layer=None):
    if layer is None:
        n = a.ndim
        return pl.BlockSpec(a.shape, lambda *_: (0,) * n)
    n = a.ndim - 1
    return pl.BlockSpec((None,) + a.shape[1:], lambda *_: (layer,) + (0,) * n)


def _moba_blocks_padded(seq):
    return -(-(seq // MOBA_BLOCK) // SUBLANES) * SUBLANES


def _prep_call(layer, tm, x, tr, anorm, win, wuq, wukvk, wukvv, gpair, g64, g32, expand, tril, vsel, par):
    bsz, seq, d = x.shape
    assert seq % tm == 0 and tm % MOBA_BLOCK == 0 and tril.shape == (MOBA_BLOCK, MOBA_BLOCK)
    nbp = _moba_blocks_padded(seq)
    bf = jnp.bfloat16
    f32 = jnp.float32

    vt = -N_HEADS * V_ROWS
    widths = [(2 * MXU_WIDTH, bf), (2 * MXU_WIDTH, bf), (vt, bf),
              (GROUP_WIDTH, bf), (GROUP_WIDTH, bf), (vt, bf),
              (N_HEADS * LANES, f32), (-SUBLANES, f32),
              (GROUP_WIDTH, bf), (GROUP_WIDTH, bf), (vt, bf), (-N_HEADS * nbp, f32),
              (GROUP_WIDTH, bf), (GROUP_WIDTH, bf), (vt, bf),
              (GROUP_WIDTH, bf)]

    def tok(width):
        if width > 0:
            return pl.BlockSpec((1, tm, width), lambda b, j: (b, j, 0))
        return pl.BlockSpec((1, -width, tm), lambda b, j: (b, 0, j))

    def shape(width):
        return (bsz, seq, width) if width > 0 else (bsz, -width, seq)

    consts = [anorm, win, wuq, wukvk, wukvv, gpair, g64, g32, expand, tril, vsel, par]
    layered = [True, True, True, True, True, False, False, False, False, False, False, True]
    return pl.pallas_call(
        _prep_body,
        grid=(bsz, seq // tm),
        in_specs=[tok(d), tok(TR_WIDTH)] + [_const_spec(c, layer if ly else None) for c, ly in zip(consts, layered)],
        out_specs=[tok(w) for w, _ in widths],
        out_shape=[jax.ShapeDtypeStruct(shape(w), dt) for w, dt in widths],
        scratch_shapes=[pltpu.VMEM((nbp, GROUP_WIDTH), jnp.float32), pltpu.VMEM((1, LANES), jnp.float32)],
        compiler_params=pltpu.CompilerParams(dimension_semantics=("arbitrary", "arbitrary"),
                                             vmem_limit_bytes=VMEM_LIMIT_BYTES),
        name="prep",
    )(x, tr, *consts)


def _memkv_body(mem_ref, mnorm_ref, w_ref, g64_ref, gain_ref, vsel_ref, k_ref, v_ref):
    m = mem_ref[0]
    mn = m * lax.rsqrt(jnp.mean(m * m, axis=-1, keepdims=True) + EPS) * mnorm_ref[...]
    kv = _dot(mn.astype(jnp.bfloat16), w_ref[...])
    k = kv[:, 0:GROUP_WIDTH]
    k = k * lax.rsqrt(_group_mean_sq(k, g64_ref[...]) + EPS) * gain_ref[...]
    k_ref[0] = k.astype(jnp.bfloat16)
    v_ref[0] = _values_t(kv[:, GROUP_WIDTH:2 * GROUP_WIDTH].astype(jnp.bfloat16), vsel_ref[...])


def _memkv_call(layer, mem, mnorm, w, g64, gain, vsel):
    bsz, mlen, d = mem.shape
    k_shape, vt_shape = (bsz, mlen, GROUP_WIDTH), (bsz, N_HEADS * V_ROWS, mlen)
    return pl.pallas_call(
        _memkv_body,
        grid=(bsz,),
        in_specs=[pl.BlockSpec((1, mlen, d), lambda b: (b, 0, 0)), _const_spec(mnorm, layer), _const_spec(w, layer),
                  _const_spec(g64), _const_spec(gain, layer), _const_spec(vsel)],
        out_specs=[pl.BlockSpec((1,) + s[1:], lambda b: (b, 0, 0)) for s in (k_shape, vt_shape)],
        out_shape=[jax.ShapeDtypeStruct(s, jnp.bfloat16) for s in (k_shape, vt_shape)],
        compiler_params=pltpu.CompilerParams(dimension_semantics=("arbitrary",), vmem_limit_bytes=VMEM_LIMIT_BYTES),
        name="mem_kv",
    )(mem, mnorm, w, g64, gain, vsel)


Q_SUB = 2 * MXU_WIDTH


class _AttnCfg:
    def __init__(self, name, vheads, n_maps, causal, decay=False, select=False, diff=False, tile=1024, lookahead=8,
                 q_sub=Q_SUB):
        self.name = name
        self.vheads = vheads
        self.n_maps = n_maps
        self.causal = causal
        self.decay = decay
        self.select = select
        self.diff = diff
        self.tile = tile
        self.lookahead = lookahead
        self.q_sub = q_sub


_PLAIN_VHEADS = [(0, h * HEAD_DIM, (h + 1) * HEAD_DIM, 0, h) for h in range(N_HEADS)]
_CFG_MLA = _AttnCfg("attn_mla", [((h // 2) * MXU_WIDTH, (h % 2) * PAIR_STRIDE, (h % 2) * PAIR_STRIDE + MLA_QK, 0, h)
                                 for h in range(N_HEADS)], 1, True, q_sub=MXU_WIDTH, lookahead=16, tile=2048)
_CFG_FOX = _AttnCfg("attn_fox", _PLAIN_VHEADS, 1, True, decay=True, tile=2048, lookahead=8, q_sub=MXU_WIDTH)
_CFG_MOBA = _AttnCfg("attn_moba", _PLAIN_VHEADS, 1, True, select=True, tile=2048, lookahead=8)
_CFG_DIFF = _AttnCfg("attn_diff", [(0, h * HEAD_DIM + c * DIFF_QK, h * HEAD_DIM + (c + 1) * DIFF_QK, c, h)
                                   for c in range(2) for h in range(N_HEADS)], 2, True, diff=True, q_sub=MXU_WIDTH,
                     lookahead=32)
_CFG_MEM = _AttnCfg("attn_mem", _PLAIN_VHEADS, 1, False, tile=2048)


ONES_ROW = HEAD_DIM
V_ROWS = HEAD_DIM + 16


def _np_value_select():
    sel = np.zeros((N_HEADS * V_ROWS, GROUP_WIDTH), np.float32)
    for h in range(N_HEADS):
        for d in range(HEAD_DIM):
            sel[h * V_ROWS + d, h * HEAD_DIM + d] = 1.0
    return sel


def _values_t(v, vsel):
    vt = _dot_nt(vsel, v)
    row = lax.broadcasted_iota(jnp.int32, vt.shape, 0)
    ones = row == ONES_ROW
    for h in range(1, N_HEADS):
        ones = ones | (row == h * V_ROWS + ONES_ROW)
    return jnp.where(ones, 1.0, vt).astype(jnp.bfloat16)


def _tile_lanes(x, width):
    return jnp.tile(x, (1, width // LANES)) if width != LANES else x


def _attn_body(cfg, qi_ref, kj_ref, *refs):
    refs = list(refs)
    q_ref, k_ref, vt_ref = refs[:3]
    pos = 3
    if cfg.decay:
        dq_ref, dk_ref = refs[pos:pos + 2]
        pos += 2
    if cfg.select:
        sel_ref = refs[pos]
        pos += 1
    if cfg.diff:
        g64_ref, gsub_ref, lam_ref = refs[pos:pos + 3]
        pos += 3
    o_ref, qm_s, m_s, acc_s = refs[pos:pos + 4]

    t = pl.program_id(1)
    i = qi_ref[t]
    j = kj_ref[t]
    tq = q_ref.shape[1]
    tk = k_ref.shape[1]

    @pl.when(j == 0)
    def _():
        for n, (off, lo, hi, _, _) in enumerate(cfg.vheads):
            qb = q_ref[0, :, off:off + MXU_WIDTH]
            qm_s[n] = jnp.where(_lane_mask(qb.shape, lo, hi), qb, jnp.zeros_like(qb))
        m_s[...] = jnp.full(m_s.shape, NEG_INF, jnp.float32)
        acc_s[...] = jnp.zeros_like(acc_s)

    def step(diag):
        qs = min(tq, cfg.q_sub)
        items = [(n, u) for n in range(len(cfg.vheads)) for u in range(tq // qs)]

        def n_keys(u):
            return (u + 1) * qs if diag else tk

        def scores(item):
            n, u = item
            off, _, _, _, h = cfg.vheads[n]
            nk = n_keys(u)
            cols = slice(u * qs, (u + 1) * qs)
            s = _dot_nt(k_ref[0, 0:nk, off:off + MXU_WIDTH], qm_s[n, cols, :])
            if cfg.decay:
                s = (dq_ref[0, h:h + 1, cols] - _tile_lanes(dk_ref[0, 0:nk, h * LANES:(h + 1) * LANES], qs)) + s
            if cfg.select:
                nbp = sel_ref.shape[1] // N_HEADS
                qpos = u * qs + lax.broadcasted_iota(jnp.int32, (1, qs), 1)
                parts = []
                for kb in range(nk // MOBA_BLOCK):
                    rows = s[kb * MOBA_BLOCK:(kb + 1) * MOBA_BLOCK, :]
                    if not (diag and kb == nk // MOBA_BLOCK - 1):
                        bias = sel_ref[0, pl.ds(h * nbp + j * (tk // MOBA_BLOCK) + kb, 1), cols]
                        if diag:
                            bias = jnp.where(qpos < (kb + 1) * MOBA_BLOCK, 0.0, bias)
                        rows = rows + bias
                    parts.append(rows)
                s = parts[0] if len(parts) == 1 else jnp.concatenate(parts, axis=0)
            if diag:
                key = lax.broadcasted_iota(jnp.int32, (nk, qs), 0)
                qry = u * qs + lax.broadcasted_iota(jnp.int32, (nk, qs), 1)
                s = jnp.where(key <= qry, s, NEG_INF)
            return s, jnp.max(s, axis=0, keepdims=True)

        raw = {it: scores(items[it]) for it in range(min(cfg.lookahead, len(items)))}
        for it, (n, u) in enumerate(items):
            h = cfg.vheads[n][4]
            nk = n_keys(u)
            cols = slice(u * qs, (u + 1) * qs)
            s, s_max = raw.pop(it)
            m_prev = m_s[n, :, cols]
            m_new = jnp.maximum(m_prev, s_max)
            alpha = jnp.exp2(m_prev - m_new)
            p = jnp.exp2(s - m_new)
            m_s[n, :, cols] = m_new
            acc_s[n, :, cols] = acc_s[n, :, cols] * alpha + _dot(vt_ref[0, h * V_ROWS:(h + 1) * V_ROWS, 0:nk],
                                                                 p.astype(jnp.bfloat16))
            if it + cfg.lookahead < len(items):
                raw[it + cfg.lookahead] = scores(items[it + cfg.lookahead])

    if cfg.causal:
        pl.when(j < i)(functools.partial(step, False))
        pl.when(j == i)(functools.partial(step, True))
        last = j == i
    else:
        step(False)
        last = j == 0

    @pl.when(last)
    def _():
        outs = []
        for c in range(cfg.n_maps):
            heads = []
            for h in range(N_HEADS):
                acc = acc_s[c * N_HEADS + h]
                heads.append(acc[0:HEAD_DIM, :] / acc[ONES_ROW:ONES_ROW + 1, :])
            outs.append(jnp.concatenate(heads, axis=0).T)
        if cfg.diff:
            o = outs[0] - lam_ref[0:1, :] * outs[1]
            o = o * lax.rsqrt(_group_mean_sq(o, g64_ref[...]) + EPS) * gsub_ref[...]
        else:
            o = outs[0]
        o_ref[0] = o.astype(o_ref.dtype)


def _attn_call(cfg, q, k, v, extras, tq, tk):
    bsz, seq, wq = q.shape
    sk = k.shape[1]
    nq = seq // tq
    assert seq % tq == 0 and sk % tk == 0
    if cfg.causal:
        assert tq == tk and sk == seq
        pairs = [(i, j) for i in range(nq) for j in range(i + 1)]
    else:
        assert sk == tk
        pairs = [(i, 0) for i in range(nq)]
    qi = jnp.asarray(np.array([p[0] for p in pairs], np.int32))
    kj = jnp.asarray(np.array([p[1] for p in pairs], np.int32))
    n_vh = len(cfg.vheads)

    in_specs = [pl.BlockSpec((1, tq, wq), lambda b, t, qi, kj: (b, qi[t], 0)),
                pl.BlockSpec((1, tk, wq), lambda b, t, qi, kj: (b, kj[t], 0)),
                pl.BlockSpec((1, N_HEADS * V_ROWS, tk), lambda b, t, qi, kj: (b, 0, kj[t]))]
    args = [q, k, v]
    if cfg.decay:
        dcol, drow = extras
        in_specs += [pl.BlockSpec((1, SUBLANES, tq), lambda b, t, qi, kj: (b, 0, qi[t])),
                     pl.BlockSpec((1, tk, N_HEADS * LANES), lambda b, t, qi, kj: (b, kj[t], 0))]
        args += [drow, dcol]
    if cfg.select:
        (sel,) = extras
        in_specs += [pl.BlockSpec((1, sel.shape[1], tq), lambda b, t, qi, kj: (b, 0, qi[t]))]
        args += [sel]
    if cfg.diff:
        layer, g64, gsub, lam_row = extras
        in_specs += [_const_spec(g64), _const_spec(gsub, layer), _const_spec(lam_row, layer)]
        args += [g64, gsub, lam_row]

    grid_spec = pltpu.PrefetchScalarGridSpec(
        num_scalar_prefetch=2,
        grid=(bsz, len(pairs)),
        in_specs=in_specs,
        out_specs=pl.BlockSpec((1, tq, GROUP_WIDTH), lambda b, t, qi, kj: (b, qi[t], 0)),
        scratch_shapes=[pltpu.VMEM((n_vh, tq, MXU_WIDTH), jnp.bfloat16),
                        pltpu.VMEM((n_vh, 1, tq), jnp.float32),
                        pltpu.VMEM((n_vh, V_ROWS, tq), jnp.float32)])
    return pl.pallas_call(
        functools.partial(_attn_body, cfg),
        grid_spec=grid_spec,
        out_shape=jax.ShapeDtypeStruct((bsz, seq, GROUP_WIDTH), jnp.bfloat16),
        compiler_params=pltpu.CompilerParams(dimension_semantics=("arbitrary", "arbitrary"),
                                             vmem_limit_bytes=VMEM_LIMIT_BYTES),
        name=cfg.name,
    )(qi, kj, *args)


def _ffn_body(nf, x_ref, oa_ref, ob_ref, oc_ref, od_ref, oe_ref, wo_ref, fnorm_ref, wg_ref, wu_ref, cw_ref, cb_ref,
              wd_ref, out_ref, xnew_s, xn_s, acc_s):
    i = pl.program_id(1)
    f = pl.program_id(2)
    tm = x_ref.shape[1]

    @pl.when(f == 0)
    def _():
        @pl.when(i == 0)
        def _():
            xn_s[0:TAIL_ROWS, :] = jnp.zeros((TAIL_ROWS, xn_s.shape[1]), xn_s.dtype)

        @pl.when(i > 0)
        def _():
            xn_s[0:TAIL_ROWS, :] = xn_s[tm:tm + TAIL_ROWS, :]

        mixed = jnp.concatenate([o_ref[0] for o_ref in (oa_ref, ob_ref, oc_ref, od_ref, oe_ref)], axis=1)
        xnew = x_ref[0] + _dot(mixed, wo_ref[...])
        xnew_s[...] = xnew
        xn = xnew * lax.rsqrt(jnp.mean(xnew * xnew, axis=-1, keepdims=True) + EPS) * fnorm_ref[...]
        xn_s[TAIL_ROWS:TAIL_ROWS + tm, :] = xn.astype(xn_s.dtype)

    def mlp_chunk():
        ge = _dot(xn_s[...], wg_ref[...])
        u = _dot(xn_s[TAIL_ROWS:TAIL_ROWS + tm, :], wu_ref[...])
        g0 = ge[TAIL_ROWS:TAIL_ROWS + tm, :]
        t1 = ge[TAIL_ROWS - 1:TAIL_ROWS, :]
        t2 = ge[TAIL_ROWS - 2:TAIL_ROWS - 1, :]
        row = lax.broadcasted_iota(jnp.int32, g0.shape, 0)
        g1 = jnp.where(row == 0, t1, pltpu.roll(g0, 1, 0))
        g2 = jnp.where(row == 0, t2, jnp.where(row == 1, t1, pltpu.roll(g0, 2, 0)))
        y = cb_ref[...] + cw_ref[0:1, :] * g2
        y = y + cw_ref[1:2, :] * g1
        y = y + cw_ref[2:3, :] * g0
        hmid = (y * (1.0 / (1.0 + jnp.exp(-y)))) * u
        return _dot(hmid.astype(jnp.bfloat16), wd_ref[...])

    @pl.when(f == 0)
    def _():
        acc_s[...] = mlp_chunk()

    if nf > 2:
        @pl.when((f > 0) & (f < nf - 1))
        def _():
            acc_s[...] += mlp_chunk()

    @pl.when(f == nf - 1)
    def _():
        out_ref[0] = xnew_s[...] + (acc_s[...] + mlp_chunk())


def _ffn_call(layer, x, outs, wo, fnorm, wg, wu, cw, cb, wd, tm, tf):
    bsz, seq, d = x.shape
    dff = wg.shape[2]
    n_tiles, nf = seq // tm, dff // tf
    assert seq % tm == 0 and dff % tf == 0 and nf >= 2
    tok = lambda w: pl.BlockSpec((1, tm, w), lambda b, i, f: (b, i, 0))
    return pl.pallas_call(
        functools.partial(_ffn_body, nf),
        grid=(bsz, n_tiles, nf),
        in_specs=[tok(d)] + [tok(GROUP_WIDTH)] * 5 + [
            _const_spec(wo, layer),
            _const_spec(fnorm, layer),
            pl.BlockSpec((None, d, tf), lambda b, i, f: (layer, 0, f)),
            pl.BlockSpec((None, d, tf), lambda b, i, f: (layer, 0, f)),
            pl.BlockSpec((None, SUBLANES, tf), lambda b, i, f: (layer, 0, f)),
            pl.BlockSpec((None, 1, tf), lambda b, i, f: (layer, 0, f)),
            pl.BlockSpec((None, tf, d), lambda b, i, f: (layer, f, 0))],
        out_specs=tok(d),
        out_shape=jax.ShapeDtypeStruct((bsz, seq, d), jnp.float32),
        scratch_shapes=[pltpu.VMEM((tm, d), jnp.float32), pltpu.VMEM((TAIL_ROWS + tm, d), jnp.bfloat16),
                        pltpu.VMEM((tm, d), jnp.float32)],
        compiler_params=pltpu.CompilerParams(dimension_semantics=("arbitrary", "arbitrary", "arbitrary"),
                                             vmem_limit_bytes=VMEM_LIMIT_BYTES),
        name="ffn",
    )(x, *outs, wo, fnorm, wg, wu, cw, cb, wd)


def _pad_rows(v, width=MXU_WIDTH):
    return jnp.pad(v.astype(jnp.float32), ((0, 0), (0, width - v.shape[1])))


def _tile_rows(g, reps):
    return jnp.tile(g.astype(jnp.float32), (1, reps))


def _pack_in_projection(w):
    idx = _np_in_index()
    pieces, start = [], 0
    while start < PK_END:
        stop = start + 1
        if idx[start] == _SRC_END:
            while stop < PK_END and idx[stop] == _SRC_END:
                stop += 1
            pieces.append(jnp.zeros(w.shape[:-1] + (stop - start,), w.dtype))
        else:
            while stop < PK_END and idx[stop] == idx[stop - 1] + 1:
                stop += 1
            pieces.append(w[..., int(idx[start]):int(idx[stop - 1]) + 1])
        start = stop
    return jnp.concatenate(pieces, axis=-1)


def _zero_col(w):
    return jnp.concatenate([w, jnp.zeros(w.shape[:-1] + (1,), w.dtype)], axis=-1)


def _rope_table(positions):
    pos = positions.astype(jnp.float32)[:, :, None]
    inv = [ROPE_THETA ** (-jnp.arange(0, rot, 2, dtype=jnp.float32) / rot) for rot in (MLA_ROPE, ROT_MOBA, ROT_DIFF)]
    inv = jnp.concatenate(inv + [jnp.zeros((N_FREQ - TR_ONE,), jnp.float32)])
    ang = pos * inv
    c, s = jnp.cos(ang), jnp.sin(ang)
    c_hi = c.astype(jnp.bfloat16)
    c_lo = (c - c_hi.astype(jnp.float32)).astype(jnp.bfloat16)
    s_hi = s.astype(jnp.bfloat16)
    s_lo = (s - s_hi.astype(jnp.float32)).astype(jnp.bfloat16)
    return jnp.concatenate([c_hi, c_lo, s_hi, s_lo], axis=-1)


def _pick_tile(n, pref):
    t = pref
    while n % t:
        t //= 2
    return t


def kernel(x, mem, positions, attn_norm, ffn_norm, mem_norm, w_in, mla_cq_norm, mla_ckv_norm, mla_w_uq, mla_w_ukv, mla_q_norm, mla_k_norm, fox_b_f, fox_q_norm, fox_k_norm, moba_q_norm, moba_k_norm, diff_lambda, diff_q_norm, diff_k_norm, diff_sub_norm, mem_w_kv, mem_q_norm, mem_k_norm, w_o, ffn_w_gate, ffn_w_up, ffn_conv_w, ffn_conv_b, ffn_w_down):
    bsz, seq, d = x.shape
    depth = w_in.shape[0]
    dff = ffn_w_gate.shape[2]
    bf = jnp.bfloat16
    f32 = jnp.float32

    uq_idx = _np_uq_index()
    ukvk_idx, ukvv_idx = _np_ukv_index()
    gpair = jnp.asarray(_np_group_matrix(_PAIR_GROUPS), bf)
    g64 = jnp.asarray(_np_group_matrix(_G64_GROUPS), bf)
    g32 = jnp.asarray(_np_group_matrix(_G32_GROUPS), bf)
    expand = jnp.asarray(_np_rope_expand_all(), bf)
    t_prep = max(_pick_tile(seq, PREP_TILE), MOBA_BLOCK)
    tril = jnp.asarray(np.tril(np.ones((MOBA_BLOCK, MOBA_BLOCK), np.float32)), bf)
    vsel = jnp.asarray(_np_value_select(), bf)
    tr = _rope_table(positions)

    tile = lambda cfg: max(_pick_tile(seq, cfg.tile), MOBA_BLOCK) if cfg.select else _pick_tile(seq, cfg.tile)
    t_ffn = _pick_tile(seq, 512)
    tf = dff // 2 if (dff // 2) % LANES == 0 else dff

    win = _pack_in_projection(w_in.astype(bf))
    wuq = jnp.take(_zero_col(mla_w_uq), uq_idx, axis=2)
    wuq = jnp.pad(wuq, ((0, 0), (0, MXU_WIDTH - MLA_Q_RANK), (0, 0))).astype(bf)
    wukv = _zero_col(mla_w_ukv)
    wukvk = jnp.take(wukv, ukvk_idx, axis=2).astype(bf)
    wukvv = jnp.take(wukv, ukvv_idx, axis=2).astype(bf)
    pair = lambda g: _pad_rows(_tile_rows(g, 2))
    rows = [jnp.zeros((depth, MXU_WIDTH), f32)] * P_ROWS
    rows[P_CQ] = _pad_rows(mla_cq_norm)
    rows[P_CKV] = _pad_rows(mla_ckv_norm)
    rows[P_GQ] = pair(mla_q_norm) * (LOG2E * MLA_QK ** -0.5)
    rows[P_GK] = pair(mla_k_norm)
    rows[P_FQ] = _tile_rows(fox_q_norm, N_HEADS) * (LOG2E * HEAD_DIM ** -0.5)
    rows[P_FK] = _tile_rows(fox_k_norm, N_HEADS)
    rows[P_FB] = _pad_rows(fox_b_f)
    rows[P_MQ] = _tile_rows(moba_q_norm, N_HEADS)
    rows[P_MK] = _tile_rows(moba_k_norm, N_HEADS)
    rows[P_DQ] = _tile_rows(diff_q_norm, 2 * N_HEADS) * (LOG2E * DIFF_QK ** -0.5)
    rows[P_DK] = _tile_rows(diff_k_norm, 2 * N_HEADS)
    rows[P_EQ] = _tile_rows(mem_q_norm, N_HEADS) * (LOG2E * HEAD_DIM ** -0.5)
    par = jnp.stack(rows, axis=1)
    anorm = attn_norm.astype(f32)[:, None, :]
    mnorm = mem_norm.astype(f32)[:, None, :]
    fnorm = ffn_norm.astype(f32)[:, None, :]
    wmem = mem_w_kv.astype(bf)
    mem_gain = _tile_rows(mem_k_norm, N_HEADS)[:, None, :]

    lam_init = jnp.asarray([0.8 - 0.6 * math.exp(-0.3 * l) for l in range(depth)], f32)
    lam_vec = diff_lambda.astype(f32)
    lam = (jnp.exp(jnp.sum(lam_vec[:, 0] * lam_vec[:, 1], axis=-1))
           - jnp.exp(jnp.sum(lam_vec[:, 2] * lam_vec[:, 3], axis=-1)) + lam_init)
    lam_row = jnp.broadcast_to(lam[:, None, None], (depth, 1, GROUP_WIDTH))
    gsub = (_tile_rows(diff_sub_norm, N_HEADS) * (1.0 - lam_init)[:, None])[:, None, :]

    wo = w_o.astype(bf)
    wg, wu, wd = ffn_w_gate.astype(bf), ffn_w_up.astype(bf), ffn_w_down.astype(bf)
    cw = jnp.pad(ffn_conv_w.astype(f32), ((0, 0), (0, SUBLANES - CONV_WIDTH), (0, 0)))
    cb = ffn_conv_b.astype(f32)[:, None, :]

    for l in range(depth):
        (aq, ak, av, fq, fk, fv, fdcol, fdrow, mq, mk, mv, msel, dq, dk, dv, eq) = _prep_call(
            l, t_prep, x, tr, anorm, win, wuq, wukvk, wukvv, gpair, g64, g32, expand, tril, vsel, par)
        ek, ev = _memkv_call(l, mem, mnorm, wmem, g64, mem_gain, vsel)

        o_a = _attn_call(_CFG_MLA, aq, ak, av, (), tile(_CFG_MLA), tile(_CFG_MLA))
        o_b = _attn_call(_CFG_FOX, fq, fk, fv, (fdcol, fdrow), tile(_CFG_FOX), tile(_CFG_FOX))
        o_c = _attn_call(_CFG_MOBA, mq, mk, mv, (msel,), tile(_CFG_MOBA), tile(_CFG_MOBA))
        o_d = _attn_call(_CFG_DIFF, dq, dk, dv, (l, g64, gsub, lam_row), tile(_CFG_DIFF), tile(_CFG_DIFF))
        o_e = _attn_call(_CFG_MEM, eq, ek, ev, (), tile(_CFG_MEM), mem.shape[1])

        x = _ffn_call(l, x, (o_a, o_b, o_c, o_d, o_e), wo, fnorm, wg, wu, cw, cb, wd, t_ffn, tf)
    return x
```

```python
import functools
import math

import numpy as np
import jax
import jax.numpy as jnp
from jax import lax
from jax.experimental import pallas as pl
from jax.experimental.pallas import tpu as pltpu

N_HEADS = 4
HEAD_DIM = 64
GROUP_WIDTH = N_HEADS * HEAD_DIM
MLA_Q_RANK = 192
MLA_KV_RANK = 128
MLA_NOPE = 64
MLA_ROPE = 32
MLA_QK = MLA_NOPE + MLA_ROPE
DIFF_QK = HEAD_DIM // 2
ROPE_THETA = 500000.0
ROT_MOBA = HEAD_DIM // 4
ROT_DIFF = DIFF_QK // 4
MOBA_BLOCK = 256
MOBA_TOPK = 3
CONV_WIDTH = 3
EPS = 1e-6
NEG_INF = -1e30
LOG2E = math.log2(math.e)
REMOVED = -3e38

LANES = 128
SUBLANES = 8
MXU_WIDTH = 256
TAIL_ROWS = 16
PREP_TILE = 512
VMEM_LIMIT_BYTES = 56 * 1024 * 1024

_SRC_CQ = 0
_SRC_CKV = _SRC_CQ + MLA_Q_RANK
_SRC_KR = _SRC_CKV + MLA_KV_RANK
_SRC_FOX = _SRC_KR + MLA_ROPE
_SRC_FOXF = _SRC_FOX + 3 * GROUP_WIDTH
_SRC_MOBA = _SRC_FOXF + N_HEADS
_SRC_DIFF = _SRC_MOBA + 3 * GROUP_WIDTH
_SRC_MEMQ = _SRC_DIFF + 3 * GROUP_WIDTH
_SRC_END = _SRC_MEMQ + GROUP_WIDTH

PK_CQ = 0
PK_CKV = 256
PK_KR = 384
PK_FQ, PK_FK, PK_FV = 896, 1152, 1408
PK_FF = 1664
PK_MQ, PK_MK, PK_MV = 1792, 2048, 2304
PK_DQ, PK_DK, PK_DV = 2560, 2816, 3072
PK_EQ = 3328
PK_END = 3584

PAIR_STRIDE = MLA_QK


def _pair_lane(h, d):
    return (h // 2) * MXU_WIDTH + (h % 2) * PAIR_STRIDE + d


N_FREQ = 32
FREQ_BASE_MLA = 0
FREQ_BASE_MOBA = MLA_ROPE // 2
FREQ_BASE_DIFF = FREQ_BASE_MOBA + ROT_MOBA // 2
TR_ONE = FREQ_BASE_DIFF + ROT_DIFF // 2
TR_WIDTH = 4 * N_FREQ
assert TR_ONE < N_FREQ and TR_WIDTH == LANES

(P_CQ, P_CKV, P_GQ, P_GK, P_FQ, P_FK, P_FB, P_MQ, P_MK, P_DQ, P_DK, P_EQ) = range(12)
P_ROWS = 16


def _np_in_index():
    idx = np.full((PK_END,), _SRC_END, np.int32)
    idx[PK_CQ:PK_CQ + MLA_Q_RANK] = np.arange(_SRC_CQ, _SRC_CQ + MLA_Q_RANK)
    idx[PK_CKV:PK_CKV + MLA_KV_RANK] = np.arange(_SRC_CKV, _SRC_CKV + MLA_KV_RANK)
    for h in range(N_HEADS):
        for d in range(MLA_ROPE):
            idx[PK_KR + _pair_lane(h, d)] = _SRC_KR + d
    idx[PK_FQ:PK_FQ + 3 * GROUP_WIDTH] = np.arange(_SRC_FOX, _SRC_FOX + 3 * GROUP_WIDTH)
    idx[PK_FF:PK_FF + N_HEADS] = np.arange(_SRC_FOXF, _SRC_FOXF + N_HEADS)
    idx[PK_MQ:PK_MQ + 3 * GROUP_WIDTH] = np.arange(_SRC_MOBA, _SRC_MOBA + 3 * GROUP_WIDTH)
    idx[PK_DQ:PK_DQ + 3 * GROUP_WIDTH] = np.arange(_SRC_DIFF, _SRC_DIFF + 3 * GROUP_WIDTH)
    idx[PK_EQ:PK_EQ + GROUP_WIDTH] = np.arange(_SRC_MEMQ, _SRC_MEMQ + GROUP_WIDTH)
    return idx


def _np_uq_index():
    idx = np.full((2 * MXU_WIDTH,), N_HEADS * MLA_QK, np.int32)
    for h in range(N_HEADS):
        for d in range(MLA_QK):
            idx[_pair_lane(h, d)] = h * MLA_QK + d
    return idx


def _np_ukv_index():
    zero = N_HEADS * (MLA_NOPE + HEAD_DIM)
    idx_k = np.full((2 * MXU_WIDTH,), zero, np.int32)
    idx_v = np.zeros((GROUP_WIDTH,), np.int32)
    for h in range(N_HEADS):
        for d in range(MLA_NOPE):
            idx_k[_pair_lane(h, MLA_ROPE + d)] = h * (MLA_NOPE + HEAD_DIM) + d
        for d in range(HEAD_DIM):
            idx_v[h * HEAD_DIM + d] = h * (MLA_NOPE + HEAD_DIM) + MLA_NOPE + d
    return idx_k, idx_v


def _np_group_matrix(groups):
    g = np.zeros((MXU_WIDTH, MXU_WIDTH), np.float32)
    for lo, size in groups:
        g[lo:lo + size, lo:lo + size] = 1.0 / size
    return g


_PAIR_GROUPS = [(0, MLA_ROPE), (MLA_ROPE, MLA_NOPE), (PAIR_STRIDE, MLA_ROPE), (PAIR_STRIDE + MLA_ROPE, MLA_NOPE)]
_G64_GROUPS = [(h * HEAD_DIM, HEAD_DIM) for h in range(N_HEADS)]
_G32_GROUPS = [(g * DIFF_QK, DIFF_QK) for g in range(2 * N_HEADS)]


def _np_rope_expand(regions, rot, base):
    half = rot // 2
    e = np.zeros((TR_WIDTH, 2 * MXU_WIDTH), np.float32)
    e[TR_ONE, 0:MXU_WIDTH] = 1.0
    for lo in regions:
        assert lo % rot == 0
        for r in range(half):
            f = base + r
            for lane, sign in ((lo + r, -1.0), (lo + half + r, 1.0)):
                e[TR_ONE, lane] = 0.0
                e[f, lane] = 1.0
                e[N_FREQ + f, lane] = 1.0
                e[2 * N_FREQ + f, MXU_WIDTH + lane] = sign
                e[3 * N_FREQ + f, MXU_WIDTH + lane] = sign
    return e


def _np_rope_expand_all():
    return np.concatenate([
        _np_rope_expand([0, PAIR_STRIDE], MLA_ROPE, FREQ_BASE_MLA),
        _np_rope_expand([h * HEAD_DIM for h in range(N_HEADS)], ROT_MOBA, FREQ_BASE_MOBA),
        _np_rope_expand([g * DIFF_QK for g in range(2 * N_HEADS)], ROT_DIFF, FREQ_BASE_DIFF),
    ], axis=1)


def _dot(a, b):
    return jnp.dot(a, b, preferred_element_type=jnp.float32)


def _dot_nt(a, b):
    return lax.dot_general(a, b, (((1,), (1,)), ((), ())), preferred_element_type=jnp.float32)


def _split2(a):
    hi = a.astype(jnp.bfloat16)
    lo = (a - hi.astype(jnp.float32)).astype(jnp.bfloat16)
    return hi, lo


def _split3(a):
    hi = a.astype(jnp.bfloat16)
    r = a - hi.astype(jnp.float32)
    mid = r.astype(jnp.bfloat16)
    lo = (r - mid.astype(jnp.float32)).astype(jnp.bfloat16)
    return hi, mid, lo


def _group_mean_sq(a, g_bf16):
    return _dot((a * a).astype(jnp.bfloat16), g_bf16)


def _rope(x, tabs, half):
    w = x.shape[-1]
    lane = lax.broadcasted_iota(jnp.int32, x.shape, 1)
    partner = jnp.where((lane & (2 * half - 1)) >= half, pltpu.roll(x, half, 1), pltpu.roll(x, w - half, 1))
    return x * tabs[:, 0:w] + partner * tabs[:, w:2 * w]


def _lane_mask(shape, lo, hi):
    lane = lax.broadcasted_iota(jnp.int32, shape, len(shape) - 1)
    return (lane >= lo) & (lane < hi)


def _prep_body(x_ref, tr_ref, anorm_ref, win_ref, wuq_ref, wukvk_ref, wukvv_ref, gpair_ref, g64_ref, g32_ref,
               exp_ref, tril_ref, vsel_ref, par_ref,
               aq_ref, ak_ref, av_ref, fq_ref, fk_ref, fv_ref, fdcol_ref, fdrow_ref, mq_ref, mk_ref, mv_ref, msel_ref,
               dq_ref, dk_ref, dv_ref, eq_ref,
               kmean_s, carry_s):
    j = pl.program_id(1)
    tm = x_ref.shape[1]
    pr = MOBA_BLOCK
    n_parts = tm // pr
    nbp = kmean_s.shape[0]
    bf = jnp.bfloat16
    gpair, g64, g32 = gpair_ref[...], g64_ref[...], g32_ref[...]
    vsel = vsel_ref[...]
    tril = tril_ref[...]

    @pl.when(j == 0)
    def _():
        kmean_s[...] = jnp.zeros_like(kmean_s)
        carry_s[...] = jnp.zeros_like(carry_s)

    def prow(r, width=MXU_WIDTH):
        return par_ref[r:r + 1, 0:width]

    def project(st):
        rows = st["rows"]
        x = x_ref[0, rows, :]
        xb = (x * lax.rsqrt(jnp.mean(x * x, axis=-1, keepdims=True) + EPS) * anorm_ref[...]).astype(bf)
        tabs = _dot(tr_ref[0, rows, :], exp_ref[...])
        st["tab_mla"] = tabs[:, 0:2 * MXU_WIDTH]
        st["tab_moba"] = tabs[:, 2 * MXU_WIDTH:4 * MXU_WIDTH]
        st["tab_diff"] = tabs[:, 4 * MXU_WIDTH:6 * MXU_WIDTH]

        def run(lo, hi):
            wide = _dot(xb, win_ref[:, lo:hi])
            return lambda off, width: wide[:, off - lo:off - lo + width]

        run_m = run(PK_MQ, PK_DQ)
        run_a = run(PK_CQ, PK_FQ)
        run_f = run(PK_FQ, PK_MQ)
        run_d = run(PK_DQ, PK_END)
        st["direct"] = {"mq": run_m(PK_MQ, MXU_WIDTH), "mk": run_m(PK_MK, MXU_WIDTH),
                        "fq": run_f(PK_FQ, MXU_WIDTH), "fk": run_f(PK_FK, MXU_WIDTH),
                        "dq": run_d(PK_DQ, MXU_WIDTH), "dk": run_d(PK_DK, MXU_WIDTH),
                        "eq": run_d(PK_EQ, MXU_WIDTH)}
        st["values"] = {"fv": run_f(PK_FV, MXU_WIDTH).astype(bf), "mv": run_m(PK_MV, MXU_WIDTH).astype(bf),
                        "dv": run_d(PK_DV, MXU_WIDTH).astype(bf)}
        p_cq, p_ckv, st["p_kr"] = run_a(PK_CQ, MXU_WIDTH), run_a(PK_CKV, MLA_KV_RANK), run_a(PK_KR, 2 * MXU_WIDTH)
        cqn = p_cq * lax.rsqrt(jnp.sum(p_cq * p_cq, axis=-1, keepdims=True) * (1.0 / MLA_Q_RANK) + EPS) * prow(P_CQ)
        ckvn = p_ckv * lax.rsqrt(jnp.mean(p_ckv * p_ckv, axis=-1, keepdims=True) + EPS) * prow(P_CKV, MLA_KV_RANK)
        st["cqb"], st["ckvb"] = cqn.astype(bf), ckvn.astype(bf)
        z = run_f(PK_FF, LANES) + prow(P_FB, LANES)
        log_f = jnp.minimum(z, 0.0) - jnp.log1p(jnp.exp(-jnp.abs(z)))
        st["log_f"] = _split3(jnp.where(_lane_mask(log_f.shape, 0, N_HEADS), log_f, 0.0))

    def second(st):
        gmat = {"fq": g64, "fk": g64, "mq": g64, "mk": g64, "dq": g32, "dk": g32, "eq": g64}
        gains = {"fq": P_FQ, "fk": P_FK, "mq": P_MQ, "mk": P_MK, "dq": P_DQ, "dk": P_DK, "eq": P_EQ}
        ms = {name: _group_mean_sq(a, gmat[name]) for name, a in st["direct"].items()}
        st["qa"] = _dot(st["cqb"], wuq_ref[...])
        st["ka"] = st["p_kr"] + _dot(st["ckvb"], wukvk_ref[...])
        st["values"]["av"] = _dot(st["ckvb"], wukvv_ref[...]).astype(bf)
        l1, l2, l3 = st["log_f"]
        st["cumsum"] = (_dot(tril, l1) + _dot(tril, l2)) + _dot(tril, l3)
        st["normed"] = {name: a * lax.rsqrt(ms[name] + EPS) * prow(gains[name]) for name, a in st["direct"].items()}

    def third(st):
        rows = st["rows"]
        mq = _rope(st["normed"]["mq"], st["tab_moba"], ROT_MOBA // 2)
        mk = _rope(st["normed"]["mk"], st["tab_moba"], ROT_MOBA // 2)
        st["mq"], st["mk"] = mq, mk
        kmean_s[pl.ds(st["blk"], 1), :] = jnp.mean(mk, axis=0, keepdims=True)
        km_hi, km_lo = _split2(kmean_s[...])
        st["gates"] = []
        for h in range(N_HEADS):
            q_hi, q_lo = _split2(jnp.where(_lane_mask(mq.shape, h * HEAD_DIM, (h + 1) * HEAD_DIM), mq, 0.0))
            st["gates"].append((_dot_nt(km_hi, q_hi) + _dot_nt(km_lo, q_hi)) + _dot_nt(km_hi, q_lo))
        st["ms_qa"] = [_group_mean_sq(st["qa"][:, p * MXU_WIDTH:(p + 1) * MXU_WIDTH], gpair) for p in range(2)]
        st["ms_ka"] = [_group_mean_sq(st["ka"][:, p * MXU_WIDTH:(p + 1) * MXU_WIDTH], gpair) for p in range(2)]
        dec = carry_s[...] + st["cumsum"]
        carry_s[...] = dec[pr - 1:pr, :]
        dec2 = dec * LOG2E
        d1, d2, d3 = _split3(dec2)
        for h in range(N_HEADS):
            fdcol_ref[0, rows, h * LANES:(h + 1) * LANES] = jnp.broadcast_to(dec2[:, h:h + 1], (pr, LANES))
        row_sel = jnp.where(lax.broadcasted_iota(jnp.int32, (SUBLANES, LANES), 0)
                            == lax.broadcasted_iota(jnp.int32, (SUBLANES, LANES), 1), 1.0, 0.0).astype(bf)
        fdrow_ref[0, :, rows] = (_dot_nt(row_sel, d1) + _dot_nt(row_sel, d2)) + _dot_nt(row_sel, d3)
        for name, ref in (("fv", fv_ref), ("mv", mv_ref), ("dv", dv_ref), ("av", av_ref)):
            ref[0, :, rows] = _values_t(st["values"][name], vsel)

    def finish(st):
        rows = st["rows"]
        normed = st["normed"]
        fq_ref[0, rows, :] = normed["fq"].astype(bf)
        fk_ref[0, rows, :] = normed["fk"].astype(bf)
        eq_ref[0, rows, :] = normed["eq"].astype(bf)
        dq_ref[0, rows, :] = _rope(normed["dq"], st["tab_diff"], ROT_DIFF // 2).astype(bf)
        dk_ref[0, rows, :] = _rope(normed["dk"], st["tab_diff"], ROT_DIFF // 2).astype(bf)
        mq_ref[0, rows, :] = (st["mq"] * (LOG2E * HEAD_DIM ** -0.5)).astype(bf)
        mk_ref[0, rows, :] = st["mk"].astype(bf)
        for p in range(2):
            sl = slice(p * MXU_WIDTH, (p + 1) * MXU_WIDTH)
            aq_ref[0, rows, sl] = _rope(st["qa"][:, sl] * lax.rsqrt(st["ms_qa"][p] + EPS) * prow(P_GQ),
                                        st["tab_mla"], MLA_ROPE // 2).astype(bf)
            ak_ref[0, rows, sl] = _rope(st["ka"][:, sl] * lax.rsqrt(st["ms_ka"][p] + EPS) * prow(P_GK),
                                        st["tab_mla"], MLA_ROPE // 2).astype(bf)
        blk = lax.broadcasted_iota(jnp.int32, (nbp, pr), 0)
        past = blk < st["blk"]
        for h in range(N_HEADS):
            work = jnp.where(past, st["gates"][h], NEG_INF)
            sel = jnp.zeros((nbp, pr), jnp.bool_)
            for _ in range(MOBA_TOPK):
                mx = jnp.max(work, axis=0, keepdims=True)
                first = jnp.min(jnp.where(work == mx, blk, nbp), axis=0, keepdims=True)
                pick = blk == first
                sel = sel | pick
                work = jnp.where(pick, REMOVED, work)
            msel_ref[0, h * nbp:(h + 1) * nbp, rows] = jnp.where(sel & past, 0.0, NEG_INF)

    parts = [{"rows": slice(p * pr, (p + 1) * pr), "blk": j * n_parts + p} for p in range(n_parts)]
    for phase in (project, second, third, finish):
        for st in parts:
            phase(st)


def _const_spec(a, layer=None):
    if layer is None:
        n = a.ndim
        return pl.BlockSpec(a.shape, lambda *_: (0,) * n)
    n = a.ndim - 1
    return pl.BlockSpec((None,) + a.shape[1:], lambda *_: (layer,) + (0,) * n)


def _moba_blocks_padded(seq):
    return -(-(seq // MOBA_BLOCK) // SUBLANES) * SUBLANES


def _prep_call(layer, tm, x, tr, anorm, win, wuq, wukvk, wukvv, gpair, g64, g32, expand, tril, vsel, par):
    bsz, seq, d = x.shape
    assert seq % tm == 0 and tm % MOBA_BLOCK == 0 and tril.shape == (MOBA_BLOCK, MOBA_BLOCK)
    nbp = _moba_blocks_padded(seq)
    bf = jnp.bfloat16
    f32 = jnp.float32

    vt = -N_HEADS * V_ROWS
    widths = [(2 * MXU_WIDTH, bf), (2 * MXU_WIDTH, bf), (vt, bf),
              (GROUP_WIDTH, bf), (GROUP_WIDTH, bf), (vt, bf),
              (N_HEADS * LANES, f32), (-SUBLANES, f32),
              (GROUP_WIDTH, bf), (GROUP_WIDTH, bf), (vt, bf), (-N_HEADS * nbp, f32),
              (GROUP_WIDTH, bf), (GROUP_WIDTH, bf), (vt, bf),
              (GROUP_WIDTH, bf)]

    def tok(width):
        if width > 0:
            return pl.BlockSpec((1, tm, width), lambda b, j: (b, j, 0))
        return pl.BlockSpec((1, -width, tm), lambda b, j: (b, 0, j))

    def shape(width):
        return (bsz, seq, width) if width > 0 else (bsz, -width, seq)

    consts = [anorm, win, wuq, wukvk, wukvv, gpair, g64, g32, expand, tril, vsel, par]
    layered = [True, True, True, True, True, False, False, False, False, False, False, True]
    return pl.pallas_call(
        _prep_body,
        grid=(bsz, seq // tm),
        in_specs=[tok(d), tok(TR_WIDTH)] + [_const_spec(c, layer if ly else None) for c, ly in zip(consts, layered)],
        out_specs=[tok(w) for w, _ in widths],
        out_shape=[jax.ShapeDtypeStruct(shape(w), dt) for w, dt in widths],
        scratch_shapes=[pltpu.VMEM((nbp, GROUP_WIDTH), jnp.float32), pltpu.VMEM((1, LANES), jnp.float32)],
        compiler_params=pltpu.CompilerParams(dimension_semantics=("arbitrary", "arbitrary"),
                                             vmem_limit_bytes=VMEM_LIMIT_BYTES),
        name="prep",
    )(x, tr, *consts)


def _memkv_body(mem_ref, mnorm_ref, w_ref, g64_ref, gain_ref, vsel_ref, k_ref, v_ref):
    m = mem_ref[0]
    mn = m * lax.rsqrt(jnp.mean(m * m, axis=-1, keepdims=True) + EPS) * mnorm_ref[...]
    kv = _dot(mn.astype(jnp.bfloat16), w_ref[...])
    k = kv[:, 0:GROUP_WIDTH]
    k = k * lax.rsqrt(_group_mean_sq(k, g64_ref[...]) + EPS) * gain_ref[...]
    k_ref[0] = k.astype(jnp.bfloat16)
    v_ref[0] = _values_t(kv[:, GROUP_WIDTH:2 * GROUP_WIDTH].astype(jnp.bfloat16), vsel_ref[...])


def _memkv_call(layer, mem, mnorm, w, g64, gain, vsel):
    bsz, mlen, d = mem.shape
    k_shape, vt_shape = (bsz, mlen, GROUP_WIDTH), (bsz, N_HEADS * V_ROWS, mlen)
    return pl.pallas_call(
        _memkv_body,
        grid=(bsz,),
        in_specs=[pl.BlockSpec((1, mlen, d), lambda b: (b, 0, 0)), _const_spec(mnorm, layer), _const_spec(w, layer),
                  _const_spec(g64), _const_spec(gain, layer), _const_spec(vsel)],
        out_specs=[pl.BlockSpec((1,) + s[1:], lambda b: (b, 0, 0)) for s in (k_shape, vt_shape)],
        out_shape=[jax.ShapeDtypeStruct(s, jnp.bfloat16) for s in (k_shape, vt_shape)],
        compiler_params=pltpu.CompilerParams(dimension_semantics=("arbitrary",), vmem_limit_bytes=VMEM_LIMIT_BYTES),
        name="mem_kv",
    )(mem, mnorm, w, g64, gain, vsel)


Q_SUB = 2 * MXU_WIDTH


class _AttnCfg:
    def __init__(self, name, vheads, n_maps, causal, decay=False, select=False, diff=False, tile=1024, lookahead=8,
                 q_sub=Q_SUB):
        self.name = name
        self.vheads = vheads
        self.n_maps = n_maps
        self.causal = causal
        self.decay = decay
        self.select = select
        self.diff = diff
        self.tile = tile
        self.lookahead = lookahead
        self.q_sub = q_sub


_PLAIN_VHEADS = [(0, h * HEAD_DIM, (h + 1) * HEAD_DIM, 0, h) for h in range(N_HEADS)]
_CFG_MLA = _AttnCfg("attn_mla", [((h // 2) * MXU_WIDTH, (h % 2) * PAIR_STRIDE, (h % 2) * PAIR_STRIDE + MLA_QK, 0, h)
                                 for h in range(N_HEADS)], 1, True, q_sub=MXU_WIDTH, lookahead=16, tile=2048)
_CFG_FOX = _AttnCfg("attn_fox", _PLAIN_VHEADS, 1, True, decay=True, tile=2048, lookahead=6)
_CFG_MOBA = _AttnCfg("attn_moba", _PLAIN_VHEADS, 1, True, select=True, tile=2048, lookahead=16, q_sub=MXU_WIDTH)
_CFG_DIFF = _AttnCfg("attn_diff", [(0, h * HEAD_DIM + c * DIFF_QK, h * HEAD_DIM + (c + 1) * DIFF_QK, c, h)
                                   for c in range(2) for h in range(N_HEADS)], 2, True, diff=True, q_sub=MXU_WIDTH,
                     lookahead=32)
_CFG_MEM = _AttnCfg("attn_mem", _PLAIN_VHEADS, 1, False, tile=2048)


ONES_ROW = HEAD_DIM
V_ROWS = HEAD_DIM + 16


def _np_value_select():
    sel = np.zeros((N_HEADS * V_ROWS, GROUP_WIDTH), np.float32)
    for h in range(N_HEADS):
        for d in range(HEAD_DIM):
            sel[h * V_ROWS + d, h * HEAD_DIM + d] = 1.0
    return sel


def _values_t(v, vsel):
    vt = _dot_nt(vsel, v)
    row = lax.broadcasted_iota(jnp.int32, vt.shape, 0)
    ones = row == ONES_ROW
    for h in range(1, N_HEADS):
        ones = ones | (row == h * V_ROWS + ONES_ROW)
    return jnp.where(ones, 1.0, vt).astype(jnp.bfloat16)


def _tile_lanes(x, width):
    return jnp.tile(x, (1, width // LANES)) if width != LANES else x


def _attn_body(cfg, qi_ref, kj_ref, *refs):
    refs = list(refs)
    q_ref, k_ref, vt_ref = refs[:3]
    pos = 3
    if cfg.decay:
        dq_ref, dk_ref = refs[pos:pos + 2]
        pos += 2
    if cfg.select:
        sel_ref = refs[pos]
        pos += 1
    if cfg.diff:
        g64_ref, gsub_ref, lam_ref = refs[pos:pos + 3]
        pos += 3
    o_ref, qm_s, m_s, acc_s = refs[pos:pos + 4]

    t = pl.program_id(1)
    i = qi_ref[t]
    j = kj_ref[t]
    tq = q_ref.shape[1]
    tk = k_ref.shape[1]

    @pl.when(j == 0)
    def _():
        for n, (off, lo, hi, _, _) in enumerate(cfg.vheads):
            qb = q_ref[0, :, off:off + MXU_WIDTH]
            qm_s[n] = jnp.where(_lane_mask(qb.shape, lo, hi), qb, jnp.zeros_like(qb))
        m_s[...] = jnp.full(m_s.shape, NEG_INF, jnp.float32)
        acc_s[...] = jnp.zeros_like(acc_s)

    def step(diag):
        qs = min(tq, cfg.q_sub)
        items = [(n, u) for n in range(len(cfg.vheads)) for u in range(tq // qs)]

        def n_keys(u):
            return (u + 1) * qs if diag else tk

        def scores(item):
            n, u = item
            off, _, _, _, h = cfg.vheads[n]
            nk = n_keys(u)
            cols = slice(u * qs, (u + 1) * qs)
            s = _dot_nt(k_ref[0, 0:nk, off:off + MXU_WIDTH], qm_s[n, cols, :])
            if cfg.decay:
                s = (dq_ref[0, h:h + 1, cols] - _tile_lanes(dk_ref[0, 0:nk, h * LANES:(h + 1) * LANES], qs)) + s
            if cfg.select:
                nbp = sel_ref.shape[1] // N_HEADS
                qpos = u * qs + lax.broadcasted_iota(jnp.int32, (1, qs), 1)
                parts = []
                for kb in range(nk // MOBA_BLOCK):
                    rows = s[kb * MOBA_BLOCK:(kb + 1) * MOBA_BLOCK, :]
                    if not (diag and kb == nk // MOBA_BLOCK - 1):
                        bias = sel_ref[0, pl.ds(h * nbp + j * (tk // MOBA_BLOCK) + kb, 1), cols]
                        if diag:
                            bias = jnp.where(qpos < (kb + 1) * MOBA_BLOCK, 0.0, bias)
                        rows = rows + bias
                    parts.append(rows)
                s = parts[0] if len(parts) == 1 else jnp.concatenate(parts, axis=0)
            if diag:
                key = lax.broadcasted_iota(jnp.int32, (nk, qs), 0)
                qry = u * qs + lax.broadcasted_iota(jnp.int32, (nk, qs), 1)
                s = jnp.where(key <= qry, s, NEG_INF)
            return s, jnp.max(s, axis=0, keepdims=True)

        raw = {it: scores(items[it]) for it in range(min(cfg.lookahead, len(items)))}
        for it, (n, u) in enumerate(items):
            h = cfg.vheads[n][4]
            nk = n_keys(u)
            cols = slice(u * qs, (u + 1) * qs)
            s, s_max = raw.pop(it)
            m_prev = m_s[n, :, cols]
            m_new = jnp.maximum(m_prev, s_max)
            alpha = jnp.exp2(m_prev - m_new)
            p = jnp.exp2(s - m_new)
            m_s[n, :, cols] = m_new
            acc_s[n, :, cols] = acc_s[n, :, cols] * alpha + _dot(vt_ref[0, h * V_ROWS:(h + 1) * V_ROWS, 0:nk],
                                                                 p.astype(jnp.bfloat16))
            if it + cfg.lookahead < len(items):
                raw[it + cfg.lookahead] = scores(items[it + cfg.lookahead])

    if cfg.causal:
        pl.when(j < i)(functools.partial(step, False))
        pl.when(j == i)(functools.partial(step, True))
        last = j == i
    else:
        step(False)
        last = j == 0

    @pl.when(last)
    def _():
        outs = []
        for c in range(cfg.n_maps):
            heads = []
            for h in range(N_HEADS):
                acc = acc_s[c * N_HEADS + h]
                heads.append(acc[0:HEAD_DIM, :] / acc[ONES_ROW:ONES_ROW + 1, :])
            outs.append(jnp.concatenate(heads, axis=0).T)
        if cfg.diff:
            o = outs[0] - lam_ref[0:1, :] * outs[1]
            o = o * lax.rsqrt(_group_mean_sq(o, g64_ref[...]) + EPS) * gsub_ref[...]
        else:
            o = outs[0]
        o_ref[0] = o.astype(o_ref.dtype)


def _attn_call(cfg, q, k, v, extras, tq, tk):
    bsz, seq, wq = q.shape
    sk = k.shape[1]
    nq = seq // tq
    assert seq % tq == 0 and sk % tk == 0
    if cfg.causal:
        assert tq == tk and sk == seq
        pairs = [(i, j) for i in range(nq) for j in range(i + 1)]
    else:
        assert sk == tk
        pairs = [(i, 0) for i in range(nq)]
    qi = jnp.asarray(np.array([p[0] for p in pairs], np.int32))
    kj = jnp.asarray(np.array([p[1] for p in pairs], np.int32))
    n_vh = len(cfg.vheads)

    in_specs = [pl.BlockSpec((1, tq, wq), lambda b, t, qi, kj: (b, qi[t], 0)),
                pl.BlockSpec((1, tk, wq), lambda b, t, qi, kj: (b, kj[t], 0)),
                pl.BlockSpec((1, N_HEADS * V_ROWS, tk), lambda b, t, qi, kj: (b, 0, kj[t]))]
    args = [q, k, v]
    if cfg.decay:
        dcol, drow = extras
        in_specs += [pl.BlockSpec((1, SUBLANES, tq), lambda b, t, qi, kj: (b, 0, qi[t])),
                     pl.BlockSpec((1, tk, N_HEADS * LANES), lambda b, t, qi, kj: (b, kj[t], 0))]
        args += [drow, dcol]
    if cfg.select:
        (sel,) = extras
        in_specs += [pl.BlockSpec((1, sel.shape[1], tq), lambda b, t, qi, kj: (b, 0, qi[t]))]
        args += [sel]
    if cfg.diff:
        layer, g64, gsub, lam_row = extras
        in_specs += [_const_spec(g64), _const_spec(gsub, layer), _const_spec(lam_row, layer)]
        args += [g64, gsub, lam_row]

    grid_spec = pltpu.PrefetchScalarGridSpec(
        num_scalar_prefetch=2,
        grid=(bsz, len(pairs)),
        in_specs=in_specs,
        out_specs=pl.BlockSpec((1, tq, GROUP_WIDTH), lambda b, t, qi, kj: (b, qi[t], 0)),
        scratch_shapes=[pltpu.VMEM((n_vh, tq, MXU_WIDTH), jnp.bfloat16),
                        pltpu.VMEM((n_vh, 1, tq), jnp.float32),
                        pltpu.VMEM((n_vh, V_ROWS, tq), jnp.float32)])
    return pl.pallas_call(
        functools.partial(_attn_body, cfg),
        grid_spec=grid_spec,
        out_shape=jax.ShapeDtypeStruct((bsz, seq, GROUP_WIDTH), jnp.bfloat16),
        compiler_params=pltpu.CompilerParams(dimension_semantics=("arbitrary", "arbitrary"),
                                             vmem_limit_bytes=VMEM_LIMIT_BYTES),
        name=cfg.name,
    )(qi, kj, *args)


def _ffn_body(nf, x_ref, oa_ref, ob_ref, oc_ref, od_ref, oe_ref, wo_ref, fnorm_ref, wg_ref, wu_ref, cw_ref, cb_ref,
              wd_ref, out_ref, xnew_s, xn_s, acc_s):
    i = pl.program_id(1)
    f = pl.program_id(2)
    tm = x_ref.shape[1]

    @pl.when(f == 0)
    def _():
        @pl.when(i == 0)
        def _():
            xn_s[0:TAIL_ROWS, :] = jnp.zeros((TAIL_ROWS, xn_s.shape[1]), xn_s.dtype)

        @pl.when(i > 0)
        def _():
            xn_s[0:TAIL_ROWS, :] = xn_s[tm:tm + TAIL_ROWS, :]

        mixed = jnp.concatenate([o_ref[0] for o_ref in (oa_ref, ob_ref, oc_ref, od_ref, oe_ref)], axis=1)
        xnew = x_ref[0] + _dot(mixed, wo_ref[...])
        xnew_s[...] = xnew
        xn = xnew * lax.rsqrt(jnp.mean(xnew * xnew, axis=-1, keepdims=True) + EPS) * fnorm_ref[...]
        xn_s[TAIL_ROWS:TAIL_ROWS + tm, :] = xn.astype(xn_s.dtype)

    def mlp_chunk():
        ge = _dot(xn_s[...], wg_ref[...])
        u = _dot(xn_s[TAIL_ROWS:TAIL_ROWS + tm, :], wu_ref[...])
        g0 = ge[TAIL_ROWS:TAIL_ROWS + tm, :]
        t1 = ge[TAIL_ROWS - 1:TAIL_ROWS, :]
        t2 = ge[TAIL_ROWS - 2:TAIL_ROWS - 1, :]
        row = lax.broadcasted_iota(jnp.int32, g0.shape, 0)
        g1 = jnp.where(row == 0, t1, pltpu.roll(g0, 1, 0))
        g2 = jnp.where(row == 0, t2, jnp.where(row == 1, t1, pltpu.roll(g0, 2, 0)))
        y = cb_ref[...] + cw_ref[0:1, :] * g2
        y = y + cw_ref[1:2, :] * g1
        y = y + cw_ref[2:3, :] * g0
        hmid = (y * (1.0 / (1.0 + jnp.exp(-y)))) * u
        return _dot(hmid.astype(jnp.bfloat16), wd_ref[...])

    @pl.when(f == 0)
    def _():
        acc_s[...] = mlp_chunk()

    if nf > 2:
        @pl.when((f > 0) & (f < nf - 1))
        def _():
            acc_s[...] += mlp_chunk()

    @pl.when(f == nf - 1)
    def _():
        out_ref[0] = xnew_s[...] + (acc_s[...] + mlp_chunk())


def _ffn_call(layer, x, outs, wo, fnorm, wg, wu, cw, cb, wd, tm, tf):
    bsz, seq, d = x.shape
    dff = wg.shape[2]
    n_tiles, nf = seq // tm, dff // tf
    assert seq % tm == 0 and dff % tf == 0 and nf >= 2
    tok = lambda w: pl.BlockSpec((1, tm, w), lambda b, i, f: (b, i, 0))
    return pl.pallas_call(
        functools.partial(_ffn_body, nf),
        grid=(bsz, n_tiles, nf),
        in_specs=[tok(d)] + [tok(GROUP_WIDTH)] * 5 + [
            _const_spec(wo, layer),
            _const_spec(fnorm, layer),
            pl.BlockSpec((None, d, tf), lambda b, i, f: (layer, 0, f)),
            pl.BlockSpec((None, d, tf), lambda b, i, f: (layer, 0, f)),
            pl.BlockSpec((None, SUBLANES, tf), lambda b, i, f: (layer, 0, f)),
            pl.BlockSpec((None, 1, tf), lambda b, i, f: (layer, 0, f)),
            pl.BlockSpec((None, tf, d), lambda b, i, f: (layer, f, 0))],
        out_specs=tok(d),
        out_shape=jax.ShapeDtypeStruct((bsz, seq, d), jnp.float32),
        scratch_shapes=[pltpu.VMEM((tm, d), jnp.float32), pltpu.VMEM((TAIL_ROWS + tm, d), jnp.bfloat16),
                        pltpu.VMEM((tm, d), jnp.float32)],
        compiler_params=pltpu.CompilerParams(dimension_semantics=("arbitrary", "arbitrary", "arbitrary"),
                                             vmem_limit_bytes=VMEM_LIMIT_BYTES),
        name="ffn",
    )(x, *outs, wo, fnorm, wg, wu, cw, cb, wd)


def _pad_rows(v, width=MXU_WIDTH):
    return jnp.pad(v.astype(jnp.float32), ((0, 0), (0, width - v.shape[1])))


def _tile_rows(g, reps):
    return jnp.tile(g.astype(jnp.float32), (1, reps))


def _pack_in_projection(w):
    idx = _np_in_index()
    pieces, start = [], 0
    while start < PK_END:
        stop = start + 1
        if idx[start] == _SRC_END:
            while stop < PK_END and idx[stop] == _SRC_END:
                stop += 1
            pieces.append(jnp.zeros(w.shape[:-1] + (stop - start,), w.dtype))
        else:
            while stop < PK_END and idx[stop] == idx[stop - 1] + 1:
                stop += 1
            pieces.append(w[..., int(idx[start]):int(idx[stop - 1]) + 1])
        start = stop
    return jnp.concatenate(pieces, axis=-1)


def _zero_col(w):
    return jnp.concatenate([w, jnp.zeros(w.shape[:-1] + (1,), w.dtype)], axis=-1)


def _rope_table(positions):
    pos = positions.astype(jnp.float32)[:, :, None]
    inv = [ROPE_THETA ** (-jnp.arange(0, rot, 2, dtype=jnp.float32) / rot) for rot in (MLA_ROPE, ROT_MOBA, ROT_DIFF)]
    inv = jnp.concatenate(inv + [jnp.zeros((N_FREQ - TR_ONE,), jnp.float32)])
    ang = pos * inv
    c, s = jnp.cos(ang), jnp.sin(ang)
    c_hi = c.astype(jnp.bfloat16)
    c_lo = (c - c_hi.astype(jnp.float32)).astype(jnp.bfloat16)
    s_hi = s.astype(jnp.bfloat16)
    s_lo = (s - s_hi.astype(jnp.float32)).astype(jnp.bfloat16)
    return jnp.concatenate([c_hi, c_lo, s_hi, s_lo], axis=-1)


def _pick_tile(n, pref):
    t = pref
    while n % t:
        t //= 2
    return t


def kernel(x, mem, positions, attn_norm, ffn_norm, mem_norm, w_in, mla_cq_norm, mla_ckv_norm, mla_w_uq, mla_w_ukv, mla_q_norm, mla_k_norm, fox_b_f, fox_q_norm, fox_k_norm, moba_q_norm, moba_k_norm, diff_lambda, diff_q_norm, diff_k_norm, diff_sub_norm, mem_w_kv, mem_q_norm, mem_k_norm, w_o, ffn_w_gate, ffn_w_up, ffn_conv_w, ffn_conv_b, ffn_w_down):
    bsz, seq, d = x.shape
    depth = w_in.shape[0]
    dff = ffn_w_gate.shape[2]
    bf = jnp.bfloat16
    f32 = jnp.float32

    uq_idx = _np_uq_index()
    ukvk_idx, ukvv_idx = _np_ukv_index()
    gpair = jnp.asarray(_np_group_matrix(_PAIR_GROUPS), bf)
    g64 = jnp.asarray(_np_group_matrix(_G64_GROUPS), bf)
    g32 = jnp.asarray(_np_group_matrix(_G32_GROUPS), bf)
    expand = jnp.asarray(_np_rope_expand_all(), bf)
    t_prep = max(_pick_tile(seq, PREP_TILE), MOBA_BLOCK)
    tril = jnp.asarray(np.tril(np.ones((MOBA_BLOCK, MOBA_BLOCK), np.float32)), bf)
    vsel = jnp.asarray(_np_value_select(), bf)
    tr = _rope_table(positions)

    tile = lambda cfg: max(_pick_tile(seq, cfg.tile), MOBA_BLOCK) if cfg.select else _pick_tile(seq, cfg.tile)
    t_ffn = _pick_tile(seq, 512)
    tf = dff // 2 if (dff // 2) % LANES == 0 else dff

    win = _pack_in_projection(w_in.astype(bf))
    wuq = jnp.take(_zero_col(mla_w_uq), uq_idx, axis=2)
    wuq = jnp.pad(wuq, ((0, 0), (0, MXU_WIDTH - MLA_Q_RANK), (0, 0))).astype(bf)
    wukv = _zero_col(mla_w_ukv)
    wukvk = jnp.take(wukv, ukvk_idx, axis=2).astype(bf)
    wukvv = jnp.take(wukv, ukvv_idx, axis=2).astype(bf)
    pair = lambda g: _pad_rows(_tile_rows(g, 2))
    rows = [jnp.zeros((depth, MXU_WIDTH), f32)] * P_ROWS
    rows[P_CQ] = _pad_rows(mla_cq_norm)
    rows[P_CKV] = _pad_rows(mla_ckv_norm)
    rows[P_GQ] = pair(mla_q_norm) * (LOG2E * MLA_QK ** -0.5)
    rows[P_GK] = pair(mla_k_norm)
    rows[P_FQ] = _tile_rows(fox_q_norm, N_HEADS) * (LOG2E * HEAD_DIM ** -0.5)
    rows[P_FK] = _tile_rows(fox_k_norm, N_HEADS)
    rows[P_FB] = _pad_rows(fox_b_f)
    rows[P_MQ] = _tile_rows(moba_q_norm, N_HEADS)
    rows[P_MK] = _tile_rows(moba_k_norm, N_HEADS)
    rows[P_DQ] = _tile_rows(diff_q_norm, 2 * N_HEADS) * (LOG2E * DIFF_QK ** -0.5)
    rows[P_DK] = _tile_rows(diff_k_norm, 2 * N_HEADS)
    rows[P_EQ] = _tile_rows(mem_q_norm, N_HEADS) * (LOG2E * HEAD_DIM ** -0.5)
    par = jnp.stack(rows, axis=1)
    anorm = attn_norm.astype(f32)[:, None, :]
    mnorm = mem_norm.astype(f32)[:, None, :]
    fnorm = ffn_norm.astype(f32)[:, None, :]
    wmem = mem_w_kv.astype(bf)
    mem_gain = _tile_rows(mem_k_norm, N_HEADS)[:, None, :]

    lam_init = jnp.asarray([0.8 - 0.6 * math.exp(-0.3 * l) for l in range(depth)], f32)
    lam_vec = diff_lambda.astype(f32)
    lam = (jnp.exp(jnp.sum(lam_vec[:, 0] * lam_vec[:, 1], axis=-1))
           - jnp.exp(jnp.sum(lam_vec[:, 2] * lam_vec[:, 3], axis=-1)) + lam_init)
    lam_row = jnp.broadcast_to(lam[:, None, None], (depth, 1, GROUP_WIDTH))
    gsub = (_tile_rows(diff_sub_norm, N_HEADS) * (1.0 - lam_init)[:, None])[:, None, :]

    wo = w_o.astype(bf)
    wg, wu, wd = ffn_w_gate.astype(bf), ffn_w_up.astype(bf), ffn_w_down.astype(bf)
    cw = jnp.pad(ffn_conv_w.astype(f32), ((0, 0), (0, SUBLANES - CONV_WIDTH), (0, 0)))
    cb = ffn_conv_b.astype(f32)[:, None, :]

    for l in range(depth):
        (aq, ak, av, fq, fk, fv, fdcol, fdrow, mq, mk, mv, msel, dq, dk, dv, eq) = _prep_call(
            l, t_prep, x, tr, anorm, win, wuq, wukvk, wukvv, gpair, g64, g32, expand, tril, vsel, par)
        ek, ev = _memkv_call(l, mem, mnorm, wmem, g64, mem_gain, vsel)

        o_a = _attn_call(_CFG_MLA, aq, ak, av, (), tile(_CFG_MLA), tile(_CFG_MLA))
        o_b = _attn_call(_CFG_FOX, fq, fk, fv, (fdcol, fdrow), tile(_CFG_FOX), tile(_CFG_FOX))
        o_c = _attn_call(_CFG_MOBA, mq, mk, mv, (msel,), tile(_CFG_MOBA), tile(_CFG_MOBA))
        o_d = _attn_call(_CFG_DIFF, dq, dk, dv, (l, g64, gsub, lam_row), tile(_CFG_DIFF), tile(_CFG_DIFF))
        o_e = _attn_call(_CFG_MEM, eq, ek, ev, (), tile(_CFG_MEM), mem.shape[1])

        x = _ffn_call(l, x, (o_a, o_b, o_c, o_d, o_e), wo, fnorm, wg, wu, cw, cb, wd, t_ffn, tf)
    return x
```

```python
import functools
import math

import numpy as np
import jax
import jax.numpy as jnp
from jax import lax
from jax.experimental import pallas as pl
from jax.experimental.pallas import tpu as pltpu

N_HEADS = 4
HEAD_DIM = 64
GROUP_WIDTH = N_HEADS * HEAD_DIM
MLA_Q_RANK = 192
MLA_KV_RANK = 128
MLA_NOPE = 64
MLA_ROPE = 32
MLA_QK = MLA_NOPE + MLA_ROPE
DIFF_QK = HEAD_DIM // 2
ROPE_THETA = 500000.0
ROT_MOBA = HEAD_DIM // 4
ROT_DIFF = DIFF_QK // 4
MOBA_BLOCK = 256
MOBA_TOPK = 3
CONV_WIDTH = 3
EPS = 1e-6
NEG_INF = -1e30
LOG2E = math.log2(math.e)
REMOVED = -3e38

LANES = 128
SUBLANES = 8
MXU_WIDTH = 256
TAIL_ROWS = 16
PREP_TILE = 512
VMEM_LIMIT_BYTES = 56 * 1024 * 1024

_SRC_CQ = 0
_SRC_CKV = _SRC_CQ + MLA_Q_RANK
_SRC_KR = _SRC_CKV + MLA_KV_RANK
_SRC_FOX = _SRC_KR + MLA_ROPE
_SRC_FOXF = _SRC_FOX + 3 * GROUP_WIDTH
_SRC_MOBA = _SRC_FOXF + N_HEADS
_SRC_DIFF = _SRC_MOBA + 3 * GROUP_WIDTH
_SRC_MEMQ = _SRC_DIFF + 3 * GROUP_WIDTH
_SRC_END = _SRC_MEMQ + GROUP_WIDTH

PK_CQ = 0
PK_CKV = 256
PK_KR = 384
PK_FQ, PK_FK, PK_FV = 896, 1152, 1408
PK_FF = 1664
PK_MQ, PK_MK, PK_MV = 1792, 2048, 2304
PK_DQ, PK_DK, PK_DV = 2560, 2816, 3072
PK_EQ = 3328
PK_END = 3584

PAIR_STRIDE = MLA_QK


def _pair_lane(h, d):
    return (h // 2) * MXU_WIDTH + (h % 2) * PAIR_STRIDE + d


N_FREQ = 32
FREQ_BASE_MLA = 0
FREQ_BASE_MOBA = MLA_ROPE // 2
FREQ_BASE_DIFF = FREQ_BASE_MOBA + ROT_MOBA // 2
TR_ONE = FREQ_BASE_DIFF + ROT_DIFF // 2
TR_WIDTH = 4 * N_FREQ
assert TR_ONE < N_FREQ and TR_WIDTH == LANES

(P_CQ, P_CKV, P_GQ, P_GK, P_FQ, P_FK, P_FB, P_MQ, P_MK, P_DQ, P_DK, P_EQ) = range(12)
P_ROWS = 16


def _np_in_index():
    idx = np.full((PK_END,), _SRC_END, np.int32)
    idx[PK_CQ:PK_CQ + MLA_Q_RANK] = np.arange(_SRC_CQ, _SRC_CQ + MLA_Q_RANK)
    idx[PK_CKV:PK_CKV + MLA_KV_RANK] = np.arange(_SRC_CKV, _SRC_CKV + MLA_KV_RANK)
    for h in range(N_HEADS):
        for d in range(MLA_ROPE):
            idx[PK_KR + _pair_lane(h, d)] = _SRC_KR + d
    idx[PK_FQ:PK_FQ + 3 * GROUP_WIDTH] = np.arange(_SRC_FOX, _SRC_FOX + 3 * GROUP_WIDTH)
    idx[PK_FF:PK_FF + N_HEADS] = np.arange(_SRC_FOXF, _SRC_FOXF + N_HEADS)
    idx[PK_MQ:PK_MQ + 3 * GROUP_WIDTH] = np.arange(_SRC_MOBA, _SRC_MOBA + 3 * GROUP_WIDTH)
    idx[PK_DQ:PK_DQ + 3 * GROUP_WIDTH] = np.arange(_SRC_DIFF, _SRC_DIFF + 3 * GROUP_WIDTH)
    idx[PK_EQ:PK_EQ + GROUP_WIDTH] = np.arange(_SRC_MEMQ, _SRC_MEMQ + GROUP_WIDTH)
    return idx


def _np_uq_index():
    idx = np.full((2 * MXU_WIDTH,), N_HEADS * MLA_QK, np.int32)
    for h in range(N_HEADS):
        for d in range(MLA_QK):
            idx[_pair_lane(h, d)] = h * MLA_QK + d
    return idx


def _np_ukv_index():
    zero = N_HEADS * (MLA_NOPE + HEAD_DIM)
    idx_k = np.full((2 * MXU_WIDTH,), zero, np.int32)
    idx_v = np.zeros((GROUP_WIDTH,), np.int32)
    for h in range(N_HEADS):
        for d in range(MLA_NOPE):
            idx_k[_pair_lane(h, MLA_ROPE + d)] = h * (MLA_NOPE + HEAD_DIM) + d
        for d in range(HEAD_DIM):
            idx_v[h * HEAD_DIM + d] = h * (MLA_NOPE + HEAD_DIM) + MLA_NOPE + d
    return idx_k, idx_v


def _np_group_matrix(groups):
    g = np.zeros((MXU_WIDTH, MXU_WIDTH), np.float32)
    for lo, size in groups:
        g[lo:lo + size, lo:lo + size] = 1.0 / size
    return g


_PAIR_GROUPS = [(0, MLA_ROPE), (MLA_ROPE, MLA_NOPE), (PAIR_STRIDE, MLA_ROPE), (PAIR_STRIDE + MLA_ROPE, MLA_NOPE)]
_G64_GROUPS = [(h * HEAD_DIM, HEAD_DIM) for h in range(N_HEADS)]
_G32_GROUPS = [(g * DIFF_QK, DIFF_QK) for g in range(2 * N_HEADS)]


def _np_rope_expand(regions, rot, base):
    half = rot // 2
    e = np.zeros((TR_WIDTH, 2 * MXU_WIDTH), np.float32)
    e[TR_ONE, 0:MXU_WIDTH] = 1.0
    for lo in regions:
        assert lo % rot == 0
        for r in range(half):
            f = base + r
            for lane, sign in ((lo + r, -1.0), (lo + half + r, 1.0)):
                e[TR_ONE, lane] = 0.0
                e[f, lane] = 1.0
                e[N_FREQ + f, lane] = 1.0
                e[2 * N_FREQ + f, MXU_WIDTH + lane] = sign
                e[3 * N_FREQ + f, MXU_WIDTH + lane] = sign
    return e


def _np_rope_expand_all():
    return np.concatenate([
        _np_rope_expand([0, PAIR_STRIDE], MLA_ROPE, FREQ_BASE_MLA),
        _np_rope_expand([h * HEAD_DIM for h in range(N_HEADS)], ROT_MOBA, FREQ_BASE_MOBA),
        _np_rope_expand([g * DIFF_QK for g in range(2 * N_HEADS)], ROT_DIFF, FREQ_BASE_DIFF),
    ], axis=1)


def _dot(a, b):
    return jnp.dot(a, b, preferred_element_type=jnp.float32)


def _dot_nt(a, b):
    return lax.dot_general(a, b, (((1,), (1,)), ((), ())), preferred_element_type=jnp.float32)


def _split2(a):
    hi = a.astype(jnp.bfloat16)
    lo = (a - hi.astype(jnp.float32)).astype(jnp.bfloat16)
    return hi, lo


def _split3(a):
    hi = a.astype(jnp.bfloat16)
    r = a - hi.astype(jnp.float32)
    mid = r.astype(jnp.bfloat16)
    lo = (r - mid.astype(jnp.float32)).astype(jnp.bfloat16)
    return hi, mid, lo


def _group_mean_sq(a, g_bf16):
    return _dot((a * a).astype(jnp.bfloat16), g_bf16)


def _rope(x, tabs, half):
    w = x.shape[-1]
    lane = lax.broadcasted_iota(jnp.int32, x.shape, 1)
    partner = jnp.where((lane & (2 * half - 1)) >= half, pltpu.roll(x, half, 1), pltpu.roll(x, w - half, 1))
    return x * tabs[:, 0:w] + partner * tabs[:, w:2 * w]


def _lane_mask(shape, lo, hi):
    lane = lax.broadcasted_iota(jnp.int32, shape, len(shape) - 1)
    return (lane >= lo) & (lane < hi)


def _prep_body(x_ref, tr_ref, anorm_ref, win_ref, wuq_ref, wukvk_ref, wukvv_ref, gpair_ref, g64_ref, g32_ref,
               exp_ref, tril_ref, vsel_ref, par_ref,
               aq_ref, ak_ref, av_ref, fq_ref, fk_ref, fv_ref, fdcol_ref, fdrow_ref, mq_ref, mk_ref, mv_ref, msel_ref,
               dq_ref, dk_ref, dv_ref, eq_ref,
               kmean_s, carry_s):
    j = pl.program_id(1)
    tm = x_ref.shape[1]
    pr = MOBA_BLOCK
    n_parts = tm // pr
    nbp = kmean_s.shape[0]
    bf = jnp.bfloat16
    gpair, g64, g32 = gpair_ref[...], g64_ref[...], g32_ref[...]
    vsel = vsel_ref[...]
    tril = tril_ref[...]

    @pl.when(j == 0)
    def _():
        kmean_s[...] = jnp.zeros_like(kmean_s)
        carry_s[...] = jnp.zeros_like(carry_s)

    def prow(r, width=MXU_WIDTH):
        return par_ref[r:r + 1, 0:width]

    def project(st):
        rows = st["rows"]
        x = x_ref[0, rows, :]
        xb = (x * lax.rsqrt(jnp.mean(x * x, axis=-1, keepdims=True) + EPS) * anorm_ref[...]).astype(bf)
        tabs = _dot(tr_ref[0, rows, :], exp_ref[...])
        st["tab_mla"] = tabs[:, 0:2 * MXU_WIDTH]
        st["tab_moba"] = tabs[:, 2 * MXU_WIDTH:4 * MXU_WIDTH]
        st["tab_diff"] = tabs[:, 4 * MXU_WIDTH:6 * MXU_WIDTH]

        def run(lo, hi):
            wide = _dot(xb, win_ref[:, lo:hi])
            return lambda off, width: wide[:, off - lo:off - lo + width]

        run_m = run(PK_MQ, PK_DQ)
        run_a = run(PK_CQ, PK_FQ)
        run_f = run(PK_FQ, PK_MQ)
        run_d = run(PK_DQ, PK_END)
        st["direct"] = {"mq": run_m(PK_MQ, MXU_WIDTH), "mk": run_m(PK_MK, MXU_WIDTH),
                        "fq": run_f(PK_FQ, MXU_WIDTH), "fk": run_f(PK_FK, MXU_WIDTH),
                        "dq": run_d(PK_DQ, MXU_WIDTH), "dk": run_d(PK_DK, MXU_WIDTH),
                        "eq": run_d(PK_EQ, MXU_WIDTH)}
        st["values"] = {"fv": run_f(PK_FV, MXU_WIDTH).astype(bf), "mv": run_m(PK_MV, MXU_WIDTH).astype(bf),
                        "dv": run_d(PK_DV, MXU_WIDTH).astype(bf)}
        p_cq, p_ckv, st["p_kr"] = run_a(PK_CQ, MXU_WIDTH), run_a(PK_CKV, MLA_KV_RANK), run_a(PK_KR, 2 * MXU_WIDTH)
        cqn = p_cq * lax.rsqrt(jnp.sum(p_cq * p_cq, axis=-1, keepdims=True) * (1.0 / MLA_Q_RANK) + EPS) * prow(P_CQ)
        ckvn = p_ckv * lax.rsqrt(jnp.mean(p_ckv * p_ckv, axis=-1, keepdims=True) + EPS) * prow(P_CKV, MLA_KV_RANK)
        st["cqb"], st["ckvb"] = cqn.astype(bf), ckvn.astype(bf)
        z = run_f(PK_FF, LANES) + prow(P_FB, LANES)
        log_f = jnp.minimum(z, 0.0) - jnp.log1p(jnp.exp(-jnp.abs(z)))
        st["log_f"] = _split3(jnp.where(_lane_mask(log_f.shape, 0, N_HEADS), log_f, 0.0))

    def second(st):
        gmat = {"fq": g64, "fk": g64, "mq": g64, "mk": g64, "dq": g32, "dk": g32, "eq": g64}
        gains = {"fq": P_FQ, "fk": P_FK, "mq": P_MQ, "mk": P_MK, "dq": P_DQ, "dk": P_DK, "eq": P_EQ}
        ms = {name: _group_mean_sq(a, gmat[name]) for name, a in st["direct"].items()}
        st["qa"] = _dot(st["cqb"], wuq_ref[...])
        st["ka"] = st["p_kr"] + _dot(st["ckvb"], wukvk_ref[...])
        st["values"]["av"] = _dot(st["ckvb"], wukvv_ref[...]).astype(bf)
        l1, l2, l3 = st["log_f"]
        st["cumsum"] = (_dot(tril, l1) + _dot(tril, l2)) + _dot(tril, l3)
        st["normed"] = {name: a * lax.rsqrt(ms[name] + EPS) * prow(gains[name]) for name, a in st["direct"].items()}

    def third(st):
        rows = st["rows"]
        mq = _rope(st["normed"]["mq"], st["tab_moba"], ROT_MOBA // 2)
        mk = _rope(st["normed"]["mk"], st["tab_moba"], ROT_MOBA // 2)
        st["mq"], st["mk"] = mq, mk
        kmean_s[pl.ds(st["blk"], 1), :] = jnp.mean(mk, axis=0, keepdims=True)
        km_hi, km_lo = _split2(kmean_s[...])
        st["gates"] = []
        for h in range(N_HEADS):
            q_hi, q_lo = _split2(jnp.where(_lane_mask(mq.shape, h * HEAD_DIM, (h + 1) * HEAD_DIM), mq, 0.0))
            st["gates"].append((_dot_nt(km_hi, q_hi) + _dot_nt(km_lo, q_hi)) + _dot_nt(km_hi, q_lo))
        st["ms_qa"] = [_group_mean_sq(st["qa"][:, p * MXU_WIDTH:(p + 1) * MXU_WIDTH], gpair) for p in range(2)]
        st["ms_ka"] = [_group_mean_sq(st["ka"][:, p * MXU_WIDTH:(p + 1) * MXU_WIDTH], gpair) for p in range(2)]
        dec = carry_s[...] + st["cumsum"]
        carry_s[...] = dec[pr - 1:pr, :]
        dec2 = dec * LOG2E
        d1, d2, d3 = _split3(dec2)
        for h in range(N_HEADS):
            fdcol_ref[0, rows, h * LANES:(h + 1) * LANES] = jnp.broadcast_to(dec2[:, h:h + 1], (pr, LANES))
        row_sel = jnp.where(lax.broadcasted_iota(jnp.int32, (SUBLANES, LANES), 0)
                            == lax.broadcasted_iota(jnp.int32, (SUBLANES, LANES), 1), 1.0, 0.0).astype(bf)
        fdrow_ref[0, :, rows] = (_dot_nt(row_sel, d1) + _dot_nt(row_sel, d2)) + _dot_nt(row_sel, d3)
        for name, ref in (("fv", fv_ref), ("mv", mv_ref), ("dv", dv_ref), ("av", av_ref)):
            ref[0, :, rows] = _values_t(st["values"][name], vsel)

    def finish(st):
        rows = st["rows"]
        normed = st["normed"]
        fq_ref[0, rows, :] = normed["fq"].astype(bf)
        fk_ref[0, rows, :] = normed["fk"].astype(bf)
        eq_ref[0, rows, :] = normed["eq"].astype(bf)
        dq_ref[0, rows, :] = _rope(normed["dq"], st["tab_diff"], ROT_DIFF // 2).astype(bf)
        dk_ref[0, rows, :] = _rope(normed["dk"], st["tab_diff"], ROT_DIFF // 2).astype(bf)
        mq_ref[0, rows, :] = (st["mq"] * (LOG2E * HEAD_DIM ** -0.5)).astype(bf)
        mk_ref[0, rows, :] = st["mk"].astype(bf)
        for p in range(2):
            sl = slice(p * MXU_WIDTH, (p + 1) * MXU_WIDTH)
            aq_ref[0, rows, sl] = _rope(st["qa"][:, sl] * lax.rsqrt(st["ms_qa"][p] + EPS) * prow(P_GQ),
                                        st["tab_mla"], MLA_ROPE // 2).astype(bf)
            ak_ref[0, rows, sl] = _rope(st["ka"][:, sl] * lax.rsqrt(st["ms_ka"][p] + EPS) * prow(P_GK),
                                        st["tab_mla"], MLA_ROPE // 2).astype(bf)
        blk = lax.broadcasted_iota(jnp.int32, (nbp, pr), 0)
        past = blk < st["blk"]
        for h in range(N_HEADS):
            work = jnp.where(past, st["gates"][h], NEG_INF)
            sel = jnp.zeros((nbp, pr), jnp.bool_)
            for _ in range(MOBA_TOPK):
                mx = jnp.max(work, axis=0, keepdims=True)
                first = jnp.min(jnp.where(work == mx, blk, nbp), axis=0, keepdims=True)
                pick = blk == first
                sel = sel | pick
                work = jnp.where(pick, REMOVED, work)
            msel_ref[0, h * nbp:(h + 1) * nbp, rows] = jnp.where(sel & past, 0.0, NEG_INF)

    parts = [{"rows": slice(p * pr, (p + 1) * pr), "blk": j * n_parts + p} for p in range(n_parts)]
    for phase in (project, second, third, finish):
        for st in parts:
            phase(st)


def _const_spec(a, layer=None):
    if layer is None:
        n = a.ndim
        return pl.BlockSpec(a.shape, lambda *_: (0,) * n)
    n = a.ndim - 1
    return pl.BlockSpec((None,) + a.shape[1:], lambda *_: (layer,) + (0,) * n)


def _moba_blocks_padded(seq):
    return -(-(seq // MOBA_BLOCK) // SUBLANES) * SUBLANES


def _prep_call(layer, tm, x, tr, anorm, win, wuq, wukvk, wukvv, gpair, g64, g32, expand, tril, vsel, par):
    bsz, seq, d = x.shape
    assert seq % tm == 0 and tm % MOBA_BLOCK == 0 and tril.shape == (MOBA_BLOCK, MOBA_BLOCK)
    nbp = _moba_blocks_padded(seq)
    bf = jnp.bfloat16
    f32 = jnp.float32

    vt = -N_HEADS * V_ROWS
    widths = [(2 * MXU_WIDTH, bf), (2 * MXU_WIDTH, bf), (vt, bf),
              (GROUP_WIDTH, bf), (GROUP_WIDTH, bf), (vt, bf),
              (N_HEADS * LANES, f32), (-SUBLANES, f32),
              (GROUP_WIDTH, bf), (GROUP_WIDTH, bf), (vt, bf), (-N_HEADS * nbp, f32),
              (GROUP_WIDTH, bf), (GROUP_WIDTH, bf), (vt, bf),
              (GROUP_WIDTH, bf)]

    def tok(width):
        if width > 0:
            return pl.BlockSpec((1, tm, width), lambda b, j: (b, j, 0))
        return pl.BlockSpec((1, -width, tm), lambda b, j: (b, 0, j))

    def shape(width):
        return (bsz, seq, width) if width > 0 else (bsz, -width, seq)

    consts = [anorm, win, wuq, wukvk, wukvv, gpair, g64, g32, expand, tril, vsel, par]
    layered = [True, True, True, True, True, False, False, False, False, False, False, True]
    return pl.pallas_call(
        _prep_body,
        grid=(bsz, seq // tm),
        in_specs=[tok(d), tok(TR_WIDTH)] + [_const_spec(c, layer if ly else None) for c, ly in zip(consts, layered)],
        out_specs=[tok(w) for w, _ in widths],
        out_shape=[jax.ShapeDtypeStruct(shape(w), dt) for w, dt in widths],
        scratch_shapes=[pltpu.VMEM((nbp, GROUP_WIDTH), jnp.float32), pltpu.VMEM((1, LANES), jnp.float32)],
        compiler_params=pltpu.CompilerParams(dimension_semantics=("arbitrary", "arbitrary"),
                                             vmem_limit_bytes=VMEM_LIMIT_BYTES),
        name="prep",
    )(x, tr, *consts)


def _memkv_body(mem_ref, mnorm_ref, w_ref, g64_ref, gain_ref, vsel_ref, k_ref, v_ref):
    m = mem_ref[0]
    mn = m * lax.rsqrt(jnp.mean(m * m, axis=-1, keepdims=True) + EPS) * mnorm_ref[...]
    kv = _dot(mn.astype(jnp.bfloat16), w_ref[...])
    k = kv[:, 0:GROUP_WIDTH]
    k = k * lax.rsqrt(_group_mean_sq(k, g64_ref[...]) + EPS) * gain_ref[...]
    k_ref[0] = k.astype(jnp.bfloat16)
    v_ref[0] = _values_t(kv[:, GROUP_WIDTH:2 * GROUP_WIDTH].astype(jnp.bfloat16), vsel_ref[...])


def _memkv_call(layer, mem, mnorm, w, g64, gain, vsel):
    bsz, mlen, d = mem.shape
    k_shape, vt_shape = (bsz, mlen, GROUP_WIDTH), (bsz, N_HEADS * V_ROWS, mlen)
    return pl.pallas_call(
        _memkv_body,
        grid=(bsz,),
        in_specs=[pl.BlockSpec((1, mlen, d), lambda b: (b, 0, 0)), _const_spec(mnorm, layer), _const_spec(w, layer),
                  _const_spec(g64), _const_spec(gain, layer), _const_spec(vsel)],
        out_specs=[pl.BlockSpec((1,) + s[1:], lambda b: (b, 0, 0)) for s in (k_shape, vt_shape)],
        out_shape=[jax.ShapeDtypeStruct(s, jnp.bfloat16) for s in (k_shape, vt_shape)],
        compiler_params=pltpu.CompilerParams(dimension_semantics=("arbitrary",), vmem_limit_bytes=VMEM_LIMIT_BYTES),
        name="mem_kv",
    )(mem, mnorm, w, g64, gain, vsel)


Q_SUB = 2 * MXU_WIDTH


class _AttnCfg:
    def __init__(self, name, vheads, n_maps, causal, decay=False, select=False, diff=False, tile=1024, lookahead=8,
                 q_sub=Q_SUB):
        self.name = name
        self.vheads = vheads
        self.n_maps = n_maps
        self.causal = causal
        self.decay = decay
        self.select = select
        self.diff = diff
        self.tile = tile
        self.lookahead = lookahead
        self.q_sub = q_sub


_PLAIN_VHEADS = [(0, h * HEAD_DIM, (h + 1) * HEAD_DIM, 0, h) for h in range(N_HEADS)]
_CFG_MLA = _AttnCfg("attn_mla", [((h // 2) * MXU_WIDTH, (h % 2) * PAIR_STRIDE, (h % 2) * PAIR_STRIDE + MLA_QK, 0, h)
                                 for h in range(N_HEADS)], 1, True, q_sub=MXU_WIDTH, lookahead=16, tile=2048)
_CFG_FOX = _AttnCfg("attn_fox", _PLAIN_VHEADS, 1, True, decay=True, tile=2048, lookahead=8)
_CFG_MOBA = _AttnCfg("attn_moba", _PLAIN_VHEADS, 1, True, select=True, tile=2048, lookahead=8)
_CFG_DIFF = _AttnCfg("attn_diff", [(0, h * HEAD_DIM + c * DIFF_QK, h * HEAD_DIM + (c + 1) * DIFF_QK, c, h)
                                   for c in range(2) for h in range(N_HEADS)], 2, True, diff=True, q_sub=MXU_WIDTH,
                     lookahead=32)
_CFG_MEM = _AttnCfg("attn_mem", _PLAIN_VHEADS, 1, False, tile=2048)


ONES_ROW = HEAD_DIM
V_ROWS = HEAD_DIM + 16


def _np_value_select():
    sel = np.zeros((N_HEADS * V_ROWS, GROUP_WIDTH), np.float32)
    for h in range(N_HEADS):
        for d in range(HEAD_DIM):
            sel[h * V_ROWS + d, h * HEAD_DIM + d] = 1.0
    return sel


def _values_t(v, vsel):
    vt = _dot_nt(vsel, v)
    row = lax.broadcasted_iota(jnp.int32, vt.shape, 0)
    ones = row == ONES_ROW
    for h in range(1, N_HEADS):
        ones = ones | (row == h * V_ROWS + ONES_ROW)
    return jnp.where(ones, 1.0, vt).astype(jnp.bfloat16)


def _tile_lanes(x, width):
    return jnp.tile(x, (1, width // LANES)) if width != LANES else x


def _attn_body(cfg, qi_ref, kj_ref, *refs):
    refs = list(refs)
    q_ref, k_ref, vt_ref = refs[:3]
    pos = 3
    if cfg.decay:
        dq_ref, dk_ref = refs[pos:pos + 2]
        pos += 2
    if cfg.select:
        sel_ref = refs[pos]
        pos += 1
    if cfg.diff:
        g64_ref, gsub_ref, lam_ref = refs[pos:pos + 3]
        pos += 3
    o_ref, qm_s, m_s, acc_s = refs[pos:pos + 4]

    t = pl.program_id(1)
    i = qi_ref[t]
    j = kj_ref[t]
    tq = q_ref.shape[1]
    tk = k_ref.shape[1]

    @pl.when(j == 0)
    def _():
        for n, (off, lo, hi, _, _) in enumerate(cfg.vheads):
            qb = q_ref[0, :, off:off + MXU_WIDTH]
            qm_s[n] = jnp.where(_lane_mask(qb.shape, lo, hi), qb, jnp.zeros_like(qb))
        m_s[...] = jnp.full(m_s.shape, NEG_INF, jnp.float32)
        acc_s[...] = jnp.zeros_like(acc_s)

    def step(diag):
        qs = min(tq, cfg.q_sub)
        items = [(n, u) for n in range(len(cfg.vheads)) for u in range(tq // qs)]

        def n_keys(u):
            return (u + 1) * qs if diag else tk

        def scores(item):
            n, u = item
            off, _, _, _, h = cfg.vheads[n]
            nk = n_keys(u)
            cols = slice(u * qs, (u + 1) * qs)
            s = _dot_nt(k_ref[0, 0:nk, off:off + MXU_WIDTH], qm_s[n, cols, :])
            if cfg.decay:
                s = (dq_ref[0, h:h + 1, cols] - _tile_lanes(dk_ref[0, 0:nk, h * LANES:(h + 1) * LANES], qs)) + s
            if cfg.select:
                nbp = sel_ref.shape[1] // N_HEADS
                qpos = u * qs + lax.broadcasted_iota(jnp.int32, (1, qs), 1)
                parts = []
                for kb in range(nk // MOBA_BLOCK):
                    rows = s[kb * MOBA_BLOCK:(kb + 1) * MOBA_BLOCK, :]
                    if not (diag and kb == nk // MOBA_BLOCK - 1):
                        bias = sel_ref[0, pl.ds(h * nbp + j * (tk // MOBA_BLOCK) + kb, 1), cols]
                        if diag:
                            bias = jnp.where(qpos < (kb + 1) * MOBA_BLOCK, 0.0, bias)
                        rows = rows + bias
                    parts.append(rows)
                s = parts[0] if len(parts) == 1 else jnp.concatenate(parts, axis=0)
            if diag:
                key = lax.broadcasted_iota(jnp.int32, (nk, qs), 0)
                qry = u * qs + lax.broadcasted_iota(jnp.int32, (nk, qs), 1)
                s = jnp.where(key <= qry, s, NEG_INF)
            return s, jnp.max(s, axis=0, keepdims=True)

        raw = {it: scores(items[it]) for it in range(min(cfg.lookahead, len(items)))}
        for it, (n, u) in enumerate(items):
            h = cfg.vheads[n][4]
            nk = n_keys(u)
            cols = slice(u * qs, (u + 1) * qs)
            s, s_max = raw.pop(it)
            m_prev = m_s[n, :, cols]
            m_new = jnp.maximum(m_prev, s_max)
            alpha = jnp.exp2(m_prev - m_new)
            p = jnp.exp2(s - m_new)
            m_s[n, :, cols] = m_new
            acc_s[n, :, cols] = acc_s[n, :, cols] * alpha + _dot(vt_ref[0, h * V_ROWS:(h + 1) * V_ROWS, 0:nk],
                                                                 p.astype(jnp.bfloat16))
            if it + cfg.lookahead < len(items):
                raw[it + cfg.lookahead] = scores(items[it + cfg.lookahead])

    if cfg.causal:
        pl.when(j < i)(functools.partial(step, False))
        pl.when(j == i)(functools.partial(step, True))
        last = j == i
    else:
        step(False)
        last = j == 0

    @pl.when(last)
    def _():
        outs = []
        for c in range(cfg.n_maps):
            heads = []
            for h in range(N_HEADS):
                acc = acc_s[c * N_HEADS + h]
                heads.append(acc[0:HEAD_DIM, :] / acc[ONES_ROW:ONES_ROW + 1, :])
            outs.append(jnp.concatenate(heads, axis=0).T)
        if cfg.diff:
            o = outs[0] - lam_ref[0:1, :] * outs[1]
            o = o * lax.rsqrt(_group_mean_sq(o, g64_ref[...]) + EPS) * gsub_ref[...]
        else:
            o = outs[0]
        o_ref[0] = o.astype(o_ref.dtype)


def _attn_call(cfg, q, k, v, extras, tq, tk):
    bsz, seq, wq = q.shape
    sk = k.shape[1]
    nq = seq // tq
    assert seq % tq == 0 and sk % tk == 0
    if cfg.causal:
        assert tq == tk and sk == seq
        pairs = [(i, j) for i in range(nq) for j in range(i + 1)]
    else:
        assert sk == tk
        pairs = [(i, 0) for i in range(nq)]
    qi = jnp.asarray(np.array([p[0] for p in pairs], np.int32))
    kj = jnp.asarray(np.array([p[1] for p in pairs], np.int32))
    n_vh = len(cfg.vheads)

    in_specs = [pl.BlockSpec((1, tq, wq), lambda b, t, qi, kj: (b, qi[t], 0)),
                pl.BlockSpec((1, tk, wq), lambda b, t, qi, kj: (b, kj[t], 0)),
                pl.BlockSpec((1, N_HEADS * V_ROWS, tk), lambda b, t, qi, kj: (b, 0, kj[t]))]
    args = [q, k, v]
    if cfg.decay:
        dcol, drow = extras
        in_specs += [pl.BlockSpec((1, SUBLANES, tq), lambda b, t, qi, kj: (b, 0, qi[t])),
                     pl.BlockSpec((1, tk, N_HEADS * LANES), lambda b, t, qi, kj: (b, kj[t], 0))]
        args += [drow, dcol]
    if cfg.select:
        (sel,) = extras
        in_specs += [pl.BlockSpec((1, sel.shape[1], tq), lambda b, t, qi, kj: (b, 0, qi[t]))]
        args += [sel]
    if cfg.diff:
        layer, g64, gsub, lam_row = extras
        in_specs += [_const_spec(g64), _const_spec(gsub, layer), _const_spec(lam_row, layer)]
        args += [g64, gsub, lam_row]

    grid_spec = pltpu.PrefetchScalarGridSpec(
        num_scalar_prefetch=2,
        grid=(bsz, len(pairs)),
        in_specs=in_specs,
        out_specs=pl.BlockSpec((1, tq, GROUP_WIDTH), lambda b, t, qi, kj: (b, qi[t], 0)),
        scratch_shapes=[pltpu.VMEM((n_vh, tq, MXU_WIDTH), jnp.bfloat16),
                        pltpu.VMEM((n_vh, 1, tq), jnp.float32),
                        pltpu.VMEM((n_vh, V_ROWS, tq), jnp.float32)])
    return pl.pallas_call(
        functools.partial(_attn_body, cfg),
        grid_spec=grid_spec,
        out_shape=jax.ShapeDtypeStruct((bsz, seq, GROUP_WIDTH), jnp.bfloat16),
        compiler_params=pltpu.CompilerParams(dimension_semantics=("arbitrary", "arbitrary"),
                                             vmem_limit_bytes=VMEM_LIMIT_BYTES),
        name=cfg.name,
    )(qi, kj, *args)


def _ffn_body(nf, x_ref, oa_ref, ob_ref, oc_ref, od_ref, oe_ref, wo_ref, fnorm_ref, wg_ref, wu_ref, cw_ref, cb_ref,
              wd_ref, out_ref, xnew_s, xn_s, acc_s):
    i = pl.program_id(1)
    f = pl.program_id(2)
    tm = x_ref.shape[1]

    @pl.when(f == 0)
    def _():
        @pl.when(i == 0)
        def _():
            xn_s[0:TAIL_ROWS, :] = jnp.zeros((TAIL_ROWS, xn_s.shape[1]), xn_s.dtype)

        @pl.when(i > 0)
        def _():
            xn_s[0:TAIL_ROWS, :] = xn_s[tm:tm + TAIL_ROWS, :]

        mixed = jnp.concatenate([o_ref[0] for o_ref in (oa_ref, ob_ref, oc_ref, od_ref, oe_ref)], axis=1)
        xnew = x_ref[0] + _dot(mixed, wo_ref[...])
        xnew_s[...] = xnew
        xn = xnew * lax.rsqrt(jnp.mean(xnew * xnew, axis=-1, keepdims=True) + EPS) * fnorm_ref[...]
        xn_s[TAIL_ROWS:TAIL_ROWS + tm, :] = xn.astype(xn_s.dtype)

    def mlp_chunk():
        ge = _dot(xn_s[...], wg_ref[...])
        u = _dot(xn_s[TAIL_ROWS:TAIL_ROWS + tm, :], wu_ref[...])
        g0 = ge[TAIL_ROWS:TAIL_ROWS + tm, :]
        t1 = ge[TAIL_ROWS - 1:TAIL_ROWS, :]
        t2 = ge[TAIL_ROWS - 2:TAIL_ROWS - 1, :]
        row = lax.broadcasted_iota(jnp.int32, g0.shape, 0)
        g1 = jnp.where(row == 0, t1, pltpu.roll(g0, 1, 0))
        g2 = jnp.where(row == 0, t2, jnp.where(row == 1, t1, pltpu.roll(g0, 2, 0)))
        y = cb_ref[...] + cw_ref[0:1, :] * g2
        y = y + cw_ref[1:2, :] * g1
        y = y + cw_ref[2:3, :] * g0
        hmid = (y * (1.0 / (1.0 + jnp.exp(-y)))) * u
        return _dot(hmid.astype(jnp.bfloat16), wd_ref[...])

    @pl.when(f == 0)
    def _():
        acc_s[...] = mlp_chunk()

    if nf > 2:
        @pl.when((f > 0) & (f < nf - 1))
        def _():
            acc_s[...] += mlp_chunk()

    @pl.when(f == nf - 1)
    def _():
        out_ref[0] = xnew_s[...] + (acc_s[...] + mlp_chunk())


def _ffn_call(layer, x, outs, wo, fnorm, wg, wu, cw, cb, wd, tm, tf):
    bsz, seq, d = x.shape
    dff = wg.shape[2]
    n_tiles, nf = seq // tm, dff // tf
    assert seq % tm == 0 and dff % tf == 0 and nf >= 2
    tok = lambda w: pl.BlockSpec((1, tm, w), lambda b, i, f: (b, i, 0))
    return pl.pallas_call(
        functools.partial(_ffn_body, nf),
        grid=(bsz, n_tiles, nf),
        in_specs=[tok(d)] + [tok(GROUP_WIDTH)] * 5 + [
            _const_spec(wo, layer),
            _const_spec(fnorm, layer),
            pl.BlockSpec((None, d, tf), lambda b, i, f: (layer, 0, f)),
            pl.BlockSpec((None, d, tf), lambda b, i, f: (layer, 0, f)),
            pl.BlockSpec((None, SUBLANES, tf), lambda b, i, f: (layer, 0, f)),
            pl.BlockSpec((None, 1, tf), lambda b, i, f: (layer, 0, f)),
            pl.BlockSpec((None, tf, d), lambda b, i, f: (layer, f, 0))],
        out_specs=tok(d),
        out_shape=jax.ShapeDtypeStruct((bsz, seq, d), jnp.float32),
        scratch_shapes=[pltpu.VMEM((tm, d), jnp.float32), pltpu.VMEM((TAIL_ROWS + tm, d), jnp.bfloat16),
                        pltpu.VMEM((tm, d), jnp.float32)],
        compiler_params=pltpu.CompilerParams(dimension_semantics=("arbitrary", "arbitrary", "arbitrary"),
                                             vmem_limit_bytes=VMEM_LIMIT_BYTES),
        name="ffn",
    )(x, *outs, wo, fnorm, wg, wu, cw, cb, wd)


def _pad_rows(v, width=MXU_WIDTH):
    return jnp.pad(v.astype(jnp.float32), ((0, 0), (0, width - v.shape[1])))


def _tile_rows(g, reps):
    return jnp.tile(g.astype(jnp.float32), (1, reps))


def _pack_in_projection(w):
    idx = _np_in_index()
    pieces, start = [], 0
    while start < PK_END:
        stop = start + 1
        if idx[start] == _SRC_END:
            while stop < PK_END and idx[stop] == _SRC_END:
                stop += 1
            pieces.append(jnp.zeros(w.shape[:-1] + (stop - start,), w.dtype))
        else:
            while stop < PK_END and idx[stop] == idx[stop - 1] + 1:
                stop += 1
            pieces.append(w[..., int(idx[start]):int(idx[stop - 1]) + 1])
        start = stop
    return jnp.concatenate(pieces, axis=-1)


def _zero_col(w):
    return jnp.concatenate([w, jnp.zeros(w.shape[:-1] + (1,), w.dtype)], axis=-1)


def _rope_table(positions):
    pos = positions.astype(jnp.float32)[:, :, None]
    inv = [ROPE_THETA ** (-jnp.arange(0, rot, 2, dtype=jnp.float32) / rot) for rot in (MLA_ROPE, ROT_MOBA, ROT_DIFF)]
    inv = jnp.concatenate(inv + [jnp.zeros((N_FREQ - TR_ONE,), jnp.float32)])
    ang = pos * inv
    c, s = jnp.cos(ang), jnp.sin(ang)
    c_hi = c.astype(jnp.bfloat16)
    c_lo = (c - c_hi.astype(jnp.float32)).astype(jnp.bfloat16)
    s_hi = s.astype(jnp.bfloat16)
    s_lo = (s - s_hi.astype(jnp.float32)).astype(jnp.bfloat16)
    return jnp.concatenate([c_hi, c_lo, s_hi, s_lo], axis=-1)


def _pick_tile(n, pref):
    t = pref
    while n % t:
        t //= 2
    return t


def kernel(x, mem, positions, attn_norm, ffn_norm, mem_norm, w_in, mla_cq_norm, mla_ckv_norm, mla_w_uq, mla_w_ukv, mla_q_norm, mla_k_norm, fox_b_f, fox_q_norm, fox_k_norm, moba_q_norm, moba_k_norm, diff_lambda, diff_q_norm, diff_k_norm, diff_sub_norm, mem_w_kv, mem_q_norm, mem_k_norm, w_o, ffn_w_gate, ffn_w_up, ffn_conv_w, ffn_conv_b, ffn_w_down):
    bsz, seq, d = x.shape
    depth = w_in.shape[0]
    dff = ffn_w_gate.shape[2]
    bf = jnp.bfloat16
    f32 = jnp.float32

    uq_idx = _np_uq_index()
    ukvk_idx, ukvv_idx = _np_ukv_index()
    gpair = jnp.asarray(_np_group_matrix(_PAIR_GROUPS), bf)
    g64 = jnp.asarray(_np_group_matrix(_G64_GROUPS), bf)
    g32 = jnp.asarray(_np_group_matrix(_G32_GROUPS), bf)
    expand = jnp.asarray(_np_rope_expand_all(), bf)
    t_prep = max(_pick_tile(seq, PREP_TILE), MOBA_BLOCK)
    tril = jnp.asarray(np.tril(np.ones((MOBA_BLOCK, MOBA_BLOCK), np.float32)), bf)
    vsel = jnp.asarray(_np_value_select(), bf)
    tr = _rope_table(positions)

    tile = lambda cfg: max(_pick_tile(seq, cfg.tile), MOBA_BLOCK) if cfg.select else _pick_tile(seq, cfg.tile)
    t_ffn = _pick_tile(seq, 512)
    tf = dff // 2 if (dff // 2) % LANES == 0 else dff

    win = _pack_in_projection(w_in.astype(bf))
    wuq = jnp.take(_zero_col(mla_w_uq), uq_idx, axis=2)
    wuq = jnp.pad(wuq, ((0, 0), (0, MXU_WIDTH - MLA_Q_RANK), (0, 0))).astype(bf)
    wukv = _zero_col(mla_w_ukv)
    wukvk = jnp.take(wukv, ukvk_idx, axis=2).astype(bf)
    wukvv = jnp.take(wukv, ukvv_idx, axis=2).astype(bf)
    pair = lambda g: _pad_rows(_tile_rows(g, 2))
    rows = [jnp.zeros((depth, MXU_WIDTH), f32)] * P_ROWS
    rows[P_CQ] = _pad_rows(mla_cq_norm)
    rows[P_CKV] = _pad_rows(mla_ckv_norm)
    rows[P_GQ] = pair(mla_q_norm) * (LOG2E * MLA_QK ** -0.5)
    rows[P_GK] = pair(mla_k_norm)
    rows[P_FQ] = _tile_rows(fox_q_norm, N_HEADS) * (LOG2E * HEAD_DIM ** -0.5)
    rows[P_FK] = _tile_rows(fox_k_norm, N_HEADS)
    rows[P_FB] = _pad_rows(fox_b_f)
    rows[P_MQ] = _tile_rows(moba_q_norm, N_HEADS)
    rows[P_MK] = _tile_rows(moba_k_norm, N_HEADS)
    rows[P_DQ] = _tile_rows(diff_q_norm, 2 * N_HEADS) * (LOG2E * DIFF_QK ** -0.5)
    rows[P_DK] = _tile_rows(diff_k_norm, 2 * N_HEADS)
    rows[P_EQ] = _tile_rows(mem_q_norm, N_HEADS) * (LOG2E * HEAD_DIM ** -0.5)
    par = jnp.stack(rows, axis=1)
    anorm = attn_norm.astype(f32)[:, None, :]
    mnorm = mem_norm.astype(f32)[:, None, :]
    fnorm = ffn_norm.astype(f32)[:, None, :]
    wmem = mem_w_kv.astype(bf)
    mem_gain = _tile_rows(mem_k_norm, N_HEADS)[:, None, :]

    lam_init = jnp.asarray([0.8 - 0.6 * math.exp(-0.3 * l) for l in range(depth)], f32)
    lam_vec = diff_lambda.astype(f32)
    lam = (jnp.exp(jnp.sum(lam_vec[:, 0] * lam_vec[:, 1], axis=-1))
           - jnp.exp(jnp.sum(lam_vec[:, 2] * lam_vec[:, 3], axis=-1)) + lam_init)
    lam_row = jnp.broadcast_to(lam[:, None, None], (depth, 1, GROUP_WIDTH))
    gsub = (_tile_rows(diff_sub_norm, N_HEADS) * (1.0 - lam_init)[:, None])[:, None, :]

    wo = w_o.astype(bf)
    wg, wu, wd = ffn_w_gate.astype(bf), ffn_w_up.astype(bf), ffn_w_down.astype(bf)
    cw = jnp.pad(ffn_conv_w.astype(f32), ((0, 0), (0, SUBLANES - CONV_WIDTH), (0, 0)))
    cb = ffn_conv_b.astype(f32)[:, None, :]

    for l in range(depth):
        (aq, ak, av, fq, fk, fv, fdcol, fdrow, mq, mk, mv, msel, dq, dk, dv, eq) = _prep_call(
            l, t_prep, x, tr, anorm, win, wuq, wukvk, wukvv, gpair, g64, g32, expand, tril, vsel, par)
        ek, ev = _memkv_call(l, mem, mnorm, wmem, g64, mem_gain, vsel)

        o_a = _attn_call(_CFG_MLA, aq, ak, av, (), tile(_CFG_MLA), tile(_CFG_MLA))
        o_b = _attn_call(_CFG_FOX, fq, fk, fv, (fdcol, fdrow), tile(_CFG_FOX), tile(_CFG_FOX))
        o_c = _attn_call(_CFG_MOBA, mq, mk, mv, (msel,), tile(_CFG_MOBA), tile(_CFG_MOBA))
        o_d = _attn_call(_CFG_DIFF, dq, dk, dv, (l, g64, gsub, lam_row), tile(_CFG_DIFF), tile(_CFG_DIFF))
        o_e = _attn_call(_CFG_MEM, eq, ek, ev, (), tile(_CFG_MEM), mem.shape[1])

        x = _ffn_call(l, x, (o_a, o_b, o_c, o_d, o_e), wo, fnorm, wg, wu, cw, cb, wd, t_ffn, tf)
    return x
```

```python
import functools
import math

import numpy as np
import jax
import jax.numpy as jnp
from jax import lax
from jax.experimental import pallas as pl
from jax.experimental.pallas import tpu as pltpu

N_HEADS = 4
HEAD_DIM = 64
GROUP_WIDTH = N_HEADS * HEAD_DIM
MLA_Q_RANK = 192
MLA_KV_RANK = 128
MLA_NOPE = 64
MLA_ROPE = 32
MLA_QK = MLA_NOPE + MLA_ROPE
DIFF_QK = HEAD_DIM // 2
ROPE_THETA = 500000.0
ROT_MOBA = HEAD_DIM // 4
ROT_DIFF = DIFF_QK // 4
MOBA_BLOCK = 256
MOBA_TOPK = 3
CONV_WIDTH = 3
EPS = 1e-6
NEG_INF = -1e30
LOG2E = math.log2(math.e)
REMOVED = -3e38

LANES = 128
SUBLANES = 8
MXU_WIDTH = 256
TAIL_ROWS = 16
PREP_TILE = 512
VMEM_LIMIT_BYTES = 56 * 1024 * 1024

_SRC_CQ = 0
_SRC_CKV = _SRC_CQ + MLA_Q_RANK
_SRC_KR = _SRC_CKV + MLA_KV_RANK
_SRC_FOX = _SRC_KR + MLA_ROPE
_SRC_FOXF = _SRC_FOX + 3 * GROUP_WIDTH
_SRC_MOBA = _SRC_FOXF + N_HEADS
_SRC_DIFF = _SRC_MOBA + 3 * GROUP_WIDTH
_SRC_MEMQ = _SRC_DIFF + 3 * GROUP_WIDTH
_SRC_END = _SRC_MEMQ + GROUP_WIDTH

PK_CQ = 0
PK_CKV = 256
PK_KR = 384
PK_FQ, PK_FK, PK_FV = 896, 1152, 1408
PK_FF = 1664
PK_MQ, PK_MK, PK_MV = 1792, 2048, 2304
PK_DQ, PK_DK, PK_DV = 2560, 2816, 3072
PK_EQ = 3328
PK_END = 3584

PAIR_STRIDE = MLA_QK


def _pair_lane(h, d):
    return (h // 2) * MXU_WIDTH + (h % 2) * PAIR_STRIDE + d


N_FREQ = 32
FREQ_BASE_MLA = 0
FREQ_BASE_MOBA = MLA_ROPE // 2
FREQ_BASE_DIFF = FREQ_BASE_MOBA + ROT_MOBA // 2
TR_ONE = FREQ_BASE_DIFF + ROT_DIFF // 2
TR_WIDTH = 4 * N_FREQ
assert TR_ONE < N_FREQ and TR_WIDTH == LANES

(P_CQ, P_CKV, P_GQ, P_GK, P_FQ, P_FK, P_FB, P_MQ, P_MK, P_DQ, P_DK, P_EQ) = range(12)
P_ROWS = 16


def _np_in_index():
    idx = np.full((PK_END,), _SRC_END, np.int32)
    idx[PK_CQ:PK_CQ + MLA_Q_RANK] = np.arange(_SRC_CQ, _SRC_CQ + MLA_Q_RANK)
    idx[PK_CKV:PK_CKV + MLA_KV_RANK] = np.arange(_SRC_CKV, _SRC_CKV + MLA_KV_RANK)
    for h in range(N_HEADS):
        for d in range(MLA_ROPE):
            idx[PK_KR + _pair_lane(h, d)] = _SRC_KR + d
    idx[PK_FQ:PK_FQ + 3 * GROUP_WIDTH] = np.arange(_SRC_FOX, _SRC_FOX + 3 * GROUP_WIDTH)
    idx[PK_FF:PK_FF + N_HEADS] = np.arange(_SRC_FOXF, _SRC_FOXF + N_HEADS)
    idx[PK_MQ:PK_MQ + 3 * GROUP_WIDTH] = np.arange(_SRC_MOBA, _SRC_MOBA + 3 * GROUP_WIDTH)
    idx[PK_DQ:PK_DQ + 3 * GROUP_WIDTH] = np.arange(_SRC_DIFF, _SRC_DIFF + 3 * GROUP_WIDTH)
    idx[PK_EQ:PK_EQ + GROUP_WIDTH] = np.arange(_SRC_MEMQ, _SRC_MEMQ + GROUP_WIDTH)
    return idx


def _np_uq_index():
    idx = np.full((2 * MXU_WIDTH,), N_HEADS * MLA_QK, np.int32)
    for h in range(N_HEADS):
        for d in range(MLA_QK):
            idx[_pair_lane(h, d)] = h * MLA_QK + d
    return idx


def _np_ukv_index():
    zero = N_HEADS * (MLA_NOPE + HEAD_DIM)
    idx_k = np.full((2 * MXU_WIDTH,), zero, np.int32)
    idx_v = np.zeros((GROUP_WIDTH,), np.int32)
    for h in range(N_HEADS):
        for d in range(MLA_NOPE):
            idx_k[_pair_lane(h, MLA_ROPE + d)] = h * (MLA_NOPE + HEAD_DIM) + d
        for d in range(HEAD_DIM):
            idx_v[h * HEAD_DIM + d] = h * (MLA_NOPE + HEAD_DIM) + MLA_NOPE + d
    return idx_k, idx_v


def _np_group_matrix(groups):
    g = np.zeros((MXU_WIDTH, MXU_WIDTH), np.float32)
    for lo, size in groups:
        g[lo:lo + size, lo:lo + size] = 1.0 / size
    return g


_PAIR_GROUPS = [(0, MLA_ROPE), (MLA_ROPE, MLA_NOPE), (PAIR_STRIDE, MLA_ROPE), (PAIR_STRIDE + MLA_ROPE, MLA_NOPE)]
_G64_GROUPS = [(h * HEAD_DIM, HEAD_DIM) for h in range(N_HEADS)]
_G32_GROUPS = [(g * DIFF_QK, DIFF_QK) for g in range(2 * N_HEADS)]


def _np_rope_expand(regions, rot, base):
    half = rot // 2
    e = np.zeros((TR_WIDTH, 2 * MXU_WIDTH), np.float32)
    e[TR_ONE, 0:MXU_WIDTH] = 1.0
    for lo in regions:
        assert lo % rot == 0
        for r in range(half):
            f = base + r
            for lane, sign in ((lo + r, -1.0), (lo + half + r, 1.0)):
                e[TR_ONE, lane] = 0.0
                e[f, lane] = 1.0
                e[N_FREQ + f, lane] = 1.0
                e[2 * N_FREQ + f, MXU_WIDTH + lane] = sign
                e[3 * N_FREQ + f, MXU_WIDTH + lane] = sign
    return e


def _np_rope_expand_all():
    return np.concatenate([
        _np_rope_expand([0, PAIR_STRIDE], MLA_ROPE, FREQ_BASE_MLA),
        _np_rope_expand([h * HEAD_DIM for h in range(N_HEADS)], ROT_MOBA, FREQ_BASE_MOBA),
        _np_rope_expand([g * DIFF_QK for g in range(2 * N_HEADS)], ROT_DIFF, FREQ_BASE_DIFF),
    ], axis=1)


def _dot(a, b):
    return jnp.dot(a, b, preferred_element_type=jnp.float32)


def _dot_nt(a, b):
    return lax.dot_general(a, b, (((1,), (1,)), ((), ())), preferred_element_type=jnp.float32)


def _split2(a):
    hi = a.astype(jnp.bfloat16)
    lo = (a - hi.astype(jnp.float32)).astype(jnp.bfloat16)
    return hi, lo


def _split3(a):
    hi = a.astype(jnp.bfloat16)
    r = a - hi.astype(jnp.float32)
    mid = r.astype(jnp.bfloat16)
    lo = (r - mid.astype(jnp.float32)).astype(jnp.bfloat16)
    return hi, mid, lo


def _group_mean_sq(a, g_bf16):
    return _dot((a * a).astype(jnp.bfloat16), g_bf16)


def _rope(x, tabs, half):
    w = x.shape[-1]
    lane = lax.broadcasted_iota(jnp.int32, x.shape, 1)
    partner = jnp.where((lane & (2 * half - 1)) >= half, pltpu.roll(x, half, 1), pltpu.roll(x, w - half, 1))
    return x * tabs[:, 0:w] + partner * tabs[:, w:2 * w]


def _lane_mask(shape, lo, hi):
    lane = lax.broadcasted_iota(jnp.int32, shape, len(shape) - 1)
    return (lane >= lo) & (lane < hi)


def _prep_body(x_ref, tr_ref, anorm_ref, win_ref, wuq_ref, wukvk_ref, wukvv_ref, gpair_ref, g64_ref, g32_ref,
               exp_ref, tril_ref, vsel_ref, par_ref,
               aq_ref, ak_ref, av_ref, fq_ref, fk_ref, fv_ref, fdcol_ref, fdrow_ref, mq_ref, mk_ref, mv_ref, msel_ref,
               dq_ref, dk_ref, dv_ref, eq_ref,
               kmean_s, carry_s):
    j = pl.program_id(1)
    tm = x_ref.shape[1]
    pr = MOBA_BLOCK
    n_parts = tm // pr
    nbp = kmean_s.shape[0]
    bf = jnp.bfloat16
    gpair, g64, g32 = gpair_ref[...], g64_ref[...], g32_ref[...]
    vsel = vsel_ref[...]
    tril = tril_ref[...]

    @pl.when(j == 0)
    def _():
        kmean_s[...] = jnp.zeros_like(kmean_s)
        carry_s[...] = jnp.zeros_like(carry_s)

    def prow(r, width=MXU_WIDTH):
        return par_ref[r:r + 1, 0:width]

    def project(st):
        rows = st["rows"]
        x = x_ref[0, rows, :]
        xb = (x * lax.rsqrt(jnp.mean(x * x, axis=-1, keepdims=True) + EPS) * anorm_ref[...]).astype(bf)
        tabs = _dot(tr_ref[0, rows, :], exp_ref[...])
        st["tab_mla"] = tabs[:, 0:2 * MXU_WIDTH]
        st["tab_moba"] = tabs[:, 2 * MXU_WIDTH:4 * MXU_WIDTH]
        st["tab_diff"] = tabs[:, 4 * MXU_WIDTH:6 * MXU_WIDTH]

        def run(lo, hi):
            wide = _dot(xb, win_ref[:, lo:hi])
            return lambda off, width: wide[:, off - lo:off - lo + width]

        run_m = run(PK_MQ, PK_DQ)
        run_a = run(PK_CQ, PK_FQ)
        run_f = run(PK_FQ, PK_MQ)
        run_d = run(PK_DQ, PK_END)
        st["direct"] = {"mq": run_m(PK_MQ, MXU_WIDTH), "mk": run_m(PK_MK, MXU_WIDTH),
                        "fq": run_f(PK_FQ, MXU_WIDTH), "fk": run_f(PK_FK, MXU_WIDTH),
                        "dq": run_d(PK_DQ, MXU_WIDTH), "dk": run_d(PK_DK, MXU_WIDTH),
                        "eq": run_d(PK_EQ, MXU_WIDTH)}
        st["values"] = {"fv": run_f(PK_FV, MXU_WIDTH).astype(bf), "mv": run_m(PK_MV, MXU_WIDTH).astype(bf),
                        "dv": run_d(PK_DV, MXU_WIDTH).astype(bf)}
        p_cq, p_ckv, st["p_kr"] = run_a(PK_CQ, MXU_WIDTH), run_a(PK_CKV, MLA_KV_RANK), run_a(PK_KR, 2 * MXU_WIDTH)
        cqn = p_cq * lax.rsqrt(jnp.sum(p_cq * p_cq, axis=-1, keepdims=True) * (1.0 / MLA_Q_RANK) + EPS) * prow(P_CQ)
        ckvn = p_ckv * lax.rsqrt(jnp.mean(p_ckv * p_ckv, axis=-1, keepdims=True) + EPS) * prow(P_CKV, MLA_KV_RANK)
        st["cqb"], st["ckvb"] = cqn.astype(bf), ckvn.astype(bf)
        z = run_f(PK_FF, LANES) + prow(P_FB, LANES)
        log_f = jnp.minimum(z, 0.0) - jnp.log1p(jnp.exp(-jnp.abs(z)))
        st["log_f"] = _split3(jnp.where(_lane_mask(log_f.shape, 0, N_HEADS), log_f, 0.0))

    def second(st):
        gmat = {"fq": g64, "fk": g64, "mq": g64, "mk": g64, "dq": g32, "dk": g32, "eq": g64}
        gains = {"fq": P_FQ, "fk": P_FK, "mq": P_MQ, "mk": P_MK, "dq": P_DQ, "dk": P_DK, "eq": P_EQ}
        ms = {name: _group_mean_sq(a, gmat[name]) for name, a in st["direct"].items()}
        st["qa"] = _dot(st["cqb"], wuq_ref[...])
        st["ka"] = st["p_kr"] + _dot(st["ckvb"], wukvk_ref[...])
        st["values"]["av"] = _dot(st["ckvb"], wukvv_ref[...]).astype(bf)
        l1, l2, l3 = st["log_f"]
        st["cumsum"] = (_dot(tril, l1) + _dot(tril, l2)) + _dot(tril, l3)
        st["normed"] = {name: a * lax.rsqrt(ms[name] + EPS) * prow(gains[name]) for name, a in st["direct"].items()}

    def third(st):
        rows = st["rows"]
        mq = _rope(st["normed"]["mq"], st["tab_moba"], ROT_MOBA // 2)
        mk = _rope(st["normed"]["mk"], st["tab_moba"], ROT_MOBA // 2)
        st["mq"], st["mk"] = mq, mk
        kmean_s[pl.ds(st["blk"], 1), :] = jnp.mean(mk, axis=0, keepdims=True)
        km_hi, km_lo = _split2(kmean_s[...])
        st["gates"] = []
        for h in range(N_HEADS):
            q_hi, q_lo = _split2(jnp.where(_lane_mask(mq.shape, h * HEAD_DIM, (h + 1) * HEAD_DIM), mq, 0.0))
            st["gates"].append((_dot_nt(km_hi, q_hi) + _dot_nt(km_lo, q_hi)) + _dot_nt(km_hi, q_lo))
        st["ms_qa"] = [_group_mean_sq(st["qa"][:, p * MXU_WIDTH:(p + 1) * MXU_WIDTH], gpair) for p in range(2)]
        st["ms_ka"] = [_group_mean_sq(st["ka"][:, p * MXU_WIDTH:(p + 1) * MXU_WIDTH], gpair) for p in range(2)]
        dec = carry_s[...] + st["cumsum"]
        carry_s[...] = dec[pr - 1:pr, :]
        dec2 = dec * LOG2E
        d1, d2, d3 = _split3(dec2)
        for h in range(N_HEADS):
            fdcol_ref[0, rows, h * LANES:(h + 1) * LANES] = jnp.broadcast_to(dec2[:, h:h + 1], (pr, LANES))
        row_sel = jnp.where(lax.broadcasted_iota(jnp.int32, (SUBLANES, LANES), 0)
                            == lax.broadcasted_iota(jnp.int32, (SUBLANES, LANES), 1), 1.0, 0.0).astype(bf)
        fdrow_ref[0, :, rows] = (_dot_nt(row_sel, d1) + _dot_nt(row_sel, d2)) + _dot_nt(row_sel, d3)
        for name, ref in (("fv", fv_ref), ("mv", mv_ref), ("dv", dv_ref), ("av", av_ref)):
            ref[0, :, rows] = _values_t(st["values"][name], vsel)

    def finish(st):
        rows = st["rows"]
        normed = st["normed"]
        fq_ref[0, rows, :] = normed["fq"].astype(bf)
        fk_ref[0, rows, :] = normed["fk"].astype(bf)
        eq_ref[0, rows, :] = normed["eq"].astype(bf)
        dq_ref[0, rows, :] = _rope(normed["dq"], st["tab_diff"], ROT_DIFF // 2).astype(bf)
        dk_ref[0, rows, :] = _rope(normed["dk"], st["tab_diff"], ROT_DIFF // 2).astype(bf)
        mq_ref[0, rows, :] = (st["mq"] * (LOG2E * HEAD_DIM ** -0.5)).astype(bf)
        mk_ref[0, rows, :] = st["mk"].astype(bf)
        for p in range(2):
            sl = slice(p * MXU_WIDTH, (p + 1) * MXU_WIDTH)
            aq_ref[0, rows, sl] = _rope(st["qa"][:, sl] * lax.rsqrt(st["ms_qa"][p] + EPS) * prow(P_GQ),
                                        st["tab_mla"], MLA_ROPE // 2).astype(bf)
            ak_ref[0, rows, sl] = _rope(st["ka"][:, sl] * lax.rsqrt(st["ms_ka"][p] + EPS) * prow(P_GK),
                                        st["tab_mla"], MLA_ROPE // 2).astype(bf)
        blk = lax.broadcasted_iota(jnp.int32, (nbp, pr), 0)
        past = blk < st["blk"]
        for h in range(N_HEADS):
            work = jnp.where(past, st["gates"][h], NEG_INF)
            sel = jnp.zeros((nbp, pr), jnp.bool_)
            for _ in range(MOBA_TOPK):
                mx = jnp.max(work, axis=0, keepdims=True)
                first = jnp.min(jnp.where(work == mx, blk, nbp), axis=0, keepdims=True)
                pick = blk == first
                sel = sel | pick
                work = jnp.where(pick, REMOVED, work)
            msel_ref[0, h * nbp:(h + 1) * nbp, rows] = jnp.where(sel & past, 0.0, NEG_INF)

    parts = [{"rows": slice(p * pr, (p + 1) * pr), "blk": j * n_parts + p} for p in range(n_parts)]
    for phase in (project, second, third, finish):
        for st in parts:
            phase(st)


def _const_spec(a, layer=None):
    if layer is None:
        n = a.ndim
        return pl.BlockSpec(a.shape, lambda *_: (0,) * n)
    n = a.ndim - 1
    return pl.BlockSpec((None,) + a.shape[1:], lambda *_: (layer,) + (0,) * n)


def _moba_blocks_padded(seq):
    return -(-(seq // MOBA_BLOCK) // SUBLANES) * SUBLANES


def _prep_call(layer, tm, x, tr, anorm, win, wuq, wukvk, wukvv, gpair, g64, g32, expand, tril, vsel, par):
    bsz, seq, d = x.shape
    assert seq % tm == 0 and tm % MOBA_BLOCK == 0 and tril.shape == (MOBA_BLOCK, MOBA_BLOCK)
    nbp = _moba_blocks_padded(seq)
    bf = jnp.bfloat16
    f32 = jnp.float32

    vt = -N_HEADS * V_ROWS
    widths = [(2 * MXU_WIDTH, bf), (2 * MXU_WIDTH, bf), (vt, bf),
              (GROUP_WIDTH, bf), (GROUP_WIDTH, bf), (vt, bf),
              (N_HEADS * LANES, f32), (-SUBLANES, f32),
              (GROUP_WIDTH, bf), (GROUP_WIDTH, bf), (vt, bf), (-N_HEADS * nbp, f32),
              (GROUP_WIDTH, bf), (GROUP_WIDTH, bf), (vt, bf),
              (GROUP_WIDTH, bf)]

    def tok(width):
        if width > 0:
            return pl.BlockSpec((1, tm, width), lambda b, j: (b, j, 0))
        return pl.BlockSpec((1, -width, tm), lambda b, j: (b, 0, j))

    def shape(width):
        return (bsz, seq, width) if width > 0 else (bsz, -width, seq)

    consts = [anorm, win, wuq, wukvk, wukvv, gpair, g64, g32, expand, tril, vsel, par]
    layered = [True, True, True, True, True, False, False, False, False, False, False, True]
    return pl.pallas_call(
        _prep_body,
        grid=(bsz, seq // tm),
        in_specs=[tok(d), tok(TR_WIDTH)] + [_const_spec(c, layer if ly else None) for c, ly in zip(consts, layered)],
        out_specs=[tok(w) for w, _ in widths],
        out_shape=[jax.ShapeDtypeStruct(shape(w), dt) for w, dt in widths],
        scratch_shapes=[pltpu.VMEM((nbp, GROUP_WIDTH), jnp.float32), pltpu.VMEM((1, LANES), jnp.float32)],
        compiler_params=pltpu.CompilerParams(dimension_semantics=("arbitrary", "arbitrary"),
                                             vmem_limit_bytes=VMEM_LIMIT_BYTES),
        name="prep",
    )(x, tr, *consts)


def _memkv_body(mem_ref, mnorm_ref, w_ref, g64_ref, gain_ref, vsel_ref, k_ref, v_ref):
    m = mem_ref[0]
    mn = m * lax.rsqrt(jnp.mean(m * m, axis=-1, keepdims=True) + EPS) * mnorm_ref[...]
    kv = _dot(mn.astype(jnp.bfloat16), w_ref[...])
    k = kv[:, 0:GROUP_WIDTH]
    k = k * lax.rsqrt(_group_mean_sq(k, g64_ref[...]) + EPS) * gain_ref[...]
    k_ref[0] = k.astype(jnp.bfloat16)
    v_ref[0] = _values_t(kv[:, GROUP_WIDTH:2 * GROUP_WIDTH].astype(jnp.bfloat16), vsel_ref[...])


def _memkv_call(layer, mem, mnorm, w, g64, gain, vsel):
    bsz, mlen, d = mem.shape
    k_shape, vt_shape = (bsz, mlen, GROUP_WIDTH), (bsz, N_HEADS * V_ROWS, mlen)
    return pl.pallas_call(
        _memkv_body,
        grid=(bsz,),
        in_specs=[pl.BlockSpec((1, mlen, d), lambda b: (b, 0, 0)), _const_spec(mnorm, layer), _const_spec(w, layer),
                  _const_spec(g64), _const_spec(gain, layer), _const_spec(vsel)],
        out_specs=[pl.BlockSpec((1,) + s[1:], lambda b: (b, 0, 0)) for s in (k_shape, vt_shape)],
        out_shape=[jax.ShapeDtypeStruct(s, jnp.bfloat16) for s in (k_shape, vt_shape)],
        compiler_params=pltpu.CompilerParams(dimension_semantics=("arbitrary",), vmem_limit_bytes=VMEM_LIMIT_BYTES),
        name="mem_kv",
    )(mem, mnorm, w, g64, gain, vsel)


Q_SUB = 2 * MXU_WIDTH


class _AttnCfg:
    def __init__(self, name, vheads, n_maps, causal, decay=False, select=False, diff=False, tile=1024, lookahead=8,
                 q_sub=Q_SUB):
        self.name = name
        self.vheads = vheads
        self.n_maps = n_maps
        self.causal = causal
        self.decay = decay
        self.select = select
        self.diff = diff
        self.tile = tile
        self.lookahead = lookahead
        self.q_sub = q_sub


_PLAIN_VHEADS = [(0, h * HEAD_DIM, (h + 1) * HEAD_DIM, 0, h) for h in range(N_HEADS)]
_CFG_MLA = _AttnCfg("attn_mla", [((h // 2) * MXU_WIDTH, (h % 2) * PAIR_STRIDE, (h % 2) * PAIR_STRIDE + MLA_QK, 0, h)
                                 for h in range(N_HEADS)], 1, True, q_sub=MXU_WIDTH, lookahead=16, tile=2048)
_CFG_FOX = _AttnCfg("attn_fox", _PLAIN_VHEADS, 1, True, decay=True, tile=2048, lookahead=8)
_CFG_MOBA = _AttnCfg("attn_moba", _PLAIN_VHEADS, 1, True, select=True, tile=2048, lookahead=10)
_CFG_DIFF = _AttnCfg("attn_diff", [(0, h * HEAD_DIM + c * DIFF_QK, h * HEAD_DIM + (c + 1) * DIFF_QK, c, h)
                                   for c in range(2) for h in range(N_HEADS)], 2, True, diff=True, q_sub=MXU_WIDTH,
                     lookahead=32)
_CFG_MEM = _AttnCfg("attn_mem", _PLAIN_VHEADS, 1, False, tile=2048)


ONES_ROW = HEAD_DIM
V_ROWS = HEAD_DIM + 16


def _np_value_select():
    sel = np.zeros((N_HEADS * V_ROWS, GROUP_WIDTH), np.float32)
    for h in range(N_HEADS):
        for d in range(HEAD_DIM):
            sel[h * V_ROWS + d, h * HEAD_DIM + d] = 1.0
    return sel


def _values_t(v, vsel):
    vt = _dot_nt(vsel, v)
    row = lax.broadcasted_iota(jnp.int32, vt.shape, 0)
    ones = row == ONES_ROW
    for h in range(1, N_HEADS):
        ones = ones | (row == h * V_ROWS + ONES_ROW)
    return jnp.where(ones, 1.0, vt).astype(jnp.bfloat16)


def _tile_lanes(x, width):
    return jnp.tile(x, (1, width // LANES)) if width != LANES else x


def _attn_body(cfg, qi_ref, kj_ref, *refs):
    refs = list(refs)
    q_ref, k_ref, vt_ref = refs[:3]
    pos = 3
    if cfg.decay:
        dq_ref, dk_ref = refs[pos:pos + 2]
        pos += 2
    if cfg.select:
        sel_ref = refs[pos]
        pos += 1
    if cfg.diff:
        g64_ref, gsub_ref, lam_ref = refs[pos:pos + 3]
        pos += 3
    o_ref, qm_s, m_s, acc_s = refs[pos:pos + 4]

    t = pl.program_id(1)
    i = qi_ref[t]
    j = kj_ref[t]
    tq = q_ref.shape[1]
    tk = k_ref.shape[1]

    @pl.when(j == 0)
    def _():
        for n, (off, lo, hi, _, _) in enumerate(cfg.vheads):
            qb = q_ref[0, :, off:off + MXU_WIDTH]
            qm_s[n] = jnp.where(_lane_mask(qb.shape, lo, hi), qb, jnp.zeros_like(qb))
        m_s[...] = jnp.full(m_s.shape, NEG_INF, jnp.float32)
        acc_s[...] = jnp.zeros_like(acc_s)

    def step(diag):
        qs = min(tq, cfg.q_sub)
        items = [(n, u) for n in range(len(cfg.vheads)) for u in range(tq // qs)]

        def n_keys(u):
            return (u + 1) * qs if diag else tk

        def scores(item):
            n, u = item
            off, _, _, _, h = cfg.vheads[n]
            nk = n_keys(u)
            cols = slice(u * qs, (u + 1) * qs)
            s = _dot_nt(k_ref[0, 0:nk, off:off + MXU_WIDTH], qm_s[n, cols, :])
            if cfg.decay:
                s = (dq_ref[0, h:h + 1, cols] - _tile_lanes(dk_ref[0, 0:nk, h * LANES:(h + 1) * LANES], qs)) + s
            if cfg.select:
                nbp = sel_ref.shape[1] // N_HEADS
                qpos = u * qs + lax.broadcasted_iota(jnp.int32, (1, qs), 1)
                parts = []
                for kb in range(nk // MOBA_BLOCK):
                    rows = s[kb * MOBA_BLOCK:(kb + 1) * MOBA_BLOCK, :]
                    if not (diag and kb == nk // MOBA_BLOCK - 1):
                        bias = sel_ref[0, pl.ds(h * nbp + j * (tk // MOBA_BLOCK) + kb, 1), cols]
                        if diag:
                            bias = jnp.where(qpos < (kb + 1) * MOBA_BLOCK, 0.0, bias)
                        rows = rows + bias
                    parts.append(rows)
                s = parts[0] if len(parts) == 1 else jnp.concatenate(parts, axis=0)
            if diag:
                key = lax.broadcasted_iota(jnp.int32, (nk, qs), 0)
                qry = u * qs + lax.broadcasted_iota(jnp.int32, (nk, qs), 1)
                s = jnp.where(key <= qry, s, NEG_INF)
            return s, jnp.max(s, axis=0, keepdims=True)

        raw = {it: scores(items[it]) for it in range(min(cfg.lookahead, len(items)))}
        for it, (n, u) in enumerate(items):
            h = cfg.vheads[n][4]
            nk = n_keys(u)
            cols = slice(u * qs, (u + 1) * qs)
            s, s_max = raw.pop(it)
            m_prev = m_s[n, :, cols]
            m_new = jnp.maximum(m_prev, s_max)
            alpha = jnp.exp2(m_prev - m_new)
            p = jnp.exp2(s - m_new)
            m_s[n, :, cols] = m_new
            acc_s[n, :, cols] = acc_s[n, :, cols] * alpha + _dot(vt_ref[0, h * V_ROWS:(h + 1) * V_ROWS, 0:nk],
                                                                 p.astype(jnp.bfloat16))
            if it + cfg.lookahead < len(items):
                raw[it + cfg.lookahead] = scores(items[it + cfg.lookahead])

    if cfg.causal:
        pl.when(j < i)(functools.partial(step, False))
        pl.when(j == i)(functools.partial(step, True))
        last = j == i
    else:
        step(False)
        last = j == 0

    @pl.when(last)
    def _():
        outs = []
        for c in range(cfg.n_maps):
            heads = []
            for h in range(N_HEADS):
                acc = acc_s[c * N_HEADS + h]
                heads.append(acc[0:HEAD_DIM, :] / acc[ONES_ROW:ONES_ROW + 1, :])
            outs.append(jnp.concatenate(heads, axis=0).T)
        if cfg.diff:
            o = outs[0] - lam_ref[0:1, :] * outs[1]
            o = o * lax.rsqrt(_group_mean_sq(o, g64_ref[...]) + EPS) * gsub_ref[...]
        else:
            o = outs[0]
        o_ref[0] = o.astype(o_ref.dtype)


def _attn_call(cfg, q, k, v, extras, tq, tk):
    bsz, seq, wq = q.shape
    sk = k.shape[1]
    nq = seq // tq
    assert seq % tq == 0 and sk % tk == 0
    if cfg.causal:
        assert tq == tk and sk == seq
        pairs = [(i, j) for i in range(nq) for j in range(i + 1)]
    else:
        assert sk == tk
        pairs = [(i, 0) for i in range(nq)]
    qi = jnp.asarray(np.array([p[0] for p in pairs], np.int32))
    kj = jnp.asarray(np.array([p[1] for p in pairs], np.int32))
    n_vh = len(cfg.vheads)

    in_specs = [pl.BlockSpec((1, tq, wq), lambda b, t, qi, kj: (b, qi[t], 0)),
                pl.BlockSpec((1, tk, wq), lambda b, t, qi, kj: (b, kj[t], 0)),
                pl.BlockSpec((1, N_HEADS * V_ROWS, tk), lambda b, t, qi, kj: (b, 0, kj[t]))]
    args = [q, k, v]
    if cfg.decay:
        dcol, drow = extras
        in_specs += [pl.BlockSpec((1, SUBLANES, tq), lambda b, t, qi, kj: (b, 0, qi[t])),
                     pl.BlockSpec((1, tk, N_HEADS * LANES), lambda b, t, qi, kj: (b, kj[t], 0))]
        args += [drow, dcol]
    if cfg.select:
        (sel,) = extras
        in_specs += [pl.BlockSpec((1, sel.shape[1], tq), lambda b, t, qi, kj: (b, 0, qi[t]))]
        args += [sel]
    if cfg.diff:
        layer, g64, gsub, lam_row = extras
        in_specs += [_const_spec(g64), _const_spec(gsub, layer), _const_spec(lam_row, layer)]
        args += [g64, gsub, lam_row]

    grid_spec = pltpu.PrefetchScalarGridSpec(
        num_scalar_prefetch=2,
        grid=(bsz, len(pairs)),
        in_specs=in_specs,
        out_specs=pl.BlockSpec((1, tq, GROUP_WIDTH), lambda b, t, qi, kj: (b, qi[t], 0)),
        scratch_shapes=[pltpu.VMEM((n_vh, tq, MXU_WIDTH), jnp.bfloat16),
                        pltpu.VMEM((n_vh, 1, tq), jnp.float32),
                        pltpu.VMEM((n_vh, V_ROWS, tq), jnp.float32)])
    return pl.pallas_call(
        functools.partial(_attn_body, cfg),
        grid_spec=grid_spec,
        out_shape=jax.ShapeDtypeStruct((bsz, seq, GROUP_WIDTH), jnp.bfloat16),
        compiler_params=pltpu.CompilerParams(dimension_semantics=("arbitrary", "arbitrary"),
                                             vmem_limit_bytes=VMEM_LIMIT_BYTES),
        name=cfg.name,
    )(qi, kj, *args)


def _ffn_body(nf, x_ref, oa_ref, ob_ref, oc_ref, od_ref, oe_ref, wo_ref, fnorm_ref, wg_ref, wu_ref, cw_ref, cb_ref,
              wd_ref, out_ref, xnew_s, xn_s, acc_s):
    i = pl.program_id(1)
    f = pl.program_id(2)
    tm = x_ref.shape[1]

    @pl.when(f == 0)
    def _():
        @pl.when(i == 0)
        def _():
            xn_s[0:TAIL_ROWS, :] = jnp.zeros((TAIL_ROWS, xn_s.shape[1]), xn_s.dtype)

        @pl.when(i > 0)
        def _():
            xn_s[0:TAIL_ROWS, :] = xn_s[tm:tm + TAIL_ROWS, :]

        mixed = jnp.concatenate([o_ref[0] for o_ref in (oa_ref, ob_ref, oc_ref, od_ref, oe_ref)], axis=1)
        xnew = x_ref[0] + _dot(mixed, wo_ref[...])
        xnew_s[...] = xnew
        xn = xnew * lax.rsqrt(jnp.mean(xnew * xnew, axis=-1, keepdims=True) + EPS) * fnorm_ref[...]
        xn_s[TAIL_ROWS:TAIL_ROWS + tm, :] = xn.astype(xn_s.dtype)

    def mlp_chunk():
        ge = _dot(xn_s[...], wg_ref[...])
        u = _dot(xn_s[TAIL_ROWS:TAIL_ROWS + tm, :], wu_ref[...])
        g0 = ge[TAIL_ROWS:TAIL_ROWS + tm, :]
        t1 = ge[TAIL_ROWS - 1:TAIL_ROWS, :]
        t2 = ge[TAIL_ROWS - 2:TAIL_ROWS - 1, :]
        row = lax.broadcasted_iota(jnp.int32, g0.shape, 0)
        g1 = jnp.where(row == 0, t1, pltpu.roll(g0, 1, 0))
        g2 = jnp.where(row == 0, t2, jnp.where(row == 1, t1, pltpu.roll(g0, 2, 0)))
        y = cb_ref[...] + cw_ref[0:1, :] * g2
        y = y + cw_ref[1:2, :] * g1
        y = y + cw_ref[2:3, :] * g0
        hmid = (y * (1.0 / (1.0 + jnp.exp(-y)))) * u
        return _dot(hmid.astype(jnp.bfloat16), wd_ref[...])

    @pl.when(f == 0)
    def _():
        acc_s[...] = mlp_chunk()

    if nf > 2:
        @pl.when((f > 0) & (f < nf - 1))
        def _():
            acc_s[...] += mlp_chunk()

    @pl.when(f == nf - 1)
    def _():
        out_ref[0] = xnew_s[...] + (acc_s[...] + mlp_chunk())


def _ffn_call(layer, x, outs, wo, fnorm, wg, wu, cw, cb, wd, tm, tf):
    bsz, seq, d = x.shape
    dff = wg.shape[2]
    n_tiles, nf = seq // tm, dff // tf
    assert seq % tm == 0 and dff % tf == 0 and nf >= 2
    tok = lambda w: pl.BlockSpec((1, tm, w), lambda b, i, f: (b, i, 0))
    return pl.pallas_call(
        functools.partial(_ffn_body, nf),
        grid=(bsz, n_tiles, nf),
        in_specs=[tok(d)] + [tok(GROUP_WIDTH)] * 5 + [
            _const_spec(wo, layer),
            _const_spec(fnorm, layer),
            pl.BlockSpec((None, d, tf), lambda b, i, f: (layer, 0, f)),
            pl.BlockSpec((None, d, tf), lambda b, i, f: (layer, 0, f)),
            pl.BlockSpec((None, SUBLANES, tf), lambda b, i, f: (layer, 0, f)),
            pl.BlockSpec((None, 1, tf), lambda b, i, f: (layer, 0, f)),
            pl.BlockSpec((None, tf, d), lambda b, i, f: (layer, f, 0))],
        out_specs=tok(d),
        out_shape=jax.ShapeDtypeStruct((bsz, seq, d), jnp.float32),
        scratch_shapes=[pltpu.VMEM((tm, d), jnp.float32), pltpu.VMEM((TAIL_ROWS + tm, d), jnp.bfloat16),
                        pltpu.VMEM((tm, d), jnp.float32)],
        compiler_params=pltpu.CompilerParams(dimension_semantics=("arbitrary", "arbitrary", "arbitrary"),
                                             vmem_limit_bytes=VMEM_LIMIT_BYTES),
        name="ffn",
    )(x, *outs, wo, fnorm, wg, wu, cw, cb, wd)


def _pad_rows(v, width=MXU_WIDTH):
    return jnp.pad(v.astype(jnp.float32), ((0, 0), (0, width - v.shape[1])))


def _tile_rows(g, reps):
    return jnp.tile(g.astype(jnp.float32), (1, reps))


def _pack_in_projection(w):
    idx = _np_in_index()
    pieces, start = [], 0
    while start < PK_END:
        stop = start + 1
        if idx[start] == _SRC_END:
            while stop < PK_END and idx[stop] == _SRC_END:
                stop += 1
            pieces.append(jnp.zeros(w.shape[:-1] + (stop - start,), w.dtype))
        else:
            while stop < PK_END and idx[stop] == idx[stop - 1] + 1:
                stop += 1
            pieces.append(w[..., int(idx[start]):int(idx[stop - 1]) + 1])
        start = stop
    return jnp.concatenate(pieces, axis=-1)


def _zero_col(w):
    return jnp.concatenate([w, jnp.zeros(w.shape[:-1] + (1,), w.dtype)], axis=-1)


def _rope_table(positions):
    pos = positions.astype(jnp.float32)[:, :, None]
    inv = [ROPE_THETA ** (-jnp.arange(0, rot, 2, dtype=jnp.float32) / rot) for rot in (MLA_ROPE, ROT_MOBA, ROT_DIFF)]
    inv = jnp.concatenate(inv + [jnp.zeros((N_FREQ - TR_ONE,), jnp.float32)])
    ang = pos * inv
    c, s = jnp.cos(ang), jnp.sin(ang)
    c_hi = c.astype(jnp.bfloat16)
    c_lo = (c - c_hi.astype(jnp.float32)).astype(jnp.bfloat16)
    s_hi = s.astype(jnp.bfloat16)
    s_lo = (s - s_hi.astype(jnp.float32)).astype(jnp.bfloat16)
    return jnp.concatenate([c_hi, c_lo, s_hi, s_lo], axis=-1)


def _pick_tile(n, pref):
    t = pref
    while n % t:
        t //= 2
    return t


def kernel(x, mem, positions, attn_norm, ffn_norm, mem_norm, w_in, mla_cq_norm, mla_ckv_norm, mla_w_uq, mla_w_ukv, mla_q_norm, mla_k_norm, fox_b_f, fox_q_norm, fox_k_norm, moba_q_norm, moba_k_norm, diff_lambda, diff_q_norm, diff_k_norm, diff_sub_norm, mem_w_kv, mem_q_norm, mem_k_norm, w_o, ffn_w_gate, ffn_w_up, ffn_conv_w, ffn_conv_b, ffn_w_down):
    bsz, seq, d = x.shape
    depth = w_in.shape[0]
    dff = ffn_w_gate.shape[2]
    bf = jnp.bfloat16
    f32 = jnp.float32

    uq_idx = _np_uq_index()
    ukvk_idx, ukvv_idx = _np_ukv_index()
    gpair = jnp.asarray(_np_group_matrix(_PAIR_GROUPS), bf)
    g64 = jnp.asarray(_np_group_matrix(_G64_GROUPS), bf)
    g32 = jnp.asarray(_np_group_matrix(_G32_GROUPS), bf)
    expand = jnp.asarray(_np_rope_expand_all(), bf)
    t_prep = max(_pick_tile(seq, PREP_TILE), MOBA_BLOCK)
    tril = jnp.asarray(np.tril(np.ones((MOBA_BLOCK, MOBA_BLOCK), np.float32)), bf)
    vsel = jnp.asarray(_np_value_select(), bf)
    tr = _rope_table(positions)

    tile = lambda cfg: max(_pick_tile(seq, cfg.tile), MOBA_BLOCK) if cfg.select else _pick_tile(seq, cfg.tile)
    t_ffn = _pick_tile(seq, 512)
    tf = dff // 2 if (dff // 2) % LANES == 0 else dff

    win = _pack_in_projection(w_in.astype(bf))
    wuq = jnp.take(_zero_col(mla_w_uq), uq_idx, axis=2)
    wuq = jnp.pad(wuq, ((0, 0), (0, MXU_WIDTH - MLA_Q_RANK), (0, 0))).astype(bf)
    wukv = _zero_col(mla_w_ukv)
    wukvk = jnp.take(wukv, ukvk_idx, axis=2).astype(bf)
    wukvv = jnp.take(wukv, ukvv_idx, axis=2).astype(bf)
    pair = lambda g: _pad_rows(_tile_rows(g, 2))
    rows = [jnp.zeros((depth, MXU_WIDTH), f32)] * P_ROWS
    rows[P_CQ] = _pad_rows(mla_cq_norm)
    rows[P_CKV] = _pad_rows(mla_ckv_norm)
    rows[P_GQ] = pair(mla_q_norm) * (LOG2E * MLA_QK ** -0.5)
    rows[P_GK] = pair(mla_k_norm)
    rows[P_FQ] = _tile_rows(fox_q_norm, N_HEADS) * (LOG2E * HEAD_DIM ** -0.5)
    rows[P_FK] = _tile_rows(fox_k_norm, N_HEADS)
    rows[P_FB] = _pad_rows(fox_b_f)
    rows[P_MQ] = _tile_rows(moba_q_norm, N_HEADS)
    rows[P_MK] = _tile_rows(moba_k_norm, N_HEADS)
    rows[P_DQ] = _tile_rows(diff_q_norm, 2 * N_HEADS) * (LOG2E * DIFF_QK ** -0.5)
    rows[P_DK] = _tile_rows(diff_k_norm, 2 * N_HEADS)
    rows[P_EQ] = _tile_rows(mem_q_norm, N_HEADS) * (LOG2E * HEAD_DIM ** -0.5)
    par = jnp.stack(rows, axis=1)
    anorm = attn_norm.astype(f32)[:, None, :]
    mnorm = mem_norm.astype(f32)[:, None, :]
    fnorm = ffn_norm.astype(f32)[:, None, :]
    wmem = mem_w_kv.astype(bf)
    mem_gain = _tile_rows(mem_k_norm, N_HEADS)[:, None, :]

    lam_init = jnp.asarray([0.8 - 0.6 * math.exp(-0.3 * l) for l in range(depth)], f32)
    lam_vec = diff_lambda.astype(f32)
    lam = (jnp.exp(jnp.sum(lam_vec[:, 0] * lam_vec[:, 1], axis=-1))
           - jnp.exp(jnp.sum(lam_vec[:, 2] * lam_vec[:, 3], axis=-1)) + lam_init)
    lam_row = jnp.broadcast_to(lam[:, None, None], (depth, 1, GROUP_WIDTH))
    gsub = (_tile_rows(diff_sub_norm, N_HEADS) * (1.0 - lam_init)[:, None])[:, None, :]

    wo = w_o.astype(bf)
    wg, wu, wd = ffn_w_gate.astype(bf), ffn_w_up.astype(bf), ffn_w_down.astype(bf)
    cw = jnp.pad(ffn_conv_w.astype(f32), ((0, 0), (0, SUBLANES - CONV_WIDTH), (0, 0)))
    cb = ffn_conv_b.astype(f32)[:, None, :]

    for l in range(depth):
        (aq, ak, av, fq, fk, fv, fdcol, fdrow, mq, mk, mv, msel, dq, dk, dv, eq) = _prep_call(
            l, t_prep, x, tr, anorm, win, wuq, wukvk, wukvv, gpair, g64, g32, expand, tril, vsel, par)
        ek, ev = _memkv_call(l, mem, mnorm, wmem, g64, mem_gain, vsel)

        o_a = _attn_call(_CFG_MLA, aq, ak, av, (), tile(_CFG_MLA), tile(_CFG_MLA))
        o_b = _attn_call(_CFG_FOX, fq, fk, fv, (fdcol, fdrow), tile(_CFG_FOX), tile(_CFG_FOX))
        o_c = _attn_call(_CFG_MOBA, mq, mk, mv, (msel,), tile(_CFG_MOBA), tile(_CFG_MOBA))
        o_d = _attn_call(_CFG_DIFF, dq, dk, dv, (l, g64, gsub, lam_row), tile(_CFG_DIFF), tile(_CFG_DIFF))
        o_e = _attn_call(_CFG_MEM, eq, ek, ev, (), tile(_CFG_MEM), mem.shape[1])

        x = _ffn_call(l, x, (o_a, o_b, o_c, o_d, o_e), wo, fnorm, wg, wu, cw, cb, wd, t_ffn, tf)
    return x
```

```python
import functools
import math

import numpy as np
import jax
import jax.numpy as jnp
from jax import lax
from jax.experimental import pallas as pl
from jax.experimental.pallas import tpu as pltpu

N_HEADS = 4
HEAD_DIM = 64
GROUP_WIDTH = N_HEADS * HEAD_DIM
MLA_Q_RANK = 192
MLA_KV_RANK = 128
MLA_NOPE = 64
MLA_ROPE = 32
MLA_QK = MLA_NOPE + MLA_ROPE
DIFF_QK = HEAD_DIM // 2
ROPE_THETA = 500000.0
ROT_MOBA = HEAD_DIM // 4
ROT_DIFF = DIFF_QK // 4
MOBA_BLOCK = 256
MOBA_TOPK = 3
CONV_WIDTH = 3
EPS = 1e-6
NEG_INF = -1e30
LOG2E = math.log2(math.e)
REMOVED = -3e38

LANES = 128
SUBLANES = 8
MXU_WIDTH = 256
TAIL_ROWS = 16
PREP_TILE = 512
VMEM_LIMIT_BYTES = 56 * 1024 * 1024

_SRC_CQ = 0
_SRC_CKV = _SRC_CQ + MLA_Q_RANK
_SRC_KR = _SRC_CKV + MLA_KV_RANK
_SRC_FOX = _SRC_KR + MLA_ROPE
_SRC_FOXF = _SRC_FOX + 3 * GROUP_WIDTH
_SRC_MOBA = _SRC_FOXF + N_HEADS
_SRC_DIFF = _SRC_MOBA + 3 * GROUP_WIDTH
_SRC_MEMQ = _SRC_DIFF + 3 * GROUP_WIDTH
_SRC_END = _SRC_MEMQ + GROUP_WIDTH

PK_CQ = 0
PK_CKV = 256
PK_KR = 384
PK_FQ, PK_FK, PK_FV = 896, 1152, 1408
PK_FF = 1664
PK_MQ, PK_MK, PK_MV = 1792, 2048, 2304
PK_DQ, PK_DK, PK_DV = 2560, 2816, 3072
PK_EQ = 3328
PK_END = 3584

PAIR_STRIDE = MLA_QK


def _pair_lane(h, d):
    return (h // 2) * MXU_WIDTH + (h % 2) * PAIR_STRIDE + d


N_FREQ = 32
FREQ_BASE_MLA = 0
FREQ_BASE_MOBA = MLA_ROPE // 2
FREQ_BASE_DIFF = FREQ_BASE_MOBA + ROT_MOBA // 2
TR_ONE = FREQ_BASE_DIFF + ROT_DIFF // 2
TR_WIDTH = 4 * N_FREQ
assert TR_ONE < N_FREQ and TR_WIDTH == LANES

(P_CQ, P_CKV, P_GQ, P_GK, P_FQ, P_FK, P_FB, P_MQ, P_MK, P_DQ, P_DK, P_EQ) = range(12)
P_ROWS = 16


def _np_in_index():
    idx = np.full((PK_END,), _SRC_END, np.int32)
    idx[PK_CQ:PK_CQ + MLA_Q_RANK] = np.arange(_SRC_CQ, _SRC_CQ + MLA_Q_RANK)
    idx[PK_CKV:PK_CKV + MLA_KV_RANK] = np.arange(_SRC_CKV, _SRC_CKV + MLA_KV_RANK)
    for h in range(N_HEADS):
        for d in range(MLA_ROPE):
            idx[PK_KR + _pair_lane(h, d)] = _SRC_KR + d
    idx[PK_FQ:PK_FQ + 3 * GROUP_WIDTH] = np.arange(_SRC_FOX, _SRC_FOX + 3 * GROUP_WIDTH)
    idx[PK_FF:PK_FF + N_HEADS] = np.arange(_SRC_FOXF, _SRC_FOXF + N_HEADS)
    idx[PK_MQ:PK_MQ + 3 * GROUP_WIDTH] = np.arange(_SRC_MOBA, _SRC_MOBA + 3 * GROUP_WIDTH)
    idx[PK_DQ:PK_DQ + 3 * GROUP_WIDTH] = np.arange(_SRC_DIFF, _SRC_DIFF + 3 * GROUP_WIDTH)
    idx[PK_EQ:PK_EQ + GROUP_WIDTH] = np.arange(_SRC_MEMQ, _SRC_MEMQ + GROUP_WIDTH)
    return idx


def _np_uq_index():
    idx = np.full((2 * MXU_WIDTH,), N_HEADS * MLA_QK, np.int32)
    for h in range(N_HEADS):
        for d in range(MLA_QK):
            idx[_pair_lane(h, d)] = h * MLA_QK + d
    return idx


def _np_ukv_index():
    zero = N_HEADS * (MLA_NOPE + HEAD_DIM)
    idx_k = np.full((2 * MXU_WIDTH,), zero, np.int32)
    idx_v = np.zeros((GROUP_WIDTH,), np.int32)
    for h in range(N_HEADS):
        for d in range(MLA_NOPE):
            idx_k[_pair_lane(h, MLA_ROPE + d)] = h * (MLA_NOPE + HEAD_DIM) + d
        for d in range(HEAD_DIM):
            idx_v[h * HEAD_DIM + d] = h * (MLA_NOPE + HEAD_DIM) + MLA_NOPE + d
    return idx_k, idx_v


def _np_group_matrix(groups):
    g = np.zeros((MXU_WIDTH, MXU_WIDTH), np.float32)
    for lo, size in groups:
        g[lo:lo + size, lo:lo + size] = 1.0 / size
    return g


_PAIR_GROUPS = [(0, MLA_ROPE), (MLA_ROPE, MLA_NOPE), (PAIR_STRIDE, MLA_ROPE), (PAIR_STRIDE + MLA_ROPE, MLA_NOPE)]
_G64_GROUPS = [(h * HEAD_DIM, HEAD_DIM) for h in range(N_HEADS)]
_G32_GROUPS = [(g * DIFF_QK, DIFF_QK) for g in range(2 * N_HEADS)]


def _np_rope_expand(regions, rot, base):
    half = rot // 2
    e = np.zeros((TR_WIDTH, 2 * MXU_WIDTH), np.float32)
    e[TR_ONE, 0:MXU_WIDTH] = 1.0
    for lo in regions:
        assert lo % rot == 0
        for r in range(half):
            f = base + r
            for lane, sign in ((lo + r, -1.0), (lo + half + r, 1.0)):
                e[TR_ONE, lane] = 0.0
                e[f, lane] = 1.0
                e[N_FREQ + f, lane] = 1.0
                e[2 * N_FREQ + f, MXU_WIDTH + lane] = sign
                e[3 * N_FREQ + f, MXU_WIDTH + lane] = sign
    return e


def _np_rope_expand_all():
    return np.concatenate([
        _np_rope_expand([0, PAIR_STRIDE], MLA_ROPE, FREQ_BASE_MLA),
        _np_rope_expand([h * HEAD_DIM for h in range(N_HEADS)], ROT_MOBA, FREQ_BASE_MOBA),
        _np_rope_expand([g * DIFF_QK for g in range(2 * N_HEADS)], ROT_DIFF, FREQ_BASE_DIFF),
    ], axis=1)


def _dot(a, b):
    return jnp.dot(a, b, preferred_element_type=jnp.float32)


def _dot_nt(a, b):
    return lax.dot_general(a, b, (((1,), (1,)), ((), ())), preferred_element_type=jnp.float32)


def _split2(a):
    hi = a.astype(jnp.bfloat16)
    lo = (a - hi.astype(jnp.float32)).astype(jnp.bfloat16)
    return hi, lo


def _split3(a):
    hi = a.astype(jnp.bfloat16)
    r = a - hi.astype(jnp.float32)
    mid = r.astype(jnp.bfloat16)
    lo = (r - mid.astype(jnp.float32)).astype(jnp.bfloat16)
    return hi, mid, lo


def _group_mean_sq(a, g_bf16):
    return _dot((a * a).astype(jnp.bfloat16), g_bf16)


def _rope(x, tabs, half):
    w = x.shape[-1]
    lane = lax.broadcasted_iota(jnp.int32, x.shape, 1)
    partner = jnp.where((lane & (2 * half - 1)) >= half, pltpu.roll(x, half, 1), pltpu.roll(x, w - half, 1))
    return x * tabs[:, 0:w] + partner * tabs[:, w:2 * w]


def _lane_mask(shape, lo, hi):
    lane = lax.broadcasted_iota(jnp.int32, shape, len(shape) - 1)
    return (lane >= lo) & (lane < hi)


def _prep_body(x_ref, tr_ref, anorm_ref, win_ref, wuq_ref, wukvk_ref, wukvv_ref, gpair_ref, g64_ref, g32_ref,
               exp_ref, tril_ref, vsel_ref, par_ref,
               aq_ref, ak_ref, av_ref, fq_ref, fk_ref, fv_ref, fdcol_ref, fdrow_ref, mq_ref, mk_ref, mv_ref, msel_ref,
               dq_ref, dk_ref, dv_ref, eq_ref,
               kmean_s, carry_s):
    j = pl.program_id(1)
    tm = x_ref.shape[1]
    pr = MOBA_BLOCK
    n_parts = tm // pr
    nbp = kmean_s.shape[0]
    bf = jnp.bfloat16
    gpair, g64, g32 = gpair_ref[...], g64_ref[...], g32_ref[...]
    vsel = vsel_ref[...]
    tril = tril_ref[...]

    @pl.when(j == 0)
    def _():
        kmean_s[...] = jnp.zeros_like(kmean_s)
        carry_s[...] = jnp.zeros_like(carry_s)

    def prow(r, width=MXU_WIDTH):
        return par_ref[r:r + 1, 0:width]

    def project(st):
        rows = st["rows"]
        x = x_ref[0, rows, :]
        xb = (x * lax.rsqrt(jnp.mean(x * x, axis=-1, keepdims=True) + EPS) * anorm_ref[...]).astype(bf)
        tabs = _dot(tr_ref[0, rows, :], exp_ref[...])
        st["tab_mla"] = tabs[:, 0:2 * MXU_WIDTH]
        st["tab_moba"] = tabs[:, 2 * MXU_WIDTH:4 * MXU_WIDTH]
        st["tab_diff"] = tabs[:, 4 * MXU_WIDTH:6 * MXU_WIDTH]

        def run(lo, hi):
            wide = _dot(xb, win_ref[:, lo:hi])
            return lambda off, width: wide[:, off - lo:off - lo + width]

        run_m = run(PK_MQ, PK_DQ)
        run_a = run(PK_CQ, PK_FQ)
        run_f = run(PK_FQ, PK_MQ)
        run_d = run(PK_DQ, PK_END)
        st["direct"] = {"mq": run_m(PK_MQ, MXU_WIDTH), "mk": run_m(PK_MK, MXU_WIDTH),
                        "fq": run_f(PK_FQ, MXU_WIDTH), "fk": run_f(PK_FK, MXU_WIDTH),
                        "dq": run_d(PK_DQ, MXU_WIDTH), "dk": run_d(PK_DK, MXU_WIDTH),
                        "eq": run_d(PK_EQ, MXU_WIDTH)}
        st["values"] = {"fv": run_f(PK_FV, MXU_WIDTH).astype(bf), "mv": run_m(PK_MV, MXU_WIDTH).astype(bf),
                        "dv": run_d(PK_DV, MXU_WIDTH).astype(bf)}
        p_cq, p_ckv, st["p_kr"] = run_a(PK_CQ, MXU_WIDTH), run_a(PK_CKV, MLA_KV_RANK), run_a(PK_KR, 2 * MXU_WIDTH)
        cqn = p_cq * lax.rsqrt(jnp.sum(p_cq * p_cq, axis=-1, keepdims=True) * (1.0 / MLA_Q_RANK) + EPS) * prow(P_CQ)
        ckvn = p_ckv * lax.rsqrt(jnp.mean(p_ckv * p_ckv, axis=-1, keepdims=True) + EPS) * prow(P_CKV, MLA_KV_RANK)
        st["cqb"], st["ckvb"] = cqn.astype(bf), ckvn.astype(bf)
        z = run_f(PK_FF, LANES) + prow(P_FB, LANES)
        log_f = jnp.minimum(z, 0.0) - jnp.log1p(jnp.exp(-jnp.abs(z)))
        st["log_f"] = _split3(jnp.where(_lane_mask(log_f.shape, 0, N_HEADS), log_f, 0.0))

    def second(st):
        gmat = {"fq": g64, "fk": g64, "mq": g64, "mk": g64, "dq": g32, "dk": g32, "eq": g64}
        gains = {"fq": P_FQ, "fk": P_FK, "mq": P_MQ, "mk": P_MK, "dq": P_DQ, "dk": P_DK, "eq": P_EQ}
        ms = {name: _group_mean_sq(a, gmat[name]) for name, a in st["direct"].items()}
        st["qa"] = _dot(st["cqb"], wuq_ref[...])
        st["ka"] = st["p_kr"] + _dot(st["ckvb"], wukvk_ref[...])
        st["values"]["av"] = _dot(st["ckvb"], wukvv_ref[...]).astype(bf)
        l1, l2, l3 = st["log_f"]
        st["cumsum"] = (_dot(tril, l1) + _dot(tril, l2)) + _dot(tril, l3)
        st["normed"] = {name: a * lax.rsqrt(ms[name] + EPS) * prow(gains[name]) for name, a in st["direct"].items()}

    def third(st):
        rows = st["rows"]
        mq = _rope(st["normed"]["mq"], st["tab_moba"], ROT_MOBA // 2)
        mk = _rope(st["normed"]["mk"], st["tab_moba"], ROT_MOBA // 2)
        st["mq"], st["mk"] = mq, mk
        kmean_s[pl.ds(st["blk"], 1), :] = jnp.mean(mk, axis=0, keepdims=True)
        km_hi, km_lo = _split2(kmean_s[...])
        st["gates"] = []
        for h in range(N_HEADS):
            q_hi, q_lo = _split2(jnp.where(_lane_mask(mq.shape, h * HEAD_DIM, (h + 1) * HEAD_DIM), mq, 0.0))
            st["gates"].append((_dot_nt(km_hi, q_hi) + _dot_nt(km_lo, q_hi)) + _dot_nt(km_hi, q_lo))
        st["ms_qa"] = [_group_mean_sq(st["qa"][:, p * MXU_WIDTH:(p + 1) * MXU_WIDTH], gpair) for p in range(2)]
        st["ms_ka"] = [_group_mean_sq(st["ka"][:, p * MXU_WIDTH:(p + 1) * MXU_WIDTH], gpair) for p in range(2)]
        dec = carry_s[...] + st["cumsum"]
        carry_s[...] = dec[pr - 1:pr, :]
        dec2 = dec * LOG2E
        d1, d2, d3 = _split3(dec2)
        for h in range(N_HEADS):
            fdcol_ref[0, rows, h * LANES:(h + 1) * LANES] = jnp.broadcast_to(dec2[:, h:h + 1], (pr, LANES))
        row_sel = jnp.where(lax.broadcasted_iota(jnp.int32, (SUBLANES, LANES), 0)
                            == lax.broadcasted_iota(jnp.int32, (SUBLANES, LANES), 1), 1.0, 0.0).astype(bf)
        fdrow_ref[0, :, rows] = (_dot_nt(row_sel, d1) + _dot_nt(row_sel, d2)) + _dot_nt(row_sel, d3)
        for name, ref in (("fv", fv_ref), ("mv", mv_ref), ("dv", dv_ref), ("av", av_ref)):
            ref[0, :, rows] = _values_t(st["values"][name], vsel)

    def finish(st):
        rows = st["rows"]
        normed = st["normed"]
        fq_ref[0, rows, :] = normed["fq"].astype(bf)
        fk_ref[0, rows, :] = normed["fk"].astype(bf)
        eq_ref[0, rows, :] = normed["eq"].astype(bf)
        dq_ref[0, rows, :] = _rope(normed["dq"], st["tab_diff"], ROT_DIFF // 2).astype(bf)
        dk_ref[0, rows, :] = _rope(normed["dk"], st["tab_diff"], ROT_DIFF // 2).astype(bf)
        mq_ref[0, rows, :] = (st["mq"] * (LOG2E * HEAD_DIM ** -0.5)).astype(bf)
        mk_ref[0, rows, :] = st["mk"].astype(bf)
        for p in range(2):
            sl = slice(p * MXU_WIDTH, (p + 1) * MXU_WIDTH)
            aq_ref[0, rows, sl] = _rope(st["qa"][:, sl] * lax.rsqrt(st["ms_qa"][p] + EPS) * prow(P_GQ),
                                        st["tab_mla"], MLA_ROPE // 2).astype(bf)
            ak_ref[0, rows, sl] = _rope(st["ka"][:, sl] * lax.rsqrt(st["ms_ka"][p] + EPS) * prow(P_GK),
                                        st["tab_mla"], MLA_ROPE // 2).astype(bf)
        blk = lax.broadcasted_iota(jnp.int32, (nbp, pr), 0)
        past = blk < st["blk"]
        for h in range(N_HEADS):
            work = jnp.where(past, st["gates"][h], NEG_INF)
            sel = jnp.zeros((nbp, pr), jnp.bool_)
            for _ in range(MOBA_TOPK):
                mx = jnp.max(work, axis=0, keepdims=True)
                first = jnp.min(jnp.where(work == mx, blk, nbp), axis=0, keepdims=True)
                pick = blk == first
                sel = sel | pick
                work = jnp.where(pick, REMOVED, work)
            msel_ref[0, h * nbp:(h + 1) * nbp, rows] = jnp.where(sel & past, 0.0, NEG_INF)

    parts = [{"rows": slice(p * pr, (p + 1) * pr), "blk": j * n_parts + p} for p in range(n_parts)]
    for phase in (project, second, third, finish):
        for st in parts:
            phase(st)


def _const_spec(a, layer=None):
    if layer is None:
        n = a.ndim
        return pl.BlockSpec(a.shape, lambda *_: (0,) * n)
    n = a.ndim - 1
    return pl.BlockSpec((None,) + a.shape[1:], lambda *_: (layer,) + (0,) * n)


def _moba_blocks_padded(seq):
    return -(-(seq // MOBA_BLOCK) // SUBLANES) * SUBLANES


def _prep_call(layer, tm, x, tr, anorm, win, wuq, wukvk, wukvv, gpair, g64, g32, expand, tril, vsel, par):
    bsz, seq, d = x.shape
    assert seq % tm == 0 and tm % MOBA_BLOCK == 0 and tril.shape == (MOBA_BLOCK, MOBA_BLOCK)
    nbp = _moba_blocks_padded(seq)
    bf = jnp.bfloat16
    f32 = jnp.float32

    vt = -N_HEADS * V_ROWS
    widths = [(2 * MXU_WIDTH, bf), (2 * MXU_WIDTH, bf), (vt, bf),
              (GROUP_WIDTH, bf), (GROUP_WIDTH, bf), (vt, bf),
              (N_HEADS * LANES, f32), (-SUBLANES, f32),
              (GROUP_WIDTH, bf), (GROUP_WIDTH, bf), (vt, bf), (-N_HEADS * nbp, f32),
              (GROUP_WIDTH, bf), (GROUP_WIDTH, bf), (vt, bf),
              (GROUP_WIDTH, bf)]

    def tok(width):
        if width > 0:
            return pl.BlockSpec((1, tm, width), lambda b, j: (b, j, 0))
        return pl.BlockSpec((1, -width, tm), lambda b, j: (b, 0, j))

    def shape(width):
        return (bsz, seq, width) if width > 0 else (bsz, -width, seq)

    consts = [anorm, win, wuq, wukvk, wukvv, gpair, g64, g32, expand, tril, vsel, par]
    layered = [True, True, True, True, True, False, False, False, False, False, False, True]
    return pl.pallas_call(
        _prep_body,
        grid=(bsz, seq // tm),
        in_specs=[tok(d), tok(TR_WIDTH)] + [_const_spec(c, layer if ly else None) for c, ly in zip(consts, layered)],
        out_specs=[tok(w) for w, _ in widths],
        out_shape=[jax.ShapeDtypeStruct(shape(w), dt) for w, dt in widths],
        scratch_shapes=[pltpu.VMEM((nbp, GROUP_WIDTH), jnp.float32), pltpu.VMEM((1, LANES), jnp.float32)],
        compiler_params=pltpu.CompilerParams(dimension_semantics=("arbitrary", "arbitrary"),
                                             vmem_limit_bytes=VMEM_LIMIT_BYTES),
        name="prep",
    )(x, tr, *consts)


def _memkv_body(mem_ref, mnorm_ref, w_ref, g64_ref, gain_ref, vsel_ref, k_ref, v_ref):
    m = mem_ref[0]
    mn = m * lax.rsqrt(jnp.mean(m * m, axis=-1, keepdims=True) + EPS) * mnorm_ref[...]
    kv = _dot(mn.astype(jnp.bfloat16), w_ref[...])
    k = kv[:, 0:GROUP_WIDTH]
    k = k * lax.rsqrt(_group_mean_sq(k, g64_ref[...]) + EPS) * gain_ref[...]
    k_ref[0] = k.astype(jnp.bfloat16)
    v_ref[0] = _values_t(kv[:, GROUP_WIDTH:2 * GROUP_WIDTH].astype(jnp.bfloat16), vsel_ref[...])


def _memkv_call(layer, mem, mnorm, w, g64, gain, vsel):
    bsz, mlen, d = mem.shape
    k_shape, vt_shape = (bsz, mlen, GROUP_WIDTH), (bsz, N_HEADS * V_ROWS, mlen)
    return pl.pallas_call(
        _memkv_body,
        grid=(bsz,),
        in_specs=[pl.BlockSpec((1, mlen, d), lambda b: (b, 0, 0)), _const_spec(mnorm, layer), _const_spec(w, layer),
                  _const_spec(g64), _const_spec(gain, layer), _const_spec(vsel)],
        out_specs=[pl.BlockSpec((1,) + s[1:], lambda b: (b, 0, 0)) for s in (k_shape, vt_shape)],
        out_shape=[jax.ShapeDtypeStruct(s, jnp.bfloat16) for s in (k_shape, vt_shape)],
        compiler_params=pltpu.CompilerParams(dimension_semantics=("arbitrary",), vmem_limit_bytes=VMEM_LIMIT_BYTES),
        name="mem_kv",
    )(mem, mnorm, w, g64, gain, vsel)


Q_SUB = 2 * MXU_WIDTH


class _AttnCfg:
    def __init__(self, name, vheads, n_maps, causal, decay=False, select=False, diff=False, tile=1024, lookahead=8,
                 q_sub=Q_SUB):
        self.name = name
        self.vheads = vheads
        self.n_maps = n_maps
        self.causal = causal
        self.decay = decay
        self.select = select
        self.diff = diff
        self.tile = tile
        self.lookahead = lookahead
        self.q_sub = q_sub


_PLAIN_VHEADS = [(0, h * HEAD_DIM, (h + 1) * HEAD_DIM, 0, h) for h in range(N_HEADS)]
_CFG_MLA = _AttnCfg("attn_mla", [((h // 2) * MXU_WIDTH, (h % 2) * PAIR_STRIDE, (h % 2) * PAIR_STRIDE + MLA_QK, 0, h)
                                 for h in range(N_HEADS)], 1, True, q_sub=MXU_WIDTH, lookahead=16, tile=2048)
_CFG_FOX = _AttnCfg("attn_fox", _PLAIN_VHEADS, 1, True, decay=True, tile=2048, lookahead=8)
_CFG_MOBA = _AttnCfg("attn_moba", _PLAIN_VHEADS, 1, True, select=True, tile=2048, lookahead=8)
_CFG_DIFF = _AttnCfg("attn_diff", [(0, h * HEAD_DIM + c * DIFF_QK, h * HEAD_DIM + (c + 1) * DIFF_QK, c, h)
                                   for c in range(2) for h in range(N_HEADS)], 2, True, diff=True, q_sub=MXU_WIDTH,
                     lookahead=32)
_CFG_MEM = _AttnCfg("attn_mem", _PLAIN_VHEADS, 1, False, tile=2048)


ONES_ROW = HEAD_DIM
V_ROWS = HEAD_DIM + 16


def _np_value_select():
    sel = np.zeros((N_HEADS * V_ROWS, GROUP_WIDTH), np.float32)
    for h in range(N_HEADS):
        for d in range(HEAD_DIM):
            sel[h * V_ROWS + d, h * HEAD_DIM + d] = 1.0
    return sel


def _values_t(v, vsel):
    vt = _dot_nt(vsel, v)
    row = lax.broadcasted_iota(jnp.int32, vt.shape, 0)
    ones = row == ONES_ROW
    for h in range(1, N_HEADS):
        ones = ones | (row == h * V_ROWS + ONES_ROW)
    return jnp.where(ones, 1.0, vt).astype(jnp.bfloat16)


def _tile_lanes(x, width):
    return jnp.tile(x, (1, width // LANES)) if width != LANES else x


def _attn_body(cfg, qi_ref, kj_ref, *refs):
    refs = list(refs)
    q_ref, k_ref, vt_ref = refs[:3]
    pos = 3
    if cfg.decay:
        dq_ref, dk_ref = refs[pos:pos + 2]
        pos += 2
    if cfg.select:
        sel_ref = refs[pos]
        pos += 1
    if cfg.diff:
        g64_ref, gsub_ref, lam_ref = refs[pos:pos + 3]
        pos += 3
    o_ref, qm_s, m_s, acc_s = refs[pos:pos + 4]

    t = pl.program_id(1)
    i = qi_ref[t]
    j = kj_ref[t]
    tq = q_ref.shape[1]
    tk = k_ref.shape[1]

    @pl.when(j == 0)
    def _():
        for n, (off, lo, hi, _, _) in enumerate(cfg.vheads):
            qb = q_ref[0, :, off:off + MXU_WIDTH]
            qm_s[n] = jnp.where(_lane_mask(qb.shape, lo, hi), qb, jnp.zeros_like(qb))
        m_s[...] = jnp.full(m_s.shape, NEG_INF, jnp.float32)
        acc_s[...] = jnp.zeros_like(acc_s)

    def step(diag):
        qs = min(tq, cfg.q_sub)
        items = [(n, u) for n in range(len(cfg.vheads)) for u in range(tq // qs)]

        def n_keys(u):
            return (u + 1) * qs if diag else tk

        def scores(item):
            n, u = item
            off, _, _, _, h = cfg.vheads[n]
            nk = n_keys(u)
            cols = slice(u * qs, (u + 1) * qs)
            s = _dot_nt(k_ref[0, 0:nk, off:off + MXU_WIDTH], qm_s[n, cols, :])
            if cfg.decay:
                s = (dq_ref[0, h:h + 1, cols] - _tile_lanes(dk_ref[0, 0:nk, h * LANES:(h + 1) * LANES], qs)) + s
            if cfg.select:
                nbp = sel_ref.shape[1] // N_HEADS
                qpos = u * qs + lax.broadcasted_iota(jnp.int32, (1, qs), 1)
                parts = []
                for kb in range(nk // MOBA_BLOCK):
                    rows = s[kb * MOBA_BLOCK:(kb + 1) * MOBA_BLOCK, :]
                    if not (diag and kb == nk // MOBA_BLOCK - 1):
                        bias = sel_ref[0, pl.ds(h * nbp + j * (tk // MOBA_BLOCK) + kb, 1), cols]
                        if diag:
                            bias = jnp.where(qpos < (kb + 1) * MOBA_BLOCK, 0.0, bias)
                        rows = rows + bias
                    parts.append(rows)
                s = parts[0] if len(parts) == 1 else jnp.concatenate(parts, axis=0)
            if diag:
                key = lax.broadcasted_iota(jnp.int32, (nk, qs), 0)
                qry = u * qs + lax.broadcasted_iota(jnp.int32, (nk, qs), 1)
                s = jnp.where(key <= qry, s, NEG_INF)
            return s, jnp.max(s, axis=0, keepdims=True)

        raw = {it: scores(items[it]) for it in range(min(cfg.lookahead, len(items)))}
        for it, (n, u) in enumerate(items):
            h = cfg.vheads[n][4]
            nk = n_keys(u)
            cols = slice(u * qs, (u + 1) * qs)
            s, s_max = raw.pop(it)
            m_prev = m_s[n, :, cols]
            m_new = jnp.maximum(m_prev, s_max)
            alpha = jnp.exp2(m_prev - m_new)
            p = jnp.exp2(s - m_new)
            m_s[n, :, cols] = m_new
            acc_s[n, :, cols] = acc_s[n, :, cols] * alpha + _dot(vt_ref[0, h * V_ROWS:(h + 1) * V_ROWS, 0:nk],
                                                                 p.astype(jnp.bfloat16))
            if it + cfg.lookahead < len(items):
                raw[it + cfg.lookahead] = scores(items[it + cfg.lookahead])

    if cfg.causal:
        pl.when(j < i)(functools.partial(step, False))
        pl.when(j == i)(functools.partial(step, True))
        last = j == i
    else:
        step(False)
        last = j == 0

    @pl.when(last)
    def _():
        outs = []
        for c in range(cfg.n_maps):
            heads = []
            for h in range(N_HEADS):
                acc = acc_s[c * N_HEADS + h]
                heads.append(acc[0:HEAD_DIM, :] / acc[ONES_ROW:ONES_ROW + 1, :])
            outs.append(jnp.concatenate(heads, axis=0).T)
        if cfg.diff:
            o = outs[0] - lam_ref[0:1, :] * outs[1]
            o = o * lax.rsqrt(_group_mean_sq(o, g64_ref[...]) + EPS) * gsub_ref[...]
        else:
            o = outs[0]
        o_ref[0] = o.astype(o_ref.dtype)


def _attn_call(cfg, q, k, v, extras, tq, tk):
    bsz, seq, wq = q.shape
    sk = k.shape[1]
    nq = seq // tq
    assert seq % tq == 0 and sk % tk == 0
    if cfg.causal:
        assert tq == tk and sk == seq
        pairs = [(i, j) for i in range(nq) for j in range(i + 1)]
    else:
        assert sk == tk
        pairs = [(i, 0) for i in range(nq)]
    qi = jnp.asarray(np.array([p[0] for p in pairs], np.int32))
    kj = jnp.asarray(np.array([p[1] for p in pairs], np.int32))
    n_vh = len(cfg.vheads)

    in_specs = [pl.BlockSpec((1, tq, wq), lambda b, t, qi, kj: (b, qi[t], 0)),
                pl.BlockSpec((1, tk, wq), lambda b, t, qi, kj: (b, kj[t], 0)),
                pl.BlockSpec((1, N_HEADS * V_ROWS, tk), lambda b, t, qi, kj: (b, 0, kj[t]))]
    args = [q, k, v]
    if cfg.decay:
        dcol, drow = extras
        in_specs += [pl.BlockSpec((1, SUBLANES, tq), lambda b, t, qi, kj: (b, 0, qi[t])),
                     pl.BlockSpec((1, tk, N_HEADS * LANES), lambda b, t, qi, kj: (b, kj[t], 0))]
        args += [drow, dcol]
    if cfg.select:
        (sel,) = extras
        in_specs += [pl.BlockSpec((1, sel.shape[1], tq), lambda b, t, qi, kj: (b, 0, qi[t]))]
        args += [sel]
    if cfg.diff:
        layer, g64, gsub, lam_row = extras
        in_specs += [_const_spec(g64), _const_spec(gsub, layer), _const_spec(lam_row, layer)]
        args += [g64, gsub, lam_row]

    grid_spec = pltpu.PrefetchScalarGridSpec(
        num_scalar_prefetch=2,
        grid=(bsz, len(pairs)),
        in_specs=in_specs,
        out_specs=pl.BlockSpec((1, tq, GROUP_WIDTH), lambda b, t, qi, kj: (b, qi[t], 0)),
        scratch_shapes=[pltpu.VMEM((n_vh, tq, MXU_WIDTH), jnp.bfloat16),
                        pltpu.VMEM((n_vh, 1, tq), jnp.float32),
                        pltpu.VMEM((n_vh, V_ROWS, tq), jnp.float32)])
    return pl.pallas_call(
        functools.partial(_attn_body, cfg),
        grid_spec=grid_spec,
        out_shape=jax.ShapeDtypeStruct((bsz, seq, GROUP_WIDTH), jnp.bfloat16),
        compiler_params=pltpu.CompilerParams(dimension_semantics=("arbitrary", "arbitrary"),
                                             vmem_limit_bytes=VMEM_LIMIT_BYTES),
        name=cfg.name,
    )(qi, kj, *args)


def _ffn_body(nf, x_ref, oa_ref, ob_ref, oc_ref, od_ref, oe_ref, wo_ref, fnorm_ref, wg_ref, wu_ref, cw_ref, cb_ref,
              wd_ref, out_ref, xnew_s, xn_s, acc_s):
    i = pl.program_id(1)
    f = pl.program_id(2)
    tm = x_ref.shape[1]

    @pl.when(f == 0)
    def _():
        @pl.when(i == 0)
        def _():
            xn_s[0:TAIL_ROWS, :] = jnp.zeros((TAIL_ROWS, xn_s.shape[1]), xn_s.dtype)

        @pl.when(i > 0)
        def _():
            xn_s[0:TAIL_ROWS, :] = xn_s[tm:tm + TAIL_ROWS, :]

        mixed = jnp.concatenate([o_ref[0] for o_ref in (oa_ref, ob_ref, oc_ref, od_ref, oe_ref)], axis=1)
        xnew = x_ref[0] + _dot(mixed, wo_ref[...])
        xnew_s[...] = xnew
        xn = xnew * lax.rsqrt(jnp.mean(xnew * xnew, axis=-1, keepdims=True) + EPS) * fnorm_ref[...]
        xn_s[TAIL_ROWS:TAIL_ROWS + tm, :] = xn.astype(xn_s.dtype)

    def mlp_chunk():
        ge = _dot(xn_s[...], wg_ref[...])
        u = _dot(xn_s[TAIL_ROWS:TAIL_ROWS + tm, :], wu_ref[...])
        g0 = ge[TAIL_ROWS:TAIL_ROWS + tm, :]
        t1 = ge[TAIL_ROWS - 1:TAIL_ROWS, :]
        t2 = ge[TAIL_ROWS - 2:TAIL_ROWS - 1, :]
        row = lax.broadcasted_iota(jnp.int32, g0.shape, 0)
        g1 = jnp.where(row == 0, t1, pltpu.roll(g0, 1, 0))
        g2 = jnp.where(row == 0, t2, jnp.where(row == 1, t1, pltpu.roll(g0, 2, 0)))
        y = cb_ref[...] + cw_ref[0:1, :] * g2
        y = y + cw_ref[1:2, :] * g1
        y = y + cw_ref[2:3, :] * g0
        hmid = (y * (1.0 / (1.0 + jnp.exp(-y)))) * u
        return _dot(hmid.astype(jnp.bfloat16), wd_ref[...])

    @pl.when(f == 0)
    def _():
        acc_s[...] = mlp_chunk()

    if nf > 2:
        @pl.when((f > 0) & (f < nf - 1))
        def _():
            acc_s[...] += mlp_chunk()

    @pl.when(f == nf - 1)
    def _():
        out_ref[0] = xnew_s[...] + (acc_s[...] + mlp_chunk())


def _ffn_call(layer, x, outs, wo, fnorm, wg, wu, cw, cb, wd, tm, tf):
    bsz, seq, d = x.shape
    dff = wg.shape[2]
    n_tiles, nf = seq // tm, dff // tf
    assert seq % tm == 0 and dff % tf == 0 and nf >= 2
    tok = lambda w: pl.BlockSpec((1, tm, w), lambda b, i, f: (b, i, 0))
    return pl.pallas_call(
        functools.partial(_ffn_body, nf),
        grid=(bsz, n_tiles, nf),
        in_specs=[tok(d)] + [tok(GROUP_WIDTH)] * 5 + [
            _const_spec(wo, layer),
            _const_spec(fnorm, layer),
            pl.BlockSpec((None, d, tf), lambda b, i, f: (layer, 0, f)),
            pl.BlockSpec((None, d, tf), lambda b, i, f: (layer, 0, f)),
            pl.BlockSpec((None, SUBLANES, tf), lambda b, i, f: (layer, 0, f)),
            pl.BlockSpec((None, 1, tf), lambda b, i, f: (layer, 0, f)),
            pl.BlockSpec((None, tf, d), lambda b, i, f: (layer, f, 0))],
        out_specs=tok(d),
        out_shape=jax.ShapeDtypeStruct((bsz, seq, d), jnp.float32),
        scratch_shapes=[pltpu.VMEM((tm, d), jnp.float32), pltpu.VMEM((TAIL_ROWS + tm, d), jnp.bfloat16),
                        pltpu.VMEM((tm, d), jnp.float32)],
        compiler_params=pltpu.CompilerParams(dimension_semantics=("arbitrary", "arbitrary", "arbitrary"),
                                             vmem_limit_bytes=VMEM_LIMIT_BYTES),
        name="ffn",
    )(x, *outs, wo, fnorm, wg, wu, cw, cb, wd)


def _pad_rows(v, width=MXU_WIDTH):
    return jnp.pad(v.astype(jnp.float32), ((0, 0), (0, width - v.shape[1])))


def _tile_rows(g, reps):
    return jnp.tile(g.astype(jnp.float32), (1, reps))


def _pack_in_projection(w):
    idx = _np_in_index()
    pieces, start = [], 0
    while start < PK_END:
        stop = start + 1
        if idx[start] == _SRC_END:
            while stop < PK_END and idx[stop] == _SRC_END:
                stop += 1
            pieces.append(jnp.zeros(w.shape[:-1] + (stop - start,), w.dtype))
        else:
            while stop < PK_END and idx[stop] == idx[stop - 1] + 1:
                stop += 1
            pieces.append(w[..., int(idx[start]):int(idx[stop - 1]) + 1])
        start = stop
    return jnp.concatenate(pieces, axis=-1)


def _zero_col(w):
    return jnp.concatenate([w, jnp.zeros(w.shape[:-1] + (1,), w.dtype)], axis=-1)


def _rope_table(positions):
    pos = positions.astype(jnp.float32)[:, :, None]
    inv = [ROPE_THETA ** (-jnp.arange(0, rot, 2, dtype=jnp.float32) / rot) for rot in (MLA_ROPE, ROT_MOBA, ROT_DIFF)]
    inv = jnp.concatenate(inv + [jnp.zeros((N_FREQ - TR_ONE,), jnp.float32)])
    ang = pos * inv
    c, s = jnp.cos(ang), jnp.sin(ang)
    c_hi = c.astype(jnp.bfloat16)
    c_lo = (c - c_hi.astype(jnp.float32)).astype(jnp.bfloat16)
    s_hi = s.astype(jnp.bfloat16)
    s_lo = (s - s_hi.astype(jnp.float32)).astype(jnp.bfloat16)
    return jnp.concatenate([c_hi, c_lo, s_hi, s_lo], axis=-1)


def _pick_tile(n, pref):
    t = pref
    while n % t:
        t //= 2
    return t


def kernel(x, mem, positions, attn_norm, ffn_norm, mem_norm, w_in, mla_cq_norm, mla_ckv_norm, mla_w_uq, mla_w_ukv, mla_q_norm, mla_k_norm, fox_b_f, fox_q_norm, fox_k_norm, moba_q_norm, moba_k_norm, diff_lambda, diff_q_norm, diff_k_norm, diff_sub_norm, mem_w_kv, mem_q_norm, mem_k_norm, w_o, ffn_w_gate, ffn_w_up, ffn_conv_w, ffn_conv_b, ffn_w_down):
    bsz, seq, d = x.shape
    depth = w_in.shape[0]
    dff = ffn_w_gate.shape[2]
    bf = jnp.bfloat16
    f32 = jnp.float32

    uq_idx = _np_uq_index()
    ukvk_idx, ukvv_idx = _np_ukv_index()
    gpair = jnp.asarray(_np_group_matrix(_PAIR_GROUPS), bf)
    g64 = jnp.asarray(_np_group_matrix(_G64_GROUPS), bf)
    g32 = jnp.asarray(_np_group_matrix(_G32_GROUPS), bf)
    expand = jnp.asarray(_np_rope_expand_all(), bf)
    t_prep = max(_pick_tile(seq, PREP_TILE), MOBA_BLOCK)
    tril = jnp.asarray(np.tril(np.ones((MOBA_BLOCK, MOBA_BLOCK), np.float32)), bf)
    vsel = jnp.asarray(_np_value_select(), bf)
    tr = _rope_table(positions)

    tile = lambda cfg: max(_pick_tile(seq, cfg.tile), MOBA_BLOCK) if cfg.select else _pick_tile(seq, cfg.tile)
    t_ffn = _pick_tile(seq, 512)
    tf = dff // 2 if (dff // 2) % LANES == 0 else dff

    win = _pack_in_projection(w_in.astype(bf))
    wuq = jnp.take(_zero_col(mla_w_uq), uq_idx, axis=2)
    wuq = jnp.pad(wuq, ((0, 0), (0, MXU_WIDTH - MLA_Q_RANK), (0, 0))).astype(bf)
    wukv = _zero_col(mla_w_ukv)
    wukvk = jnp.take(wukv, ukvk_idx, axis=2).astype(bf)
    wukvv = jnp.take(wukv, ukvv_idx, axis=2).astype(bf)
    pair = lambda g: _pad_rows(_tile_rows(g, 2))
    rows = [jnp.zeros((depth, MXU_WIDTH), f32)] * P_ROWS
    rows[P_CQ] = _pad_rows(mla_cq_norm)
    rows[P_CKV] = _pad_rows(mla_ckv_norm)
    rows[P_GQ] = pair(mla_q_norm) * (LOG2E * MLA_QK ** -0.5)
    rows[P_GK] = pair(mla_k_norm)
    rows[P_FQ] = _tile_rows(fox_q_norm, N_HEADS) * (LOG2E * HEAD_DIM ** -0.5)
    rows[P_FK] = _tile_rows(fox_k_norm, N_HEADS)
    rows[P_FB] = _pad_rows(fox_b_f)
    rows[P_MQ] = _tile_rows(moba_q_norm, N_HEADS)
    rows[P_MK] = _tile_rows(moba_k_norm, N_HEADS)
    rows[P_DQ] = _tile_rows(diff_q_norm, 2 * N_HEADS) * (LOG2E * DIFF_QK ** -0.5)
    rows[P_DK] = _tile_rows(diff_k_norm, 2 * N_HEADS)
    rows[P_EQ] = _tile_rows(mem_q_norm, N_HEADS) * (LOG2E * HEAD_DIM ** -0.5)
    par = jnp.stack(rows, axis=1)
    anorm = attn_norm.astype(f32)[:, None, :]
    mnorm = mem_norm.astype(f32)[:, None, :]
    fnorm = ffn_norm.astype(f32)[:, None, :]
    wmem = mem_w_kv.astype(bf)
    mem_gain = _tile_rows(mem_k_norm, N_HEADS)[:, None, :]

    lam_init = jnp.asarray([0.8 - 0.6 * math.exp(-0.3 * l) for l in range(depth)], f32)
    lam_vec = diff_lambda.astype(f32)
    lam = (jnp.exp(jnp.sum(lam_vec[:, 0] * lam_vec[:, 1], axis=-1))
           - jnp.exp(jnp.sum(lam_vec[:, 2] * lam_vec[:, 3], axis=-1)) + lam_init)
    lam_row = jnp.broadcast_to(lam[:, None, None], (depth, 1, GROUP_WIDTH))
    gsub = (_tile_rows(diff_sub_norm, N_HEADS) * (1.0 - lam_init)[:, None])[:, None, :]

    wo = w_o.astype(bf)
    wg, wu, wd = ffn_w_gate.astype(bf), ffn_w_up.astype(bf), ffn_w_down.astype(bf)
    cw = jnp.pad(ffn_conv_w.astype(f32), ((0, 0), (0, SUBLANES - CONV_WIDTH), (0, 0)))
    cb = ffn_conv_b.astype(f32)[:, None, :]

    for l in range(depth):
        (aq, ak, av, fq, fk, fv, fdcol, fdrow, mq, mk, mv, msel, dq, dk, dv, eq) = _prep_call(
            l, t_prep, x, tr, anorm, win, wuq, wukvk, wukvv, gpair, g64, g32, expand, tril, vsel, par)
        ek, ev = _memkv_call(l, mem, mnorm, wmem, g64, mem_gain, vsel)

        o_a = _attn_call(_CFG_MLA, aq, ak, av, (), tile(_CFG_MLA), tile(_CFG_MLA))
        o_b = _attn_call(_CFG_FOX, fq, fk, fv, (fdcol, fdrow), tile(_CFG_FOX), tile(_CFG_FOX))
        o_c = _attn_call(_CFG_MOBA, mq, mk, mv, (msel,), tile(_CFG_MOBA), tile(_CFG_MOBA))
        o_d = _attn_call(_CFG_DIFF, dq, dk, dv, (l, g64, gsub, lam_row), tile(_CFG_DIFF), tile(_CFG_DIFF))
        o_e = _attn_call(_CFG_MEM, eq, ek, ev, (), tile(_CFG_MEM), mem.shape[1])

        x = _ffn_call(l, x, (o_a, o_b, o_c, o_d, o_e), wo, fnorm, wg, wu, cw, cb, wd, t_ffn, tf)
    return x
```

```python
import functools
import math

import numpy as np
import jax
import jax.numpy as jnp
from jax import lax
from jax.experimental import pallas as pl
from jax.experimental.pallas import tpu as pltpu

N_HEADS = 4
HEAD_DIM = 64
GROUP_WIDTH = N_HEADS * HEAD_DIM
MLA_Q_RANK = 192
MLA_KV_RANK = 128
MLA_NOPE = 64
MLA_ROPE = 32
MLA_QK = MLA_NOPE + MLA_ROPE
DIFF_QK = HEAD_DIM // 2
ROPE_THETA = 500000.0
ROT_MOBA = HEAD_DIM // 4
ROT_DIFF = DIFF_QK // 4
MOBA_BLOCK = 256
MOBA_TOPK = 3
CONV_WIDTH = 3
EPS = 1e-6
NEG_INF = -1e30
LOG2E = math.log2(math.e)
REMOVED = -3e38

LANES = 128
SUBLANES = 8
MXU_WIDTH = 256
TAIL_ROWS = 16
PREP_TILE = 512
VMEM_LIMIT_BYTES = 56 * 1024 * 1024

_SRC_CQ = 0
_SRC_CKV = _SRC_CQ + MLA_Q_RANK
_SRC_KR = _SRC_CKV + MLA_KV_RANK
_SRC_FOX = _SRC_KR + MLA_ROPE
_SRC_FOXF = _SRC_FOX + 3 * GROUP_WIDTH
_SRC_MOBA = _SRC_FOXF + N_HEADS
_SRC_DIFF = _SRC_MOBA + 3 * GROUP_WIDTH
_SRC_MEMQ = _SRC_DIFF + 3 * GROUP_WIDTH
_SRC_END = _SRC_MEMQ + GROUP_WIDTH

PK_CQ = 0
PK_CKV = 256
PK_KR = 384
PK_FQ, PK_FK, PK_FV = 896, 1152, 1408
PK_FF = 1664
PK_MQ, PK_MK, PK_MV = 1792, 2048, 2304
PK_DQ, PK_DK, PK_DV = 2560, 2816, 3072
PK_EQ = 3328
PK_END = 3584

PAIR_STRIDE = MLA_QK


def _pair_lane(h, d):
    return (h // 2) * MXU_WIDTH + (h % 2) * PAIR_STRIDE + d


N_FREQ = 32
FREQ_BASE_MLA = 0
FREQ_BASE_MOBA = MLA_ROPE // 2
FREQ_BASE_DIFF = FREQ_BASE_MOBA + ROT_MOBA // 2
TR_ONE = FREQ_BASE_DIFF + ROT_DIFF // 2
TR_WIDTH = 4 * N_FREQ
assert TR_ONE < N_FREQ and TR_WIDTH == LANES

(P_CQ, P_CKV, P_GQ, P_GK, P_FQ, P_FK, P_FB, P_MQ, P_MK, P_DQ, P_DK, P_EQ) = range(12)
P_ROWS = 16


def _np_in_index():
    idx = np.full((PK_END,), _SRC_END, np.int32)
    idx[PK_CQ:PK_CQ + MLA_Q_RANK] = np.arange(_SRC_CQ, _SRC_CQ + MLA_Q_RANK)
    idx[PK_CKV:PK_CKV + MLA_KV_RANK] = np.arange(_SRC_CKV, _SRC_CKV + MLA_KV_RANK)
    for h in range(N_HEADS):
        for d in range(MLA_ROPE):
            idx[PK_KR + _pair_lane(h, d)] = _SRC_KR + d
    idx[PK_FQ:PK_FQ + 3 * GROUP_WIDTH] = np.arange(_SRC_FOX, _SRC_FOX + 3 * GROUP_WIDTH)
    idx[PK_FF:PK_FF + N_HEADS] = np.arange(_SRC_FOXF, _SRC_FOXF + N_HEADS)
    idx[PK_MQ:PK_MQ + 3 * GROUP_WIDTH] = np.arange(_SRC_MOBA, _SRC_MOBA + 3 * GROUP_WIDTH)
    idx[PK_DQ:PK_DQ + 3 * GROUP_WIDTH] = np.arange(_SRC_DIFF, _SRC_DIFF + 3 * GROUP_WIDTH)
    idx[PK_EQ:PK_EQ + GROUP_WIDTH] = np.arange(_SRC_MEMQ, _SRC_MEMQ + GROUP_WIDTH)
    return idx


def _np_uq_index():
    idx = np.full((2 * MXU_WIDTH,), N_HEADS * MLA_QK, np.int32)
    for h in range(N_HEADS):
        for d in range(MLA_QK):
            idx[_pair_lane(h, d)] = h * MLA_QK + d
    return idx


def _np_ukv_index():
    zero = N_HEADS * (MLA_NOPE + HEAD_DIM)
    idx_k = np.full((2 * MXU_WIDTH,), zero, np.int32)
    idx_v = np.zeros((GROUP_WIDTH,), np.int32)
    for h in range(N_HEADS):
        for d in range(MLA_NOPE):
            idx_k[_pair_lane(h, MLA_ROPE + d)] = h * (MLA_NOPE + HEAD_DIM) + d
        for d in range(HEAD_DIM):
            idx_v[h * HEAD_DIM + d] = h * (MLA_NOPE + HEAD_DIM) + MLA_NOPE + d
    return idx_k, idx_v


def _np_group_matrix(groups):
    g = np.zeros((MXU_WIDTH, MXU_WIDTH), np.float32)
    for lo, size in groups:
        g[lo:lo + size, lo:lo + size] = 1.0 / size
    return g


_PAIR_GROUPS = [(0, MLA_ROPE), (MLA_ROPE, MLA_NOPE), (PAIR_STRIDE, MLA_ROPE), (PAIR_STRIDE + MLA_ROPE, MLA_NOPE)]
_G64_GROUPS = [(h * HEAD_DIM, HEAD_DIM) for h in range(N_HEADS)]
_G32_GROUPS = [(g * DIFF_QK, DIFF_QK) for g in range(2 * N_HEADS)]


def _np_rope_expand(regions, rot, base):
    half = rot // 2
    e = np.zeros((TR_WIDTH, 2 * MXU_WIDTH), np.float32)
    e[TR_ONE, 0:MXU_WIDTH] = 1.0
    for lo in regions:
        assert lo % rot == 0
        for r in range(half):
            f = base + r
            for lane, sign in ((lo + r, -1.0), (lo + half + r, 1.0)):
                e[TR_ONE, lane] = 0.0
                e[f, lane] = 1.0
                e[N_FREQ + f, lane] = 1.0
                e[2 * N_FREQ + f, MXU_WIDTH + lane] = sign
                e[3 * N_FREQ + f, MXU_WIDTH + lane] = sign
    return e


def _np_rope_expand_all():
    return np.concatenate([
        _np_rope_expand([0, PAIR_STRIDE], MLA_ROPE, FREQ_BASE_MLA),
        _np_rope_expand([h * HEAD_DIM for h in range(N_HEADS)], ROT_MOBA, FREQ_BASE_MOBA),
        _np_rope_expand([g * DIFF_QK for g in range(2 * N_HEADS)], ROT_DIFF, FREQ_BASE_DIFF),
    ], axis=1)


def _dot(a, b):
    return jnp.dot(a, b, preferred_element_type=jnp.float32)


def _dot_nt(a, b):
    return lax.dot_general(a, b, (((1,), (1,)), ((), ())), preferred_element_type=jnp.float32)


def _split2(a):
    hi = a.astype(jnp.bfloat16)
    lo = (a - hi.astype(jnp.float32)).astype(jnp.bfloat16)
    return hi, lo


def _split3(a):
    hi = a.astype(jnp.bfloat16)
    r = a - hi.astype(jnp.float32)
    mid = r.astype(jnp.bfloat16)
    lo = (r - mid.astype(jnp.float32)).astype(jnp.bfloat16)
    return hi, mid, lo


def _group_mean_sq(a, g_bf16):
    return _dot((a * a).astype(jnp.bfloat16), g_bf16)


def _rope(x, tabs, half):
    w = x.shape[-1]
    lane = lax.broadcasted_iota(jnp.int32, x.shape, 1)
    partner = jnp.where((lane & (2 * half - 1)) >= half, pltpu.roll(x, half, 1), pltpu.roll(x, w - half, 1))
    return x * tabs[:, 0:w] + partner * tabs[:, w:2 * w]


def _lane_mask(shape, lo, hi):
    lane = lax.broadcasted_iota(jnp.int32, shape, len(shape) - 1)
    return (lane >= lo) & (lane < hi)


def _prep_body(x_ref, tr_ref, anorm_ref, win_ref, wuq_ref, wukvk_ref, wukvv_ref, gpair_ref, g64_ref, g32_ref,
               exp_ref, tril_ref, vsel_ref, par_ref,
               aq_ref, ak_ref, av_ref, fq_ref, fk_ref, fv_ref, fdcol_ref, fdrow_ref, mq_ref, mk_ref, mv_ref, msel_ref,
               dq_ref, dk_ref, dv_ref, eq_ref,
               kmean_s, carry_s):
    j = pl.program_id(1)
    tm = x_ref.shape[1]
    pr = MOBA_BLOCK
    n_parts = tm // pr
    nbp = kmean_s.shape[0]
    bf = jnp.bfloat16
    gpair, g64, g32 = gpair_ref[...], g64_ref[...], g32_ref[...]
    vsel = vsel_ref[...]
    tril = tril_ref[...]

    @pl.when(j == 0)
    def _():
        kmean_s[...] = jnp.zeros_like(kmean_s)
        carry_s[...] = jnp.zeros_like(carry_s)

    def prow(r, width=MXU_WIDTH):
        return par_ref[r:r + 1, 0:width]

    def project(st):
        rows = st["rows"]
        x = x_ref[0, rows, :]
        xb = (x * lax.rsqrt(jnp.mean(x * x, axis=-1, keepdims=True) + EPS) * anorm_ref[...]).astype(bf)
        tabs = _dot(tr_ref[0, rows, :], exp_ref[...])
        st["tab_mla"] = tabs[:, 0:2 * MXU_WIDTH]
        st["tab_moba"] = tabs[:, 2 * MXU_WIDTH:4 * MXU_WIDTH]
        st["tab_diff"] = tabs[:, 4 * MXU_WIDTH:6 * MXU_WIDTH]

        def run(lo, hi):
            wide = _dot(xb, win_ref[:, lo:hi])
            return lambda off, width: wide[:, off - lo:off - lo + width]

        run_m = run(PK_MQ, PK_DQ)
        run_a = run(PK_CQ, PK_FQ)
        run_f = run(PK_FQ, PK_MQ)
        run_d = run(PK_DQ, PK_END)
        st["direct"] = {"mq": run_m(PK_MQ, MXU_WIDTH), "mk": run_m(PK_MK, MXU_WIDTH),
                        "fq": run_f(PK_FQ, MXU_WIDTH), "fk": run_f(PK_FK, MXU_WIDTH),
                        "dq": run_d(PK_DQ, MXU_WIDTH), "dk": run_d(PK_DK, MXU_WIDTH),
                        "eq": run_d(PK_EQ, MXU_WIDTH)}
        st["values"] = {"fv": run_f(PK_FV, MXU_WIDTH).astype(bf), "mv": run_m(PK_MV, MXU_WIDTH).astype(bf),
                        "dv": run_d(PK_DV, MXU_WIDTH).astype(bf)}
        p_cq, p_ckv, st["p_kr"] = run_a(PK_CQ, MXU_WIDTH), run_a(PK_CKV, MLA_KV_RANK), run_a(PK_KR, 2 * MXU_WIDTH)
        cqn = p_cq * lax.rsqrt(jnp.sum(p_cq * p_cq, axis=-1, keepdims=True) * (1.0 / MLA_Q_RANK) + EPS) * prow(P_CQ)
        ckvn = p_ckv * lax.rsqrt(jnp.mean(p_ckv * p_ckv, axis=-1, keepdims=True) + EPS) * prow(P_CKV, MLA_KV_RANK)
        st["cqb"], st["ckvb"] = cqn.astype(bf), ckvn.astype(bf)
        z = run_f(PK_FF, LANES) + prow(P_FB, LANES)
        log_f = jnp.minimum(z, 0.0) - jnp.log1p(jnp.exp(-jnp.abs(z)))
        st["log_f"] = _split3(jnp.where(_lane_mask(log_f.shape, 0, N_HEADS), log_f, 0.0))

    def second(st):
        gmat = {"fq": g64, "fk": g64, "mq": g64, "mk": g64, "dq": g32, "dk": g32, "eq": g64}
        gains = {"fq": P_FQ, "fk": P_FK, "mq": P_MQ, "mk": P_MK, "dq": P_DQ, "dk": P_DK, "eq": P_EQ}
        ms = {name: _group_mean_sq(a, gmat[name]) for name, a in st["direct"].items()}
        st["qa"] = _dot(st["cqb"], wuq_ref[...])
        st["ka"] = st["p_kr"] + _dot(st["ckvb"], wukvk_ref[...])
        st["values"]["av"] = _dot(st["ckvb"], wukvv_ref[...]).astype(bf)
        l1, l2, l3 = st["log_f"]
        st["cumsum"] = (_dot(tril, l1) + _dot(tril, l2)) + _dot(tril, l3)
        st["normed"] = {name: a * lax.rsqrt(ms[name] + EPS) * prow(gains[name]) for name, a in st["direct"].items()}

    def third(st):
        rows = st["rows"]
        mq = _rope(st["normed"]["mq"], st["tab_moba"], ROT_MOBA // 2)
        mk = _rope(st["normed"]["mk"], st["tab_moba"], ROT_MOBA // 2)
        st["mq"], st["mk"] = mq, mk
        kmean_s[pl.ds(st["blk"], 1), :] = jnp.mean(mk, axis=0, keepdims=True)
        km_hi, km_lo = _split2(kmean_s[...])
        st["gates"] = []
        for h in range(N_HEADS):
            q_hi, q_lo = _split2(jnp.where(_lane_mask(mq.shape, h * HEAD_DIM, (h + 1) * HEAD_DIM), mq, 0.0))
            st["gates"].append((_dot_nt(km_hi, q_hi) + _dot_nt(km_lo, q_hi)) + _dot_nt(km_hi, q_lo))
        st["ms_qa"] = [_group_mean_sq(st["qa"][:, p * MXU_WIDTH:(p + 1) * MXU_WIDTH], gpair) for p in range(2)]
        st["ms_ka"] = [_group_mean_sq(st["ka"][:, p * MXU_WIDTH:(p + 1) * MXU_WIDTH], gpair) for p in range(2)]
        dec = carry_s[...] + st["cumsum"]
        carry_s[...] = dec[pr - 1:pr, :]
        dec2 = dec * LOG2E
        d1, d2, d3 = _split3(dec2)
        for h in range(N_HEADS):
            fdcol_ref[0, rows, h * LANES:(h + 1) * LANES] = jnp.broadcast_to(dec2[:, h:h + 1], (pr, LANES))
        row_sel = jnp.where(lax.broadcasted_iota(jnp.int32, (SUBLANES, LANES), 0)
                            == lax.broadcasted_iota(jnp.int32, (SUBLANES, LANES), 1), 1.0, 0.0).astype(bf)
        fdrow_ref[0, :, rows] = (_dot_nt(row_sel, d1) + _dot_nt(row_sel, d2)) + _dot_nt(row_sel, d3)
        for name, ref in (("fv", fv_ref), ("mv", mv_ref), ("dv", dv_ref), ("av", av_ref)):
            ref[0, :, rows] = _values_t(st["values"][name], vsel)

    def finish(st):
        rows = st["rows"]
        normed = st["normed"]
        fq_ref[0, rows, :] = normed["fq"].astype(bf)
        fk_ref[0, rows, :] = normed["fk"].astype(bf)
        eq_ref[0, rows, :] = normed["eq"].astype(bf)
        dq_ref[0, rows, :] = _rope(normed["dq"], st["tab_diff"], ROT_DIFF // 2).astype(bf)
        dk_ref[0, rows, :] = _rope(normed["dk"], st["tab_diff"], ROT_DIFF // 2).astype(bf)
        mq_ref[0, rows, :] = (st["mq"] * (LOG2E * HEAD_DIM ** -0.5)).astype(bf)
        mk_ref[0, rows, :] = st["mk"].astype(bf)
        for p in range(2):
            sl = slice(p * MXU_WIDTH, (p + 1) * MXU_WIDTH)
            aq_ref[0, rows, sl] = _rope(st["qa"][:, sl] * lax.rsqrt(st["ms_qa"][p] + EPS) * prow(P_GQ),
                                        st["tab_mla"], MLA_ROPE // 2).astype(bf)
            ak_ref[0, rows, sl] = _rope(st["ka"][:, sl] * lax.rsqrt(st["ms_ka"][p] + EPS) * prow(P_GK),
                                        st["tab_mla"], MLA_ROPE // 2).astype(bf)
        blk = lax.broadcasted_iota(jnp.int32, (nbp, pr), 0)
        past = blk < st["blk"]
        for h in range(N_HEADS):
            work = jnp.where(past, st["gates"][h], NEG_INF)
            sel = jnp.zeros((nbp, pr), jnp.bool_)
            for _ in range(MOBA_TOPK):
                mx = jnp.max(work, axis=0, keepdims=True)
                first = jnp.min(jnp.where(work == mx, blk, nbp), axis=0, keepdims=True)
                pick = blk == first
                sel = sel | pick
                work = jnp.where(pick, REMOVED, work)
            msel_ref[0, h * nbp:(h + 1) * nbp, rows] = jnp.where(sel & past, 0.0, NEG_INF)

    parts = [{"rows": slice(p * pr, (p + 1) * pr), "blk": j * n_parts + p} for p in range(n_parts)]
    for phase in (project, second, third, finish):
        for st in parts:
            phase(st)


def _const_spec(a, layer=None):
    if layer is None:
        n = a.ndim
        return pl.BlockSpec(a.shape, lambda *_: (0,) * n)
    n = a.ndim - 1
    return pl.BlockSpec((None,) + a.shape[1:], lambda *_: (layer,) + (0,) * n)


def _moba_blocks_padded(seq):
    return -(-(seq // MOBA_BLOCK) // SUBLANES) * SUBLANES


def _prep_call(layer, tm, x, tr, anorm, win, wuq, wukvk, wukvv, gpair, g64, g32, expand, tril, vsel, par):
    bsz, seq, d = x.shape
    assert seq % tm == 0 and tm % MOBA_BLOCK == 0 and tril.shape == (MOBA_BLOCK, MOBA_BLOCK)
    nbp = _moba_blocks_padded(seq)
    bf = jnp.bfloat16
    f32 = jnp.float32

    vt = -N_HEADS * V_ROWS
    widths = [(2 * MXU_WIDTH, bf), (2 * MXU_WIDTH, bf), (vt, bf),
              (GROUP_WIDTH, bf), (GROUP_WIDTH, bf), (vt, bf),
              (N_HEADS * LANES, f32), (-SUBLANES, f32),
              (GROUP_WIDTH, bf), (GROUP_WIDTH, bf), (vt, bf), (-N_HEADS * nbp, f32),
              (GROUP_WIDTH, bf), (GROUP_WIDTH, bf), (vt, bf),
              (GROUP_WIDTH, bf)]

    def tok(width):
        if width > 0:
            return pl.BlockSpec((1, tm, width), lambda b, j: (b, j, 0))
        return pl.BlockSpec((1, -width, tm), lambda b, j: (b, 0, j))

    def shape(width):
        return (bsz, seq, width) if width > 0 else (bsz, -width, seq)

    consts = [anorm, win, wuq, wukvk, wukvv, gpair, g64, g32, expand, tril, vsel, par]
    layered = [True, True, True, True, True, False, False, False, False, False, False, True]
    return pl.pallas_call(
        _prep_body,
        grid=(bsz, seq // tm),
        in_specs=[tok(d), tok(TR_WIDTH)] + [_const_spec(c, layer if ly else None) for c, ly in zip(consts, layered)],
        out_specs=[tok(w) for w, _ in widths],
        out_shape=[jax.ShapeDtypeStruct(shape(w), dt) for w, dt in widths],
        scratch_shapes=[pltpu.VMEM((nbp, GROUP_WIDTH), jnp.float32), pltpu.VMEM((1, LANES), jnp.float32)],
        compiler_params=pltpu.CompilerParams(dimension_semantics=("arbitrary", "arbitrary"),
                                             vmem_limit_bytes=VMEM_LIMIT_BYTES),
        name="prep",
    )(x, tr, *consts)


def _memkv_body(mem_ref, mnorm_ref, w_ref, g64_ref, gain_ref, vsel_ref, k_ref, v_ref):
    m = mem_ref[0]
    mn = m * lax.rsqrt(jnp.mean(m * m, axis=-1, keepdims=True) + EPS) * mnorm_ref[...]
    kv = _dot(mn.astype(jnp.bfloat16), w_ref[...])
    k = kv[:, 0:GROUP_WIDTH]
    k = k * lax.rsqrt(_group_mean_sq(k, g64_ref[...]) + EPS) * gain_ref[...]
    k_ref[0] = k.astype(jnp.bfloat16)
    v_ref[0] = _values_t(kv[:, GROUP_WIDTH:2 * GROUP_WIDTH].astype(jnp.bfloat16), vsel_ref[...])


def _memkv_call(layer, mem, mnorm, w, g64, gain, vsel):
    bsz, mlen, d = mem.shape
    k_shape, vt_shape = (bsz, mlen, GROUP_WIDTH), (bsz, N_HEADS * V_ROWS, mlen)
    return pl.pallas_call(
        _memkv_body,
        grid=(bsz,),
        in_specs=[pl.BlockSpec((1, mlen, d), lambda b: (b, 0, 0)), _const_spec(mnorm, layer), _const_spec(w, layer),
                  _const_spec(g64), _const_spec(gain, layer), _const_spec(vsel)],
        out_specs=[pl.BlockSpec((1,) + s[1:], lambda b: (b, 0, 0)) for s in (k_shape, vt_shape)],
        out_shape=[jax.ShapeDtypeStruct(s, jnp.bfloat16) for s in (k_shape, vt_shape)],
        compiler_params=pltpu.CompilerParams(dimension_semantics=("arbitrary",), vmem_limit_bytes=VMEM_LIMIT_BYTES),
        name="mem_kv",
    )(mem, mnorm, w, g64, gain, vsel)


Q_SUB = 2 * MXU_WIDTH


class _AttnCfg:
    def __init__(self, name, vheads, n_maps, causal, decay=False, select=False, diff=False, tile=1024, lookahead=8,
                 q_sub=Q_SUB):
        self.name = name
        self.vheads = vheads
        self.n_maps = n_maps
        self.causal = causal
        self.decay = decay
        self.select = select
        self.diff = diff
        self.tile = tile
        self.lookahead = lookahead
        self.q_sub = q_sub


_PLAIN_VHEADS = [(0, h * HEAD_DIM, (h + 1) * HEAD_DIM, 0, h) for h in range(N_HEADS)]
_CFG_MLA = _AttnCfg("attn_mla", [((h // 2) * MXU_WIDTH, (h % 2) * PAIR_STRIDE, (h % 2) * PAIR_STRIDE + MLA_QK, 0, h)
                                 for h in range(N_HEADS)], 1, True, q_sub=MXU_WIDTH, lookahead=16, tile=2048)
_CFG_FOX = _AttnCfg("attn_fox", _PLAIN_VHEADS, 1, True, decay=True, tile=2048, lookahead=8)
_CFG_MOBA = _AttnCfg("attn_moba", _PLAIN_VHEADS, 1, True, select=True, tile=2048, lookahead=8)
_CFG_DIFF = _AttnCfg("attn_diff", [(0, h * HEAD_DIM + c * DIFF_QK, h * HEAD_DIM + (c + 1) * DIFF_QK, c, h)
                                   for c in range(2) for h in range(N_HEADS)], 2, True, diff=True, q_sub=MXU_WIDTH,
                     lookahead=32)
_CFG_MEM = _AttnCfg("attn_mem", _PLAIN_VHEADS, 1, False, tile=2048)


ONES_ROW = HEAD_DIM
V_ROWS = HEAD_DIM + 16


def _np_value_select():
    sel = np.zeros((N_HEADS * V_ROWS, GROUP_WIDTH), np.float32)
    for h in range(N_HEADS):
        for d in range(HEAD_DIM):
            sel[h * V_ROWS + d, h * HEAD_DIM + d] = 1.0
    return sel


def _values_t(v, vsel):
    vt = _dot_nt(vsel, v)
    row = lax.broadcasted_iota(jnp.int32, vt.shape, 0)
    ones = row == ONES_ROW
    for h in range(1, N_HEADS):
        ones = ones | (row == h * V_ROWS + ONES_ROW)
    return jnp.where(ones, 1.0, vt).astype(jnp.bfloat16)


def _tile_lanes(x, width):
    return jnp.tile(x, (1, width // LANES)) if width != LANES else x


def _attn_body(cfg, qi_ref, kj_ref, *refs):
    refs = list(refs)
    q_ref, k_ref, vt_ref = refs[:3]
    pos = 3
    if cfg.decay:
        dq_ref, dk_ref = refs[pos:pos + 2]
        pos += 2
    if cfg.select:
        sel_ref = refs[pos]
        pos += 1
    if cfg.diff:
        g64_ref, gsub_ref, lam_ref = refs[pos:pos + 3]
        pos += 3
    o_ref, qm_s, m_s, acc_s = refs[pos:pos + 4]

    t = pl.program_id(1)
    i = qi_ref[t]
    j = kj_ref[t]
    tq = q_ref.shape[1]
    tk = k_ref.shape[1]

    @pl.when(j == 0)
    def _():
        for n, (off, lo, hi, _, _) in enumerate(cfg.vheads):
            qb = q_ref[0, :, off:off + MXU_WIDTH]
            qm_s[n] = jnp.where(_lane_mask(qb.shape, lo, hi), qb, jnp.zeros_like(qb))
        m_s[...] = jnp.full(m_s.shape, NEG_INF, jnp.float32)
        acc_s[...] = jnp.zeros_like(acc_s)

    def step(diag):
        qs = min(tq, cfg.q_sub)
        items = [(n, u) for n in range(len(cfg.vheads)) for u in range(tq // qs)]

        def n_keys(u):
            return (u + 1) * qs if diag else tk

        def scores(item):
            n, u = item
            off, _, _, _, h = cfg.vheads[n]
            nk = n_keys(u)
            cols = slice(u * qs, (u + 1) * qs)
            s = _dot_nt(k_ref[0, 0:nk, off:off + MXU_WIDTH], qm_s[n, cols, :])
            if cfg.decay:
                s = (dq_ref[0, h:h + 1, cols] - _tile_lanes(dk_ref[0, 0:nk, h * LANES:(h + 1) * LANES], qs)) + s
            if cfg.select:
                nbp = sel_ref.shape[1] // N_HEADS
                qpos = u * qs + lax.broadcasted_iota(jnp.int32, (1, qs), 1)
                parts = []
                for kb in range(nk // MOBA_BLOCK):
                    rows = s[kb * MOBA_BLOCK:(kb + 1) * MOBA_BLOCK, :]
                    if not (diag and kb == nk // MOBA_BLOCK - 1):
                        bias = sel_ref[0, pl.ds(h * nbp + j * (tk // MOBA_BLOCK) + kb, 1), cols]
                        if diag:
                            bias = jnp.where(qpos < (kb + 1) * MOBA_BLOCK, 0.0, bias)
                        rows = rows + bias
                    parts.append(rows)
                s = parts[0] if len(parts) == 1 else jnp.concatenate(parts, axis=0)
            if diag:
                key = lax.broadcasted_iota(jnp.int32, (nk, qs), 0)
                qry = u * qs + lax.broadcasted_iota(jnp.int32, (nk, qs), 1)
                s = jnp.where(key <= qry, s, NEG_INF)
            return s, jnp.max(s, axis=0, keepdims=True)

        raw = {it: scores(items[it]) for it in range(min(cfg.lookahead, len(items)))}
        for it, (n, u) in enumerate(items):
            h = cfg.vheads[n][4]
            nk = n_keys(u)
            cols = slice(u * qs, (u + 1) * qs)
            s, s_max = raw.pop(it)
            m_prev = m_s[n, :, cols]
            m_new = jnp.maximum(m_prev, s_max)
            alpha = jnp.exp2(m_prev - m_new)
            p = jnp.exp2(s - m_new)
            m_s[n, :, cols] = m_new
            acc_s[n, :, cols] = acc_s[n, :, cols] * alpha + _dot(vt_ref[0, h * V_ROWS:(h + 1) * V_ROWS, 0:nk],
                                                                 p.astype(jnp.bfloat16))
            if it + cfg.lookahead < len(items):
                raw[it + cfg.lookahead] = scores(items[it + cfg.lookahead])

    if cfg.causal:
        pl.when(j < i)(functools.partial(step, False))
        pl.when(j == i)(functools.partial(step, True))
        last = j == i
    else:
        step(False)
        last = j == 0

    @pl.when(last)
    def _():
        outs = []
        for c in range(cfg.n_maps):
            heads = []
            for h in range(N_HEADS):
                acc = acc_s[c * N_HEADS + h]
                heads.append(acc[0:HEAD_DIM, :] / acc[ONES_ROW:ONES_ROW + 1, :])
            outs.append(jnp.concatenate(heads, axis=0).T)
        if cfg.diff:
            o = outs[0] - lam_ref[0:1, :] * outs[1]
            o = o * lax.rsqrt(_group_mean_sq(o, g64_ref[...]) + EPS) * gsub_ref[...]
        else:
            o = outs[0]
        o_ref[0] = o.astype(o_ref.dtype)


def _mem_attn_body(cfg, qi_ref, kj_ref, q_ref, mem_ref, mnorm_ref, w_ref, g64_ref, gain_ref, vsel_ref, o_ref,
                   k_s, vt_s, qm_s, m_s, acc_s):
    @pl.when(pl.program_id(1) == 0)
    def _():
        _memkv_body(mem_ref, mnorm_ref, w_ref, g64_ref, gain_ref, vsel_ref, k_s, vt_s)

    _attn_body(cfg, qi_ref, kj_ref, q_ref, k_s, vt_s, o_ref, qm_s, m_s, acc_s)


def _mem_attn_call(layer, cfg, q, mem, mnorm, w, g64, gain, vsel, tq):
    bsz, seq, wq = q.shape
    _, mlen, d = mem.shape
    nq = seq // tq
    assert seq % tq == 0 and not cfg.causal
    qi = jnp.asarray(np.arange(nq, dtype=np.int32))
    kj = jnp.zeros((nq,), jnp.int32)
    n_vh = len(cfg.vheads)
    grid_spec = pltpu.PrefetchScalarGridSpec(
        num_scalar_prefetch=2,
        grid=(bsz, nq),
        in_specs=[pl.BlockSpec((1, tq, wq), lambda b, t, qi, kj: (b, qi[t], 0)),
                  pl.BlockSpec((1, mlen, d), lambda b, t, qi, kj: (b, 0, 0)),
                  _const_spec(mnorm, layer), _const_spec(w, layer), _const_spec(g64), _const_spec(gain, layer),
                  _const_spec(vsel)],
        out_specs=pl.BlockSpec((1, tq, GROUP_WIDTH), lambda b, t, qi, kj: (b, qi[t], 0)),
        scratch_shapes=[pltpu.VMEM((1, mlen, GROUP_WIDTH), jnp.bfloat16),
                        pltpu.VMEM((1, N_HEADS * V_ROWS, mlen), jnp.bfloat16),
                        pltpu.VMEM((n_vh, tq, MXU_WIDTH), jnp.bfloat16),
                        pltpu.VMEM((n_vh, 1, tq), jnp.float32),
                        pltpu.VMEM((n_vh, V_ROWS, tq), jnp.float32)])
    return pl.pallas_call(
        functools.partial(_mem_attn_body, cfg),
        grid_spec=grid_spec,
        out_shape=jax.ShapeDtypeStruct((bsz, seq, GROUP_WIDTH), jnp.bfloat16),
        compiler_params=pltpu.CompilerParams(dimension_semantics=("arbitrary", "arbitrary"),
                                             vmem_limit_bytes=VMEM_LIMIT_BYTES),
        name=cfg.name,
    )(qi, kj, q, mem, mnorm, w, g64, gain, vsel)


def _attn_call(cfg, q, k, v, extras, tq, tk):
    bsz, seq, wq = q.shape
    sk = k.shape[1]
    nq = seq // tq
    assert seq % tq == 0 and sk % tk == 0
    if cfg.causal:
        assert tq == tk and sk == seq
        pairs = [(i, j) for i in range(nq) for j in range(i + 1)]
    else:
        assert sk == tk
        pairs = [(i, 0) for i in range(nq)]
    qi = jnp.asarray(np.array([p[0] for p in pairs], np.int32))
    kj = jnp.asarray(np.array([p[1] for p in pairs], np.int32))
    n_vh = len(cfg.vheads)

    in_specs = [pl.BlockSpec((1, tq, wq), lambda b, t, qi, kj: (b, qi[t], 0)),
                pl.BlockSpec((1, tk, wq), lambda b, t, qi, kj: (b, kj[t], 0)),
                pl.BlockSpec((1, N_HEADS * V_ROWS, tk), lambda b, t, qi, kj: (b, 0, kj[t]))]
    args = [q, k, v]
    if cfg.decay:
        dcol, drow = extras
        in_specs += [pl.BlockSpec((1, SUBLANES, tq), lambda b, t, qi, kj: (b, 0, qi[t])),
                     pl.BlockSpec((1, tk, N_HEADS * LANES), lambda b, t, qi, kj: (b, kj[t], 0))]
        args += [drow, dcol]
    if cfg.select:
        (sel,) = extras
        in_specs += [pl.BlockSpec((1, sel.shape[1], tq), lambda b, t, qi, kj: (b, 0, qi[t]))]
        args += [sel]
    if cfg.diff:
        layer, g64, gsub, lam_row = extras
        in_specs += [_const_spec(g64), _const_spec(gsub, layer), _const_spec(lam_row, layer)]
        args += [g64, gsub, lam_row]

    grid_spec = pltpu.PrefetchScalarGridSpec(
        num_scalar_prefetch=2,
        grid=(bsz, len(pairs)),
        in_specs=in_specs,
        out_specs=pl.BlockSpec((1, tq, GROUP_WIDTH), lambda b, t, qi, kj: (b, qi[t], 0)),
        scratch_shapes=[pltpu.VMEM((n_vh, tq, MXU_WIDTH), jnp.bfloat16),
                        pltpu.VMEM((n_vh, 1, tq), jnp.float32),
                        pltpu.VMEM((n_vh, V_ROWS, tq), jnp.float32)])
    return pl.pallas_call(
        functools.partial(_attn_body, cfg),
        grid_spec=grid_spec,
        out_shape=jax.ShapeDtypeStruct((bsz, seq, GROUP_WIDTH), jnp.bfloat16),
        compiler_params=pltpu.CompilerParams(dimension_semantics=("arbitrary", "arbitrary"),
                                             vmem_limit_bytes=VMEM_LIMIT_BYTES),
        name=cfg.name,
    )(qi, kj, *args)


def _ffn_body(nf, x_ref, oa_ref, ob_ref, oc_ref, od_ref, oe_ref, wo_ref, fnorm_ref, wg_ref, wu_ref, cw_ref, cb_ref,
              wd_ref, out_ref, xnew_s, xn_s, acc_s):
    i = pl.program_id(1)
    f = pl.program_id(2)
    tm = x_ref.shape[1]

    @pl.when(f == 0)
    def _():
        @pl.when(i == 0)
        def _():
            xn_s[0:TAIL_ROWS, :] = jnp.zeros((TAIL_ROWS, xn_s.shape[1]), xn_s.dtype)

        @pl.when(i > 0)
        def _():
            xn_s[0:TAIL_ROWS, :] = xn_s[tm:tm + TAIL_ROWS, :]

        mixed = jnp.concatenate([o_ref[0] for o_ref in (oa_ref, ob_ref, oc_ref, od_ref, oe_ref)], axis=1)
        xnew = x_ref[0] + _dot(mixed, wo_ref[...])
        xnew_s[...] = xnew
        xn = xnew * lax.rsqrt(jnp.mean(xnew * xnew, axis=-1, keepdims=True) + EPS) * fnorm_ref[...]
        xn_s[TAIL_ROWS:TAIL_ROWS + tm, :] = xn.astype(xn_s.dtype)

    def mlp_chunk():
        ge = _dot(xn_s[...], wg_ref[...])
        u = _dot(xn_s[TAIL_ROWS:TAIL_ROWS + tm, :], wu_ref[...])
        g0 = ge[TAIL_ROWS:TAIL_ROWS + tm, :]
        t1 = ge[TAIL_ROWS - 1:TAIL_ROWS, :]
        t2 = ge[TAIL_ROWS - 2:TAIL_ROWS - 1, :]
        row = lax.broadcasted_iota(jnp.int32, g0.shape, 0)
        g1 = jnp.where(row == 0, t1, pltpu.roll(g0, 1, 0))
        g2 = jnp.where(row == 0, t2, jnp.where(row == 1, t1, pltpu.roll(g0, 2, 0)))
        y = cb_ref[...] + cw_ref[0:1, :] * g2
        y = y + cw_ref[1:2, :] * g1
        y = y + cw_ref[2:3, :] * g0
        hmid = (y * (1.0 / (1.0 + jnp.exp(-y)))) * u
        return _dot(hmid.astype(jnp.bfloat16), wd_ref[...])

    @pl.when(f == 0)
    def _():
        acc_s[...] = mlp_chunk()

    if nf > 2:
        @pl.when((f > 0) & (f < nf - 1))
        def _():
            acc_s[...] += mlp_chunk()

    @pl.when(f == nf - 1)
    def _():
        out_ref[0] = xnew_s[...] + (acc_s[...] + mlp_chunk())


def _ffn_call(layer, x, outs, wo, fnorm, wg, wu, cw, cb, wd, tm, tf):
    bsz, seq, d = x.shape
    dff = wg.shape[2]
    n_tiles, nf = seq // tm, dff // tf
    assert seq % tm == 0 and dff % tf == 0 and nf >= 2
    tok = lambda w: pl.BlockSpec((1, tm, w), lambda b, i, f: (b, i, 0))
    return pl.pallas_call(
        functools.partial(_ffn_body, nf),
        grid=(bsz, n_tiles, nf),
        in_specs=[tok(d)] + [tok(GROUP_WIDTH)] * 5 + [
            _const_spec(wo, layer),
            _const_spec(fnorm, layer),
            pl.BlockSpec((None, d, tf), lambda b, i, f: (layer, 0, f)),
            pl.BlockSpec((None, d, tf), lambda b, i, f: (layer, 0, f)),
            pl.BlockSpec((None, SUBLANES, tf), lambda b, i, f: (layer, 0, f)),
            pl.BlockSpec((None, 1, tf), lambda b, i, f: (layer, 0, f)),
            pl.BlockSpec((None, tf, d), lambda b, i, f: (layer, f, 0))],
        out_specs=tok(d),
        out_shape=jax.ShapeDtypeStruct((bsz, seq, d), jnp.float32),
        scratch_shapes=[pltpu.VMEM((tm, d), jnp.float32), pltpu.VMEM((TAIL_ROWS + tm, d), jnp.bfloat16),
                        pltpu.VMEM((tm, d), jnp.float32)],
        compiler_params=pltpu.CompilerParams(dimension_semantics=("arbitrary", "arbitrary", "arbitrary"),
                                             vmem_limit_bytes=VMEM_LIMIT_BYTES),
        name="ffn",
    )(x, *outs, wo, fnorm, wg, wu, cw, cb, wd)


def _pad_rows(v, width=MXU_WIDTH):
    return jnp.pad(v.astype(jnp.float32), ((0, 0), (0, width - v.shape[1])))


def _tile_rows(g, reps):
    return jnp.tile(g.astype(jnp.float32), (1, reps))


def _pack_in_projection(w):
    idx = _np_in_index()
    pieces, start = [], 0
    while start < PK_END:
        stop = start + 1
        if idx[start] == _SRC_END:
            while stop < PK_END and idx[stop] == _SRC_END:
                stop += 1
            pieces.append(jnp.zeros(w.shape[:-1] + (stop - start,), w.dtype))
        else:
            while stop < PK_END and idx[stop] == idx[stop - 1] + 1:
                stop += 1
            pieces.append(w[..., int(idx[start]):int(idx[stop - 1]) + 1])
        start = stop
    return jnp.concatenate(pieces, axis=-1)


def _zero_col(w):
    return jnp.concatenate([w, jnp.zeros(w.shape[:-1] + (1,), w.dtype)], axis=-1)


def _rope_table(positions):
    pos = positions.astype(jnp.float32)[:, :, None]
    inv = [ROPE_THETA ** (-jnp.arange(0, rot, 2, dtype=jnp.float32) / rot) for rot in (MLA_ROPE, ROT_MOBA, ROT_DIFF)]
    inv = jnp.concatenate(inv + [jnp.zeros((N_FREQ - TR_ONE,), jnp.float32)])
    ang = pos * inv
    c, s = jnp.cos(ang), jnp.sin(ang)
    c_hi = c.astype(jnp.bfloat16)
    c_lo = (c - c_hi.astype(jnp.float32)).astype(jnp.bfloat16)
    s_hi = s.astype(jnp.bfloat16)
    s_lo = (s - s_hi.astype(jnp.float32)).astype(jnp.bfloat16)
    return jnp.concatenate([c_hi, c_lo, s_hi, s_lo], axis=-1)


def _pick_tile(n, pref):
    t = pref
    while n % t:
        t //= 2
    return t


def kernel(x, mem, positions, attn_norm, ffn_norm, mem_norm, w_in, mla_cq_norm, mla_ckv_norm, mla_w_uq, mla_w_ukv, mla_q_norm, mla_k_norm, fox_b_f, fox_q_norm, fox_k_norm, moba_q_norm, moba_k_norm, diff_lambda, diff_q_norm, diff_k_norm, diff_sub_norm, mem_w_kv, mem_q_norm, mem_k_norm, w_o, ffn_w_gate, ffn_w_up, ffn_conv_w, ffn_conv_b, ffn_w_down):
    bsz, seq, d = x.shape
    depth = w_in.shape[0]
    dff = ffn_w_gate.shape[2]
    bf = jnp.bfloat16
    f32 = jnp.float32

    uq_idx = _np_uq_index()
    ukvk_idx, ukvv_idx = _np_ukv_index()
    gpair = jnp.asarray(_np_group_matrix(_PAIR_GROUPS), bf)
    g64 = jnp.asarray(_np_group_matrix(_G64_GROUPS), bf)
    g32 = jnp.asarray(_np_group_matrix(_G32_GROUPS), bf)
    expand = jnp.asarray(_np_rope_expand_all(), bf)
    t_prep = max(_pick_tile(seq, PREP_TILE), MOBA_BLOCK)
    tril = jnp.asarray(np.tril(np.ones((MOBA_BLOCK, MOBA_BLOCK), np.float32)), bf)
    vsel = jnp.asarray(_np_value_select(), bf)
    tr = _rope_table(positions)

    tile = lambda cfg: max(_pick_tile(seq, cfg.tile), MOBA_BLOCK) if cfg.select else _pick_tile(seq, cfg.tile)
    t_ffn = _pick_tile(seq, 512)
    tf = dff // 2 if (dff // 2) % LANES == 0 else dff

    win = _pack_in_projection(w_in.astype(bf))
    wuq = jnp.take(_zero_col(mla_w_uq), uq_idx, axis=2)
    wuq = jnp.pad(wuq, ((0, 0), (0, MXU_WIDTH - MLA_Q_RANK), (0, 0))).astype(bf)
    wukv = _zero_col(mla_w_ukv)
    wukvk = jnp.take(wukv, ukvk_idx, axis=2).astype(bf)
    wukvv = jnp.take(wukv, ukvv_idx, axis=2).astype(bf)
    pair = lambda g: _pad_rows(_tile_rows(g, 2))
    rows = [jnp.zeros((depth, MXU_WIDTH), f32)] * P_ROWS
    rows[P_CQ] = _pad_rows(mla_cq_norm)
    rows[P_CKV] = _pad_rows(mla_ckv_norm)
    rows[P_GQ] = pair(mla_q_norm) * (LOG2E * MLA_QK ** -0.5)
    rows[P_GK] = pair(mla_k_norm)
    rows[P_FQ] = _tile_rows(fox_q_norm, N_HEADS) * (LOG2E * HEAD_DIM ** -0.5)
    rows[P_FK] = _tile_rows(fox_k_norm, N_HEADS)
    rows[P_FB] = _pad_rows(fox_b_f)
    rows[P_MQ] = _tile_rows(moba_q_norm, N_HEADS)
    rows[P_MK] = _tile_rows(moba_k_norm, N_HEADS)
    rows[P_DQ] = _tile_rows(diff_q_norm, 2 * N_HEADS) * (LOG2E * DIFF_QK ** -0.5)
    rows[P_DK] = _tile_rows(diff_k_norm, 2 * N_HEADS)
    rows[P_EQ] = _tile_rows(mem_q_norm, N_HEADS) * (LOG2E * HEAD_DIM ** -0.5)
    par = jnp.stack(rows, axis=1)
    anorm = attn_norm.astype(f32)[:, None, :]
    mnorm = mem_norm.astype(f32)[:, None, :]
    fnorm = ffn_norm.astype(f32)[:, None, :]
    wmem = mem_w_kv.astype(bf)
    mem_gain = _tile_rows(mem_k_norm, N_HEADS)[:, None, :]

    lam_init = jnp.asarray([0.8 - 0.6 * math.exp(-0.3 * l) for l in range(depth)], f32)
    lam_vec = diff_lambda.astype(f32)
    lam = (jnp.exp(jnp.sum(lam_vec[:, 0] * lam_vec[:, 1], axis=-1))
           - jnp.exp(jnp.sum(lam_vec[:, 2] * lam_vec[:, 3], axis=-1)) + lam_init)
    lam_row = jnp.broadcast_to(lam[:, None, None], (depth, 1, GROUP_WIDTH))
    gsub = (_tile_rows(diff_sub_norm, N_HEADS) * (1.0 - lam_init)[:, None])[:, None, :]

    wo = w_o.astype(bf)
    wg, wu, wd = ffn_w_gate.astype(bf), ffn_w_up.astype(bf), ffn_w_down.astype(bf)
    cw = jnp.pad(ffn_conv_w.astype(f32), ((0, 0), (0, SUBLANES - CONV_WIDTH), (0, 0)))
    cb = ffn_conv_b.astype(f32)[:, None, :]

    for l in range(depth):
        (aq, ak, av, fq, fk, fv, fdcol, fdrow, mq, mk, mv, msel, dq, dk, dv, eq) = _prep_call(
            l, t_prep, x, tr, anorm, win, wuq, wukvk, wukvv, gpair, g64, g32, expand, tril, vsel, par)

        o_a = _attn_call(_CFG_MLA, aq, ak, av, (), tile(_CFG_MLA), tile(_CFG_MLA))
        o_b = _attn_call(_CFG_FOX, fq, fk, fv, (fdcol, fdrow), tile(_CFG_FOX), tile(_CFG_FOX))
        o_c = _attn_call(_CFG_MOBA, mq, mk, mv, (msel,), tile(_CFG_MOBA), tile(_CFG_MOBA))
        o_d = _attn_call(_CFG_DIFF, dq, dk, dv, (l, g64, gsub, lam_row), tile(_CFG_DIFF), tile(_CFG_DIFF))
        o_e = _mem_attn_call(l, _CFG_MEM, eq, mem, mnorm, wmem, g64, mem_gain, vsel, tile(_CFG_MEM))

        x = _ffn_call(l, x, (o_a, o_b, o_c, o_d, o_e), wo, fnorm, wg, wu, cw, cb, wd, t_ffn, tf)
    return x
```
